```python
import jax, jax.numpy as jnp
from jax import lax
import numpy as np

D_MODEL = 1024
BATCH = 8
SEQ = 4096
DEPTH = 4

CHUNK = 64
Q_BLOCK = 128
SC_WIDTH = 512
SC_GROUPS = 8
SC_KERNEL = 3
MLA_HEADS = 8
QK_NOPE = 64
QK_ROPE = 32
V_HEAD = 64
Q_LORA = 256
KV_LORA = 128
MLA_WIDTH = MLA_HEADS * V_HEAD
ROPE_THETA = 10000.0
CONF_WIDTH = D_MODEL
CONF_KERNEL = 31
EVEN_SPLITS = (SC_WIDTH, SC_WIDTH, SC_WIDTH, SC_WIDTH, Q_LORA, KV_LORA, QK_ROPE, MLA_WIDTH)
EVEN_IN = sum(EVEN_SPLITS)
ODD_IN = 3 * CONF_WIDTH
N_EVEN = (DEPTH + 1) // 2
N_ODD = DEPTH // 2
EPS = 1e-6

kernel_name = 'hybrid_chunk_causal_conv_mla_conformer_trunk'


def rms_norm(x, g):
    xf = x.astype(jnp.float32)
    y = xf * lax.rsqrt(jnp.mean(xf * xf, axis=-1, keepdims=True) + EPS)
    return (y * g.astype(jnp.float32)).astype(x.dtype)


def layer_norm(x, g, b):
    xf = x.astype(jnp.float32)
    mu = jnp.mean(xf, axis=-1, keepdims=True)
    var = jnp.mean(jnp.square(xf - mu), axis=-1, keepdims=True)
    y = (xf - mu) * lax.rsqrt(var + EPS)
    return (y * g.astype(jnp.float32) + b.astype(jnp.float32)).astype(x.dtype)


def split_cols(z, sizes):
    idx = np.cumsum(sizes)[:-1].tolist()
    return jnp.split(z, idx, axis=-1)


def causal_depthwise_conv(u, w, b):
    k, ch = w.shape
    y = lax.conv_general_dilated(u, w[:, None, :].astype(u.dtype), window_strides=(1,),
                                 padding=[(k - 1, 0)], dimension_numbers=('NWC', 'WIO', 'NWC'),
                                 feature_group_count=ch)
    return y + b.astype(u.dtype)


def rope_tables(positions):
    inv_freq = 1.0 / (ROPE_THETA ** (jnp.arange(0, QK_ROPE, 2, dtype=jnp.float32) / QK_ROPE))
    ang = positions.astype(jnp.float32)[..., None] * inv_freq
    return jnp.cos(ang), jnp.sin(ang)


def apply_rope(t, cos, sin):
    tf = t.astype(jnp.float32)
    t1, t2 = jnp.split(tf, 2, axis=-1)
    return jnp.concatenate([t1 * cos - t2 * sin, t2 * cos + t1 * sin], axis=-1).astype(t.dtype)


def block_causal_attention(q, k, v):
    b, h, s, dqk = q.shape
    nb = s // Q_BLOCK
    scale = 1.0 / np.sqrt(dqk)
    k_chunk = jnp.arange(s) // CHUNK
    qb = q.reshape(b, h, nb, Q_BLOCK, dqk).transpose(2, 0, 1, 3, 4)

    def one_block(args):
        q_blk, blk = args
        scores = jnp.einsum('bhqd,bhkd->bhqk', q_blk, k, preferred_element_type=jnp.float32) * scale
        q_chunk = (blk * Q_BLOCK + jnp.arange(Q_BLOCK)) // CHUNK
        allowed = k_chunk[None, :] <= q_chunk[:, None]
        scores = jnp.where(allowed, scores, jnp.finfo(jnp.float32).min)
        p = jax.nn.softmax(scores, axis=-1)
        return jnp.einsum('bhqk,bhkd->bhqd', p.astype(v.dtype), v)

    out = lax.map(one_block, (qb, jnp.arange(nb)))
    return out.transpose(1, 0, 3, 2, 4).reshape(b, s, h * v.shape[-1])


def even_mixer(h, cos, sin, w_in, sc_conv_w, sc_conv_b, q_norm_g, kv_norm_g, w_uq, w_ukv, w_out):
    bsz, s, _ = h.shape
    z = h @ w_in
    a_b, a_c, a_x, a_gate, c_q, c_kv, k_rope_raw, b_gate = split_cols(z, EVEN_SPLITS)
    y_a = a_b * causal_depthwise_conv(a_c * a_x, sc_conv_w, sc_conv_b)
    y_a = y_a * jax.nn.silu(a_gate)
    q = (rms_norm(c_q, q_norm_g) @ w_uq).reshape(bsz, s, MLA_HEADS, QK_NOPE + QK_ROPE)
    q_nope, q_rope = q[..., :QK_NOPE], q[..., QK_NOPE:]
    q_rope = apply_rope(q_rope, cos[:, :, None, :], sin[:, :, None, :])
    kv = (rms_norm(c_kv, kv_norm_g) @ w_ukv).reshape(bsz, s, MLA_HEADS, QK_NOPE + V_HEAD)
    k_nope, v = kv[..., :QK_NOPE], kv[..., QK_NOPE:]
    k_rope = apply_rope(k_rope_raw, cos, sin)
    k_rope = jnp.broadcast_to(k_rope[:, :, None, :], (bsz, s, MLA_HEADS, QK_ROPE))
    q_full = jnp.concatenate([q_nope, q_rope], axis=-1).transpose(0, 2, 1, 3)
    k_full = jnp.concatenate([k_nope, k_rope], axis=-1).transpose(0, 2, 1, 3)
    y_b = block_causal_attention(q_full, k_full, v.transpose(0, 2, 1, 3))
    y_b = y_b * jax.nn.silu(b_gate)
    return jnp.concatenate([y_a, y_b], axis=-1) @ w_out


def odd_mixer(h, w_in, conv_w, conv_b, ln_g, ln_b, w_out):
    z = h @ w_in
    val, glu_gate, silu_gate = jnp.split(z, 3, axis=-1)
    u = val * jax.nn.sigmoid(glu_gate)
    u = causal_depthwise_conv(u, conv_w, conv_b)
    u = jax.nn.silu(layer_norm(u, ln_g, ln_b))
    return (u * jax.nn.silu(silu_gate)) @ w_out


def _fwd_setup_inputs(seed: int = 0) -> dict:
    key = jax.random.key(seed)
    ks = iter(jax.random.split(key, 32))

    def nrm(shape, scale):
        return jax.random.normal(next(ks), shape, jnp.float32) * scale

    d = D_MODEL
    return {
        'x': nrm((BATCH, SEQ, d), 1.0),
        'c': nrm((BATCH, d), 1.0),
        'positions': (jax.random.randint(next(ks), (BATCH, 1), 0, 4096, dtype=jnp.int32)
                      + jnp.arange(SEQ, dtype=jnp.int32)[None, :]),
        'ada_w': nrm((DEPTH, d, 3 * d), 0.5 * d ** -0.5),
        'ada_b': nrm((DEPTH, 3 * d), 0.1),
        'pre_norm_g': 1.0 + nrm((DEPTH, d), 0.05),
        'post_norm_g': 1.0 + nrm((DEPTH, d), 0.05),
        'even_w_in': nrm((N_EVEN, d, EVEN_IN), d ** -0.5),
        'even_sc_conv_w': nrm((N_EVEN, SC_KERNEL, SC_WIDTH), SC_KERNEL ** -0.5),
        'even_sc_conv_b': nrm((N_EVEN, SC_WIDTH), 0.01),
        'even_q_norm_g': 1.0 + nrm((N_EVEN, Q_LORA), 0.05),
        'even_kv_norm_g': 1.0 + nrm((N_EVEN, KV_LORA), 0.05),
        'even_w_uq': nrm((N_EVEN, Q_LORA, MLA_HEADS * (QK_NOPE + QK_ROPE)), Q_LORA ** -0.5),
        'even_w_ukv': nrm((N_EVEN, KV_LORA, MLA_HEADS * (QK_NOPE + V_HEAD)), KV_LORA ** -0.5),
        'even_w_out': nrm((N_EVEN, SC_WIDTH + MLA_WIDTH, d), (SC_WIDTH + MLA_WIDTH) ** -0.5),
        'odd_w_in': nrm((N_ODD, d, ODD_IN), d ** -0.5),
        'odd_conv_w': nrm((N_ODD, CONF_KERNEL, CONF_WIDTH), CONF_KERNEL ** -0.5),
        'odd_conv_b': nrm((N_ODD, CONF_WIDTH), 0.01),
        'odd_ln_g': 1.0 + nrm((N_ODD, CONF_WIDTH), 0.05),
        'odd_ln_b': nrm((N_ODD, CONF_WIDTH), 0.01),
        'odd_w_out': nrm((N_ODD, CONF_WIDTH, d), CONF_WIDTH ** -0.5),
    }


def _fwd_reference(x, c, positions, ada_w, ada_b, pre_norm_g, post_norm_g,
              even_w_in, even_sc_conv_w, even_sc_conv_b, even_q_norm_g, even_kv_norm_g,
              even_w_uq, even_w_ukv, even_w_out,
              odd_w_in, odd_conv_w, odd_conv_b, odd_ln_g, odd_ln_b, odd_w_out):
    cos, sin = rope_tables(positions)
    c_act = jax.nn.silu(c)
    for layer in range(DEPTH):
        mod = c_act @ ada_w[layer] + ada_b[layer]
        shift, scale, gate = jnp.split(mod, 3, axis=-1)
        h = rms_norm(x, pre_norm_g[layer]) * (1.0 + scale[:, None, :]) + shift[:, None, :]
        i = layer // 2
        if layer % 2 == 0:
            y = even_mixer(h, cos, sin, even_w_in[i], even_sc_conv_w[i], even_sc_conv_b[i],
                           even_q_norm_g[i], even_kv_norm_g[i], even_w_uq[i], even_w_ukv[i],
                           even_w_out[i])
        else:
            y = odd_mixer(h, odd_w_in[i], odd_conv_w[i], odd_conv_b[i], odd_ln_g[i],
                          odd_ln_b[i], odd_w_out[i])
        x = x + gate[:, None, :] * rms_norm(y, post_norm_g[layer])
    return x


import jax as _jax
import jax.numpy as _jnp

TWIN_FORMAT = 'train_step'
FWD_PARAMS = ['x', 'c', 'positions', 'ada_w', 'ada_b', 'pre_norm_g', 'post_norm_g', 'even_w_in', 'even_sc_conv_w', 'even_sc_conv_b', 'even_q_norm_g', 'even_kv_norm_g', 'even_w_uq', 'even_w_ukv', 'even_w_out', 'odd_w_in', 'odd_conv_w', 'odd_conv_b', 'odd_ln_g', 'odd_ln_b', 'odd_w_out']
TWIN_WEIGHTS = ['ada_w', 'ada_b', 'pre_norm_g', 'post_norm_g', 'even_w_in', 'even_sc_conv_w', 'even_sc_conv_b', 'even_q_norm_g', 'even_kv_norm_g', 'even_w_uq', 'even_w_ukv', 'even_w_out', 'odd_w_in', 'odd_conv_w', 'odd_conv_b', 'odd_ln_g', 'odd_ln_b', 'odd_w_out']
TWIN_DIFF_INPUT = 'x'
TWIN_INPUTS = ['x', 'c', 'positions', 'ada_w', 'ada_b', 'pre_norm_g', 'post_norm_g', 'even_w_in', 'even_sc_conv_w', 'even_sc_conv_b', 'even_q_norm_g', 'even_kv_norm_g', 'even_w_uq', 'even_w_ukv', 'even_w_out', 'odd_w_in', 'odd_conv_w', 'odd_conv_b', 'odd_ln_g', 'odd_ln_b', 'odd_w_out', 'loss_target', 'm_ada_w', 'm_ada_b', 'm_pre_norm_g', 'm_post_norm_g', 'm_even_w_in', 'm_even_sc_conv_w', 'm_even_sc_conv_b', 'm_even_q_norm_g', 'm_even_kv_norm_g', 'm_even_w_uq', 'm_even_w_ukv', 'm_even_w_out', 'm_odd_w_in', 'm_odd_conv_w', 'm_odd_conv_b', 'm_odd_ln_g', 'm_odd_ln_b', 'm_odd_w_out', 'v_ada_w', 'v_ada_b', 'v_pre_norm_g', 'v_post_norm_g', 'v_even_w_in', 'v_even_sc_conv_w', 'v_even_sc_conv_b', 'v_even_q_norm_g', 'v_even_kv_norm_g', 'v_even_w_uq', 'v_even_w_ukv', 'v_even_w_out', 'v_odd_w_in', 'v_odd_conv_w', 'v_odd_conv_b', 'v_odd_ln_g', 'v_odd_ln_b', 'v_odd_w_out']
TWIN_OUTPUTS = ['loss', 'grad_x', 'grad_ada_w', 'grad_ada_b', 'grad_pre_norm_g', 'grad_post_norm_g', 'grad_even_w_in', 'grad_even_sc_conv_w', 'grad_even_sc_conv_b', 'grad_even_q_norm_g', 'grad_even_kv_norm_g', 'grad_even_w_uq', 'grad_even_w_ukv', 'grad_even_w_out', 'grad_odd_w_in', 'grad_odd_conv_w', 'grad_odd_conv_b', 'grad_odd_ln_g', 'grad_odd_ln_b', 'grad_odd_w_out', 'delta_ada_w', 'delta_ada_b', 'delta_pre_norm_g', 'delta_post_norm_g', 'delta_even_w_in', 'delta_even_sc_conv_w', 'delta_even_sc_conv_b', 'delta_even_q_norm_g', 'delta_even_kv_norm_g', 'delta_even_w_uq', 'delta_even_w_ukv', 'delta_even_w_out', 'delta_odd_w_in', 'delta_odd_conv_w', 'delta_odd_conv_b', 'delta_odd_ln_g', 'delta_odd_ln_b', 'delta_odd_w_out', 'new_m_ada_w', 'new_m_ada_b', 'new_m_pre_norm_g', 'new_m_post_norm_g', 'new_m_even_w_in', 'new_m_even_sc_conv_w', 'new_m_even_sc_conv_b', 'new_m_even_q_norm_g', 'new_m_even_kv_norm_g', 'new_m_even_w_uq', 'new_m_even_w_ukv', 'new_m_even_w_out', 'new_m_odd_w_in', 'new_m_odd_conv_w', 'new_m_odd_conv_b', 'new_m_odd_ln_g', 'new_m_odd_ln_b', 'new_m_odd_w_out', 'new_v_ada_w', 'new_v_ada_b', 'new_v_pre_norm_g', 'new_v_post_norm_g', 'new_v_even_w_in', 'new_v_even_sc_conv_w', 'new_v_even_sc_conv_b', 'new_v_even_q_norm_g', 'new_v_even_kv_norm_g', 'new_v_even_w_uq', 'new_v_even_w_ukv', 'new_v_even_w_out', 'new_v_odd_w_in', 'new_v_odd_conv_w', 'new_v_odd_conv_b', 'new_v_odd_ln_g', 'new_v_odd_ln_b', 'new_v_odd_w_out']
TWIN_LEAF_KINDS = {'loss': 'loss', 'grad_x': 'grad_x', 'grad_ada_w': 'grad_w', 'grad_ada_b': 'grad_w', 'grad_pre_norm_g': 'grad_w', 'grad_post_norm_g': 'grad_w', 'grad_even_w_in': 'grad_w', 'grad_even_sc_conv_w': 'grad_w', 'grad_even_sc_conv_b': 'grad_w', 'grad_even_q_norm_g': 'grad_w', 'grad_even_kv_norm_g': 'grad_w', 'grad_even_w_uq': 'grad_w', 'grad_even_w_ukv': 'grad_w', 'grad_even_w_out': 'grad_w', 'grad_odd_w_in': 'grad_w', 'grad_odd_conv_w': 'grad_w', 'grad_odd_conv_b': 'grad_w', 'grad_odd_ln_g': 'grad_w', 'grad_odd_ln_b': 'grad_w', 'grad_odd_w_out': 'grad_w', 'delta_ada_w': 'delta_w', 'delta_ada_b': 'delta_w', 'delta_pre_norm_g': 'delta_w', 'delta_post_norm_g': 'delta_w', 'delta_even_w_in': 'delta_w', 'delta_even_sc_conv_w': 'delta_w', 'delta_even_sc_conv_b': 'delta_w', 'delta_even_q_norm_g': 'delta_w', 'delta_even_kv_norm_g': 'delta_w', 'delta_even_w_uq': 'delta_w', 'delta_even_w_ukv': 'delta_w', 'delta_even_w_out': 'delta_w', 'delta_odd_w_in': 'delta_w', 'delta_odd_conv_w': 'delta_w', 'delta_odd_conv_b': 'delta_w', 'delta_odd_ln_g': 'delta_w', 'delta_odd_ln_b': 'delta_w', 'delta_odd_w_out': 'delta_w', 'new_m_ada_w': 'new_m', 'new_m_ada_b': 'new_m', 'new_m_pre_norm_g': 'new_m', 'new_m_post_norm_g': 'new_m', 'new_m_even_w_in': 'new_m', 'new_m_even_sc_conv_w': 'new_m', 'new_m_even_sc_conv_b': 'new_m', 'new_m_even_q_norm_g': 'new_m', 'new_m_even_kv_norm_g': 'new_m', 'new_m_even_w_uq': 'new_m', 'new_m_even_w_ukv': 'new_m', 'new_m_even_w_out': 'new_m', 'new_m_odd_w_in': 'new_m', 'new_m_odd_conv_w': 'new_m', 'new_m_odd_conv_b': 'new_m', 'new_m_odd_ln_g': 'new_m', 'new_m_odd_ln_b': 'new_m', 'new_m_odd_w_out': 'new_m', 'new_v_ada_w': 'new_v', 'new_v_ada_b': 'new_v', 'new_v_pre_norm_g': 'new_v', 'new_v_post_norm_g': 'new_v', 'new_v_even_w_in': 'new_v', 'new_v_even_sc_conv_w': 'new_v', 'new_v_even_sc_conv_b': 'new_v', 'new_v_even_q_norm_g': 'new_v', 'new_v_even_kv_norm_g': 'new_v', 'new_v_even_w_uq': 'new_v', 'new_v_even_w_ukv': 'new_v', 'new_v_even_w_out': 'new_v', 'new_v_odd_w_in': 'new_v', 'new_v_odd_conv_w': 'new_v', 'new_v_odd_conv_b': 'new_v', 'new_v_odd_ln_g': 'new_v', 'new_v_odd_ln_b': 'new_v', 'new_v_odd_w_out': 'new_v'}


def _forward(args):
    return _fwd_reference(*[args[k] for k in FWD_PARAMS])


def _output_shape():
    def fwd():
        inp = _fwd_setup_inputs(0)
        return _fwd_reference(*[inp[k] for k in FWD_PARAMS])
    out = _jax.eval_shape(fwd)
    return out.shape, out.dtype

N_MICROBATCH = 1
ADAM_LR = 0.001
ADAM_B1 = 0.9
ADAM_B2 = 0.999
ADAM_EPS = 1e-08
ADAM_WD = 0.01
ADAM_STEP = 10
PER_EXAMPLE_BATCH_AXIS = {'x': 0, 'c': 0, 'positions': 0, 'loss_target': 0}
SHARED_INPUTS = []
_WEIGHT_DTYPES = {'ada_w': _jnp.float32, 'ada_b': _jnp.float32, 'pre_norm_g': _jnp.float32, 'post_norm_g': _jnp.float32, 'even_w_in': _jnp.float32, 'even_sc_conv_w': _jnp.float32, 'even_sc_conv_b': _jnp.float32, 'even_q_norm_g': _jnp.float32, 'even_kv_norm_g': _jnp.float32, 'even_w_uq': _jnp.float32, 'even_w_ukv': _jnp.float32, 'even_w_out': _jnp.float32, 'odd_w_in': _jnp.float32, 'odd_conv_w': _jnp.float32, 'odd_conv_b': _jnp.float32, 'odd_ln_g': _jnp.float32, 'odd_ln_b': _jnp.float32, 'odd_w_out': _jnp.float32}
MOMENT_SCALE = {'ada_w': 1.344470e+00, 'ada_b': 3.266049e+00, 'pre_norm_g': 1.890870e-01, 'post_norm_g': 3.769803e+00, 'even_w_in': 1.362517e-01, 'even_sc_conv_w': 1.655933e-01, 'even_sc_conv_b': 1.610673e-01, 'even_q_norm_g': 2.904427e-02, 'even_kv_norm_g': 1.252618e-01, 'even_w_uq': 1.573512e-02, 'even_w_ukv': 4.051327e-02, 'even_w_out': 1.358493e-01, 'odd_w_in': 9.820824e-02, 'odd_conv_w': 1.156245e-01, 'odd_conv_b': 2.992933e-01, 'odd_ln_g': 1.805880e-01, 'odd_ln_b': 1.983286e-01, 'odd_w_out': 1.323917e-01}


def _to_microbatches(a, axis):
    t = _jnp.moveaxis(a, axis, 0)
    t = t.reshape((N_MICROBATCH, t.shape[0] // N_MICROBATCH) + t.shape[1:])
    return _jnp.moveaxis(t, 1, axis + 1)


def setup_inputs(seed: int = 0) -> dict:
    inp = _fwd_setup_inputs(seed)
    key = _jax.random.fold_in(_jax.random.key(seed), 7919)
    shape, _ = _output_shape()
    out = dict(inp)
    out["loss_target"] = _jax.random.normal(_jax.random.fold_in(key, 0), shape, _jnp.float32)
    for i, name in enumerate(TWIN_WEIGHTS):
        w = inp[name].astype(_jnp.float32)
        if MOMENT_SCALE is None:
            s = _jnp.sqrt(_jnp.mean(_jnp.square(w)) + 1e-30)
        else:
            s = MOMENT_SCALE[name]
        km, kv = _jax.random.split(_jax.random.fold_in(key, i + 1))
        out[name] = w
        out["m_" + name] = s * _jax.random.normal(km, w.shape, _jnp.float32)
        out["v_" + name] = (s * s) * _jax.random.uniform(kv, w.shape, _jnp.float32, 0.5, 1.5)
    if N_MICROBATCH > 1:
        for name, axis in PER_EXAMPLE_BATCH_AXIS.items():
            out[name] = _to_microbatches(out[name], axis)
    return {'x': out['x'], 'c': out['c'], 'positions': out['positions'], 'ada_w': out['ada_w'], 'ada_b': out['ada_b'], 'pre_norm_g': out['pre_norm_g'], 'post_norm_g': out['post_norm_g'], 'even_w_in': out['even_w_in'], 'even_sc_conv_w': out['even_sc_conv_w'], 'even_sc_conv_b': out['even_sc_conv_b'], 'even_q_norm_g': out['even_q_norm_g'], 'even_kv_norm_g': out['even_kv_norm_g'], 'even_w_uq': out['even_w_uq'], 'even_w_ukv': out['even_w_ukv'], 'even_w_out': out['even_w_out'], 'odd_w_in': out['odd_w_in'], 'odd_conv_w': out['odd_conv_w'], 'odd_conv_b': out['odd_conv_b'], 'odd_ln_g': out['odd_ln_g'], 'odd_ln_b': out['odd_ln_b'], 'odd_w_out': out['odd_w_out'], 'loss_target': out['loss_target'], 'm_ada_w': out['m_ada_w'], 'm_ada_b': out['m_ada_b'], 'm_pre_norm_g': out['m_pre_norm_g'], 'm_post_norm_g': out['m_post_norm_g'], 'm_even_w_in': out['m_even_w_in'], 'm_even_sc_conv_w': out['m_even_sc_conv_w'], 'm_even_sc_conv_b': out['m_even_sc_conv_b'], 'm_even_q_norm_g': out['m_even_q_norm_g'], 'm_even_kv_norm_g': out['m_even_kv_norm_g'], 'm_even_w_uq': out['m_even_w_uq'], 'm_even_w_ukv': out['m_even_w_ukv'], 'm_even_w_out': out['m_even_w_out'], 'm_odd_w_in': out['m_odd_w_in'], 'm_odd_conv_w': out['m_odd_conv_w'], 'm_odd_conv_b': out['m_odd_conv_b'], 'm_odd_ln_g': out['m_odd_ln_g'], 'm_odd_ln_b': out['m_odd_ln_b'], 'm_odd_w_out': out['m_odd_w_out'], 'v_ada_w': out['v_ada_w'], 'v_ada_b': out['v_ada_b'], 'v_pre_norm_g': out['v_pre_norm_g'], 'v_post_norm_g': out['v_post_norm_g'], 'v_even_w_in': out['v_even_w_in'], 'v_even_sc_conv_w': out['v_even_sc_conv_w'], 'v_even_sc_conv_b': out['v_even_sc_conv_b'], 'v_even_q_norm_g': out['v_even_q_norm_g'], 'v_even_kv_norm_g': out['v_even_kv_norm_g'], 'v_even_w_uq': out['v_even_w_uq'], 'v_even_w_ukv': out['v_even_w_ukv'], 'v_even_w_out': out['v_even_w_out'], 'v_odd_w_in': out['v_odd_w_in'], 'v_odd_conv_w': out['v_odd_conv_w'], 'v_odd_conv_b': out['v_odd_conv_b'], 'v_odd_ln_g': out['v_odd_ln_g'], 'v_odd_ln_b': out['v_odd_ln_b'], 'v_odd_w_out': out['v_odd_w_out']}


def _loss(weights, diff, rest, loss_target):
    with _jax.named_scope("forward"):
        args = {**rest, TWIN_DIFF_INPUT: diff, **{k: w.astype(_WEIGHT_DTYPES[k]) for k, w in weights.items()}}
        y = _forward(args)
    with _jax.named_scope("loss_head"):
        err = _jnp.square(y.astype(_jnp.float32) - loss_target)
        return 0.5 * _jnp.sum(_jnp.mean(err, axis=-1)) if err.ndim else 0.5 * err


def _adamw(w, g, m, v):
    m = ADAM_B1 * m + (1.0 - ADAM_B1) * g
    v = ADAM_B2 * v + (1.0 - ADAM_B2) * _jnp.square(g)
    m_hat = m / (1.0 - ADAM_B1 ** ADAM_STEP)
    v_hat = v / (1.0 - ADAM_B2 ** ADAM_STEP)
    delta = -ADAM_LR * (m_hat / (_jnp.sqrt(v_hat) + ADAM_EPS) + ADAM_WD * w)
    return delta, m, v


def reference(x, c, positions, ada_w, ada_b, pre_norm_g, post_norm_g, even_w_in, even_sc_conv_w, even_sc_conv_b, even_q_norm_g, even_kv_norm_g, even_w_uq, even_w_ukv, even_w_out, odd_w_in, odd_conv_w, odd_conv_b, odd_ln_g, odd_ln_b, odd_w_out, loss_target, m_ada_w, m_ada_b, m_pre_norm_g, m_post_norm_g, m_even_w_in, m_even_sc_conv_w, m_even_sc_conv_b, m_even_q_norm_g, m_even_kv_norm_g, m_even_w_uq, m_even_w_ukv, m_even_w_out, m_odd_w_in, m_odd_conv_w, m_odd_conv_b, m_odd_ln_g, m_odd_ln_b, m_odd_w_out, v_ada_w, v_ada_b, v_pre_norm_g, v_post_norm_g, v_even_w_in, v_even_sc_conv_w, v_even_sc_conv_b, v_even_q_norm_g, v_even_kv_norm_g, v_even_w_uq, v_even_w_ukv, v_even_w_out, v_odd_w_in, v_odd_conv_w, v_odd_conv_b, v_odd_ln_g, v_odd_ln_b, v_odd_w_out):
    given = dict(x=x, c=c, positions=positions, ada_w=ada_w, ada_b=ada_b, pre_norm_g=pre_norm_g, post_norm_g=post_norm_g, even_w_in=even_w_in, even_sc_conv_w=even_sc_conv_w, even_sc_conv_b=even_sc_conv_b, even_q_norm_g=even_q_norm_g, even_kv_norm_g=even_kv_norm_g, even_w_uq=even_w_uq, even_w_ukv=even_w_ukv, even_w_out=even_w_out, odd_w_in=odd_w_in, odd_conv_w=odd_conv_w, odd_conv_b=odd_conv_b, odd_ln_g=odd_ln_g, odd_ln_b=odd_ln_b, odd_w_out=odd_w_out, loss_target=loss_target, m_ada_w=m_ada_w, m_ada_b=m_ada_b, m_pre_norm_g=m_pre_norm_g, m_post_norm_g=m_post_norm_g, m_even_w_in=m_even_w_in, m_even_sc_conv_w=m_even_sc_conv_w, m_even_sc_conv_b=m_even_sc_conv_b, m_even_q_norm_g=m_even_q_norm_g, m_even_kv_norm_g=m_even_kv_norm_g, m_even_w_uq=m_even_w_uq, m_even_w_ukv=m_even_w_ukv, m_even_w_out=m_even_w_out, m_odd_w_in=m_odd_w_in, m_odd_conv_w=m_odd_conv_w, m_odd_conv_b=m_odd_conv_b, m_odd_ln_g=m_odd_ln_g, m_odd_ln_b=m_odd_ln_b, m_odd_w_out=m_odd_w_out, v_ada_w=v_ada_w, v_ada_b=v_ada_b, v_pre_norm_g=v_pre_norm_g, v_post_norm_g=v_post_norm_g, v_even_w_in=v_even_w_in, v_even_sc_conv_w=v_even_sc_conv_w, v_even_sc_conv_b=v_even_sc_conv_b, v_even_q_norm_g=v_even_q_norm_g, v_even_kv_norm_g=v_even_kv_norm_g, v_even_w_uq=v_even_w_uq, v_even_w_ukv=v_even_w_ukv, v_even_w_out=v_even_w_out, v_odd_w_in=v_odd_w_in, v_odd_conv_w=v_odd_conv_w, v_odd_conv_b=v_odd_conv_b, v_odd_ln_g=v_odd_ln_g, v_odd_ln_b=v_odd_ln_b, v_odd_w_out=v_odd_w_out)
    weights = {n: given[n] for n in TWIN_WEIGHTS}
    shared = {n: given[n] for n in SHARED_INPUTS}
    per_example = {n: given[n] for n in ['x', 'c', 'positions']}
    grad_fn = _jax.value_and_grad(_loss, argnums=(0, 1))

    def one_microbatch(ex, loss_target):
        ex = dict(ex)
        diff = ex.pop(TWIN_DIFF_INPUT)
        return grad_fn(weights, diff, {**shared, **ex}, loss_target)

    if N_MICROBATCH == 1:
        loss, (grad_w, grad_x) = one_microbatch(per_example, given["loss_target"])
    else:
        def body(carry, xs):
            loss_sum, grad_sum = carry
            l_k, (gw_k, gx_k) = one_microbatch(xs[0], xs[1])
            with _jax.named_scope("update"):
                return (loss_sum + l_k, _jax.tree.map(_jnp.add, grad_sum, gw_k)), gx_k

        init = (_jnp.zeros((), _jnp.float32), _jax.tree.map(_jnp.zeros_like, weights))
        (loss, grad_w), grad_x = _jax.lax.scan(body, init, (per_example, given["loss_target"]))
    with _jax.named_scope("update"):
        delta_w, new_m, new_v = {}, {}, {}
        for n in TWIN_WEIGHTS:
            delta_w[n], new_m[n], new_v[n] = _adamw(weights[n], grad_w[n], given["m_" + n], given["v_" + n])
    return (loss, grad_x, *[grad_w[n] for n in TWIN_WEIGHTS], *[delta_w[n] for n in TWIN_WEIGHTS],
            *[new_m[n] for n in TWIN_WEIGHTS], *[new_v[n] for n in TWIN_WEIGHTS])
```

```python
import functools

import jax
import jax.numpy as jnp
from jax import lax
from jax.experimental import pallas as pl
from jax.experimental.pallas import tpu as pltpu

F32 = jnp.float32
MXU_DTYPE = jnp.bfloat16
MESH = pl.DeviceIdType.MESH
VMEM_LIMIT_V7X = 56 * 2 ** 20

EPS = 1e-6
D_MODEL = 1024
DEPTH = 4
CHUNK = 64
SC_WIDTH = 512
SC_KERNEL = 3
SC_HALO = 8
HEADS = 8
QK_NOPE = 64
QK_ROPE = 32
V_HEAD = 64
HEAD_PAD = 128
Q_LORA = 256
KV_LORA = 128
ROPE_THETA = 10000.0
CONF_KERNEL = 31
CONF_HALO = 32
CONV_ROWS = 32
EVEN_IN = 2976
EVEN_PAD = 3072
ODD_IN = 3072
N_CHIPS = 4
N_DEV = 8
NEG = -1e30

ADAM_LR = 0.001
ADAM_B1 = 0.9
ADAM_B2 = 0.999
ADAM_EPS = 1e-08
ADAM_WD = 0.01
ADAM_STEP = 10

SHARD_ELEMS = 2 * (1024 * 744 + 256 * 192 + 128 * 256 + 256 * 1024 + 1024 * 768 + 256 * 1024)
RS_ROWS, RS_COLS = 264, 8192
RS_PAD = 2 * RS_ROWS * RS_COLS - SHARD_ELEMS


def _cp(n_grid=0, **kw):
    return pltpu.CompilerParams(dimension_semantics=("arbitrary",) * n_grid,
                                vmem_limit_bytes=VMEM_LIMIT_V7X, **kw)


def _sigmoid(x):
    return 1.0 / (1.0 + jnp.exp(-x))


def _silu(x):
    return x * _sigmoid(x)


def _dsilu(x):
    s = _sigmoid(x)
    return s * (1.0 + x * (1.0 - s))


def _rms(x, g):
    return x * lax.rsqrt(jnp.mean(x * x, axis=-1, keepdims=True) + EPS) * g


def _dot(a, b, dims):
    return lax.dot_general(a.astype(MXU_DTYPE), b.astype(MXU_DTYPE), (dims, ((), ())),
                           preferred_element_type=F32)


def _dot_nn(a, b):
    return _dot(a, b, ((1,), (0,)))


def _dot_nt(a, b):
    return _dot(a, b, ((1,), (1,)))


def _dot_tn(a, b):
    return _dot(a, b, ((0,), (0,)))


def _rows(ts, w, cb=0):
    return pl.BlockSpec((ts, w), lambda i: (i, cb))


def _vec(w, cb=0, r=1):
    return pl.BlockSpec((r, w), lambda i: (0, cb))


def _prev_halo(ts, hr, w, cb):
    return pl.BlockSpec((hr, w), lambda i: (jnp.maximum(i * (ts // hr) - 1, 0), cb))


def _next_halo(ts, hr, w, cb, s):
    return pl.BlockSpec((hr, w), lambda i: (jnp.minimum((i + 1) * (ts // hr), s // hr - 1), cb))


def _sds(shape, dtype=F32):
    return jax.ShapeDtypeStruct(shape, dtype)


def _mm(a, b, mode, out_dtype, tm, tn, name):
    tm = min(tm, a.shape[1] if mode == "tn" else a.shape[0])
    tn = min(tn, b.shape[0] if mode == "nt" else b.shape[1])
    if mode == "nn":
        (m, k), n = a.shape, b.shape[1]
        a_spec = pl.BlockSpec((tm, k), lambda i, j: (i, 0))
        b_spec = pl.BlockSpec((k, tn), lambda i, j: (0, j))
        dot = _dot_nn
    elif mode == "nt":
        (m, k), n = a.shape, b.shape[0]
        a_spec = pl.BlockSpec((tm, k), lambda i, j: (i, 0))
        b_spec = pl.BlockSpec((tn, k), lambda i, j: (j, 0))
        dot = _dot_nt
    else:
        (k, m), n = a.shape, b.shape[1]
        a_spec = pl.BlockSpec((k, tm), lambda i, j: (0, i))
        b_spec = pl.BlockSpec((k, tn), lambda i, j: (0, j))
        dot = _dot_tn
    assert m % tm == 0 and n % tn == 0, (name, m, n, tm, tn)

    def body(a_ref, b_ref, o_ref):
        o_ref[...] = dot(a_ref[...], b_ref[...]).astype(o_ref.dtype)

    return pl.pallas_call(
        body, name=name, grid=(m // tm, n // tn), in_specs=[a_spec, b_spec],
        out_specs=pl.BlockSpec((tm, tn), lambda i, j: (i, j)), out_shape=_sds((m, n), out_dtype),
        compiler_params=_cp(2))(a, b)


def _rope_tables(pos_col, invf):
    s = pos_col.shape[0]
    ts = min(512, s)

    def body(p_ref, f_ref, c_ref, s_ref):
        ang = p_ref[...].astype(F32) * f_ref[...]
        lane = lax.broadcasted_iota(jnp.int32, ang.shape, 1)
        rope = (lane >= QK_NOPE) & (lane < QK_NOPE + QK_ROPE)
        c_ref[...] = jnp.where(lane < QK_NOPE, 1.0, jnp.where(rope, jnp.cos(ang), 0.0))
        s_ref[...] = jnp.where(rope, jnp.sin(ang), 0.0)

    return pl.pallas_call(
        body, name="rope_tables", grid=(s // ts,), in_specs=[_rows(ts, 1), _vec(HEAD_PAD)],
        out_specs=[_rows(ts, HEAD_PAD)] * 2, out_shape=[_sds((s, HEAD_PAD))] * 2,
        compiler_params=_cp(1))(pos_col, invf)


def _pre_fwd(x, g, mod_l, ts):
    s, d = x.shape

    def body(x_ref, g_ref, sh_ref, sc_ref, h_ref):
        h = _rms(x_ref[...], g_ref[...]) * (1.0 + sc_ref[...]) + sh_ref[...]
        h_ref[...] = h.astype(h_ref.dtype)

    return pl.pallas_call(
        body, name="pre_fwd", grid=(s // ts,),
        in_specs=[_rows(ts, d), _vec(d), _vec(d, 0), _vec(d, 1)],
        out_specs=_rows(ts, d), out_shape=_sds((s, d), MXU_DTYPE), compiler_params=_cp(1))(x, g, mod_l, mod_l)


def _pre_bwd(dh, dx_out, x, g, mod_l, ts):
    s, d = x.shape

    def f(xv, gv, sh, sc):
        return _rms(xv, gv) * (1.0 + sc) + sh

    def body(dh_ref, dxo_ref, x_ref, g_ref, sh_ref, sc_ref, dx_ref, dsh_ref, dsc_ref, dg_ref):
        i = pl.program_id(0)
        _, vjp = jax.vjp(f, x_ref[...], g_ref[...], sh_ref[...], sc_ref[...])
        dx, dg, dsh, dsc = vjp(dh_ref[...])
        dx_ref[...] = dxo_ref[...] + dx

        @pl.when(i == 0)
        def _():
            dsh_ref[...] = jnp.zeros_like(dsh_ref)
            dsc_ref[...] = jnp.zeros_like(dsc_ref)
            dg_ref[...] = jnp.zeros_like(dg_ref)

        dsh_ref[...] += dsh
        dsc_ref[...] += dsc
        dg_ref[...] += dg

    return pl.pallas_call(
        body, name="pre_bwd", grid=(s // ts,),
        in_specs=[_rows(ts, d), _rows(ts, d), _rows(ts, d), _vec(d), _vec(d, 0), _vec(d, 1)],
        out_specs=[_rows(ts, d), _vec(d), _vec(d), _vec(d)],
        out_shape=[_sds((s, d)), _sds((1, d)), _sds((1, d)), _sds((1, d))],
        compiler_params=_cp(1))(dh, dx_out, x, g, mod_l, mod_l)


def _post_fwd(x, yo, g, mod_l, ts):
    s, d = x.shape

    def body(x_ref, yo_ref, g_ref, gate_ref, o_ref):
        o_ref[...] = x_ref[...] + gate_ref[...] * _rms(yo_ref[...], g_ref[...])

    return pl.pallas_call(
        body, name="post_fwd", grid=(s // ts,),
        in_specs=[_rows(ts, d), _rows(ts, d), _vec(d), _vec(d, 2)],
        out_specs=_rows(ts, d), out_shape=_sds((s, d)), compiler_params=_cp(1))(x, yo, g, mod_l)


def _post_bwd(dx_out, yo, g, mod_l, ts):
    s, d = yo.shape

    def f(yov, gv, gate):
        return gate * _rms(yov, gv)

    def body(dx_ref, yo_ref, g_ref, gate_ref, dyo_ref, dgate_ref, dg_ref):
        i = pl.program_id(0)
        _, vjp = jax.vjp(f, yo_ref[...], g_ref[...], gate_ref[...])
        dyo, dg, dgate = vjp(dx_ref[...])
        dyo_ref[...] = dyo.astype(dyo_ref.dtype)

        @pl.when(i == 0)
        def _():
            dgate_ref[...] = jnp.zeros_like(dgate_ref)
            dg_ref[...] = jnp.zeros_like(dg_ref)

        dgate_ref[...] += dgate
        dg_ref[...] += dg

    return pl.pallas_call(
        body, name="post_bwd", grid=(s // ts,),
        in_specs=[_rows(ts, d), _rows(ts, d), _vec(d), _vec(d, 2)],
        out_specs=[_rows(ts, d), _vec(d), _vec(d)],
        out_shape=[_sds((s, d), MXU_DTYPE), _sds((1, d)), _sds((1, d))],
        compiler_params=_cp(1))(dx_out, yo, g, mod_l)


def _loss_fwd_bwd(x, target, ts):
    s, d = x.shape

    def body(x_ref, t_ref, loss_ref, dx_ref):
        i = pl.program_id(0)
        err = x_ref[...] - t_ref[...]
        dx_ref[...] = err * (1.0 / d)

        @pl.when(i == 0)
        def _():
            loss_ref[...] = jnp.zeros_like(loss_ref)

        loss_ref[...] += 0.5 * jnp.sum(jnp.sum(err * err, axis=-1, keepdims=True) * (1.0 / d), axis=0, keepdims=True)

    return pl.pallas_call(
        body, name="loss", grid=(s // ts,), in_specs=[_rows(ts, d), _rows(ts, d)],
        out_specs=[_vec(1), _rows(ts, d)], out_shape=[_sds((1, 1)), _sds((s, d))],
        compiler_params=_cp(1))(x, target)


def _rope(t, cos, sin):
    lane = lax.broadcasted_iota(jnp.int32, t.shape, 1)
    first = (lane >= QK_NOPE) & (lane < QK_NOPE + QK_ROPE // 2)
    second = (lane >= QK_NOPE + QK_ROPE // 2) & (lane < QK_NOPE + QK_ROPE)
    up = pltpu.roll(t, QK_ROPE // 2, 1)
    down = pltpu.roll(t, HEAD_PAD - QK_ROPE // 2, 1)
    return t * cos + jnp.where(first, -down, jnp.where(second, up, 0.0)) * sin


def _rope_transposed(g, cos, sin):
    lane = lax.broadcasted_iota(jnp.int32, g.shape, 1)
    first = (lane >= QK_NOPE) & (lane < QK_NOPE + QK_ROPE // 2)
    second = (lane >= QK_NOPE + QK_ROPE // 2) & (lane < QK_NOPE + QK_ROPE)
    u = g * sin
    up = pltpu.roll(u, QK_ROPE // 2, 1)
    down = pltpu.roll(u, HEAD_PAD - QK_ROPE // 2, 1)
    return g * cos + jnp.where(first, down, jnp.where(second, -up, 0.0))


def _mla_prep_fwd(z, cos, sin, qg, kvg, wq, wuk, wuv, ts):
    s = z.shape[0]

    def body(cq_ref, ckv_ref, kr_ref, cos_ref, sin_ref, qg_ref, kvg_ref, wq_ref, wuk_ref, wuv_ref,
             q_ref, k_ref, v_ref):
        cos_v, sin_v = cos_ref[...], sin_ref[...]
        cqn = _rms(cq_ref[...], qg_ref[...])
        q_ref[0] = _rope(_dot_nn(cqn, wq_ref[0]), cos_v, sin_v).astype(q_ref.dtype)
        ckvn = _rms(ckv_ref[...], kvg_ref[...])
        k_ref[0] = (_dot_nn(ckvn, wuk_ref[0]) + _rope(kr_ref[...], cos_v, sin_v)).astype(k_ref.dtype)
        v_ref[0] = _dot_nn(ckvn, wuv_ref[0]).astype(v_ref.dtype)

    row = lambda w, cb: pl.BlockSpec((ts, w), lambda i, h: (i, cb))
    vec = lambda w: pl.BlockSpec((1, w), lambda i, h: (0, 0))
    wsp = lambda k: pl.BlockSpec((1, k, HEAD_PAD), lambda i, h: (h, 0, 0))
    out = pl.BlockSpec((1, ts, HEAD_PAD), lambda i, h: (h, i, 0))
    return pl.pallas_call(
        body, name="mla_prep_fwd", grid=(s // ts, HEADS),
        in_specs=[row(Q_LORA, 8), row(KV_LORA, 18), row(HEAD_PAD, 19), row(HEAD_PAD, 0), row(HEAD_PAD, 0),
                  vec(Q_LORA), vec(KV_LORA), wsp(Q_LORA), wsp(KV_LORA), wsp(KV_LORA)],
        out_specs=[out] * 3, out_shape=[_sds((HEADS, s, HEAD_PAD), MXU_DTYPE)] * 3,
        compiler_params=_cp(2))(z, z, z, cos, sin, qg, kvg, wq, wuk, wuv)


def _mla_prep_bwd(dz, dq, dk, dv, z, cos, sin, qg, kvg, wq, wuk, wuv, ts):
    s = z.shape[0]

    def fq(cq, g):
        return _rms(cq, g)

    def body(dz_in_ref, dq_ref, dk_ref, dv_ref, cq_ref, ckv_ref, cos_ref, sin_ref, qg_ref, kvg_ref,
             wq_ref, wuk_ref, wuv_ref, dz_ref, dwq_ref, dwuk_ref, dwuv_ref, dqg_ref, dkvg_ref,
             dcqn_acc, dckvn_acc, dkr_acc):
        del dz_in_ref
        i, h = pl.program_id(0), pl.program_id(1)
        cos_v, sin_v = cos_ref[...], sin_ref[...]

        @pl.when((i == 0) & (h == 0))
        def _():
            dwq_ref[...] = jnp.zeros_like(dwq_ref)
            dwuk_ref[...] = jnp.zeros_like(dwuk_ref)
            dwuv_ref[...] = jnp.zeros_like(dwuv_ref)
            dqg_ref[...] = jnp.zeros_like(dqg_ref)
            dkvg_ref[...] = jnp.zeros_like(dkvg_ref)

        @pl.when(h == 0)
        def _():
            dcqn_acc[...] = jnp.zeros_like(dcqn_acc)
            dckvn_acc[...] = jnp.zeros_like(dckvn_acc)
            dkr_acc[...] = jnp.zeros_like(dkr_acc)

        cqn = _rms(cq_ref[...], qg_ref[...])
        ckvn = _rms(ckv_ref[...], kvg_ref[...])
        dq_lin = _rope_transposed(dq_ref[0], cos_v, sin_v)
        dcqn_acc[...] += _dot_nt(dq_lin, wq_ref[0])
        dwq_ref[h] += _dot_tn(cqn, dq_lin)
        dkh, dvh = dk_ref[0], dv_ref[0]
        lane = lax.broadcasted_iota(jnp.int32, dkh.shape, 1)
        dkr_acc[...] += jnp.where((lane >= QK_NOPE) & (lane < QK_NOPE + QK_ROPE), dkh, 0.0)
        dckvn_acc[...] += _dot_nt(dkh, wuk_ref[0]) + _dot_nt(dvh, wuv_ref[0])
        dwuk_ref[h] += _dot_tn(ckvn, dkh)
        dwuv_ref[h] += _dot_tn(ckvn, dvh)

        @pl.when(h == HEADS - 1)
        def _():
            _, vjp_q = jax.vjp(fq, cq_ref[...], qg_ref[...])
            dcq, dqg = vjp_q(dcqn_acc[...])
            _, vjp_kv = jax.vjp(fq, ckv_ref[...], kvg_ref[...])
            dckv, dkvg = vjp_kv(dckvn_acc[...])
            dz_ref[:, 0:Q_LORA] = dcq
            dz_ref[:, Q_LORA:Q_LORA + KV_LORA] = dckv
            dz_ref[:, Q_LORA + KV_LORA:] = _rope_transposed(dkr_acc[...], cos_v, sin_v)
            dqg_ref[...] += dqg
            dkvg_ref[...] += dkvg

    row = lambda w, cb: pl.BlockSpec((ts, w), lambda i, h: (i, cb))
    vec = lambda w: pl.BlockSpec((1, w), lambda i, h: (0, 0))
    wsp = lambda k: pl.BlockSpec((1, k, HEAD_PAD), lambda i, h: (h, 0, 0))
    hrow = pl.BlockSpec((1, ts, HEAD_PAD), lambda i, h: (h, i, 0))
    whole = lambda k: pl.BlockSpec((HEADS, k, HEAD_PAD), lambda i, h: (0, 0, 0))
    return pl.pallas_call(
        body, name="mla_prep_bwd", grid=(s // ts, HEADS),
        in_specs=[pl.BlockSpec(memory_space=pl.ANY), hrow, hrow, hrow, row(Q_LORA, 8), row(KV_LORA, 18),
                  row(HEAD_PAD, 0), row(HEAD_PAD, 0), vec(Q_LORA), vec(KV_LORA), wsp(Q_LORA), wsp(KV_LORA), wsp(KV_LORA)],
        out_specs=[row(512, 4), whole(Q_LORA), whole(KV_LORA), whole(KV_LORA), vec(Q_LORA), vec(KV_LORA)],
        out_shape=[_sds(dz.shape), _sds((HEADS, Q_LORA, HEAD_PAD)), _sds((HEADS, KV_LORA, HEAD_PAD)),
                   _sds((HEADS, KV_LORA, HEAD_PAD)), _sds((1, Q_LORA)), _sds((1, KV_LORA))],
        scratch_shapes=[pltpu.VMEM((ts, Q_LORA), F32), pltpu.VMEM((ts, KV_LORA), F32), pltpu.VMEM((ts, HEAD_PAD), F32)],
        input_output_aliases={0: 0}, compiler_params=_cp(2))(dz, dq, dk, dv, z, z, cos, sin, qg, kvg, wq, wuk, wuv)


def _chunk_mask(q0, k0, tq, tk):
    rows = q0 + lax.broadcasted_iota(jnp.int32, (tq, tk), 0)
    cols = k0 + lax.broadcasted_iota(jnp.int32, (tq, tk), 1)
    return lax.shift_right_logical(cols, 6) <= lax.shift_right_logical(rows, 6)


def _attn_fwd(q, k, v, tq):
    s = q.shape[1]
    nq = s // tq
    scale = 1.0 / float(QK_NOPE + QK_ROPE) ** 0.5

    def body(q_ref, k_ref, v_ref, o_ref, lse_ref):
        qi, hh = pl.program_id(1), pl.program_id(2)
        qv = q_ref[0]

        def step(kj, carry, masked):
            m, l, acc = carry
            k0 = pl.multiple_of(kj * tq, tq)
            sc = _dot_nt(qv, k_ref[0, pl.ds(k0, tq), :]) * scale
            if masked:
                sc = jnp.where(_chunk_mask(qi * tq, k0, tq, tq), sc, NEG)
            m_new = jnp.maximum(m, jnp.max(sc, axis=-1, keepdims=True))
            alpha = jnp.exp(m - m_new)
            p = jnp.exp(sc - m_new)
            l = alpha * l + jnp.sum(p, axis=-1, keepdims=True)
            acc = alpha * acc + _dot_nn(p, v_ref[0, pl.ds(k0, tq), :])
            return m_new, l, acc

        init = (jnp.full((tq, 1), NEG, F32), jnp.zeros((tq, 1), F32), jnp.zeros((tq, HEAD_PAD), F32))
        carry = lax.fori_loop(0, qi, lambda kj, c: step(kj, c, False), init)
        m, l, acc = step(qi, carry, True)
        o = acc / l
        lse_ref[0] = m + jnp.log(l)

        @pl.when(hh == 0)
        def _():
            o_ref[...] = o

        @pl.when(hh == 1)
        def _():
            o_ref[...] += o

    head = lambda hp, qi, hh: 2 * hp + hh
    return pl.pallas_call(
        body, name="attn_fwd", grid=(HEADS // 2, nq, 2),
        in_specs=[pl.BlockSpec((1, tq, HEAD_PAD), lambda hp, qi, hh: (head(hp, qi, hh), qi, 0)),
                  pl.BlockSpec((1, s, HEAD_PAD), lambda hp, qi, hh: (head(hp, qi, hh), 0, 0)),
                  pl.BlockSpec((1, s, HEAD_PAD), lambda hp, qi, hh: (head(hp, qi, hh), 0, 0))],
        out_specs=[pl.BlockSpec((tq, HEAD_PAD), lambda hp, qi, hh: (qi, hp)),
                   pl.BlockSpec((1, tq, 1), lambda hp, qi, hh: (head(hp, qi, hh), qi, 0))],
        out_shape=[_sds((s, HEADS * V_HEAD)), _sds((HEADS, s, 1))],
        compiler_params=_cp(3))(q, k, v)


def _attn_bwd(q, k, v, do, o, lse, tq):
    s = q.shape[1]
    nq = s // tq
    scale = 1.0 / float(QK_NOPE + QK_ROPE) ** 0.5

    def body(q_ref, k_ref, v_ref, do_ref, o_ref, lse_ref, dq_ref, dk_ref, dv_ref):
        hh, kj = pl.program_id(1), pl.program_id(2)

        @pl.when(kj == 0)
        def _():
            dq_ref[...] = jnp.zeros_like(dq_ref)

        kv, vv = k_ref[0], v_ref[0]
        lane = lax.broadcasted_iota(jnp.int32, (tq, HEAD_PAD), 1)
        mine = lax.shift_right_logical(lane, 6) == hh

        def step(qi, carry, masked):
            dk_acc, dv_acc = carry
            q0 = pl.multiple_of(qi * tq, tq)
            qv = q_ref[0, pl.ds(q0, tq), :]
            dov = do_ref[pl.ds(q0, tq), :]
            delta = jnp.sum(jnp.where(mine, dov * o_ref[pl.ds(q0, tq), :], 0.0), axis=-1, keepdims=True)
            sc = _dot_nt(qv, kv) * scale
            if masked:
                sc = jnp.where(_chunk_mask(q0, kj * tq, tq, tq), sc, NEG)
            p = jnp.exp(sc - lse_ref[0, pl.ds(q0, tq), :])
            do_b = dov.astype(MXU_DTYPE)
            ds = (p * (_dot_nt(do_b, vv) - delta) * scale).astype(MXU_DTYPE)
            dv_acc = dv_acc + _dot_tn(p, do_b)
            dk_acc = dk_acc + _dot_tn(ds, qv)
            dq_ref[0, pl.ds(q0, tq), :] += _dot_nn(ds, kv)
            return dk_acc, dv_acc

        zero = jnp.zeros((tq, HEAD_PAD), F32)
        carry = step(kj, (zero, zero), True)
        dk_acc, dv_acc = lax.fori_loop(kj + 1, nq, lambda qi, c: step(qi, c, False), carry)
        dk_ref[0] = dk_acc
        dv_ref[0] = dv_acc

    head = lambda hp, hh, kj: 2 * hp + hh
    full = pl.BlockSpec((1, s, HEAD_PAD), lambda hp, hh, kj: (head(hp, hh, kj), 0, 0))
    blk = pl.BlockSpec((1, tq, HEAD_PAD), lambda hp, hh, kj: (head(hp, hh, kj), kj, 0))
    pair = pl.BlockSpec((s, HEAD_PAD), lambda hp, hh, kj: (0, hp))
    return pl.pallas_call(
        body, name="attn_bwd", grid=(HEADS // 2, 2, nq),
        in_specs=[full, blk, blk, pair, pair, pl.BlockSpec((1, s, 1), lambda hp, hh, kj: (head(hp, hh, kj), 0, 0))],
        out_specs=[full, blk, blk], out_shape=[_sds((HEADS, s, HEAD_PAD))] * 3,
        compiler_params=_cp(3))(q, k, v, do, o, lse)


def _sc_conv(u, ubuf, w_ref, b_ref, ts):
    return (w_ref[2:3, :] * u + w_ref[1:2, :] * ubuf[pl.ds(SC_HALO - 1, ts), :]
            + w_ref[0:1, :] * ubuf[pl.ds(SC_HALO - 2, ts), :] + b_ref[...])


def _even_gate_fwd(z, o, sc_w, sc_b, ts):
    s = z.shape[0]
    w = SC_WIDTH

    def body(ab_ref, ac_ref, ax_ref, ag_ref, bg_ref, hc_ref, hx_ref, o_ref, w_ref, b_ref, y_ref, ubuf):
        i = pl.program_id(0)
        u = ac_ref[...] * ax_ref[...]
        ubuf[0:SC_HALO, :] = jnp.where(i > 0, hc_ref[...] * hx_ref[...], 0.0)
        ubuf[SC_HALO:, :] = u
        conv = _sc_conv(u, ubuf, w_ref, b_ref, ts)
        y_ref[:, 0:w] = (ab_ref[...] * conv * _silu(ag_ref[...])).astype(y_ref.dtype)
        y_ref[:, w:] = (o_ref[...] * _silu(bg_ref[...])).astype(y_ref.dtype)

    return pl.pallas_call(
        body, name="even_gate_fwd", grid=(s // ts,),
        in_specs=[_rows(ts, w, 0), _rows(ts, w, 1), _rows(ts, w, 2), _rows(ts, w, 3), _rows(ts, w, 5),
                  _prev_halo(ts, SC_HALO, w, 1), _prev_halo(ts, SC_HALO, w, 2), _rows(ts, w),
                  _vec(w, 0, SC_KERNEL), _vec(w)],
        out_specs=_rows(ts, 2 * w), out_shape=_sds((s, 2 * w), MXU_DTYPE),
        scratch_shapes=[pltpu.VMEM((ts + SC_HALO, w), F32)],
        compiler_params=_cp(1))(z, z, z, z, z, z, z, o, sc_w, sc_b)


def _even_gate_bwd(dy, z, o, sc_w, sc_b, ts):
    s = z.shape[0]
    w = SC_WIDTH
    n = s // ts

    def body(dya_ref, dyb_ref, dyan_ref, ab_ref, ac_ref, ax_ref, ag_ref, bg_ref, hc_ref, hx_ref, abn_ref, agn_ref,
             o_ref, w_ref, b_ref, dz_ref, do_ref, dw_ref, db_ref, ubuf, dbuf):
        i = pl.program_id(0)
        ab, ac, ax, ag, bg = ab_ref[...], ac_ref[...], ax_ref[...], ag_ref[...], bg_ref[...]
        dya, dyb = dya_ref[...], dyb_ref[...]
        u = ac * ax
        ubuf[0:SC_HALO, :] = jnp.where(i > 0, hc_ref[...] * hx_ref[...], 0.0)
        ubuf[SC_HALO:, :] = u
        conv = _sc_conv(u, ubuf, w_ref, b_ref, ts)
        sg = _silu(ag)
        dconv = dya * ab * sg
        dbuf[0:ts, :] = dconv
        dbuf[ts:, :] = jnp.where(i < n - 1, dyan_ref[...] * abn_ref[...] * _silu(agn_ref[...]), 0.0)
        du = w_ref[2:3, :] * dconv + w_ref[1:2, :] * dbuf[pl.ds(1, ts), :] + w_ref[0:1, :] * dbuf[pl.ds(2, ts), :]
        dz_ref[:, 0:w] = dya * conv * sg
        dz_ref[:, w:2 * w] = du * ax
        dz_ref[:, 2 * w:3 * w] = du * ac
        dz_ref[:, 3 * w:4 * w] = dya * ab * conv * _dsilu(ag)
        dz_ref[:, 4 * w:5 * w] = jnp.zeros((ts, w), F32)
        dz_ref[:, 5 * w:] = dyb * o_ref[...] * _dsilu(bg)
        do_ref[...] = dyb * _silu(bg)

        @pl.when(i == 0)
        def _():
            dw_ref[...] = jnp.zeros_like(dw_ref)
            db_ref[...] = jnp.zeros_like(db_ref)

        dw_ref[0:1, :] += jnp.sum(dconv * ubuf[pl.ds(SC_HALO - 2, ts), :], axis=0, keepdims=True)
        dw_ref[1:2, :] += jnp.sum(dconv * ubuf[pl.ds(SC_HALO - 1, ts), :], axis=0, keepdims=True)
        dw_ref[2:3, :] += jnp.sum(dconv * u, axis=0, keepdims=True)
        db_ref[...] += jnp.sum(dconv, axis=0, keepdims=True)

    return pl.pallas_call(
        body, name="even_gate_bwd", grid=(n,),
        in_specs=[_rows(ts, w, 0), _rows(ts, w, 1), _next_halo(ts, SC_HALO, w, 0, s),
                  _rows(ts, w, 0), _rows(ts, w, 1), _rows(ts, w, 2), _rows(ts, w, 3), _rows(ts, w, 5),
                  _prev_halo(ts, SC_HALO, w, 1), _prev_halo(ts, SC_HALO, w, 2),
                  _next_halo(ts, SC_HALO, w, 0, s), _next_halo(ts, SC_HALO, w, 3, s),
                  _rows(ts, w), _vec(w, 0, SC_KERNEL), _vec(w)],
        out_specs=[_rows(ts, EVEN_PAD), _rows(ts, w), _vec(w, 0, SC_KERNEL), _vec(w)],
        out_shape=[_sds((s, EVEN_PAD)), _sds((s, w)), _sds((SC_KERNEL, w)), _sds((1, w))],
        scratch_shapes=[pltpu.VMEM((ts + SC_HALO, w), F32), pltpu.VMEM((ts + SC_HALO, w), F32)],
        compiler_params=_cp(1))(dy, dy, dy, z, z, z, z, z, z, z, z, z, o, sc_w, sc_b)


def _ln_act(uc, sg, g, b):
    mu = jnp.mean(uc, axis=-1, keepdims=True)
    var = jnp.mean(jnp.square(uc - mu), axis=-1, keepdims=True)
    return _silu((uc - mu) * lax.rsqrt(var + EPS) * g + b) * _silu(sg)


def _odd_fwd(z, conv_w, conv_b, ln_g, ln_b, ts):
    s = z.shape[0]
    d = D_MODEL
    k = CONF_KERNEL

    def body(val_ref, glu_ref, sg_ref, hval_ref, hglu_ref, w_ref, b_ref, g_ref, beta_ref, y_ref, uc_ref, ubuf):
        i = pl.program_id(0)
        ubuf[0:CONF_HALO, :] = jnp.where(i > 0, hval_ref[...] * _sigmoid(hglu_ref[...]), 0.0)
        ubuf[CONF_HALO:, :] = val_ref[...] * _sigmoid(glu_ref[...])
        for r0 in range(0, ts, CONV_ROWS):
            acc = jnp.broadcast_to(b_ref[...], (CONV_ROWS, d))
            for j in range(k):
                acc = acc + w_ref[j:j + 1, :] * ubuf[pl.ds(r0 + CONF_HALO - (k - 1) + j, CONV_ROWS), :]
            uc_ref[r0:r0 + CONV_ROWS, :] = acc
        y_ref[...] = _ln_act(uc_ref[...], sg_ref[...], g_ref[...], beta_ref[...]).astype(y_ref.dtype)

    return pl.pallas_call(
        body, name="odd_fwd", grid=(s // ts,),
        in_specs=[_rows(ts, d, 0), _rows(ts, d, 1), _rows(ts, d, 2),
                  _prev_halo(ts, CONF_HALO, d, 0), _prev_halo(ts, CONF_HALO, d, 1),
                  _vec(d, 0, k), _vec(d), _vec(d), _vec(d)],
        out_specs=[_rows(ts, d), _rows(ts, d)], out_shape=[_sds((s, d), MXU_DTYPE), _sds((s, d))],
        scratch_shapes=[pltpu.VMEM((ts + CONF_HALO, d), F32)],
        compiler_params=_cp(1))(z, z, z, z, z, conv_w, conv_b, ln_g, ln_b)


def _odd_bwd(dy, z, uc, conv_w, ln_g, ln_b, ts):
    s = z.shape[0]
    d = D_MODEL
    k = CONF_KERNEL
    n = s // ts

    def body(dy_ref, dyn_ref, val_ref, glu_ref, sg_ref, sgn_ref, hval_ref, hglu_ref, uc_ref, ucn_ref,
             w_ref, g_ref, beta_ref, dz_ref, dw_ref, db_ref, dg_ref, dbeta_ref, ubuf, dbuf):
        i = pl.program_id(0)
        val, glu = val_ref[...], glu_ref[...]
        sig = _sigmoid(glu)
        ubuf[0:CONF_HALO, :] = jnp.where(i > 0, hval_ref[...] * _sigmoid(hglu_ref[...]), 0.0)
        ubuf[CONF_HALO:, :] = val * sig
        _, vjp = jax.vjp(_ln_act, uc_ref[...], sg_ref[...], g_ref[...], beta_ref[...])
        duc, dsg, dg, dbeta = vjp(dy_ref[...])
        _, vjp_n = jax.vjp(_ln_act, ucn_ref[...], sgn_ref[...], g_ref[...], beta_ref[...])
        dbuf[0:ts, :] = duc
        dbuf[ts:, :] = jnp.where(i < n - 1, vjp_n(dyn_ref[...])[0], 0.0)
        dz_ref[:, 2 * d:] = dsg

        @pl.when(i == 0)
        def _():
            dw_ref[...] = jnp.zeros_like(dw_ref)
            db_ref[...] = jnp.zeros_like(db_ref)
            dg_ref[...] = jnp.zeros_like(dg_ref)
            dbeta_ref[...] = jnp.zeros_like(dbeta_ref)

        db_ref[...] += jnp.sum(duc, axis=0, keepdims=True)
        dg_ref[...] += dg
        dbeta_ref[...] += dbeta
        for r0 in range(0, ts, CONV_ROWS):
            acc = jnp.zeros((CONV_ROWS, d), F32)
            for j in range(k):
                acc = acc + w_ref[j:j + 1, :] * dbuf[pl.ds(r0 + (k - 1) - j, CONV_ROWS), :]
            sig_r = sig[r0:r0 + CONV_ROWS, :]
            dz_ref[r0:r0 + CONV_ROWS, 0:d] = acc * sig_r
            dz_ref[r0:r0 + CONV_ROWS, d:2 * d] = acc * val[r0:r0 + CONV_ROWS, :] * sig_r * (1.0 - sig_r)
        for j in range(k):
            dw_ref[j:j + 1, :] += jnp.sum(duc * ubuf[pl.ds(CONF_HALO - (k - 1) + j, ts), :], axis=0, keepdims=True)

    return pl.pallas_call(
        body, name="odd_bwd", grid=(n,),
        in_specs=[_rows(ts, d), _next_halo(ts, CONF_HALO, d, 0, s),
                  _rows(ts, d, 0), _rows(ts, d, 1), _rows(ts, d, 2), _next_halo(ts, CONF_HALO, d, 2, s),
                  _prev_halo(ts, CONF_HALO, d, 0), _prev_halo(ts, CONF_HALO, d, 1),
                  _rows(ts, d), _next_halo(ts, CONF_HALO, d, 0, s),
                  _vec(d, 0, k), _vec(d), _vec(d)],
        out_specs=[_rows(ts, ODD_IN), _vec(d, 0, k), _vec(d), _vec(d), _vec(d)],
        out_shape=[_sds((s, ODD_IN)), _sds((k, d)), _sds((1, d)), _sds((1, d)), _sds((1, d))],
        scratch_shapes=[pltpu.VMEM((ts + CONF_HALO, d), F32), pltpu.VMEM((ts + CONF_HALO, d), F32)],
        compiler_params=_cp(1))(dy, dy, z, z, z, z, z, z, uc, uc, conv_w, ln_g, ln_b)


def _local_step(x, target, cos, sin, mod, p):
    s = x.shape[0]
    tsf, tsb = min(512, s // 2), min(256, s // 2)
    tq = min(512, s // 2)
    row1 = lambda a, i: a[i:i + 1]
    saved = []
    for layer in range(DEPTH):
        i = layer // 2
        mod_l = row1(mod, layer)
        h = _pre_fwd(x, row1(p["pre_norm_g"], layer), mod_l, tsf)
        if layer % 2 == 0:
            z = _mm(h, p["even_w_in"][i], "nn", F32, 512, 1024, "even_in_fwd")
            q, k, v = _mla_prep_fwd(z, cos, sin, row1(p["even_q_norm_g"], i), row1(p["even_kv_norm_g"], i),
                                    p["even_wq"][i], p["even_wuk"][i], p["even_wuv"][i], tsf)
            o, lse = _attn_fwd(q, k, v, tq)
            y = _even_gate_fwd(z, o, p["even_sc_conv_w"][i], row1(p["even_sc_conv_b"], i), tsf)
            yo = _mm(y, p["even_w_out"][i], "nn", F32, 512, 1024, "even_out_fwd")
            saved.append((x, h, z, y, yo, (q, k, v, o, lse)))
        else:
            z = _mm(h, p["odd_w_in"][i], "nn", F32, 512, 1024, "odd_in_fwd")
            y, uc = _odd_fwd(z, p["odd_conv_w"][i], row1(p["odd_conv_b"], i), row1(p["odd_ln_g"], i),
                             row1(p["odd_ln_b"], i), tsf)
            yo = _mm(y, p["odd_w_out"][i], "nn", F32, 512, 1024, "odd_out_fwd")
            saved.append((x, h, z, y, yo, uc))
        x = _post_fwd(x, yo, row1(p["post_norm_g"], layer), mod_l, tsf)

    loss, dx = _loss_fwd_bwd(x, target, tsf)

    g = {n: [None] * (DEPTH if n in ("pre_norm_g", "post_norm_g") else DEPTH // 2) for n in (
        "pre_norm_g", "post_norm_g", "even_w_in", "even_sc_conv_w", "even_sc_conv_b", "even_q_norm_g",
        "even_kv_norm_g", "even_wq", "even_wuk", "even_wuv", "even_w_out", "odd_w_in", "odd_conv_w", "odd_conv_b",
        "odd_ln_g", "odd_ln_b", "odd_w_out")}
    dmod = [None] * DEPTH
    for layer in reversed(range(DEPTH)):
        i = layer // 2
        mod_l = row1(mod, layer)
        x_in, h, z, y, yo, extra = saved[layer]
        dyo, dgate, g["post_norm_g"][layer] = _post_bwd(dx, yo, row1(p["post_norm_g"], layer), mod_l, tsb)
        if layer % 2 == 0:
            q, k, v, o, lse = extra
            dy = _mm(dyo, p["even_w_out"][i], "nt", F32, 512, 1024, "even_out_bwd_x")
            g["even_w_out"][i] = _mm(y, dyo, "tn", F32, 512, 512, "even_out_bwd_w")
            dz, do, g["even_sc_conv_w"][i], g["even_sc_conv_b"][i] = _even_gate_bwd(
                dy, z, o, p["even_sc_conv_w"][i], row1(p["even_sc_conv_b"], i), tsb)
            dq, dk, dv = _attn_bwd(q, k, v, do, o, lse, tq)
            dz, g["even_wq"][i], g["even_wuk"][i], g["even_wuv"][i], g["even_q_norm_g"][i], g["even_kv_norm_g"][i] = (
                _mla_prep_bwd(dz, dq, dk, dv, z, cos, sin, row1(p["even_q_norm_g"], i), row1(p["even_kv_norm_g"], i),
                              p["even_wq"][i], p["even_wuk"][i], p["even_wuv"][i], tsb))
            dh = _mm(dz, p["even_w_in"][i], "nt", F32, 256, 1024, "even_in_bwd_x")
            g["even_w_in"][i] = _mm(h, dz, "tn", F32, 512, 512, "even_in_bwd_w")
        else:
            uc = extra
            dy = _mm(dyo, p["odd_w_out"][i], "nt", F32, 512, 1024, "odd_out_bwd_x")
            g["odd_w_out"][i] = _mm(y, dyo, "tn", F32, 512, 512, "odd_out_bwd_w")
            dz, g["odd_conv_w"][i], g["odd_conv_b"][i], g["odd_ln_g"][i], g["odd_ln_b"][i] = _odd_bwd(
                dy, z, uc, p["odd_conv_w"][i], row1(p["odd_ln_g"], i), row1(p["odd_ln_b"], i), tsb)
            dh = _mm(dz, p["odd_w_in"][i], "nt", F32, 256, 1024, "odd_in_bwd_x")
            g["odd_w_in"][i] = _mm(h, dz, "tn", F32, 512, 512, "odd_in_bwd_w")
        dx, dshift, dscale, g["pre_norm_g"][layer] = _pre_bwd(dh, dx, x_in, row1(p["pre_norm_g"], layer), mod_l, tsb)
        dmod[layer] = jnp.concatenate([dshift, dscale, dgate], axis=-1)
    stack = lambda parts: jnp.stack([a[0] if a.shape[0] == 1 and a.ndim == 2 else a for a in parts])
    return loss, dx, jnp.concatenate(dmod, axis=0), {n: stack(parts) for n, parts in g.items()}


def _pad_even_w_in(w):
    zeros = lambda n: jnp.zeros(w.shape[:-1] + (n,), w.dtype)
    return jnp.concatenate([w[..., :2432], zeros(64), w[..., 2432:2464], zeros(32), w[..., 2464:]], axis=-1)


def _unpad_even_w_in(w):
    return jnp.concatenate([w[..., :2432], w[..., 2496:2528], w[..., 2560:]], axis=-1)


def _uq_to_heads(w):
    l = w.shape[0]
    w = w.reshape(l, Q_LORA, HEADS, QK_NOPE + QK_ROPE).transpose(0, 2, 1, 3)
    return jnp.pad(w, ((0, 0), (0, 0), (0, 0), (0, HEAD_PAD - QK_NOPE - QK_ROPE)))


def _uq_from_heads(w):
    l = w.shape[0]
    return w[..., :QK_NOPE + QK_ROPE].transpose(0, 2, 1, 3).reshape(l, Q_LORA, HEADS * (QK_NOPE + QK_ROPE))


def _ukv_to_heads(w):
    l = w.shape[0]
    w = w.reshape(l, KV_LORA, HEADS, QK_NOPE + V_HEAD).transpose(0, 2, 1, 3)
    wk = jnp.pad(w[..., :QK_NOPE], ((0, 0), (0, 0), (0, 0), (0, HEAD_PAD - QK_NOPE)))
    wv = w[..., QK_NOPE:]
    zero = jnp.zeros_like(wv)
    odd = (jnp.arange(HEADS) % 2 == 1)[None, :, None, None]
    wv = jnp.concatenate([jnp.where(odd, zero, wv), jnp.where(odd, wv, zero)], axis=-1)
    return wk, wv


def _ukv_from_heads(wk, wv):
    l = wk.shape[0]
    odd = (jnp.arange(HEADS) % 2 == 1)[None, :, None, None]
    v = jnp.where(odd, wv[..., V_HEAD:], wv[..., :V_HEAD])
    w = jnp.concatenate([wk[..., :QK_NOPE], v], axis=-1)
    return w.transpose(0, 2, 1, 3).reshape(l, KV_LORA, HEADS * (QK_NOPE + V_HEAD))


def _place():
    return lax.axis_index("x"), lax.axis_index("y"), lax.axis_index("c")


def _flip(v, bit):
    return 1 - v if bit else v


def _remote(src, dst, send_sem, recv_sem, peer):
    return pltpu.make_async_remote_copy(src_ref=src, dst_ref=dst, send_sem=send_sem, recv_sem=recv_sem,
                                        device_id=peer, device_id_type=MESH)


_VMEM_SPEC = pl.BlockSpec(memory_space=pltpu.VMEM)
_HBM_SPEC = pl.BlockSpec(memory_space=pl.ANY)


def _ada_fwd(c8, ada_w, ada_b_sh):
    depth, d, cols = ada_w.shape

    def body(c_ref, w_ref, b_ref, call_ref, mod_ref, s1, r1, s2, r2):
        x, y, c = _place()
        chip = 2 * x + y
        me = 2 * chip + c
        call_ref[me] = c_ref[...]
        sends = []
        for k in range(1, N_DEV):
            peer = (_flip(x, k & 4), _flip(y, k & 2), _flip(c, k & 1))
            cp = _remote(c_ref, call_ref.at[me], s1.at[k - 1], r1.at[k - 1], peer)
            cp.start()
            sends.append(cp)
        for k in range(1, N_DEV):
            src = 4 * _flip(x, k & 4) + 2 * _flip(y, k & 2) + _flip(c, k & 1)
            _remote(c_ref, call_ref.at[src], s1.at[k - 1], r1.at[k - 1], (x, y, c)).wait_recv()
        act = _silu(call_ref[...]).reshape(N_DEV * 8, d)
        for l in range(depth):
            mod_ref[chip, l] = _dot_nn(act, w_ref[l]) + b_ref[l:l + 1, :]
        for k in range(1, N_CHIPS):
            peer = (_flip(x, k & 2), _flip(y, k & 1), c)
            cp = _remote(mod_ref.at[chip], mod_ref.at[chip], s2.at[k - 1], r2.at[k - 1], peer)
            cp.start()
            sends.append(cp)
        for k in range(1, N_CHIPS):
            src = 2 * _flip(x, k & 2) + _flip(y, k & 1)
            _remote(mod_ref.at[src], mod_ref.at[src], s2.at[k - 1], r2.at[k - 1], (x, y, c)).wait_recv()
        for cp in sends:
            cp.wait_send()

    return pl.pallas_call(
        body, name="ada_fwd", in_specs=[_VMEM_SPEC] * 3, out_specs=[_VMEM_SPEC] * 2,
        out_shape=[_sds((N_DEV, 8, d)), _sds((N_CHIPS, depth, N_DEV * 8, cols))],
        scratch_shapes=[pltpu.SemaphoreType.DMA((N_DEV - 1,)), pltpu.SemaphoreType.DMA((N_DEV - 1,)),
                        pltpu.SemaphoreType.DMA((N_CHIPS - 1,)), pltpu.SemaphoreType.DMA((N_CHIPS - 1,))],
        compiler_params=pltpu.CompilerParams(vmem_limit_bytes=VMEM_LIMIT_V7X))(c8, ada_w, ada_b_sh)


def _ada_bwd(c_t, dmod_sh):
    depth, n, cols = dmod_sh.shape
    d = c_t.shape[0]
    tr = 256

    def body(c_ref, dm_ref, o_ref):
        act = _silu(c_ref[...])
        acc = act[:, 0:1] * dm_ref[0, 0:1, :]
        for e in range(1, n):
            acc = acc + act[:, e:e + 1] * dm_ref[0, e:e + 1, :]
        o_ref[0] = acc

    return pl.pallas_call(
        body, name="ada_bwd", grid=(depth, d // tr),
        in_specs=[pl.BlockSpec((tr, n), lambda l, i: (i, 0)), pl.BlockSpec((1, n, cols), lambda l, i: (l, 0, 0))],
        out_specs=pl.BlockSpec((1, tr, cols), lambda l, i: (l, i, 0)), out_shape=_sds((depth, d, cols)),
        compiler_params=_cp(2))(c_t, dmod_sh)


def _gather_chips(arrays):
    n = len(arrays)

    def body(*refs):
        ins, outs = refs[:n], refs[n:2 * n]
        send_sems, recv_sems, local_sems = refs[2 * n:]
        x, y, c = _place()
        chip = 2 * x + y
        local = [pltpu.make_async_copy(ins[a], outs[a].at[chip], local_sems.at[a]) for a in range(n)]
        for cp in local:
            cp.start()
        sends = []
        for a in range(n):
            for k in range(1, N_CHIPS):
                peer = (_flip(x, k & 2), _flip(y, k & 1), c)
                cp = _remote(ins[a], outs[a].at[chip], send_sems.at[a, k - 1], recv_sems.at[a, k - 1], peer)
                cp.start()
                sends.append(cp)
        for a in range(n):
            for k in range(1, N_CHIPS):
                src = 2 * _flip(x, k & 2) + _flip(y, k & 1)
                _remote(ins[a], outs[a].at[src], send_sems.at[a, k - 1], recv_sems.at[a, k - 1], (x, y, c)).wait_recv()
        for cp in sends:
            cp.wait_send()
        for cp in local:
            cp.wait()

    return pl.pallas_call(
        body, name="gather_chips", in_specs=[_HBM_SPEC] * n, out_specs=[_HBM_SPEC] * n,
        out_shape=[_sds((N_CHIPS,) + a.shape, a.dtype) for a in arrays],
        scratch_shapes=[pltpu.SemaphoreType.DMA((n, N_CHIPS - 1)), pltpu.SemaphoreType.DMA((n, N_CHIPS - 1)),
                        pltpu.SemaphoreType.DMA((n,))])(*arrays)


def _gather_sum_all(small):
    r, w = small.shape

    def body(in_ref, all_ref, sum_ref, send_sems, recv_sems):
        x, y, c = _place()
        me = 4 * x + 2 * y + c
        all_ref[me] = in_ref[...]
        sends = []
        for k in range(1, N_DEV):
            peer = (_flip(x, k & 4), _flip(y, k & 2), _flip(c, k & 1))
            cp = _remote(in_ref, all_ref.at[me], send_sems.at[k - 1], recv_sems.at[k - 1], peer)
            cp.start()
            sends.append(cp)
        for k in range(1, N_DEV):
            src = 4 * _flip(x, k & 4) + 2 * _flip(y, k & 2) + _flip(c, k & 1)
            _remote(in_ref, all_ref.at[src], send_sems.at[k - 1], recv_sems.at[k - 1], (x, y, c)).wait_recv()
        acc = all_ref[0]
        for e in range(1, N_DEV):
            acc = acc + all_ref[e]
        sum_ref[...] = acc
        for cp in sends:
            cp.wait_send()

    return pl.pallas_call(
        body, name="gather_sum_all", in_specs=[_VMEM_SPEC], out_specs=[_VMEM_SPEC] * 2,
        out_shape=[_sds((N_DEV, r, w)), _sds((r, w))],
        scratch_shapes=[pltpu.SemaphoreType.DMA((N_DEV - 1,)), pltpu.SemaphoreType.DMA((N_DEV - 1,))],
        compiler_params=pltpu.CompilerParams(vmem_limit_bytes=VMEM_LIMIT_V7X))(small)


def _rs_sibling(g2):
    def body(g_ref, t_ref, send_sem, recv_sem):
        x, y, c = _place()
        cp = _remote(g_ref.at[1 - c], t_ref, send_sem, recv_sem, (x, y, 1 - c))
        cp.start()
        cp.wait()

    return pl.pallas_call(
        body, name="rs_sibling", in_specs=[_HBM_SPEC], out_specs=_HBM_SPEC, out_shape=_sds(g2.shape[1:]),
        scratch_shapes=[pltpu.SemaphoreType.DMA, pltpu.SemaphoreType.DMA])(g2)


def _rs_chips(p):
    def body(p_ref, t_ref, send_sems, recv_sems):
        x, y, c = _place()
        chip = 2 * x + y
        sends = []
        for k in range(1, N_CHIPS):
            tx, ty = _flip(x, k & 2), _flip(y, k & 1)
            cp = _remote(p_ref.at[2 * tx + ty], t_ref.at[k - 1], send_sems.at[k - 1], recv_sems.at[k - 1], (tx, ty, c))
            cp.start()
            sends.append(cp)
        for k in range(1, N_CHIPS):
            _remote(p_ref.at[chip], t_ref.at[k - 1], send_sems.at[k - 1], recv_sems.at[k - 1], (x, y, c)).wait_recv()
        for cp in sends:
            cp.wait_send()

    return pl.pallas_call(
        body, name="rs_chips", in_specs=[_HBM_SPEC], out_specs=_HBM_SPEC,
        out_shape=_sds((N_CHIPS - 1,) + p.shape[1:]),
        scratch_shapes=[pltpu.SemaphoreType.DMA((N_CHIPS - 1,)), pltpu.SemaphoreType.DMA((N_CHIPS - 1,))])(p)


def _rs_final(q):
    def body(q_ref, f_ref, send_sem, recv_sem, local_sem):
        x, y, c = _place()
        mine = pltpu.make_async_copy(q_ref, f_ref.at[c], local_sem)
        mine.start()
        cp = _remote(q_ref, f_ref.at[c], send_sem, recv_sem, (x, y, 1 - c))
        cp.start()
        _remote(q_ref, f_ref.at[1 - c], send_sem, recv_sem, (x, y, c)).wait_recv()
        cp.wait_send()
        mine.wait()

    return pl.pallas_call(
        body, name="rs_final", in_specs=[_HBM_SPEC], out_specs=_HBM_SPEC, out_shape=_sds((2,) + q.shape),
        scratch_shapes=[pltpu.SemaphoreType.DMA, pltpu.SemaphoreType.DMA, pltpu.SemaphoreType.DMA])(q)


RS_BLOCK_COLS = 2048


def _add_sibling_half(g2, t1, c_idx):
    _, n, r, cc = g2.shape
    blk = pl.BlockSpec((1, r, RS_BLOCK_COLS), lambda j, i, c: (j, 0, i))

    def body(c_ref, g_ref, t_ref, o_ref):
        del c_ref
        o_ref[...] = g_ref[0] + t_ref[...]

    return pl.pallas_call(
        body, name="add_sibling_half", out_shape=_sds((n, r, cc)),
        grid_spec=pltpu.PrefetchScalarGridSpec(
            num_scalar_prefetch=1, grid=(n, cc // RS_BLOCK_COLS),
            in_specs=[pl.BlockSpec((1, 1, r, RS_BLOCK_COLS), lambda j, i, c: (c[0], j, 0, i)), blk], out_specs=blk),
        compiler_params=_cp(2))(c_idx, g2, t1)


def _add_chips(p, t2, chip_idx):
    _, r, cc = p.shape

    def body(c_ref, p_ref, t_ref, o_ref):
        del c_ref
        o_ref[...] = p_ref[0] + t_ref[0] + t_ref[1] + t_ref[2]

    return pl.pallas_call(
        body, name="add_chips", out_shape=_sds((r, cc)),
        grid_spec=pltpu.PrefetchScalarGridSpec(
            num_scalar_prefetch=1, grid=(cc // RS_BLOCK_COLS,),
            in_specs=[pl.BlockSpec((1, r, RS_BLOCK_COLS), lambda i, c: (c[0], 0, i)),
                      pl.BlockSpec((N_CHIPS - 1, r, RS_BLOCK_COLS), lambda i, c: (0, 0, i))],
            out_specs=pl.BlockSpec((r, RS_BLOCK_COLS), lambda i, c: (0, i))),
        compiler_params=_cp(1))(chip_idx, p, t2)


def _adamw(w, g, m, v, name):
    shape = w.shape
    cols = shape[-1]
    rows = _size(shape[:-1])
    tr = 512 if rows % 512 == 0 else rows
    spec = pl.BlockSpec((tr, cols), lambda i: (i, 0))

    def body(w_ref, g_ref, m_ref, v_ref, d_ref, nm_ref, nv_ref):
        gv = g_ref[...]
        mv = ADAM_B1 * m_ref[...] + (1.0 - ADAM_B1) * gv
        vv = ADAM_B2 * v_ref[...] + (1.0 - ADAM_B2) * jnp.square(gv)
        m_hat = mv / (1.0 - ADAM_B1 ** ADAM_STEP)
        v_hat = vv / (1.0 - ADAM_B2 ** ADAM_STEP)
        d_ref[...] = -ADAM_LR * (m_hat / (jnp.sqrt(v_hat) + ADAM_EPS) + ADAM_WD * w_ref[...])
        nm_ref[...] = mv
        nv_ref[...] = vv

    outs = pl.pallas_call(
        body, name="adamw_" + name, grid=(rows // tr,), in_specs=[spec] * 4, out_specs=[spec] * 3,
        out_shape=[_sds((rows, cols))] * 3, compiler_params=_cp(1))(
            *[a.reshape(rows, cols) for a in (w, g, m, v)])
    return tuple(o.reshape(shape) for o in outs)


def _size(shape):
    n = 1
    for s in shape:
        n *= s
    return n


def _col_shards(a):
    l, r, cc = a.shape
    return a.reshape(l, r, N_CHIPS, cc // N_CHIPS).transpose(2, 0, 1, 3).reshape(N_CHIPS, -1)


def _row_shards(a):
    l, r, cc = a.shape
    return a.reshape(l, N_CHIPS, r // N_CHIPS, cc).transpose(1, 0, 2, 3).reshape(N_CHIPS, -1)


_BIG = (("even_w_in", (2, 1024, 744), _col_shards), ("even_w_uq", (2, 256, 192), _col_shards),
        ("even_w_ukv", (2, 128, 256), _col_shards), ("even_w_out", (2, 256, 1024), _row_shards),
        ("odd_w_in", (2, 1024, 768), _col_shards), ("odd_w_out", (2, 256, 1024), _row_shards))
assert sum(_size(shape) for _, shape, _ in _BIG) == SHARD_ELEMS


def _pack_big(g):
    flat = jnp.concatenate([shard(g[n]) for n, _, shard in _BIG] + [jnp.zeros((N_CHIPS, RS_PAD), F32)], axis=1)
    return flat.reshape(N_CHIPS, 2, RS_ROWS, RS_COLS).transpose(1, 0, 2, 3)


def _unpack_big(f):
    flat = f.reshape(-1)
    out, at = {}, 0
    for n, shape, _ in _BIG:
        out[n] = flat[at:at + _size(shape)].reshape(shape)
        at += _size(shape)
    return out


_SMALL = (("dmod", (DEPTH, 3 * D_MODEL)), ("pre_norm_g", (DEPTH, D_MODEL)), ("post_norm_g", (DEPTH, D_MODEL)),
          ("even_sc_conv_w", (2, SC_KERNEL, SC_WIDTH)), ("even_sc_conv_b", (2, SC_WIDTH)),
          ("even_q_norm_g", (2, Q_LORA)), ("even_kv_norm_g", (2, KV_LORA)),
          ("odd_conv_w", (2, CONF_KERNEL, D_MODEL)), ("odd_conv_b", (2, D_MODEL)), ("odd_ln_g", (2, D_MODEL)),
          ("odd_ln_b", (2, D_MODEL)))
SMALL_ROWS = -(-sum(_size(s) for _, s in _SMALL) // (8 * 128)) * 8

_SMALL_W = (("even_sc_conv_w", (2, SC_KERNEL, SC_WIDTH // N_CHIPS)), ("odd_conv_w", (2, CONF_KERNEL, D_MODEL // N_CHIPS)),
            ("odd_conv_b", (2, D_MODEL // N_CHIPS)), ("odd_ln_g", (2, D_MODEL // N_CHIPS)),
            ("odd_ln_b", (2, D_MODEL // N_CHIPS)))
SMALL_W_ROWS = -(-sum(_size(s) for _, s in _SMALL_W) // (8 * 128)) * 8


def _pack_rows(arrays, layout, rows):
    flat = jnp.concatenate([arrays[n].reshape(-1) for n, _ in layout])
    return jnp.pad(flat, (0, rows * 128 - flat.shape[0])).reshape(rows, 128)


def _unpack_small(t):
    flat = t.reshape(-1)
    out, at = {}, 0
    for n, shape in _SMALL:
        out[n] = flat[at:at + _size(shape)].reshape(shape)
        at += _size(shape)
    return out


def _unpack_small_w(t):
    flat = t.reshape(N_CHIPS, -1)
    out, at = {}, 0
    for n, shape in _SMALL_W:
        a = flat[:, at:at + _size(shape)].reshape((N_CHIPS,) + shape)
        out[n] = jnp.moveaxis(a, 0, -2).reshape(shape[:-1] + (N_CHIPS * shape[-1],))
        at += _size(shape)
    return out


def _chip_cols(a, chip):
    n = a.shape[-1] // N_CHIPS
    return lax.dynamic_slice_in_dim(a, chip * n, n, axis=a.ndim - 1)


def _join_cols(a):
    _, l, r, cc = a.shape
    return a.transpose(1, 2, 0, 3).reshape(l, r, N_CHIPS * cc)


def _join_rows(a):
    _, l, r, cc = a.shape
    return a.transpose(1, 0, 2, 3).reshape(l, N_CHIPS * r, cc)


WEIGHT_NAMES = ("ada_w", "ada_b", "pre_norm_g", "post_norm_g", "even_w_in", "even_sc_conv_w", "even_sc_conv_b",
                "even_q_norm_g", "even_kv_norm_g", "even_w_uq", "even_w_ukv", "even_w_out", "odd_w_in", "odd_conv_w",
                "odd_conv_b", "odd_ln_g", "odd_ln_b", "odd_w_out")
MATMUL_WEIGHTS = tuple(n for n, _, _ in _BIG)


def kernel(x, c, positions, ada_w, ada_b, pre_norm_g, post_norm_g, even_w_in, even_sc_conv_w, even_sc_conv_b, even_q_norm_g, even_kv_norm_g, even_w_uq, even_w_ukv, even_w_out, odd_w_in, odd_conv_w, odd_conv_b, odd_ln_g, odd_ln_b, odd_w_out, loss_target, m_ada_w, m_ada_b, m_pre_norm_g, m_post_norm_g, m_even_w_in, m_even_sc_conv_w, m_even_sc_conv_b, m_even_q_norm_g, m_even_kv_norm_g, m_even_w_uq, m_even_w_ukv, m_even_w_out, m_odd_w_in, m_odd_conv_w, m_odd_conv_b, m_odd_ln_g, m_odd_ln_b, m_odd_w_out, v_ada_w, v_ada_b, v_pre_norm_g, v_post_norm_g, v_even_w_in, v_even_sc_conv_w, v_even_sc_conv_b, v_even_q_norm_g, v_even_kv_norm_g, v_even_w_uq, v_even_w_ukv, v_even_w_out, v_odd_w_in, v_odd_conv_w, v_odd_conv_b, v_odd_ln_g, v_odd_ln_b, v_odd_w_out):
    w = dict(zip(WEIGHT_NAMES, (ada_w, ada_b, pre_norm_g, post_norm_g, even_w_in, even_sc_conv_w, even_sc_conv_b,
                                even_q_norm_g, even_kv_norm_g, even_w_uq, even_w_ukv, even_w_out, odd_w_in, odd_conv_w,
                                odd_conv_b, odd_ln_g, odd_ln_b, odd_w_out)))
    m = dict(zip(WEIGHT_NAMES, (m_ada_w, m_ada_b, m_pre_norm_g, m_post_norm_g, m_even_w_in, m_even_sc_conv_w,
                                m_even_sc_conv_b, m_even_q_norm_g, m_even_kv_norm_g, m_even_w_uq, m_even_w_ukv,
                                m_even_w_out, m_odd_w_in, m_odd_conv_w, m_odd_conv_b, m_odd_ln_g, m_odd_ln_b, m_odd_w_out)))
    v = dict(zip(WEIGHT_NAMES, (v_ada_w, v_ada_b, v_pre_norm_g, v_post_norm_g, v_even_w_in, v_even_sc_conv_w,
                                v_even_sc_conv_b, v_even_q_norm_g, v_even_kv_norm_g, v_even_w_uq, v_even_w_ukv,
                                v_even_w_out, v_odd_w_in, v_odd_conv_w, v_odd_conv_b, v_odd_ln_g, v_odd_ln_b, v_odd_w_out)))
    ix, iy, ic = _place()
    chip = 2 * ix + iy
    me = 2 * chip + ic
    s = x.shape[1]

    c_all, mod_all = _ada_fwd(jnp.broadcast_to(c, (8, D_MODEL)), ada_w, _chip_cols(ada_b, chip))
    mod = lax.dynamic_index_in_dim(mod_all, 8 * me, axis=2, keepdims=False)
    mod = mod.transpose(1, 0, 2).reshape(DEPTH, 3 * D_MODEL)

    gathered = _gather_chips([w[n].astype(MXU_DTYPE) for n in MATMUL_WEIGHTS] + [_pack_rows(w, _SMALL_W, SMALL_W_ROWS)])
    full = dict(zip(MATMUL_WEIGHTS, gathered[:-1]))
    small_w = _unpack_small_w(gathered[-1])
    wuk, wuv = _ukv_to_heads(_join_cols(full["even_w_ukv"]))
    p = {
        "pre_norm_g": pre_norm_g, "post_norm_g": post_norm_g,
        "even_w_in": _pad_even_w_in(_join_cols(full["even_w_in"])),
        "even_sc_conv_w": small_w["even_sc_conv_w"], "even_sc_conv_b": even_sc_conv_b,
        "even_q_norm_g": even_q_norm_g, "even_kv_norm_g": even_kv_norm_g,
        "even_wq": _uq_to_heads(_join_cols(full["even_w_uq"])), "even_wuk": wuk, "even_wuv": wuv,
        "even_w_out": _join_rows(full["even_w_out"]),
        "odd_w_in": _join_cols(full["odd_w_in"]), "odd_conv_w": small_w["odd_conv_w"],
        "odd_conv_b": small_w["odd_conv_b"], "odd_ln_g": small_w["odd_ln_g"], "odd_ln_b": small_w["odd_ln_b"],
        "odd_w_out": _join_rows(full["odd_w_out"]),
    }

    inv_freq = 1.0 / (ROPE_THETA ** (jnp.arange(0, QK_ROPE, 2, dtype=F32) / QK_ROPE))
    inv_freq = jnp.zeros((1, HEAD_PAD), F32).at[0, QK_NOPE:QK_NOPE + QK_ROPE].set(jnp.tile(inv_freq, 2))
    cos, sin = _rope_tables(positions.reshape(s, 1), inv_freq)

    loss, dx, dmod, g = _local_step(x[0], loss_target[0], cos, sin, mod, p)
    g["even_w_in"] = _unpad_even_w_in(g["even_w_in"])
    g["even_w_uq"] = _uq_from_heads(g["even_wq"])
    g["even_w_ukv"] = _ukv_from_heads(g["even_wuk"], g["even_wuv"])
    g["dmod"] = dmod

    g2 = _pack_big(g)
    half = _add_sibling_half(g2, _rs_sibling(g2), ic.reshape(1))
    mine = _add_chips(half, _rs_chips(half), chip.reshape(1))
    grads = _unpack_big(_rs_final(mine))

    small_all, small_sum = _gather_sum_all(_pack_rows(g, _SMALL, SMALL_ROWS))
    tot = _unpack_small(small_sum)
    dmod_all = small_all[:, :DEPTH * 3 * D_MODEL // 128].reshape(N_DEV, DEPTH, 3 * D_MODEL)
    grads["ada_w"] = _ada_bwd(c_all[:, 0, :].T, _chip_cols(dmod_all, chip).transpose(1, 0, 2))
    grads["ada_b"] = tot["dmod"]
    for n in ("pre_norm_g", "post_norm_g", "even_sc_conv_b", "even_q_norm_g", "even_kv_norm_g"):
        grads[n] = tot[n]
    for n in ("even_sc_conv_w", "odd_conv_w", "odd_conv_b", "odd_ln_g", "odd_ln_b"):
        grads[n] = _chip_cols(tot[n], chip)

    deltas, new_m, new_v = {}, {}, {}
    for n in WEIGHT_NAMES:
        deltas[n], new_m[n], new_v[n] = _adamw(w[n], grads[n], m[n], v[n], n)

    total_loss = lax.psum(loss[0, 0], ("x", "y", "c"))
    return (total_loss, dx[None], *[grads[n] for n in WEIGHT_NAMES], *[deltas[n] for n in WEIGHT_NAMES],
            *[new_m[n] for n in WEIGHT_NAMES], *[new_v[n] for n in WEIGHT_NAMES])
```

```python
import functools

import jax
import jax.numpy as jnp
from jax import lax
from jax.experimental import pallas as pl
from jax.experimental.pallas import tpu as pltpu

F32 = jnp.float32
MXU_DTYPE = jnp.bfloat16
MESH = pl.DeviceIdType.MESH
VMEM_LIMIT_V7X = 56 * 2 ** 20

EPS = 1e-6
D_MODEL = 1024
DEPTH = 4
CHUNK = 64
SC_WIDTH = 512
SC_KERNEL = 3
SC_HALO = 8
HEADS = 8
QK_NOPE = 64
QK_ROPE = 32
V_HEAD = 64
HEAD_PAD = 128
Q_LORA = 256
KV_LORA = 128
ROPE_THETA = 10000.0
CONF_KERNEL = 31
CONF_HALO = 32
CONV_ROWS = 32
EVEN_IN = 2976
EVEN_PAD = 3072
ODD_IN = 3072
N_CHIPS = 4
N_DEV = 8
NEG = -1e30

ADAM_LR = 0.001
ADAM_B1 = 0.9
ADAM_B2 = 0.999
ADAM_EPS = 1e-08
ADAM_WD = 0.01
ADAM_STEP = 10

N_PAIRS = DEPTH // 2
EVEN_SHARD = EVEN_IN // N_CHIPS
EVEN_SHARD_PAD = 768
MLA_ROWS = Q_LORA + 2 * KV_LORA


def _cp(n_grid=0, **kw):
    return pltpu.CompilerParams(dimension_semantics=("arbitrary",) * n_grid,
                                vmem_limit_bytes=VMEM_LIMIT_V7X, **kw)


def _sigmoid(x):
    return 1.0 / (1.0 + jnp.exp(-x))


def _silu(x):
    return x * _sigmoid(x)


def _dsilu(x):
    s = _sigmoid(x)
    return s * (1.0 + x * (1.0 - s))


def _rms(x, g):
    return x * lax.rsqrt(jnp.mean(x * x, axis=-1, keepdims=True) + EPS) * g


def _dot(a, b, dims):
    return lax.dot_general(a.astype(MXU_DTYPE), b.astype(MXU_DTYPE), (dims, ((), ())),
                           preferred_element_type=F32)


def _dot_nn(a, b):
    return _dot(a, b, ((1,), (0,)))


def _dot_nt(a, b):
    return _dot(a, b, ((1,), (1,)))


def _dot_tn(a, b):
    return _dot(a, b, ((0,), (0,)))


def _rows(ts, w, cb=0):
    return pl.BlockSpec((ts, w), lambda i: (i, cb))


def _vec(w, cb=0, r=1):
    return pl.BlockSpec((r, w), lambda i: (0, cb))


def _prev_halo(ts, hr, w, cb):
    return pl.BlockSpec((hr, w), lambda i: (jnp.maximum(i * (ts // hr) - 1, 0), cb))


def _next_halo(ts, hr, w, cb, s):
    return pl.BlockSpec((hr, w), lambda i: (jnp.minimum((i + 1) * (ts // hr), s // hr - 1), cb))


def _sds(shape, dtype=F32):
    return jax.ShapeDtypeStruct(shape, dtype)


def _mm(a, b, mode, out_dtype, tm, tn, name):
    tm = min(tm, a.shape[1] if mode == "tn" else a.shape[0])
    tn = min(tn, b.shape[0] if mode == "nt" else b.shape[1])
    if mode == "nn":
        (m, k), n = a.shape, b.shape[1]
        a_spec = pl.BlockSpec((tm, k), lambda i, j: (i, 0))
        b_spec = pl.BlockSpec((k, tn), lambda i, j: (0, j))
        dot = _dot_nn
    elif mode == "nt":
        (m, k), n = a.shape, b.shape[0]
        a_spec = pl.BlockSpec((tm, k), lambda i, j: (i, 0))
        b_spec = pl.BlockSpec((tn, k), lambda i, j: (j, 0))
        dot = _dot_nt
    else:
        (k, m), n = a.shape, b.shape[1]
        a_spec = pl.BlockSpec((k, tm), lambda i, j: (0, i))
        b_spec = pl.BlockSpec((k, tn), lambda i, j: (0, j))
        dot = _dot_tn
    assert m % tm == 0 and n % tn == 0, (name, m, n, tm, tn)

    def body(a_ref, b_ref, o_ref):
        o_ref[...] = dot(a_ref[...], b_ref[...]).astype(o_ref.dtype)

    return pl.pallas_call(
        body, name=name, grid=(m // tm, n // tn), in_specs=[a_spec, b_spec],
        out_specs=pl.BlockSpec((tm, tn), lambda i, j: (i, j)), out_shape=_sds((m, n), out_dtype),
        compiler_params=_cp(2))(a, b)


def _mm_tn_shards(a, b, buf, layer, by, name):
    k, m = a.shape
    n = b.shape[1]
    if by == "cols":
        tm, tn = 512, n // N_CHIPS
        shape, grid = (N_CHIPS, N_PAIRS, m, tn), (m // tm, N_CHIPS)
        out_spec = pl.BlockSpec((1, 1, tm, tn), lambda i, j: (j, layer, i, 0))
    else:
        tm, tn = m // N_CHIPS, 512
        shape, grid = (N_CHIPS, N_PAIRS, tm, n), (N_CHIPS, n // tn)
        out_spec = pl.BlockSpec((1, 1, tm, tn), lambda i, j: (i, layer, 0, j))

    def body(a_ref, b_ref, *rest):
        rest[-1][0, 0] = _dot_tn(a_ref[...], b_ref[...])

    in_specs = [pl.BlockSpec((k, tm), lambda i, j: (0, i)), pl.BlockSpec((k, tn), lambda i, j: (0, j))]
    args = (a, b) if buf is None else (a, b, buf)
    return pl.pallas_call(
        body, name=name, grid=grid, in_specs=in_specs + [_HBM_SPEC] * (buf is not None), out_specs=out_spec,
        out_shape=_sds(shape), input_output_aliases={} if buf is None else {2: 0}, compiler_params=_cp(2))(*args)


def _even_col(q):
    return q if q < 2432 else (q + 64 if q < 2464 else q + 96)


def _shard_pieces(j):
    lo, hi = EVEN_SHARD * j, EVEN_SHARD * (j + 1)
    cuts = [lo] + [b for b in (2432, 2464) if lo < b < hi] + [hi]
    return [(a - lo, _even_col(a), b - a) for a, b in zip(cuts[:-1], cuts[1:])]


def _ein_from_shards(w):
    pairs, _, d, _ = w.shape
    tr = 256

    def body(w_ref, o_ref):
        parts, at = [], 0
        for j in range(N_CHIPS):
            for d0, s0, n in _shard_pieces(j):
                if s0 > at:
                    parts.append(jnp.zeros((tr, s0 - at), F32))
                parts.append(w_ref[0, j, :, d0:d0 + n].astype(F32))
                at = s0 + n
        o_ref[0] = jnp.concatenate(parts, axis=1).astype(o_ref.dtype)

    return pl.pallas_call(
        body, name="ein_from_shards", grid=(pairs, d // tr),
        in_specs=[pl.BlockSpec((1, N_CHIPS, tr, EVEN_SHARD), lambda l, i: (l, 0, i, 0))],
        out_specs=pl.BlockSpec((1, tr, EVEN_PAD), lambda l, i: (l, i, 0)),
        out_shape=_sds((pairs, d, EVEN_PAD), w.dtype), compiler_params=_cp(2))(w)


def _ein_to_shards(dw, buf, layer):
    d = dw.shape[0]
    tr = 256

    def body(dw_ref, *rest):
        for j in range(N_CHIPS):
            parts = [dw_ref[:, s0:s0 + n] for _, s0, n in _shard_pieces(j)]
            rest[-1][j, 0] = jnp.concatenate(parts + [jnp.zeros((tr, EVEN_SHARD_PAD - EVEN_SHARD), F32)], axis=1)

    args = (dw,) if buf is None else (dw, buf)
    return pl.pallas_call(
        body, name="ein_to_shards", grid=(d // tr,),
        in_specs=[_rows(tr, EVEN_PAD)] + [_HBM_SPEC] * (buf is not None),
        out_specs=pl.BlockSpec((N_CHIPS, 1, tr, EVEN_SHARD_PAD), lambda i: (0, layer, i, 0)),
        out_shape=_sds((N_CHIPS, N_PAIRS, d, EVEN_SHARD_PAD)),
        input_output_aliases={} if buf is None else {1: 0}, compiler_params=_cp(1))(*args)


def _rope_tables(pos_col, invf):
    s = pos_col.shape[0]
    ts = min(512, s)

    def body(p_ref, f_ref, c_ref, s_ref):
        ang = p_ref[...].astype(F32) * f_ref[...]
        lane = lax.broadcasted_iota(jnp.int32, ang.shape, 1)
        rope = (lane >= QK_NOPE) & (lane < QK_NOPE + QK_ROPE)
        c_ref[...] = jnp.where(lane < QK_NOPE, 1.0, jnp.where(rope, jnp.cos(ang), 0.0))
        s_ref[...] = jnp.where(rope, jnp.sin(ang), 0.0)

    return pl.pallas_call(
        body, name="rope_tables", grid=(s // ts,), in_specs=[_rows(ts, 1), _vec(HEAD_PAD)],
        out_specs=[_rows(ts, HEAD_PAD)] * 2, out_shape=[_sds((s, HEAD_PAD))] * 2,
        compiler_params=_cp(1))(pos_col, invf)


def _pre_fwd(x, g, mod_l, ts):
    s, d = x.shape

    def body(x_ref, g_ref, sh_ref, sc_ref, h_ref):
        h = _rms(x_ref[...], g_ref[...]) * (1.0 + sc_ref[...]) + sh_ref[...]
        h_ref[...] = h.astype(h_ref.dtype)

    return pl.pallas_call(
        body, name="pre_fwd", grid=(s // ts,),
        in_specs=[_rows(ts, d), _vec(d), _vec(d, 0), _vec(d, 1)],
        out_specs=_rows(ts, d), out_shape=_sds((s, d), MXU_DTYPE), compiler_params=_cp(1))(x, g, mod_l, mod_l)


def _pre_bwd(dh, dx_out, x, g, mod_l, ts):
    s, d = x.shape

    def f(xv, gv, sh, sc):
        return _rms(xv, gv) * (1.0 + sc) + sh

    def body(dh_ref, dxo_ref, x_ref, g_ref, sh_ref, sc_ref, dx_ref, dsh_ref, dsc_ref, dg_ref):
        i = pl.program_id(0)
        _, vjp = jax.vjp(f, x_ref[...], g_ref[...], sh_ref[...], sc_ref[...])
        dx, dg, dsh, dsc = vjp(dh_ref[...])
        dx_ref[...] = dxo_ref[...] + dx

        @pl.when(i == 0)
        def _():
            dsh_ref[...] = jnp.zeros_like(dsh_ref)
            dsc_ref[...] = jnp.zeros_like(dsc_ref)
            dg_ref[...] = jnp.zeros_like(dg_ref)

        dsh_ref[...] += dsh
        dsc_ref[...] += dsc
        dg_ref[...] += dg

    return pl.pallas_call(
        body, name="pre_bwd", grid=(s // ts,),
        in_specs=[_rows(ts, d), _rows(ts, d), _rows(ts, d), _vec(d), _vec(d, 0), _vec(d, 1)],
        out_specs=[_rows(ts, d), _vec(d), _vec(d), _vec(d)],
        out_shape=[_sds((s, d)), _sds((1, d)), _sds((1, d)), _sds((1, d))],
        compiler_params=_cp(1))(dh, dx_out, x, g, mod_l, mod_l)


def _post_fwd(x, yo, g, mod_l, ts):
    s, d = x.shape

    def body(x_ref, yo_ref, g_ref, gate_ref, o_ref):
        o_ref[...] = x_ref[...] + gate_ref[...] * _rms(yo_ref[...], g_ref[...])

    return pl.pallas_call(
        body, name="post_fwd", grid=(s // ts,),
        in_specs=[_rows(ts, d), _rows(ts, d), _vec(d), _vec(d, 2)],
        out_specs=_rows(ts, d), out_shape=_sds((s, d)), compiler_params=_cp(1))(x, yo, g, mod_l)


def _post_bwd(dx_out, yo, g, mod_l, ts):
    s, d = yo.shape

    def f(yov, gv, gate):
        return gate * _rms(yov, gv)

    def body(dx_ref, yo_ref, g_ref, gate_ref, dyo_ref, dgate_ref, dg_ref):
        i = pl.program_id(0)
        _, vjp = jax.vjp(f, yo_ref[...], g_ref[...], gate_ref[...])
        dyo, dg, dgate = vjp(dx_ref[...])
        dyo_ref[...] = dyo.astype(dyo_ref.dtype)

        @pl.when(i == 0)
        def _():
            dgate_ref[...] = jnp.zeros_like(dgate_ref)
            dg_ref[...] = jnp.zeros_like(dg_ref)

        dgate_ref[...] += dgate
        dg_ref[...] += dg

    return pl.pallas_call(
        body, name="post_bwd", grid=(s // ts,),
        in_specs=[_rows(ts, d), _rows(ts, d), _vec(d), _vec(d, 2)],
        out_specs=[_rows(ts, d), _vec(d), _vec(d)],
        out_shape=[_sds((s, d), MXU_DTYPE), _sds((1, d)), _sds((1, d))],
        compiler_params=_cp(1))(dx_out, yo, g, mod_l)


def _loss_fwd_bwd(x, target, ts):
    s, d = x.shape

    def body(x_ref, t_ref, loss_ref, dx_ref):
        i = pl.program_id(0)
        err = x_ref[...] - t_ref[...]
        dx_ref[...] = err * (1.0 / d)

        @pl.when(i == 0)
        def _():
            loss_ref[...] = jnp.zeros_like(loss_ref)

        loss_ref[...] += 0.5 * jnp.sum(jnp.sum(err * err, axis=-1, keepdims=True) * (1.0 / d), axis=0, keepdims=True)

    return pl.pallas_call(
        body, name="loss", grid=(s // ts,), in_specs=[_rows(ts, d), _rows(ts, d)],
        out_specs=[_vec(1), _rows(ts, d)], out_shape=[_sds((1, 1)), _sds((s, d))],
        compiler_params=_cp(1))(x, target)


def _rope(t, cos, sin):
    lane = lax.broadcasted_iota(jnp.int32, t.shape, 1)
    first = (lane >= QK_NOPE) & (lane < QK_NOPE + QK_ROPE // 2)
    second = (lane >= QK_NOPE + QK_ROPE // 2) & (lane < QK_NOPE + QK_ROPE)
    up = pltpu.roll(t, QK_ROPE // 2, 1)
    down = pltpu.roll(t, HEAD_PAD - QK_ROPE // 2, 1)
    return t * cos + jnp.where(first, -down, jnp.where(second, up, 0.0)) * sin


def _rope_transposed(g, cos, sin):
    lane = lax.broadcasted_iota(jnp.int32, g.shape, 1)
    first = (lane >= QK_NOPE) & (lane < QK_NOPE + QK_ROPE // 2)
    second = (lane >= QK_NOPE + QK_ROPE // 2) & (lane < QK_NOPE + QK_ROPE)
    u = g * sin
    up = pltpu.roll(u, QK_ROPE // 2, 1)
    down = pltpu.roll(u, HEAD_PAD - QK_ROPE // 2, 1)
    return g * cos + jnp.where(first, down, jnp.where(second, -up, 0.0))


def _mla_prep_fwd(z, cos, sin, qg, kvg, wq, wuk, wuv, ts):
    s = z.shape[0]

    def body(cq_ref, ckv_ref, kr_ref, cos_ref, sin_ref, qg_ref, kvg_ref, wq_ref, wuk_ref, wuv_ref,
             q_ref, k_ref, v_ref):
        cos_v, sin_v = cos_ref[...], sin_ref[...]
        cqn = _rms(cq_ref[...], qg_ref[...])
        q_ref[0] = _rope(_dot_nn(cqn, wq_ref[0]), cos_v, sin_v).astype(q_ref.dtype)
        ckvn = _rms(ckv_ref[...], kvg_ref[...])
        k_ref[0] = (_dot_nn(ckvn, wuk_ref[0]) + _rope(kr_ref[...], cos_v, sin_v)).astype(k_ref.dtype)
        v_ref[0] = _dot_nn(ckvn, wuv_ref[0]).astype(v_ref.dtype)

    row = lambda w, cb: pl.BlockSpec((ts, w), lambda i, h: (i, cb))
    vec = lambda w: pl.BlockSpec((1, w), lambda i, h: (0, 0))
    wsp = lambda k: pl.BlockSpec((1, k, HEAD_PAD), lambda i, h: (h, 0, 0))
    out = pl.BlockSpec((1, ts, HEAD_PAD), lambda i, h: (h, i, 0))
    return pl.pallas_call(
        body, name="mla_prep_fwd", grid=(s // ts, HEADS),
        in_specs=[row(Q_LORA, 8), row(KV_LORA, 18), row(HEAD_PAD, 19), row(HEAD_PAD, 0), row(HEAD_PAD, 0),
                  vec(Q_LORA), vec(KV_LORA), wsp(Q_LORA), wsp(KV_LORA), wsp(KV_LORA)],
        out_specs=[out] * 3, out_shape=[_sds((HEADS, s, HEAD_PAD), MXU_DTYPE)] * 3,
        compiler_params=_cp(2))(z, z, z, cos, sin, qg, kvg, wq, wuk, wuv)


def _mla_prep_bwd(dz, dq, dk, dv, z, cos, sin, qg, kvg, wq, wuk, wuv, buf, layer, ts):
    s = z.shape[0]

    def fq(cq, g):
        return _rms(cq, g)

    def body(*refs):
        (dq_ref, dk_ref, dv_ref, cq_ref, ckv_ref, cos_ref, sin_ref, qg_ref, kvg_ref, wq_ref, wuk_ref, wuv_ref,
         dz_ref, dw_ref, dqg_ref, dkvg_ref, dcqn_acc, dckvn_acc, dkr_acc) = refs[-19:]
        i, h = pl.program_id(0), pl.program_id(1)
        cos_v, sin_v = cos_ref[...], sin_ref[...]
        row0 = pl.multiple_of((h % 2) * MLA_ROWS, MLA_ROWS)
        dwq_ref = dw_ref.at[h // 2, 0, pl.ds(row0, Q_LORA)]
        dwuk_ref = dw_ref.at[h // 2, 0, pl.ds(row0 + Q_LORA, KV_LORA)]
        dwuv_ref = dw_ref.at[h // 2, 0, pl.ds(row0 + Q_LORA + KV_LORA, KV_LORA)]

        @pl.when((i == 0) & (h == 0))
        def _():
            dw_ref[...] = jnp.zeros_like(dw_ref)
            dqg_ref[...] = jnp.zeros_like(dqg_ref)
            dkvg_ref[...] = jnp.zeros_like(dkvg_ref)

        @pl.when(h == 0)
        def _():
            dcqn_acc[...] = jnp.zeros_like(dcqn_acc)
            dckvn_acc[...] = jnp.zeros_like(dckvn_acc)
            dkr_acc[...] = jnp.zeros_like(dkr_acc)

        cqn = _rms(cq_ref[...], qg_ref[...])
        ckvn = _rms(ckv_ref[...], kvg_ref[...])
        dq_lin = _rope_transposed(dq_ref[0], cos_v, sin_v)
        dcqn_acc[...] += _dot_nt(dq_lin, wq_ref[0])
        dwq_ref[...] += _dot_tn(cqn, dq_lin)
        dkh, dvh = dk_ref[0], dv_ref[0]
        lane = lax.broadcasted_iota(jnp.int32, dkh.shape, 1)
        dkr_acc[...] += jnp.where((lane >= QK_NOPE) & (lane < QK_NOPE + QK_ROPE), dkh, 0.0)
        dckvn_acc[...] += _dot_nt(dkh, wuk_ref[0]) + _dot_nt(dvh, wuv_ref[0])
        dwuk_ref[...] += _dot_tn(ckvn, dkh)
        dwuv_ref[...] += _dot_tn(ckvn, dvh)

        @pl.when(h == HEADS - 1)
        def _():
            _, vjp_q = jax.vjp(fq, cq_ref[...], qg_ref[...])
            dcq, dqg = vjp_q(dcqn_acc[...])
            _, vjp_kv = jax.vjp(fq, ckv_ref[...], kvg_ref[...])
            dckv, dkvg = vjp_kv(dckvn_acc[...])
            dz_ref[:, 0:Q_LORA] = dcq
            dz_ref[:, Q_LORA:Q_LORA + KV_LORA] = dckv
            dz_ref[:, Q_LORA + KV_LORA:] = _rope_transposed(dkr_acc[...], cos_v, sin_v)
            dqg_ref[...] += dqg
            dkvg_ref[...] += dkvg

    row = lambda w, cb: pl.BlockSpec((ts, w), lambda i, h: (i, cb))
    vec = lambda w: pl.BlockSpec((1, w), lambda i, h: (0, 0))
    wsp = lambda k: pl.BlockSpec((1, k, HEAD_PAD), lambda i, h: (h, 0, 0))
    hrow = pl.BlockSpec((1, ts, HEAD_PAD), lambda i, h: (h, i, 0))
    whole = pl.BlockSpec((N_CHIPS, 1, 2 * MLA_ROWS, HEAD_PAD), lambda i, h: (0, layer, 0, 0))
    passed = (dz,) if buf is None else (dz, buf)
    return pl.pallas_call(
        body, name="mla_prep_bwd", grid=(s // ts, HEADS),
        in_specs=[_HBM_SPEC] * len(passed) + [hrow, hrow, hrow, row(Q_LORA, 8), row(KV_LORA, 18),
                  row(HEAD_PAD, 0), row(HEAD_PAD, 0), vec(Q_LORA), vec(KV_LORA), wsp(Q_LORA), wsp(KV_LORA), wsp(KV_LORA)],
        out_specs=[row(512, 4), whole, vec(Q_LORA), vec(KV_LORA)],
        out_shape=[_sds(dz.shape), _sds((N_CHIPS, N_PAIRS, 2 * MLA_ROWS, HEAD_PAD)), _sds((1, Q_LORA)), _sds((1, KV_LORA))],
        scratch_shapes=[pltpu.VMEM((ts, Q_LORA), F32), pltpu.VMEM((ts, KV_LORA), F32), pltpu.VMEM((ts, HEAD_PAD), F32)],
        input_output_aliases={0: 0} if buf is None else {0: 0, 1: 1},
        compiler_params=_cp(2))(*passed, dq, dk, dv, z, z, cos, sin, qg, kvg, wq, wuk, wuv)


def _chunk_mask(q0, k0, tq, tk):
    rows = q0 + lax.broadcasted_iota(jnp.int32, (tq, tk), 0)
    cols = k0 + lax.broadcasted_iota(jnp.int32, (tq, tk), 1)
    return lax.shift_right_logical(cols, 6) <= lax.shift_right_logical(rows, 6)


def _attn_fwd(q, k, v, tq):
    s = q.shape[1]
    nq = s // tq
    scale = 1.0 / float(QK_NOPE + QK_ROPE) ** 0.5

    def body(q_ref, k_ref, v_ref, o_ref, lse_ref):
        qi, hh = pl.program_id(1), pl.program_id(2)
        qv = q_ref[0]

        def step(kj, carry, masked):
            m, l, acc = carry
            k0 = pl.multiple_of(kj * tq, tq)
            sc = _dot_nt(qv, k_ref[0, pl.ds(k0, tq), :]) * scale
            if masked:
                sc = jnp.where(_chunk_mask(qi * tq, k0, tq, tq), sc, NEG)
            m_new = jnp.maximum(m, jnp.max(sc, axis=-1, keepdims=True))
            alpha = jnp.exp(m - m_new)
            p = jnp.exp(sc - m_new)
            l = alpha * l + jnp.sum(p, axis=-1, keepdims=True)
            acc = alpha * acc + _dot_nn(p, v_ref[0, pl.ds(k0, tq), :])
            return m_new, l, acc

        init = (jnp.full((tq, 1), NEG, F32), jnp.zeros((tq, 1), F32), jnp.zeros((tq, HEAD_PAD), F32))
        carry = lax.fori_loop(0, qi, lambda kj, c: step(kj, c, False), init)
        m, l, acc = step(qi, carry, True)
        o = acc / l
        lse_ref[0] = m + jnp.log(l)

        @pl.when(hh == 0)
        def _():
            o_ref[...] = o

        @pl.when(hh == 1)
        def _():
            o_ref[...] += o

    head = lambda hp, qi, hh: 2 * hp + hh
    return pl.pallas_call(
        body, name="attn_fwd", grid=(HEADS // 2, nq, 2),
        in_specs=[pl.BlockSpec((1, tq, HEAD_PAD), lambda hp, qi, hh: (head(hp, qi, hh), qi, 0)),
                  pl.BlockSpec((1, s, HEAD_PAD), lambda hp, qi, hh: (head(hp, qi, hh), 0, 0)),
                  pl.BlockSpec((1, s, HEAD_PAD), lambda hp, qi, hh: (head(hp, qi, hh), 0, 0))],
        out_specs=[pl.BlockSpec((tq, HEAD_PAD), lambda hp, qi, hh: (qi, hp)),
                   pl.BlockSpec((1, tq, 1), lambda hp, qi, hh: (head(hp, qi, hh), qi, 0))],
        out_shape=[_sds((s, HEADS * V_HEAD)), _sds((HEADS, s, 1))],
        compiler_params=_cp(3))(q, k, v)


def _attn_bwd(q, k, v, do, o, lse, tq):
    s = q.shape[1]
    nq = s // tq
    scale = 1.0 / float(QK_NOPE + QK_ROPE) ** 0.5

    def body(q_ref, k_ref, v_ref, do_ref, o_ref, lse_ref, dq_ref, dk_ref, dv_ref):
        hh, kj = pl.program_id(1), pl.program_id(2)

        @pl.when(kj == 0)
        def _():
            dq_ref[...] = jnp.zeros_like(dq_ref)

        kv, vv = k_ref[0], v_ref[0]
        lane = lax.broadcasted_iota(jnp.int32, (tq, HEAD_PAD), 1)
        mine = lax.shift_right_logical(lane, 6) == hh

        def step(qi, carry, masked):
            dk_acc, dv_acc = carry
            q0 = pl.multiple_of(qi * tq, tq)
            qv = q_ref[0, pl.ds(q0, tq), :]
            dov = do_ref[pl.ds(q0, tq), :]
            delta = jnp.sum(jnp.where(mine, dov * o_ref[pl.ds(q0, tq), :], 0.0), axis=-1, keepdims=True)
            sc = _dot_nt(qv, kv) * scale
            if masked:
                sc = jnp.where(_chunk_mask(q0, kj * tq, tq, tq), sc, NEG)
            p = jnp.exp(sc - lse_ref[0, pl.ds(q0, tq), :])
            do_b = dov.astype(MXU_DTYPE)
            ds = (p * (_dot_nt(do_b, vv) - delta) * scale).astype(MXU_DTYPE)
            dv_acc = dv_acc + _dot_tn(p, do_b)
            dk_acc = dk_acc + _dot_tn(ds, qv)
            dq_ref[0, pl.ds(q0, tq), :] += _dot_nn(ds, kv)
            return dk_acc, dv_acc

        zero = jnp.zeros((tq, HEAD_PAD), F32)
        carry = step(kj, (zero, zero), True)
        dk_acc, dv_acc = lax.fori_loop(kj + 1, nq, lambda qi, c: step(qi, c, False), carry)
        dk_ref[0] = dk_acc
        dv_ref[0] = dv_acc

    head = lambda hp, hh, kj: 2 * hp + hh
    full = pl.BlockSpec((1, s, HEAD_PAD), lambda hp, hh, kj: (head(hp, hh, kj), 0, 0))
    blk = pl.BlockSpec((1, tq, HEAD_PAD), lambda hp, hh, kj: (head(hp, hh, kj), kj, 0))
    pair = pl.BlockSpec((s, HEAD_PAD), lambda hp, hh, kj: (0, hp))
    return pl.pallas_call(
        body, name="attn_bwd", grid=(HEADS // 2, 2, nq),
        in_specs=[full, blk, blk, pair, pair, pl.BlockSpec((1, s, 1), lambda hp, hh, kj: (head(hp, hh, kj), 0, 0))],
        out_specs=[full, blk, blk], out_shape=[_sds((HEADS, s, HEAD_PAD))] * 3,
        compiler_params=_cp(3))(q, k, v, do, o, lse)


def _sc_conv(u, ubuf, w_ref, b_ref, ts):
    return (w_ref[2:3, :] * u + w_ref[1:2, :] * ubuf[pl.ds(SC_HALO - 1, ts), :]
            + w_ref[0:1, :] * ubuf[pl.ds(SC_HALO - 2, ts), :] + b_ref[...])


def _even_gate_fwd(z, o, sc_w, sc_b, ts):
    s = z.shape[0]
    w = SC_WIDTH

    def body(ab_ref, ac_ref, ax_ref, ag_ref, bg_ref, hc_ref, hx_ref, o_ref, w_ref, b_ref, y_ref, ubuf):
        i = pl.program_id(0)
        u = ac_ref[...] * ax_ref[...]
        ubuf[0:SC_HALO, :] = jnp.where(i > 0, hc_ref[...] * hx_ref[...], 0.0)
        ubuf[SC_HALO:, :] = u
        conv = _sc_conv(u, ubuf, w_ref, b_ref, ts)
        y_ref[:, 0:w] = (ab_ref[...] * conv * _silu(ag_ref[...])).astype(y_ref.dtype)
        y_ref[:, w:] = (o_ref[...] * _silu(bg_ref[...])).astype(y_ref.dtype)

    return pl.pallas_call(
        body, name="even_gate_fwd", grid=(s // ts,),
        in_specs=[_rows(ts, w, 0), _rows(ts, w, 1), _rows(ts, w, 2), _rows(ts, w, 3), _rows(ts, w, 5),
                  _prev_halo(ts, SC_HALO, w, 1), _prev_halo(ts, SC_HALO, w, 2), _rows(ts, w),
                  _vec(w, 0, SC_KERNEL), _vec(w)],
        out_specs=_rows(ts, 2 * w), out_shape=_sds((s, 2 * w), MXU_DTYPE),
        scratch_shapes=[pltpu.VMEM((ts + SC_HALO, w), F32)],
        compiler_params=_cp(1))(z, z, z, z, z, z, z, o, sc_w, sc_b)


def _even_gate_bwd(dy, z, o, sc_w, sc_b, ts):
    s = z.shape[0]
    w = SC_WIDTH
    n = s // ts

    def body(dya_ref, dyb_ref, dyan_ref, ab_ref, ac_ref, ax_ref, ag_ref, bg_ref, hc_ref, hx_ref, abn_ref, agn_ref,
             o_ref, w_ref, b_ref, dz_ref, do_ref, dw_ref, db_ref, ubuf, dbuf):
        i = pl.program_id(0)
        ab, ac, ax, ag, bg = ab_ref[...], ac_ref[...], ax_ref[...], ag_ref[...], bg_ref[...]
        dya, dyb = dya_ref[...], dyb_ref[...]
        u = ac * ax
        ubuf[0:SC_HALO, :] = jnp.where(i > 0, hc_ref[...] * hx_ref[...], 0.0)
        ubuf[SC_HALO:, :] = u
        conv = _sc_conv(u, ubuf, w_ref, b_ref, ts)
        sg = _silu(ag)
        dconv = dya * ab * sg
        dbuf[0:ts, :] = dconv
        dbuf[ts:, :] = jnp.where(i < n - 1, dyan_ref[...] * abn_ref[...] * _silu(agn_ref[...]), 0.0)
        du = w_ref[2:3, :] * dconv + w_ref[1:2, :] * dbuf[pl.ds(1, ts), :] + w_ref[0:1, :] * dbuf[pl.ds(2, ts), :]
        dz_ref[:, 0:w] = dya * conv * sg
        dz_ref[:, w:2 * w] = du * ax
        dz_ref[:, 2 * w:3 * w] = du * ac
        dz_ref[:, 3 * w:4 * w] = dya * ab * conv * _dsilu(ag)
        dz_ref[:, 4 * w:5 * w] = jnp.zeros((ts, w), F32)
        dz_ref[:, 5 * w:] = dyb * o_ref[...] * _dsilu(bg)
        do_ref[...] = dyb * _silu(bg)

        @pl.when(i == 0)
        def _():
            dw_ref[...] = jnp.zeros_like(dw_ref)
            db_ref[...] = jnp.zeros_like(db_ref)

        dw_ref[0:1, :] += jnp.sum(dconv * ubuf[pl.ds(SC_HALO - 2, ts), :], axis=0, keepdims=True)
        dw_ref[1:2, :] += jnp.sum(dconv * ubuf[pl.ds(SC_HALO - 1, ts), :], axis=0, keepdims=True)
        dw_ref[2:3, :] += jnp.sum(dconv * u, axis=0, keepdims=True)
        db_ref[...] += jnp.sum(dconv, axis=0, keepdims=True)

    return pl.pallas_call(
        body, name="even_gate_bwd", grid=(n,),
        in_specs=[_rows(ts, w, 0), _rows(ts, w, 1), _next_halo(ts, SC_HALO, w, 0, s),
                  _rows(ts, w, 0), _rows(ts, w, 1), _rows(ts, w, 2), _rows(ts, w, 3), _rows(ts, w, 5),
                  _prev_halo(ts, SC_HALO, w, 1), _prev_halo(ts, SC_HALO, w, 2),
                  _next_halo(ts, SC_HALO, w, 0, s), _next_halo(ts, SC_HALO, w, 3, s),
                  _rows(ts, w), _vec(w, 0, SC_KERNEL), _vec(w)],
        out_specs=[_rows(ts, EVEN_PAD), _rows(ts, w), _vec(w, 0, SC_KERNEL), _vec(w)],
        out_shape=[_sds((s, EVEN_PAD)), _sds((s, w)), _sds((SC_KERNEL, w)), _sds((1, w))],
        scratch_shapes=[pltpu.VMEM((ts + SC_HALO, w), F32), pltpu.VMEM((ts + SC_HALO, w), F32)],
        compiler_params=_cp(1))(dy, dy, dy, z, z, z, z, z, z, z, z, z, o, sc_w, sc_b)


def _ln_act(uc, sg, g, b):
    mu = jnp.mean(uc, axis=-1, keepdims=True)
    var = jnp.mean(jnp.square(uc - mu), axis=-1, keepdims=True)
    return _silu((uc - mu) * lax.rsqrt(var + EPS) * g + b) * _silu(sg)


def _odd_fwd(z, conv_w, conv_b, ln_g, ln_b, ts):
    s = z.shape[0]
    d = D_MODEL
    k = CONF_KERNEL

    def body(val_ref, glu_ref, sg_ref, hval_ref, hglu_ref, w_ref, b_ref, g_ref, beta_ref, y_ref, uc_ref, ubuf):
        i = pl.program_id(0)
        ubuf[0:CONF_HALO, :] = jnp.where(i > 0, hval_ref[...] * _sigmoid(hglu_ref[...]), 0.0)
        ubuf[CONF_HALO:, :] = val_ref[...] * _sigmoid(glu_ref[...])
        for r0 in range(0, ts, CONV_ROWS):
            acc = jnp.broadcast_to(b_ref[...], (CONV_ROWS, d))
            for j in range(k):
                acc = acc + w_ref[j:j + 1, :] * ubuf[pl.ds(r0 + CONF_HALO - (k - 1) + j, CONV_ROWS), :]
            uc_ref[r0:r0 + CONV_ROWS, :] = acc
        y_ref[...] = _ln_act(uc_ref[...], sg_ref[...], g_ref[...], beta_ref[...]).astype(y_ref.dtype)

    return pl.pallas_call(
        body, name="odd_fwd", grid=(s // ts,),
        in_specs=[_rows(ts, d, 0), _rows(ts, d, 1), _rows(ts, d, 2),
                  _prev_halo(ts, CONF_HALO, d, 0), _prev_halo(ts, CONF_HALO, d, 1),
                  _vec(d, 0, k), _vec(d), _vec(d), _vec(d)],
        out_specs=[_rows(ts, d), _rows(ts, d)], out_shape=[_sds((s, d), MXU_DTYPE), _sds((s, d))],
        scratch_shapes=[pltpu.VMEM((ts + CONF_HALO, d), F32)],
        compiler_params=_cp(1))(z, z, z, z, z, conv_w, conv_b, ln_g, ln_b)


def _odd_bwd(dy, z, uc, conv_w, ln_g, ln_b, ts):
    s = z.shape[0]
    d = D_MODEL
    k = CONF_KERNEL
    n = s // ts

    def body(dy_ref, dyn_ref, val_ref, glu_ref, sg_ref, sgn_ref, hval_ref, hglu_ref, uc_ref, ucn_ref,
             w_ref, g_ref, beta_ref, dz_ref, dw_ref, db_ref, dg_ref, dbeta_ref, ubuf, dbuf):
        i = pl.program_id(0)
        val, glu = val_ref[...], glu_ref[...]
        sig = _sigmoid(glu)
        ubuf[0:CONF_HALO, :] = jnp.where(i > 0, hval_ref[...] * _sigmoid(hglu_ref[...]), 0.0)
        ubuf[CONF_HALO:, :] = val * sig
        _, vjp = jax.vjp(_ln_act, uc_ref[...], sg_ref[...], g_ref[...], beta_ref[...])
        duc, dsg, dg, dbeta = vjp(dy_ref[...])
        _, vjp_n = jax.vjp(_ln_act, ucn_ref[...], sgn_ref[...], g_ref[...], beta_ref[...])
        dbuf[0:ts, :] = duc
        dbuf[ts:, :] = jnp.where(i < n - 1, vjp_n(dyn_ref[...])[0], 0.0)
        dz_ref[:, 2 * d:] = dsg

        @pl.when(i == 0)
        def _():
            dw_ref[...] = jnp.zeros_like(dw_ref)
            db_ref[...] = jnp.zeros_like(db_ref)
            dg_ref[...] = jnp.zeros_like(dg_ref)
            dbeta_ref[...] = jnp.zeros_like(dbeta_ref)

        db_ref[...] += jnp.sum(duc, axis=0, keepdims=True)
        dg_ref[...] += dg
        dbeta_ref[...] += dbeta
        for r0 in range(0, ts, CONV_ROWS):
            acc = jnp.zeros((CONV_ROWS, d), F32)
            for j in range(k):
                acc = acc + w_ref[j:j + 1, :] * dbuf[pl.ds(r0 + (k - 1) - j, CONV_ROWS), :]
            sig_r = sig[r0:r0 + CONV_ROWS, :]
            dz_ref[r0:r0 + CONV_ROWS, 0:d] = acc * sig_r
            dz_ref[r0:r0 + CONV_ROWS, d:2 * d] = acc * val[r0:r0 + CONV_ROWS, :] * sig_r * (1.0 - sig_r)
        for j in range(k):
            dw_ref[j:j + 1, :] += jnp.sum(duc * ubuf[pl.ds(CONF_HALO - (k - 1) + j, ts), :], axis=0, keepdims=True)

    return pl.pallas_call(
        body, name="odd_bwd", grid=(n,),
        in_specs=[_rows(ts, d), _next_halo(ts, CONF_HALO, d, 0, s),
                  _rows(ts, d, 0), _rows(ts, d, 1), _rows(ts, d, 2), _next_halo(ts, CONF_HALO, d, 2, s),
                  _prev_halo(ts, CONF_HALO, d, 0), _prev_halo(ts, CONF_HALO, d, 1),
                  _rows(ts, d), _next_halo(ts, CONF_HALO, d, 0, s),
                  _vec(d, 0, k), _vec(d), _vec(d)],
        out_specs=[_rows(ts, ODD_IN), _vec(d, 0, k), _vec(d), _vec(d), _vec(d)],
        out_shape=[_sds((s, ODD_IN)), _sds((k, d)), _sds((1, d)), _sds((1, d)), _sds((1, d))],
        scratch_shapes=[pltpu.VMEM((ts + CONF_HALO, d), F32), pltpu.VMEM((ts + CONF_HALO, d), F32)],
        compiler_params=_cp(1))(dy, dy, z, z, z, z, z, z, uc, uc, conv_w, ln_g, ln_b)


def _local_step(x, target, cos, sin, mod, p):
    s = x.shape[0]
    tsf, tsb = min(512, s // 2), min(256, s // 2)
    tq = min(512, s // 2)
    row1 = lambda a, i: a[i:i + 1]
    saved = []
    for layer in range(DEPTH):
        i = layer // 2
        mod_l = row1(mod, layer)
        h = _pre_fwd(x, row1(p["pre_norm_g"], layer), mod_l, tsf)
        if layer % 2 == 0:
            z = _mm(h, p["even_w_in"][i], "nn", F32, 512, 1024, "even_in_fwd")
            q, k, v = _mla_prep_fwd(z, cos, sin, row1(p["even_q_norm_g"], i), row1(p["even_kv_norm_g"], i),
                                    p["even_wq"][i], p["even_wuk"][i], p["even_wuv"][i], tsf)
            o, lse = _attn_fwd(q, k, v, tq)
            y = _even_gate_fwd(z, o, p["even_sc_conv_w"][i], row1(p["even_sc_conv_b"], i), tsf)
            yo = _mm(y, p["even_w_out"][i], "nn", F32, 512, 1024, "even_out_fwd")
            saved.append((x, h, z, y, yo, (q, k, v, o, lse)))
        else:
            z = _mm(h, p["odd_w_in"][i], "nn", F32, 512, 1024, "odd_in_fwd")
            y, uc = _odd_fwd(z, p["odd_conv_w"][i], row1(p["odd_conv_b"], i), row1(p["odd_ln_g"], i),
                             row1(p["odd_ln_b"], i), tsf)
            yo = _mm(y, p["odd_w_out"][i], "nn", F32, 512, 1024, "odd_out_fwd")
            saved.append((x, h, z, y, yo, uc))
        x = _post_fwd(x, yo, row1(p["post_norm_g"], layer), mod_l, tsf)

    loss, dx = _loss_fwd_bwd(x, target, tsf)

    g = {n: [None] * (DEPTH if n in ("pre_norm_g", "post_norm_g") else N_PAIRS) for n in (
        "pre_norm_g", "post_norm_g", "even_sc_conv_w", "even_sc_conv_b", "even_q_norm_g", "even_kv_norm_g",
        "odd_conv_w", "odd_conv_b", "odd_ln_g", "odd_ln_b")}
    bufs = dict.fromkeys(("even_w_in", "even_mla", "even_w_out", "odd_w_in", "odd_w_out"))
    dmod = [None] * DEPTH
    for layer in reversed(range(DEPTH)):
        i = layer // 2
        mod_l = row1(mod, layer)
        x_in, h, z, y, yo, extra = saved[layer]
        dyo, dgate, g["post_norm_g"][layer] = _post_bwd(dx, yo, row1(p["post_norm_g"], layer), mod_l, tsb)
        if layer % 2 == 0:
            q, k, v, o, lse = extra
            dy = _mm(dyo, p["even_w_out"][i], "nt", F32, 512, 1024, "even_out_bwd_x")
            bufs["even_w_out"] = _mm_tn_shards(y, dyo, bufs["even_w_out"], i, "rows", "even_out_bwd_w")
            dz, do, g["even_sc_conv_w"][i], g["even_sc_conv_b"][i] = _even_gate_bwd(
                dy, z, o, p["even_sc_conv_w"][i], row1(p["even_sc_conv_b"], i), tsb)
            dq, dk, dv = _attn_bwd(q, k, v, do, o, lse, tq)
            dz, bufs["even_mla"], g["even_q_norm_g"][i], g["even_kv_norm_g"][i] = _mla_prep_bwd(
                dz, dq, dk, dv, z, cos, sin, row1(p["even_q_norm_g"], i), row1(p["even_kv_norm_g"], i),
                p["even_wq"][i], p["even_wuk"][i], p["even_wuv"][i], bufs["even_mla"], i, tsb)
            dh = _mm(dz, p["even_w_in"][i], "nt", F32, 256, 1024, "even_in_bwd_x")
            bufs["even_w_in"] = _ein_to_shards(_mm(h, dz, "tn", F32, 512, 512, "even_in_bwd_w"), bufs["even_w_in"], i)
        else:
            uc = extra
            dy = _mm(dyo, p["odd_w_out"][i], "nt", F32, 512, 1024, "odd_out_bwd_x")
            bufs["odd_w_out"] = _mm_tn_shards(y, dyo, bufs["odd_w_out"], i, "rows", "odd_out_bwd_w")
            dz, g["odd_conv_w"][i], g["odd_conv_b"][i], g["odd_ln_g"][i], g["odd_ln_b"][i] = _odd_bwd(
                dy, z, uc, p["odd_conv_w"][i], row1(p["odd_ln_g"], i), row1(p["odd_ln_b"], i), tsb)
            dh = _mm(dz, p["odd_w_in"][i], "nt", F32, 256, 1024, "odd_in_bwd_x")
            bufs["odd_w_in"] = _mm_tn_shards(h, dz, bufs["odd_w_in"], i, "cols", "odd_in_bwd_w")
        dx, dshift, dscale, g["pre_norm_g"][layer] = _pre_bwd(dh, dx, x_in, row1(p["pre_norm_g"], layer), mod_l, tsb)
        dmod[layer] = jnp.concatenate([dshift, dscale, dgate], axis=-1)
    stack = lambda parts: jnp.stack([a[0] if a.shape[0] == 1 and a.ndim == 2 else a for a in parts])
    small = {n: stack(parts) for n, parts in g.items()}
    small["dmod"] = jnp.concatenate(dmod, axis=0)
    return loss, dx, small, bufs


def _uq_to_heads(w):
    l = w.shape[0]
    w = w.reshape(l, Q_LORA, HEADS, QK_NOPE + QK_ROPE).transpose(0, 2, 1, 3)
    return jnp.pad(w, ((0, 0), (0, 0), (0, 0), (0, HEAD_PAD - QK_NOPE - QK_ROPE)))


def _ukv_to_heads(w):
    l = w.shape[0]
    w = w.reshape(l, KV_LORA, HEADS, QK_NOPE + V_HEAD).transpose(0, 2, 1, 3)
    wk = jnp.pad(w[..., :QK_NOPE], ((0, 0), (0, 0), (0, 0), (0, HEAD_PAD - QK_NOPE)))
    wv = w[..., QK_NOPE:]
    zero = jnp.zeros_like(wv)
    odd = (jnp.arange(HEADS) % 2 == 1)[None, :, None, None]
    wv = jnp.concatenate([jnp.where(odd, zero, wv), jnp.where(odd, wv, zero)], axis=-1)
    return wk, wv


def _mla_local(q):
    blocks = q.reshape(q.shape[0], 2, MLA_ROWS, HEAD_PAD)
    uq = jnp.concatenate([blocks[:, r, :Q_LORA, :QK_NOPE + QK_ROPE] for r in range(2)], axis=-1)
    ukv = jnp.concatenate(
        [jnp.concatenate([blocks[:, r, Q_LORA:Q_LORA + KV_LORA, :QK_NOPE],
                          blocks[:, r, Q_LORA + KV_LORA:, V_HEAD * r:V_HEAD * (r + 1)]], axis=-1) for r in range(2)], axis=-1)
    return uq, ukv


def _place():
    return lax.axis_index("x"), lax.axis_index("y"), lax.axis_index("c")


def _flip(v, bit):
    return 1 - v if bit else v


def _remote(src, dst, send_sem, recv_sem, peer):
    return pltpu.make_async_remote_copy(src_ref=src, dst_ref=dst, send_sem=send_sem, recv_sem=recv_sem,
                                        device_id=peer, device_id_type=MESH)


_VMEM_SPEC = pl.BlockSpec(memory_space=pltpu.VMEM)
_HBM_SPEC = pl.BlockSpec(memory_space=pl.ANY)


def _ada_fwd(c8, ada_w, ada_b_sh):
    depth, d, cols = ada_w.shape

    def body(c_ref, w_ref, b_ref, call_ref, mod_ref, s1, r1, s2, r2):
        x, y, c = _place()
        chip = 2 * x + y
        me = 2 * chip + c
        call_ref[me] = c_ref[...]
        sends = []
        for k in range(1, N_DEV):
            peer = (_flip(x, k & 4), _flip(y, k & 2), _flip(c, k & 1))
            cp = _remote(c_ref, call_ref.at[me], s1.at[k - 1], r1.at[k - 1], peer)
            cp.start()
            sends.append(cp)
        for k in range(1, N_DEV):
            src = 4 * _flip(x, k & 4) + 2 * _flip(y, k & 2) + _flip(c, k & 1)
            _remote(c_ref, call_ref.at[src], s1.at[k - 1], r1.at[k - 1], (x, y, c)).wait_recv()
        act = _silu(call_ref[...]).reshape(N_DEV * 8, d)
        for l in range(depth):
            mod_ref[chip, l] = _dot_nn(act, w_ref[l]) + b_ref[l:l + 1, :]
        for k in range(1, N_CHIPS):
            peer = (_flip(x, k & 2), _flip(y, k & 1), c)
            cp = _remote(mod_ref.at[chip], mod_ref.at[chip], s2.at[k - 1], r2.at[k - 1], peer)
            cp.start()
            sends.append(cp)
        for k in range(1, N_CHIPS):
            src = 2 * _flip(x, k & 2) + _flip(y, k & 1)
            _remote(mod_ref.at[src], mod_ref.at[src], s2.at[k - 1], r2.at[k - 1], (x, y, c)).wait_recv()
        for cp in sends:
            cp.wait_send()

    return pl.pallas_call(
        body, name="ada_fwd", in_specs=[_VMEM_SPEC] * 3, out_specs=[_VMEM_SPEC] * 2,
        out_shape=[_sds((N_DEV, 8, d)), _sds((N_CHIPS, depth, N_DEV * 8, cols))],
        scratch_shapes=[pltpu.SemaphoreType.DMA((N_DEV - 1,)), pltpu.SemaphoreType.DMA((N_DEV - 1,)),
                        pltpu.SemaphoreType.DMA((N_CHIPS - 1,)), pltpu.SemaphoreType.DMA((N_CHIPS - 1,))],
        compiler_params=pltpu.CompilerParams(vmem_limit_bytes=VMEM_LIMIT_V7X))(c8, ada_w, ada_b_sh)


def _ada_bwd(c_t, dmod_sh):
    depth, n, cols = dmod_sh.shape
    d = c_t.shape[0]
    tr = 256

    def body(c_ref, dm_ref, o_ref):
        act = _silu(c_ref[...])
        acc = act[:, 0:1] * dm_ref[0, 0:1, :]
        for e in range(1, n):
            acc = acc + act[:, e:e + 1] * dm_ref[0, e:e + 1, :]
        o_ref[0] = acc

    return pl.pallas_call(
        body, name="ada_bwd", grid=(depth, d // tr),
        in_specs=[pl.BlockSpec((tr, n), lambda l, i: (i, 0)), pl.BlockSpec((1, n, cols), lambda l, i: (l, 0, 0))],
        out_specs=pl.BlockSpec((1, tr, cols), lambda l, i: (l, i, 0)), out_shape=_sds((depth, d, cols)),
        compiler_params=_cp(2))(c_t, dmod_sh)


def _gathered_shape(shape, how):
    if how == "slot":
        return (N_CHIPS,) + shape
    if how == "slot1":
        return shape[:1] + (N_CHIPS,) + shape[1:]
    l, r, cc = shape
    return (l, r, N_CHIPS * cc) if how == "cols" else (l, N_CHIPS * r, cc)


def _gathered_part(ref, shape, how, chip):
    if how == "slot":
        return ref.at[chip]
    if how == "slot1":
        return ref.at[:, chip]
    if how == "cols":
        return ref.at[:, :, pl.ds(pl.multiple_of(chip * shape[2], 128), shape[2])]
    return ref.at[:, pl.ds(pl.multiple_of(chip * shape[1], 8), shape[1]), :]


def _gather_chips(items):
    n = len(items)
    arrays = [a for a, _ in items]

    def body(*refs):
        ins, outs = refs[:n], refs[n:2 * n]
        send_sems, recv_sems, local_sems = refs[2 * n:]
        x, y, c = _place()
        chip = 2 * x + y
        part = lambda a, j: _gathered_part(outs[a], items[a][0].shape, items[a][1], j)
        local = [pltpu.make_async_copy(ins[a], part(a, chip), local_sems.at[a]) for a in range(n)]
        for cp in local:
            cp.start()
        sends = []
        for a in range(n):
            for k in range(1, N_CHIPS):
                peer = (_flip(x, k & 2), _flip(y, k & 1), c)
                cp = _remote(ins[a], part(a, chip), send_sems.at[a, k - 1], recv_sems.at[a, k - 1], peer)
                cp.start()
                sends.append(cp)
        for a in range(n):
            for k in range(1, N_CHIPS):
                src = 2 * _flip(x, k & 2) + _flip(y, k & 1)
                _remote(ins[a], part(a, src), send_sems.at[a, k - 1], recv_sems.at[a, k - 1], (x, y, c)).wait_recv()
        for cp in sends:
            cp.wait_send()
        for cp in local:
            cp.wait()

    return pl.pallas_call(
        body, name="gather_chips", in_specs=[_HBM_SPEC] * n, out_specs=[_HBM_SPEC] * n,
        out_shape=[_sds(_gathered_shape(a.shape, how), a.dtype) for a, how in items],
        scratch_shapes=[pltpu.SemaphoreType.DMA((n, N_CHIPS - 1)), pltpu.SemaphoreType.DMA((n, N_CHIPS - 1)),
                        pltpu.SemaphoreType.DMA((n,))])(*arrays)


def _gather_sum_all(small):
    r, w = small.shape

    def body(in_ref, all_ref, sum_ref, send_sems, recv_sems):
        x, y, c = _place()
        me = 4 * x + 2 * y + c
        all_ref[me] = in_ref[...]
        sends = []
        for k in range(1, N_DEV):
            peer = (_flip(x, k & 4), _flip(y, k & 2), _flip(c, k & 1))
            cp = _remote(in_ref, all_ref.at[me], send_sems.at[k - 1], recv_sems.at[k - 1], peer)
            cp.start()
            sends.append(cp)
        for k in range(1, N_DEV):
            src = 4 * _flip(x, k & 4) + 2 * _flip(y, k & 2) + _flip(c, k & 1)
            _remote(in_ref, all_ref.at[src], send_sems.at[k - 1], recv_sems.at[k - 1], (x, y, c)).wait_recv()
        acc = all_ref[0]
        for e in range(1, N_DEV):
            acc = acc + all_ref[e]
        sum_ref[...] = acc
        for cp in sends:
            cp.wait_send()

    return pl.pallas_call(
        body, name="gather_sum_all", in_specs=[_VMEM_SPEC], out_specs=[_VMEM_SPEC] * 2,
        out_shape=[_sds((N_DEV, r, w)), _sds((r, w))],
        scratch_shapes=[pltpu.SemaphoreType.DMA((N_DEV - 1,)), pltpu.SemaphoreType.DMA((N_DEV - 1,))],
        compiler_params=pltpu.CompilerParams(vmem_limit_bytes=VMEM_LIMIT_V7X))(small)


def _rs_chips(srcs):
    n = len(srcs)

    def body(*refs):
        bufs, outs = refs[:n], refs[n:2 * n]
        send_sems, recv_sems = refs[2 * n:]
        x, y, c = _place()
        chip = 2 * x + y
        sends = []
        for a in range(n):
            for k in range(1, N_CHIPS):
                tx, ty = _flip(x, k & 2), _flip(y, k & 1)
                cp = _remote(bufs[a].at[2 * tx + ty, srcs[a][1]], outs[a].at[k - 1],
                             send_sems.at[a, k - 1], recv_sems.at[a, k - 1], (tx, ty, c))
                cp.start()
                sends.append(cp)
        for a in range(n):
            for k in range(1, N_CHIPS):
                _remote(bufs[a].at[chip, srcs[a][1]], outs[a].at[k - 1],
                        send_sems.at[a, k - 1], recv_sems.at[a, k - 1], (x, y, c)).wait_recv()
        for cp in sends:
            cp.wait_send()

    return pl.pallas_call(
        body, name="rs_chips", in_specs=[_HBM_SPEC] * n, out_specs=[_HBM_SPEC] * n,
        out_shape=[_sds((N_CHIPS - 1,) + buf.shape[2:]) for buf, _ in srcs],
        scratch_shapes=[pltpu.SemaphoreType.DMA((n, N_CHIPS - 1)), pltpu.SemaphoreType.DMA((n, N_CHIPS - 1))])(
            *[buf for buf, _ in srcs])


def _add_chips(buf, layer, t, chip_idx):
    r, cc = buf.shape[2:]
    tr = min(256, r)

    def body(c_ref, p_ref, t_ref, o_ref):
        del c_ref
        o_ref[...] = p_ref[0, 0] + t_ref[0] + t_ref[1] + t_ref[2]

    return pl.pallas_call(
        body, name="add_chips", out_shape=_sds((r, cc)),
        grid_spec=pltpu.PrefetchScalarGridSpec(
            num_scalar_prefetch=1, grid=(r // tr,),
            in_specs=[pl.BlockSpec((1, 1, tr, cc), lambda i, c: (c[0], layer, i, 0)),
                      pl.BlockSpec((N_CHIPS - 1, tr, cc), lambda i, c: (0, i, 0))],
            out_specs=pl.BlockSpec((tr, cc), lambda i, c: (i, 0))),
        compiler_params=_cp(1))(chip_idx, buf, t)


def _rs_sibling(qs):
    n = len(qs)

    def body(*refs):
        ins, outs = refs[:n], refs[n:2 * n]
        send_sems, recv_sems = refs[2 * n:]
        x, y, c = _place()
        copies = [_remote(ins[a], outs[a], send_sems.at[a], recv_sems.at[a], (x, y, 1 - c)) for a in range(n)]
        for cp in copies:
            cp.start()
        for cp in copies:
            cp.wait()

    return pl.pallas_call(
        body, name="rs_sibling", in_specs=[_HBM_SPEC] * n, out_specs=[_HBM_SPEC] * n,
        out_shape=[_sds(q.shape) for q in qs],
        scratch_shapes=[pltpu.SemaphoreType.DMA((n,)), pltpu.SemaphoreType.DMA((n,))])(*qs)


def _adamw_update(w, g, m, v):
    m = ADAM_B1 * m + (1.0 - ADAM_B1) * g
    v = ADAM_B2 * v + (1.0 - ADAM_B2) * jnp.square(g)
    m_hat = m / (1.0 - ADAM_B1 ** ADAM_STEP)
    v_hat = v / (1.0 - ADAM_B2 ** ADAM_STEP)
    return -ADAM_LR * (m_hat / (jnp.sqrt(v_hat) + ADAM_EPS) + ADAM_WD * w), m, v


def _adamw(w, g_parts, m, v, name):
    shape = w.shape
    cols = shape[-1]
    rows = _size(shape[:-1])
    tr = 512 if rows % 512 == 0 else rows
    spec = pl.BlockSpec((tr, cols), lambda i: (i, 0))
    n = len(g_parts)

    def body(*refs):
        w_ref, m_ref, v_ref = refs[:3]
        g_ref, d_ref, nm_ref, nv_ref = refs[3 + n:]
        g = refs[3][...]
        for r in refs[4:3 + n]:
            g = g + r[...]
        g_ref[...] = g
        d_ref[...], nm_ref[...], nv_ref[...] = _adamw_update(w_ref[...], g, m_ref[...], v_ref[...])

    outs = pl.pallas_call(
        body, name="adamw_" + name, grid=(rows // tr,), in_specs=[spec] * (3 + n), out_specs=[spec] * 4,
        out_shape=[_sds((rows, cols))] * 4, compiler_params=_cp(1))(
            *[a.reshape(rows, cols) for a in (w, m, v, *g_parts)])
    return tuple(o.reshape(shape) for o in outs)


def _adamw_layer(w, g_parts, m, v, layer, prev, name):
    _, r, cc = w.shape
    tr = 512 if r % 512 == 0 else r
    spec = pl.BlockSpec((1, tr, cc), lambda i: (layer, i, 0))
    n = len(g_parts)

    def body(*refs):
        w_ref, m_ref, v_ref = refs[:3]
        g_ref, d_ref, nm_ref, nv_ref = refs[-4:]
        g = refs[3][...]
        for q in refs[4:3 + n]:
            g = g + q[...]
        g = g[:, :cc]
        g_ref[0] = g
        d_ref[0], nm_ref[0], nv_ref[0] = _adamw_update(w_ref[0], g, m_ref[0], v_ref[0])

    g_specs = [pl.BlockSpec((tr, q.shape[1]), lambda i: (i, 0)) for q in g_parts]
    passed = () if prev is None else tuple(prev)
    return pl.pallas_call(
        body, name="adamw_" + name, grid=(r // tr,),
        in_specs=[spec] * 3 + g_specs + [_HBM_SPEC] * len(passed), out_specs=[spec] * 4,
        out_shape=[_sds(w.shape)] * 4, input_output_aliases={3 + n + k: k for k in range(len(passed))},
        compiler_params=_cp(1))(w, m, v, *g_parts, *passed)


def _size(shape):
    n = 1
    for s in shape:
        n *= s
    return n


_SMALL = (("dmod", (DEPTH, 3 * D_MODEL)), ("pre_norm_g", (DEPTH, D_MODEL)), ("post_norm_g", (DEPTH, D_MODEL)),
          ("even_sc_conv_w", (2, SC_KERNEL, SC_WIDTH)), ("even_sc_conv_b", (2, SC_WIDTH)),
          ("even_q_norm_g", (2, Q_LORA)), ("even_kv_norm_g", (2, KV_LORA)),
          ("odd_conv_w", (2, CONF_KERNEL, D_MODEL)), ("odd_conv_b", (2, D_MODEL)), ("odd_ln_g", (2, D_MODEL)),
          ("odd_ln_b", (2, D_MODEL)))
SMALL_ROWS = -(-sum(_size(s) for _, s in _SMALL) // (8 * 128)) * 8

_SMALL_W = (("even_sc_conv_w", (2, SC_KERNEL, SC_WIDTH // N_CHIPS)), ("odd_conv_w", (2, CONF_KERNEL, D_MODEL // N_CHIPS)),
            ("odd_conv_b", (2, D_MODEL // N_CHIPS)), ("odd_ln_g", (2, D_MODEL // N_CHIPS)),
            ("odd_ln_b", (2, D_MODEL // N_CHIPS)))
SMALL_W_ROWS = -(-sum(_size(s) for _, s in _SMALL_W) // (8 * 128)) * 8


def _pack_rows(arrays, layout, rows):
    flat = jnp.concatenate([arrays[n].reshape(-1) for n, _ in layout])
    return jnp.pad(flat, (0, rows * 128 - flat.shape[0])).reshape(rows, 128)


def _unpack_small(t):
    flat = t.reshape(-1)
    out, at = {}, 0
    for n, shape in _SMALL:
        out[n] = flat[at:at + _size(shape)].reshape(shape)
        at += _size(shape)
    return out


def _unpack_small_w(t):
    flat = t.reshape(N_CHIPS, -1)
    out, at = {}, 0
    for n, shape in _SMALL_W:
        a = flat[:, at:at + _size(shape)].reshape((N_CHIPS,) + shape)
        out[n] = jnp.moveaxis(a, 0, -2).reshape(shape[:-1] + (N_CHIPS * shape[-1],))
        at += _size(shape)
    return out


def _chip_cols(a, chip):
    n = a.shape[-1] // N_CHIPS
    return lax.dynamic_slice_in_dim(a, chip * n, n, axis=a.ndim - 1)


def _join_cols(a):
    _, l, r, cc = a.shape
    return a.transpose(1, 2, 0, 3).reshape(l, r, N_CHIPS * cc)


WEIGHT_NAMES = ("ada_w", "ada_b", "pre_norm_g", "post_norm_g", "even_w_in", "even_sc_conv_w", "even_sc_conv_b",
                "even_q_norm_g", "even_kv_norm_g", "even_w_uq", "even_w_ukv", "even_w_out", "odd_w_in", "odd_conv_w",
                "odd_conv_b", "odd_ln_g", "odd_ln_b", "odd_w_out")
GATHER_HOW = (("even_w_in", "slot1"), ("even_w_uq", "slot"), ("even_w_ukv", "slot"), ("even_w_out", "rows"),
              ("odd_w_in", "cols"), ("odd_w_out", "rows"))
LAYER_BUFS = (("odd_w_in", "odd_w_out"), ("even_w_in", "even_mla", "even_w_out"))


def kernel(x, c, positions, ada_w, ada_b, pre_norm_g, post_norm_g, even_w_in, even_sc_conv_w, even_sc_conv_b, even_q_norm_g, even_kv_norm_g, even_w_uq, even_w_ukv, even_w_out, odd_w_in, odd_conv_w, odd_conv_b, odd_ln_g, odd_ln_b, odd_w_out, loss_target, m_ada_w, m_ada_b, m_pre_norm_g, m_post_norm_g, m_even_w_in, m_even_sc_conv_w, m_even_sc_conv_b, m_even_q_norm_g, m_even_kv_norm_g, m_even_w_uq, m_even_w_ukv, m_even_w_out, m_odd_w_in, m_odd_conv_w, m_odd_conv_b, m_odd_ln_g, m_odd_ln_b, m_odd_w_out, v_ada_w, v_ada_b, v_pre_norm_g, v_post_norm_g, v_even_w_in, v_even_sc_conv_w, v_even_sc_conv_b, v_even_q_norm_g, v_even_kv_norm_g, v_even_w_uq, v_even_w_ukv, v_even_w_out, v_odd_w_in, v_odd_conv_w, v_odd_conv_b, v_odd_ln_g, v_odd_ln_b, v_odd_w_out):
    w = dict(zip(WEIGHT_NAMES, (ada_w, ada_b, pre_norm_g, post_norm_g, even_w_in, even_sc_conv_w, even_sc_conv_b,
                                even_q_norm_g, even_kv_norm_g, even_w_uq, even_w_ukv, even_w_out, odd_w_in, odd_conv_w,
                                odd_conv_b, odd_ln_g, odd_ln_b, odd_w_out)))
    m = dict(zip(WEIGHT_NAMES, (m_ada_w, m_ada_b, m_pre_norm_g, m_post_norm_g, m_even_w_in, m_even_sc_conv_w,
                                m_even_sc_conv_b, m_even_q_norm_g, m_even_kv_norm_g, m_even_w_uq, m_even_w_ukv,
                                m_even_w_out, m_odd_w_in, m_odd_conv_w, m_odd_conv_b, m_odd_ln_g, m_odd_ln_b, m_odd_w_out)))
    v = dict(zip(WEIGHT_NAMES, (v_ada_w, v_ada_b, v_pre_norm_g, v_post_norm_g, v_even_w_in, v_even_sc_conv_w,
                                v_even_sc_conv_b, v_even_q_norm_g, v_even_kv_norm_g, v_even_w_uq, v_even_w_ukv,
                                v_even_w_out, v_odd_w_in, v_odd_conv_w, v_odd_conv_b, v_odd_ln_g, v_odd_ln_b, v_odd_w_out)))
    ix, iy, ic = _place()
    chip = 2 * ix + iy
    me = 2 * chip + ic
    s = x.shape[1]

    c_all, mod_all = _ada_fwd(jnp.broadcast_to(c, (8, D_MODEL)), ada_w, _chip_cols(ada_b, chip))
    mod = lax.dynamic_index_in_dim(mod_all, 8 * me, axis=2, keepdims=False)
    mod = mod.transpose(1, 0, 2).reshape(DEPTH, 3 * D_MODEL)

    gathered = _gather_chips([(w[n].astype(MXU_DTYPE), how) for n, how in GATHER_HOW]
                             + [(_pack_rows(w, _SMALL_W, SMALL_W_ROWS), "slot")])
    full = dict(zip([n for n, _ in GATHER_HOW], gathered[:-1]))
    small_w = _unpack_small_w(gathered[-1])
    wuk, wuv = _ukv_to_heads(_join_cols(full["even_w_ukv"]))
    p = {
        "pre_norm_g": pre_norm_g, "post_norm_g": post_norm_g,
        "even_w_in": _ein_from_shards(full["even_w_in"]),
        "even_sc_conv_w": small_w["even_sc_conv_w"], "even_sc_conv_b": even_sc_conv_b,
        "even_q_norm_g": even_q_norm_g, "even_kv_norm_g": even_kv_norm_g,
        "even_wq": _uq_to_heads(_join_cols(full["even_w_uq"])), "even_wuk": wuk, "even_wuv": wuv,
        "even_w_out": full["even_w_out"],
        "odd_w_in": full["odd_w_in"], "odd_conv_w": small_w["odd_conv_w"],
        "odd_conv_b": small_w["odd_conv_b"], "odd_ln_g": small_w["odd_ln_g"], "odd_ln_b": small_w["odd_ln_b"],
        "odd_w_out": full["odd_w_out"],
    }

    inv_freq = 1.0 / (ROPE_THETA ** (jnp.arange(0, QK_ROPE, 2, dtype=F32) / QK_ROPE))
    inv_freq = jnp.zeros((1, HEAD_PAD), F32).at[0, QK_NOPE:QK_NOPE + QK_ROPE].set(jnp.tile(inv_freq, 2))
    cos, sin = _rope_tables(positions.reshape(s, 1), inv_freq)

    loss, dx, g, bufs = _local_step(x[0], loss_target[0], cos, sin, mod, p)

    own, sib = {}, {}
    for i in reversed(range(N_PAIRS)):
        for names in LAYER_BUFS:
            arrived = _rs_chips([(bufs[n], i) for n in names])
            sums = [_add_chips(bufs[n], i, t, chip.reshape(1)) for n, t in zip(names, arrived)]
            for n, mine, theirs in zip(names, sums, _rs_sibling(sums)):
                own[n, i], sib[n, i] = mine, theirs
    grads, deltas, new_m, new_v = {}, {}, {}, {}
    for n in ("even_w_in", "even_w_out", "odd_w_in", "odd_w_out"):
        res = None
        for i in range(N_PAIRS):
            res = _adamw_layer(w[n], [own[n, i], sib[n, i]], m[n], v[n], i, res, n)
        grads[n], deltas[n], new_m[n], new_v[n] = res
    uq_parts, ukv_parts = zip(*[_mla_local(jnp.stack([q["even_mla", i] for i in range(N_PAIRS)])) for q in (own, sib)])
    for n, parts in (("even_w_uq", uq_parts), ("even_w_ukv", ukv_parts)):
        grads[n], deltas[n], new_m[n], new_v[n] = _adamw(w[n], list(parts), m[n], v[n], n)

    small_all, small_sum = _gather_sum_all(_pack_rows(g, _SMALL, SMALL_ROWS))
    tot = _unpack_small(small_sum)
    dmod_all = small_all[:, :DEPTH * 3 * D_MODEL // 128].reshape(N_DEV, DEPTH, 3 * D_MODEL)
    grads["ada_w"] = _ada_bwd(c_all[:, 0, :].T, _chip_cols(dmod_all, chip).transpose(1, 0, 2))
    grads["ada_b"] = tot["dmod"]
    for n in ("pre_norm_g", "post_norm_g", "even_sc_conv_b", "even_q_norm_g", "even_kv_norm_g"):
        grads[n] = tot[n]
    for n in ("even_sc_conv_w", "odd_conv_w", "odd_conv_b", "odd_ln_g", "odd_ln_b"):
        grads[n] = _chip_cols(tot[n], chip)

    for n in WEIGHT_NAMES:
        if n not in deltas:
            _, deltas[n], new_m[n], new_v[n] = _adamw(w[n], [grads[n]], m[n], v[n], n)

    total_loss = lax.psum(loss[0, 0], ("x", "y", "c"))
    return (total_loss, dx[None], *[grads[n] for n in WEIGHT_NAMES], *[deltas[n] for n in WEIGHT_NAMES],
            *[new_m[n] for n in WEIGHT_NAMES], *[new_v[n] for n in WEIGHT_NAMES])
```

```python
import functools

import jax
import jax.numpy as jnp
from jax import lax
from jax.experimental import pallas as pl
from jax.experimental.pallas import tpu as pltpu

F32 = jnp.float32
MXU_DTYPE = jnp.bfloat16
MESH = pl.DeviceIdType.MESH
VMEM_LIMIT_V7X = 56 * 2 ** 20

EPS = 1e-6
D_MODEL = 1024
DEPTH = 4
CHUNK = 64
SC_WIDTH = 512
SC_KERNEL = 3
SC_HALO = 8
HEADS = 8
QK_NOPE = 64
QK_ROPE = 32
V_HEAD = 64
HEAD_PAD = 128
Q_LORA = 256
KV_LORA = 128
ROPE_THETA = 10000.0
CONF_KERNEL = 31
CONF_HALO = 32
CONV_ROWS = 32
EVEN_IN = 2976
EVEN_PAD = 3072
ODD_IN = 3072
N_CHIPS = 4
N_DEV = 8
NEG = -1e30

ADAM_LR = 0.001
ADAM_B1 = 0.9
ADAM_B2 = 0.999
ADAM_EPS = 1e-08
ADAM_WD = 0.01
ADAM_STEP = 10

N_PAIRS = DEPTH // 2
EVEN_SHARD = EVEN_IN // N_CHIPS
EVEN_SHARD_PAD = 768
MLA_ROWS = Q_LORA + 2 * KV_LORA


def _cp(n_grid=0, **kw):
    return pltpu.CompilerParams(dimension_semantics=("arbitrary",) * n_grid,
                                vmem_limit_bytes=VMEM_LIMIT_V7X, **kw)


def _sigmoid(x):
    return 1.0 / (1.0 + jnp.exp(-x))


def _silu(x):
    return x * _sigmoid(x)


def _dsilu(x):
    s = _sigmoid(x)
    return s * (1.0 + x * (1.0 - s))


def _rms(x, g):
    return x * lax.rsqrt(jnp.mean(x * x, axis=-1, keepdims=True) + EPS) * g


def _dot(a, b, dims):
    return lax.dot_general(a.astype(MXU_DTYPE), b.astype(MXU_DTYPE), (dims, ((), ())),
                           preferred_element_type=F32)


def _dot_nn(a, b):
    return _dot(a, b, ((1,), (0,)))


def _dot_nt(a, b):
    return _dot(a, b, ((1,), (1,)))


def _dot_tn(a, b):
    return _dot(a, b, ((0,), (0,)))


def _rows(ts, w, cb=0):
    return pl.BlockSpec((ts, w), lambda i: (i, cb))


def _vec(w, cb=0, r=1):
    return pl.BlockSpec((r, w), lambda i: (0, cb))


def _prev_halo(ts, hr, w, cb):
    return pl.BlockSpec((hr, w), lambda i: (jnp.maximum(i * (ts // hr) - 1, 0), cb))


def _next_halo(ts, hr, w, cb, s):
    return pl.BlockSpec((hr, w), lambda i: (jnp.minimum((i + 1) * (ts // hr), s // hr - 1), cb))


def _sds(shape, dtype=F32):
    return jax.ShapeDtypeStruct(shape, dtype)


def _mm(a, b, mode, out_dtype, tm, tn, name):
    tm = min(tm, a.shape[1] if mode == "tn" else a.shape[0])
    tn = min(tn, b.shape[0] if mode == "nt" else b.shape[1])
    if mode == "nn":
        (m, k), n = a.shape, b.shape[1]
        a_spec = pl.BlockSpec((tm, k), lambda i, j: (i, 0))
        b_spec = pl.BlockSpec((k, tn), lambda i, j: (0, j))
        dot = _dot_nn
    elif mode == "nt":
        (m, k), n = a.shape, b.shape[0]
        a_spec = pl.BlockSpec((tm, k), lambda i, j: (i, 0))
        b_spec = pl.BlockSpec((tn, k), lambda i, j: (j, 0))
        dot = _dot_nt
    else:
        (k, m), n = a.shape, b.shape[1]
        a_spec = pl.BlockSpec((k, tm), lambda i, j: (0, i))
        b_spec = pl.BlockSpec((k, tn), lambda i, j: (0, j))
        dot = _dot_tn
    assert m % tm == 0 and n % tn == 0, (name, m, n, tm, tn)

    def body(a_ref, b_ref, o_ref):
        o_ref[...] = dot(a_ref[...], b_ref[...]).astype(o_ref.dtype)

    return pl.pallas_call(
        body, name=name, grid=(m // tm, n // tn), in_specs=[a_spec, b_spec],
        out_specs=pl.BlockSpec((tm, tn), lambda i, j: (i, j)), out_shape=_sds((m, n), out_dtype),
        compiler_params=_cp(2))(a, b)


def _mm_tn_shards(a, b, by, name):
    k, m = a.shape
    n = b.shape[1]
    if by == "cols":
        tm, tn = 512, n // N_CHIPS
        shape, grid = (N_CHIPS, m, tn), (m // tm, N_CHIPS)
        out_spec = pl.BlockSpec((1, tm, tn), lambda i, j: (j, i, 0))
    else:
        tm, tn = m // N_CHIPS, 512
        shape, grid = (N_CHIPS, tm, n), (N_CHIPS, n // tn)
        out_spec = pl.BlockSpec((1, tm, tn), lambda i, j: (i, 0, j))

    def body(a_ref, b_ref, o_ref):
        o_ref[0] = _dot_tn(a_ref[...], b_ref[...])

    return pl.pallas_call(
        body, name=name, grid=grid,
        in_specs=[pl.BlockSpec((k, tm), lambda i, j: (0, i)), pl.BlockSpec((k, tn), lambda i, j: (0, j))],
        out_specs=out_spec, out_shape=_sds(shape), compiler_params=_cp(2))(a, b)


def _even_col(q):
    return q if q < 2432 else (q + 64 if q < 2464 else q + 96)


def _shard_pieces(j):
    lo, hi = EVEN_SHARD * j, EVEN_SHARD * (j + 1)
    cuts = [lo] + [b for b in (2432, 2464) if lo < b < hi] + [hi]
    return [(a - lo, _even_col(a), b - a) for a, b in zip(cuts[:-1], cuts[1:])]


def _ein_from_shards(w):
    _, d, _ = w.shape
    tr = 256

    def body(w_ref, o_ref):
        parts, at = [], 0
        for j in range(N_CHIPS):
            for d0, s0, n in _shard_pieces(j):
                if s0 > at:
                    parts.append(jnp.zeros((tr, s0 - at), F32))
                parts.append(w_ref[j, :, d0:d0 + n].astype(F32))
                at = s0 + n
        o_ref[...] = jnp.concatenate(parts, axis=1).astype(o_ref.dtype)

    return pl.pallas_call(
        body, name="ein_from_shards", grid=(d // tr,),
        in_specs=[pl.BlockSpec((N_CHIPS, tr, EVEN_SHARD), lambda i: (0, i, 0))],
        out_specs=_rows(tr, EVEN_PAD), out_shape=_sds((d, EVEN_PAD), w.dtype), compiler_params=_cp(1))(w)


def _ein_to_shards(dw):
    d = dw.shape[0]
    tr = 256

    def body(dw_ref, o_ref):
        for j in range(N_CHIPS):
            parts = [dw_ref[:, s0:s0 + n] for _, s0, n in _shard_pieces(j)]
            o_ref[j] = jnp.concatenate(parts + [jnp.zeros((tr, EVEN_SHARD_PAD - EVEN_SHARD), F32)], axis=1)

    return pl.pallas_call(
        body, name="ein_to_shards", grid=(d // tr,), in_specs=[_rows(tr, EVEN_PAD)],
        out_specs=pl.BlockSpec((N_CHIPS, tr, EVEN_SHARD_PAD), lambda i: (0, i, 0)),
        out_shape=_sds((N_CHIPS, d, EVEN_SHARD_PAD)), compiler_params=_cp(1))(dw)


def _rope_tables(pos_col, invf):
    s = pos_col.shape[0]
    ts = min(512, s)

    def body(p_ref, f_ref, c_ref, s_ref):
        ang = p_ref[...].astype(F32) * f_ref[...]
        lane = lax.broadcasted_iota(jnp.int32, ang.shape, 1)
        rope = (lane >= QK_NOPE) & (lane < QK_NOPE + QK_ROPE)
        c_ref[...] = jnp.where(lane < QK_NOPE, 1.0, jnp.where(rope, jnp.cos(ang), 0.0))
        s_ref[...] = jnp.where(rope, jnp.sin(ang), 0.0)

    return pl.pallas_call(
        body, name="rope_tables", grid=(s // ts,), in_specs=[_rows(ts, 1), _vec(HEAD_PAD)],
        out_specs=[_rows(ts, HEAD_PAD)] * 2, out_shape=[_sds((s, HEAD_PAD))] * 2,
        compiler_params=_cp(1))(pos_col, invf)


def _after(dep):
    return () if dep is None else (dep,)


def _pre_fwd(x, g, mod_l, ts, dep=None):
    s, d = x.shape

    def body(x_ref, g_ref, sh_ref, sc_ref, *rest):
        h = _rms(x_ref[...], g_ref[...]) * (1.0 + sc_ref[...]) + sh_ref[...]
        rest[-1][...] = h.astype(rest[-1].dtype)

    return pl.pallas_call(
        body, name="pre_fwd", grid=(s // ts,),
        in_specs=[_rows(ts, d), _vec(d), _vec(d, 0), _vec(d, 1)] + [_HBM_SPEC] * len(_after(dep)),
        out_specs=_rows(ts, d), out_shape=_sds((s, d), MXU_DTYPE), compiler_params=_cp(1))(
            x, g, mod_l, mod_l, *_after(dep))


def _pre_bwd(dh, dx_out, x, g, mod_l, ts):
    s, d = x.shape

    def f(xv, gv, sh, sc):
        return _rms(xv, gv) * (1.0 + sc) + sh

    def body(dh_ref, dxo_ref, x_ref, g_ref, sh_ref, sc_ref, dx_ref, dsh_ref, dsc_ref, dg_ref):
        i = pl.program_id(0)
        _, vjp = jax.vjp(f, x_ref[...], g_ref[...], sh_ref[...], sc_ref[...])
        dx, dg, dsh, dsc = vjp(dh_ref[...])
        dx_ref[...] = dxo_ref[...] + dx

        @pl.when(i == 0)
        def _():
            dsh_ref[...] = jnp.zeros_like(dsh_ref)
            dsc_ref[...] = jnp.zeros_like(dsc_ref)
            dg_ref[...] = jnp.zeros_like(dg_ref)

        dsh_ref[...] += dsh
        dsc_ref[...] += dsc
        dg_ref[...] += dg

    return pl.pallas_call(
        body, name="pre_bwd", grid=(s // ts,),
        in_specs=[_rows(ts, d), _rows(ts, d), _rows(ts, d), _vec(d), _vec(d, 0), _vec(d, 1)],
        out_specs=[_rows(ts, d), _vec(d), _vec(d), _vec(d)],
        out_shape=[_sds((s, d)), _sds((1, d)), _sds((1, d)), _sds((1, d))],
        compiler_params=_cp(1))(dh, dx_out, x, g, mod_l, mod_l)


def _post_fwd(x, yo, g, mod_l, ts):
    s, d = x.shape

    def body(x_ref, yo_ref, g_ref, gate_ref, o_ref):
        o_ref[...] = x_ref[...] + gate_ref[...] * _rms(yo_ref[...], g_ref[...])

    return pl.pallas_call(
        body, name="post_fwd", grid=(s // ts,),
        in_specs=[_rows(ts, d), _rows(ts, d), _vec(d), _vec(d, 2)],
        out_specs=_rows(ts, d), out_shape=_sds((s, d)), compiler_params=_cp(1))(x, yo, g, mod_l)


def _post_bwd(dx_out, yo, g, mod_l, ts, dep=None):
    s, d = yo.shape

    def f(yov, gv, gate):
        return gate * _rms(yov, gv)

    def body(dx_ref, yo_ref, g_ref, gate_ref, *rest):
        dyo_ref, dgate_ref, dg_ref = rest[-3:]
        i = pl.program_id(0)
        _, vjp = jax.vjp(f, yo_ref[...], g_ref[...], gate_ref[...])
        dyo, dg, dgate = vjp(dx_ref[...])
        dyo_ref[...] = dyo.astype(dyo_ref.dtype)

        @pl.when(i == 0)
        def _():
            dgate_ref[...] = jnp.zeros_like(dgate_ref)
            dg_ref[...] = jnp.zeros_like(dg_ref)

        dgate_ref[...] += dgate
        dg_ref[...] += dg

    return pl.pallas_call(
        body, name="post_bwd", grid=(s // ts,),
        in_specs=[_rows(ts, d), _rows(ts, d), _vec(d), _vec(d, 2)] + [_HBM_SPEC] * len(_after(dep)),
        out_specs=[_rows(ts, d), _vec(d), _vec(d)],
        out_shape=[_sds((s, d), MXU_DTYPE), _sds((1, d)), _sds((1, d))],
        compiler_params=_cp(1))(dx_out, yo, g, mod_l, *_after(dep))


def _loss_fwd_bwd(x, target, ts):
    s, d = x.shape

    def body(x_ref, t_ref, loss_ref, dx_ref):
        i = pl.program_id(0)
        err = x_ref[...] - t_ref[...]
        dx_ref[...] = err * (1.0 / d)

        @pl.when(i == 0)
        def _():
            loss_ref[...] = jnp.zeros_like(loss_ref)

        loss_ref[...] += 0.5 * jnp.sum(jnp.sum(err * err, axis=-1, keepdims=True) * (1.0 / d), axis=0, keepdims=True)

    return pl.pallas_call(
        body, name="loss", grid=(s // ts,), in_specs=[_rows(ts, d), _rows(ts, d)],
        out_specs=[_vec(1), _rows(ts, d)], out_shape=[_sds((1, 1)), _sds((s, d))],
        compiler_params=_cp(1))(x, target)


def _rope(t, cos, sin):
    lane = lax.broadcasted_iota(jnp.int32, t.shape, 1)
    first = (lane >= QK_NOPE) & (lane < QK_NOPE + QK_ROPE // 2)
    second = (lane >= QK_NOPE + QK_ROPE // 2) & (lane < QK_NOPE + QK_ROPE)
    up = pltpu.roll(t, QK_ROPE // 2, 1)
    down = pltpu.roll(t, HEAD_PAD - QK_ROPE // 2, 1)
    return t * cos + jnp.where(first, -down, jnp.where(second, up, 0.0)) * sin


def _rope_transposed(g, cos, sin):
    lane = lax.broadcasted_iota(jnp.int32, g.shape, 1)
    first = (lane >= QK_NOPE) & (lane < QK_NOPE + QK_ROPE // 2)
    second = (lane >= QK_NOPE + QK_ROPE // 2) & (lane < QK_NOPE + QK_ROPE)
    u = g * sin
    up = pltpu.roll(u, QK_ROPE // 2, 1)
    down = pltpu.roll(u, HEAD_PAD - QK_ROPE // 2, 1)
    return g * cos + jnp.where(first, down, jnp.where(second, -up, 0.0))


def _mla_prep_fwd(z, cos, sin, qg, kvg, wq, wuk, wuv, ts):
    s = z.shape[0]

    def body(cq_ref, ckv_ref, kr_ref, cos_ref, sin_ref, qg_ref, kvg_ref, wq_ref, wuk_ref, wuv_ref,
             q_ref, k_ref, v_ref):
        cos_v, sin_v = cos_ref[...], sin_ref[...]
        cqn = _rms(cq_ref[...], qg_ref[...])
        q_ref[0] = _rope(_dot_nn(cqn, wq_ref[0]), cos_v, sin_v).astype(q_ref.dtype)
        ckvn = _rms(ckv_ref[...], kvg_ref[...])
        k_ref[0] = (_dot_nn(ckvn, wuk_ref[0]) + _rope(kr_ref[...], cos_v, sin_v)).astype(k_ref.dtype)
        v_ref[0] = _dot_nn(ckvn, wuv_ref[0]).astype(v_ref.dtype)

    row = lambda w, cb: pl.BlockSpec((ts, w), lambda i, h: (i, cb))
    vec = lambda w: pl.BlockSpec((1, w), lambda i, h: (0, 0))
    wsp = lambda k: pl.BlockSpec((1, k, HEAD_PAD), lambda i, h: (h, 0, 0))
    out = pl.BlockSpec((1, ts, HEAD_PAD), lambda i, h: (h, i, 0))
    return pl.pallas_call(
        body, name="mla_prep_fwd", grid=(s // ts, HEADS),
        in_specs=[row(Q_LORA, 8), row(KV_LORA, 18), row(HEAD_PAD, 19), row(HEAD_PAD, 0), row(HEAD_PAD, 0),
                  vec(Q_LORA), vec(KV_LORA), wsp(Q_LORA), wsp(KV_LORA), wsp(KV_LORA)],
        out_specs=[out] * 3, out_shape=[_sds((HEADS, s, HEAD_PAD), MXU_DTYPE)] * 3,
        compiler_params=_cp(2))(z, z, z, cos, sin, qg, kvg, wq, wuk, wuv)


def _mla_prep_bwd(dz, dq, dk, dv, z, cos, sin, qg, kvg, wq, wuk, wuv, ts):
    s = z.shape[0]

    def fq(cq, g):
        return _rms(cq, g)

    def body(dz_in_ref, dq_ref, dk_ref, dv_ref, cq_ref, ckv_ref, cos_ref, sin_ref, qg_ref, kvg_ref, wq_ref, wuk_ref,
             wuv_ref, dz_ref, dw_ref, dqg_ref, dkvg_ref, dcqn_acc, dckvn_acc, dkr_acc):
        del dz_in_ref
        i, h = pl.program_id(0), pl.program_id(1)
        cos_v, sin_v = cos_ref[...], sin_ref[...]
        row0 = pl.multiple_of((h % 2) * MLA_ROWS, MLA_ROWS)
        dwq_ref = dw_ref.at[h // 2, pl.ds(row0, Q_LORA)]
        dwuk_ref = dw_ref.at[h // 2, pl.ds(row0 + Q_LORA, KV_LORA)]
        dwuv_ref = dw_ref.at[h // 2, pl.ds(row0 + Q_LORA + KV_LORA, KV_LORA)]

        @pl.when((i == 0) & (h == 0))
        def _():
            dw_ref[...] = jnp.zeros_like(dw_ref)
            dqg_ref[...] = jnp.zeros_like(dqg_ref)
            dkvg_ref[...] = jnp.zeros_like(dkvg_ref)

        @pl.when(h == 0)
        def _():
            dcqn_acc[...] = jnp.zeros_like(dcqn_acc)
            dckvn_acc[...] = jnp.zeros_like(dckvn_acc)
            dkr_acc[...] = jnp.zeros_like(dkr_acc)

        cqn = _rms(cq_ref[...], qg_ref[...])
        ckvn = _rms(ckv_ref[...], kvg_ref[...])
        dq_lin = _rope_transposed(dq_ref[0], cos_v, sin_v)
        dcqn_acc[...] += _dot_nt(dq_lin, wq_ref[0])
        dwq_ref[...] += _dot_tn(cqn, dq_lin)
        dkh, dvh = dk_ref[0], dv_ref[0]
        lane = lax.broadcasted_iota(jnp.int32, dkh.shape, 1)
        dkr_acc[...] += jnp.where((lane >= QK_NOPE) & (lane < QK_NOPE + QK_ROPE), dkh, 0.0)
        dckvn_acc[...] += _dot_nt(dkh, wuk_ref[0]) + _dot_nt(dvh, wuv_ref[0])
        dwuk_ref[...] += _dot_tn(ckvn, dkh)
        dwuv_ref[...] += _dot_tn(ckvn, dvh)

        @pl.when(h == HEADS - 1)
        def _():
            _, vjp_q = jax.vjp(fq, cq_ref[...], qg_ref[...])
            dcq, dqg = vjp_q(dcqn_acc[...])
            _, vjp_kv = jax.vjp(fq, ckv_ref[...], kvg_ref[...])
            dckv, dkvg = vjp_kv(dckvn_acc[...])
            dz_ref[:, 0:Q_LORA] = dcq
            dz_ref[:, Q_LORA:Q_LORA + KV_LORA] = dckv
            dz_ref[:, Q_LORA + KV_LORA:] = _rope_transposed(dkr_acc[...], cos_v, sin_v)
            dqg_ref[...] += dqg
            dkvg_ref[...] += dkvg

    row = lambda w, cb: pl.BlockSpec((ts, w), lambda i, h: (i, cb))
    vec = lambda w: pl.BlockSpec((1, w), lambda i, h: (0, 0))
    wsp = lambda k: pl.BlockSpec((1, k, HEAD_PAD), lambda i, h: (h, 0, 0))
    hrow = pl.BlockSpec((1, ts, HEAD_PAD), lambda i, h: (h, i, 0))
    whole = pl.BlockSpec((N_CHIPS, 2 * MLA_ROWS, HEAD_PAD), lambda i, h: (0, 0, 0))
    return pl.pallas_call(
        body, name="mla_prep_bwd", grid=(s // ts, HEADS),
        in_specs=[_HBM_SPEC, hrow, hrow, hrow, row(Q_LORA, 8), row(KV_LORA, 18),
                  row(HEAD_PAD, 0), row(HEAD_PAD, 0), vec(Q_LORA), vec(KV_LORA), wsp(Q_LORA), wsp(KV_LORA), wsp(KV_LORA)],
        out_specs=[row(512, 4), whole, vec(Q_LORA), vec(KV_LORA)],
        out_shape=[_sds(dz.shape), _sds((N_CHIPS, 2 * MLA_ROWS, HEAD_PAD)), _sds((1, Q_LORA)), _sds((1, KV_LORA))],
        scratch_shapes=[pltpu.VMEM((ts, Q_LORA), F32), pltpu.VMEM((ts, KV_LORA), F32), pltpu.VMEM((ts, HEAD_PAD), F32)],
        input_output_aliases={0: 0}, compiler_params=_cp(2))(dz, dq, dk, dv, z, z, cos, sin, qg, kvg, wq, wuk, wuv)


def _chunk_mask(q0, k0, tq, tk):
    rows = q0 + lax.broadcasted_iota(jnp.int32, (tq, tk), 0)
    cols = k0 + lax.broadcasted_iota(jnp.int32, (tq, tk), 1)
    return lax.shift_right_logical(cols, 6) <= lax.shift_right_logical(rows, 6)


def _attn_fwd(q, k, v, tq):
    s = q.shape[1]
    nq = s // tq
    scale = 1.0 / float(QK_NOPE + QK_ROPE) ** 0.5

    def body(q_ref, k_ref, v_ref, o_ref, lse_ref):
        qi, hh = pl.program_id(1), pl.program_id(2)
        qv = q_ref[0]

        def step(kj, carry, masked):
            m, l, acc = carry
            k0 = pl.multiple_of(kj * tq, tq)
            sc = _dot_nt(qv, k_ref[0, pl.ds(k0, tq), :]) * scale
            if masked:
                sc = jnp.where(_chunk_mask(qi * tq, k0, tq, tq), sc, NEG)
            m_new = jnp.maximum(m, jnp.max(sc, axis=-1, keepdims=True))
            alpha = jnp.exp(m - m_new)
            p = jnp.exp(sc - m_new)
            l = alpha * l + jnp.sum(p, axis=-1, keepdims=True)
            acc = alpha * acc + _dot_nn(p, v_ref[0, pl.ds(k0, tq), :])
            return m_new, l, acc

        init = (jnp.full((tq, 1), NEG, F32), jnp.zeros((tq, 1), F32), jnp.zeros((tq, HEAD_PAD), F32))
        carry = lax.fori_loop(0, qi, lambda kj, c: step(kj, c, False), init)
        m, l, acc = step(qi, carry, True)
        o = acc / l
        lse_ref[0] = m + jnp.log(l)

        @pl.when(hh == 0)
        def _():
            o_ref[...] = o

        @pl.when(hh == 1)
        def _():
            o_ref[...] += o

    head = lambda hp, qi, hh: 2 * hp + hh
    return pl.pallas_call(
        body, name="attn_fwd", grid=(HEADS // 2, nq, 2),
        in_specs=[pl.BlockSpec((1, tq, HEAD_PAD), lambda hp, qi, hh: (head(hp, qi, hh), qi, 0)),
                  pl.BlockSpec((1, s, HEAD_PAD), lambda hp, qi, hh: (head(hp, qi, hh), 0, 0)),
                  pl.BlockSpec((1, s, HEAD_PAD), lambda hp, qi, hh: (head(hp, qi, hh), 0, 0))],
        out_specs=[pl.BlockSpec((tq, HEAD_PAD), lambda hp, qi, hh: (qi, hp)),
                   pl.BlockSpec((1, tq, 1), lambda hp, qi, hh: (head(hp, qi, hh), qi, 0))],
        out_shape=[_sds((s, HEADS * V_HEAD)), _sds((HEADS, s, 1))],
        compiler_params=_cp(3))(q, k, v)


def _attn_bwd(q, k, v, do, o, lse, tq):
    s = q.shape[1]
    nq = s // tq
    scale = 1.0 / float(QK_NOPE + QK_ROPE) ** 0.5

    def body(q_ref, k_ref, v_ref, do_ref, o_ref, lse_ref, dq_ref, dk_ref, dv_ref):
        hh, kj = pl.program_id(1), pl.program_id(2)

        @pl.when(kj == 0)
        def _():
            dq_ref[...] = jnp.zeros_like(dq_ref)

        kv, vv = k_ref[0], v_ref[0]
        lane = lax.broadcasted_iota(jnp.int32, (tq, HEAD_PAD), 1)
        mine = lax.shift_right_logical(lane, 6) == hh

        def step(qi, carry, masked):
            dk_acc, dv_acc = carry
            q0 = pl.multiple_of(qi * tq, tq)
            qv = q_ref[0, pl.ds(q0, tq), :]
            dov = do_ref[pl.ds(q0, tq), :]
            delta = jnp.sum(jnp.where(mine, dov * o_ref[pl.ds(q0, tq), :], 0.0), axis=-1, keepdims=True)
            sc = _dot_nt(qv, kv) * scale
            if masked:
                sc = jnp.where(_chunk_mask(q0, kj * tq, tq, tq), sc, NEG)
            p = jnp.exp(sc - lse_ref[0, pl.ds(q0, tq), :])
            do_b = dov.astype(MXU_DTYPE)
            ds = (p * (_dot_nt(do_b, vv) - delta) * scale).astype(MXU_DTYPE)
            dv_acc = dv_acc + _dot_tn(p, do_b)
            dk_acc = dk_acc + _dot_tn(ds, qv)
            dq_ref[0, pl.ds(q0, tq), :] += _dot_nn(ds, kv)
            return dk_acc, dv_acc

        zero = jnp.zeros((tq, HEAD_PAD), F32)
        carry = step(kj, (zero, zero), True)
        dk_acc, dv_acc = lax.fori_loop(kj + 1, nq, lambda qi, c: step(qi, c, False), carry)
        dk_ref[0] = dk_acc
        dv_ref[0] = dv_acc

    head = lambda hp, hh, kj: 2 * hp + hh
    full = pl.BlockSpec((1, s, HEAD_PAD), lambda hp, hh, kj: (head(hp, hh, kj), 0, 0))
    blk = pl.BlockSpec((1, tq, HEAD_PAD), lambda hp, hh, kj: (head(hp, hh, kj), kj, 0))
    pair = pl.BlockSpec((s, HEAD_PAD), lambda hp, hh, kj: (0, hp))
    return pl.pallas_call(
        body, name="attn_bwd", grid=(HEADS // 2, 2, nq),
        in_specs=[full, blk, blk, pair, pair, pl.BlockSpec((1, s, 1), lambda hp, hh, kj: (head(hp, hh, kj), 0, 0))],
        out_specs=[full, blk, blk], out_shape=[_sds((HEADS, s, HEAD_PAD))] * 3,
        compiler_params=_cp(3))(q, k, v, do, o, lse)


def _sc_conv(u, ubuf, w_ref, b_ref, ts):
    return (w_ref[2:3, :] * u + w_ref[1:2, :] * ubuf[pl.ds(SC_HALO - 1, ts), :]
            + w_ref[0:1, :] * ubuf[pl.ds(SC_HALO - 2, ts), :] + b_ref[...])


def _even_gate_fwd(z, o, sc_w, sc_b, ts):
    s = z.shape[0]
    w = SC_WIDTH

    def body(ab_ref, ac_ref, ax_ref, ag_ref, bg_ref, hc_ref, hx_ref, o_ref, w_ref, b_ref, y_ref, ubuf):
        i = pl.program_id(0)
        u = ac_ref[...] * ax_ref[...]
        ubuf[0:SC_HALO, :] = jnp.where(i > 0, hc_ref[...] * hx_ref[...], 0.0)
        ubuf[SC_HALO:, :] = u
        conv = _sc_conv(u, ubuf, w_ref, b_ref, ts)
        y_ref[:, 0:w] = (ab_ref[...] * conv * _silu(ag_ref[...])).astype(y_ref.dtype)
        y_ref[:, w:] = (o_ref[...] * _silu(bg_ref[...])).astype(y_ref.dtype)

    return pl.pallas_call(
        body, name="even_gate_fwd", grid=(s // ts,),
        in_specs=[_rows(ts, w, 0), _rows(ts, w, 1), _rows(ts, w, 2), _rows(ts, w, 3), _rows(ts, w, 5),
                  _prev_halo(ts, SC_HALO, w, 1), _prev_halo(ts, SC_HALO, w, 2), _rows(ts, w),
                  _vec(w, 0, SC_KERNEL), _vec(w)],
        out_specs=_rows(ts, 2 * w), out_shape=_sds((s, 2 * w), MXU_DTYPE),
        scratch_shapes=[pltpu.VMEM((ts + SC_HALO, w), F32)],
        compiler_params=_cp(1))(z, z, z, z, z, z, z, o, sc_w, sc_b)


def _even_gate_bwd(dy, z, o, sc_w, sc_b, ts):
    s = z.shape[0]
    w = SC_WIDTH
    n = s // ts

    def body(dya_ref, dyb_ref, dyan_ref, ab_ref, ac_ref, ax_ref, ag_ref, bg_ref, hc_ref, hx_ref, abn_ref, agn_ref,
             o_ref, w_ref, b_ref, dz_ref, do_ref, dw_ref, db_ref, ubuf, dbuf):
        i = pl.program_id(0)
        ab, ac, ax, ag, bg = ab_ref[...], ac_ref[...], ax_ref[...], ag_ref[...], bg_ref[...]
        dya, dyb = dya_ref[...], dyb_ref[...]
        u = ac * ax
        ubuf[0:SC_HALO, :] = jnp.where(i > 0, hc_ref[...] * hx_ref[...], 0.0)
        ubuf[SC_HALO:, :] = u
        conv = _sc_conv(u, ubuf, w_ref, b_ref, ts)
        sg = _silu(ag)
        dconv = dya * ab * sg
        dbuf[0:ts, :] = dconv
        dbuf[ts:, :] = jnp.where(i < n - 1, dyan_ref[...] * abn_ref[...] * _silu(agn_ref[...]), 0.0)
        du = w_ref[2:3, :] * dconv + w_ref[1:2, :] * dbuf[pl.ds(1, ts), :] + w_ref[0:1, :] * dbuf[pl.ds(2, ts), :]
        dz_ref[:, 0:w] = dya * conv * sg
        dz_ref[:, w:2 * w] = du * ax
        dz_ref[:, 2 * w:3 * w] = du * ac
        dz_ref[:, 3 * w:4 * w] = dya * ab * conv * _dsilu(ag)
        dz_ref[:, 4 * w:5 * w] = jnp.zeros((ts, w), F32)
        dz_ref[:, 5 * w:] = dyb * o_ref[...] * _dsilu(bg)
        do_ref[...] = dyb * _silu(bg)

        @pl.when(i == 0)
        def _():
            dw_ref[...] = jnp.zeros_like(dw_ref)
            db_ref[...] = jnp.zeros_like(db_ref)

        dw_ref[0:1, :] += jnp.sum(dconv * ubuf[pl.ds(SC_HALO - 2, ts), :], axis=0, keepdims=True)
        dw_ref[1:2, :] += jnp.sum(dconv * ubuf[pl.ds(SC_HALO - 1, ts), :], axis=0, keepdims=True)
        dw_ref[2:3, :] += jnp.sum(dconv * u, axis=0, keepdims=True)
        db_ref[...] += jnp.sum(dconv, axis=0, keepdims=True)

    return pl.pallas_call(
        body, name="even_gate_bwd", grid=(n,),
        in_specs=[_rows(ts, w, 0), _rows(ts, w, 1), _next_halo(ts, SC_HALO, w, 0, s),
                  _rows(ts, w, 0), _rows(ts, w, 1), _rows(ts, w, 2), _rows(ts, w, 3), _rows(ts, w, 5),
                  _prev_halo(ts, SC_HALO, w, 1), _prev_halo(ts, SC_HALO, w, 2),
                  _next_halo(ts, SC_HALO, w, 0, s), _next_halo(ts, SC_HALO, w, 3, s),
                  _rows(ts, w), _vec(w, 0, SC_KERNEL), _vec(w)],
        out_specs=[_rows(ts, EVEN_PAD), _rows(ts, w), _vec(w, 0, SC_KERNEL), _vec(w)],
        out_shape=[_sds((s, EVEN_PAD)), _sds((s, w)), _sds((SC_KERNEL, w)), _sds((1, w))],
        scratch_shapes=[pltpu.VMEM((ts + SC_HALO, w), F32), pltpu.VMEM((ts + SC_HALO, w), F32)],
        compiler_params=_cp(1))(dy, dy, dy, z, z, z, z, z, z, z, z, z, o, sc_w, sc_b)


def _ln_act(uc, sg, g, b):
    mu = jnp.mean(uc, axis=-1, keepdims=True)
    var = jnp.mean(jnp.square(uc - mu), axis=-1, keepdims=True)
    return _silu((uc - mu) * lax.rsqrt(var + EPS) * g + b) * _silu(sg)


def _odd_fwd(z, conv_w, conv_b, ln_g, ln_b, ts):
    s = z.shape[0]
    d = D_MODEL
    k = CONF_KERNEL

    def body(val_ref, glu_ref, sg_ref, hval_ref, hglu_ref, w_ref, b_ref, g_ref, beta_ref, y_ref, uc_ref, ubuf):
        i = pl.program_id(0)
        ubuf[0:CONF_HALO, :] = jnp.where(i > 0, hval_ref[...] * _sigmoid(hglu_ref[...]), 0.0)
        ubuf[CONF_HALO:, :] = val_ref[...] * _sigmoid(glu_ref[...])
        for r0 in range(0, ts, CONV_ROWS):
            acc = jnp.broadcast_to(b_ref[...], (CONV_ROWS, d))
            for j in range(k):
                acc = acc + w_ref[j:j + 1, :] * ubuf[pl.ds(r0 + CONF_HALO - (k - 1) + j, CONV_ROWS), :]
            uc_ref[r0:r0 + CONV_ROWS, :] = acc
        y_ref[...] = _ln_act(uc_ref[...], sg_ref[...], g_ref[...], beta_ref[...]).astype(y_ref.dtype)

    return pl.pallas_call(
        body, name="odd_fwd", grid=(s // ts,),
        in_specs=[_rows(ts, d, 0), _rows(ts, d, 1), _rows(ts, d, 2),
                  _prev_halo(ts, CONF_HALO, d, 0), _prev_halo(ts, CONF_HALO, d, 1),
                  _vec(d, 0, k), _vec(d), _vec(d), _vec(d)],
        out_specs=[_rows(ts, d), _rows(ts, d)], out_shape=[_sds((s, d), MXU_DTYPE), _sds((s, d))],
        scratch_shapes=[pltpu.VMEM((ts + CONF_HALO, d), F32)],
        compiler_params=_cp(1))(z, z, z, z, z, conv_w, conv_b, ln_g, ln_b)


def _odd_bwd(dy, z, uc, conv_w, ln_g, ln_b, ts):
    s = z.shape[0]
    d = D_MODEL
    k = CONF_KERNEL
    n = s // ts

    def body(dy_ref, dyn_ref, val_ref, glu_ref, sg_ref, sgn_ref, hval_ref, hglu_ref, uc_ref, ucn_ref,
             w_ref, g_ref, beta_ref, dz_ref, dw_ref, db_ref, dg_ref, dbeta_ref, ubuf, dbuf):
        i = pl.program_id(0)
        val, glu = val_ref[...], glu_ref[...]
        sig = _sigmoid(glu)
        ubuf[0:CONF_HALO, :] = jnp.where(i > 0, hval_ref[...] * _sigmoid(hglu_ref[...]), 0.0)
        ubuf[CONF_HALO:, :] = val * sig
        _, vjp = jax.vjp(_ln_act, uc_ref[...], sg_ref[...], g_ref[...], beta_ref[...])
        duc, dsg, dg, dbeta = vjp(dy_ref[...])
        _, vjp_n = jax.vjp(_ln_act, ucn_ref[...], sgn_ref[...], g_ref[...], beta_ref[...])
        dbuf[0:ts, :] = duc
        dbuf[ts:, :] = jnp.where(i < n - 1, vjp_n(dyn_ref[...])[0], 0.0)
        dz_ref[:, 2 * d:] = dsg

        @pl.when(i == 0)
        def _():
            dw_ref[...] = jnp.zeros_like(dw_ref)
            db_ref[...] = jnp.zeros_like(db_ref)
            dg_ref[...] = jnp.zeros_like(dg_ref)
            dbeta_ref[...] = jnp.zeros_like(dbeta_ref)

        db_ref[...] += jnp.sum(duc, axis=0, keepdims=True)
        dg_ref[...] += dg
        dbeta_ref[...] += dbeta
        for r0 in range(0, ts, CONV_ROWS):
            acc = jnp.zeros((CONV_ROWS, d), F32)
            for j in range(k):
                acc = acc + w_ref[j:j + 1, :] * dbuf[pl.ds(r0 + (k - 1) - j, CONV_ROWS), :]
            sig_r = sig[r0:r0 + CONV_ROWS, :]
            dz_ref[r0:r0 + CONV_ROWS, 0:d] = acc * sig_r
            dz_ref[r0:r0 + CONV_ROWS, d:2 * d] = acc * val[r0:r0 + CONV_ROWS, :] * sig_r * (1.0 - sig_r)
        for j in range(k):
            dw_ref[j:j + 1, :] += jnp.sum(duc * ubuf[pl.ds(CONF_HALO - (k - 1) + j, ts), :], axis=0, keepdims=True)

    return pl.pallas_call(
        body, name="odd_bwd", grid=(n,),
        in_specs=[_rows(ts, d), _next_halo(ts, CONF_HALO, d, 0, s),
                  _rows(ts, d, 0), _rows(ts, d, 1), _rows(ts, d, 2), _next_halo(ts, CONF_HALO, d, 2, s),
                  _prev_halo(ts, CONF_HALO, d, 0), _prev_halo(ts, CONF_HALO, d, 1),
                  _rows(ts, d), _next_halo(ts, CONF_HALO, d, 0, s),
                  _vec(d, 0, k), _vec(d), _vec(d)],
        out_specs=[_rows(ts, ODD_IN), _vec(d, 0, k), _vec(d), _vec(d), _vec(d)],
        out_shape=[_sds((s, ODD_IN)), _sds((k, d)), _sds((1, d)), _sds((1, d)), _sds((1, d))],
        scratch_shapes=[pltpu.VMEM((ts + CONF_HALO, d), F32), pltpu.VMEM((ts + CONF_HALO, d), F32)],
        compiler_params=_cp(1))(dy, dy, z, z, z, z, z, z, uc, uc, conv_w, ln_g, ln_b)


def _local_step(x, target, cos, sin, mod, p, layer_weights, fwd_dep=None, grads_done=None):
    s = x.shape[0]
    tsf, tsb = min(512, s // 2), min(256, s // 2)
    tq = min(512, s // 2)
    row1 = lambda a, i: a[i:i + 1]
    saved = []
    for layer in range(DEPTH):
        i = layer // 2
        mod_l = row1(mod, layer)
        wl = layer_weights(layer, x)
        h = _pre_fwd(x, row1(p["pre_norm_g"], layer), mod_l, tsf, fwd_dep if layer == 0 else None)
        if layer % 2 == 0:
            z = _mm(h, wl["w_in"], "nn", F32, 512, 1024, "even_in_fwd")
            q, k, v = _mla_prep_fwd(z, cos, sin, row1(p["even_q_norm_g"], i), row1(p["even_kv_norm_g"], i),
                                    wl["wq"], wl["wuk"], wl["wuv"], tsf)
            o, lse = _attn_fwd(q, k, v, tq)
            y = _even_gate_fwd(z, o, wl["sc_conv_w"], row1(p["even_sc_conv_b"], i), tsf)
            yo = _mm(y, wl["w_out"], "nn", F32, 512, 1024, "even_out_fwd")
            saved.append((x, h, z, y, yo, wl, (q, k, v, o, lse)))
        else:
            z = _mm(h, wl["w_in"], "nn", F32, 512, 1024, "odd_in_fwd")
            y, uc = _odd_fwd(z, wl["conv_w"], wl["conv_b"], wl["ln_g"], wl["ln_b"], tsf)
            yo = _mm(y, wl["w_out"], "nn", F32, 512, 1024, "odd_out_fwd")
            saved.append((x, h, z, y, yo, wl, uc))
        x = _post_fwd(x, yo, row1(p["post_norm_g"], layer), mod_l, tsf)

    loss, dx = _loss_fwd_bwd(x, target, tsf)

    g = {n: [None] * (DEPTH if n in ("pre_norm_g", "post_norm_g") else N_PAIRS) for n in (
        "pre_norm_g", "post_norm_g", "even_sc_conv_w", "even_sc_conv_b", "even_q_norm_g", "even_kv_norm_g",
        "odd_conv_w", "odd_conv_b", "odd_ln_g", "odd_ln_b")}
    dmod = [None] * DEPTH
    dep = None
    for layer in reversed(range(DEPTH)):
        i = layer // 2
        mod_l = row1(mod, layer)
        x_in, h, z, y, yo, wl, extra = saved[layer]
        dyo, dgate, g["post_norm_g"][layer] = _post_bwd(dx, yo, row1(p["post_norm_g"], layer), mod_l, tsb, dep)
        bufs = {}
        if layer % 2 == 0:
            q, k, v, o, lse = extra
            dy = _mm(dyo, wl["w_out"], "nt", F32, 512, 1024, "even_out_bwd_x")
            bufs["even_w_out"] = _mm_tn_shards(y, dyo, "rows", "even_out_bwd_w")
            dz, do, g["even_sc_conv_w"][i], g["even_sc_conv_b"][i] = _even_gate_bwd(
                dy, z, o, wl["sc_conv_w"], row1(p["even_sc_conv_b"], i), tsb)
            dq, dk, dv = _attn_bwd(q, k, v, do, o, lse, tq)
            dz, bufs["even_mla"], g["even_q_norm_g"][i], g["even_kv_norm_g"][i] = _mla_prep_bwd(
                dz, dq, dk, dv, z, cos, sin, row1(p["even_q_norm_g"], i), row1(p["even_kv_norm_g"], i),
                wl["wq"], wl["wuk"], wl["wuv"], tsb)
            dh = _mm(dz, wl["w_in"], "nt", F32, 256, 1024, "even_in_bwd_x")
            bufs["even_w_in"] = _ein_to_shards(_mm(h, dz, "tn", F32, 512, 512, "even_in_bwd_w"))
        else:
            uc = extra
            dy = _mm(dyo, wl["w_out"], "nt", F32, 512, 1024, "odd_out_bwd_x")
            bufs["odd_w_out"] = _mm_tn_shards(y, dyo, "rows", "odd_out_bwd_w")
            dz, g["odd_conv_w"][i], g["odd_conv_b"][i], g["odd_ln_g"][i], g["odd_ln_b"][i] = _odd_bwd(
                dy, z, uc, wl["conv_w"], wl["ln_g"], wl["ln_b"], tsb)
            dh = _mm(dz, wl["w_in"], "nt", F32, 256, 1024, "odd_in_bwd_x")
            bufs["odd_w_in"] = _mm_tn_shards(h, dz, "cols", "odd_in_bwd_w")
        dx, dshift, dscale, g["pre_norm_g"][layer] = _pre_bwd(dh, dx, x_in, row1(p["pre_norm_g"], layer), mod_l, tsb)
        dmod[layer] = jnp.concatenate([dshift, dscale, dgate], axis=-1)
        dep = grads_done(layer, bufs, dx) if grads_done is not None else None
    stack = lambda parts: jnp.stack([a[0] if a.shape[0] == 1 and a.ndim == 2 else a for a in parts])
    small = {n: stack(parts) for n, parts in g.items()}
    small["dmod"] = jnp.concatenate(dmod, axis=0)
    return loss, dx, small


def _uq_to_heads(w):
    w = w.reshape(N_CHIPS, Q_LORA, 2, QK_NOPE + QK_ROPE).transpose(0, 2, 1, 3).reshape(HEADS, Q_LORA, QK_NOPE + QK_ROPE)
    return jnp.pad(w, ((0, 0), (0, 0), (0, HEAD_PAD - QK_NOPE - QK_ROPE)))


def _ukv_to_heads(w):
    w = w.reshape(N_CHIPS, KV_LORA, 2, QK_NOPE + V_HEAD).transpose(0, 2, 1, 3).reshape(HEADS, KV_LORA, QK_NOPE + V_HEAD)
    wk = jnp.pad(w[..., :QK_NOPE], ((0, 0), (0, 0), (0, HEAD_PAD - QK_NOPE)))
    wv = w[..., QK_NOPE:]
    zero = jnp.zeros_like(wv)
    odd = (jnp.arange(HEADS) % 2 == 1)[:, None, None]
    wv = jnp.concatenate([jnp.where(odd, zero, wv), jnp.where(odd, wv, zero)], axis=-1)
    return wk, wv


def _mla_local(q):
    blocks = q.reshape(2, MLA_ROWS, HEAD_PAD)
    uq = jnp.concatenate([blocks[r, :Q_LORA, :QK_NOPE + QK_ROPE] for r in range(2)], axis=-1)
    ukv = jnp.concatenate(
        [jnp.concatenate([blocks[r, Q_LORA:Q_LORA + KV_LORA, :QK_NOPE],
                          blocks[r, Q_LORA + KV_LORA:, V_HEAD * r:V_HEAD * (r + 1)]], axis=-1) for r in range(2)], axis=-1)
    return uq, ukv


def _place():
    return lax.axis_index("x"), lax.axis_index("y"), lax.axis_index("c")


def _flip(v, bit):
    return 1 - v if bit else v


def _sem(a, k):
    return a * (N_CHIPS - 1) + k - 1


def _remote(src, dst, send_sem, recv_sem, peer):
    return pltpu.make_async_remote_copy(src_ref=src, dst_ref=dst, send_sem=send_sem, recv_sem=recv_sem,
                                        device_id=peer, device_id_type=MESH)


_VMEM_SPEC = pl.BlockSpec(memory_space=pltpu.VMEM)
_HBM_SPEC = pl.BlockSpec(memory_space=pl.ANY)


def _ada_fwd(c8, ada_w, ada_b_sh):
    depth, d, cols = ada_w.shape

    def body(c_ref, w_ref, b_ref, call_ref, mod_ref, s1, r1, s2, r2):
        x, y, c = _place()
        chip = 2 * x + y
        me = 2 * chip + c
        call_ref[me] = c_ref[...]
        sends = []
        for k in range(1, N_DEV):
            peer = (_flip(x, k & 4), _flip(y, k & 2), _flip(c, k & 1))
            cp = _remote(c_ref, call_ref.at[me], s1.at[k - 1], r1.at[k - 1], peer)
            cp.start()
            sends.append(cp)
        for k in range(1, N_DEV):
            src = 4 * _flip(x, k & 4) + 2 * _flip(y, k & 2) + _flip(c, k & 1)
            _remote(c_ref, call_ref.at[src], s1.at[k - 1], r1.at[k - 1], (x, y, c)).wait_recv()
        act = _silu(call_ref[...]).reshape(N_DEV * 8, d)
        for l in range(depth):
            mod_ref[chip, l] = _dot_nn(act, w_ref[l]) + b_ref[l:l + 1, :]
        for k in range(1, N_CHIPS):
            peer = (_flip(x, k & 2), _flip(y, k & 1), c)
            cp = _remote(mod_ref.at[chip], mod_ref.at[chip], s2.at[k - 1], r2.at[k - 1], peer)
            cp.start()
            sends.append(cp)
        for k in range(1, N_CHIPS):
            src = 2 * _flip(x, k & 2) + _flip(y, k & 1)
            _remote(mod_ref.at[src], mod_ref.at[src], s2.at[k - 1], r2.at[k - 1], (x, y, c)).wait_recv()
        for cp in sends:
            cp.wait_send()

    return pl.pallas_call(
        body, name="ada_fwd", in_specs=[_VMEM_SPEC] * 3, out_specs=[_VMEM_SPEC] * 2,
        out_shape=[_sds((N_DEV, 8, d)), _sds((N_CHIPS, depth, N_DEV * 8, cols))],
        scratch_shapes=[pltpu.SemaphoreType.DMA((N_DEV - 1,)), pltpu.SemaphoreType.DMA((N_DEV - 1,)),
                        pltpu.SemaphoreType.DMA((N_CHIPS - 1,)), pltpu.SemaphoreType.DMA((N_CHIPS - 1,))],
        compiler_params=pltpu.CompilerParams(vmem_limit_bytes=VMEM_LIMIT_V7X))(c8, ada_w, ada_b_sh)


def _ada_bwd(c_t, dmod_sh):
    depth, n, cols = dmod_sh.shape
    d = c_t.shape[0]
    tr = 256

    def body(c_ref, dm_ref, o_ref):
        act = _silu(c_ref[...])
        acc = act[:, 0:1] * dm_ref[0, 0:1, :]
        for e in range(1, n):
            acc = acc + act[:, e:e + 1] * dm_ref[0, e:e + 1, :]
        o_ref[0] = acc

    return pl.pallas_call(
        body, name="ada_bwd", grid=(depth, d // tr),
        in_specs=[pl.BlockSpec((tr, n), lambda l, i: (i, 0)), pl.BlockSpec((1, n, cols), lambda l, i: (l, 0, 0))],
        out_specs=pl.BlockSpec((1, tr, cols), lambda l, i: (l, i, 0)), out_shape=_sds((depth, d, cols)),
        compiler_params=_cp(2))(c_t, dmod_sh)


def _gathered_shape(shape, how):
    if how == "slot":
        return (N_CHIPS,) + shape
    r, cc = shape
    return (r, N_CHIPS * cc) if how == "cols" else (N_CHIPS * r, cc)


def _gathered_part(ref, shape, how, chip):
    if how == "slot":
        return ref.at[chip]
    if how == "cols":
        return ref.at[:, pl.ds(pl.multiple_of(chip * shape[1], 128), shape[1])]
    return ref.at[pl.ds(pl.multiple_of(chip * shape[0], 8), shape[0]), :]


_SEM_SPEC = pl.BlockSpec(memory_space=pltpu.SEMAPHORE)
_TOKEN = jax.ShapeDtypeStruct((8, 128), F32)
_SPLIT_COPY = pltpu.CompilerParams(has_side_effects=pltpu.SideEffectType.DATAFLOW_SIDE_EFFECTING)


def _in_hbm(a):
    return pltpu.with_memory_space_constraint(a, pltpu.HBM)


def _place_own(items):
    n = len(items)

    def body(*refs):
        ins, outs, sems = refs[:n], refs[n:2 * n], refs[2 * n]
        x, y, _ = _place()
        copies = [pltpu.make_async_copy(ins[a], _gathered_part(outs[a], items[a][0].shape, items[a][1], 2 * x + y),
                                        sems.at[a]) for a in range(n)]
        for cp in copies:
            cp.start()
        for cp in copies:
            cp.wait()

    return pl.pallas_call(
        body, name="place_own", in_specs=[_HBM_SPEC] * n, out_specs=[_HBM_SPEC] * n,
        out_shape=[_sds(_gathered_shape(a.shape, how), a.dtype) for a, how in items],
        scratch_shapes=[pltpu.SemaphoreType.DMA((n,))])(*[a for a, _ in items])


def _gather_start(items, gathered, name):
    n = len(items)

    def body(*refs):
        ins, outs, send_sems, recv_sems = refs[:n], refs[n:2 * n], refs[2 * n], refs[2 * n + 1]
        x, y, c = _place()
        for a in range(n):
            for k in range(1, N_CHIPS):
                part = _gathered_part(outs[a], items[a][0].shape, items[a][1], 2 * x + y)
                _remote(ins[a], part, send_sems.at[_sem(a, k)], recv_sems.at[_sem(a, k)],
                        (_flip(x, k & 2), _flip(y, k & 1), c)).start()
        refs[-1][...] = jnp.zeros(_TOKEN.shape, _TOKEN.dtype)

    arrays = [_in_hbm(a) for a, _ in items] + [_in_hbm(a) for a in gathered]
    res = pl.pallas_call(
        body, name=name, in_specs=[_HBM_SPEC] * (2 * n),
        out_specs=[_SEM_SPEC, _SEM_SPEC] + [_HBM_SPEC] * (2 * n) + [_VMEM_SPEC],
        out_shape=[pltpu.SemaphoreType.DMA((n * (N_CHIPS - 1),)), pltpu.SemaphoreType.DMA((n * (N_CHIPS - 1),))]
        + [pltpu.HBM(a.shape, a.dtype) for a in arrays] + [_TOKEN],
        input_output_aliases={a: 2 + a for a in range(2 * n)}, compiler_params=_SPLIT_COPY)(*arrays)
    return res[0], res[1], res[2:2 + n], res[2 + n:2 + 2 * n], res[-1]


def _gather_wait(items, started, after, name):
    n = len(items)
    send_sems, recv_sems, shards, gathered, _ = started

    def body(*refs):
        ins, outs, send_sems, recv_sems = refs[:n], refs[n:2 * n], refs[2 * n], refs[2 * n + 1]
        x, y, c = _place()
        for a in range(n):
            for k in range(1, N_CHIPS):
                part = _gathered_part(outs[a], items[a][0].shape, items[a][1], 2 * _flip(x, k & 2) + _flip(y, k & 1))
                cp = _remote(ins[a], part, send_sems.at[_sem(a, k)], recv_sems.at[_sem(a, k)], (x, y, c))
                cp.wait_send()
                cp.wait_recv()

    res = pl.pallas_call(
        body, name=name, in_specs=[_HBM_SPEC] * (2 * n) + [_SEM_SPEC, _SEM_SPEC] + [_HBM_SPEC] * len(after),
        out_specs=[_HBM_SPEC] * (2 * n), out_shape=[pltpu.HBM(a.shape, a.dtype) for a in (*shards, *gathered)],
        input_output_aliases={a: a for a in range(2 * n)}, compiler_params=_SPLIT_COPY)(
            *shards, *gathered, send_sems, recv_sems, *after)
    return res[n:]


def _rs_start(bufs, name):
    n = len(bufs)

    def body(*refs):
        srcs, lands, send_sems, recv_sems = refs[:n], refs[n:2 * n], refs[2 * n], refs[2 * n + 1]
        x, y, c = _place()
        for a in range(n):
            for k in range(1, N_CHIPS):
                tx, ty = _flip(x, k & 2), _flip(y, k & 1)
                _remote(srcs[a].at[2 * tx + ty], lands[a].at[k - 1], send_sems.at[_sem(a, k)], recv_sems.at[_sem(a, k)],
                        (tx, ty, c)).start()
        refs[-1][...] = jnp.zeros(_TOKEN.shape, _TOKEN.dtype)

    arrays = [_in_hbm(b) for b in bufs] + [_in_hbm(lax.empty((N_CHIPS - 1,) + b.shape[1:], b.dtype)) for b in bufs]
    res = pl.pallas_call(
        body, name=name, in_specs=[_HBM_SPEC] * (2 * n),
        out_specs=[_SEM_SPEC, _SEM_SPEC] + [_HBM_SPEC] * (2 * n) + [_VMEM_SPEC],
        out_shape=[pltpu.SemaphoreType.DMA((n * (N_CHIPS - 1),)), pltpu.SemaphoreType.DMA((n * (N_CHIPS - 1),))]
        + [pltpu.HBM(a.shape, a.dtype) for a in arrays] + [_TOKEN],
        input_output_aliases={a: 2 + a for a in range(2 * n)}, compiler_params=_SPLIT_COPY)(*arrays)
    return res[0], res[1], res[2:2 + n], res[2 + n:2 + 2 * n], res[-1]


def _rs_wait(started, after, name):
    send_sems, recv_sems, bufs, lands, _ = started
    n = len(bufs)

    def body(*refs):
        srcs, lnds, send_sems, recv_sems = refs[:n], refs[n:2 * n], refs[2 * n], refs[2 * n + 1]
        x, y, c = _place()
        for a in range(n):
            for k in range(1, N_CHIPS):
                cp = _remote(srcs[a].at[0], lnds[a].at[k - 1], send_sems.at[_sem(a, k)], recv_sems.at[_sem(a, k)], (x, y, c))
                cp.wait_send()
                cp.wait_recv()

    res = pl.pallas_call(
        body, name=name, in_specs=[_HBM_SPEC] * (2 * n) + [_SEM_SPEC, _SEM_SPEC] + [_HBM_SPEC] * len(after),
        out_specs=[_HBM_SPEC] * (2 * n), out_shape=[pltpu.HBM(a.shape, a.dtype) for a in (*bufs, *lands)],
        input_output_aliases={a: a for a in range(2 * n)}, compiler_params=_SPLIT_COPY)(
            *bufs, *lands, send_sems, recv_sems, *after)
    return res[:n], res[n:]


def _gather_chips(items):
    n = len(items)
    arrays = [a for a, _ in items]

    def body(*refs):
        ins, outs = refs[:n], refs[n:2 * n]
        send_sems, recv_sems, local_sems = refs[2 * n:]
        x, y, c = _place()
        chip = 2 * x + y
        part = lambda a, j: _gathered_part(outs[a], items[a][0].shape, items[a][1], j)
        local = [pltpu.make_async_copy(ins[a], part(a, chip), local_sems.at[a]) for a in range(n)]
        for cp in local:
            cp.start()
        sends = []
        for a in range(n):
            for k in range(1, N_CHIPS):
                peer = (_flip(x, k & 2), _flip(y, k & 1), c)
                cp = _remote(ins[a], part(a, chip), send_sems.at[_sem(a, k)], recv_sems.at[_sem(a, k)], peer)
                cp.start()
                sends.append(cp)
        for a in range(n):
            for k in range(1, N_CHIPS):
                src = 2 * _flip(x, k & 2) + _flip(y, k & 1)
                _remote(ins[a], part(a, src), send_sems.at[_sem(a, k)], recv_sems.at[_sem(a, k)], (x, y, c)).wait_recv()
        for cp in sends:
            cp.wait_send()
        for cp in local:
            cp.wait()

    return pl.pallas_call(
        body, name="gather_chips", in_specs=[_HBM_SPEC] * n, out_specs=[_HBM_SPEC] * n,
        out_shape=[_sds(_gathered_shape(a.shape, how), a.dtype) for a, how in items],
        scratch_shapes=[pltpu.SemaphoreType.DMA((n * (N_CHIPS - 1),)), pltpu.SemaphoreType.DMA((n * (N_CHIPS - 1),)),
                        pltpu.SemaphoreType.DMA((n,))])(*arrays)


def _gather_sum_all(small):
    r, w = small.shape

    def body(in_ref, all_ref, sum_ref, send_sems, recv_sems):
        x, y, c = _place()
        me = 4 * x + 2 * y + c
        all_ref[me] = in_ref[...]
        sends = []
        for k in range(1, N_DEV):
            peer = (_flip(x, k & 4), _flip(y, k & 2), _flip(c, k & 1))
            cp = _remote(in_ref, all_ref.at[me], send_sems.at[k - 1], recv_sems.at[k - 1], peer)
            cp.start()
            sends.append(cp)
        for k in range(1, N_DEV):
            src = 4 * _flip(x, k & 4) + 2 * _flip(y, k & 2) + _flip(c, k & 1)
            _remote(in_ref, all_ref.at[src], send_sems.at[k - 1], recv_sems.at[k - 1], (x, y, c)).wait_recv()
        acc = all_ref[0]
        for e in range(1, N_DEV):
            acc = acc + all_ref[e]
        sum_ref[...] = acc
        for cp in sends:
            cp.wait_send()

    return pl.pallas_call(
        body, name="gather_sum_all", in_specs=[_VMEM_SPEC], out_specs=[_VMEM_SPEC] * 2,
        out_shape=[_sds((N_DEV, r, w)), _sds((r, w))],
        scratch_shapes=[pltpu.SemaphoreType.DMA((N_DEV - 1,)), pltpu.SemaphoreType.DMA((N_DEV - 1,))],
        compiler_params=pltpu.CompilerParams(vmem_limit_bytes=VMEM_LIMIT_V7X))(small)


def _add_chips(buf, t, chip_idx):
    r, cc = buf.shape[1:]
    tr = min(256, r)

    def body(c_ref, p_ref, t_ref, o_ref):
        del c_ref
        o_ref[...] = p_ref[0] + t_ref[0] + t_ref[1] + t_ref[2]

    return pl.pallas_call(
        body, name="add_chips", out_shape=_sds((r, cc)),
        grid_spec=pltpu.PrefetchScalarGridSpec(
            num_scalar_prefetch=1, grid=(r // tr,),
            in_specs=[pl.BlockSpec((1, tr, cc), lambda i, c: (c[0], i, 0)),
                      pl.BlockSpec((N_CHIPS - 1, tr, cc), lambda i, c: (0, i, 0))],
            out_specs=pl.BlockSpec((tr, cc), lambda i, c: (i, 0))),
        compiler_params=_cp(1))(chip_idx, buf, t)


def _rs_sibling(qs):
    n = len(qs)

    def body(*refs):
        ins, outs = refs[:n], refs[n:2 * n]
        send_sems, recv_sems = refs[2 * n:]
        x, y, c = _place()
        copies = [_remote(ins[a], outs[a], send_sems.at[a], recv_sems.at[a], (x, y, 1 - c)) for a in range(n)]
        for cp in copies:
            cp.start()
        for cp in copies:
            cp.wait()

    return pl.pallas_call(
        body, name="rs_sibling", in_specs=[_HBM_SPEC] * n, out_specs=[_HBM_SPEC] * n,
        out_shape=[_sds(q.shape) for q in qs],
        scratch_shapes=[pltpu.SemaphoreType.DMA((n,)), pltpu.SemaphoreType.DMA((n,))])(*qs)


def _adamw_update(w, g, m, v):
    m = ADAM_B1 * m + (1.0 - ADAM_B1) * g
    v = ADAM_B2 * v + (1.0 - ADAM_B2) * jnp.square(g)
    m_hat = m / (1.0 - ADAM_B1 ** ADAM_STEP)
    v_hat = v / (1.0 - ADAM_B2 ** ADAM_STEP)
    return -ADAM_LR * (m_hat / (jnp.sqrt(v_hat) + ADAM_EPS) + ADAM_WD * w), m, v


def _adamw(w, g_parts, m, v, name):
    shape = w.shape
    cols = shape[-1]
    rows = _size(shape[:-1])
    tr = 512 if rows % 512 == 0 else rows
    spec = pl.BlockSpec((tr, cols), lambda i: (i, 0))
    n = len(g_parts)

    def body(*refs):
        w_ref, m_ref, v_ref = refs[:3]
        g_ref, d_ref, nm_ref, nv_ref = refs[3 + n:]
        g = refs[3][...]
        for r in refs[4:3 + n]:
            g = g + r[...]
        g_ref[...] = g
        d_ref[...], nm_ref[...], nv_ref[...] = _adamw_update(w_ref[...], g, m_ref[...], v_ref[...])

    outs = pl.pallas_call(
        body, name="adamw_" + name, grid=(rows // tr,), in_specs=[spec] * (3 + n), out_specs=[spec] * 4,
        out_shape=[_sds((rows, cols))] * 4, compiler_params=_cp(1))(
            *[a.reshape(rows, cols) for a in (w, m, v, *g_parts)])
    return tuple(o.reshape(shape) for o in outs)


def _adamw_layer(w, g_parts, m, v, layer, prev, name):
    _, r, cc = w.shape
    tr = 512 if r % 512 == 0 else r
    spec = pl.BlockSpec((1, tr, cc), lambda i: (layer, i, 0))
    n = len(g_parts)

    def body(*refs):
        w_ref, m_ref, v_ref = refs[:3]
        g_ref, d_ref, nm_ref, nv_ref = refs[-4:]
        g = refs[3][...]
        for q in refs[4:3 + n]:
            g = g + q[...]
        g = g[:, :cc]
        g_ref[0] = g
        d_ref[0], nm_ref[0], nv_ref[0] = _adamw_update(w_ref[0], g, m_ref[0], v_ref[0])

    g_specs = [pl.BlockSpec((tr, q.shape[1]), lambda i: (i, 0)) for q in g_parts]
    passed = () if prev is None else tuple(prev)
    return pl.pallas_call(
        body, name="adamw_" + name, grid=(r // tr,),
        in_specs=[spec] * 3 + g_specs + [_HBM_SPEC] * len(passed), out_specs=[spec] * 4,
        out_shape=[_sds(w.shape)] * 4, input_output_aliases={3 + n + k: k for k in range(len(passed))},
        compiler_params=_cp(1))(w, m, v, *g_parts, *passed)


def _size(shape):
    n = 1
    for s in shape:
        n *= s
    return n


_SMALL = (("dmod", (DEPTH, 3 * D_MODEL)), ("pre_norm_g", (DEPTH, D_MODEL)), ("post_norm_g", (DEPTH, D_MODEL)),
          ("even_sc_conv_w", (2, SC_KERNEL, SC_WIDTH)), ("even_sc_conv_b", (2, SC_WIDTH)),
          ("even_q_norm_g", (2, Q_LORA)), ("even_kv_norm_g", (2, KV_LORA)),
          ("odd_conv_w", (2, CONF_KERNEL, D_MODEL)), ("odd_conv_b", (2, D_MODEL)), ("odd_ln_g", (2, D_MODEL)),
          ("odd_ln_b", (2, D_MODEL)))
SMALL_ROWS = -(-sum(_size(s) for _, s in _SMALL) // (8 * 128)) * 8

_SMALL_W = (("even_sc_conv_w", (2, SC_KERNEL, SC_WIDTH // N_CHIPS)), ("odd_conv_w", (2, CONF_KERNEL, D_MODEL // N_CHIPS)),
            ("odd_conv_b", (2, D_MODEL // N_CHIPS)), ("odd_ln_g", (2, D_MODEL // N_CHIPS)),
            ("odd_ln_b", (2, D_MODEL // N_CHIPS)))
SMALL_W_ROWS = -(-sum(_size(s) for _, s in _SMALL_W) // (8 * 128)) * 8


def _pack_rows(arrays, layout, rows):
    flat = jnp.concatenate([arrays[n].reshape(-1) for n, _ in layout])
    return jnp.pad(flat, (0, rows * 128 - flat.shape[0])).reshape(rows, 128)


def _unpack_small(t):
    flat = t.reshape(-1)
    out, at = {}, 0
    for n, shape in _SMALL:
        out[n] = flat[at:at + _size(shape)].reshape(shape)
        at += _size(shape)
    return out


def _unpack_small_w(t):
    flat = t.reshape(N_CHIPS, -1)
    out, at = {}, 0
    for n, shape in _SMALL_W:
        a = flat[:, at:at + _size(shape)].reshape((N_CHIPS,) + shape)
        out[n] = jnp.moveaxis(a, 0, -2).reshape(shape[:-1] + (N_CHIPS * shape[-1],))
        at += _size(shape)
    return out


def _chip_cols(a, chip):
    n = a.shape[-1] // N_CHIPS
    return lax.dynamic_slice_in_dim(a, chip * n, n, axis=a.ndim - 1)


def _join_cols(a):
    _, l, r, cc = a.shape
    return a.transpose(1, 2, 0, 3).reshape(l, r, N_CHIPS * cc)


WEIGHT_NAMES = ("ada_w", "ada_b", "pre_norm_g", "post_norm_g", "even_w_in", "even_sc_conv_w", "even_sc_conv_b",
                "even_q_norm_g", "even_kv_norm_g", "even_w_uq", "even_w_ukv", "even_w_out", "odd_w_in", "odd_conv_w",
                "odd_conv_b", "odd_ln_g", "odd_ln_b", "odd_w_out")
GATHER_HOW = ((("even_w_in", "slot"), ("even_w_uq", "slot"), ("even_w_ukv", "slot"), ("even_w_out", "rows")),
              (("odd_w_in", "cols"), ("odd_w_out", "rows")))


def kernel(x, c, positions, ada_w, ada_b, pre_norm_g, post_norm_g, even_w_in, even_sc_conv_w, even_sc_conv_b, even_q_norm_g, even_kv_norm_g, even_w_uq, even_w_ukv, even_w_out, odd_w_in, odd_conv_w, odd_conv_b, odd_ln_g, odd_ln_b, odd_w_out, loss_target, m_ada_w, m_ada_b, m_pre_norm_g, m_post_norm_g, m_even_w_in, m_even_sc_conv_w, m_even_sc_conv_b, m_even_q_norm_g, m_even_kv_norm_g, m_even_w_uq, m_even_w_ukv, m_even_w_out, m_odd_w_in, m_odd_conv_w, m_odd_conv_b, m_odd_ln_g, m_odd_ln_b, m_odd_w_out, v_ada_w, v_ada_b, v_pre_norm_g, v_post_norm_g, v_even_w_in, v_even_sc_conv_w, v_even_sc_conv_b, v_even_q_norm_g, v_even_kv_norm_g, v_even_w_uq, v_even_w_ukv, v_even_w_out, v_odd_w_in, v_odd_conv_w, v_odd_conv_b, v_odd_ln_g, v_odd_ln_b, v_odd_w_out):
    w = dict(zip(WEIGHT_NAMES, (ada_w, ada_b, pre_norm_g, post_norm_g, even_w_in, even_sc_conv_w, even_sc_conv_b,
                                even_q_norm_g, even_kv_norm_g, even_w_uq, even_w_ukv, even_w_out, odd_w_in, odd_conv_w,
                                odd_conv_b, odd_ln_g, odd_ln_b, odd_w_out)))
    m = dict(zip(WEIGHT_NAMES, (m_ada_w, m_ada_b, m_pre_norm_g, m_post_norm_g, m_even_w_in, m_even_sc_conv_w,
                                m_even_sc_conv_b, m_even_q_norm_g, m_even_kv_norm_g, m_even_w_uq, m_even_w_ukv,
                                m_even_w_out, m_odd_w_in, m_odd_conv_w, m_odd_conv_b, m_odd_ln_g, m_odd_ln_b, m_odd_w_out)))
    v = dict(zip(WEIGHT_NAMES, (v_ada_w, v_ada_b, v_pre_norm_g, v_post_norm_g, v_even_w_in, v_even_sc_conv_w,
                                v_even_sc_conv_b, v_even_q_norm_g, v_even_kv_norm_g, v_even_w_uq, v_even_w_ukv,
                                v_even_w_out, v_odd_w_in, v_odd_conv_w, v_odd_conv_b, v_odd_ln_g, v_odd_ln_b, v_odd_w_out)))
    ix, iy, ic = _place()
    chip = 2 * ix + iy
    me = 2 * chip + ic
    s = x.shape[1]

    c_all, mod_all = _ada_fwd(jnp.broadcast_to(c, (8, D_MODEL)), ada_w, _chip_cols(ada_b, chip))
    mod = lax.dynamic_index_in_dim(mod_all, 8 * me, axis=2, keepdims=False)
    mod = mod.transpose(1, 0, 2).reshape(DEPTH, 3 * D_MODEL)

    items = [[(w[n][layer // 2].astype(MXU_DTYPE), how) for n, how in GATHER_HOW[layer % 2]] for layer in range(DEPTH)]
    first = _gather_chips(items[0] + [(_pack_rows(w, _SMALL_W, SMALL_W_ROWS), "slot")])
    small_w = _unpack_small_w(first[-1])
    later_items = [item for layer_items in items[1:] for item in layer_items]
    weights_sent = _gather_start(later_items, _place_own(later_items), "gather_start")
    later = []

    def layer_weights(layer, x_in):
        i = layer // 2
        if layer == 0:
            arrays = first[:-1]
        else:
            if not later:
                later.extend(_gather_wait(later_items, weights_sent, [x_in], "gather_wait"))
            at = sum(len(layer_items) for layer_items in items[1:layer])
            arrays = later[at:at + len(items[layer])]
        if layer % 2 == 0:
            ein, uq, ukv, eout = arrays
            wuk, wuv = _ukv_to_heads(ukv)
            return {"w_in": _ein_from_shards(ein), "wq": _uq_to_heads(uq), "wuk": wuk, "wuv": wuv, "w_out": eout,
                    "sc_conv_w": small_w["even_sc_conv_w"][i]}
        oin, oout = arrays
        return {"w_in": oin, "w_out": oout, "conv_w": small_w["odd_conv_w"][i], "conv_b": small_w["odd_conv_b"][i:i + 1],
                "ln_g": small_w["odd_ln_g"][i:i + 1], "ln_b": small_w["odd_ln_b"][i:i + 1]}

    in_flight, own, sib = {}, {}, {}

    def land(layer, after):
        names, started = in_flight.pop(layer)
        bufs, arrived = _rs_wait(started, after, "rs_wait_%d" % layer)
        sums = [_add_chips(b, t, chip.reshape(1)) for b, t in zip(bufs, arrived)]
        for n, mine, theirs in zip(names, sums, _rs_sibling(sums)):
            own[n, layer // 2], sib[n, layer // 2] = mine, theirs

    def grads_done(layer, bufs, dx_in):
        if layer + 1 in in_flight:
            land(layer + 1, [dx_in])
        names = sorted(bufs)
        in_flight[layer] = (names, _rs_start([bufs[n] for n in names], "rs_start_%d" % layer))
        return in_flight[layer][1][-1]

    p = {"pre_norm_g": pre_norm_g, "post_norm_g": post_norm_g, "even_sc_conv_b": even_sc_conv_b,
         "even_q_norm_g": even_q_norm_g, "even_kv_norm_g": even_kv_norm_g}
    inv_freq = 1.0 / (ROPE_THETA ** (jnp.arange(0, QK_ROPE, 2, dtype=F32) / QK_ROPE))
    inv_freq = jnp.zeros((1, HEAD_PAD), F32).at[0, QK_NOPE:QK_NOPE + QK_ROPE].set(jnp.tile(inv_freq, 2))
    cos, sin = _rope_tables(positions.reshape(s, 1), inv_freq)

    loss, dx, g = _local_step(x[0], loss_target[0], cos, sin, mod, p, layer_weights, weights_sent[-1], grads_done)

    grads, deltas, new_m, new_v = {}, {}, {}, {}

    def update_layers(n, results, pairs):
        for i in pairs:
            results = _adamw_layer(w[n], [own[n, i], sib[n, i]], m[n], v[n], i, results, n)
        return results

    small_all, small_sum = _gather_sum_all(_pack_rows(g, _SMALL, SMALL_ROWS))
    tot = _unpack_small(small_sum)
    dmod_all = small_all[:, :DEPTH * 3 * D_MODEL // 128].reshape(N_DEV, DEPTH, 3 * D_MODEL)
    grads["ada_w"] = _ada_bwd(c_all[:, 0, :].T, _chip_cols(dmod_all, chip).transpose(1, 0, 2))
    grads["ada_b"] = tot["dmod"]
    for n in ("pre_norm_g", "post_norm_g", "even_sc_conv_b", "even_q_norm_g", "even_kv_norm_g"):
        grads[n] = tot[n]
    for n in ("even_sc_conv_w", "odd_conv_w", "odd_conv_b", "odd_ln_g", "odd_ln_b"):
        grads[n] = _chip_cols(tot[n], chip)
    for n in list(grads):
        _, deltas[n], new_m[n], new_v[n] = _adamw(w[n], [grads[n]], m[n], v[n], n)

    for n in ("odd_w_in", "odd_w_out"):
        grads[n], deltas[n], new_m[n], new_v[n] = update_layers(n, None, (1, 0))
    partly = {n: update_layers(n, None, (1,)) for n in ("even_w_in", "even_w_out")}
    land(0, [deltas["ada_w"], deltas["odd_w_in"], partly["even_w_in"][1]])
    for n in ("even_w_in", "even_w_out"):
        grads[n], deltas[n], new_m[n], new_v[n] = update_layers(n, partly[n], (0,))
    uq_parts, ukv_parts = zip(*[[jnp.stack(part) for part in zip(*[_mla_local(q["even_mla", i]) for i in range(N_PAIRS)])]
                                for q in (own, sib)])
    for n, parts in (("even_w_uq", uq_parts), ("even_w_ukv", ukv_parts)):
        grads[n], deltas[n], new_m[n], new_v[n] = _adamw(w[n], list(parts), m[n], v[n], n)

    total_loss = lax.psum(loss[0, 0], ("x", "y", "c"))
    return (total_loss, dx[None], *[grads[n] for n in WEIGHT_NAMES], *[deltas[n] for n in WEIGHT_NAMES],
            *[new_m[n] for n in WEIGHT_NAMES], *[new_v[n] for n in WEIGHT_NAMES])
```

```python
import functools

import jax
import jax.numpy as jnp
from jax import lax
from jax.experimental import pallas as pl
from jax.experimental.pallas import tpu as pltpu

F32 = jnp.float32
MXU_DTYPE = jnp.bfloat16
MESH = pl.DeviceIdType.MESH
VMEM_LIMIT_V7X = 56 * 2 ** 20

EPS = 1e-6
D_MODEL = 1024
DEPTH = 4
CHUNK = 64
SC_WIDTH = 512
SC_KERNEL = 3
SC_HALO = 8
HEADS = 8
QK_NOPE = 64
QK_ROPE = 32
V_HEAD = 64
HEAD_PAD = 128
Q_LORA = 256
KV_LORA = 128
ROPE_THETA = 10000.0
CONF_KERNEL = 31
CONF_HALO = 32
CONV_ROWS = 32
EVEN_IN = 2976
EVEN_PAD = 3072
ODD_IN = 3072
N_CHIPS = 4
N_DEV = 8
NEG = -1e30

ADAM_LR = 0.001
ADAM_B1 = 0.9
ADAM_B2 = 0.999
ADAM_EPS = 1e-08
ADAM_WD = 0.01
ADAM_STEP = 10

N_PAIRS = DEPTH // 2
EVEN_SHARD = EVEN_IN // N_CHIPS
EVEN_SHARD_PAD = 768
MLA_ROWS = Q_LORA + 2 * KV_LORA


def _cp(n_grid=0, **kw):
    return pltpu.CompilerParams(dimension_semantics=("arbitrary",) * n_grid,
                                vmem_limit_bytes=VMEM_LIMIT_V7X, **kw)


def _sigmoid(x):
    return 1.0 / (1.0 + jnp.exp(-x))


def _silu(x):
    return x * _sigmoid(x)


def _dsilu(x):
    s = _sigmoid(x)
    return s * (1.0 + x * (1.0 - s))


def _rms(x, g):
    return x * lax.rsqrt(jnp.mean(x * x, axis=-1, keepdims=True) + EPS) * g


def _dot(a, b, dims):
    return lax.dot_general(a.astype(MXU_DTYPE), b.astype(MXU_DTYPE), (dims, ((), ())),
                           preferred_element_type=F32)


def _dot_nn(a, b):
    return _dot(a, b, ((1,), (0,)))


def _dot_nt(a, b):
    return _dot(a, b, ((1,), (1,)))


def _dot_tn(a, b):
    return _dot(a, b, ((0,), (0,)))


def _rows(ts, w, cb=0):
    return pl.BlockSpec((ts, w), lambda i: (i, cb))


def _vec(w, cb=0, r=1):
    return pl.BlockSpec((r, w), lambda i: (0, cb))


def _prev_halo(ts, hr, w, cb):
    return pl.BlockSpec((hr, w), lambda i: (jnp.maximum(i * (ts // hr) - 1, 0), cb))


def _next_halo(ts, hr, w, cb, s):
    return pl.BlockSpec((hr, w), lambda i: (jnp.minimum((i + 1) * (ts // hr), s // hr - 1), cb))


def _sds(shape, dtype=F32):
    return jax.ShapeDtypeStruct(shape, dtype)


def _mm(a, b, mode, out_dtype, tm, tn, name):
    tm = min(tm, a.shape[1] if mode == "tn" else a.shape[0])
    tn = min(tn, b.shape[0] if mode == "nt" else b.shape[1])
    if mode == "nn":
        (m, k), n = a.shape, b.shape[1]
        a_spec = pl.BlockSpec((tm, k), lambda i, j: (i, 0))
        b_spec = pl.BlockSpec((k, tn), lambda i, j: (0, j))
        dot = _dot_nn
    elif mode == "nt":
        (m, k), n = a.shape, b.shape[0]
        a_spec = pl.BlockSpec((tm, k), lambda i, j: (i, 0))
        b_spec = pl.BlockSpec((tn, k), lambda i, j: (j, 0))
        dot = _dot_nt
    else:
        (k, m), n = a.shape, b.shape[1]
        a_spec = pl.BlockSpec((k, tm), lambda i, j: (0, i))
        b_spec = pl.BlockSpec((k, tn), lambda i, j: (0, j))
        dot = _dot_tn
    assert m % tm == 0 and n % tn == 0, (name, m, n, tm, tn)

    def body(a_ref, b_ref, o_ref):
        o_ref[...] = dot(a_ref[...], b_ref[...]).astype(o_ref.dtype)

    return pl.pallas_call(
        body, name=name, grid=(m // tm, n // tn), in_specs=[a_spec, b_spec],
        out_specs=pl.BlockSpec((tm, tn), lambda i, j: (i, j)), out_shape=_sds((m, n), out_dtype),
        compiler_params=_cp(2))(a, b)


def _mm_tn_shards(a, b, by, name):
    k, m = a.shape
    n = b.shape[1]
    if by == "cols":
        tm, tn = 512, n // N_CHIPS
        shape, grid = (N_CHIPS, m, tn), (m // tm, N_CHIPS)
        out_spec = pl.BlockSpec((1, tm, tn), lambda i, j: (j, i, 0))
    else:
        tm, tn = m // N_CHIPS, 512
        shape, grid = (N_CHIPS, tm, n), (N_CHIPS, n // tn)
        out_spec = pl.BlockSpec((1, tm, tn), lambda i, j: (i, 0, j))

    def body(a_ref, b_ref, o_ref):
        o_ref[0] = _dot_tn(a_ref[...], b_ref[...])

    return pl.pallas_call(
        body, name=name, grid=grid,
        in_specs=[pl.BlockSpec((k, tm), lambda i, j: (0, i)), pl.BlockSpec((k, tn), lambda i, j: (0, j))],
        out_specs=out_spec, out_shape=_sds(shape), compiler_params=_cp(2))(a, b)


def _even_col(q):
    return q if q < 2432 else (q + 64 if q < 2464 else q + 96)


def _shard_pieces(j):
    lo, hi = EVEN_SHARD * j, EVEN_SHARD * (j + 1)
    cuts = [lo] + [b for b in (2432, 2464) if lo < b < hi] + [hi]
    return [(a - lo, _even_col(a), b - a) for a, b in zip(cuts[:-1], cuts[1:])]


def _ein_from_shards(w):
    _, d, _ = w.shape
    tr = 256

    def body(w_ref, o_ref):
        parts, at = [], 0
        for j in range(N_CHIPS):
            for d0, s0, n in _shard_pieces(j):
                if s0 > at:
                    parts.append(jnp.zeros((tr, s0 - at), F32))
                parts.append(w_ref[j, :, d0:d0 + n].astype(F32))
                at = s0 + n
        o_ref[...] = jnp.concatenate(parts, axis=1).astype(o_ref.dtype)

    return pl.pallas_call(
        body, name="ein_from_shards", grid=(d // tr,),
        in_specs=[pl.BlockSpec((N_CHIPS, tr, EVEN_SHARD), lambda i: (0, i, 0))],
        out_specs=_rows(tr, EVEN_PAD), out_shape=_sds((d, EVEN_PAD), w.dtype), compiler_params=_cp(1))(w)


def _ein_to_shards(dw):
    d = dw.shape[0]
    tr = 256

    def body(dw_ref, o_ref):
        for j in range(N_CHIPS):
            parts = [dw_ref[:, s0:s0 + n] for _, s0, n in _shard_pieces(j)]
            o_ref[j] = jnp.concatenate(parts + [jnp.zeros((tr, EVEN_SHARD_PAD - EVEN_SHARD), F32)], axis=1)

    return pl.pallas_call(
        body, name="ein_to_shards", grid=(d // tr,), in_specs=[_rows(tr, EVEN_PAD)],
        out_specs=pl.BlockSpec((N_CHIPS, tr, EVEN_SHARD_PAD), lambda i: (0, i, 0)),
        out_shape=_sds((N_CHIPS, d, EVEN_SHARD_PAD)), compiler_params=_cp(1))(dw)


def _rope_tables(pos_col, invf):
    s = pos_col.shape[0]
    ts = min(512, s)

    def body(p_ref, f_ref, c_ref, s_ref):
        ang = p_ref[...].astype(F32) * f_ref[...]
        lane = lax.broadcasted_iota(jnp.int32, ang.shape, 1)
        rope = (lane >= QK_NOPE) & (lane < QK_NOPE + QK_ROPE)
        c_ref[...] = jnp.where(lane < QK_NOPE, 1.0, jnp.where(rope, jnp.cos(ang), 0.0))
        s_ref[...] = jnp.where(rope, jnp.sin(ang), 0.0)

    return pl.pallas_call(
        body, name="rope_tables", grid=(s // ts,), in_specs=[_rows(ts, 1), _vec(HEAD_PAD)],
        out_specs=[_rows(ts, HEAD_PAD)] * 2, out_shape=[_sds((s, HEAD_PAD))] * 2,
        compiler_params=_cp(1))(pos_col, invf)


def _after(dep):
    return () if dep is None else (dep,)


def _pre_fwd(x, g, mod_l, ts, dep=None):
    s, d = x.shape

    def body(x_ref, g_ref, sh_ref, sc_ref, *rest):
        h = _rms(x_ref[...], g_ref[...]) * (1.0 + sc_ref[...]) + sh_ref[...]
        rest[-1][...] = h.astype(rest[-1].dtype)

    return pl.pallas_call(
        body, name="pre_fwd", grid=(s // ts,),
        in_specs=[_rows(ts, d), _vec(d), _vec(d, 0), _vec(d, 1)] + [_HBM_SPEC] * len(_after(dep)),
        out_specs=_rows(ts, d), out_shape=_sds((s, d), MXU_DTYPE), compiler_params=_cp(1))(
            x, g, mod_l, mod_l, *_after(dep))


def _pre_bwd(dh, dx_out, x, g, mod_l, ts):
    s, d = x.shape

    def f(xv, gv, sh, sc):
        return _rms(xv, gv) * (1.0 + sc) + sh

    def body(dh_ref, dxo_ref, x_ref, g_ref, sh_ref, sc_ref, dx_ref, dsh_ref, dsc_ref, dg_ref):
        i = pl.program_id(0)
        _, vjp = jax.vjp(f, x_ref[...], g_ref[...], sh_ref[...], sc_ref[...])
        dx, dg, dsh, dsc = vjp(dh_ref[...])
        dx_ref[...] = dxo_ref[...] + dx

        @pl.when(i == 0)
        def _():
            dsh_ref[...] = jnp.zeros_like(dsh_ref)
            dsc_ref[...] = jnp.zeros_like(dsc_ref)
            dg_ref[...] = jnp.zeros_like(dg_ref)

        dsh_ref[...] += dsh
        dsc_ref[...] += dsc
        dg_ref[...] += dg

    return pl.pallas_call(
        body, name="pre_bwd", grid=(s // ts,),
        in_specs=[_rows(ts, d), _rows(ts, d), _rows(ts, d), _vec(d), _vec(d, 0), _vec(d, 1)],
        out_specs=[_rows(ts, d), _vec(d), _vec(d), _vec(d)],
        out_shape=[_sds((s, d)), _sds((1, d)), _sds((1, d)), _sds((1, d))],
        compiler_params=_cp(1))(dh, dx_out, x, g, mod_l, mod_l)


def _post_fwd(x, yo, g, mod_l, ts):
    s, d = x.shape

    def body(x_ref, yo_ref, g_ref, gate_ref, o_ref):
        o_ref[...] = x_ref[...] + gate_ref[...] * _rms(yo_ref[...], g_ref[...])

    return pl.pallas_call(
        body, name="post_fwd", grid=(s // ts,),
        in_specs=[_rows(ts, d), _rows(ts, d), _vec(d), _vec(d, 2)],
        out_specs=_rows(ts, d), out_shape=_sds((s, d)), compiler_params=_cp(1))(x, yo, g, mod_l)


def _post_bwd(dx_out, yo, g, mod_l, ts, dep=None):
    s, d = yo.shape

    def f(yov, gv, gate):
        return gate * _rms(yov, gv)

    def body(dx_ref, yo_ref, g_ref, gate_ref, *rest):
        dyo_ref, dgate_ref, dg_ref = rest[-3:]
        i = pl.program_id(0)
        _, vjp = jax.vjp(f, yo_ref[...], g_ref[...], gate_ref[...])
        dyo, dg, dgate = vjp(dx_ref[...])
        dyo_ref[...] = dyo.astype(dyo_ref.dtype)

        @pl.when(i == 0)
        def _():
            dgate_ref[...] = jnp.zeros_like(dgate_ref)
            dg_ref[...] = jnp.zeros_like(dg_ref)

        dgate_ref[...] += dgate
        dg_ref[...] += dg

    return pl.pallas_call(
        body, name="post_bwd", grid=(s // ts,),
        in_specs=[_rows(ts, d), _rows(ts, d), _vec(d), _vec(d, 2)] + [_HBM_SPEC] * len(_after(dep)),
        out_specs=[_rows(ts, d), _vec(d), _vec(d)],
        out_shape=[_sds((s, d), MXU_DTYPE), _sds((1, d)), _sds((1, d))],
        compiler_params=_cp(1))(dx_out, yo, g, mod_l, *_after(dep))


def _loss_fwd_bwd(x, target, ts):
    s, d = x.shape

    def body(x_ref, t_ref, loss_ref, dx_ref):
        i = pl.program_id(0)
        err = x_ref[...] - t_ref[...]
        dx_ref[...] = err * (1.0 / d)

        @pl.when(i == 0)
        def _():
            loss_ref[...] = jnp.zeros_like(loss_ref)

        loss_ref[...] += 0.5 * jnp.sum(jnp.sum(err * err, axis=-1, keepdims=True) * (1.0 / d), axis=0, keepdims=True)

    return pl.pallas_call(
        body, name="loss", grid=(s // ts,), in_specs=[_rows(ts, d), _rows(ts, d)],
        out_specs=[_vec(1), _rows(ts, d)], out_shape=[_sds((1, 1)), _sds((s, d))],
        compiler_params=_cp(1))(x, target)


def _rope(t, cos, sin):
    lane = lax.broadcasted_iota(jnp.int32, t.shape, 1)
    first = (lane >= QK_NOPE) & (lane < QK_NOPE + QK_ROPE // 2)
    second = (lane >= QK_NOPE + QK_ROPE // 2) & (lane < QK_NOPE + QK_ROPE)
    up = pltpu.roll(t, QK_ROPE // 2, 1)
    down = pltpu.roll(t, HEAD_PAD - QK_ROPE // 2, 1)
    return t * cos + jnp.where(first, -down, jnp.where(second, up, 0.0)) * sin


def _rope_transposed(g, cos, sin):
    lane = lax.broadcasted_iota(jnp.int32, g.shape, 1)
    first = (lane >= QK_NOPE) & (lane < QK_NOPE + QK_ROPE // 2)
    second = (lane >= QK_NOPE + QK_ROPE // 2) & (lane < QK_NOPE + QK_ROPE)
    u = g * sin
    up = pltpu.roll(u, QK_ROPE // 2, 1)
    down = pltpu.roll(u, HEAD_PAD - QK_ROPE // 2, 1)
    return g * cos + jnp.where(first, down, jnp.where(second, -up, 0.0))


def _mla_prep_fwd(z, cos, sin, qg, kvg, wq, wuk, wuv, ts):
    s = z.shape[0]

    def body(cq_ref, ckv_ref, kr_ref, cos_ref, sin_ref, qg_ref, kvg_ref, wq_ref, wuk_ref, wuv_ref,
             q_ref, k_ref, v_ref):
        cos_v, sin_v = cos_ref[...], sin_ref[...]
        cqn = _rms(cq_ref[...], qg_ref[...])
        q_ref[0] = _rope(_dot_nn(cqn, wq_ref[0]), cos_v, sin_v).astype(q_ref.dtype)
        ckvn = _rms(ckv_ref[...], kvg_ref[...])
        k_ref[0] = (_dot_nn(ckvn, wuk_ref[0]) + _rope(kr_ref[...], cos_v, sin_v)).astype(k_ref.dtype)
        v_ref[0] = _dot_nn(ckvn, wuv_ref[0]).astype(v_ref.dtype)

    row = lambda w, cb: pl.BlockSpec((ts, w), lambda i, h: (i, cb))
    vec = lambda w: pl.BlockSpec((1, w), lambda i, h: (0, 0))
    wsp = lambda k: pl.BlockSpec((1, k, HEAD_PAD), lambda i, h: (h, 0, 0))
    out = pl.BlockSpec((1, ts, HEAD_PAD), lambda i, h: (h, i, 0))
    return pl.pallas_call(
        body, name="mla_prep_fwd", grid=(s // ts, HEADS),
        in_specs=[row(Q_LORA, 8), row(KV_LORA, 18), row(HEAD_PAD, 19), row(HEAD_PAD, 0), row(HEAD_PAD, 0),
                  vec(Q_LORA), vec(KV_LORA), wsp(Q_LORA), wsp(KV_LORA), wsp(KV_LORA)],
        out_specs=[out] * 3, out_shape=[_sds((HEADS, s, HEAD_PAD), MXU_DTYPE)] * 3,
        compiler_params=_cp(2))(z, z, z, cos, sin, qg, kvg, wq, wuk, wuv)


def _mla_prep_bwd(dz, dq, dk, dv, z, cos, sin, qg, kvg, wq, wuk, wuv, ts):
    s = z.shape[0]

    def fq(cq, g):
        return _rms(cq, g)

    def body(dz_in_ref, dq_ref, dk_ref, dv_ref, cq_ref, ckv_ref, cos_ref, sin_ref, qg_ref, kvg_ref, wq_ref, wuk_ref,
             wuv_ref, dz_ref, dw_ref, dqg_ref, dkvg_ref, dcqn_acc, dckvn_acc, dkr_acc):
        del dz_in_ref
        i, h = pl.program_id(0), pl.program_id(1)
        cos_v, sin_v = cos_ref[...], sin_ref[...]
        row0 = pl.multiple_of((h % 2) * MLA_ROWS, MLA_ROWS)
        dwq_ref = dw_ref.at[h // 2, pl.ds(row0, Q_LORA)]
        dwuk_ref = dw_ref.at[h // 2, pl.ds(row0 + Q_LORA, KV_LORA)]
        dwuv_ref = dw_ref.at[h // 2, pl.ds(row0 + Q_LORA + KV_LORA, KV_LORA)]

        @pl.when((i == 0) & (h == 0))
        def _():
            dw_ref[...] = jnp.zeros_like(dw_ref)
            dqg_ref[...] = jnp.zeros_like(dqg_ref)
            dkvg_ref[...] = jnp.zeros_like(dkvg_ref)

        @pl.when(h == 0)
        def _():
            dcqn_acc[...] = jnp.zeros_like(dcqn_acc)
            dckvn_acc[...] = jnp.zeros_like(dckvn_acc)
            dkr_acc[...] = jnp.zeros_like(dkr_acc)

        cqn = _rms(cq_ref[...], qg_ref[...])
        ckvn = _rms(ckv_ref[...], kvg_ref[...])
        dq_lin = _rope_transposed(dq_ref[0], cos_v, sin_v)
        dcqn_acc[...] += _dot_nt(dq_lin, wq_ref[0])
        dwq_ref[...] += _dot_tn(cqn, dq_lin)
        dkh, dvh = dk_ref[0], dv_ref[0]
        lane = lax.broadcasted_iota(jnp.int32, dkh.shape, 1)
        dkr_acc[...] += jnp.where((lane >= QK_NOPE) & (lane < QK_NOPE + QK_ROPE), dkh, 0.0)
        dckvn_acc[...] += _dot_nt(dkh, wuk_ref[0]) + _dot_nt(dvh, wuv_ref[0])
        dwuk_ref[...] += _dot_tn(ckvn, dkh)
        dwuv_ref[...] += _dot_tn(ckvn, dvh)

        @pl.when(h == HEADS - 1)
        def _():
            _, vjp_q = jax.vjp(fq, cq_ref[...], qg_ref[...])
            dcq, dqg = vjp_q(dcqn_acc[...])
            _, vjp_kv = jax.vjp(fq, ckv_ref[...], kvg_ref[...])
            dckv, dkvg = vjp_kv(dckvn_acc[...])
            dz_ref[:, 0:Q_LORA] = dcq
            dz_ref[:, Q_LORA:Q_LORA + KV_LORA] = dckv
            dz_ref[:, Q_LORA + KV_LORA:] = _rope_transposed(dkr_acc[...], cos_v, sin_v)
            dqg_ref[...] += dqg
            dkvg_ref[...] += dkvg

    row = lambda w, cb: pl.BlockSpec((ts, w), lambda i, h: (i, cb))
    vec = lambda w: pl.BlockSpec((1, w), lambda i, h: (0, 0))
    wsp = lambda k: pl.BlockSpec((1, k, HEAD_PAD), lambda i, h: (h, 0, 0))
    hrow = pl.BlockSpec((1, ts, HEAD_PAD), lambda i, h: (h, i, 0))
    whole = pl.BlockSpec((N_CHIPS, 2 * MLA_ROWS, HEAD_PAD), lambda i, h: (0, 0, 0))
    return pl.pallas_call(
        body, name="mla_prep_bwd", grid=(s // ts, HEADS),
        in_specs=[_HBM_SPEC, hrow, hrow, hrow, row(Q_LORA, 8), row(KV_LORA, 18),
                  row(HEAD_PAD, 0), row(HEAD_PAD, 0), vec(Q_LORA), vec(KV_LORA), wsp(Q_LORA), wsp(KV_LORA), wsp(KV_LORA)],
        out_specs=[row(512, 4), whole, vec(Q_LORA), vec(KV_LORA)],
        out_shape=[_sds(dz.shape), _sds((N_CHIPS, 2 * MLA_ROWS, HEAD_PAD)), _sds((1, Q_LORA)), _sds((1, KV_LORA))],
        scratch_shapes=[pltpu.VMEM((ts, Q_LORA), F32), pltpu.VMEM((ts, KV_LORA), F32), pltpu.VMEM((ts, HEAD_PAD), F32)],
        input_output_aliases={0: 0}, compiler_params=_cp(2))(dz, dq, dk, dv, z, z, cos, sin, qg, kvg, wq, wuk, wuv)


def _chunk_mask(q0, k0, tq, tk):
    rows = q0 + lax.broadcasted_iota(jnp.int32, (tq, tk), 0)
    cols = k0 + lax.broadcasted_iota(jnp.int32, (tq, tk), 1)
    return lax.shift_right_logical(cols, 6) <= lax.shift_right_logical(rows, 6)


def _attn_fwd(q, k, v, tq):
    s = q.shape[1]
    nq = s // tq
    scale = 1.0 / float(QK_NOPE + QK_ROPE) ** 0.5

    def body(q_ref, k_ref, v_ref, o_ref, lse_ref):
        qi, hh = pl.program_id(1), pl.program_id(2)
        qv = q_ref[0]

        def step(kj, carry, masked):
            m, l, acc = carry
            k0 = pl.multiple_of(kj * tq, tq)
            sc = _dot_nt(qv, k_ref[0, pl.ds(k0, tq), :]) * scale
            if masked:
                sc = jnp.where(_chunk_mask(qi * tq, k0, tq, tq), sc, NEG)
            m_new = jnp.maximum(m, jnp.max(sc, axis=-1, keepdims=True))
            alpha = jnp.exp(m - m_new)
            p = jnp.exp(sc - m_new)
            l = alpha * l + jnp.sum(p, axis=-1, keepdims=True)
            acc = alpha * acc + _dot_nn(p, v_ref[0, pl.ds(k0, tq), :])
            return m_new, l, acc

        init = (jnp.full((tq, 1), NEG, F32), jnp.zeros((tq, 1), F32), jnp.zeros((tq, HEAD_PAD), F32))
        carry = lax.fori_loop(0, qi, lambda kj, c: step(kj, c, False), init)
        m, l, acc = step(qi, carry, True)
        o = acc / l
        lse_ref[0] = m + jnp.log(l)

        @pl.when(hh == 0)
        def _():
            o_ref[...] = o

        @pl.when(hh == 1)
        def _():
            o_ref[...] += o

    head = lambda hp, qi, hh: 2 * hp + hh
    return pl.pallas_call(
        body, name="attn_fwd", grid=(HEADS // 2, nq, 2),
        in_specs=[pl.BlockSpec((1, tq, HEAD_PAD), lambda hp, qi, hh: (head(hp, qi, hh), qi, 0)),
                  pl.BlockSpec((1, s, HEAD_PAD), lambda hp, qi, hh: (head(hp, qi, hh), 0, 0)),
                  pl.BlockSpec((1, s, HEAD_PAD), lambda hp, qi, hh: (head(hp, qi, hh), 0, 0))],
        out_specs=[pl.BlockSpec((tq, HEAD_PAD), lambda hp, qi, hh: (qi, hp)),
                   pl.BlockSpec((1, tq, 1), lambda hp, qi, hh: (head(hp, qi, hh), qi, 0))],
        out_shape=[_sds((s, HEADS * V_HEAD)), _sds((HEADS, s, 1))],
        compiler_params=_cp(3))(q, k, v)


def _attn_bwd(q, k, v, do, o, lse, tq):
    s = q.shape[1]
    nq = s // tq
    scale = 1.0 / float(QK_NOPE + QK_ROPE) ** 0.5

    def body(q_ref, k_ref, v_ref, do_ref, o_ref, lse_ref, dq_ref, dk_ref, dv_ref):
        hh, kj = pl.program_id(1), pl.program_id(2)

        @pl.when(kj == 0)
        def _():
            dq_ref[...] = jnp.zeros_like(dq_ref)

        kv, vv = k_ref[0], v_ref[0]
        lane = lax.broadcasted_iota(jnp.int32, (tq, HEAD_PAD), 1)
        mine = lax.shift_right_logical(lane, 6) == hh

        def step(qi, carry, masked):
            dk_acc, dv_acc = carry
            q0 = pl.multiple_of(qi * tq, tq)
            qv = q_ref[0, pl.ds(q0, tq), :]
            dov = do_ref[pl.ds(q0, tq), :]
            delta = jnp.sum(jnp.where(mine, dov * o_ref[pl.ds(q0, tq), :], 0.0), axis=-1, keepdims=True)
            sc = _dot_nt(qv, kv) * scale
            if masked:
                sc = jnp.where(_chunk_mask(q0, kj * tq, tq, tq), sc, NEG)
            p = jnp.exp(sc - lse_ref[0, pl.ds(q0, tq), :])
            do_b = dov.astype(MXU_DTYPE)
            ds = (p * (_dot_nt(do_b, vv) - delta) * scale).astype(MXU_DTYPE)
            dv_acc = dv_acc + _dot_tn(p, do_b)
            dk_acc = dk_acc + _dot_tn(ds, qv)
            dq_ref[0, pl.ds(q0, tq), :] += _dot_nn(ds, kv)
            return dk_acc, dv_acc

        zero = jnp.zeros((tq, HEAD_PAD), F32)
        carry = step(kj, (zero, zero), True)
        dk_acc, dv_acc = lax.fori_loop(kj + 1, nq, lambda qi, c: step(qi, c, False), carry)
        dk_ref[0] = dk_acc
        dv_ref[0] = dv_acc

    head = lambda hp, hh, kj: 2 * hp + hh
    full = pl.BlockSpec((1, s, HEAD_PAD), lambda hp, hh, kj: (head(hp, hh, kj), 0, 0))
    blk = pl.BlockSpec((1, tq, HEAD_PAD), lambda hp, hh, kj: (head(hp, hh, kj), kj, 0))
    pair = pl.BlockSpec((s, HEAD_PAD), lambda hp, hh, kj: (0, hp))
    return pl.pallas_call(
        body, name="attn_bwd", grid=(HEADS // 2, 2, nq),
        in_specs=[full, blk, blk, pair, pair, pl.BlockSpec((1, s, 1), lambda hp, hh, kj: (head(hp, hh, kj), 0, 0))],
        out_specs=[full, blk, blk], out_shape=[_sds((HEADS, s, HEAD_PAD))] * 3,
        compiler_params=_cp(3))(q, k, v, do, o, lse)


def _sc_conv(u, ubuf, w_ref, b_ref, ts):
    return (w_ref[2:3, :] * u + w_ref[1:2, :] * ubuf[pl.ds(SC_HALO - 1, ts), :]
            + w_ref[0:1, :] * ubuf[pl.ds(SC_HALO - 2, ts), :] + b_ref[...])


def _even_gate_fwd(z, o, sc_w, sc_b, ts):
    s = z.shape[0]
    w = SC_WIDTH

    def body(ab_ref, ac_ref, ax_ref, ag_ref, bg_ref, hc_ref, hx_ref, o_ref, w_ref, b_ref, y_ref, ubuf):
        i = pl.program_id(0)
        u = ac_ref[...] * ax_ref[...]
        ubuf[0:SC_HALO, :] = jnp.where(i > 0, hc_ref[...] * hx_ref[...], 0.0)
        ubuf[SC_HALO:, :] = u
        conv = _sc_conv(u, ubuf, w_ref, b_ref, ts)
        y_ref[:, 0:w] = (ab_ref[...] * conv * _silu(ag_ref[...])).astype(y_ref.dtype)
        y_ref[:, w:] = (o_ref[...] * _silu(bg_ref[...])).astype(y_ref.dtype)

    return pl.pallas_call(
        body, name="even_gate_fwd", grid=(s // ts,),
        in_specs=[_rows(ts, w, 0), _rows(ts, w, 1), _rows(ts, w, 2), _rows(ts, w, 3), _rows(ts, w, 5),
                  _prev_halo(ts, SC_HALO, w, 1), _prev_halo(ts, SC_HALO, w, 2), _rows(ts, w),
                  _vec(w, 0, SC_KERNEL), _vec(w)],
        out_specs=_rows(ts, 2 * w), out_shape=_sds((s, 2 * w), MXU_DTYPE),
        scratch_shapes=[pltpu.VMEM((ts + SC_HALO, w), F32)],
        compiler_params=_cp(1))(z, z, z, z, z, z, z, o, sc_w, sc_b)


def _even_gate_bwd(dy, z, o, sc_w, sc_b, ts):
    s = z.shape[0]
    w = SC_WIDTH
    n = s // ts

    def body(dya_ref, dyb_ref, dyan_ref, ab_ref, ac_ref, ax_ref, ag_ref, bg_ref, hc_ref, hx_ref, abn_ref, agn_ref,
             o_ref, w_ref, b_ref, dz_ref, do_ref, dw_ref, db_ref, ubuf, dbuf):
        i = pl.program_id(0)
        ab, ac, ax, ag, bg = ab_ref[...], ac_ref[...], ax_ref[...], ag_ref[...], bg_ref[...]
        dya, dyb = dya_ref[...], dyb_ref[...]
        u = ac * ax
        ubuf[0:SC_HALO, :] = jnp.where(i > 0, hc_ref[...] * hx_ref[...], 0.0)
        ubuf[SC_HALO:, :] = u
        conv = _sc_conv(u, ubuf, w_ref, b_ref, ts)
        sg = _silu(ag)
        dconv = dya * ab * sg
        dbuf[0:ts, :] = dconv
        dbuf[ts:, :] = jnp.where(i < n - 1, dyan_ref[...] * abn_ref[...] * _silu(agn_ref[...]), 0.0)
        du = w_ref[2:3, :] * dconv + w_ref[1:2, :] * dbuf[pl.ds(1, ts), :] + w_ref[0:1, :] * dbuf[pl.ds(2, ts), :]
        dz_ref[:, 0:w] = dya * conv * sg
        dz_ref[:, w:2 * w] = du * ax
        dz_ref[:, 2 * w:3 * w] = du * ac
        dz_ref[:, 3 * w:4 * w] = dya * ab * conv * _dsilu(ag)
        dz_ref[:, 4 * w:5 * w] = jnp.zeros((ts, w), F32)
        dz_ref[:, 5 * w:] = dyb * o_ref[...] * _dsilu(bg)
        do_ref[...] = dyb * _silu(bg)

        @pl.when(i == 0)
        def _():
            dw_ref[...] = jnp.zeros_like(dw_ref)
            db_ref[...] = jnp.zeros_like(db_ref)

        dw_ref[0:1, :] += jnp.sum(dconv * ubuf[pl.ds(SC_HALO - 2, ts), :], axis=0, keepdims=True)
        dw_ref[1:2, :] += jnp.sum(dconv * ubuf[pl.ds(SC_HALO - 1, ts), :], axis=0, keepdims=True)
        dw_ref[2:3, :] += jnp.sum(dconv * u, axis=0, keepdims=True)
        db_ref[...] += jnp.sum(dconv, axis=0, keepdims=True)

    return pl.pallas_call(
        body, name="even_gate_bwd", grid=(n,),
        in_specs=[_rows(ts, w, 0), _rows(ts, w, 1), _next_halo(ts, SC_HALO, w, 0, s),
                  _rows(ts, w, 0), _rows(ts, w, 1), _rows(ts, w, 2), _rows(ts, w, 3), _rows(ts, w, 5),
                  _prev_halo(ts, SC_HALO, w, 1), _prev_halo(ts, SC_HALO, w, 2),
                  _next_halo(ts, SC_HALO, w, 0, s), _next_halo(ts, SC_HALO, w, 3, s),
                  _rows(ts, w), _vec(w, 0, SC_KERNEL), _vec(w)],
        out_specs=[_rows(ts, EVEN_PAD), _rows(ts, w), _vec(w, 0, SC_KERNEL), _vec(w)],
        out_shape=[_sds((s, EVEN_PAD)), _sds((s, w)), _sds((SC_KERNEL, w)), _sds((1, w))],
        scratch_shapes=[pltpu.VMEM((ts + SC_HALO, w), F32), pltpu.VMEM((ts + SC_HALO, w), F32)],
        compiler_params=_cp(1))(dy, dy, dy, z, z, z, z, z, z, z, z, z, o, sc_w, sc_b)


def _ln_act(uc, sg, g, b):
    mu = jnp.mean(uc, axis=-1, keepdims=True)
    var = jnp.mean(jnp.square(uc - mu), axis=-1, keepdims=True)
    return _silu((uc - mu) * lax.rsqrt(var + EPS) * g + b) * _silu(sg)


def _odd_fwd(z, conv_w, conv_b, ln_g, ln_b, ts):
    s = z.shape[0]
    d = D_MODEL
    k = CONF_KERNEL

    def body(val_ref, glu_ref, sg_ref, hval_ref, hglu_ref, w_ref, b_ref, g_ref, beta_ref, y_ref, uc_ref, ubuf):
        i = pl.program_id(0)
        ubuf[0:CONF_HALO, :] = jnp.where(i > 0, hval_ref[...] * _sigmoid(hglu_ref[...]), 0.0)
        ubuf[CONF_HALO:, :] = val_ref[...] * _sigmoid(glu_ref[...])
        for r0 in range(0, ts, CONV_ROWS):
            acc = jnp.broadcast_to(b_ref[...], (CONV_ROWS, d))
            for j in range(k):
                acc = acc + w_ref[j:j + 1, :] * ubuf[pl.ds(r0 + CONF_HALO - (k - 1) + j, CONV_ROWS), :]
            uc_ref[r0:r0 + CONV_ROWS, :] = acc
        y_ref[...] = _ln_act(uc_ref[...], sg_ref[...], g_ref[...], beta_ref[...]).astype(y_ref.dtype)

    return pl.pallas_call(
        body, name="odd_fwd", grid=(s // ts,),
        in_specs=[_rows(ts, d, 0), _rows(ts, d, 1), _rows(ts, d, 2),
                  _prev_halo(ts, CONF_HALO, d, 0), _prev_halo(ts, CONF_HALO, d, 1),
                  _vec(d, 0, k), _vec(d), _vec(d), _vec(d)],
        out_specs=[_rows(ts, d), _rows(ts, d)], out_shape=[_sds((s, d), MXU_DTYPE), _sds((s, d))],
        scratch_shapes=[pltpu.VMEM((ts + CONF_HALO, d), F32)],
        compiler_params=_cp(1))(z, z, z, z, z, conv_w, conv_b, ln_g, ln_b)


def _odd_bwd(dy, z, uc, conv_w, ln_g, ln_b, ts):
    s = z.shape[0]
    d = D_MODEL
    k = CONF_KERNEL
    n = s // ts

    def body(dy_ref, dyn_ref, val_ref, glu_ref, sg_ref, sgn_ref, hval_ref, hglu_ref, uc_ref, ucn_ref,
             w_ref, g_ref, beta_ref, dz_ref, dw_ref, db_ref, dg_ref, dbeta_ref, ubuf, dbuf):
        i = pl.program_id(0)
        val, glu = val_ref[...], glu_ref[...]
        sig = _sigmoid(glu)
        ubuf[0:CONF_HALO, :] = jnp.where(i > 0, hval_ref[...] * _sigmoid(hglu_ref[...]), 0.0)
        ubuf[CONF_HALO:, :] = val * sig
        _, vjp = jax.vjp(_ln_act, uc_ref[...], sg_ref[...], g_ref[...], beta_ref[...])
        duc, dsg, dg, dbeta = vjp(dy_ref[...])
        _, vjp_n = jax.vjp(_ln_act, ucn_ref[...], sgn_ref[...], g_ref[...], beta_ref[...])
        dbuf[0:ts, :] = duc
        dbuf[ts:, :] = jnp.where(i < n - 1, vjp_n(dyn_ref[...])[0], 0.0)
        dz_ref[:, 2 * d:] = dsg

        @pl.when(i == 0)
        def _():
            dw_ref[...] = jnp.zeros_like(dw_ref)
            db_ref[...] = jnp.zeros_like(db_ref)
            dg_ref[...] = jnp.zeros_like(dg_ref)
            dbeta_ref[...] = jnp.zeros_like(dbeta_ref)

        db_ref[...] += jnp.sum(duc, axis=0, keepdims=True)
        dg_ref[...] += dg
        dbeta_ref[...] += dbeta
        for r0 in range(0, ts, CONV_ROWS):
            acc = jnp.zeros((CONV_ROWS, d), F32)
            for j in range(k):
                acc = acc + w_ref[j:j + 1, :] * dbuf[pl.ds(r0 + (k - 1) - j, CONV_ROWS), :]
            sig_r = sig[r0:r0 + CONV_ROWS, :]
            dz_ref[r0:r0 + CONV_ROWS, 0:d] = acc * sig_r
            dz_ref[r0:r0 + CONV_ROWS, d:2 * d] = acc * val[r0:r0 + CONV_ROWS, :] * sig_r * (1.0 - sig_r)
        for j in range(k):
            dw_ref[j:j + 1, :] += jnp.sum(duc * ubuf[pl.ds(CONF_HALO - (k - 1) + j, ts), :], axis=0, keepdims=True)

    return pl.pallas_call(
        body, name="odd_bwd", grid=(n,),
        in_specs=[_rows(ts, d), _next_halo(ts, CONF_HALO, d, 0, s),
                  _rows(ts, d, 0), _rows(ts, d, 1), _rows(ts, d, 2), _next_halo(ts, CONF_HALO, d, 2, s),
                  _prev_halo(ts, CONF_HALO, d, 0), _prev_halo(ts, CONF_HALO, d, 1),
                  _rows(ts, d), _next_halo(ts, CONF_HALO, d, 0, s),
                  _vec(d, 0, k), _vec(d), _vec(d)],
        out_specs=[_rows(ts, ODD_IN), _vec(d, 0, k), _vec(d), _vec(d), _vec(d)],
        out_shape=[_sds((s, ODD_IN)), _sds((k, d)), _sds((1, d)), _sds((1, d)), _sds((1, d))],
        scratch_shapes=[pltpu.VMEM((ts + CONF_HALO, d), F32), pltpu.VMEM((ts + CONF_HALO, d), F32)],
        compiler_params=_cp(1))(dy, dy, z, z, z, z, z, z, uc, uc, conv_w, ln_g, ln_b)


def _local_step(x, target, cos, sin, mod, p, layer_weights, fwd_dep=None, grads_done=None):
    s = x.shape[0]
    tsf, tsb = min(512, s // 2), min(256, s // 2)
    tq = min(512, s // 2)
    row1 = lambda a, i: a[i:i + 1]
    saved = []
    for layer in range(DEPTH):
        i = layer // 2
        mod_l = row1(mod, layer)
        wl = layer_weights(layer, x)
        h = _pre_fwd(x, row1(p["pre_norm_g"], layer), mod_l, tsf, fwd_dep if layer == 0 else None)
        if layer % 2 == 0:
            z = _mm(h, wl["w_in"], "nn", F32, 512, 1024, "even_in_fwd")
            q, k, v = _mla_prep_fwd(z, cos, sin, row1(p["even_q_norm_g"], i), row1(p["even_kv_norm_g"], i),
                                    wl["wq"], wl["wuk"], wl["wuv"], tsf)
            o, lse = _attn_fwd(q, k, v, tq)
            y = _even_gate_fwd(z, o, wl["sc_conv_w"], row1(p["even_sc_conv_b"], i), tsf)
            yo = _mm(y, wl["w_out"], "nn", F32, 512, 1024, "even_out_fwd")
            saved.append((x, h, z, y, yo, wl, (q, k, v, o, lse)))
        else:
            z = _mm(h, wl["w_in"], "nn", F32, 512, 1024, "odd_in_fwd")
            y, uc = _odd_fwd(z, wl["conv_w"], wl["conv_b"], wl["ln_g"], wl["ln_b"], tsf)
            yo = _mm(y, wl["w_out"], "nn", F32, 512, 1024, "odd_out_fwd")
            saved.append((x, h, z, y, yo, wl, uc))
        x = _post_fwd(x, yo, row1(p["post_norm_g"], layer), mod_l, tsf)

    loss, dx = _loss_fwd_bwd(x, target, tsf)

    g = {n: [None] * (DEPTH if n in ("pre_norm_g", "post_norm_g") else N_PAIRS) for n in (
        "pre_norm_g", "post_norm_g", "even_sc_conv_w", "even_sc_conv_b", "even_q_norm_g", "even_kv_norm_g",
        "odd_conv_w", "odd_conv_b", "odd_ln_g", "odd_ln_b")}
    dmod = [None] * DEPTH
    dep = None
    for layer in reversed(range(DEPTH)):
        i = layer // 2
        mod_l = row1(mod, layer)
        x_in, h, z, y, yo, wl, extra = saved[layer]
        dyo, dgate, g["post_norm_g"][layer] = _post_bwd(dx, yo, row1(p["post_norm_g"], layer), mod_l, tsb, dep)
        bufs = {}
        if layer % 2 == 0:
            q, k, v, o, lse = extra
            dy = _mm(dyo, wl["w_out"], "nt", F32, 512, 1024, "even_out_bwd_x")
            bufs["even_w_out"] = _mm_tn_shards(y, dyo, "rows", "even_out_bwd_w")
            dz, do, g["even_sc_conv_w"][i], g["even_sc_conv_b"][i] = _even_gate_bwd(
                dy, z, o, wl["sc_conv_w"], row1(p["even_sc_conv_b"], i), tsb)
            dq, dk, dv = _attn_bwd(q, k, v, do, o, lse, tq)
            dz, bufs["even_mla"], g["even_q_norm_g"][i], g["even_kv_norm_g"][i] = _mla_prep_bwd(
                dz, dq, dk, dv, z, cos, sin, row1(p["even_q_norm_g"], i), row1(p["even_kv_norm_g"], i),
                wl["wq"], wl["wuk"], wl["wuv"], tsb)
            dh = _mm(dz, wl["w_in"], "nt", F32, 256, 1024, "even_in_bwd_x")
            bufs["even_w_in"] = _ein_to_shards(_mm(h, dz, "tn", F32, 512, 512, "even_in_bwd_w"))
        else:
            uc = extra
            dy = _mm(dyo, wl["w_out"], "nt", F32, 512, 1024, "odd_out_bwd_x")
            bufs["odd_w_out"] = _mm_tn_shards(y, dyo, "rows", "odd_out_bwd_w")
            dz, g["odd_conv_w"][i], g["odd_conv_b"][i], g["odd_ln_g"][i], g["odd_ln_b"][i] = _odd_bwd(
                dy, z, uc, wl["conv_w"], wl["ln_g"], wl["ln_b"], tsb)
            dh = _mm(dz, wl["w_in"], "nt", F32, 256, 1024, "odd_in_bwd_x")
            bufs["odd_w_in"] = _mm_tn_shards(h, dz, "cols", "odd_in_bwd_w")
        dx, dshift, dscale, g["pre_norm_g"][layer] = _pre_bwd(dh, dx, x_in, row1(p["pre_norm_g"], layer), mod_l, tsb)
        dmod[layer] = jnp.concatenate([dshift, dscale, dgate], axis=-1)
        dep = grads_done(layer, bufs, dx) if grads_done is not None else None
    stack = lambda parts: jnp.stack([a[0] if a.shape[0] == 1 and a.ndim == 2 else a for a in parts])
    small = {n: stack(parts) for n, parts in g.items()}
    small["dmod"] = jnp.concatenate(dmod, axis=0)
    return loss, dx, small


def _uq_to_heads(w):
    w = w.reshape(N_CHIPS, Q_LORA, 2, QK_NOPE + QK_ROPE).transpose(0, 2, 1, 3).reshape(HEADS, Q_LORA, QK_NOPE + QK_ROPE)
    return jnp.pad(w, ((0, 0), (0, 0), (0, HEAD_PAD - QK_NOPE - QK_ROPE)))


def _ukv_to_heads(w):
    w = w.reshape(N_CHIPS, KV_LORA, 2, QK_NOPE + V_HEAD).transpose(0, 2, 1, 3).reshape(HEADS, KV_LORA, QK_NOPE + V_HEAD)
    wk = jnp.pad(w[..., :QK_NOPE], ((0, 0), (0, 0), (0, HEAD_PAD - QK_NOPE)))
    wv = w[..., QK_NOPE:]
    zero = jnp.zeros_like(wv)
    odd = (jnp.arange(HEADS) % 2 == 1)[:, None, None]
    wv = jnp.concatenate([jnp.where(odd, zero, wv), jnp.where(odd, wv, zero)], axis=-1)
    return wk, wv


def _mla_local(q):
    blocks = q.reshape(2, MLA_ROWS, HEAD_PAD)
    uq = jnp.concatenate([blocks[r, :Q_LORA, :QK_NOPE + QK_ROPE] for r in range(2)], axis=-1)
    ukv = jnp.concatenate(
        [jnp.concatenate([blocks[r, Q_LORA:Q_LORA + KV_LORA, :QK_NOPE],
                          blocks[r, Q_LORA + KV_LORA:, V_HEAD * r:V_HEAD * (r + 1)]], axis=-1) for r in range(2)], axis=-1)
    return uq, ukv


def _place():
    return lax.axis_index("x"), lax.axis_index("y"), lax.axis_index("c")


def _flip(v, bit):
    return 1 - v if bit else v


def _sem(a, k):
    return a * (N_CHIPS - 1) + k - 1


def _remote(src, dst, send_sem, recv_sem, peer):
    return pltpu.make_async_remote_copy(src_ref=src, dst_ref=dst, send_sem=send_sem, recv_sem=recv_sem,
                                        device_id=peer, device_id_type=MESH)


_VMEM_SPEC = pl.BlockSpec(memory_space=pltpu.VMEM)
_HBM_SPEC = pl.BlockSpec(memory_space=pl.ANY)


def _ada_fwd(c8, ada_w, ada_b_sh):
    depth, d, cols = ada_w.shape

    def body(c_ref, w_ref, b_ref, call_ref, mod_ref, s1, r1, s2, r2):
        x, y, c = _place()
        chip = 2 * x + y
        me = 2 * chip + c
        call_ref[me] = c_ref[...]
        sends = []
        for k in range(1, N_DEV):
            peer = (_flip(x, k & 4), _flip(y, k & 2), _flip(c, k & 1))
            cp = _remote(c_ref, call_ref.at[me], s1.at[k - 1], r1.at[k - 1], peer)
            cp.start()
            sends.append(cp)
        for k in range(1, N_DEV):
            src = 4 * _flip(x, k & 4) + 2 * _flip(y, k & 2) + _flip(c, k & 1)
            _remote(c_ref, call_ref.at[src], s1.at[k - 1], r1.at[k - 1], (x, y, c)).wait_recv()
        act = _silu(call_ref[...]).reshape(N_DEV * 8, d)
        for l in range(depth):
            mod_ref[chip, l] = _dot_nn(act, w_ref[l]) + b_ref[l:l + 1, :]
        for k in range(1, N_CHIPS):
            peer = (_flip(x, k & 2), _flip(y, k & 1), c)
            cp = _remote(mod_ref.at[chip], mod_ref.at[chip], s2.at[k - 1], r2.at[k - 1], peer)
            cp.start()
            sends.append(cp)
        for k in range(1, N_CHIPS):
            src = 2 * _flip(x, k & 2) + _flip(y, k & 1)
            _remote(mod_ref.at[src], mod_ref.at[src], s2.at[k - 1], r2.at[k - 1], (x, y, c)).wait_recv()
        for cp in sends:
            cp.wait_send()

    return pl.pallas_call(
        body, name="ada_fwd", in_specs=[_VMEM_SPEC] * 3, out_specs=[_VMEM_SPEC] * 2,
        out_shape=[_sds((N_DEV, 8, d)), _sds((N_CHIPS, depth, N_DEV * 8, cols))],
        scratch_shapes=[pltpu.SemaphoreType.DMA((N_DEV - 1,)), pltpu.SemaphoreType.DMA((N_DEV - 1,)),
                        pltpu.SemaphoreType.DMA((N_CHIPS - 1,)), pltpu.SemaphoreType.DMA((N_CHIPS - 1,))],
        compiler_params=pltpu.CompilerParams(vmem_limit_bytes=VMEM_LIMIT_V7X))(c8, ada_w, ada_b_sh)


def _ada_bwd(c_t, dmod_sh):
    depth, n, cols = dmod_sh.shape
    d = c_t.shape[0]
    tr = 256

    def body(c_ref, dm_ref, o_ref):
        act = _silu(c_ref[...])
        acc = act[:, 0:1] * dm_ref[0, 0:1, :]
        for e in range(1, n):
            acc = acc + act[:, e:e + 1] * dm_ref[0, e:e + 1, :]
        o_ref[0] = acc

    return pl.pallas_call(
        body, name="ada_bwd", grid=(depth, d // tr),
        in_specs=[pl.BlockSpec((tr, n), lambda l, i: (i, 0)), pl.BlockSpec((1, n, cols), lambda l, i: (l, 0, 0))],
        out_specs=pl.BlockSpec((1, tr, cols), lambda l, i: (l, i, 0)), out_shape=_sds((depth, d, cols)),
        compiler_params=_cp(2))(c_t, dmod_sh)


def _gathered_shape(shape, how):
    if how == "slot":
        return (N_CHIPS,) + shape
    r, cc = shape
    return (r, N_CHIPS * cc) if how == "cols" else (N_CHIPS * r, cc)


def _gathered_part(ref, shape, how, chip):
    if how == "slot":
        return ref.at[chip]
    if how == "cols":
        return ref.at[:, pl.ds(pl.multiple_of(chip * shape[1], 128), shape[1])]
    return ref.at[pl.ds(pl.multiple_of(chip * shape[0], 8), shape[0]), :]


_SEM_SPEC = pl.BlockSpec(memory_space=pltpu.SEMAPHORE)
_TOKEN = jax.ShapeDtypeStruct((8, 128), F32)
_SPLIT_COPY = pltpu.CompilerParams(has_side_effects=pltpu.SideEffectType.DATAFLOW_SIDE_EFFECTING)


def _in_hbm(a):
    return pltpu.with_memory_space_constraint(a, pltpu.HBM)


def _gather_start(items, gathered, name):
    n = len(items)

    def body(*refs):
        ins, outs, send_sems, recv_sems = refs[:n], refs[n:2 * n], refs[2 * n], refs[2 * n + 1]
        x, y, c = _place()
        for a in range(n):
            for k in range(1, N_CHIPS):
                part = _gathered_part(outs[a], items[a][0].shape, items[a][1], 2 * x + y)
                _remote(ins[a], part, send_sems.at[_sem(a, k)], recv_sems.at[_sem(a, k)],
                        (_flip(x, k & 2), _flip(y, k & 1), c)).start()
        refs[-1][...] = jnp.zeros(_TOKEN.shape, _TOKEN.dtype)

    arrays = [_in_hbm(a) for a, _ in items] + [_in_hbm(a) for a in gathered]
    res = pl.pallas_call(
        body, name=name, in_specs=[_HBM_SPEC] * (2 * n),
        out_specs=[_SEM_SPEC, _SEM_SPEC] + [_HBM_SPEC] * (2 * n) + [_VMEM_SPEC],
        out_shape=[pltpu.SemaphoreType.DMA((n * (N_CHIPS - 1),)), pltpu.SemaphoreType.DMA((n * (N_CHIPS - 1),))]
        + [pltpu.HBM(a.shape, a.dtype) for a in arrays] + [_TOKEN],
        input_output_aliases={a: 2 + a for a in range(2 * n)}, compiler_params=_SPLIT_COPY)(*arrays)
    return res[0], res[1], res[2:2 + n], res[2 + n:2 + 2 * n], res[-1]


def _gather_wait(items, started, after, name):
    n = len(items)
    send_sems, recv_sems, shards, gathered, _ = started

    def body(*refs):
        ins, outs, send_sems, recv_sems = refs[:n], refs[n:2 * n], refs[2 * n], refs[2 * n + 1]
        x, y, c = _place()
        for a in range(n):
            for k in range(1, N_CHIPS):
                part = _gathered_part(outs[a], items[a][0].shape, items[a][1], 2 * _flip(x, k & 2) + _flip(y, k & 1))
                cp = _remote(ins[a], part, send_sems.at[_sem(a, k)], recv_sems.at[_sem(a, k)], (x, y, c))
                cp.wait_send()
                cp.wait_recv()

    res = pl.pallas_call(
        body, name=name, in_specs=[_HBM_SPEC] * (2 * n) + [_SEM_SPEC, _SEM_SPEC] + [_HBM_SPEC] * len(after),
        out_specs=[_HBM_SPEC] * (2 * n), out_shape=[pltpu.HBM(a.shape, a.dtype) for a in (*shards, *gathered)],
        input_output_aliases={a: a for a in range(2 * n)}, compiler_params=_SPLIT_COPY)(
            *shards, *gathered, send_sems, recv_sems, *after)
    return res[n:]


def _rs_start(bufs, name, after=()):
    n = len(bufs)

    def body(*refs):
        srcs, lands = refs[:n], refs[n:2 * n]
        send_sems, recv_sems = refs[2 * n + len(after)], refs[2 * n + len(after) + 1]
        x, y, c = _place()
        for a in range(n):
            for k in range(1, N_CHIPS):
                tx, ty = _flip(x, k & 2), _flip(y, k & 1)
                _remote(srcs[a].at[2 * tx + ty], lands[a].at[k - 1], send_sems.at[_sem(a, k)], recv_sems.at[_sem(a, k)],
                        (tx, ty, c)).start()
        refs[-1][...] = jnp.zeros(_TOKEN.shape, _TOKEN.dtype)

    arrays = [_in_hbm(b) for b in bufs] + [_in_hbm(lax.empty((N_CHIPS - 1,) + b.shape[1:], b.dtype)) for b in bufs]
    res = pl.pallas_call(
        body, name=name, in_specs=[_HBM_SPEC] * (2 * n + len(after)),
        out_specs=[_SEM_SPEC, _SEM_SPEC] + [_HBM_SPEC] * (2 * n) + [_VMEM_SPEC],
        out_shape=[pltpu.SemaphoreType.DMA((n * (N_CHIPS - 1),)), pltpu.SemaphoreType.DMA((n * (N_CHIPS - 1),))]
        + [pltpu.HBM(a.shape, a.dtype) for a in arrays] + [_TOKEN],
        input_output_aliases={a: 2 + a for a in range(2 * n)}, compiler_params=_SPLIT_COPY)(*arrays, *after)
    return res[0], res[1], res[2:2 + n], res[2 + n:2 + 2 * n], res[-1]


def _rs_wait(started, after, name):
    send_sems, recv_sems, bufs, lands, _ = started
    n = len(bufs)

    def body(*refs):
        srcs, lnds, send_sems, recv_sems = refs[:n], refs[n:2 * n], refs[2 * n], refs[2 * n + 1]
        x, y, c = _place()
        for a in range(n):
            for k in range(1, N_CHIPS):
                cp = _remote(srcs[a].at[0], lnds[a].at[k - 1], send_sems.at[_sem(a, k)], recv_sems.at[_sem(a, k)], (x, y, c))
                cp.wait_send()
                cp.wait_recv()

    res = pl.pallas_call(
        body, name=name, in_specs=[_HBM_SPEC] * (2 * n) + [_SEM_SPEC, _SEM_SPEC] + [_HBM_SPEC] * len(after),
        out_specs=[_HBM_SPEC] * (2 * n), out_shape=[pltpu.HBM(a.shape, a.dtype) for a in (*bufs, *lands)],
        input_output_aliases={a: a for a in range(2 * n)}, compiler_params=_SPLIT_COPY)(
            *bufs, *lands, send_sems, recv_sems, *after)
    return res[:n], res[n:]


def _gather_chips(items, own_only=(), after=()):
    n, n_all = len(items), len(items) + len(own_only)
    items = list(items) + list(own_only)
    arrays = [a for a, _ in items]

    def body(*refs):
        ins, outs = refs[:n_all], refs[-n_all - 3:-3]
        send_sems, recv_sems, local_sems = refs[-3:]
        x, y, c = _place()
        chip = 2 * x + y
        part = lambda a, j: _gathered_part(outs[a], items[a][0].shape, items[a][1], j)
        local = [pltpu.make_async_copy(ins[a], part(a, chip), local_sems.at[a]) for a in range(n_all)]
        for cp in local[:n]:
            cp.start()
        sends = []
        for a in range(n):
            for k in range(1, N_CHIPS):
                peer = (_flip(x, k & 2), _flip(y, k & 1), c)
                cp = _remote(ins[a], part(a, chip), send_sems.at[_sem(a, k)], recv_sems.at[_sem(a, k)], peer)
                cp.start()
                sends.append(cp)
        for cp in local[n:]:
            cp.start()
        for a in range(n):
            for k in range(1, N_CHIPS):
                src = 2 * _flip(x, k & 2) + _flip(y, k & 1)
                _remote(ins[a], part(a, src), send_sems.at[_sem(a, k)], recv_sems.at[_sem(a, k)], (x, y, c)).wait_recv()
        for cp in sends:
            cp.wait_send()
        for cp in local:
            cp.wait()

    return pl.pallas_call(
        body, name="gather_chips", in_specs=[_HBM_SPEC] * (n_all + len(after)), out_specs=[_HBM_SPEC] * n_all,
        out_shape=[_sds(_gathered_shape(a.shape, how), a.dtype) for a, how in items],
        scratch_shapes=[pltpu.SemaphoreType.DMA((n * (N_CHIPS - 1),)), pltpu.SemaphoreType.DMA((n * (N_CHIPS - 1),)),
                        pltpu.SemaphoreType.DMA((n_all,))])(*arrays, *after)


def _gather_sum_all(small):
    r, w = small.shape

    def body(in_ref, all_ref, sum_ref, send_sems, recv_sems):
        x, y, c = _place()
        me = 4 * x + 2 * y + c
        all_ref[me] = in_ref[...]
        sends = []
        for k in range(1, N_DEV):
            peer = (_flip(x, k & 4), _flip(y, k & 2), _flip(c, k & 1))
            cp = _remote(in_ref, all_ref.at[me], send_sems.at[k - 1], recv_sems.at[k - 1], peer)
            cp.start()
            sends.append(cp)
        for k in range(1, N_DEV):
            src = 4 * _flip(x, k & 4) + 2 * _flip(y, k & 2) + _flip(c, k & 1)
            _remote(in_ref, all_ref.at[src], send_sems.at[k - 1], recv_sems.at[k - 1], (x, y, c)).wait_recv()
        acc = all_ref[0]
        for e in range(1, N_DEV):
            acc = acc + all_ref[e]
        sum_ref[...] = acc
        for cp in sends:
            cp.wait_send()

    return pl.pallas_call(
        body, name="gather_sum_all", in_specs=[_VMEM_SPEC], out_specs=[_VMEM_SPEC] * 2,
        out_shape=[_sds((N_DEV, r, w)), _sds((r, w))],
        scratch_shapes=[pltpu.SemaphoreType.DMA((N_DEV - 1,)), pltpu.SemaphoreType.DMA((N_DEV - 1,))],
        compiler_params=pltpu.CompilerParams(vmem_limit_bytes=VMEM_LIMIT_V7X))(small)


def _add_chips(buf, t, chip_idx):
    r, cc = buf.shape[1:]
    tr = min(256, r)

    def body(c_ref, p_ref, t_ref, o_ref):
        del c_ref
        o_ref[...] = p_ref[0] + t_ref[0].astype(F32) + t_ref[1].astype(F32) + t_ref[2].astype(F32)

    return pl.pallas_call(
        body, name="add_chips", out_shape=_sds((r, cc)),
        grid_spec=pltpu.PrefetchScalarGridSpec(
            num_scalar_prefetch=1, grid=(r // tr,),
            in_specs=[pl.BlockSpec((1, tr, cc), lambda i, c: (c[0], i, 0)),
                      pl.BlockSpec((N_CHIPS - 1, tr, cc), lambda i, c: (0, i, 0))],
            out_specs=pl.BlockSpec((tr, cc), lambda i, c: (i, 0))),
        compiler_params=_cp(1))(chip_idx, buf, t)


def _rs_sibling(qs):
    n = len(qs)

    def body(*refs):
        ins, outs = refs[:n], refs[n:2 * n]
        send_sems, recv_sems = refs[2 * n:]
        x, y, c = _place()
        copies = [_remote(ins[a], outs[a], send_sems.at[a], recv_sems.at[a], (x, y, 1 - c)) for a in range(n)]
        for cp in copies:
            cp.start()
        for cp in copies:
            cp.wait()

    return pl.pallas_call(
        body, name="rs_sibling", in_specs=[_HBM_SPEC] * n, out_specs=[_HBM_SPEC] * n,
        out_shape=[_sds(q.shape) for q in qs],
        scratch_shapes=[pltpu.SemaphoreType.DMA((n,)), pltpu.SemaphoreType.DMA((n,))])(*qs)


def _adamw_update(w, g, m, v):
    m = ADAM_B1 * m + (1.0 - ADAM_B1) * g
    v = ADAM_B2 * v + (1.0 - ADAM_B2) * jnp.square(g)
    m_hat = m / (1.0 - ADAM_B1 ** ADAM_STEP)
    v_hat = v / (1.0 - ADAM_B2 ** ADAM_STEP)
    return -ADAM_LR * (m_hat / (jnp.sqrt(v_hat) + ADAM_EPS) + ADAM_WD * w), m, v


def _adamw(w, g_parts, m, v, name):
    shape = w.shape
    cols = shape[-1]
    rows = _size(shape[:-1])
    tr = 512 if rows % 512 == 0 else rows
    spec = pl.BlockSpec((tr, cols), lambda i: (i, 0))
    n = len(g_parts)

    def body(*refs):
        w_ref, m_ref, v_ref = refs[:3]
        g_ref, d_ref, nm_ref, nv_ref = refs[3 + n:]
        g = refs[3][...]
        for r in refs[4:3 + n]:
            g = g + r[...]
        g_ref[...] = g
        d_ref[...], nm_ref[...], nv_ref[...] = _adamw_update(w_ref[...], g, m_ref[...], v_ref[...])

    outs = pl.pallas_call(
        body, name="adamw_" + name, grid=(rows // tr,), in_specs=[spec] * (3 + n), out_specs=[spec] * 4,
        out_shape=[_sds((rows, cols))] * 4, compiler_params=_cp(1))(
            *[a.reshape(rows, cols) for a in (w, m, v, *g_parts)])
    return tuple(o.reshape(shape) for o in outs)


def _adamw_layer(w, g_parts, m, v, layer, prev, name):
    _, r, cc = w.shape
    tr = 512 if r % 512 == 0 else r
    spec = pl.BlockSpec((1, tr, cc), lambda i: (layer, i, 0))
    n = len(g_parts)

    def body(*refs):
        w_ref, m_ref, v_ref = refs[:3]
        g_ref, d_ref, nm_ref, nv_ref = refs[-4:]
        g = refs[3][...]
        for q in refs[4:3 + n]:
            g = g + q[...]
        g = g[:, :cc]
        g_ref[0] = g
        d_ref[0], nm_ref[0], nv_ref[0] = _adamw_update(w_ref[0], g, m_ref[0], v_ref[0])

    g_specs = [pl.BlockSpec((tr, q.shape[1]), lambda i: (i, 0)) for q in g_parts]
    passed = () if prev is None else tuple(prev)
    return pl.pallas_call(
        body, name="adamw_" + name, grid=(r // tr,),
        in_specs=[spec] * 3 + g_specs + [_HBM_SPEC] * len(passed), out_specs=[spec] * 4,
        out_shape=[_sds(w.shape)] * 4, input_output_aliases={3 + n + k: k for k in range(len(passed))},
        compiler_params=_cp(1))(w, m, v, *g_parts, *passed)


def _size(shape):
    n = 1
    for s in shape:
        n *= s
    return n


_SMALL = (("dmod", (DEPTH, 3 * D_MODEL)), ("pre_norm_g", (DEPTH, D_MODEL)), ("post_norm_g", (DEPTH, D_MODEL)),
          ("even_sc_conv_w", (2, SC_KERNEL, SC_WIDTH)), ("even_sc_conv_b", (2, SC_WIDTH)),
          ("even_q_norm_g", (2, Q_LORA)), ("even_kv_norm_g", (2, KV_LORA)),
          ("odd_conv_w", (2, CONF_KERNEL, D_MODEL)), ("odd_conv_b", (2, D_MODEL)), ("odd_ln_g", (2, D_MODEL)),
          ("odd_ln_b", (2, D_MODEL)))
SMALL_ROWS = -(-sum(_size(s) for _, s in _SMALL) // (8 * 128)) * 8

_SMALL_W = (("even_sc_conv_w", (2, SC_KERNEL, SC_WIDTH // N_CHIPS)), ("odd_conv_w", (2, CONF_KERNEL, D_MODEL // N_CHIPS)),
            ("odd_conv_b", (2, D_MODEL // N_CHIPS)), ("odd_ln_g", (2, D_MODEL // N_CHIPS)),
            ("odd_ln_b", (2, D_MODEL // N_CHIPS)))
SMALL_W_ROWS = -(-sum(_size(s) for _, s in _SMALL_W) // (8 * 128)) * 8


def _pack_rows(arrays, layout, rows):
    flat = jnp.concatenate([arrays[n].reshape(-1) for n, _ in layout])
    return jnp.pad(flat, (0, rows * 128 - flat.shape[0])).reshape(rows, 128)


def _unpack_small(t):
    flat = t.reshape(-1)
    out, at = {}, 0
    for n, shape in _SMALL:
        out[n] = flat[at:at + _size(shape)].reshape(shape)
        at += _size(shape)
    return out


def _unpack_small_w(t):
    flat = t.reshape(N_CHIPS, -1)
    out, at = {}, 0
    for n, shape in _SMALL_W:
        a = flat[:, at:at + _size(shape)].reshape((N_CHIPS,) + shape)
        out[n] = jnp.moveaxis(a, 0, -2).reshape(shape[:-1] + (N_CHIPS * shape[-1],))
        at += _size(shape)
    return out


def _chip_cols(a, chip):
    n = a.shape[-1] // N_CHIPS
    return lax.dynamic_slice_in_dim(a, chip * n, n, axis=a.ndim - 1)


def _join_cols(a):
    _, l, r, cc = a.shape
    return a.transpose(1, 2, 0, 3).reshape(l, r, N_CHIPS * cc)


WEIGHT_NAMES = ("ada_w", "ada_b", "pre_norm_g", "post_norm_g", "even_w_in", "even_sc_conv_w", "even_sc_conv_b",
                "even_q_norm_g", "even_kv_norm_g", "even_w_uq", "even_w_ukv", "even_w_out", "odd_w_in", "odd_conv_w",
                "odd_conv_b", "odd_ln_g", "odd_ln_b", "odd_w_out")
GATHER_HOW = ((("even_w_in", "slot"), ("even_w_uq", "slot"), ("even_w_ukv", "slot"), ("even_w_out", "rows")),
              (("odd_w_in", "cols"), ("odd_w_out", "rows")))


def kernel(x, c, positions, ada_w, ada_b, pre_norm_g, post_norm_g, even_w_in, even_sc_conv_w, even_sc_conv_b, even_q_norm_g, even_kv_norm_g, even_w_uq, even_w_ukv, even_w_out, odd_w_in, odd_conv_w, odd_conv_b, odd_ln_g, odd_ln_b, odd_w_out, loss_target, m_ada_w, m_ada_b, m_pre_norm_g, m_post_norm_g, m_even_w_in, m_even_sc_conv_w, m_even_sc_conv_b, m_even_q_norm_g, m_even_kv_norm_g, m_even_w_uq, m_even_w_ukv, m_even_w_out, m_odd_w_in, m_odd_conv_w, m_odd_conv_b, m_odd_ln_g, m_odd_ln_b, m_odd_w_out, v_ada_w, v_ada_b, v_pre_norm_g, v_post_norm_g, v_even_w_in, v_even_sc_conv_w, v_even_sc_conv_b, v_even_q_norm_g, v_even_kv_norm_g, v_even_w_uq, v_even_w_ukv, v_even_w_out, v_odd_w_in, v_odd_conv_w, v_odd_conv_b, v_odd_ln_g, v_odd_ln_b, v_odd_w_out):
    w = dict(zip(WEIGHT_NAMES, (ada_w, ada_b, pre_norm_g, post_norm_g, even_w_in, even_sc_conv_w, even_sc_conv_b,
                                even_q_norm_g, even_kv_norm_g, even_w_uq, even_w_ukv, even_w_out, odd_w_in, odd_conv_w,
                                odd_conv_b, odd_ln_g, odd_ln_b, odd_w_out)))
    m = dict(zip(WEIGHT_NAMES, (m_ada_w, m_ada_b, m_pre_norm_g, m_post_norm_g, m_even_w_in, m_even_sc_conv_w,
                                m_even_sc_conv_b, m_even_q_norm_g, m_even_kv_norm_g, m_even_w_uq, m_even_w_ukv,
                                m_even_w_out, m_odd_w_in, m_odd_conv_w, m_odd_conv_b, m_odd_ln_g, m_odd_ln_b, m_odd_w_out)))
    v = dict(zip(WEIGHT_NAMES, (v_ada_w, v_ada_b, v_pre_norm_g, v_post_norm_g, v_even_w_in, v_even_sc_conv_w,
                                v_even_sc_conv_b, v_even_q_norm_g, v_even_kv_norm_g, v_even_w_uq, v_even_w_ukv,
                                v_even_w_out, v_odd_w_in, v_odd_conv_w, v_odd_conv_b, v_odd_ln_g, v_odd_ln_b, v_odd_w_out)))
    ix, iy, ic = _place()
    chip = 2 * ix + iy
    me = 2 * chip + ic
    s = x.shape[1]

    c_all, mod_all = _ada_fwd(jnp.broadcast_to(c, (8, D_MODEL)), ada_w, _chip_cols(ada_b, chip))
    mod = lax.dynamic_index_in_dim(mod_all, 8 * me, axis=2, keepdims=False)
    mod = mod.transpose(1, 0, 2).reshape(DEPTH, 3 * D_MODEL)

    items = [[(w[n][layer // 2].astype(MXU_DTYPE), how) for n, how in GATHER_HOW[layer % 2]] for layer in range(DEPTH)]
    later_items = [item for layer_items in items[1:] for item in layer_items]
    first = _gather_chips(items[0] + [(_pack_rows(w, _SMALL_W, SMALL_W_ROWS), "slot")], later_items, [mod_all])
    small_w = _unpack_small_w(first[len(items[0])])
    weights_sent = _gather_start(later_items, first[len(items[0]) + 1:], "gather_start")
    later = []

    def layer_weights(layer, x_in):
        i = layer // 2
        if layer == 0:
            arrays = first[:len(items[0])]
        else:
            if not later:
                later.extend(_gather_wait(later_items, weights_sent, [x_in], "gather_wait"))
            at = sum(len(layer_items) for layer_items in items[1:layer])
            arrays = later[at:at + len(items[layer])]
        if layer % 2 == 0:
            ein, uq, ukv, eout = arrays
            wuk, wuv = _ukv_to_heads(ukv)
            return {"w_in": _ein_from_shards(ein), "wq": _uq_to_heads(uq), "wuk": wuk, "wuv": wuv, "w_out": eout,
                    "sc_conv_w": small_w["even_sc_conv_w"][i]}
        oin, oout = arrays
        return {"w_in": oin, "w_out": oout, "conv_w": small_w["odd_conv_w"][i], "conv_b": small_w["odd_conv_b"][i:i + 1],
                "ln_g": small_w["odd_ln_g"][i:i + 1], "ln_b": small_w["odd_ln_b"][i:i + 1]}

    in_flight, own, sib, last = {}, {}, {}, {}

    def land(layer, after):
        names, started, kept = in_flight.pop(layer)
        bufs, arrived = _rs_wait(started, after, "rs_wait_%d" % layer)
        sums = [_add_chips(b, t, chip.reshape(1)) for b, t in zip(bufs if kept is None else kept, arrived)]
        for n, mine, theirs in zip(names, sums, _rs_sibling(sums)):
            own[n, layer // 2], sib[n, layer // 2] = mine, theirs

    def grads_done(layer, bufs, dx_in):
        if layer + 1 in in_flight:
            land(layer + 1, [dx_in])
        if layer == 0:
            last.update(bufs)
            return None
        names = sorted(bufs)
        in_flight[layer] = (names, _rs_start([bufs[n] for n in names], "rs_start_%d" % layer), None)
        return in_flight[layer][1][-1]

    p = {"pre_norm_g": pre_norm_g, "post_norm_g": post_norm_g, "even_sc_conv_b": even_sc_conv_b,
         "even_q_norm_g": even_q_norm_g, "even_kv_norm_g": even_kv_norm_g}
    inv_freq = 1.0 / (ROPE_THETA ** (jnp.arange(0, QK_ROPE, 2, dtype=F32) / QK_ROPE))
    inv_freq = jnp.zeros((1, HEAD_PAD), F32).at[0, QK_NOPE:QK_NOPE + QK_ROPE].set(jnp.tile(inv_freq, 2))
    cos, sin = _rope_tables(positions.reshape(s, 1), inv_freq)

    loss, dx, g = _local_step(x[0], loss_target[0], cos, sin, mod, p, layer_weights, weights_sent[-1], grads_done)

    grads, deltas, new_m, new_v = {}, {}, {}, {}

    def update_layers(n, results, pairs):
        for i in pairs:
            results = _adamw_layer(w[n], [own[n, i], sib[n, i]], m[n], v[n], i, results, n)
        return results

    small_all, small_sum = _gather_sum_all(_pack_rows(g, _SMALL, SMALL_ROWS))
    names = sorted(last)
    kept = [last[n] for n in names]
    in_flight[0] = (names, _rs_start([b.astype(jnp.bfloat16) for b in kept], "rs_start_0", [small_sum]), kept)
    tot = _unpack_small(small_sum)
    dmod_all = small_all[:, :DEPTH * 3 * D_MODEL // 128].reshape(N_DEV, DEPTH, 3 * D_MODEL)
    grads["ada_w"] = _ada_bwd(c_all[:, 0, :].T, _chip_cols(dmod_all, chip).transpose(1, 0, 2))
    grads["ada_b"] = tot["dmod"]
    for n in ("pre_norm_g", "post_norm_g", "even_sc_conv_b", "even_q_norm_g", "even_kv_norm_g"):
        grads[n] = tot[n]
    for n in ("even_sc_conv_w", "odd_conv_w", "odd_conv_b", "odd_ln_g", "odd_ln_b"):
        grads[n] = _chip_cols(tot[n], chip)
    for n in list(grads):
        _, deltas[n], new_m[n], new_v[n] = _adamw(w[n], [grads[n]], m[n], v[n], n)

    for n in ("odd_w_in", "odd_w_out"):
        grads[n], deltas[n], new_m[n], new_v[n] = update_layers(n, None, (1, 0))
    partly = {n: update_layers(n, None, (1,)) for n in ("even_w_in", "even_w_out")}
    land(0, [deltas["ada_w"], deltas["odd_w_in"], partly["even_w_in"][1]])
    for n in ("even_w_in", "even_w_out"):
        grads[n], deltas[n], new_m[n], new_v[n] = update_layers(n, partly[n], (0,))
    uq_parts, ukv_parts = zip(*[[jnp.stack(part) for part in zip(*[_mla_local(q["even_mla", i]) for i in range(N_PAIRS)])]
                                for q in (own, sib)])
    for n, parts in (("even_w_uq", uq_parts), ("even_w_ukv", ukv_parts)):
        grads[n], deltas[n], new_m[n], new_v[n] = _adamw(w[n], list(parts), m[n], v[n], n)

    total_loss = lax.psum(loss[0, 0], ("x", "y", "c"))
    return (total_loss, dx[None], *[grads[n] for n in WEIGHT_NAMES], *[deltas[n] for n in WEIGHT_NAMES],
            *[new_m[n] for n in WEIGHT_NAMES], *[new_v[n] for n in WEIGHT_NAMES])
```

```python
import functools

import jax
import jax.numpy as jnp
from jax import lax
from jax.experimental import pallas as pl
from jax.experimental.pallas import tpu as pltpu

F32 = jnp.float32
MXU_DTYPE = jnp.bfloat16
MESH = pl.DeviceIdType.MESH
VMEM_LIMIT_V7X = 56 * 2 ** 20

EPS = 1e-6
D_MODEL = 1024
DEPTH = 4
CHUNK = 64
SC_WIDTH = 512
SC_KERNEL = 3
SC_HALO = 8
HEADS = 8
QK_NOPE = 64
QK_ROPE = 32
V_HEAD = 64
HEAD_PAD = 128
Q_LORA = 256
KV_LORA = 128
ROPE_THETA = 10000.0
CONF_KERNEL = 31
CONF_HALO = 32
CONV_ROWS = 32
SUBLANES = 8
EVEN_IN = 2976
EVEN_PAD = 3072
ODD_IN = 3072
N_CHIPS = 4
N_DEV = 8
NEG = -1e30

ADAM_LR = 0.001
ADAM_B1 = 0.9
ADAM_B2 = 0.999
ADAM_EPS = 1e-08
ADAM_WD = 0.01
ADAM_STEP = 10

N_PAIRS = DEPTH // 2
EVEN_SHARD = EVEN_IN // N_CHIPS
EVEN_SHARD_PAD = 768
MLA_ROWS = Q_LORA + 2 * KV_LORA


def _cp(n_grid=0, **kw):
    return pltpu.CompilerParams(dimension_semantics=("arbitrary",) * n_grid,
                                vmem_limit_bytes=VMEM_LIMIT_V7X, **kw)


def _sigmoid(x):
    return 1.0 / (1.0 + jnp.exp(-x))


def _silu(x):
    return x * _sigmoid(x)


def _dsilu(x):
    s = _sigmoid(x)
    return s * (1.0 + x * (1.0 - s))


def _rms(x, g):
    return x * lax.rsqrt(jnp.mean(x * x, axis=-1, keepdims=True) + EPS) * g


def _dot(a, b, dims):
    return lax.dot_general(a.astype(MXU_DTYPE), b.astype(MXU_DTYPE), (dims, ((), ())),
                           preferred_element_type=F32)


def _dot_nn(a, b):
    return _dot(a, b, ((1,), (0,)))


def _dot_nt(a, b):
    return _dot(a, b, ((1,), (1,)))


def _dot_tn(a, b):
    return _dot(a, b, ((0,), (0,)))


def _rows(ts, w, cb=0):
    return pl.BlockSpec((ts, w), lambda i: (i, cb))


def _vec(w, cb=0, r=1):
    return pl.BlockSpec((r, w), lambda i: (0, cb))


def _prev_halo(ts, hr, w, cb):
    return pl.BlockSpec((hr, w), lambda i: (jnp.maximum(i * (ts // hr) - 1, 0), cb))


def _next_halo(ts, hr, w, cb, s):
    return pl.BlockSpec((hr, w), lambda i: (jnp.minimum((i + 1) * (ts // hr), s // hr - 1), cb))


def _sds(shape, dtype=F32):
    return jax.ShapeDtypeStruct(shape, dtype)


def _mm(a, b, mode, out_dtype, tm, tn, name):
    tm = min(tm, a.shape[1] if mode == "tn" else a.shape[0])
    tn = min(tn, b.shape[0] if mode == "nt" else b.shape[1])
    if mode == "nn":
        (m, k), n = a.shape, b.shape[1]
        a_spec = pl.BlockSpec((tm, k), lambda i, j: (i, 0))
        b_spec = pl.BlockSpec((k, tn), lambda i, j: (0, j))
        dot = _dot_nn
    elif mode == "nt":
        (m, k), n = a.shape, b.shape[0]
        a_spec = pl.BlockSpec((tm, k), lambda i, j: (i, 0))
        b_spec = pl.BlockSpec((tn, k), lambda i, j: (j, 0))
        dot = _dot_nt
    else:
        (k, m), n = a.shape, b.shape[1]
        a_spec = pl.BlockSpec((k, tm), lambda i, j: (0, i))
        b_spec = pl.BlockSpec((k, tn), lambda i, j: (0, j))
        dot = _dot_tn
    assert m % tm == 0 and n % tn == 0, (name, m, n, tm, tn)

    def body(a_ref, b_ref, o_ref):
        o_ref[...] = dot(a_ref[...], b_ref[...]).astype(o_ref.dtype)

    return pl.pallas_call(
        body, name=name, grid=(m // tm, n // tn), in_specs=[a_spec, b_spec],
        out_specs=pl.BlockSpec((tm, tn), lambda i, j: (i, j)), out_shape=_sds((m, n), out_dtype),
        compiler_params=_cp(2))(a, b)


def _mm_tn_shards(a, b, by, name):
    k, m = a.shape
    n = b.shape[1]
    if by == "cols":
        tm, tn = 512, n // N_CHIPS
        shape, grid = (N_CHIPS, m, tn), (m // tm, N_CHIPS)
        out_spec = pl.BlockSpec((1, tm, tn), lambda i, j: (j, i, 0))
    else:
        tm, tn = m // N_CHIPS, 512
        shape, grid = (N_CHIPS, tm, n), (N_CHIPS, n // tn)
        out_spec = pl.BlockSpec((1, tm, tn), lambda i, j: (i, 0, j))

    def body(a_ref, b_ref, o_ref):
        o_ref[0] = _dot_tn(a_ref[...], b_ref[...])

    return pl.pallas_call(
        body, name=name, grid=grid,
        in_specs=[pl.BlockSpec((k, tm), lambda i, j: (0, i)), pl.BlockSpec((k, tn), lambda i, j: (0, j))],
        out_specs=out_spec, out_shape=_sds(shape), compiler_params=_cp(2))(a, b)


def _even_col(q):
    return q if q < 2432 else (q + 64 if q < 2464 else q + 96)


def _shard_pieces(j):
    lo, hi = EVEN_SHARD * j, EVEN_SHARD * (j + 1)
    cuts = [lo] + [b for b in (2432, 2464) if lo < b < hi] + [hi]
    return [(a - lo, _even_col(a), b - a) for a, b in zip(cuts[:-1], cuts[1:])]


def _ein_from_shards(w):
    _, d, _ = w.shape
    tr = 256

    def body(w_ref, o_ref):
        parts, at = [], 0
        for j in range(N_CHIPS):
            for d0, s0, n in _shard_pieces(j):
                if s0 > at:
                    parts.append(jnp.zeros((tr, s0 - at), F32))
                parts.append(w_ref[j, :, d0:d0 + n].astype(F32))
                at = s0 + n
        o_ref[...] = jnp.concatenate(parts, axis=1).astype(o_ref.dtype)

    return pl.pallas_call(
        body, name="ein_from_shards", grid=(d // tr,),
        in_specs=[pl.BlockSpec((N_CHIPS, tr, EVEN_SHARD), lambda i: (0, i, 0))],
        out_specs=_rows(tr, EVEN_PAD), out_shape=_sds((d, EVEN_PAD), w.dtype), compiler_params=_cp(1))(w)


def _ein_to_shards(dw):
    d = dw.shape[0]
    tr = 256

    def body(dw_ref, o_ref):
        for j in range(N_CHIPS):
            parts = [dw_ref[:, s0:s0 + n] for _, s0, n in _shard_pieces(j)]
            o_ref[j] = jnp.concatenate(parts + [jnp.zeros((tr, EVEN_SHARD_PAD - EVEN_SHARD), F32)], axis=1)

    return pl.pallas_call(
        body, name="ein_to_shards", grid=(d // tr,), in_specs=[_rows(tr, EVEN_PAD)],
        out_specs=pl.BlockSpec((N_CHIPS, tr, EVEN_SHARD_PAD), lambda i: (0, i, 0)),
        out_shape=_sds((N_CHIPS, d, EVEN_SHARD_PAD)), compiler_params=_cp(1))(dw)


def _rope_tables(pos_col, invf):
    s = pos_col.shape[0]
    ts = min(512, s)

    def body(p_ref, f_ref, c_ref, s_ref):
        ang = p_ref[...].astype(F32) * f_ref[...]
        lane = lax.broadcasted_iota(jnp.int32, ang.shape, 1)
        rope = (lane >= QK_NOPE) & (lane < QK_NOPE + QK_ROPE)
        c_ref[...] = jnp.where(lane < QK_NOPE, 1.0, jnp.where(rope, jnp.cos(ang), 0.0))
        s_ref[...] = jnp.where(rope, jnp.sin(ang), 0.0)

    return pl.pallas_call(
        body, name="rope_tables", grid=(s // ts,), in_specs=[_rows(ts, 1), _vec(HEAD_PAD)],
        out_specs=[_rows(ts, HEAD_PAD)] * 2, out_shape=[_sds((s, HEAD_PAD))] * 2,
        compiler_params=_cp(1))(pos_col, invf)


def _after(dep):
    return () if dep is None else (dep,)


def _pre_fwd(x, g, mod_l, ts, dep=None):
    s, d = x.shape

    def body(x_ref, g_ref, sh_ref, sc_ref, *rest):
        h = _rms(x_ref[...], g_ref[...]) * (1.0 + sc_ref[...]) + sh_ref[...]
        rest[-1][...] = h.astype(rest[-1].dtype)

    return pl.pallas_call(
        body, name="pre_fwd", grid=(s // ts,),
        in_specs=[_rows(ts, d), _vec(d), _vec(d, 0), _vec(d, 1)] + [_HBM_SPEC] * len(_after(dep)),
        out_specs=_rows(ts, d), out_shape=_sds((s, d), MXU_DTYPE), compiler_params=_cp(1))(
            x, g, mod_l, mod_l, *_after(dep))


def _pre_bwd(dh, dx_out, x, g, mod_l, ts):
    s, d = x.shape

    def f(xv, gv, sh, sc):
        return _rms(xv, gv) * (1.0 + sc) + sh

    def body(dh_ref, dxo_ref, x_ref, g_ref, sh_ref, sc_ref, dx_ref, dsh_ref, dsc_ref, dg_ref):
        i = pl.program_id(0)
        _, vjp = jax.vjp(f, x_ref[...], g_ref[...], sh_ref[...], sc_ref[...])
        dx, dg, dsh, dsc = vjp(dh_ref[...])
        dx_ref[...] = dxo_ref[...] + dx

        @pl.when(i == 0)
        def _():
            dsh_ref[...] = jnp.zeros_like(dsh_ref)
            dsc_ref[...] = jnp.zeros_like(dsc_ref)
            dg_ref[...] = jnp.zeros_like(dg_ref)

        dsh_ref[...] += dsh
        dsc_ref[...] += dsc
        dg_ref[...] += dg

    return pl.pallas_call(
        body, name="pre_bwd", grid=(s // ts,),
        in_specs=[_rows(ts, d), _rows(ts, d), _rows(ts, d), _vec(d), _vec(d, 0), _vec(d, 1)],
        out_specs=[_rows(ts, d), _vec(d), _vec(d), _vec(d)],
        out_shape=[_sds((s, d)), _sds((1, d)), _sds((1, d)), _sds((1, d))],
        compiler_params=_cp(1))(dh, dx_out, x, g, mod_l, mod_l)


def _post_fwd(x, yo, g, mod_l, ts):
    s, d = x.shape

    def body(x_ref, yo_ref, g_ref, gate_ref, o_ref):
        o_ref[...] = x_ref[...] + gate_ref[...] * _rms(yo_ref[...], g_ref[...])

    return pl.pallas_call(
        body, name="post_fwd", grid=(s // ts,),
        in_specs=[_rows(ts, d), _rows(ts, d), _vec(d), _vec(d, 2)],
        out_specs=_rows(ts, d), out_shape=_sds((s, d)), compiler_params=_cp(1))(x, yo, g, mod_l)


def _post_bwd(dx_out, yo, g, mod_l, ts, dep=None):
    s, d = yo.shape

    def f(yov, gv, gate):
        return gate * _rms(yov, gv)

    def body(dx_ref, yo_ref, g_ref, gate_ref, *rest):
        dyo_ref, dgate_ref, dg_ref = rest[-3:]
        i = pl.program_id(0)
        _, vjp = jax.vjp(f, yo_ref[...], g_ref[...], gate_ref[...])
        dyo, dg, dgate = vjp(dx_ref[...])
        dyo_ref[...] = dyo.astype(dyo_ref.dtype)

        @pl.when(i == 0)
        def _():
            dgate_ref[...] = jnp.zeros_like(dgate_ref)
            dg_ref[...] = jnp.zeros_like(dg_ref)

        dgate_ref[...] += dgate
        dg_ref[...] += dg

    return pl.pallas_call(
        body, name="post_bwd", grid=(s // ts,),
        in_specs=[_rows(ts, d), _rows(ts, d), _vec(d), _vec(d, 2)] + [_HBM_SPEC] * len(_after(dep)),
        out_specs=[_rows(ts, d), _vec(d), _vec(d)],
        out_shape=[_sds((s, d), MXU_DTYPE), _sds((1, d)), _sds((1, d))],
        compiler_params=_cp(1))(dx_out, yo, g, mod_l, *_after(dep))


def _loss_fwd_bwd(x, target, ts):
    s, d = x.shape

    def body(x_ref, t_ref, loss_ref, dx_ref):
        i = pl.program_id(0)
        err = x_ref[...] - t_ref[...]
        dx_ref[...] = err * (1.0 / d)

        @pl.when(i == 0)
        def _():
            loss_ref[...] = jnp.zeros_like(loss_ref)

        loss_ref[...] += 0.5 * jnp.sum(jnp.sum(err * err, axis=-1, keepdims=True) * (1.0 / d), axis=0, keepdims=True)

    return pl.pallas_call(
        body, name="loss", grid=(s // ts,), in_specs=[_rows(ts, d), _rows(ts, d)],
        out_specs=[_vec(1), _rows(ts, d)], out_shape=[_sds((1, 1)), _sds((s, d))],
        compiler_params=_cp(1))(x, target)


def _rope(t, cos, sin):
    lane = lax.broadcasted_iota(jnp.int32, t.shape, 1)
    first = (lane >= QK_NOPE) & (lane < QK_NOPE + QK_ROPE // 2)
    second = (lane >= QK_NOPE + QK_ROPE // 2) & (lane < QK_NOPE + QK_ROPE)
    up = pltpu.roll(t, QK_ROPE // 2, 1)
    down = pltpu.roll(t, HEAD_PAD - QK_ROPE // 2, 1)
    return t * cos + jnp.where(first, -down, jnp.where(second, up, 0.0)) * sin


def _rope_transposed(g, cos, sin):
    lane = lax.broadcasted_iota(jnp.int32, g.shape, 1)
    first = (lane >= QK_NOPE) & (lane < QK_NOPE + QK_ROPE // 2)
    second = (lane >= QK_NOPE + QK_ROPE // 2) & (lane < QK_NOPE + QK_ROPE)
    u = g * sin
    up = pltpu.roll(u, QK_ROPE // 2, 1)
    down = pltpu.roll(u, HEAD_PAD - QK_ROPE // 2, 1)
    return g * cos + jnp.where(first, down, jnp.where(second, -up, 0.0))


def _mla_prep_fwd(z, cos, sin, qg, kvg, wq, wuk, wuv, ts):
    s = z.shape[0]

    def body(cq_ref, ckv_ref, kr_ref, cos_ref, sin_ref, qg_ref, kvg_ref, wq_ref, wuk_ref, wuv_ref,
             q_ref, k_ref, v_ref, cqn_s, ckvn_s, kr_s):
        cos_v, sin_v = cos_ref[...], sin_ref[...]

        @pl.when(pl.program_id(1) == 0)
        def _():
            cqn_s[...] = _rms(cq_ref[...], qg_ref[...]).astype(cqn_s.dtype)
            ckvn_s[...] = _rms(ckv_ref[...], kvg_ref[...]).astype(ckvn_s.dtype)
            kr_s[...] = _rope(kr_ref[...], cos_v, sin_v)

        q_ref[0] = _rope(_dot_nn(cqn_s[...], wq_ref[0]), cos_v, sin_v).astype(q_ref.dtype)
        k_ref[0] = (_dot_nn(ckvn_s[...], wuk_ref[0]) + kr_s[...]).astype(k_ref.dtype)
        v_ref[0] = _dot_nn(ckvn_s[...], wuv_ref[0]).astype(v_ref.dtype)

    row = lambda w, cb: pl.BlockSpec((ts, w), lambda i, h: (i, cb))
    vec = lambda w: pl.BlockSpec((1, w), lambda i, h: (0, 0))
    wsp = lambda k: pl.BlockSpec((1, k, HEAD_PAD), lambda i, h: (h, 0, 0))
    out = pl.BlockSpec((1, ts, HEAD_PAD), lambda i, h: (h, i, 0))
    return pl.pallas_call(
        body, name="mla_prep_fwd", grid=(s // ts, HEADS),
        in_specs=[row(Q_LORA, 8), row(KV_LORA, 18), row(HEAD_PAD, 19), row(HEAD_PAD, 0), row(HEAD_PAD, 0),
                  vec(Q_LORA), vec(KV_LORA), wsp(Q_LORA), wsp(KV_LORA), wsp(KV_LORA)],
        out_specs=[out] * 3, out_shape=[_sds((HEADS, s, HEAD_PAD), MXU_DTYPE)] * 3,
        scratch_shapes=[pltpu.VMEM((ts, Q_LORA), MXU_DTYPE), pltpu.VMEM((ts, KV_LORA), MXU_DTYPE),
                        pltpu.VMEM((ts, HEAD_PAD), F32)],
        compiler_params=_cp(2))(z, z, z, cos, sin, qg, kvg, wq, wuk, wuv)


def _mla_prep_bwd(dz, dq, dk, dv, z, cos, sin, qg, kvg, wq, wuk, wuv, ts):
    s = z.shape[0]

    def fq(cq, g):
        return _rms(cq, g)

    def body(dz_in_ref, dq_ref, dk_ref, dv_ref, cq_ref, ckv_ref, cos_ref, sin_ref, qg_ref, kvg_ref, wq_ref, wuk_ref,
             wuv_ref, dz_ref, dw_ref, dqg_ref, dkvg_ref, dcqn_acc, dckvn_acc, dkr_acc, cqn_s, ckvn_s):
        del dz_in_ref
        i, h = pl.program_id(0), pl.program_id(1)
        cos_v, sin_v = cos_ref[...], sin_ref[...]
        row0 = pl.multiple_of((h % 2) * MLA_ROWS, MLA_ROWS)
        dwq_ref = dw_ref.at[h // 2, pl.ds(row0, Q_LORA)]
        dwuk_ref = dw_ref.at[h // 2, pl.ds(row0 + Q_LORA, KV_LORA)]
        dwuv_ref = dw_ref.at[h // 2, pl.ds(row0 + Q_LORA + KV_LORA, KV_LORA)]

        @pl.when((i == 0) & (h == 0))
        def _():
            dw_ref[...] = jnp.zeros_like(dw_ref)
            dqg_ref[...] = jnp.zeros_like(dqg_ref)
            dkvg_ref[...] = jnp.zeros_like(dkvg_ref)

        @pl.when(h == 0)
        def _():
            dcqn_acc[...] = jnp.zeros_like(dcqn_acc)
            dckvn_acc[...] = jnp.zeros_like(dckvn_acc)
            dkr_acc[...] = jnp.zeros_like(dkr_acc)
            cqn_s[...] = _rms(cq_ref[...], qg_ref[...]).astype(cqn_s.dtype)
            ckvn_s[...] = _rms(ckv_ref[...], kvg_ref[...]).astype(ckvn_s.dtype)

        cqn, ckvn = cqn_s[...], ckvn_s[...]
        dq_lin = _rope_transposed(dq_ref[0], cos_v, sin_v)
        dcqn_acc[...] += _dot_nt(dq_lin, wq_ref[0])
        dwq_ref[...] += _dot_tn(cqn, dq_lin)
        dkh, dvh = dk_ref[0], dv_ref[0]
        lane = lax.broadcasted_iota(jnp.int32, dkh.shape, 1)
        dkr_acc[...] += jnp.where((lane >= QK_NOPE) & (lane < QK_NOPE + QK_ROPE), dkh, 0.0)
        dckvn_acc[...] += _dot_nt(dkh, wuk_ref[0]) + _dot_nt(dvh, wuv_ref[0])
        dwuk_ref[...] += _dot_tn(ckvn, dkh)
        dwuv_ref[...] += _dot_tn(ckvn, dvh)

        @pl.when(h == HEADS - 1)
        def _():
            _, vjp_q = jax.vjp(fq, cq_ref[...], qg_ref[...])
            dcq, dqg = vjp_q(dcqn_acc[...])
            _, vjp_kv = jax.vjp(fq, ckv_ref[...], kvg_ref[...])
            dckv, dkvg = vjp_kv(dckvn_acc[...])
            dz_ref[:, 0:Q_LORA] = dcq
            dz_ref[:, Q_LORA:Q_LORA + KV_LORA] = dckv
            dz_ref[:, Q_LORA + KV_LORA:] = _rope_transposed(dkr_acc[...], cos_v, sin_v)
            dqg_ref[...] += dqg
            dkvg_ref[...] += dkvg

    row = lambda w, cb: pl.BlockSpec((ts, w), lambda i, h: (i, cb))
    vec = lambda w: pl.BlockSpec((1, w), lambda i, h: (0, 0))
    wsp = lambda k: pl.BlockSpec((1, k, HEAD_PAD), lambda i, h: (h, 0, 0))
    hrow = pl.BlockSpec((1, ts, HEAD_PAD), lambda i, h: (h, i, 0))
    whole = pl.BlockSpec((N_CHIPS, 2 * MLA_ROWS, HEAD_PAD), lambda i, h: (0, 0, 0))
    return pl.pallas_call(
        body, name="mla_prep_bwd", grid=(s // ts, HEADS),
        in_specs=[_HBM_SPEC, hrow, hrow, hrow, row(Q_LORA, 8), row(KV_LORA, 18),
                  row(HEAD_PAD, 0), row(HEAD_PAD, 0), vec(Q_LORA), vec(KV_LORA), wsp(Q_LORA), wsp(KV_LORA), wsp(KV_LORA)],
        out_specs=[row(512, 4), whole, vec(Q_LORA), vec(KV_LORA)],
        out_shape=[_sds(dz.shape), _sds((N_CHIPS, 2 * MLA_ROWS, HEAD_PAD)), _sds((1, Q_LORA)), _sds((1, KV_LORA))],
        scratch_shapes=[pltpu.VMEM((ts, Q_LORA), F32), pltpu.VMEM((ts, KV_LORA), F32), pltpu.VMEM((ts, HEAD_PAD), F32),
                        pltpu.VMEM((ts, Q_LORA), MXU_DTYPE), pltpu.VMEM((ts, KV_LORA), MXU_DTYPE)],
        input_output_aliases={0: 0}, compiler_params=_cp(2))(dz, dq, dk, dv, z, z, cos, sin, qg, kvg, wq, wuk, wuv)


def _chunk_mask(q0, k0, tq, tk):
    rows = q0 + lax.broadcasted_iota(jnp.int32, (tq, tk), 0)
    cols = k0 + lax.broadcasted_iota(jnp.int32, (tq, tk), 1)
    return lax.shift_right_logical(cols, 6) <= lax.shift_right_logical(rows, 6)


def _attn_fwd(q, k, v, tq):
    s = q.shape[1]
    nq = s // tq
    scale = 1.0 / float(QK_NOPE + QK_ROPE) ** 0.5

    def body(q_ref, k_ref, v_ref, o_ref, lse_ref):
        qi, hh = pl.program_id(1), pl.program_id(2)
        qv = q_ref[0]

        def step(kj, carry, masked):
            m, l, acc = carry
            k0 = pl.multiple_of(kj * tq, tq)
            sc = _dot_nt(qv, k_ref[0, pl.ds(k0, tq), :]) * scale
            if masked:
                sc = jnp.where(_chunk_mask(qi * tq, k0, tq, tq), sc, NEG)
            m_new = jnp.maximum(m, jnp.max(sc, axis=-1, keepdims=True))
            alpha = jnp.exp(m - m_new)
            p = jnp.exp(sc - m_new)
            l = alpha * l + jnp.sum(p, axis=-1, keepdims=True)
            acc = alpha * acc + _dot_nn(p, v_ref[0, pl.ds(k0, tq), :])
            return m_new, l, acc

        init = (jnp.full((tq, 1), NEG, F32), jnp.zeros((tq, 1), F32), jnp.zeros((tq, HEAD_PAD), F32))
        carry = lax.fori_loop(0, qi, lambda kj, c: step(kj, c, False), init)
        m, l, acc = step(qi, carry, True)
        o = acc / l
        lse_ref[0] = m + jnp.log(l)

        @pl.when(hh == 0)
        def _():
            o_ref[...] = o

        @pl.when(hh == 1)
        def _():
            o_ref[...] += o

    head = lambda hp, qi, hh: 2 * hp + hh
    return pl.pallas_call(
        body, name="attn_fwd", grid=(HEADS // 2, nq, 2),
        in_specs=[pl.BlockSpec((1, tq, HEAD_PAD), lambda hp, qi, hh: (head(hp, qi, hh), qi, 0)),
                  pl.BlockSpec((1, s, HEAD_PAD), lambda hp, qi, hh: (head(hp, qi, hh), 0, 0)),
                  pl.BlockSpec((1, s, HEAD_PAD), lambda hp, qi, hh: (head(hp, qi, hh), 0, 0))],
        out_specs=[pl.BlockSpec((tq, HEAD_PAD), lambda hp, qi, hh: (qi, hp)),
                   pl.BlockSpec((1, tq, 1), lambda hp, qi, hh: (head(hp, qi, hh), qi, 0))],
        out_shape=[_sds((s, HEADS * V_HEAD)), _sds((HEADS, s, 1))],
        compiler_params=_cp(3))(q, k, v)


def _attn_bwd(q, k, v, do, o, lse, tq):
    s = q.shape[1]
    nq = s // tq
    scale = 1.0 / float(QK_NOPE + QK_ROPE) ** 0.5

    def body(q_ref, k_ref, v_ref, do_ref, o_ref, lse_ref, dq_ref, dk_ref, dv_ref):
        hh, kj = pl.program_id(1), pl.program_id(2)

        @pl.when(kj == 0)
        def _():
            dq_ref[...] = jnp.zeros_like(dq_ref)

        kv, vv = k_ref[0], v_ref[0]
        lane = lax.broadcasted_iota(jnp.int32, (tq, HEAD_PAD), 1)
        mine = lax.shift_right_logical(lane, 6) == hh

        def step(qi, carry, masked):
            dk_acc, dv_acc = carry
            q0 = pl.multiple_of(qi * tq, tq)
            qv = q_ref[0, pl.ds(q0, tq), :]
            dov = do_ref[pl.ds(q0, tq), :]
            delta = jnp.sum(jnp.where(mine, dov * o_ref[pl.ds(q0, tq), :], 0.0), axis=-1, keepdims=True)
            sc = _dot_nt(qv, kv) * scale
            if masked:
                sc = jnp.where(_chunk_mask(q0, kj * tq, tq, tq), sc, NEG)
            p = jnp.exp(sc - lse_ref[0, pl.ds(q0, tq), :])
            do_b = dov.astype(MXU_DTYPE)
            ds = (p * (_dot_nt(do_b, vv) - delta) * scale).astype(MXU_DTYPE)
            dv_acc = dv_acc + _dot_tn(p, do_b)
            dk_acc = dk_acc + _dot_tn(ds, qv)
            dq_ref[0, pl.ds(q0, tq), :] += _dot_nn(ds, kv)
            return dk_acc, dv_acc

        zero = jnp.zeros((tq, HEAD_PAD), F32)
        carry = step(kj, (zero, zero), True)
        dk_acc, dv_acc = lax.fori_loop(kj + 1, nq, lambda qi, c: step(qi, c, False), carry)
        dk_ref[0] = dk_acc
        dv_ref[0] = dv_acc

    head = lambda hp, hh, kj: 2 * hp + hh
    full = pl.BlockSpec((1, s, HEAD_PAD), lambda hp, hh, kj: (head(hp, hh, kj), 0, 0))
    blk = pl.BlockSpec((1, tq, HEAD_PAD), lambda hp, hh, kj: (head(hp, hh, kj), kj, 0))
    pair = pl.BlockSpec((s, HEAD_PAD), lambda hp, hh, kj: (0, hp))
    return pl.pallas_call(
        body, name="attn_bwd", grid=(HEADS // 2, 2, nq),
        in_specs=[full, blk, blk, pair, pair, pl.BlockSpec((1, s, 1), lambda hp, hh, kj: (head(hp, hh, kj), 0, 0))],
        out_specs=[full, blk, blk], out_shape=[_sds((HEADS, s, HEAD_PAD))] * 3,
        compiler_params=_cp(3))(q, k, v, do, o, lse)


def _sc_conv(u, ubuf, w_ref, b_ref, ts):
    return (w_ref[2:3, :] * u + w_ref[1:2, :] * ubuf[pl.ds(SC_HALO - 1, ts), :]
            + w_ref[0:1, :] * ubuf[pl.ds(SC_HALO - 2, ts), :] + b_ref[...])


def _even_gate_fwd(z, o, sc_w, sc_b, ts):
    s = z.shape[0]
    w = SC_WIDTH

    def body(ab_ref, ac_ref, ax_ref, ag_ref, bg_ref, hc_ref, hx_ref, o_ref, w_ref, b_ref, y_ref, ubuf):
        i = pl.program_id(0)
        u = ac_ref[...] * ax_ref[...]
        ubuf[0:SC_HALO, :] = jnp.where(i > 0, hc_ref[...] * hx_ref[...], 0.0)
        ubuf[SC_HALO:, :] = u
        conv = _sc_conv(u, ubuf, w_ref, b_ref, ts)
        y_ref[:, 0:w] = (ab_ref[...] * conv * _silu(ag_ref[...])).astype(y_ref.dtype)
        y_ref[:, w:] = (o_ref[...] * _silu(bg_ref[...])).astype(y_ref.dtype)

    return pl.pallas_call(
        body, name="even_gate_fwd", grid=(s // ts,),
        in_specs=[_rows(ts, w, 0), _rows(ts, w, 1), _rows(ts, w, 2), _rows(ts, w, 3), _rows(ts, w, 5),
                  _prev_halo(ts, SC_HALO, w, 1), _prev_halo(ts, SC_HALO, w, 2), _rows(ts, w),
                  _vec(w, 0, SC_KERNEL), _vec(w)],
        out_specs=_rows(ts, 2 * w), out_shape=_sds((s, 2 * w), MXU_DTYPE),
        scratch_shapes=[pltpu.VMEM((ts + SC_HALO, w), F32)],
        compiler_params=_cp(1))(z, z, z, z, z, z, z, o, sc_w, sc_b)


def _even_gate_bwd(dy, z, o, sc_w, sc_b, ts):
    s = z.shape[0]
    w = SC_WIDTH
    n = s // ts

    def body(dya_ref, dyb_ref, dyan_ref, ab_ref, ac_ref, ax_ref, ag_ref, bg_ref, hc_ref, hx_ref, abn_ref, agn_ref,
             o_ref, w_ref, b_ref, dz_ref, do_ref, dw_ref, db_ref, ubuf, dbuf):
        i = pl.program_id(0)
        ab, ac, ax, ag, bg = ab_ref[...], ac_ref[...], ax_ref[...], ag_ref[...], bg_ref[...]
        dya, dyb = dya_ref[...], dyb_ref[...]
        u = ac * ax
        ubuf[0:SC_HALO, :] = jnp.where(i > 0, hc_ref[...] * hx_ref[...], 0.0)
        ubuf[SC_HALO:, :] = u
        conv = _sc_conv(u, ubuf, w_ref, b_ref, ts)
        sg = _silu(ag)
        dconv = dya * ab * sg
        dbuf[0:ts, :] = dconv
        dbuf[ts:, :] = jnp.where(i < n - 1, dyan_ref[...] * abn_ref[...] * _silu(agn_ref[...]), 0.0)
        du = w_ref[2:3, :] * dconv + w_ref[1:2, :] * dbuf[pl.ds(1, ts), :] + w_ref[0:1, :] * dbuf[pl.ds(2, ts), :]
        dz_ref[:, 0:w] = dya * conv * sg
        dz_ref[:, w:2 * w] = du * ax
        dz_ref[:, 2 * w:3 * w] = du * ac
        dz_ref[:, 3 * w:4 * w] = dya * ab * conv * _dsilu(ag)
        dz_ref[:, 4 * w:5 * w] = jnp.zeros((ts, w), F32)
        dz_ref[:, 5 * w:] = dyb * o_ref[...] * _dsilu(bg)
        do_ref[...] = dyb * _silu(bg)

        @pl.when(i == 0)
        def _():
            dw_ref[...] = jnp.zeros_like(dw_ref)
            db_ref[...] = jnp.zeros_like(db_ref)

        dw_ref[0:1, :] += jnp.sum(dconv * ubuf[pl.ds(SC_HALO - 2, ts), :], axis=0, keepdims=True)
        dw_ref[1:2, :] += jnp.sum(dconv * ubuf[pl.ds(SC_HALO - 1, ts), :], axis=0, keepdims=True)
        dw_ref[2:3, :] += jnp.sum(dconv * u, axis=0, keepdims=True)
        db_ref[...] += jnp.sum(dconv, axis=0, keepdims=True)

    return pl.pallas_call(
        body, name="even_gate_bwd", grid=(n,),
        in_specs=[_rows(ts, w, 0), _rows(ts, w, 1), _next_halo(ts, SC_HALO, w, 0, s),
                  _rows(ts, w, 0), _rows(ts, w, 1), _rows(ts, w, 2), _rows(ts, w, 3), _rows(ts, w, 5),
                  _prev_halo(ts, SC_HALO, w, 1), _prev_halo(ts, SC_HALO, w, 2),
                  _next_halo(ts, SC_HALO, w, 0, s), _next_halo(ts, SC_HALO, w, 3, s),
                  _rows(ts, w), _vec(w, 0, SC_KERNEL), _vec(w)],
        out_specs=[_rows(ts, EVEN_PAD), _rows(ts, w), _vec(w, 0, SC_KERNEL), _vec(w)],
        out_shape=[_sds((s, EVEN_PAD)), _sds((s, w)), _sds((SC_KERNEL, w)), _sds((1, w))],
        scratch_shapes=[pltpu.VMEM((ts + SC_HALO, w), F32), pltpu.VMEM((ts + SC_HALO, w), F32)],
        compiler_params=_cp(1))(dy, dy, dy, z, z, z, z, z, z, z, z, z, o, sc_w, sc_b)


def _ln_act(uc, sg, g, b):
    mu = jnp.mean(uc, axis=-1, keepdims=True)
    var = jnp.mean(jnp.square(uc - mu), axis=-1, keepdims=True)
    return _silu((uc - mu) * lax.rsqrt(var + EPS) * g + b) * _silu(sg)


def _shifted_copies(buf, shifted, rows):
    for b in range(1, SUBLANES):
        shifted[b - 1, 0:rows, :] = buf[pl.ds(b, rows), :]


def _rows_at(buf, shifted, start, n):
    a, b = divmod(start, SUBLANES)
    return buf[pl.ds(SUBLANES * a, n), :] if b == 0 else shifted[b - 1, pl.ds(SUBLANES * a, n), :]


def _odd_fwd(z, conv_w, conv_b, ln_g, ln_b, ts):
    s = z.shape[0]
    d = D_MODEL
    k = CONF_KERNEL

    def body(val_ref, glu_ref, sg_ref, hval_ref, hglu_ref, w_ref, b_ref, g_ref, beta_ref, y_ref, uc_ref, ubuf, ush):
        i = pl.program_id(0)
        ubuf[0:CONF_HALO, :] = jnp.where(i > 0, hval_ref[...] * _sigmoid(hglu_ref[...]), 0.0)
        ubuf[CONF_HALO:, :] = val_ref[...] * _sigmoid(glu_ref[...])
        _shifted_copies(ubuf, ush, ts + CONF_HALO - SUBLANES)
        for r0 in range(0, ts, CONV_ROWS):
            acc = jnp.broadcast_to(b_ref[...], (CONV_ROWS, d))
            for j in range(k):
                acc = acc + w_ref[j:j + 1, :] * _rows_at(ubuf, ush, r0 + CONF_HALO - (k - 1) + j, CONV_ROWS)
            uc_ref[r0:r0 + CONV_ROWS, :] = acc
        y_ref[...] = _ln_act(uc_ref[...], sg_ref[...], g_ref[...], beta_ref[...]).astype(y_ref.dtype)

    return pl.pallas_call(
        body, name="odd_fwd", grid=(s // ts,),
        in_specs=[_rows(ts, d, 0), _rows(ts, d, 1), _rows(ts, d, 2),
                  _prev_halo(ts, CONF_HALO, d, 0), _prev_halo(ts, CONF_HALO, d, 1),
                  _vec(d, 0, k), _vec(d), _vec(d), _vec(d)],
        out_specs=[_rows(ts, d), _rows(ts, d)], out_shape=[_sds((s, d), MXU_DTYPE), _sds((s, d))],
        scratch_shapes=[pltpu.VMEM((ts + CONF_HALO, d), F32),
                        pltpu.VMEM((SUBLANES - 1, ts + CONF_HALO - SUBLANES, d), F32)],
        compiler_params=_cp(1))(z, z, z, z, z, conv_w, conv_b, ln_g, ln_b)


def _odd_bwd(dy, z, uc, conv_w, ln_g, ln_b, ts):
    s = z.shape[0]
    d = D_MODEL
    k = CONF_KERNEL
    n = s // ts

    def body(dy_ref, dyn_ref, val_ref, glu_ref, sg_ref, sgn_ref, hval_ref, hglu_ref, uc_ref, ucn_ref,
             w_ref, g_ref, beta_ref, dz_ref, dw_ref, db_ref, dg_ref, dbeta_ref, ubuf, dbuf, ush, dsh, dw_acc):
        i = pl.program_id(0)
        val, glu = val_ref[...], glu_ref[...]
        sig = _sigmoid(glu)
        ubuf[0:CONF_HALO, :] = jnp.where(i > 0, hval_ref[...] * _sigmoid(hglu_ref[...]), 0.0)
        ubuf[CONF_HALO:, :] = val * sig
        _, vjp = jax.vjp(_ln_act, uc_ref[...], sg_ref[...], g_ref[...], beta_ref[...])
        duc, dsg, dg, dbeta = vjp(dy_ref[...])
        _, vjp_n = jax.vjp(_ln_act, ucn_ref[...], sgn_ref[...], g_ref[...], beta_ref[...])
        dbuf[0:ts, :] = duc
        dbuf[ts:, :] = jnp.where(i < n - 1, vjp_n(dyn_ref[...])[0], 0.0)
        dz_ref[:, 2 * d:] = dsg
        _shifted_copies(ubuf, ush, ts + CONF_HALO - SUBLANES)
        _shifted_copies(dbuf, dsh, ts + CONF_HALO - SUBLANES)

        @pl.when(i == 0)
        def _():
            dw_acc[...] = jnp.zeros_like(dw_acc)
            db_ref[...] = jnp.zeros_like(db_ref)
            dg_ref[...] = jnp.zeros_like(dg_ref)
            dbeta_ref[...] = jnp.zeros_like(dbeta_ref)

        db_ref[...] += jnp.sum(duc, axis=0, keepdims=True)
        dg_ref[...] += dg
        dbeta_ref[...] += dbeta
        for r0 in range(0, ts, CONV_ROWS):
            acc = jnp.zeros((CONV_ROWS, d), F32)
            for j in range(k):
                acc = acc + w_ref[j:j + 1, :] * _rows_at(dbuf, dsh, r0 + (k - 1) - j, CONV_ROWS)
            sig_r = sig[r0:r0 + CONV_ROWS, :]
            dz_ref[r0:r0 + CONV_ROWS, 0:d] = acc * sig_r
            dz_ref[r0:r0 + CONV_ROWS, d:2 * d] = acc * val[r0:r0 + CONV_ROWS, :] * sig_r * (1.0 - sig_r)
        for j in range(k):
            prod = duc * _rows_at(ubuf, ush, CONF_HALO - (k - 1) + j, ts)
            dw_acc[j] += jnp.sum(prod.reshape(ts // SUBLANES, SUBLANES, d), axis=0)

        @pl.when(i == n - 1)
        def _():
            dw_ref[...] = jnp.sum(dw_acc[...], axis=1)

    return pl.pallas_call(
        body, name="odd_bwd", grid=(n,),
        in_specs=[_rows(ts, d), _next_halo(ts, CONF_HALO, d, 0, s),
                  _rows(ts, d, 0), _rows(ts, d, 1), _rows(ts, d, 2), _next_halo(ts, CONF_HALO, d, 2, s),
                  _prev_halo(ts, CONF_HALO, d, 0), _prev_halo(ts, CONF_HALO, d, 1),
                  _rows(ts, d), _next_halo(ts, CONF_HALO, d, 0, s),
                  _vec(d, 0, k), _vec(d), _vec(d)],
        out_specs=[_rows(ts, ODD_IN), _vec(d, 0, k), _vec(d), _vec(d), _vec(d)],
        out_shape=[_sds((s, ODD_IN)), _sds((k, d)), _sds((1, d)), _sds((1, d)), _sds((1, d))],
        scratch_shapes=[pltpu.VMEM((ts + CONF_HALO, d), F32), pltpu.VMEM((ts + CONF_HALO, d), F32)]
        + [pltpu.VMEM((SUBLANES - 1, ts + CONF_HALO - SUBLANES, d), F32)] * 2 + [pltpu.VMEM((k, SUBLANES, d), F32)],
        compiler_params=_cp(1))(dy, dy, z, z, z, z, z, z, uc, uc, conv_w, ln_g, ln_b)


def _local_step(x, target, cos, sin, mod, p, layer_weights, fwd_dep=None, grads_done=None):
    s = x.shape[0]
    tsf, tsb = min(512, s // 2), min(256, s // 2)
    tq = min(512, s // 2)
    row1 = lambda a, i: a[i:i + 1]
    saved = []
    for layer in range(DEPTH):
        i = layer // 2
        mod_l = row1(mod, layer)
        wl = layer_weights(layer, x)
        h = _pre_fwd(x, row1(p["pre_norm_g"], layer), mod_l, tsf, fwd_dep if layer == 0 else None)
        if layer % 2 == 0:
            z = _mm(h, wl["w_in"], "nn", F32, 512, 1024, "even_in_fwd")
            q, k, v = _mla_prep_fwd(z, cos, sin, row1(p["even_q_norm_g"], i), row1(p["even_kv_norm_g"], i),
                                    wl["wq"], wl["wuk"], wl["wuv"], tsf)
            o, lse = _attn_fwd(q, k, v, tq)
            y = _even_gate_fwd(z, o, wl["sc_conv_w"], row1(p["even_sc_conv_b"], i), tsf)
            yo = _mm(y, wl["w_out"], "nn", F32, 512, 1024, "even_out_fwd")
            saved.append((x, h, z, y, yo, wl, (q, k, v, o, lse)))
        else:
            z = _mm(h, wl["w_in"], "nn", F32, 512, 1024, "odd_in_fwd")
            y, uc = _odd_fwd(z, wl["conv_w"], wl["conv_b"], wl["ln_g"], wl["ln_b"], tsf)
            yo = _mm(y, wl["w_out"], "nn", F32, 512, 1024, "odd_out_fwd")
            saved.append((x, h, z, y, yo, wl, uc))
        x = _post_fwd(x, yo, row1(p["post_norm_g"], layer), mod_l, tsf)

    loss, dx = _loss_fwd_bwd(x, target, tsf)

    g = {n: [None] * (DEPTH if n in ("pre_norm_g", "post_norm_g") else N_PAIRS) for n in (
        "pre_norm_g", "post_norm_g", "even_sc_conv_w", "even_sc_conv_b", "even_q_norm_g", "even_kv_norm_g",
        "odd_conv_w", "odd_conv_b", "odd_ln_g", "odd_ln_b")}
    dmod = [None] * DEPTH
    dep = None
    for layer in reversed(range(DEPTH)):
        i = layer // 2
        mod_l = row1(mod, layer)
        x_in, h, z, y, yo, wl, extra = saved[layer]
        dyo, dgate, g["post_norm_g"][layer] = _post_bwd(dx, yo, row1(p["post_norm_g"], layer), mod_l, tsb, dep)
        bufs = {}
        if layer % 2 == 0:
            q, k, v, o, lse = extra
            dy = _mm(dyo, wl["w_out"], "nt", F32, 512, 1024, "even_out_bwd_x")
            bufs["even_w_out"] = _mm_tn_shards(y, dyo, "rows", "even_out_bwd_w")
            dz, do, g["even_sc_conv_w"][i], g["even_sc_conv_b"][i] = _even_gate_bwd(
                dy, z, o, wl["sc_conv_w"], row1(p["even_sc_conv_b"], i), tsb)
            dq, dk, dv = _attn_bwd(q, k, v, do, o, lse, tq)
            dz, bufs["even_mla"], g["even_q_norm_g"][i], g["even_kv_norm_g"][i] = _mla_prep_bwd(
                dz, dq, dk, dv, z, cos, sin, row1(p["even_q_norm_g"], i), row1(p["even_kv_norm_g"], i),
                wl["wq"], wl["wuk"], wl["wuv"], tsb)
            dh = _mm(dz, wl["w_in"], "nt", F32, 256, 1024, "even_in_bwd_x")
            bufs["even_w_in"] = _ein_to_shards(_mm(h, dz, "tn", F32, 512, 512, "even_in_bwd_w"))
        else:
            uc = extra
            dy = _mm(dyo, wl["w_out"], "nt", F32, 512, 1024, "odd_out_bwd_x")
            bufs["odd_w_out"] = _mm_tn_shards(y, dyo, "rows", "odd_out_bwd_w")
            dz, g["odd_conv_w"][i], g["odd_conv_b"][i], g["odd_ln_g"][i], g["odd_ln_b"][i] = _odd_bwd(
                dy, z, uc, wl["conv_w"], wl["ln_g"], wl["ln_b"], tsb)
            dh = _mm(dz, wl["w_in"], "nt", F32, 256, 1024, "odd_in_bwd_x")
            bufs["odd_w_in"] = _mm_tn_shards(h, dz, "cols", "odd_in_bwd_w")
        dx, dshift, dscale, g["pre_norm_g"][layer] = _pre_bwd(dh, dx, x_in, row1(p["pre_norm_g"], layer), mod_l, tsb)
        dmod[layer] = jnp.concatenate([dshift, dscale, dgate], axis=-1)
        dep = grads_done(layer, bufs, dx) if grads_done is not None else None
    stack = lambda parts: jnp.stack([a[0] if a.shape[0] == 1 and a.ndim == 2 else a for a in parts])
    small = {n: stack(parts) for n, parts in g.items()}
    small["dmod"] = jnp.concatenate(dmod, axis=0)
    return loss, dx, small


def _uq_to_heads(w):
    w = w.reshape(N_CHIPS, Q_LORA, 2, QK_NOPE + QK_ROPE).transpose(0, 2, 1, 3).reshape(HEADS, Q_LORA, QK_NOPE + QK_ROPE)
    return jnp.pad(w, ((0, 0), (0, 0), (0, HEAD_PAD - QK_NOPE - QK_ROPE)))


def _ukv_to_heads(w):
    w = w.reshape(N_CHIPS, KV_LORA, 2, QK_NOPE + V_HEAD).transpose(0, 2, 1, 3).reshape(HEADS, KV_LORA, QK_NOPE + V_HEAD)
    wk = jnp.pad(w[..., :QK_NOPE], ((0, 0), (0, 0), (0, HEAD_PAD - QK_NOPE)))
    wv = w[..., QK_NOPE:]
    zero = jnp.zeros_like(wv)
    odd = (jnp.arange(HEADS) % 2 == 1)[:, None, None]
    wv = jnp.concatenate([jnp.where(odd, zero, wv), jnp.where(odd, wv, zero)], axis=-1)
    return wk, wv


def _mla_local(q):
    blocks = q.reshape(2, MLA_ROWS, HEAD_PAD)
    uq = jnp.concatenate([blocks[r, :Q_LORA, :QK_NOPE + QK_ROPE] for r in range(2)], axis=-1)
    ukv = jnp.concatenate(
        [jnp.concatenate([blocks[r, Q_LORA:Q_LORA + KV_LORA, :QK_NOPE],
                          blocks[r, Q_LORA + KV_LORA:, V_HEAD * r:V_HEAD * (r + 1)]], axis=-1) for r in range(2)], axis=-1)
    return uq, ukv


def _place():
    return lax.axis_index("x"), lax.axis_index("y"), lax.axis_index("c")


def _flip(v, bit):
    return 1 - v if bit else v


def _sem(a, k):
    return a * (N_CHIPS - 1) + k - 1


def _remote(src, dst, send_sem, recv_sem, peer):
    return pltpu.make_async_remote_copy(src_ref=src, dst_ref=dst, send_sem=send_sem, recv_sem=recv_sem,
                                        device_id=peer, device_id_type=MESH)


_VMEM_SPEC = pl.BlockSpec(memory_space=pltpu.VMEM)
_HBM_SPEC = pl.BlockSpec(memory_space=pl.ANY)


def _ada_fwd(c8, ada_w, ada_b_sh):
    depth, d, cols = ada_w.shape

    def body(c_ref, w_ref, b_ref, call_ref, mod_ref, s1, r1, s2, r2):
        x, y, c = _place()
        chip = 2 * x + y
        me = 2 * chip + c
        call_ref[me] = c_ref[...]
        sends = []
        for k in range(1, N_DEV):
            peer = (_flip(x, k & 4), _flip(y, k & 2), _flip(c, k & 1))
            cp = _remote(c_ref, call_ref.at[me], s1.at[k - 1], r1.at[k - 1], peer)
            cp.start()
            sends.append(cp)
        for k in range(1, N_DEV):
            src = 4 * _flip(x, k & 4) + 2 * _flip(y, k & 2) + _flip(c, k & 1)
            _remote(c_ref, call_ref.at[src], s1.at[k - 1], r1.at[k - 1], (x, y, c)).wait_recv()
        act = _silu(call_ref[...]).reshape(N_DEV * 8, d)
        for l in range(depth):
            mod_ref[chip, l] = _dot_nn(act, w_ref[l]) + b_ref[l:l + 1, :]
        for k in range(1, N_CHIPS):
            peer = (_flip(x, k & 2), _flip(y, k & 1), c)
            cp = _remote(mod_ref.at[chip], mod_ref.at[chip], s2.at[k - 1], r2.at[k - 1], peer)
            cp.start()
            sends.append(cp)
        for k in range(1, N_CHIPS):
            src = 2 * _flip(x, k & 2) + _flip(y, k & 1)
            _remote(mod_ref.at[src], mod_ref.at[src], s2.at[k - 1], r2.at[k - 1], (x, y, c)).wait_recv()
        for cp in sends:
            cp.wait_send()

    return pl.pallas_call(
        body, name="ada_fwd", in_specs=[_VMEM_SPEC] * 3, out_specs=[_VMEM_SPEC] * 2,
        out_shape=[_sds((N_DEV, 8, d)), _sds((N_CHIPS, depth, N_DEV * 8, cols))],
        scratch_shapes=[pltpu.SemaphoreType.DMA((N_DEV - 1,)), pltpu.SemaphoreType.DMA((N_DEV - 1,)),
                        pltpu.SemaphoreType.DMA((N_CHIPS - 1,)), pltpu.SemaphoreType.DMA((N_CHIPS - 1,))],
        compiler_params=pltpu.CompilerParams(vmem_limit_bytes=VMEM_LIMIT_V7X))(c8, ada_w, ada_b_sh)


def _ada_bwd(c_t, dmod_sh):
    depth, n, cols = dmod_sh.shape
    d = c_t.shape[0]
    tr = 256

    def body(c_ref, dm_ref, o_ref):
        act = _silu(c_ref[...])
        acc = act[:, 0:1] * dm_ref[0, 0:1, :]
        for e in range(1, n):
            acc = acc + act[:, e:e + 1] * dm_ref[0, e:e + 1, :]
        o_ref[0] = acc

    return pl.pallas_call(
        body, name="ada_bwd", grid=(depth, d // tr),
        in_specs=[pl.BlockSpec((tr, n), lambda l, i: (i, 0)), pl.BlockSpec((1, n, cols), lambda l, i: (l, 0, 0))],
        out_specs=pl.BlockSpec((1, tr, cols), lambda l, i: (l, i, 0)), out_shape=_sds((depth, d, cols)),
        compiler_params=_cp(2))(c_t, dmod_sh)


def _gathered_shape(shape, how):
    if how == "slot":
        return (N_CHIPS,) + shape
    r, cc = shape
    return (r, N_CHIPS * cc) if how == "cols" else (N_CHIPS * r, cc)


def _gathered_part(ref, shape, how, chip):
    if how == "slot":
        return ref.at[chip]
    if how == "cols":
        return ref.at[:, pl.ds(pl.multiple_of(chip * shape[1], 128), shape[1])]
    return ref.at[pl.ds(pl.multiple_of(chip * shape[0], 8), shape[0]), :]


_SEM_SPEC = pl.BlockSpec(memory_space=pltpu.SEMAPHORE)
_TOKEN = jax.ShapeDtypeStruct((8, 128), F32)
_SPLIT_COPY = pltpu.CompilerParams(has_side_effects=pltpu.SideEffectType.DATAFLOW_SIDE_EFFECTING)


def _in_hbm(a):
    return pltpu.with_memory_space_constraint(a, pltpu.HBM)


def _gather_start(items, gathered, name, after=()):
    n = len(items)

    def body(*refs):
        ins, outs = refs[:n], refs[n:2 * n]
        send_sems, recv_sems = refs[2 * n + len(after)], refs[2 * n + len(after) + 1]
        x, y, c = _place()
        for a in range(n):
            for k in range(1, N_CHIPS):
                part = _gathered_part(outs[a], items[a][0].shape, items[a][1], 2 * x + y)
                _remote(ins[a], part, send_sems.at[_sem(a, k)], recv_sems.at[_sem(a, k)],
                        (_flip(x, k & 2), _flip(y, k & 1), c)).start()
        refs[-1][...] = jnp.zeros(_TOKEN.shape, _TOKEN.dtype)

    arrays = [_in_hbm(a) for a, _ in items] + [_in_hbm(a) for a in gathered]
    res = pl.pallas_call(
        body, name=name, in_specs=[_HBM_SPEC] * (2 * n + len(after)),
        out_specs=[_SEM_SPEC, _SEM_SPEC] + [_HBM_SPEC] * (2 * n) + [_VMEM_SPEC],
        out_shape=[pltpu.SemaphoreType.DMA((n * (N_CHIPS - 1),)), pltpu.SemaphoreType.DMA((n * (N_CHIPS - 1),))]
        + [pltpu.HBM(a.shape, a.dtype) for a in arrays] + [_TOKEN],
        input_output_aliases={a: 2 + a for a in range(2 * n)}, compiler_params=_SPLIT_COPY)(*arrays, *after)
    return res[0], res[1], res[2:2 + n], res[2 + n:2 + 2 * n], res[-1]


def _gather_wait(items, started, after, name):
    n = len(items)
    send_sems, recv_sems, shards, gathered, _ = started

    def body(*refs):
        ins, outs, send_sems, recv_sems = refs[:n], refs[n:2 * n], refs[2 * n], refs[2 * n + 1]
        x, y, c = _place()
        for a in range(n):
            for k in range(1, N_CHIPS):
                part = _gathered_part(outs[a], items[a][0].shape, items[a][1], 2 * _flip(x, k & 2) + _flip(y, k & 1))
                cp = _remote(ins[a], part, send_sems.at[_sem(a, k)], recv_sems.at[_sem(a, k)], (x, y, c))
                cp.wait_send()
                cp.wait_recv()

    res = pl.pallas_call(
        body, name=name, in_specs=[_HBM_SPEC] * (2 * n) + [_SEM_SPEC, _SEM_SPEC] + [_HBM_SPEC] * len(after),
        out_specs=[_HBM_SPEC] * (2 * n), out_shape=[pltpu.HBM(a.shape, a.dtype) for a in (*shards, *gathered)],
        input_output_aliases={a: a for a in range(2 * n)}, compiler_params=_SPLIT_COPY)(
            *shards, *gathered, send_sems, recv_sems, *after)
    return res[n:]


def _rs_start(bufs, name, after=()):
    n = len(bufs)

    def body(*refs):
        srcs, lands = refs[:n], refs[n:2 * n]
        send_sems, recv_sems = refs[2 * n + len(after)], refs[2 * n + len(after) + 1]
        x, y, c = _place()
        for a in range(n):
            for k in range(1, N_CHIPS):
                tx, ty = _flip(x, k & 2), _flip(y, k & 1)
                _remote(srcs[a].at[2 * tx + ty], lands[a].at[k - 1], send_sems.at[_sem(a, k)], recv_sems.at[_sem(a, k)],
                        (tx, ty, c)).start()
        refs[-1][...] = jnp.zeros(_TOKEN.shape, _TOKEN.dtype)

    arrays = [_in_hbm(b) for b in bufs] + [_in_hbm(lax.empty((N_CHIPS - 1,) + b.shape[1:], b.dtype)) for b in bufs]
    res = pl.pallas_call(
        body, name=name, in_specs=[_HBM_SPEC] * (2 * n + len(after)),
        out_specs=[_SEM_SPEC, _SEM_SPEC] + [_HBM_SPEC] * (2 * n) + [_VMEM_SPEC],
        out_shape=[pltpu.SemaphoreType.DMA((n * (N_CHIPS - 1),)), pltpu.SemaphoreType.DMA((n * (N_CHIPS - 1),))]
        + [pltpu.HBM(a.shape, a.dtype) for a in arrays] + [_TOKEN],
        input_output_aliases={a: 2 + a for a in range(2 * n)}, compiler_params=_SPLIT_COPY)(*arrays, *after)
    return res[0], res[1], res[2:2 + n], res[2 + n:2 + 2 * n], res[-1]


def _rs_wait(started, after, name):
    send_sems, recv_sems, bufs, lands, _ = started
    n = len(bufs)

    def body(*refs):
        srcs, lnds, send_sems, recv_sems = refs[:n], refs[n:2 * n], refs[2 * n], refs[2 * n + 1]
        x, y, c = _place()
        for a in range(n):
            for k in range(1, N_CHIPS):
                cp = _remote(srcs[a].at[0], lnds[a].at[k - 1], send_sems.at[_sem(a, k)], recv_sems.at[_sem(a, k)], (x, y, c))
                cp.wait_send()
                cp.wait_recv()

    res = pl.pallas_call(
        body, name=name, in_specs=[_HBM_SPEC] * (2 * n) + [_SEM_SPEC, _SEM_SPEC] + [_HBM_SPEC] * len(after),
        out_specs=[_HBM_SPEC] * (2 * n), out_shape=[pltpu.HBM(a.shape, a.dtype) for a in (*bufs, *lands)],
        input_output_aliases={a: a for a in range(2 * n)}, compiler_params=_SPLIT_COPY)(
            *bufs, *lands, send_sems, recv_sems, *after)
    return res[:n], res[n:]


def _place_own(shard, how, chip_idx):
    r, cc = shard.shape
    block, index = {"slot": ((1, r, cc), lambda i, c: (c[0], 0, 0)), "cols": ((r, cc), lambda i, c: (0, c[0])),
                    "rows": ((r, cc), lambda i, c: (c[0], 0))}[how]

    def body(c_ref, in_ref, o_ref):
        del c_ref
        o_ref[...] = in_ref[...].reshape(o_ref.shape)

    return pl.pallas_call(
        body, name="place_own", out_shape=_sds(_gathered_shape(shard.shape, how), shard.dtype),
        grid_spec=pltpu.PrefetchScalarGridSpec(
            num_scalar_prefetch=1, grid=(1,), in_specs=[pl.BlockSpec((r, cc), lambda i, c: (0, 0))],
            out_specs=pl.BlockSpec(block, index)),
        compiler_params=_cp(1))(chip_idx, shard)


def _gather_chips(items, after=()):
    n = n_all = len(items)
    arrays = [a for a, _ in items]

    def body(*refs):
        ins, outs = refs[:n_all], refs[-n_all - 3:-3]
        send_sems, recv_sems, local_sems = refs[-3:]
        x, y, c = _place()
        chip = 2 * x + y
        part = lambda a, j: _gathered_part(outs[a], items[a][0].shape, items[a][1], j)
        local = [pltpu.make_async_copy(ins[a], part(a, chip), local_sems.at[a]) for a in range(n_all)]
        for cp in local[:n]:
            cp.start()
        sends = []
        for a in range(n):
            for k in range(1, N_CHIPS):
                peer = (_flip(x, k & 2), _flip(y, k & 1), c)
                cp = _remote(ins[a], part(a, chip), send_sems.at[_sem(a, k)], recv_sems.at[_sem(a, k)], peer)
                cp.start()
                sends.append(cp)
        for cp in local[n:]:
            cp.start()
        for a in range(n):
            for k in range(1, N_CHIPS):
                src = 2 * _flip(x, k & 2) + _flip(y, k & 1)
                _remote(ins[a], part(a, src), send_sems.at[_sem(a, k)], recv_sems.at[_sem(a, k)], (x, y, c)).wait_recv()
        for cp in sends:
            cp.wait_send()
        for cp in local:
            cp.wait()

    return pl.pallas_call(
        body, name="gather_chips", in_specs=[_HBM_SPEC] * (n_all + len(after)), out_specs=[_HBM_SPEC] * n_all,
        out_shape=[_sds(_gathered_shape(a.shape, how), a.dtype) for a, how in items],
        scratch_shapes=[pltpu.SemaphoreType.DMA((n * (N_CHIPS - 1),)), pltpu.SemaphoreType.DMA((n * (N_CHIPS - 1),)),
                        pltpu.SemaphoreType.DMA((n_all,))])(*arrays, *after)


def _gather_sum_all(small):
    r, w = small.shape

    def body(in_ref, all_ref, sum_ref, send_sems, recv_sems):
        x, y, c = _place()
        me = 4 * x + 2 * y + c
        all_ref[me] = in_ref[...]
        sends = []
        for k in range(1, N_DEV):
            peer = (_flip(x, k & 4), _flip(y, k & 2), _flip(c, k & 1))
            cp = _remote(in_ref, all_ref.at[me], send_sems.at[k - 1], recv_sems.at[k - 1], peer)
            cp.start()
            sends.append(cp)
        for k in range(1, N_DEV):
            src = 4 * _flip(x, k & 4) + 2 * _flip(y, k & 2) + _flip(c, k & 1)
            _remote(in_ref, all_ref.at[src], send_sems.at[k - 1], recv_sems.at[k - 1], (x, y, c)).wait_recv()
        acc = all_ref[0]
        for e in range(1, N_DEV):
            acc = acc + all_ref[e]
        sum_ref[...] = acc
        for cp in sends:
            cp.wait_send()

    return pl.pallas_call(
        body, name="gather_sum_all", in_specs=[_VMEM_SPEC], out_specs=[_VMEM_SPEC] * 2,
        out_shape=[_sds((N_DEV, r, w)), _sds((r, w))],
        scratch_shapes=[pltpu.SemaphoreType.DMA((N_DEV - 1,)), pltpu.SemaphoreType.DMA((N_DEV - 1,))],
        compiler_params=pltpu.CompilerParams(vmem_limit_bytes=VMEM_LIMIT_V7X))(small)


def _add_chips(buf, t, chip_idx):
    r, cc = buf.shape[1:]
    tr = min(256, r)

    def body(c_ref, p_ref, t_ref, o_ref):
        del c_ref
        o_ref[...] = p_ref[0] + t_ref[0].astype(F32) + t_ref[1].astype(F32) + t_ref[2].astype(F32)

    return pl.pallas_call(
        body, name="add_chips", out_shape=_sds((r, cc)),
        grid_spec=pltpu.PrefetchScalarGridSpec(
            num_scalar_prefetch=1, grid=(r // tr,),
            in_specs=[pl.BlockSpec((1, tr, cc), lambda i, c: (c[0], i, 0)),
                      pl.BlockSpec((N_CHIPS - 1, tr, cc), lambda i, c: (0, i, 0))],
            out_specs=pl.BlockSpec((tr, cc), lambda i, c: (i, 0))),
        compiler_params=_cp(1))(chip_idx, buf, t)


def _rs_sibling(qs):
    n = len(qs)

    def body(*refs):
        ins, outs = refs[:n], refs[n:2 * n]
        send_sems, recv_sems = refs[2 * n:]
        x, y, c = _place()
        copies = [_remote(ins[a], outs[a], send_sems.at[a], recv_sems.at[a], (x, y, 1 - c)) for a in range(n)]
        for cp in copies:
            cp.start()
        for cp in copies:
            cp.wait()

    return pl.pallas_call(
        body, name="rs_sibling", in_specs=[_HBM_SPEC] * n, out_specs=[_HBM_SPEC] * n,
        out_shape=[_sds(q.shape) for q in qs],
        scratch_shapes=[pltpu.SemaphoreType.DMA((n,)), pltpu.SemaphoreType.DMA((n,))])(*qs)


def _adamw_update(w, g, m, v):
    m = ADAM_B1 * m + (1.0 - ADAM_B1) * g
    v = ADAM_B2 * v + (1.0 - ADAM_B2) * jnp.square(g)
    m_hat = m / (1.0 - ADAM_B1 ** ADAM_STEP)
    v_hat = v / (1.0 - ADAM_B2 ** ADAM_STEP)
    return -ADAM_LR * (m_hat / (jnp.sqrt(v_hat) + ADAM_EPS) + ADAM_WD * w), m, v


def _adamw(w, g_parts, m, v, name):
    shape = w.shape
    cols = shape[-1]
    rows = _size(shape[:-1])
    tr = 512 if rows % 512 == 0 else rows
    spec = pl.BlockSpec((tr, cols), lambda i: (i, 0))
    n = len(g_parts)

    def body(*refs):
        w_ref, m_ref, v_ref = refs[:3]
        g_ref, d_ref, nm_ref, nv_ref = refs[3 + n:]
        g = refs[3][...]
        for r in refs[4:3 + n]:
            g = g + r[...]
        g_ref[...] = g
        d_ref[...], nm_ref[...], nv_ref[...] = _adamw_update(w_ref[...], g, m_ref[...], v_ref[...])

    outs = pl.pallas_call(
        body, name="adamw_" + name, grid=(rows // tr,), in_specs=[spec] * (3 + n), out_specs=[spec] * 4,
        out_shape=[_sds((rows, cols))] * 4, compiler_params=_cp(1))(
            *[a.reshape(rows, cols) for a in (w, m, v, *g_parts)])
    return tuple(o.reshape(shape) for o in outs)


def _adamw_layer(w, g_parts, m, v, layer, prev, name):
    _, r, cc = w.shape
    tr = 512 if r % 512 == 0 else r
    spec = pl.BlockSpec((1, tr, cc), lambda i: (layer, i, 0))
    n = len(g_parts)

    def body(*refs):
        w_ref, m_ref, v_ref = refs[:3]
        g_ref, d_ref, nm_ref, nv_ref = refs[-4:]
        g = refs[3][...]
        for q in refs[4:3 + n]:
            g = g + q[...]
        g = g[:, :cc]
        g_ref[0] = g
        d_ref[0], nm_ref[0], nv_ref[0] = _adamw_update(w_ref[0], g, m_ref[0], v_ref[0])

    g_specs = [pl.BlockSpec((tr, q.shape[1]), lambda i: (i, 0)) for q in g_parts]
    passed = () if prev is None else tuple(prev)
    return pl.pallas_call(
        body, name="adamw_" + name, grid=(r // tr,),
        in_specs=[spec] * 3 + g_specs + [_HBM_SPEC] * len(passed), out_specs=[spec] * 4,
        out_shape=[_sds(w.shape)] * 4, input_output_aliases={3 + n + k: k for k in range(len(passed))},
        compiler_params=_cp(1))(w, m, v, *g_parts, *passed)


def _size(shape):
    n = 1
    for s in shape:
        n *= s
    return n


_SMALL = (("dmod", (DEPTH, 3 * D_MODEL)), ("pre_norm_g", (DEPTH, D_MODEL)), ("post_norm_g", (DEPTH, D_MODEL)),
          ("even_sc_conv_w", (2, SC_KERNEL, SC_WIDTH)), ("even_sc_conv_b", (2, SC_WIDTH)),
          ("even_q_norm_g", (2, Q_LORA)), ("even_kv_norm_g", (2, KV_LORA)),
          ("odd_conv_w", (2, CONF_KERNEL, D_MODEL)), ("odd_conv_b", (2, D_MODEL)), ("odd_ln_g", (2, D_MODEL)),
          ("odd_ln_b", (2, D_MODEL)))
SMALL_ROWS = -(-sum(_size(s) for _, s in _SMALL) // (8 * 128)) * 8

_SMALL_W = (("even_sc_conv_w", (2, SC_KERNEL, SC_WIDTH // N_CHIPS)), ("odd_conv_w", (2, CONF_KERNEL, D_MODEL // N_CHIPS)),
            ("odd_conv_b", (2, D_MODEL // N_CHIPS)), ("odd_ln_g", (2, D_MODEL // N_CHIPS)),
            ("odd_ln_b", (2, D_MODEL // N_CHIPS)))
SMALL_W_ROWS = -(-sum(_size(s) for _, s in _SMALL_W) // (8 * 128)) * 8


def _pack_rows(arrays, layout, rows):
    flat = jnp.concatenate([arrays[n].reshape(-1) for n, _ in layout])
    return jnp.pad(flat, (0, rows * 128 - flat.shape[0])).reshape(rows, 128)


def _unpack_small(t):
    flat = t.reshape(-1)
    out, at = {}, 0
    for n, shape in _SMALL:
        out[n] = flat[at:at + _size(shape)].reshape(shape)
        at += _size(shape)
    return out


def _unpack_small_w(t):
    flat = t.reshape(N_CHIPS, -1)
    out, at = {}, 0
    for n, shape in _SMALL_W:
        a = flat[:, at:at + _size(shape)].reshape((N_CHIPS,) + shape)
        out[n] = jnp.moveaxis(a, 0, -2).reshape(shape[:-1] + (N_CHIPS * shape[-1],))
        at += _size(shape)
    return out


def _chip_cols(a, chip):
    n = a.shape[-1] // N_CHIPS
    return lax.dynamic_slice_in_dim(a, chip * n, n, axis=a.ndim - 1)


def _join_cols(a):
    _, l, r, cc = a.shape
    return a.transpose(1, 2, 0, 3).reshape(l, r, N_CHIPS * cc)


WEIGHT_NAMES = ("ada_w", "ada_b", "pre_norm_g", "post_norm_g", "even_w_in", "even_sc_conv_w", "even_sc_conv_b",
                "even_q_norm_g", "even_kv_norm_g", "even_w_uq", "even_w_ukv", "even_w_out", "odd_w_in", "odd_conv_w",
                "odd_conv_b", "odd_ln_g", "odd_ln_b", "odd_w_out")
GATHER_HOW = ((("even_w_in", "slot"), ("even_w_uq", "slot"), ("even_w_ukv", "slot"), ("even_w_out", "rows")),
              (("odd_w_in", "cols"), ("odd_w_out", "rows")))


def kernel(x, c, positions, ada_w, ada_b, pre_norm_g, post_norm_g, even_w_in, even_sc_conv_w, even_sc_conv_b, even_q_norm_g, even_kv_norm_g, even_w_uq, even_w_ukv, even_w_out, odd_w_in, odd_conv_w, odd_conv_b, odd_ln_g, odd_ln_b, odd_w_out, loss_target, m_ada_w, m_ada_b, m_pre_norm_g, m_post_norm_g, m_even_w_in, m_even_sc_conv_w, m_even_sc_conv_b, m_even_q_norm_g, m_even_kv_norm_g, m_even_w_uq, m_even_w_ukv, m_even_w_out, m_odd_w_in, m_odd_conv_w, m_odd_conv_b, m_odd_ln_g, m_odd_ln_b, m_odd_w_out, v_ada_w, v_ada_b, v_pre_norm_g, v_post_norm_g, v_even_w_in, v_even_sc_conv_w, v_even_sc_conv_b, v_even_q_norm_g, v_even_kv_norm_g, v_even_w_uq, v_even_w_ukv, v_even_w_out, v_odd_w_in, v_odd_conv_w, v_odd_conv_b, v_odd_ln_g, v_odd_ln_b, v_odd_w_out):
    w = dict(zip(WEIGHT_NAMES, (ada_w, ada_b, pre_norm_g, post_norm_g, even_w_in, even_sc_conv_w, even_sc_conv_b,
                                even_q_norm_g, even_kv_norm_g, even_w_uq, even_w_ukv, even_w_out, odd_w_in, odd_conv_w,
                                odd_conv_b, odd_ln_g, odd_ln_b, odd_w_out)))
    m = dict(zip(WEIGHT_NAMES, (m_ada_w, m_ada_b, m_pre_norm_g, m_post_norm_g, m_even_w_in, m_even_sc_conv_w,
                                m_even_sc_conv_b, m_even_q_norm_g, m_even_kv_norm_g, m_even_w_uq, m_even_w_ukv,
                                m_even_w_out, m_odd_w_in, m_odd_conv_w, m_odd_conv_b, m_odd_ln_g, m_odd_ln_b, m_odd_w_out)))
    v = dict(zip(WEIGHT_NAMES, (v_ada_w, v_ada_b, v_pre_norm_g, v_post_norm_g, v_even_w_in, v_even_sc_conv_w,
                                v_even_sc_conv_b, v_even_q_norm_g, v_even_kv_norm_g, v_even_w_uq, v_even_w_ukv,
                                v_even_w_out, v_odd_w_in, v_odd_conv_w, v_odd_conv_b, v_odd_ln_g, v_odd_ln_b, v_odd_w_out)))
    ix, iy, ic = _place()
    chip = 2 * ix + iy
    me = 2 * chip + ic
    s = x.shape[1]

    c_all, mod_all = _ada_fwd(jnp.broadcast_to(c, (8, D_MODEL)), ada_w, _chip_cols(ada_b, chip))
    mod = lax.dynamic_index_in_dim(mod_all, 8 * me, axis=2, keepdims=False)
    mod = mod.transpose(1, 0, 2).reshape(DEPTH, 3 * D_MODEL)

    items = [[(w[n][layer // 2].astype(MXU_DTYPE), how) for n, how in GATHER_HOW[layer % 2]] for layer in range(DEPTH)]
    later_items = [item for layer_items in items[1:] for item in layer_items]
    first = _gather_chips(items[0] + [(_pack_rows(w, _SMALL_W, SMALL_W_ROWS), "slot")], [mod_all])
    small_w = _unpack_small_w(first[len(items[0])])
    weights_sent = _gather_start(later_items, [_place_own(a, how, chip.reshape(1)) for a, how in later_items],
                                 "gather_start", [first[0]])
    later = []

    def layer_weights(layer, x_in):
        i = layer // 2
        if layer == 0:
            arrays = first[:len(items[0])]
        else:
            if not later:
                later.extend(_gather_wait(later_items, weights_sent, [x_in], "gather_wait"))
            at = sum(len(layer_items) for layer_items in items[1:layer])
            arrays = later[at:at + len(items[layer])]
        if layer % 2 == 0:
            ein, uq, ukv, eout = arrays
            wuk, wuv = _ukv_to_heads(ukv)
            return {"w_in": _ein_from_shards(ein), "wq": _uq_to_heads(uq), "wuk": wuk, "wuv": wuv, "w_out": eout,
                    "sc_conv_w": small_w["even_sc_conv_w"][i]}
        oin, oout = arrays
        return {"w_in": oin, "w_out": oout, "conv_w": small_w["odd_conv_w"][i], "conv_b": small_w["odd_conv_b"][i:i + 1],
                "ln_g": small_w["odd_ln_g"][i:i + 1], "ln_b": small_w["odd_ln_b"][i:i + 1]}

    in_flight, own, sib, last = {}, {}, {}, {}

    def land(layer, after):
        names, started, kept = in_flight.pop(layer)
        bufs, arrived = _rs_wait(started, after, "rs_wait_%d" % layer)
        sums = [_add_chips(b, t, chip.reshape(1)) for b, t in zip(bufs if kept is None else kept, arrived)]
        for n, mine, theirs in zip(names, sums, _rs_sibling(sums)):
            own[n, layer // 2], sib[n, layer // 2] = mine, theirs

    def grads_done(layer, bufs, dx_in):
        if layer + 1 in in_flight:
            land(layer + 1, [dx_in])
        if layer == 0:
            last.update(bufs)
            return None
        names = sorted(bufs)
        in_flight[layer] = (names, _rs_start([bufs[n] for n in names], "rs_start_%d" % layer), None)
        return in_flight[layer][1][-1]

    p = {"pre_norm_g": pre_norm_g, "post_norm_g": post_norm_g, "even_sc_conv_b": even_sc_conv_b,
         "even_q_norm_g": even_q_norm_g, "even_kv_norm_g": even_kv_norm_g}
    inv_freq = 1.0 / (ROPE_THETA ** (jnp.arange(0, QK_ROPE, 2, dtype=F32) / QK_ROPE))
    inv_freq = jnp.zeros((1, HEAD_PAD), F32).at[0, QK_NOPE:QK_NOPE + QK_ROPE].set(jnp.tile(inv_freq, 2))
    cos, sin = _rope_tables(positions.reshape(s, 1), inv_freq)

    loss, dx, g = _local_step(x[0], loss_target[0], cos, sin, mod, p, layer_weights, weights_sent[-1], grads_done)

    grads, deltas, new_m, new_v = {}, {}, {}, {}

    def update_layers(n, results, pairs):
        for i in pairs:
            results = _adamw_layer(w[n], [own[n, i], sib[n, i]], m[n], v[n], i, results, n)
        return results

    small_all, small_sum = _gather_sum_all(_pack_rows(g, _SMALL, SMALL_ROWS))
    names = sorted(last)
    kept = [last[n] for n in names]
    in_flight[0] = (names, _rs_start([b.astype(jnp.bfloat16) for b in kept], "rs_start_0", [small_sum]), kept)
    tot = _unpack_small(small_sum)
    dmod_all = small_all[:, :DEPTH * 3 * D_MODEL // 128].reshape(N_DEV, DEPTH, 3 * D_MODEL)
    grads["ada_w"] = _ada_bwd(c_all[:, 0, :].T, _chip_cols(dmod_all, chip).transpose(1, 0, 2))
    grads["ada_b"] = tot["dmod"]
    for n in ("pre_norm_g", "post_norm_g", "even_sc_conv_b", "even_q_norm_g", "even_kv_norm_g"):
        grads[n] = tot[n]
    for n in ("even_sc_conv_w", "odd_conv_w", "odd_conv_b", "odd_ln_g", "odd_ln_b"):
        grads[n] = _chip_cols(tot[n], chip)
    for n in list(grads):
        _, deltas[n], new_m[n], new_v[n] = _adamw(w[n], [grads[n]], m[n], v[n], n)

    for n in ("odd_w_in", "odd_w_out"):
        grads[n], deltas[n], new_m[n], new_v[n] = update_layers(n, None, (1, 0))
    partly = {n: update_layers(n, None, (1,)) for n in ("even_w_in", "even_w_out")}
    land(0, [deltas["ada_w"], deltas["odd_w_in"], partly["even_w_in"][1]])
    for n in ("even_w_in", "even_w_out"):
        grads[n], deltas[n], new_m[n], new_v[n] = update_layers(n, partly[n], (0,))
    uq_parts, ukv_parts = zip(*[[jnp.stack(part) for part in zip(*[_mla_local(q["even_mla", i]) for i in range(N_PAIRS)])]
                                for q in (own, sib)])
    for n, parts in (("even_w_uq", uq_parts), ("even_w_ukv", ukv_parts)):
        grads[n], deltas[n], new_m[n], new_v[n] = _adamw(w[n], list(parts), m[n], v[n], n)

    total_loss = lax.psum(loss[0, 0], ("x", "y", "c"))
    return (total_loss, dx[None], *[grads[n] for n in WEIGHT_NAMES], *[deltas[n] for n in WEIGHT_NAMES],
            *[new_m[n] for n in WEIGHT_NAMES], *[new_v[n] for n in WEIGHT_NAMES])
```

```python
import functools

import jax
import jax.numpy as jnp
from jax import lax
from jax.experimental import pallas as pl
from jax.experimental.pallas import tpu as pltpu

F32 = jnp.float32
MXU_DTYPE = jnp.bfloat16
MESH = pl.DeviceIdType.MESH
VMEM_LIMIT_V7X = 56 * 2 ** 20

EPS = 1e-6
D_MODEL = 1024
DEPTH = 4
CHUNK = 64
SC_WIDTH = 512
SC_KERNEL = 3
SC_HALO = 8
HEADS = 8
QK_NOPE = 64
QK_ROPE = 32
V_HEAD = 64
HEAD_PAD = 128
Q_LORA = 256
KV_LORA = 128
ROPE_THETA = 10000.0
CONF_KERNEL = 31
CONF_HALO = 32
CONV_ROWS = 32
SUBLANES = 8
EVEN_IN = 2976
EVEN_PAD = 3072
ODD_IN = 3072
N_CHIPS = 4
N_DEV = 8
NEG = -1e30

ADAM_LR = 0.001
ADAM_B1 = 0.9
ADAM_B2 = 0.999
ADAM_EPS = 1e-08
ADAM_WD = 0.01
ADAM_STEP = 10

N_PAIRS = DEPTH // 2
EVEN_SHARD = EVEN_IN // N_CHIPS
EVEN_SHARD_PAD = 768
MLA_ROWS = Q_LORA + 2 * KV_LORA


def _cp(n_grid=0, **kw):
    return pltpu.CompilerParams(dimension_semantics=("arbitrary",) * n_grid,
                                vmem_limit_bytes=VMEM_LIMIT_V7X, **kw)


def _sigmoid(x):
    return 1.0 / (1.0 + jnp.exp(-x))


def _silu(x):
    return x * _sigmoid(x)


def _dsilu(x):
    s = _sigmoid(x)
    return s * (1.0 + x * (1.0 - s))


def _rms(x, g):
    return x * lax.rsqrt(jnp.mean(x * x, axis=-1, keepdims=True) + EPS) * g


def _dot(a, b, dims):
    return lax.dot_general(a.astype(MXU_DTYPE), b.astype(MXU_DTYPE), (dims, ((), ())),
                           preferred_element_type=F32)


def _dot_nn(a, b):
    return _dot(a, b, ((1,), (0,)))


def _dot_nt(a, b):
    return _dot(a, b, ((1,), (1,)))


def _dot_tn(a, b):
    return _dot(a, b, ((0,), (0,)))


def _rows(ts, w, cb=0):
    return pl.BlockSpec((ts, w), lambda i: (i, cb))


def _vec(w, cb=0, r=1):
    return pl.BlockSpec((r, w), lambda i: (0, cb))


def _prev_halo(ts, hr, w, cb):
    return pl.BlockSpec((hr, w), lambda i: (jnp.maximum(i * (ts // hr) - 1, 0), cb))


def _next_halo(ts, hr, w, cb, s):
    return pl.BlockSpec((hr, w), lambda i: (jnp.minimum((i + 1) * (ts // hr), s // hr - 1), cb))


def _sds(shape, dtype=F32):
    return jax.ShapeDtypeStruct(shape, dtype)


def _mm(a, b, mode, out_dtype, tm, tn, name):
    tm = min(tm, a.shape[1] if mode == "tn" else a.shape[0])
    tn = min(tn, b.shape[0] if mode == "nt" else b.shape[1])
    if mode == "nn":
        (m, k), n = a.shape, b.shape[1]
        a_spec = pl.BlockSpec((tm, k), lambda i, j: (i, 0))
        b_spec = pl.BlockSpec((k, tn), lambda i, j: (0, j))
        dot = _dot_nn
    elif mode == "nt":
        (m, k), n = a.shape, b.shape[0]
        a_spec = pl.BlockSpec((tm, k), lambda i, j: (i, 0))
        b_spec = pl.BlockSpec((tn, k), lambda i, j: (j, 0))
        dot = _dot_nt
    else:
        (k, m), n = a.shape, b.shape[1]
        a_spec = pl.BlockSpec((k, tm), lambda i, j: (0, i))
        b_spec = pl.BlockSpec((k, tn), lambda i, j: (0, j))
        dot = _dot_tn
    assert m % tm == 0 and n % tn == 0, (name, m, n, tm, tn)

    def body(a_ref, b_ref, o_ref):
        o_ref[...] = dot(a_ref[...], b_ref[...]).astype(o_ref.dtype)

    return pl.pallas_call(
        body, name=name, grid=(m // tm, n // tn), in_specs=[a_spec, b_spec],
        out_specs=pl.BlockSpec((tm, tn), lambda i, j: (i, j)), out_shape=_sds((m, n), out_dtype),
        compiler_params=_cp(2))(a, b)


def _mm_tn_shards(a, b, by, name):
    k, m = a.shape
    n = b.shape[1]
    if by == "cols":
        tm, tn = m, n // N_CHIPS
        shape, grid = (N_CHIPS, m, tn), (1, N_CHIPS)
        out_spec = pl.BlockSpec((1, tm, tn), lambda i, j: (j, i, 0))
    else:
        tm, tn = m // N_CHIPS, n
        shape, grid = (N_CHIPS, tm, n), (N_CHIPS, 1)
        out_spec = pl.BlockSpec((1, tm, tn), lambda i, j: (i, 0, j))

    def body(a_ref, b_ref, o_ref):
        o_ref[0] = _dot_tn(a_ref[...], b_ref[...])

    return pl.pallas_call(
        body, name=name, grid=grid,
        in_specs=[pl.BlockSpec((k, tm), lambda i, j: (0, i)), pl.BlockSpec((k, tn), lambda i, j: (0, j))],
        out_specs=out_spec, out_shape=_sds(shape), compiler_params=_cp(2))(a, b)


def _even_col(q):
    return q if q < 2432 else (q + 64 if q < 2464 else q + 96)


def _shard_pieces(j):
    lo, hi = EVEN_SHARD * j, EVEN_SHARD * (j + 1)
    cuts = [lo] + [b for b in (2432, 2464) if lo < b < hi] + [hi]
    return [(a - lo, _even_col(a), b - a) for a, b in zip(cuts[:-1], cuts[1:])]


def _ein_from_shards(w):
    _, d, _ = w.shape
    tr = 256

    def body(w_ref, o_ref):
        parts, at = [], 0
        for j in range(N_CHIPS):
            for d0, s0, n in _shard_pieces(j):
                if s0 > at:
                    parts.append(jnp.zeros((tr, s0 - at), F32))
                parts.append(w_ref[j, :, d0:d0 + n].astype(F32))
                at = s0 + n
        o_ref[...] = jnp.concatenate(parts, axis=1).astype(o_ref.dtype)

    return pl.pallas_call(
        body, name="ein_from_shards", grid=(d // tr,),
        in_specs=[pl.BlockSpec((N_CHIPS, tr, EVEN_SHARD), lambda i: (0, i, 0))],
        out_specs=_rows(tr, EVEN_PAD), out_shape=_sds((d, EVEN_PAD), w.dtype), compiler_params=_cp(1))(w)


def _ein_to_shards(dw):
    d = dw.shape[0]
    tr = 256

    def body(dw_ref, o_ref):
        for j in range(N_CHIPS):
            parts = [dw_ref[:, s0:s0 + n] for _, s0, n in _shard_pieces(j)]
            o_ref[j] = jnp.concatenate(parts + [jnp.zeros((tr, EVEN_SHARD_PAD - EVEN_SHARD), F32)], axis=1)

    return pl.pallas_call(
        body, name="ein_to_shards", grid=(d // tr,), in_specs=[_rows(tr, EVEN_PAD)],
        out_specs=pl.BlockSpec((N_CHIPS, tr, EVEN_SHARD_PAD), lambda i: (0, i, 0)),
        out_shape=_sds((N_CHIPS, d, EVEN_SHARD_PAD)), compiler_params=_cp(1))(dw)


def _rope_tables(pos_col, invf):
    s = pos_col.shape[0]
    ts = min(512, s)

    def body(p_ref, f_ref, c_ref, s_ref):
        ang = p_ref[...].astype(F32) * f_ref[...]
        lane = lax.broadcasted_iota(jnp.int32, ang.shape, 1)
        rope = (lane >= QK_NOPE) & (lane < QK_NOPE + QK_ROPE)
        c_ref[...] = jnp.where(lane < QK_NOPE, 1.0, jnp.where(rope, jnp.cos(ang), 0.0))
        s_ref[...] = jnp.where(rope, jnp.sin(ang), 0.0)

    return pl.pallas_call(
        body, name="rope_tables", grid=(s // ts,), in_specs=[_rows(ts, 1), _vec(HEAD_PAD)],
        out_specs=[_rows(ts, HEAD_PAD)] * 2, out_shape=[_sds((s, HEAD_PAD))] * 2,
        compiler_params=_cp(1))(pos_col, invf)


def _after(dep):
    return () if dep is None else (dep,)


def _pre_fwd(x, g, mod_l, ts, dep=None):
    s, d = x.shape

    def body(x_ref, g_ref, sh_ref, sc_ref, *rest):
        h = _rms(x_ref[...], g_ref[...]) * (1.0 + sc_ref[...]) + sh_ref[...]
        rest[-1][...] = h.astype(rest[-1].dtype)

    return pl.pallas_call(
        body, name="pre_fwd", grid=(s // ts,),
        in_specs=[_rows(ts, d), _vec(d), _vec(d, 0), _vec(d, 1)] + [_HBM_SPEC] * len(_after(dep)),
        out_specs=_rows(ts, d), out_shape=_sds((s, d), MXU_DTYPE), compiler_params=_cp(1))(
            x, g, mod_l, mod_l, *_after(dep))


def _pre_bwd(dz, w_in, dx_out, x, g, mod_l, ts):
    s, d = x.shape
    n_in = dz.shape[1]

    def f(xv, gv, sh, sc):
        return _rms(xv, gv) * (1.0 + sc) + sh

    def body(dz_ref, w_ref, dxo_ref, x_ref, g_ref, sh_ref, sc_ref, dx_ref, dsh_ref, dsc_ref, dg_ref):
        i = pl.program_id(0)
        _, vjp = jax.vjp(f, x_ref[...], g_ref[...], sh_ref[...], sc_ref[...])
        dx, dg, dsh, dsc = vjp(_dot_nt(dz_ref[...], w_ref[...]))
        dx_ref[...] = dxo_ref[...] + dx

        @pl.when(i == 0)
        def _():
            dsh_ref[...] = jnp.zeros_like(dsh_ref)
            dsc_ref[...] = jnp.zeros_like(dsc_ref)
            dg_ref[...] = jnp.zeros_like(dg_ref)

        dsh_ref[...] += dsh
        dsc_ref[...] += dsc
        dg_ref[...] += dg

    return pl.pallas_call(
        body, name="pre_bwd", grid=(s // ts,),
        in_specs=[_rows(ts, n_in), _vec(n_in, 0, d), _rows(ts, d), _rows(ts, d), _vec(d), _vec(d, 0), _vec(d, 1)],
        out_specs=[_rows(ts, d), _vec(d), _vec(d), _vec(d)],
        out_shape=[_sds((s, d)), _sds((1, d)), _sds((1, d)), _sds((1, d))],
        compiler_params=_cp(1))(dz, w_in, dx_out, x, g, mod_l, mod_l)


def _post_fwd(x, yo, g, mod_l, ts):
    s, d = x.shape

    def body(x_ref, yo_ref, g_ref, gate_ref, o_ref):
        o_ref[...] = x_ref[...] + gate_ref[...] * _rms(yo_ref[...], g_ref[...])

    return pl.pallas_call(
        body, name="post_fwd", grid=(s // ts,),
        in_specs=[_rows(ts, d), _rows(ts, d), _vec(d), _vec(d, 2)],
        out_specs=_rows(ts, d), out_shape=_sds((s, d)), compiler_params=_cp(1))(x, yo, g, mod_l)


def _post_bwd(dx_out, yo, g, mod_l, ts, dep=None):
    s, d = yo.shape

    def f(yov, gv, gate):
        return gate * _rms(yov, gv)

    def body(dx_ref, yo_ref, g_ref, gate_ref, *rest):
        dyo_ref, dgate_ref, dg_ref = rest[-3:]
        i = pl.program_id(0)
        _, vjp = jax.vjp(f, yo_ref[...], g_ref[...], gate_ref[...])
        dyo, dg, dgate = vjp(dx_ref[...])
        dyo_ref[...] = dyo.astype(dyo_ref.dtype)

        @pl.when(i == 0)
        def _():
            dgate_ref[...] = jnp.zeros_like(dgate_ref)
            dg_ref[...] = jnp.zeros_like(dg_ref)

        dgate_ref[...] += dgate
        dg_ref[...] += dg

    return pl.pallas_call(
        body, name="post_bwd", grid=(s // ts,),
        in_specs=[_rows(ts, d), _rows(ts, d), _vec(d), _vec(d, 2)] + [_HBM_SPEC] * len(_after(dep)),
        out_specs=[_rows(ts, d), _vec(d), _vec(d)],
        out_shape=[_sds((s, d), MXU_DTYPE), _sds((1, d)), _sds((1, d))],
        compiler_params=_cp(1))(dx_out, yo, g, mod_l, *_after(dep))


def _loss_fwd_bwd(x, target, ts):
    s, d = x.shape

    def body(x_ref, t_ref, loss_ref, dx_ref):
        i = pl.program_id(0)
        err = x_ref[...] - t_ref[...]
        dx_ref[...] = err * (1.0 / d)

        @pl.when(i == 0)
        def _():
            loss_ref[...] = jnp.zeros_like(loss_ref)

        loss_ref[...] += 0.5 * jnp.sum(jnp.sum(err * err, axis=-1, keepdims=True) * (1.0 / d), axis=0, keepdims=True)

    return pl.pallas_call(
        body, name="loss", grid=(s // ts,), in_specs=[_rows(ts, d), _rows(ts, d)],
        out_specs=[_vec(1), _rows(ts, d)], out_shape=[_sds((1, 1)), _sds((s, d))],
        compiler_params=_cp(1))(x, target)


def _rope(t, cos, sin):
    lane = lax.broadcasted_iota(jnp.int32, t.shape, 1)
    first = (lane >= QK_NOPE) & (lane < QK_NOPE + QK_ROPE // 2)
    second = (lane >= QK_NOPE + QK_ROPE // 2) & (lane < QK_NOPE + QK_ROPE)
    up = pltpu.roll(t, QK_ROPE // 2, 1)
    down = pltpu.roll(t, HEAD_PAD - QK_ROPE // 2, 1)
    return t * cos + jnp.where(first, -down, jnp.where(second, up, 0.0)) * sin


def _rope_transposed(g, cos, sin):
    lane = lax.broadcasted_iota(jnp.int32, g.shape, 1)
    first = (lane >= QK_NOPE) & (lane < QK_NOPE + QK_ROPE // 2)
    second = (lane >= QK_NOPE + QK_ROPE // 2) & (lane < QK_NOPE + QK_ROPE)
    u = g * sin
    up = pltpu.roll(u, QK_ROPE // 2, 1)
    down = pltpu.roll(u, HEAD_PAD - QK_ROPE // 2, 1)
    return g * cos + jnp.where(first, down, jnp.where(second, -up, 0.0))


def _mla_prep_fwd(z, cos, sin, qg, kvg, wq, wq_rot, wuk, wuv, ts):
    s = z.shape[0]

    def body(cq_ref, ckv_ref, kr_ref, cos_ref, sin_ref, qg_ref, kvg_ref, wq_ref, wqr_ref, wuk_ref, wuv_ref,
             q_ref, k_ref, v_ref, cqn_s, ckvn_s, kr_s):
        cos_v, sin_v = cos_ref[...], sin_ref[...]

        @pl.when(pl.program_id(1) == 0)
        def _():
            cqn_s[...] = _rms(cq_ref[...], qg_ref[...]).astype(cqn_s.dtype)
            ckvn_s[...] = _rms(ckv_ref[...], kvg_ref[...]).astype(ckvn_s.dtype)
            kr_s[...] = _rope(kr_ref[...], cos_v, sin_v)

        q_ref[0] = (_dot_nn(cqn_s[...], wq_ref[0]) * cos_v + _dot_nn(cqn_s[...], wqr_ref[0]) * sin_v).astype(q_ref.dtype)
        k_ref[0] = (_dot_nn(ckvn_s[...], wuk_ref[0]) + kr_s[...]).astype(k_ref.dtype)
        v_ref[0] = _dot_nn(ckvn_s[...], wuv_ref[0]).astype(v_ref.dtype)

    row = lambda w, cb: pl.BlockSpec((ts, w), lambda i, h: (i, cb))
    vec = lambda w: pl.BlockSpec((1, w), lambda i, h: (0, 0))
    wsp = lambda k: pl.BlockSpec((1, k, HEAD_PAD), lambda i, h: (h, 0, 0))
    out = pl.BlockSpec((1, ts, HEAD_PAD), lambda i, h: (h, i, 0))
    return pl.pallas_call(
        body, name="mla_prep_fwd", grid=(s // ts, HEADS),
        in_specs=[row(Q_LORA, 8), row(KV_LORA, 18), row(HEAD_PAD, 19), row(HEAD_PAD, 0), row(HEAD_PAD, 0),
                  vec(Q_LORA), vec(KV_LORA), wsp(Q_LORA), wsp(Q_LORA), wsp(KV_LORA), wsp(KV_LORA)],
        out_specs=[out] * 3, out_shape=[_sds((HEADS, s, HEAD_PAD), MXU_DTYPE)] * 3,
        scratch_shapes=[pltpu.VMEM((ts, Q_LORA), MXU_DTYPE), pltpu.VMEM((ts, KV_LORA), MXU_DTYPE),
                        pltpu.VMEM((ts, HEAD_PAD), F32)],
        compiler_params=_cp(2))(z, z, z, cos, sin, qg, kvg, wq, wq_rot, wuk, wuv)


def _mla_prep_bwd(dz, dq, dk, dv, z, cos, sin, qg, kvg, wq, wuk, wuv, ts):
    s = z.shape[0]

    def fq(cq, g):
        return _rms(cq, g)

    def body(dz_in_ref, dq_ref, dk_ref, dv_ref, cq_ref, ckv_ref, cos_ref, sin_ref, qg_ref, kvg_ref, wq_ref, wuk_ref,
             wuv_ref, dz_ref, dw_ref, dqg_ref, dkvg_ref, dcqn_acc, dckvn_acc, dkr_acc, cqn_s, ckvn_s):
        del dz_in_ref
        i, h = pl.program_id(0), pl.program_id(1)
        cos_v, sin_v = cos_ref[...], sin_ref[...]
        row0 = pl.multiple_of((h % 2) * MLA_ROWS, MLA_ROWS)
        dwq_ref = dw_ref.at[h // 2, pl.ds(row0, Q_LORA)]
        dwuk_ref = dw_ref.at[h // 2, pl.ds(row0 + Q_LORA, KV_LORA)]
        dwuv_ref = dw_ref.at[h // 2, pl.ds(row0 + Q_LORA + KV_LORA, KV_LORA)]

        @pl.when((i == 0) & (h == 0))
        def _():
            dw_ref[...] = jnp.zeros_like(dw_ref)
            dqg_ref[...] = jnp.zeros_like(dqg_ref)
            dkvg_ref[...] = jnp.zeros_like(dkvg_ref)

        @pl.when(h == 0)
        def _():
            dcqn_acc[...] = jnp.zeros_like(dcqn_acc)
            dckvn_acc[...] = jnp.zeros_like(dckvn_acc)
            dkr_acc[...] = jnp.zeros_like(dkr_acc)
            cqn_s[...] = _rms(cq_ref[...], qg_ref[...]).astype(cqn_s.dtype)
            ckvn_s[...] = _rms(ckv_ref[...], kvg_ref[...]).astype(ckvn_s.dtype)

        cqn, ckvn = cqn_s[...], ckvn_s[...]
        dq_lin = _rope_transposed(dq_ref[0], cos_v, sin_v)
        dcqn_acc[...] += _dot_nt(dq_lin, wq_ref[0])
        dwq_ref[...] += _dot_tn(cqn, dq_lin)
        dkh, dvh = dk_ref[0], dv_ref[0]
        lane = lax.broadcasted_iota(jnp.int32, dkh.shape, 1)
        dkr_acc[...] += jnp.where((lane >= QK_NOPE) & (lane < QK_NOPE + QK_ROPE), dkh, 0.0)
        dckvn_acc[...] += _dot_nt(dkh, wuk_ref[0]) + _dot_nt(dvh, wuv_ref[0])
        dwuk_ref[...] += _dot_tn(ckvn, dkh)
        dwuv_ref[...] += _dot_tn(ckvn, dvh)

        @pl.when(h == HEADS - 1)
        def _():
            _, vjp_q = jax.vjp(fq, cq_ref[...], qg_ref[...])
            dcq, dqg = vjp_q(dcqn_acc[...])
            _, vjp_kv = jax.vjp(fq, ckv_ref[...], kvg_ref[...])
            dckv, dkvg = vjp_kv(dckvn_acc[...])
            dz_ref[:, 0:Q_LORA] = dcq.astype(dz_ref.dtype)
            dz_ref[:, Q_LORA:Q_LORA + KV_LORA] = dckv.astype(dz_ref.dtype)
            dz_ref[:, Q_LORA + KV_LORA:] = _rope_transposed(dkr_acc[...], cos_v, sin_v).astype(dz_ref.dtype)
            dqg_ref[...] += dqg
            dkvg_ref[...] += dkvg

    row = lambda w, cb: pl.BlockSpec((ts, w), lambda i, h: (i, cb))
    vec = lambda w: pl.BlockSpec((1, w), lambda i, h: (0, 0))
    wsp = lambda k: pl.BlockSpec((1, k, HEAD_PAD), lambda i, h: (h, 0, 0))
    hrow = pl.BlockSpec((1, ts, HEAD_PAD), lambda i, h: (h, i, 0))
    whole = pl.BlockSpec((N_CHIPS, 2 * MLA_ROWS, HEAD_PAD), lambda i, h: (0, 0, 0))
    return pl.pallas_call(
        body, name="mla_prep_bwd", grid=(s // ts, HEADS),
        in_specs=[_HBM_SPEC, hrow, hrow, hrow, row(Q_LORA, 8), row(KV_LORA, 18),
                  row(HEAD_PAD, 0), row(HEAD_PAD, 0), vec(Q_LORA), vec(KV_LORA), wsp(Q_LORA), wsp(KV_LORA), wsp(KV_LORA)],
        out_specs=[row(512, 4), whole, vec(Q_LORA), vec(KV_LORA)],
        out_shape=[_sds(dz.shape, dz.dtype), _sds((N_CHIPS, 2 * MLA_ROWS, HEAD_PAD)), _sds((1, Q_LORA)), _sds((1, KV_LORA))],
        scratch_shapes=[pltpu.VMEM((ts, Q_LORA), F32), pltpu.VMEM((ts, KV_LORA), F32), pltpu.VMEM((ts, HEAD_PAD), F32),
                        pltpu.VMEM((ts, Q_LORA), MXU_DTYPE), pltpu.VMEM((ts, KV_LORA), MXU_DTYPE)],
        input_output_aliases={0: 0}, compiler_params=_cp(2))(dz, dq, dk, dv, z, z, cos, sin, qg, kvg, wq, wuk, wuv)


def _chunk_mask(q0, k0, tq, tk):
    rows = q0 + lax.broadcasted_iota(jnp.int32, (tq, tk), 0)
    cols = k0 + lax.broadcasted_iota(jnp.int32, (tq, tk), 1)
    return lax.shift_right_logical(cols, 6) <= lax.shift_right_logical(rows, 6)


def _attn_fwd(q, k, v, tq):
    s = q.shape[1]
    nq = s // tq
    scale = 1.0 / float(QK_NOPE + QK_ROPE) ** 0.5

    def body(q_ref, k_ref, v_ref, o_ref, lse_ref):
        qi, hh = pl.program_id(1), pl.program_id(2)
        qv = q_ref[0]

        def step(kj, carry, masked):
            m, l, acc = carry
            k0 = pl.multiple_of(kj * tq, tq)
            sc = _dot_nt(qv, k_ref[0, pl.ds(k0, tq), :]) * scale
            if masked:
                sc = jnp.where(_chunk_mask(qi * tq, k0, tq, tq), sc, NEG)
            m_new = jnp.maximum(m, jnp.max(sc, axis=-1, keepdims=True))
            alpha = jnp.exp(m - m_new)
            p = jnp.exp(sc - m_new)
            l = alpha * l + jnp.sum(p, axis=-1, keepdims=True)
            acc = alpha * acc + _dot_nn(p, v_ref[0, pl.ds(k0, tq), :])
            return m_new, l, acc

        init = (jnp.full((tq, 1), NEG, F32), jnp.zeros((tq, 1), F32), jnp.zeros((tq, HEAD_PAD), F32))
        carry = lax.fori_loop(0, qi, lambda kj, c: step(kj, c, False), init)
        m, l, acc = step(qi, carry, True)
        o = acc / l
        lse_ref[0] = m + jnp.log(l)

        @pl.when(hh == 0)
        def _():
            o_ref[...] = o

        @pl.when(hh == 1)
        def _():
            o_ref[...] += o

    head = lambda hp, qi, hh: 2 * hp + hh
    return pl.pallas_call(
        body, name="attn_fwd", grid=(HEADS // 2, nq, 2),
        in_specs=[pl.BlockSpec((1, tq, HEAD_PAD), lambda hp, qi, hh: (head(hp, qi, hh), qi, 0)),
                  pl.BlockSpec((1, s, HEAD_PAD), lambda hp, qi, hh: (head(hp, qi, hh), 0, 0)),
                  pl.BlockSpec((1, s, HEAD_PAD), lambda hp, qi, hh: (head(hp, qi, hh), 0, 0))],
        out_specs=[pl.BlockSpec((tq, HEAD_PAD), lambda hp, qi, hh: (qi, hp)),
                   pl.BlockSpec((1, tq, 1), lambda hp, qi, hh: (head(hp, qi, hh), qi, 0))],
        out_shape=[_sds((s, HEADS * V_HEAD)), _sds((HEADS, s, 1))],
        compiler_params=_cp(3))(q, k, v)


def _attn_bwd(q, k, v, do, o, lse, tq):
    s = q.shape[1]
    nq = s // tq
    scale = 1.0 / float(QK_NOPE + QK_ROPE) ** 0.5

    def body(q_ref, k_ref, v_ref, do_ref, o_ref, lse_ref, dq_ref, dk_ref, dv_ref):
        hh, kj = pl.program_id(1), pl.program_id(2)

        @pl.when(kj == 0)
        def _():
            dq_ref[...] = jnp.zeros_like(dq_ref)

        kv, vv = k_ref[0], v_ref[0]
        lane = lax.broadcasted_iota(jnp.int32, (tq, HEAD_PAD), 1)
        mine = lax.shift_right_logical(lane, 6) == hh

        def step(qi, carry, masked):
            dk_acc, dv_acc = carry
            q0 = pl.multiple_of(qi * tq, tq)
            qv = q_ref[0, pl.ds(q0, tq), :]
            dov = do_ref[pl.ds(q0, tq), :]
            delta = jnp.sum(jnp.where(mine, dov * o_ref[pl.ds(q0, tq), :], 0.0), axis=-1, keepdims=True)
            sc = _dot_nt(qv, kv) * scale
            if masked:
                sc = jnp.where(_chunk_mask(q0, kj * tq, tq, tq), sc, NEG)
            p = jnp.exp(sc - lse_ref[0, pl.ds(q0, tq), :])
            do_b = dov.astype(MXU_DTYPE)
            ds = (p * (_dot_nt(do_b, vv) - delta) * scale).astype(MXU_DTYPE)
            dv_acc = dv_acc + _dot_tn(p, do_b)
            dk_acc = dk_acc + _dot_tn(ds, qv)
            dq_ref[0, pl.ds(q0, tq), :] += _dot_nn(ds, kv)
            return dk_acc, dv_acc

        zero = jnp.zeros((tq, HEAD_PAD), F32)
        carry = step(kj, (zero, zero), True)
        dk_acc, dv_acc = lax.fori_loop(kj + 1, nq, lambda qi, c: step(qi, c, False), carry)
        dk_ref[0] = dk_acc
        dv_ref[0] = dv_acc

    head = lambda hp, hh, kj: 2 * hp + hh
    full = pl.BlockSpec((1, s, HEAD_PAD), lambda hp, hh, kj: (head(hp, hh, kj), 0, 0))
    blk = pl.BlockSpec((1, tq, HEAD_PAD), lambda hp, hh, kj: (head(hp, hh, kj), kj, 0))
    pair = pl.BlockSpec((s, HEAD_PAD), lambda hp, hh, kj: (0, hp))
    return pl.pallas_call(
        body, name="attn_bwd", grid=(HEADS // 2, 2, nq),
        in_specs=[full, blk, blk, pair, pair, pl.BlockSpec((1, s, 1), lambda hp, hh, kj: (head(hp, hh, kj), 0, 0))],
        out_specs=[full, blk, blk], out_shape=[_sds((HEADS, s, HEAD_PAD))] * 3,
        compiler_params=_cp(3))(q, k, v, do, o, lse)


def _sc_conv(u, ubuf, w_ref, b_ref, ts):
    return (w_ref[2:3, :] * u + w_ref[1:2, :] * ubuf[pl.ds(SC_HALO - 1, ts), :]
            + w_ref[0:1, :] * ubuf[pl.ds(SC_HALO - 2, ts), :] + b_ref[...])


def _even_gate_fwd(z, o, sc_w, sc_b, ts):
    s = z.shape[0]
    w = SC_WIDTH

    def body(ab_ref, ac_ref, ax_ref, ag_ref, bg_ref, hc_ref, hx_ref, o_ref, w_ref, b_ref, y_ref, ubuf):
        i = pl.program_id(0)
        u = ac_ref[...] * ax_ref[...]
        ubuf[0:SC_HALO, :] = jnp.where(i > 0, hc_ref[...] * hx_ref[...], 0.0)
        ubuf[SC_HALO:, :] = u
        conv = _sc_conv(u, ubuf, w_ref, b_ref, ts)
        y_ref[:, 0:w] = (ab_ref[...] * conv * _silu(ag_ref[...])).astype(y_ref.dtype)
        y_ref[:, w:] = (o_ref[...] * _silu(bg_ref[...])).astype(y_ref.dtype)

    return pl.pallas_call(
        body, name="even_gate_fwd", grid=(s // ts,),
        in_specs=[_rows(ts, w, 0), _rows(ts, w, 1), _rows(ts, w, 2), _rows(ts, w, 3), _rows(ts, w, 5),
                  _prev_halo(ts, SC_HALO, w, 1), _prev_halo(ts, SC_HALO, w, 2), _rows(ts, w),
                  _vec(w, 0, SC_KERNEL), _vec(w)],
        out_specs=_rows(ts, 2 * w), out_shape=_sds((s, 2 * w), MXU_DTYPE),
        scratch_shapes=[pltpu.VMEM((ts + SC_HALO, w), F32)],
        compiler_params=_cp(1))(z, z, z, z, z, z, z, o, sc_w, sc_b)


def _even_gate_bwd(dy, z, o, sc_w, sc_b, ts):
    s = z.shape[0]
    w = SC_WIDTH
    n = s // ts

    def body(dya_ref, dyb_ref, dyan_ref, ab_ref, ac_ref, ax_ref, ag_ref, bg_ref, hc_ref, hx_ref, abn_ref, agn_ref,
             o_ref, w_ref, b_ref, dz_ref, do_ref, dw_ref, db_ref, ubuf, dbuf):
        i = pl.program_id(0)
        ab, ac, ax, ag, bg = ab_ref[...], ac_ref[...], ax_ref[...], ag_ref[...], bg_ref[...]
        dya, dyb = dya_ref[...], dyb_ref[...]
        u = ac * ax
        ubuf[0:SC_HALO, :] = jnp.where(i > 0, hc_ref[...] * hx_ref[...], 0.0)
        ubuf[SC_HALO:, :] = u
        conv = _sc_conv(u, ubuf, w_ref, b_ref, ts)
        sg = _silu(ag)
        dconv = dya * ab * sg
        dbuf[0:ts, :] = dconv
        dbuf[ts:, :] = jnp.where(i < n - 1, dyan_ref[...] * abn_ref[...] * _silu(agn_ref[...]), 0.0)
        du = w_ref[2:3, :] * dconv + w_ref[1:2, :] * dbuf[pl.ds(1, ts), :] + w_ref[0:1, :] * dbuf[pl.ds(2, ts), :]
        dz_ref[:, 0:w] = (dya * conv * sg).astype(dz_ref.dtype)
        dz_ref[:, w:2 * w] = (du * ax).astype(dz_ref.dtype)
        dz_ref[:, 2 * w:3 * w] = (du * ac).astype(dz_ref.dtype)
        dz_ref[:, 3 * w:4 * w] = (dya * ab * conv * _dsilu(ag)).astype(dz_ref.dtype)
        dz_ref[:, 4 * w:5 * w] = jnp.zeros((ts, w), dz_ref.dtype)
        dz_ref[:, 5 * w:] = (dyb * o_ref[...] * _dsilu(bg)).astype(dz_ref.dtype)
        do_ref[...] = dyb * _silu(bg)

        @pl.when(i == 0)
        def _():
            dw_ref[...] = jnp.zeros_like(dw_ref)
            db_ref[...] = jnp.zeros_like(db_ref)

        dw_ref[0:1, :] += jnp.sum(dconv * ubuf[pl.ds(SC_HALO - 2, ts), :], axis=0, keepdims=True)
        dw_ref[1:2, :] += jnp.sum(dconv * ubuf[pl.ds(SC_HALO - 1, ts), :], axis=0, keepdims=True)
        dw_ref[2:3, :] += jnp.sum(dconv * u, axis=0, keepdims=True)
        db_ref[...] += jnp.sum(dconv, axis=0, keepdims=True)

    return pl.pallas_call(
        body, name="even_gate_bwd", grid=(n,),
        in_specs=[_rows(ts, w, 0), _rows(ts, w, 1), _next_halo(ts, SC_HALO, w, 0, s),
                  _rows(ts, w, 0), _rows(ts, w, 1), _rows(ts, w, 2), _rows(ts, w, 3), _rows(ts, w, 5),
                  _prev_halo(ts, SC_HALO, w, 1), _prev_halo(ts, SC_HALO, w, 2),
                  _next_halo(ts, SC_HALO, w, 0, s), _next_halo(ts, SC_HALO, w, 3, s),
                  _rows(ts, w), _vec(w, 0, SC_KERNEL), _vec(w)],
        out_specs=[_rows(ts, EVEN_PAD), _rows(ts, w), _vec(w, 0, SC_KERNEL), _vec(w)],
        out_shape=[_sds((s, EVEN_PAD), MXU_DTYPE), _sds((s, w)), _sds((SC_KERNEL, w)), _sds((1, w))],
        scratch_shapes=[pltpu.VMEM((ts + SC_HALO, w), F32), pltpu.VMEM((ts + SC_HALO, w), F32)],
        compiler_params=_cp(1))(dy, dy, dy, z, z, z, z, z, z, z, z, z, o, sc_w, sc_b)


def _ln_act(uc, sg, g, b):
    mu = jnp.mean(uc, axis=-1, keepdims=True)
    var = jnp.mean(jnp.square(uc - mu), axis=-1, keepdims=True)
    return _silu((uc - mu) * lax.rsqrt(var + EPS) * g + b) * _silu(sg)


def _shifted_copies(buf, shifted, rows):
    for b in range(1, SUBLANES):
        shifted[b - 1, 0:rows, :] = buf[pl.ds(b, rows), :]


def _rows_at(buf, shifted, start, n):
    a, b = divmod(start, SUBLANES)
    return buf[pl.ds(SUBLANES * a, n), :] if b == 0 else shifted[b - 1, pl.ds(SUBLANES * a, n), :]


def _odd_fwd(z, conv_w, conv_b, ln_g, ln_b, ts):
    s = z.shape[0]
    d = D_MODEL
    k = CONF_KERNEL

    def body(val_ref, glu_ref, sg_ref, hval_ref, hglu_ref, w_ref, b_ref, g_ref, beta_ref, y_ref, uc_ref, ubuf, ush):
        i = pl.program_id(0)
        ubuf[0:CONF_HALO, :] = jnp.where(i > 0, hval_ref[...] * _sigmoid(hglu_ref[...]), 0.0)
        ubuf[CONF_HALO:, :] = val_ref[...] * _sigmoid(glu_ref[...])
        _shifted_copies(ubuf, ush, ts + CONF_HALO - SUBLANES)
        for r0 in range(0, ts, CONV_ROWS):
            acc = jnp.broadcast_to(b_ref[...], (CONV_ROWS, d))
            for j in range(k):
                acc = acc + w_ref[j:j + 1, :] * _rows_at(ubuf, ush, r0 + CONF_HALO - (k - 1) + j, CONV_ROWS)
            uc_ref[r0:r0 + CONV_ROWS, :] = acc
        y_ref[...] = _ln_act(uc_ref[...], sg_ref[...], g_ref[...], beta_ref[...]).astype(y_ref.dtype)

    return pl.pallas_call(
        body, name="odd_fwd", grid=(s // ts,),
        in_specs=[_rows(ts, d, 0), _rows(ts, d, 1), _rows(ts, d, 2),
                  _prev_halo(ts, CONF_HALO, d, 0), _prev_halo(ts, CONF_HALO, d, 1),
                  _vec(d, 0, k), _vec(d), _vec(d), _vec(d)],
        out_specs=[_rows(ts, d), _rows(ts, d)], out_shape=[_sds((s, d), MXU_DTYPE), _sds((s, d))],
        scratch_shapes=[pltpu.VMEM((ts + CONF_HALO, d), F32),
                        pltpu.VMEM((SUBLANES - 1, ts + CONF_HALO - SUBLANES, d), F32)],
        compiler_params=_cp(1))(z, z, z, z, z, conv_w, conv_b, ln_g, ln_b)


def _odd_bwd(dy, z, uc, conv_w, ln_g, ln_b, ts):
    s = z.shape[0]
    d = D_MODEL
    k = CONF_KERNEL
    n = s // ts

    def body(dy_ref, dyn_ref, val_ref, glu_ref, sg_ref, sgn_ref, hval_ref, hglu_ref, uc_ref, ucn_ref,
             w_ref, g_ref, beta_ref, dz_ref, dw_ref, db_ref, dg_ref, dbeta_ref, ubuf, dbuf, ush, dsh, dw_acc):
        i = pl.program_id(0)
        val, glu = val_ref[...], glu_ref[...]
        sig = _sigmoid(glu)
        ubuf[0:CONF_HALO, :] = jnp.where(i > 0, hval_ref[...] * _sigmoid(hglu_ref[...]), 0.0)
        ubuf[CONF_HALO:, :] = val * sig
        _, vjp = jax.vjp(_ln_act, uc_ref[...], sg_ref[...], g_ref[...], beta_ref[...])
        duc, dsg, dg, dbeta = vjp(dy_ref[...])
        _, vjp_n = jax.vjp(_ln_act, ucn_ref[...], sgn_ref[...], g_ref[...], beta_ref[...])
        dbuf[0:ts, :] = duc
        dbuf[ts:, :] = jnp.where(i < n - 1, vjp_n(dyn_ref[...])[0], 0.0)
        dz_ref[:, 2 * d:] = dsg.astype(dz_ref.dtype)
        _shifted_copies(ubuf, ush, ts + CONF_HALO - SUBLANES)
        _shifted_copies(dbuf, dsh, ts + CONF_HALO - SUBLANES)

        @pl.when(i == 0)
        def _():
            dw_acc[...] = jnp.zeros_like(dw_acc)
            db_ref[...] = jnp.zeros_like(db_ref)
            dg_ref[...] = jnp.zeros_like(dg_ref)
            dbeta_ref[...] = jnp.zeros_like(dbeta_ref)

        db_ref[...] += jnp.sum(duc, axis=0, keepdims=True)
        dg_ref[...] += dg
        dbeta_ref[...] += dbeta
        for r0 in range(0, ts, CONV_ROWS):
            acc = jnp.zeros((CONV_ROWS, d), F32)
            for j in range(k):
                acc = acc + w_ref[j:j + 1, :] * _rows_at(dbuf, dsh, r0 + (k - 1) - j, CONV_ROWS)
            sig_r = sig[r0:r0 + CONV_ROWS, :]
            dz_ref[r0:r0 + CONV_ROWS, 0:d] = (acc * sig_r).astype(dz_ref.dtype)
            dz_ref[r0:r0 + CONV_ROWS, d:2 * d] = (acc * val[r0:r0 + CONV_ROWS, :] * sig_r * (1.0 - sig_r)).astype(dz_ref.dtype)
        for j in range(k):
            prod = duc * _rows_at(ubuf, ush, CONF_HALO - (k - 1) + j, ts)
            dw_acc[j] += jnp.sum(prod.reshape(ts // SUBLANES, SUBLANES, d), axis=0)

        @pl.when(i == n - 1)
        def _():
            dw_ref[...] = jnp.sum(dw_acc[...], axis=1)

    return pl.pallas_call(
        body, name="odd_bwd", grid=(n,),
        in_specs=[_rows(ts, d), _next_halo(ts, CONF_HALO, d, 0, s),
                  _rows(ts, d, 0), _rows(ts, d, 1), _rows(ts, d, 2), _next_halo(ts, CONF_HALO, d, 2, s),
                  _prev_halo(ts, CONF_HALO, d, 0), _prev_halo(ts, CONF_HALO, d, 1),
                  _rows(ts, d), _next_halo(ts, CONF_HALO, d, 0, s),
                  _vec(d, 0, k), _vec(d), _vec(d)],
        out_specs=[_rows(ts, ODD_IN), _vec(d, 0, k), _vec(d), _vec(d), _vec(d)],
        out_shape=[_sds((s, ODD_IN), MXU_DTYPE), _sds((k, d)), _sds((1, d)), _sds((1, d)), _sds((1, d))],
        scratch_shapes=[pltpu.VMEM((ts + CONF_HALO, d), F32), pltpu.VMEM((ts + CONF_HALO, d), F32)]
        + [pltpu.VMEM((SUBLANES - 1, ts + CONF_HALO - SUBLANES, d), F32)] * 2 + [pltpu.VMEM((k, SUBLANES, d), F32)],
        compiler_params=_cp(1))(dy, dy, z, z, z, z, z, z, uc, uc, conv_w, ln_g, ln_b)


def _local_step(x, target, cos, sin, mod, p, layer_weights, fwd_dep=None, grads_done=None):
    s = x.shape[0]
    tsf, tsb = min(512, s // 2), min(256, s // 2)
    tq = min(512, s // 2)
    row1 = lambda a, i: a[i:i + 1]
    saved = []
    for layer in range(DEPTH):
        i = layer // 2
        mod_l = row1(mod, layer)
        wl = layer_weights(layer, x)
        h = _pre_fwd(x, row1(p["pre_norm_g"], layer), mod_l, tsf, fwd_dep if layer == 0 else None)
        if layer % 2 == 0:
            z = _mm(h, wl["w_in"], "nn", F32, 256, EVEN_PAD, "even_in_fwd")
            q, k, v = _mla_prep_fwd(z, cos, sin, row1(p["even_q_norm_g"], i), row1(p["even_kv_norm_g"], i),
                                    wl["wq"], wl["wq_rot"], wl["wuk"], wl["wuv"], tsf)
            o, lse = _attn_fwd(q, k, v, tq)
            y = _even_gate_fwd(z, o, wl["sc_conv_w"], row1(p["even_sc_conv_b"], i), tsf)
            yo = _mm(y, wl["w_out"], "nn", F32, 512, 1024, "even_out_fwd")
            saved.append((x, h, z, y, yo, wl, (q, k, v, o, lse)))
        else:
            z = _mm(h, wl["w_in"], "nn", F32, 256, ODD_IN, "odd_in_fwd")
            y, uc = _odd_fwd(z, wl["conv_w"], wl["conv_b"], wl["ln_g"], wl["ln_b"], tsf)
            yo = _mm(y, wl["w_out"], "nn", F32, 512, 1024, "odd_out_fwd")
            saved.append((x, h, z, y, yo, wl, uc))
        x = _post_fwd(x, yo, row1(p["post_norm_g"], layer), mod_l, tsf)

    loss, dx = _loss_fwd_bwd(x, target, tsf)

    g = {n: [None] * (DEPTH if n in ("pre_norm_g", "post_norm_g") else N_PAIRS) for n in (
        "pre_norm_g", "post_norm_g", "even_sc_conv_w", "even_sc_conv_b", "even_q_norm_g", "even_kv_norm_g",
        "odd_conv_w", "odd_conv_b", "odd_ln_g", "odd_ln_b")}
    dmod = [None] * DEPTH
    dep = None
    for layer in reversed(range(DEPTH)):
        i = layer // 2
        mod_l = row1(mod, layer)
        x_in, h, z, y, yo, wl, extra = saved[layer]
        dyo, dgate, g["post_norm_g"][layer] = _post_bwd(dx, yo, row1(p["post_norm_g"], layer), mod_l, tsb, dep)
        bufs = {}
        if layer % 2 == 0:
            q, k, v, o, lse = extra
            dy = _mm(dyo, wl["w_out"], "nt", F32, 512, 1024, "even_out_bwd_x")
            bufs["even_w_out"] = _mm_tn_shards(y, dyo, "rows", "even_out_bwd_w")
            dz, do, g["even_sc_conv_w"][i], g["even_sc_conv_b"][i] = _even_gate_bwd(
                dy, z, o, wl["sc_conv_w"], row1(p["even_sc_conv_b"], i), tsb)
            dq, dk, dv = _attn_bwd(q, k, v, do, o, lse, tq)
            dz, bufs["even_mla"], g["even_q_norm_g"][i], g["even_kv_norm_g"][i] = _mla_prep_bwd(
                dz, dq, dk, dv, z, cos, sin, row1(p["even_q_norm_g"], i), row1(p["even_kv_norm_g"], i),
                wl["wq"], wl["wuk"], wl["wuv"], tsb)
            bufs["even_w_in"] = _ein_to_shards(_mm(h, dz, "tn", F32, D_MODEL, 512, "even_in_bwd_w"))
        else:
            uc = extra
            dy = _mm(dyo, wl["w_out"], "nt", F32, 512, 1024, "odd_out_bwd_x")
            bufs["odd_w_out"] = _mm_tn_shards(y, dyo, "rows", "odd_out_bwd_w")
            dz, g["odd_conv_w"][i], g["odd_conv_b"][i], g["odd_ln_g"][i], g["odd_ln_b"][i] = _odd_bwd(
                dy, z, uc, wl["conv_w"], wl["ln_g"], wl["ln_b"], tsb)
            bufs["odd_w_in"] = _mm_tn_shards(h, dz, "cols", "odd_in_bwd_w")
        dx, dshift, dscale, g["pre_norm_g"][layer] = _pre_bwd(
            dz, wl["w_in"], dx, x_in, row1(p["pre_norm_g"], layer), mod_l, tsb)
        dmod[layer] = jnp.concatenate([dshift, dscale, dgate], axis=-1)
        dep = grads_done(layer, bufs, dx) if grads_done is not None else None
    stack = lambda parts: jnp.stack([a[0] if a.shape[0] == 1 and a.ndim == 2 else a for a in parts])
    small = {n: stack(parts) for n, parts in g.items()}
    small["dmod"] = jnp.concatenate(dmod, axis=0)
    return loss, dx, small


def _uq_to_heads(w):
    w = w.reshape(N_CHIPS, Q_LORA, 2, QK_NOPE + QK_ROPE).transpose(0, 2, 1, 3).reshape(HEADS, Q_LORA, QK_NOPE + QK_ROPE)
    half = QK_ROPE // 2
    rotated = jnp.concatenate([jnp.zeros_like(w[..., :QK_NOPE]), -w[..., QK_NOPE + half:], w[..., QK_NOPE:QK_NOPE + half]],
                              axis=-1)
    pad = ((0, 0), (0, 0), (0, HEAD_PAD - QK_NOPE - QK_ROPE))
    return jnp.pad(w, pad), jnp.pad(rotated, pad)


def _ukv_to_heads(w):
    w = w.reshape(N_CHIPS, KV_LORA, 2, QK_NOPE + V_HEAD).transpose(0, 2, 1, 3).reshape(HEADS, KV_LORA, QK_NOPE + V_HEAD)
    wk = jnp.pad(w[..., :QK_NOPE], ((0, 0), (0, 0), (0, HEAD_PAD - QK_NOPE)))
    wv = w[..., QK_NOPE:]
    zero = jnp.zeros_like(wv)
    odd = (jnp.arange(HEADS) % 2 == 1)[:, None, None]
    wv = jnp.concatenate([jnp.where(odd, zero, wv), jnp.where(odd, wv, zero)], axis=-1)
    return wk, wv


def _mla_local(q):
    blocks = q.reshape(2, MLA_ROWS, HEAD_PAD)
    uq = jnp.concatenate([blocks[r, :Q_LORA, :QK_NOPE + QK_ROPE] for r in range(2)], axis=-1)
    ukv = jnp.concatenate(
        [jnp.concatenate([blocks[r, Q_LORA:Q_LORA + KV_LORA, :QK_NOPE],
                          blocks[r, Q_LORA + KV_LORA:, V_HEAD * r:V_HEAD * (r + 1)]], axis=-1) for r in range(2)], axis=-1)
    return uq, ukv


def _place():
    return lax.axis_index("x"), lax.axis_index("y"), lax.axis_index("c")


def _flip(v, bit):
    return 1 - v if bit else v


def _sem(a, k):
    return a * (N_CHIPS - 1) + k - 1


def _remote(src, dst, send_sem, recv_sem, peer):
    return pltpu.make_async_remote_copy(src_ref=src, dst_ref=dst, send_sem=send_sem, recv_sem=recv_sem,
                                        device_id=peer, device_id_type=MESH)


_VMEM_SPEC = pl.BlockSpec(memory_space=pltpu.VMEM)
_HBM_SPEC = pl.BlockSpec(memory_space=pl.ANY)


def _ada_fwd(c8, ada_w, ada_b_sh):
    depth, d, cols = ada_w.shape

    def body(c_ref, w_ref, b_ref, call_ref, mod_ref, s1, r1, s2, r2):
        x, y, c = _place()
        chip = 2 * x + y
        me = 2 * chip + c
        call_ref[me] = c_ref[...]
        sends = []
        for k in range(1, N_DEV):
            peer = (_flip(x, k & 4), _flip(y, k & 2), _flip(c, k & 1))
            cp = _remote(c_ref, call_ref.at[me], s1.at[k - 1], r1.at[k - 1], peer)
            cp.start()
            sends.append(cp)
        for k in range(1, N_DEV):
            src = 4 * _flip(x, k & 4) + 2 * _flip(y, k & 2) + _flip(c, k & 1)
            _remote(c_ref, call_ref.at[src], s1.at[k - 1], r1.at[k - 1], (x, y, c)).wait_recv()
        act = _silu(call_ref[...]).reshape(N_DEV * 8, d)
        for l in range(depth):
            mod_ref[chip, l] = _dot_nn(act, w_ref[l]) + b_ref[l:l + 1, :]
        for k in range(1, N_CHIPS):
            peer = (_flip(x, k & 2), _flip(y, k & 1), c)
            cp = _remote(mod_ref.at[chip], mod_ref.at[chip], s2.at[k - 1], r2.at[k - 1], peer)
            cp.start()
            sends.append(cp)
        for k in range(1, N_CHIPS):
            src = 2 * _flip(x, k & 2) + _flip(y, k & 1)
            _remote(mod_ref.at[src], mod_ref.at[src], s2.at[k - 1], r2.at[k - 1], (x, y, c)).wait_recv()
        for cp in sends:
            cp.wait_send()

    return pl.pallas_call(
        body, name="ada_fwd", in_specs=[_VMEM_SPEC] * 3, out_specs=[_VMEM_SPEC] * 2,
        out_shape=[_sds((N_DEV, 8, d)), _sds((N_CHIPS, depth, N_DEV * 8, cols))],
        scratch_shapes=[pltpu.SemaphoreType.DMA((N_DEV - 1,)), pltpu.SemaphoreType.DMA((N_DEV - 1,)),
                        pltpu.SemaphoreType.DMA((N_CHIPS - 1,)), pltpu.SemaphoreType.DMA((N_CHIPS - 1,))],
        compiler_params=pltpu.CompilerParams(vmem_limit_bytes=VMEM_LIMIT_V7X))(c8, ada_w, ada_b_sh)


def _ada_bwd(c_t, dmod_sh):
    depth, n, cols = dmod_sh.shape
    d = c_t.shape[0]
    tr = 256

    def body(c_ref, dm_ref, o_ref):
        act = _silu(c_ref[...])
        acc = act[:, 0:1] * dm_ref[0, 0:1, :]
        for e in range(1, n):
            acc = acc + act[:, e:e + 1] * dm_ref[0, e:e + 1, :]
        o_ref[0] = acc

    return pl.pallas_call(
        body, name="ada_bwd", grid=(depth, d // tr),
        in_specs=[pl.BlockSpec((tr, n), lambda l, i: (i, 0)), pl.BlockSpec((1, n, cols), lambda l, i: (l, 0, 0))],
        out_specs=pl.BlockSpec((1, tr, cols), lambda l, i: (l, i, 0)), out_shape=_sds((depth, d, cols)),
        compiler_params=_cp(2))(c_t, dmod_sh)


def _gathered_shape(shape, how):
    if how == "slot":
        return (N_CHIPS,) + shape
    r, cc = shape
    return (r, N_CHIPS * cc) if how == "cols" else (N_CHIPS * r, cc)


def _gathered_part(ref, shape, how, chip):
    if how == "slot":
        return ref.at[chip]
    if how == "cols":
        return ref.at[:, pl.ds(pl.multiple_of(chip * shape[1], 128), shape[1])]
    return ref.at[pl.ds(pl.multiple_of(chip * shape[0], 8), shape[0]), :]


_SEM_SPEC = pl.BlockSpec(memory_space=pltpu.SEMAPHORE)
_TOKEN = jax.ShapeDtypeStruct((8, 128), F32)
_SPLIT_COPY = pltpu.CompilerParams(has_side_effects=pltpu.SideEffectType.DATAFLOW_SIDE_EFFECTING)


def _in_hbm(a):
    return pltpu.with_memory_space_constraint(a, pltpu.HBM)


def _gather_start(items, gathered, name, after=()):
    n = len(items)

    def body(*refs):
        ins, outs = refs[:n], refs[n:2 * n]
        send_sems, recv_sems = refs[2 * n + len(after)], refs[2 * n + len(after) + 1]
        x, y, c = _place()
        for a in range(n):
            for k in range(1, N_CHIPS):
                part = _gathered_part(outs[a], items[a][0].shape, items[a][1], 2 * x + y)
                _remote(ins[a], part, send_sems.at[_sem(a, k)], recv_sems.at[_sem(a, k)],
                        (_flip(x, k & 2), _flip(y, k & 1), c)).start()
        refs[-1][...] = jnp.zeros(_TOKEN.shape, _TOKEN.dtype)

    arrays = [_in_hbm(a) for a, _ in items] + [_in_hbm(a) for a in gathered]
    res = pl.pallas_call(
        body, name=name, in_specs=[_HBM_SPEC] * (2 * n + len(after)),
        out_specs=[_SEM_SPEC, _SEM_SPEC] + [_HBM_SPEC] * (2 * n) + [_VMEM_SPEC],
        out_shape=[pltpu.SemaphoreType.DMA((n * (N_CHIPS - 1),)), pltpu.SemaphoreType.DMA((n * (N_CHIPS - 1),))]
        + [pltpu.HBM(a.shape, a.dtype) for a in arrays] + [_TOKEN],
        input_output_aliases={a: 2 + a for a in range(2 * n)}, compiler_params=_SPLIT_COPY)(*arrays, *after)
    return res[0], res[1], res[2:2 + n], res[2 + n:2 + 2 * n], res[-1]


def _gather_wait(items, started, after, name):
    n = len(items)
    send_sems, recv_sems, shards, gathered, _ = started

    def body(*refs):
        ins, outs, send_sems, recv_sems = refs[:n], refs[n:2 * n], refs[2 * n], refs[2 * n + 1]
        x, y, c = _place()
        for a in range(n):
            for k in range(1, N_CHIPS):
                part = _gathered_part(outs[a], items[a][0].shape, items[a][1], 2 * _flip(x, k & 2) + _flip(y, k & 1))
                cp = _remote(ins[a], part, send_sems.at[_sem(a, k)], recv_sems.at[_sem(a, k)], (x, y, c))
                cp.wait_send()
                cp.wait_recv()

    res = pl.pallas_call(
        body, name=name, in_specs=[_HBM_SPEC] * (2 * n) + [_SEM_SPEC, _SEM_SPEC] + [_HBM_SPEC] * len(after),
        out_specs=[_HBM_SPEC] * (2 * n), out_shape=[pltpu.HBM(a.shape, a.dtype) for a in (*shards, *gathered)],
        input_output_aliases={a: a for a in range(2 * n)}, compiler_params=_SPLIT_COPY)(
            *shards, *gathered, send_sems, recv_sems, *after)
    return res[n:]


def _rs_start(bufs, name, after=()):
    n = len(bufs)

    def body(*refs):
        srcs, lands = refs[:n], refs[n:2 * n]
        send_sems, recv_sems = refs[2 * n + len(after)], refs[2 * n + len(after) + 1]
        x, y, c = _place()
        for a in range(n):
            for k in range(1, N_CHIPS):
                tx, ty = _flip(x, k & 2), _flip(y, k & 1)
                _remote(srcs[a].at[2 * tx + ty], lands[a].at[k - 1], send_sems.at[_sem(a, k)], recv_sems.at[_sem(a, k)],
                        (tx, ty, c)).start()
        refs[-1][...] = jnp.zeros(_TOKEN.shape, _TOKEN.dtype)

    arrays = [_in_hbm(b) for b in bufs] + [_in_hbm(lax.empty((N_CHIPS - 1,) + b.shape[1:], b.dtype)) for b in bufs]
    res = pl.pallas_call(
        body, name=name, in_specs=[_HBM_SPEC] * (2 * n + len(after)),
        out_specs=[_SEM_SPEC, _SEM_SPEC] + [_HBM_SPEC] * (2 * n) + [_VMEM_SPEC],
        out_shape=[pltpu.SemaphoreType.DMA((n * (N_CHIPS - 1),)), pltpu.SemaphoreType.DMA((n * (N_CHIPS - 1),))]
        + [pltpu.HBM(a.shape, a.dtype) for a in arrays] + [_TOKEN],
        input_output_aliases={a: 2 + a for a in range(2 * n)}, compiler_params=_SPLIT_COPY)(*arrays, *after)
    return res[0], res[1], res[2:2 + n], res[2 + n:2 + 2 * n], res[-1]


def _rs_wait(started, after, name):
    send_sems, recv_sems, bufs, lands, _ = started
    n = len(bufs)

    def body(*refs):
        srcs, lnds, send_sems, recv_sems = refs[:n], refs[n:2 * n], refs[2 * n], refs[2 * n + 1]
        x, y, c = _place()
        for a in range(n):
            for k in range(1, N_CHIPS):
                cp = _remote(srcs[a].at[0], lnds[a].at[k - 1], send_sems.at[_sem(a, k)], recv_sems.at[_sem(a, k)], (x, y, c))
                cp.wait_send()
                cp.wait_recv()

    res = pl.pallas_call(
        body, name=name, in_specs=[_HBM_SPEC] * (2 * n) + [_SEM_SPEC, _SEM_SPEC] + [_HBM_SPEC] * len(after),
        out_specs=[_HBM_SPEC] * (2 * n), out_shape=[pltpu.HBM(a.shape, a.dtype) for a in (*bufs, *lands)],
        input_output_aliases={a: a for a in range(2 * n)}, compiler_params=_SPLIT_COPY)(
            *bufs, *lands, send_sems, recv_sems, *after)
    return res[:n], res[n:]


def _place_own(shard, how, chip_idx):
    r, cc = shard.shape
    block, index = {"slot": ((1, r, cc), lambda i, c: (c[0], 0, 0)), "cols": ((r, cc), lambda i, c: (0, c[0])),
                    "rows": ((r, cc), lambda i, c: (c[0], 0))}[how]

    def body(c_ref, in_ref, o_ref):
        del c_ref
        o_ref[...] = in_ref[...].reshape(o_ref.shape)

    return pl.pallas_call(
        body, name="place_own", out_shape=_sds(_gathered_shape(shard.shape, how), shard.dtype),
        grid_spec=pltpu.PrefetchScalarGridSpec(
            num_scalar_prefetch=1, grid=(1,), in_specs=[pl.BlockSpec((r, cc), lambda i, c: (0, 0))],
            out_specs=pl.BlockSpec(block, index)),
        compiler_params=_cp(1))(chip_idx, shard)


def _gather_chips(items, after=()):
    n = n_all = len(items)
    arrays = [a for a, _ in items]

    def body(*refs):
        ins, outs = refs[:n_all], refs[-n_all - 3:-3]
        send_sems, recv_sems, local_sems = refs[-3:]
        x, y, c = _place()
        chip = 2 * x + y
        part = lambda a, j: _gathered_part(outs[a], items[a][0].shape, items[a][1], j)
        local = [pltpu.make_async_copy(ins[a], part(a, chip), local_sems.at[a]) for a in range(n_all)]
        for cp in local[:n]:
            cp.start()
        sends = []
        for a in range(n):
            for k in range(1, N_CHIPS):
                peer = (_flip(x, k & 2), _flip(y, k & 1), c)
                cp = _remote(ins[a], part(a, chip), send_sems.at[_sem(a, k)], recv_sems.at[_sem(a, k)], peer)
                cp.start()
                sends.append(cp)
        for cp in local[n:]:
            cp.start()
        for a in range(n):
            for k in range(1, N_CHIPS):
                src = 2 * _flip(x, k & 2) + _flip(y, k & 1)
                _remote(ins[a], part(a, src), send_sems.at[_sem(a, k)], recv_sems.at[_sem(a, k)], (x, y, c)).wait_recv()
        for cp in sends:
            cp.wait_send()
        for cp in local:
            cp.wait()

    return pl.pallas_call(
        body, name="gather_chips", in_specs=[_HBM_SPEC] * (n_all + len(after)), out_specs=[_HBM_SPEC] * n_all,
        out_shape=[_sds(_gathered_shape(a.shape, how), a.dtype) for a, how in items],
        scratch_shapes=[pltpu.SemaphoreType.DMA((n * (N_CHIPS - 1),)), pltpu.SemaphoreType.DMA((n * (N_CHIPS - 1),)),
                        pltpu.SemaphoreType.DMA((n_all,))])(*arrays, *after)


def _gather_sum_all(small):
    r, w = small.shape

    def body(in_ref, all_ref, sum_ref, send_sems, recv_sems):
        x, y, c = _place()
        me = 4 * x + 2 * y + c
        all_ref[me] = in_ref[...]
        sends = []
        for k in range(1, N_DEV):
            peer = (_flip(x, k & 4), _flip(y, k & 2), _flip(c, k & 1))
            cp = _remote(in_ref, all_ref.at[me], send_sems.at[k - 1], recv_sems.at[k - 1], peer)
            cp.start()
            sends.append(cp)
        for k in range(1, N_DEV):
            src = 4 * _flip(x, k & 4) + 2 * _flip(y, k & 2) + _flip(c, k & 1)
            _remote(in_ref, all_ref.at[src], send_sems.at[k - 1], recv_sems.at[k - 1], (x, y, c)).wait_recv()
        acc = all_ref[0]
        for e in range(1, N_DEV):
            acc = acc + all_ref[e]
        sum_ref[...] = acc
        for cp in sends:
            cp.wait_send()

    return pl.pallas_call(
        body, name="gather_sum_all", in_specs=[_VMEM_SPEC], out_specs=[_VMEM_SPEC] * 2,
        out_shape=[_sds((N_DEV, r, w)), _sds((r, w))],
        scratch_shapes=[pltpu.SemaphoreType.DMA((N_DEV - 1,)), pltpu.SemaphoreType.DMA((N_DEV - 1,))],
        compiler_params=pltpu.CompilerParams(vmem_limit_bytes=VMEM_LIMIT_V7X))(small)


def _add_chips(buf, t, chip_idx):
    r, cc = buf.shape[1:]
    tr = min(256, r)

    def body(c_ref, p_ref, t_ref, o_ref):
        del c_ref
        o_ref[...] = p_ref[0] + t_ref[0].astype(F32) + t_ref[1].astype(F32) + t_ref[2].astype(F32)

    return pl.pallas_call(
        body, name="add_chips", out_shape=_sds((r, cc)),
        grid_spec=pltpu.PrefetchScalarGridSpec(
            num_scalar_prefetch=1, grid=(r // tr,),
            in_specs=[pl.BlockSpec((1, tr, cc), lambda i, c: (c[0], i, 0)),
                      pl.BlockSpec((N_CHIPS - 1, tr, cc), lambda i, c: (0, i, 0))],
            out_specs=pl.BlockSpec((tr, cc), lambda i, c: (i, 0))),
        compiler_params=_cp(1))(chip_idx, buf, t)


def _rs_sibling(qs):
    n = len(qs)

    def body(*refs):
        ins, outs = refs[:n], refs[n:2 * n]
        send_sems, recv_sems = refs[2 * n:]
        x, y, c = _place()
        copies = [_remote(ins[a], outs[a], send_sems.at[a], recv_sems.at[a], (x, y, 1 - c)) for a in range(n)]
        for cp in copies:
            cp.start()
        for cp in copies:
            cp.wait()

    return pl.pallas_call(
        body, name="rs_sibling", in_specs=[_HBM_SPEC] * n, out_specs=[_HBM_SPEC] * n,
        out_shape=[_sds(q.shape) for q in qs],
        scratch_shapes=[pltpu.SemaphoreType.DMA((n,)), pltpu.SemaphoreType.DMA((n,))])(*qs)


def _adamw_update(w, g, m, v):
    m = ADAM_B1 * m + (1.0 - ADAM_B1) * g
    v = ADAM_B2 * v + (1.0 - ADAM_B2) * jnp.square(g)
    m_hat = m / (1.0 - ADAM_B1 ** ADAM_STEP)
    v_hat = v / (1.0 - ADAM_B2 ** ADAM_STEP)
    return -ADAM_LR * (m_hat / (jnp.sqrt(v_hat) + ADAM_EPS) + ADAM_WD * w), m, v


def _adamw(w, g_parts, m, v, name):
    shape = w.shape
    cols = shape[-1]
    rows = _size(shape[:-1])
    tr = 512 if rows % 512 == 0 else rows
    spec = pl.BlockSpec((tr, cols), lambda i: (i, 0))
    n = len(g_parts)

    def body(*refs):
        w_ref, m_ref, v_ref = refs[:3]
        g_ref, d_ref, nm_ref, nv_ref = refs[3 + n:]
        g = refs[3][...]
        for r in refs[4:3 + n]:
            g = g + r[...]
        g_ref[...] = g
        d_ref[...], nm_ref[...], nv_ref[...] = _adamw_update(w_ref[...], g, m_ref[...], v_ref[...])

    outs = pl.pallas_call(
        body, name="adamw_" + name, grid=(rows // tr,), in_specs=[spec] * (3 + n), out_specs=[spec] * 4,
        out_shape=[_sds((rows, cols))] * 4, compiler_params=_cp(1))(
            *[a.reshape(rows, cols) for a in (w, m, v, *g_parts)])
    return tuple(o.reshape(shape) for o in outs)


def _adamw_layer(w, g_parts, m, v, layer, prev, name):
    _, r, cc = w.shape
    tr = 512 if r % 512 == 0 else r
    spec = pl.BlockSpec((1, tr, cc), lambda i: (layer, i, 0))
    n = len(g_parts)

    def body(*refs):
        w_ref, m_ref, v_ref = refs[:3]
        g_ref, d_ref, nm_ref, nv_ref = refs[-4:]
        g = refs[3][...]
        for q in refs[4:3 + n]:
            g = g + q[...]
        g = g[:, :cc]
        g_ref[0] = g
        d_ref[0], nm_ref[0], nv_ref[0] = _adamw_update(w_ref[0], g, m_ref[0], v_ref[0])

    g_specs = [pl.BlockSpec((tr, q.shape[1]), lambda i: (i, 0)) for q in g_parts]
    passed = () if prev is None else tuple(prev)
    return pl.pallas_call(
        body, name="adamw_" + name, grid=(r // tr,),
        in_specs=[spec] * 3 + g_specs + [_HBM_SPEC] * len(passed), out_specs=[spec] * 4,
        out_shape=[_sds(w.shape)] * 4, input_output_aliases={3 + n + k: k for k in range(len(passed))},
        compiler_params=_cp(1))(w, m, v, *g_parts, *passed)


def _size(shape):
    n = 1
    for s in shape:
        n *= s
    return n


_SMALL = (("dmod", (DEPTH, 3 * D_MODEL)), ("pre_norm_g", (DEPTH, D_MODEL)), ("post_norm_g", (DEPTH, D_MODEL)),
          ("even_sc_conv_w", (2, SC_KERNEL, SC_WIDTH)), ("even_sc_conv_b", (2, SC_WIDTH)),
          ("even_q_norm_g", (2, Q_LORA)), ("even_kv_norm_g", (2, KV_LORA)),
          ("odd_conv_w", (2, CONF_KERNEL, D_MODEL)), ("odd_conv_b", (2, D_MODEL)), ("odd_ln_g", (2, D_MODEL)),
          ("odd_ln_b", (2, D_MODEL)))
SMALL_ROWS = -(-sum(_size(s) for _, s in _SMALL) // (8 * 128)) * 8

_SMALL_W = (("even_sc_conv_w", (2, SC_KERNEL, SC_WIDTH // N_CHIPS)), ("odd_conv_w", (2, CONF_KERNEL, D_MODEL // N_CHIPS)),
            ("odd_conv_b", (2, D_MODEL // N_CHIPS)), ("odd_ln_g", (2, D_MODEL // N_CHIPS)),
            ("odd_ln_b", (2, D_MODEL // N_CHIPS)))
SMALL_W_ROWS = -(-sum(_size(s) for _, s in _SMALL_W) // (8 * 128)) * 8


def _pack_rows(arrays, layout, rows):
    flat = jnp.concatenate([arrays[n].reshape(-1) for n, _ in layout])
    return jnp.pad(flat, (0, rows * 128 - flat.shape[0])).reshape(rows, 128)


def _unpack_small(t):
    flat = t.reshape(-1)
    out, at = {}, 0
    for n, shape in _SMALL:
        out[n] = flat[at:at + _size(shape)].reshape(shape)
        at += _size(shape)
    return out


def _unpack_small_w(t):
    flat = t.reshape(N_CHIPS, -1)
    out, at = {}, 0
    for n, shape in _SMALL_W:
        a = flat[:, at:at + _size(shape)].reshape((N_CHIPS,) + shape)
        out[n] = jnp.moveaxis(a, 0, -2).reshape(shape[:-1] + (N_CHIPS * shape[-1],))
        at += _size(shape)
    return out


def _chip_cols(a, chip):
    n = a.shape[-1] // N_CHIPS
    return lax.dynamic_slice_in_dim(a, chip * n, n, axis=a.ndim - 1)


def _join_cols(a):
    _, l, r, cc = a.shape
    return a.transpose(1, 2, 0, 3).reshape(l, r, N_CHIPS * cc)


WEIGHT_NAMES = ("ada_w", "ada_b", "pre_norm_g", "post_norm_g", "even_w_in", "even_sc_conv_w", "even_sc_conv_b",
                "even_q_norm_g", "even_kv_norm_g", "even_w_uq", "even_w_ukv", "even_w_out", "odd_w_in", "odd_conv_w",
                "odd_conv_b", "odd_ln_g", "odd_ln_b", "odd_w_out")
GATHER_HOW = ((("even_w_in", "slot"), ("even_w_uq", "slot"), ("even_w_ukv", "slot"), ("even_w_out", "rows")),
              (("odd_w_in", "cols"), ("odd_w_out", "rows")))


def kernel(x, c, positions, ada_w, ada_b, pre_norm_g, post_norm_g, even_w_in, even_sc_conv_w, even_sc_conv_b, even_q_norm_g, even_kv_norm_g, even_w_uq, even_w_ukv, even_w_out, odd_w_in, odd_conv_w, odd_conv_b, odd_ln_g, odd_ln_b, odd_w_out, loss_target, m_ada_w, m_ada_b, m_pre_norm_g, m_post_norm_g, m_even_w_in, m_even_sc_conv_w, m_even_sc_conv_b, m_even_q_norm_g, m_even_kv_norm_g, m_even_w_uq, m_even_w_ukv, m_even_w_out, m_odd_w_in, m_odd_conv_w, m_odd_conv_b, m_odd_ln_g, m_odd_ln_b, m_odd_w_out, v_ada_w, v_ada_b, v_pre_norm_g, v_post_norm_g, v_even_w_in, v_even_sc_conv_w, v_even_sc_conv_b, v_even_q_norm_g, v_even_kv_norm_g, v_even_w_uq, v_even_w_ukv, v_even_w_out, v_odd_w_in, v_odd_conv_w, v_odd_conv_b, v_odd_ln_g, v_odd_ln_b, v_odd_w_out):
    w = dict(zip(WEIGHT_NAMES, (ada_w, ada_b, pre_norm_g, post_norm_g, even_w_in, even_sc_conv_w, even_sc_conv_b,
                                even_q_norm_g, even_kv_norm_g, even_w_uq, even_w_ukv, even_w_out, odd_w_in, odd_conv_w,
                                odd_conv_b, odd_ln_g, odd_ln_b, odd_w_out)))
    m = dict(zip(WEIGHT_NAMES, (m_ada_w, m_ada_b, m_pre_norm_g, m_post_norm_g, m_even_w_in, m_even_sc_conv_w,
                                m_even_sc_conv_b, m_even_q_norm_g, m_even_kv_norm_g, m_even_w_uq, m_even_w_ukv,
                                m_even_w_out, m_odd_w_in, m_odd_conv_w, m_odd_conv_b, m_odd_ln_g, m_odd_ln_b, m_odd_w_out)))
    v = dict(zip(WEIGHT_NAMES, (v_ada_w, v_ada_b, v_pre_norm_g, v_post_norm_g, v_even_w_in, v_even_sc_conv_w,
                                v_even_sc_conv_b, v_even_q_norm_g, v_even_kv_norm_g, v_even_w_uq, v_even_w_ukv,
                                v_even_w_out, v_odd_w_in, v_odd_conv_w, v_odd_conv_b, v_odd_ln_g, v_odd_ln_b, v_odd_w_out)))
    ix, iy, ic = _place()
    chip = 2 * ix + iy
    me = 2 * chip + ic
    s = x.shape[1]

    c_all, mod_all = _ada_fwd(jnp.broadcast_to(c, (8, D_MODEL)), ada_w, _chip_cols(ada_b, chip))
    mod = lax.dynamic_index_in_dim(mod_all, 8 * me, axis=2, keepdims=False)
    mod = mod.transpose(1, 0, 2).reshape(DEPTH, 3 * D_MODEL)

    items = [[(w[n][layer // 2].astype(MXU_DTYPE), how) for n, how in GATHER_HOW[layer % 2]] for layer in range(DEPTH)]
    later_items = [item for layer_items in items[1:] for item in layer_items]
    first = _gather_chips(items[0] + [(_pack_rows(w, _SMALL_W, SMALL_W_ROWS), "slot")], [mod_all])
    small_w = _unpack_small_w(first[len(items[0])])
    weights_sent = _gather_start(later_items, [_place_own(a, how, chip.reshape(1)) for a, how in later_items],
                                 "gather_start", [first[0]])
    later = []

    def layer_weights(layer, x_in):
        i = layer // 2
        if layer == 0:
            arrays = first[:len(items[0])]
        else:
            if not later:
                later.extend(_gather_wait(later_items, weights_sent, [x_in], "gather_wait"))
            at = sum(len(layer_items) for layer_items in items[1:layer])
            arrays = later[at:at + len(items[layer])]
        if layer % 2 == 0:
            ein, uq, ukv, eout = arrays
            wuk, wuv = _ukv_to_heads(ukv)
            wq, wq_rot = _uq_to_heads(uq)
            return {"w_in": _ein_from_shards(ein), "wq": wq, "wq_rot": wq_rot, "wuk": wuk, "wuv": wuv, "w_out": eout,
                    "sc_conv_w": small_w["even_sc_conv_w"][i]}
        oin, oout = arrays
        return {"w_in": oin, "w_out": oout, "conv_w": small_w["odd_conv_w"][i], "conv_b": small_w["odd_conv_b"][i:i + 1],
                "ln_g": small_w["odd_ln_g"][i:i + 1], "ln_b": small_w["odd_ln_b"][i:i + 1]}

    in_flight, own, sib, last = {}, {}, {}, {}

    def land(layer, after):
        names, started, kept = in_flight.pop(layer)
        bufs, arrived = _rs_wait(started, after, "rs_wait_%d" % layer)
        sums = [_add_chips(b, t, chip.reshape(1)) for b, t in zip(bufs if kept is None else kept, arrived)]
        for n, mine, theirs in zip(names, sums, _rs_sibling(sums)):
            own[n, layer // 2], sib[n, layer // 2] = mine, theirs

    def grads_done(layer, bufs, dx_in):
        if layer + 1 in in_flight:
            land(layer + 1, [dx_in])
        if layer == 0:
            last.update(bufs)
            return None
        names = sorted(bufs)
        in_flight[layer] = (names, _rs_start([bufs[n] for n in names], "rs_start_%d" % layer), None)
        return in_flight[layer][1][-1]

    p = {"pre_norm_g": pre_norm_g, "post_norm_g": post_norm_g, "even_sc_conv_b": even_sc_conv_b,
         "even_q_norm_g": even_q_norm_g, "even_kv_norm_g": even_kv_norm_g}
    inv_freq = 1.0 / (ROPE_THETA ** (jnp.arange(0, QK_ROPE, 2, dtype=F32) / QK_ROPE))
    inv_freq = jnp.zeros((1, HEAD_PAD), F32).at[0, QK_NOPE:QK_NOPE + QK_ROPE].set(jnp.tile(inv_freq, 2))
    cos, sin = _rope_tables(positions.reshape(s, 1), inv_freq)

    loss, dx, g = _local_step(x[0], loss_target[0], cos, sin, mod, p, layer_weights, weights_sent[-1], grads_done)

    grads, deltas, new_m, new_v = {}, {}, {}, {}

    def update_layers(n, results, pairs):
        for i in pairs:
            results = _adamw_layer(w[n], [own[n, i], sib[n, i]], m[n], v[n], i, results, n)
        return results

    small_all, small_sum = _gather_sum_all(_pack_rows(g, _SMALL, SMALL_ROWS))
    names = sorted(last)
    kept = [last[n] for n in names]
    in_flight[0] = (names, _rs_start([b.astype(jnp.bfloat16) for b in kept], "rs_start_0", [small_sum]), kept)
    tot = _unpack_small(small_sum)
    dmod_all = small_all[:, :DEPTH * 3 * D_MODEL // 128].reshape(N_DEV, DEPTH, 3 * D_MODEL)
    grads["ada_w"] = _ada_bwd(c_all[:, 0, :].T, _chip_cols(dmod_all, chip).transpose(1, 0, 2))
    grads["ada_b"] = tot["dmod"]
    for n in ("pre_norm_g", "post_norm_g", "even_sc_conv_b", "even_q_norm_g", "even_kv_norm_g"):
        grads[n] = tot[n]
    for n in ("even_sc_conv_w", "odd_conv_w", "odd_conv_b", "odd_ln_g", "odd_ln_b"):
        grads[n] = _chip_cols(tot[n], chip)
    for n in list(grads):
        _, deltas[n], new_m[n], new_v[n] = _adamw(w[n], [grads[n]], m[n], v[n], n)

    for n in ("odd_w_in", "odd_w_out"):
        grads[n], deltas[n], new_m[n], new_v[n] = update_layers(n, None, (1, 0))
    partly = {n: update_layers(n, None, (1,)) for n in ("even_w_in", "even_w_out")}
    land(0, [deltas["ada_w"], deltas["odd_w_in"], partly["even_w_in"][1]])
    for n in ("even_w_in", "even_w_out"):
        grads[n], deltas[n], new_m[n], new_v[n] = update_layers(n, partly[n], (0,))
    uq_parts, ukv_parts = zip(*[[jnp.stack(part) for part in zip(*[_mla_local(q["even_mla", i]) for i in range(N_PAIRS)])]
                                for q in (own, sib)])
    for n, parts in (("even_w_uq", uq_parts), ("even_w_ukv", ukv_parts)):
        grads[n], deltas[n], new_m[n], new_v[n] = _adamw(w[n], list(parts), m[n], v[n], n)

    total_loss = lax.psum(loss[0, 0], ("x", "y", "c"))
    return (total_loss, dx[None], *[grads[n] for n in WEIGHT_NAMES], *[deltas[n] for n in WEIGHT_NAMES],
            *[new_m[n] for n in WEIGHT_NAMES], *[new_v[n] for n in WEIGHT_NAMES])
```

```python
import functools

import jax
import jax.numpy as jnp
from jax import lax
from jax.experimental import pallas as pl
from jax.experimental.pallas import tpu as pltpu

F32 = jnp.float32
MXU_DTYPE = jnp.bfloat16
MESH = pl.DeviceIdType.MESH
VMEM_LIMIT_V7X = 56 * 2 ** 20

EPS = 1e-6
D_MODEL = 1024
DEPTH = 4
CHUNK = 64
SC_WIDTH = 512
SC_KERNEL = 3
SC_HALO = 8
HEADS = 8
QK_NOPE = 64
QK_ROPE = 32
V_HEAD = 64
HEAD_PAD = 128
Q_LORA = 256
KV_LORA = 128
ROPE_THETA = 10000.0
CONF_KERNEL = 31
CONF_HALO = 32
CONV_ROWS = 32
SUBLANES = 8
EVEN_IN = 2976
EVEN_PAD = 3072
ODD_IN = 3072
N_CHIPS = 4
N_DEV = 8
NEG = -1e30

ADAM_LR = 0.001
ADAM_B1 = 0.9
ADAM_B2 = 0.999
ADAM_EPS = 1e-08
ADAM_WD = 0.01
ADAM_STEP = 10

N_PAIRS = DEPTH // 2
EVEN_SHARD = EVEN_IN // N_CHIPS
EVEN_SHARD_PAD = 768
MLA_ROWS = Q_LORA + 2 * KV_LORA


def _cp(n_grid=0, **kw):
    return pltpu.CompilerParams(dimension_semantics=("arbitrary",) * n_grid,
                                vmem_limit_bytes=VMEM_LIMIT_V7X, **kw)


def _sigmoid(x):
    return 1.0 / (1.0 + jnp.exp(-x))


def _silu(x):
    return x * _sigmoid(x)


def _dsilu(x):
    s = _sigmoid(x)
    return s * (1.0 + x * (1.0 - s))


def _rms(x, g):
    return x * lax.rsqrt(jnp.mean(x * x, axis=-1, keepdims=True) + EPS) * g


def _dot(a, b, dims):
    return lax.dot_general(a.astype(MXU_DTYPE), b.astype(MXU_DTYPE), (dims, ((), ())),
                           preferred_element_type=F32)


def _dot_nn(a, b):
    return _dot(a, b, ((1,), (0,)))


def _dot_nt(a, b):
    return _dot(a, b, ((1,), (1,)))


def _dot_tn(a, b):
    return _dot(a, b, ((0,), (0,)))


def _rows(ts, w, cb=0):
    return pl.BlockSpec((ts, w), lambda i: (i, cb))


def _vec(w, cb=0, r=1):
    return pl.BlockSpec((r, w), lambda i: (0, cb))


def _prev_halo(ts, hr, w, cb):
    return pl.BlockSpec((hr, w), lambda i: (jnp.maximum(i * (ts // hr) - 1, 0), cb))


def _next_halo(ts, hr, w, cb, s):
    return pl.BlockSpec((hr, w), lambda i: (jnp.minimum((i + 1) * (ts // hr), s // hr - 1), cb))


def _sds(shape, dtype=F32):
    return jax.ShapeDtypeStruct(shape, dtype)


def _mm(a, b, mode, out_dtype, tm, tn, name):
    tm = min(tm, a.shape[1] if mode == "tn" else a.shape[0])
    tn = min(tn, b.shape[0] if mode == "nt" else b.shape[1])
    if mode == "nn":
        (m, k), n = a.shape, b.shape[1]
        a_spec = pl.BlockSpec((tm, k), lambda i, j: (i, 0))
        b_spec = pl.BlockSpec((k, tn), lambda i, j: (0, j))
        dot = _dot_nn
    elif mode == "nt":
        (m, k), n = a.shape, b.shape[0]
        a_spec = pl.BlockSpec((tm, k), lambda i, j: (i, 0))
        b_spec = pl.BlockSpec((tn, k), lambda i, j: (j, 0))
        dot = _dot_nt
    else:
        (k, m), n = a.shape, b.shape[1]
        a_spec = pl.BlockSpec((k, tm), lambda i, j: (0, i))
        b_spec = pl.BlockSpec((k, tn), lambda i, j: (0, j))
        dot = _dot_tn
    assert m % tm == 0 and n % tn == 0, (name, m, n, tm, tn)

    def body(a_ref, b_ref, o_ref):
        o_ref[...] = dot(a_ref[...], b_ref[...]).astype(o_ref.dtype)

    return pl.pallas_call(
        body, name=name, grid=(m // tm, n // tn), in_specs=[a_spec, b_spec],
        out_specs=pl.BlockSpec((tm, tn), lambda i, j: (i, j)), out_shape=_sds((m, n), out_dtype),
        compiler_params=_cp(2))(a, b)


def _mm_tn_shards(a, b, by, name):
    k, m = a.shape
    n = b.shape[1]
    if by == "cols":
        tm, tn = m, n // N_CHIPS
        shape, grid = (N_CHIPS, m, tn), (1, N_CHIPS)
        out_spec = pl.BlockSpec((1, tm, tn), lambda i, j: (j, i, 0))
    else:
        tm, tn = m // N_CHIPS, n
        shape, grid = (N_CHIPS, tm, n), (N_CHIPS, 1)
        out_spec = pl.BlockSpec((1, tm, tn), lambda i, j: (i, 0, j))

    def body(a_ref, b_ref, o_ref):
        o_ref[0] = _dot_tn(a_ref[...], b_ref[...])

    return pl.pallas_call(
        body, name=name, grid=grid,
        in_specs=[pl.BlockSpec((k, tm), lambda i, j: (0, i)), pl.BlockSpec((k, tn), lambda i, j: (0, j))],
        out_specs=out_spec, out_shape=_sds(shape), compiler_params=_cp(2))(a, b)


def _even_col(q):
    return q if q < 2432 else (q + 64 if q < 2464 else q + 96)


def _shard_pieces(j):
    lo, hi = EVEN_SHARD * j, EVEN_SHARD * (j + 1)
    cuts = [lo] + [b for b in (2432, 2464) if lo < b < hi] + [hi]
    return [(a - lo, _even_col(a), b - a) for a, b in zip(cuts[:-1], cuts[1:])]


def _ein_from_shards(w):
    _, d, _ = w.shape
    tr = 256

    def body(w_ref, o_ref):
        parts, at = [], 0
        for j in range(N_CHIPS):
            for d0, s0, n in _shard_pieces(j):
                if s0 > at:
                    parts.append(jnp.zeros((tr, s0 - at), F32))
                parts.append(w_ref[j, :, d0:d0 + n].astype(F32))
                at = s0 + n
        o_ref[...] = jnp.concatenate(parts, axis=1).astype(o_ref.dtype)

    return pl.pallas_call(
        body, name="ein_from_shards", grid=(d // tr,),
        in_specs=[pl.BlockSpec((N_CHIPS, tr, EVEN_SHARD), lambda i: (0, i, 0))],
        out_specs=_rows(tr, EVEN_PAD), out_shape=_sds((d, EVEN_PAD), w.dtype), compiler_params=_cp(1))(w)


def _ein_to_shards(dw):
    d = dw.shape[0]
    tr = 256

    def body(dw_ref, o_ref):
        for j in range(N_CHIPS):
            parts = [dw_ref[:, s0:s0 + n] for _, s0, n in _shard_pieces(j)]
            o_ref[j] = jnp.concatenate(parts + [jnp.zeros((tr, EVEN_SHARD_PAD - EVEN_SHARD), F32)], axis=1)

    return pl.pallas_call(
        body, name="ein_to_shards", grid=(d // tr,), in_specs=[_rows(tr, EVEN_PAD)],
        out_specs=pl.BlockSpec((N_CHIPS, tr, EVEN_SHARD_PAD), lambda i: (0, i, 0)),
        out_shape=_sds((N_CHIPS, d, EVEN_SHARD_PAD)), compiler_params=_cp(1))(dw)


def _rope_tables(pos_col, invf):
    s = pos_col.shape[0]
    ts = min(512, s)

    def body(p_ref, f_ref, c_ref, s_ref):
        ang = p_ref[...].astype(F32) * f_ref[...]
        lane = lax.broadcasted_iota(jnp.int32, ang.shape, 1)
        rope = (lane >= QK_NOPE) & (lane < QK_NOPE + QK_ROPE)
        c_ref[...] = jnp.where(lane < QK_NOPE, 1.0, jnp.where(rope, jnp.cos(ang), 0.0))
        s_ref[...] = jnp.where(rope, jnp.sin(ang), 0.0)

    return pl.pallas_call(
        body, name="rope_tables", grid=(s // ts,), in_specs=[_rows(ts, 1), _vec(HEAD_PAD)],
        out_specs=[_rows(ts, HEAD_PAD)] * 2, out_shape=[_sds((s, HEAD_PAD))] * 2,
        compiler_params=_cp(1))(pos_col, invf)


def _after(dep):
    return () if dep is None else (dep,)


def _pre_fwd(x, g, mod_l, ts, dep=None):
    s, d = x.shape

    def body(x_ref, g_ref, sh_ref, sc_ref, *rest):
        h = _rms(x_ref[...], g_ref[...]) * (1.0 + sc_ref[...]) + sh_ref[...]
        rest[-1][...] = h.astype(rest[-1].dtype)

    return pl.pallas_call(
        body, name="pre_fwd", grid=(s // ts,),
        in_specs=[_rows(ts, d), _vec(d), _vec(d, 0), _vec(d, 1)] + [_HBM_SPEC] * len(_after(dep)),
        out_specs=_rows(ts, d), out_shape=_sds((s, d), MXU_DTYPE), compiler_params=_cp(1))(
            x, g, mod_l, mod_l, *_after(dep))


def _pre_bwd(dz, w_in, dx_out, x, g, mod_l, ts):
    s, d = x.shape
    n_in = dz.shape[1]

    def f(xv, gv, sh, sc):
        return _rms(xv, gv) * (1.0 + sc) + sh

    def body(dz_ref, w_ref, dxo_ref, x_ref, g_ref, sh_ref, sc_ref, dx_ref, dsh_ref, dsc_ref, dg_ref):
        i = pl.program_id(0)
        _, vjp = jax.vjp(f, x_ref[...], g_ref[...], sh_ref[...], sc_ref[...])
        dx, dg, dsh, dsc = vjp(_dot_nt(dz_ref[...], w_ref[...]))
        dx_ref[...] = dxo_ref[...] + dx

        @pl.when(i == 0)
        def _():
            dsh_ref[...] = jnp.zeros_like(dsh_ref)
            dsc_ref[...] = jnp.zeros_like(dsc_ref)
            dg_ref[...] = jnp.zeros_like(dg_ref)

        dsh_ref[...] += dsh
        dsc_ref[...] += dsc
        dg_ref[...] += dg

    return pl.pallas_call(
        body, name="pre_bwd", grid=(s // ts,),
        in_specs=[_rows(ts, n_in), _vec(n_in, 0, d), _rows(ts, d), _rows(ts, d), _vec(d), _vec(d, 0), _vec(d, 1)],
        out_specs=[_rows(ts, d), _vec(d), _vec(d), _vec(d)],
        out_shape=[_sds((s, d)), _sds((1, d)), _sds((1, d)), _sds((1, d))],
        compiler_params=_cp(1))(dz, w_in, dx_out, x, g, mod_l, mod_l)


def _post_fwd(x, yo, g, mod_l, ts):
    s, d = x.shape

    def body(x_ref, yo_ref, g_ref, gate_ref, o_ref):
        o_ref[...] = x_ref[...] + gate_ref[...] * _rms(yo_ref[...], g_ref[...])

    return pl.pallas_call(
        body, name="post_fwd", grid=(s // ts,),
        in_specs=[_rows(ts, d), _rows(ts, d), _vec(d), _vec(d, 2)],
        out_specs=_rows(ts, d), out_shape=_sds((s, d)), compiler_params=_cp(1))(x, yo, g, mod_l)


def _post_bwd(dx_out, yo, g, mod_l, ts, dep=None):
    s, d = yo.shape

    def f(yov, gv, gate):
        return gate * _rms(yov, gv)

    def body(dx_ref, yo_ref, g_ref, gate_ref, *rest):
        dyo_ref, dgate_ref, dg_ref = rest[-3:]
        i = pl.program_id(0)
        _, vjp = jax.vjp(f, yo_ref[...], g_ref[...], gate_ref[...])
        dyo, dg, dgate = vjp(dx_ref[...])
        dyo_ref[...] = dyo.astype(dyo_ref.dtype)

        @pl.when(i == 0)
        def _():
            dgate_ref[...] = jnp.zeros_like(dgate_ref)
            dg_ref[...] = jnp.zeros_like(dg_ref)

        dgate_ref[...] += dgate
        dg_ref[...] += dg

    return pl.pallas_call(
        body, name="post_bwd", grid=(s // ts,),
        in_specs=[_rows(ts, d), _rows(ts, d), _vec(d), _vec(d, 2)] + [_HBM_SPEC] * len(_after(dep)),
        out_specs=[_rows(ts, d), _vec(d), _vec(d)],
        out_shape=[_sds((s, d), MXU_DTYPE), _sds((1, d)), _sds((1, d))],
        compiler_params=_cp(1))(dx_out, yo, g, mod_l, *_after(dep))


def _loss_fwd_bwd(x, target, ts):
    s, d = x.shape

    def body(x_ref, t_ref, loss_ref, dx_ref):
        i = pl.program_id(0)
        err = x_ref[...] - t_ref[...]
        dx_ref[...] = err * (1.0 / d)

        @pl.when(i == 0)
        def _():
            loss_ref[...] = jnp.zeros_like(loss_ref)

        loss_ref[...] += 0.5 * jnp.sum(jnp.sum(err * err, axis=-1, keepdims=True) * (1.0 / d), axis=0, keepdims=True)

    return pl.pallas_call(
        body, name="loss", grid=(s // ts,), in_specs=[_rows(ts, d), _rows(ts, d)],
        out_specs=[_vec(1), _rows(ts, d)], out_shape=[_sds((1, 1)), _sds((s, d))],
        compiler_params=_cp(1))(x, target)


def _rope(t, cos, sin):
    lane = lax.broadcasted_iota(jnp.int32, t.shape, 1)
    first = (lane >= QK_NOPE) & (lane < QK_NOPE + QK_ROPE // 2)
    second = (lane >= QK_NOPE + QK_ROPE // 2) & (lane < QK_NOPE + QK_ROPE)
    up = pltpu.roll(t, QK_ROPE // 2, 1)
    down = pltpu.roll(t, HEAD_PAD - QK_ROPE // 2, 1)
    return t * cos + jnp.where(first, -down, jnp.where(second, up, 0.0)) * sin


def _rope_transposed(g, cos, sin):
    lane = lax.broadcasted_iota(jnp.int32, g.shape, 1)
    first = (lane >= QK_NOPE) & (lane < QK_NOPE + QK_ROPE // 2)
    second = (lane >= QK_NOPE + QK_ROPE // 2) & (lane < QK_NOPE + QK_ROPE)
    u = g * sin
    up = pltpu.roll(u, QK_ROPE // 2, 1)
    down = pltpu.roll(u, HEAD_PAD - QK_ROPE // 2, 1)
    return g * cos + jnp.where(first, down, jnp.where(second, -up, 0.0))


def _mla_prep_fwd(z, cos, sin, qg, kvg, wq, wq_rot, wuk, wuv, ts):
    s = z.shape[0]
    wide = HEADS * HEAD_PAD

    def body(cq_ref, ckv_ref, kr_ref, cos_ref, sin_ref, qg_ref, kvg_ref, wq_ref, wqr_ref, wuk_ref, wuv_ref,
             q_ref, k_ref, v_ref):
        cos_v, sin_v = cos_ref[...], sin_ref[...]
        cqn = _rms(cq_ref[...], qg_ref[...])
        ckvn = _rms(ckv_ref[...], kvg_ref[...])
        kr = _rope(kr_ref[...], cos_v, sin_v)
        q_lin, q_rot = _dot_nn(cqn, wq_ref[...]), _dot_nn(cqn, wqr_ref[...])
        k_lin, v_all = _dot_nn(ckvn, wuk_ref[...]), _dot_nn(ckvn, wuv_ref[...])
        for h in range(HEADS):
            lanes = slice(h * HEAD_PAD, (h + 1) * HEAD_PAD)
            q_ref[h] = (q_lin[:, lanes] * cos_v + q_rot[:, lanes] * sin_v).astype(q_ref.dtype)
            k_ref[h] = (k_lin[:, lanes] + kr).astype(k_ref.dtype)
            v_ref[h] = v_all[:, lanes].astype(v_ref.dtype)

    out = pl.BlockSpec((HEADS, ts, HEAD_PAD), lambda i: (0, i, 0))
    return pl.pallas_call(
        body, name="mla_prep_fwd", grid=(s // ts,),
        in_specs=[_rows(ts, Q_LORA, 8), _rows(ts, KV_LORA, 18), _rows(ts, HEAD_PAD, 19), _rows(ts, HEAD_PAD), _rows(ts, HEAD_PAD),
                  _vec(Q_LORA), _vec(KV_LORA), _vec(wide, 0, Q_LORA), _vec(wide, 0, Q_LORA), _vec(wide, 0, KV_LORA),
                  _vec(wide, 0, KV_LORA)],
        out_specs=[out] * 3, out_shape=[_sds((HEADS, s, HEAD_PAD), MXU_DTYPE)] * 3,
        compiler_params=_cp(1))(z, z, z, cos, sin, qg, kvg, wq, wq_rot, wuk, wuv)


def _mla_prep_bwd(dz, dq, dk, dv, z, cos, sin, qg, kvg, wq, wuk, wuv, ts):
    s = z.shape[0]

    def fq(cq, g):
        return _rms(cq, g)

    def body(dz_in_ref, dq_ref, dk_ref, dv_ref, cq_ref, ckv_ref, cos_ref, sin_ref, qg_ref, kvg_ref, wq_ref, wuk_ref,
             wuv_ref, dz_ref, dw_ref, dqg_ref, dkvg_ref):
        del dz_in_ref
        cos_v, sin_v = cos_ref[...], sin_ref[...]

        @pl.when(pl.program_id(0) == 0)
        def _():
            dw_ref[...] = jnp.zeros_like(dw_ref)
            dqg_ref[...] = jnp.zeros_like(dqg_ref)
            dkvg_ref[...] = jnp.zeros_like(dkvg_ref)

        cqn, vjp_q = jax.vjp(fq, cq_ref[...], qg_ref[...])
        ckvn, vjp_kv = jax.vjp(fq, ckv_ref[...], kvg_ref[...])
        lane = lax.broadcasted_iota(jnp.int32, (ts, HEAD_PAD), 1)
        rope_lanes = (lane >= QK_NOPE) & (lane < QK_NOPE + QK_ROPE)
        dq_lin = jnp.concatenate([_rope_transposed(dq_ref[h], cos_v, sin_v).astype(MXU_DTYPE) for h in range(HEADS)], axis=1)
        dk_all = jnp.concatenate([dk_ref[h].astype(MXU_DTYPE) for h in range(HEADS)], axis=1)
        dv_all = jnp.concatenate([dv_ref[h].astype(MXU_DTYPE) for h in range(HEADS)], axis=1)
        dkr = jnp.where(rope_lanes, dk_ref[0], 0.0)
        for h in range(1, HEADS):
            dkr = dkr + jnp.where(rope_lanes, dk_ref[h], 0.0)
        dcq, dqg = vjp_q(_dot_nt(dq_lin, wq_ref[...]))
        dckv, dkvg = vjp_kv(_dot_nt(dk_all, wuk_ref[...]) + _dot_nt(dv_all, wuv_ref[...]))
        dz_ref[:, 0:Q_LORA] = dcq.astype(dz_ref.dtype)
        dz_ref[:, Q_LORA:Q_LORA + KV_LORA] = dckv.astype(dz_ref.dtype)
        dz_ref[:, Q_LORA + KV_LORA:] = _rope_transposed(dkr, cos_v, sin_v).astype(dz_ref.dtype)
        dqg_ref[...] += dqg
        dkvg_ref[...] += dkvg
        dwq, dwuk, dwuv = _dot_tn(cqn, dq_lin), _dot_tn(ckvn, dk_all), _dot_tn(ckvn, dv_all)
        for h in range(HEADS):
            lanes = slice(h * HEAD_PAD, (h + 1) * HEAD_PAD)
            row0 = (h % 2) * MLA_ROWS
            dw_ref[h // 2, row0:row0 + Q_LORA, :] += dwq[:, lanes]
            dw_ref[h // 2, row0 + Q_LORA:row0 + Q_LORA + KV_LORA, :] += dwuk[:, lanes]
            dw_ref[h // 2, row0 + Q_LORA + KV_LORA:row0 + MLA_ROWS, :] += dwuv[:, lanes]

    wide = HEADS * HEAD_PAD
    heads = pl.BlockSpec((HEADS, ts, HEAD_PAD), lambda i: (0, i, 0))
    whole = pl.BlockSpec((N_CHIPS, 2 * MLA_ROWS, HEAD_PAD), lambda i: (0, 0, 0))
    return pl.pallas_call(
        body, name="mla_prep_bwd", grid=(s // ts,),
        in_specs=[_HBM_SPEC, heads, heads, heads, _rows(ts, Q_LORA, 8), _rows(ts, KV_LORA, 18),
                  _rows(ts, HEAD_PAD), _rows(ts, HEAD_PAD), _vec(Q_LORA), _vec(KV_LORA), _vec(wide, 0, Q_LORA),
                  _vec(wide, 0, KV_LORA), _vec(wide, 0, KV_LORA)],
        out_specs=[_rows(ts, 512, 4), whole, _vec(Q_LORA), _vec(KV_LORA)],
        out_shape=[_sds(dz.shape, dz.dtype), _sds((N_CHIPS, 2 * MLA_ROWS, HEAD_PAD)), _sds((1, Q_LORA)), _sds((1, KV_LORA))],
        input_output_aliases={0: 0}, compiler_params=_cp(1))(dz, dq, dk, dv, z, z, cos, sin, qg, kvg, wq, wuk, wuv)


def _chunk_mask(q0, k0, tq, tk):
    rows = q0 + lax.broadcasted_iota(jnp.int32, (tq, tk), 0)
    cols = k0 + lax.broadcasted_iota(jnp.int32, (tq, tk), 1)
    return lax.shift_right_logical(cols, 6) <= lax.shift_right_logical(rows, 6)


def _attn_fwd(q, k, v, tq):
    s = q.shape[1]
    nq = s // tq
    scale = 1.0 / float(QK_NOPE + QK_ROPE) ** 0.5

    def body(q_ref, k_ref, v_ref, o_ref, lse_ref):
        qi, hh = pl.program_id(1), pl.program_id(2)
        qv = q_ref[0]

        def step(kj, carry, masked):
            m, l, acc = carry
            k0 = pl.multiple_of(kj * tq, tq)
            sc = _dot_nt(qv, k_ref[0, pl.ds(k0, tq), :]) * scale
            if masked:
                sc = jnp.where(_chunk_mask(qi * tq, k0, tq, tq), sc, NEG)
            m_new = jnp.maximum(m, jnp.max(sc, axis=-1, keepdims=True))
            alpha = jnp.exp(m - m_new)
            p = jnp.exp(sc - m_new)
            l = alpha * l + jnp.sum(p, axis=-1, keepdims=True)
            acc = alpha * acc + _dot_nn(p, v_ref[0, pl.ds(k0, tq), :])
            return m_new, l, acc

        init = (jnp.full((tq, 1), NEG, F32), jnp.zeros((tq, 1), F32), jnp.zeros((tq, HEAD_PAD), F32))
        carry = lax.fori_loop(0, qi, lambda kj, c: step(kj, c, False), init)
        m, l, acc = step(qi, carry, True)
        o = acc / l
        lse_ref[0] = m + jnp.log(l)

        @pl.when(hh == 0)
        def _():
            o_ref[...] = o

        @pl.when(hh == 1)
        def _():
            o_ref[...] += o

    head = lambda hp, qi, hh: 2 * hp + hh
    return pl.pallas_call(
        body, name="attn_fwd", grid=(HEADS // 2, nq, 2),
        in_specs=[pl.BlockSpec((1, tq, HEAD_PAD), lambda hp, qi, hh: (head(hp, qi, hh), qi, 0)),
                  pl.BlockSpec((1, s, HEAD_PAD), lambda hp, qi, hh: (head(hp, qi, hh), 0, 0)),
                  pl.BlockSpec((1, s, HEAD_PAD), lambda hp, qi, hh: (head(hp, qi, hh), 0, 0))],
        out_specs=[pl.BlockSpec((tq, HEAD_PAD), lambda hp, qi, hh: (qi, hp)),
                   pl.BlockSpec((1, tq, 1), lambda hp, qi, hh: (head(hp, qi, hh), qi, 0))],
        out_shape=[_sds((s, HEADS * V_HEAD)), _sds((HEADS, s, 1))],
        compiler_params=_cp(3))(q, k, v)


def _attn_bwd(q, k, v, do, o, lse, tq):
    s = q.shape[1]
    nq = s // tq
    scale = 1.0 / float(QK_NOPE + QK_ROPE) ** 0.5

    def body(q_ref, k_ref, v_ref, do_ref, o_ref, lse_ref, dq_ref, dk_ref, dv_ref):
        hh, kj = pl.program_id(1), pl.program_id(2)

        @pl.when(kj == 0)
        def _():
            dq_ref[...] = jnp.zeros_like(dq_ref)

        kv, vv = k_ref[0], v_ref[0]
        lane = lax.broadcasted_iota(jnp.int32, (tq, HEAD_PAD), 1)
        mine = lax.shift_right_logical(lane, 6) == hh

        def step(qi, carry, masked):
            dk_acc, dv_acc = carry
            q0 = pl.multiple_of(qi * tq, tq)
            qv = q_ref[0, pl.ds(q0, tq), :]
            dov = do_ref[pl.ds(q0, tq), :]
            delta = jnp.sum(jnp.where(mine, dov * o_ref[pl.ds(q0, tq), :], 0.0), axis=-1, keepdims=True)
            sc = _dot_nt(qv, kv) * scale
            if masked:
                sc = jnp.where(_chunk_mask(q0, kj * tq, tq, tq), sc, NEG)
            p = jnp.exp(sc - lse_ref[0, pl.ds(q0, tq), :])
            do_b = dov.astype(MXU_DTYPE)
            ds = (p * (_dot_nt(do_b, vv) - delta) * scale).astype(MXU_DTYPE)
            dv_acc = dv_acc + _dot_tn(p, do_b)
            dk_acc = dk_acc + _dot_tn(ds, qv)
            dq_ref[0, pl.ds(q0, tq), :] += _dot_nn(ds, kv)
            return dk_acc, dv_acc

        zero = jnp.zeros((tq, HEAD_PAD), F32)
        carry = step(kj, (zero, zero), True)
        dk_acc, dv_acc = lax.fori_loop(kj + 1, nq, lambda qi, c: step(qi, c, False), carry)
        dk_ref[0] = dk_acc
        dv_ref[0] = dv_acc

    head = lambda hp, hh, kj: 2 * hp + hh
    full = pl.BlockSpec((1, s, HEAD_PAD), lambda hp, hh, kj: (head(hp, hh, kj), 0, 0))
    blk = pl.BlockSpec((1, tq, HEAD_PAD), lambda hp, hh, kj: (head(hp, hh, kj), kj, 0))
    pair = pl.BlockSpec((s, HEAD_PAD), lambda hp, hh, kj: (0, hp))
    return pl.pallas_call(
        body, name="attn_bwd", grid=(HEADS // 2, 2, nq),
        in_specs=[full, blk, blk, pair, pair, pl.BlockSpec((1, s, 1), lambda hp, hh, kj: (head(hp, hh, kj), 0, 0))],
        out_specs=[full, blk, blk], out_shape=[_sds((HEADS, s, HEAD_PAD))] * 3,
        compiler_params=_cp(3))(q, k, v, do, o, lse)


def _sc_conv(u, ubuf, w_ref, b_ref, ts):
    return (w_ref[2:3, :] * u + w_ref[1:2, :] * ubuf[pl.ds(SC_HALO - 1, ts), :]
            + w_ref[0:1, :] * ubuf[pl.ds(SC_HALO - 2, ts), :] + b_ref[...])


def _even_gate_fwd(z, o, sc_w, sc_b, ts):
    s = z.shape[0]
    w = SC_WIDTH

    def body(ab_ref, ac_ref, ax_ref, ag_ref, bg_ref, hc_ref, hx_ref, o_ref, w_ref, b_ref, y_ref, ubuf):
        i = pl.program_id(0)
        u = ac_ref[...] * ax_ref[...]
        ubuf[0:SC_HALO, :] = jnp.where(i > 0, hc_ref[...] * hx_ref[...], 0.0)
        ubuf[SC_HALO:, :] = u
        conv = _sc_conv(u, ubuf, w_ref, b_ref, ts)
        y_ref[:, 0:w] = (ab_ref[...] * conv * _silu(ag_ref[...])).astype(y_ref.dtype)
        y_ref[:, w:] = (o_ref[...] * _silu(bg_ref[...])).astype(y_ref.dtype)

    return pl.pallas_call(
        body, name="even_gate_fwd", grid=(s // ts,),
        in_specs=[_rows(ts, w, 0), _rows(ts, w, 1), _rows(ts, w, 2), _rows(ts, w, 3), _rows(ts, w, 5),
                  _prev_halo(ts, SC_HALO, w, 1), _prev_halo(ts, SC_HALO, w, 2), _rows(ts, w),
                  _vec(w, 0, SC_KERNEL), _vec(w)],
        out_specs=_rows(ts, 2 * w), out_shape=_sds((s, 2 * w), MXU_DTYPE),
        scratch_shapes=[pltpu.VMEM((ts + SC_HALO, w), F32)],
        compiler_params=_cp(1))(z, z, z, z, z, z, z, o, sc_w, sc_b)


def _even_gate_bwd(dy, z, o, sc_w, sc_b, ts):
    s = z.shape[0]
    w = SC_WIDTH
    n = s // ts

    def body(dya_ref, dyb_ref, dyan_ref, ab_ref, ac_ref, ax_ref, ag_ref, bg_ref, hc_ref, hx_ref, abn_ref, agn_ref,
             o_ref, w_ref, b_ref, dz_ref, do_ref, dw_ref, db_ref, ubuf, dbuf):
        i = pl.program_id(0)
        ab, ac, ax, ag, bg = ab_ref[...], ac_ref[...], ax_ref[...], ag_ref[...], bg_ref[...]
        dya, dyb = dya_ref[...], dyb_ref[...]
        u = ac * ax
        ubuf[0:SC_HALO, :] = jnp.where(i > 0, hc_ref[...] * hx_ref[...], 0.0)
        ubuf[SC_HALO:, :] = u
        conv = _sc_conv(u, ubuf, w_ref, b_ref, ts)
        sg = _silu(ag)
        dconv = dya * ab * sg
        dbuf[0:ts, :] = dconv
        dbuf[ts:, :] = jnp.where(i < n - 1, dyan_ref[...] * abn_ref[...] * _silu(agn_ref[...]), 0.0)
        du = w_ref[2:3, :] * dconv + w_ref[1:2, :] * dbuf[pl.ds(1, ts), :] + w_ref[0:1, :] * dbuf[pl.ds(2, ts), :]
        dz_ref[:, 0:w] = (dya * conv * sg).astype(dz_ref.dtype)
        dz_ref[:, w:2 * w] = (du * ax).astype(dz_ref.dtype)
        dz_ref[:, 2 * w:3 * w] = (du * ac).astype(dz_ref.dtype)
        dz_ref[:, 3 * w:4 * w] = (dya * ab * conv * _dsilu(ag)).astype(dz_ref.dtype)
        dz_ref[:, 4 * w:5 * w] = jnp.zeros((ts, w), dz_ref.dtype)
        dz_ref[:, 5 * w:] = (dyb * o_ref[...] * _dsilu(bg)).astype(dz_ref.dtype)
        do_ref[...] = dyb * _silu(bg)

        @pl.when(i == 0)
        def _():
            dw_ref[...] = jnp.zeros_like(dw_ref)
            db_ref[...] = jnp.zeros_like(db_ref)

        dw_ref[0:1, :] += jnp.sum(dconv * ubuf[pl.ds(SC_HALO - 2, ts), :], axis=0, keepdims=True)
        dw_ref[1:2, :] += jnp.sum(dconv * ubuf[pl.ds(SC_HALO - 1, ts), :], axis=0, keepdims=True)
        dw_ref[2:3, :] += jnp.sum(dconv * u, axis=0, keepdims=True)
        db_ref[...] += jnp.sum(dconv, axis=0, keepdims=True)

    return pl.pallas_call(
        body, name="even_gate_bwd", grid=(n,),
        in_specs=[_rows(ts, w, 0), _rows(ts, w, 1), _next_halo(ts, SC_HALO, w, 0, s),
                  _rows(ts, w, 0), _rows(ts, w, 1), _rows(ts, w, 2), _rows(ts, w, 3), _rows(ts, w, 5),
                  _prev_halo(ts, SC_HALO, w, 1), _prev_halo(ts, SC_HALO, w, 2),
                  _next_halo(ts, SC_HALO, w, 0, s), _next_halo(ts, SC_HALO, w, 3, s),
                  _rows(ts, w), _vec(w, 0, SC_KERNEL), _vec(w)],
        out_specs=[_rows(ts, EVEN_PAD), _rows(ts, w), _vec(w, 0, SC_KERNEL), _vec(w)],
        out_shape=[_sds((s, EVEN_PAD), MXU_DTYPE), _sds((s, w)), _sds((SC_KERNEL, w)), _sds((1, w))],
        scratch_shapes=[pltpu.VMEM((ts + SC_HALO, w), F32), pltpu.VMEM((ts + SC_HALO, w), F32)],
        compiler_params=_cp(1))(dy, dy, dy, z, z, z, z, z, z, z, z, z, o, sc_w, sc_b)


def _ln_act(uc, sg, g, b):
    mu = jnp.mean(uc, axis=-1, keepdims=True)
    var = jnp.mean(jnp.square(uc - mu), axis=-1, keepdims=True)
    return _silu((uc - mu) * lax.rsqrt(var + EPS) * g + b) * _silu(sg)


def _shifted_copies(buf, shifted, rows):
    for b in range(1, SUBLANES):
        shifted[b - 1, 0:rows, :] = buf[pl.ds(b, rows), :]


def _rows_at(buf, shifted, start, n):
    a, b = divmod(start, SUBLANES)
    return buf[pl.ds(SUBLANES * a, n), :] if b == 0 else shifted[b - 1, pl.ds(SUBLANES * a, n), :]


def _odd_fwd(z, conv_w, conv_b, ln_g, ln_b, ts):
    s = z.shape[0]
    d = D_MODEL
    k = CONF_KERNEL

    def body(val_ref, glu_ref, sg_ref, hval_ref, hglu_ref, w_ref, b_ref, g_ref, beta_ref, y_ref, uc_ref, ubuf, ush):
        i = pl.program_id(0)
        ubuf[0:CONF_HALO, :] = jnp.where(i > 0, hval_ref[...] * _sigmoid(hglu_ref[...]), 0.0)
        ubuf[CONF_HALO:, :] = val_ref[...] * _sigmoid(glu_ref[...])
        _shifted_copies(ubuf, ush, ts + CONF_HALO - SUBLANES)
        for r0 in range(0, ts, CONV_ROWS):
            acc = jnp.broadcast_to(b_ref[...], (CONV_ROWS, d))
            for j in range(k):
                acc = acc + w_ref[j:j + 1, :] * _rows_at(ubuf, ush, r0 + CONF_HALO - (k - 1) + j, CONV_ROWS)
            uc_ref[r0:r0 + CONV_ROWS, :] = acc
        y_ref[...] = _ln_act(uc_ref[...], sg_ref[...], g_ref[...], beta_ref[...]).astype(y_ref.dtype)

    return pl.pallas_call(
        body, name="odd_fwd", grid=(s // ts,),
        in_specs=[_rows(ts, d, 0), _rows(ts, d, 1), _rows(ts, d, 2),
                  _prev_halo(ts, CONF_HALO, d, 0), _prev_halo(ts, CONF_HALO, d, 1),
                  _vec(d, 0, k), _vec(d), _vec(d), _vec(d)],
        out_specs=[_rows(ts, d), _rows(ts, d)], out_shape=[_sds((s, d), MXU_DTYPE), _sds((s, d))],
        scratch_shapes=[pltpu.VMEM((ts + CONF_HALO, d), F32),
                        pltpu.VMEM((SUBLANES - 1, ts + CONF_HALO - SUBLANES, d), F32)],
        compiler_params=_cp(1))(z, z, z, z, z, conv_w, conv_b, ln_g, ln_b)


def _odd_bwd(dy, z, uc, conv_w, ln_g, ln_b, ts):
    s = z.shape[0]
    d = D_MODEL
    k = CONF_KERNEL
    n = s // ts

    def body(dy_ref, dyn_ref, val_ref, glu_ref, sg_ref, sgn_ref, uc_ref, ucn_ref,
             w_ref, g_ref, beta_ref, dz_ref, dw_ref, db_ref, dg_ref, dbeta_ref, dbuf, dsh, dw_acc):
        i = pl.program_id(0)
        val, glu = val_ref[...], glu_ref[...]
        sig = _sigmoid(glu)
        u = val * sig
        _, vjp = jax.vjp(_ln_act, uc_ref[...], sg_ref[...], g_ref[...], beta_ref[...])
        duc, dsg, dg, dbeta = vjp(dy_ref[...])
        _, vjp_n = jax.vjp(_ln_act, ucn_ref[...], sgn_ref[...], g_ref[...], beta_ref[...])
        dbuf[0:ts, :] = duc
        dbuf[ts:, :] = jnp.where(i < n - 1, vjp_n(dyn_ref[...])[0], 0.0)
        dz_ref[:, 2 * d:] = dsg.astype(dz_ref.dtype)
        _shifted_copies(dbuf, dsh, ts + CONF_HALO - SUBLANES)

        @pl.when(i == 0)
        def _():
            dw_acc[...] = jnp.zeros_like(dw_acc)
            db_ref[...] = jnp.zeros_like(db_ref)
            dg_ref[...] = jnp.zeros_like(dg_ref)
            dbeta_ref[...] = jnp.zeros_like(dbeta_ref)

        db_ref[...] += jnp.sum(duc, axis=0, keepdims=True)
        dg_ref[...] += dg
        dbeta_ref[...] += dbeta
        for r0 in range(0, ts, CONV_ROWS):
            acc = jnp.zeros((CONV_ROWS, d), F32)
            for j in range(k):
                acc = acc + w_ref[j:j + 1, :] * _rows_at(dbuf, dsh, r0 + (k - 1) - j, CONV_ROWS)
            sig_r = sig[r0:r0 + CONV_ROWS, :]
            dz_ref[r0:r0 + CONV_ROWS, 0:d] = (acc * sig_r).astype(dz_ref.dtype)
            dz_ref[r0:r0 + CONV_ROWS, d:2 * d] = (acc * val[r0:r0 + CONV_ROWS, :] * sig_r * (1.0 - sig_r)).astype(dz_ref.dtype)
        for j in range(k):
            prod = _rows_at(dbuf, dsh, (k - 1) - j, ts) * u
            dw_acc[j] += jnp.sum(prod.reshape(ts // SUBLANES, SUBLANES, d), axis=0)

        @pl.when(i == n - 1)
        def _():
            dw_ref[...] = jnp.sum(dw_acc[...], axis=1)

    return pl.pallas_call(
        body, name="odd_bwd", grid=(n,),
        in_specs=[_rows(ts, d), _next_halo(ts, CONF_HALO, d, 0, s),
                  _rows(ts, d, 0), _rows(ts, d, 1), _rows(ts, d, 2), _next_halo(ts, CONF_HALO, d, 2, s),
                  _rows(ts, d), _next_halo(ts, CONF_HALO, d, 0, s),
                  _vec(d, 0, k), _vec(d), _vec(d)],
        out_specs=[_rows(ts, ODD_IN), _vec(d, 0, k), _vec(d), _vec(d), _vec(d)],
        out_shape=[_sds((s, ODD_IN), MXU_DTYPE), _sds((k, d)), _sds((1, d)), _sds((1, d)), _sds((1, d))],
        scratch_shapes=[pltpu.VMEM((ts + CONF_HALO, d), F32),
                        pltpu.VMEM((SUBLANES - 1, ts + CONF_HALO - SUBLANES, d), F32), pltpu.VMEM((k, SUBLANES, d), F32)],
        compiler_params=_cp(1))(dy, dy, z, z, z, z, uc, uc, conv_w, ln_g, ln_b)


def _local_step(x, target, cos, sin, mod, p, layer_weights, fwd_dep=None, grads_done=None):
    s = x.shape[0]
    tsf, tsb = min(512, s // 2), min(256, s // 2)
    tq = min(512, s // 2)
    row1 = lambda a, i: a[i:i + 1]
    saved = []
    for layer in range(DEPTH):
        i = layer // 2
        mod_l = row1(mod, layer)
        wl = layer_weights(layer, x)
        h = _pre_fwd(x, row1(p["pre_norm_g"], layer), mod_l, tsf, fwd_dep if layer == 0 else None)
        if layer % 2 == 0:
            z = _mm(h, wl["w_in"], "nn", F32, 256, EVEN_PAD, "even_in_fwd")
            q, k, v = _mla_prep_fwd(z, cos, sin, row1(p["even_q_norm_g"], i), row1(p["even_kv_norm_g"], i),
                                    wl["wq"], wl["wq_rot"], wl["wuk"], wl["wuv"], tsf)
            o, lse = _attn_fwd(q, k, v, tq)
            y = _even_gate_fwd(z, o, wl["sc_conv_w"], row1(p["even_sc_conv_b"], i), tsf)
            yo = _mm(y, wl["w_out"], "nn", F32, 512, 1024, "even_out_fwd")
            saved.append((x, h, z, y, yo, wl, (q, k, v, o, lse)))
        else:
            z = _mm(h, wl["w_in"], "nn", F32, 256, ODD_IN, "odd_in_fwd")
            y, uc = _odd_fwd(z, wl["conv_w"], wl["conv_b"], wl["ln_g"], wl["ln_b"], tsf)
            yo = _mm(y, wl["w_out"], "nn", F32, 512, 1024, "odd_out_fwd")
            saved.append((x, h, z, y, yo, wl, uc))
        x = _post_fwd(x, yo, row1(p["post_norm_g"], layer), mod_l, tsf)

    loss, dx = _loss_fwd_bwd(x, target, tsf)

    g = {n: [None] * (DEPTH if n in ("pre_norm_g", "post_norm_g") else N_PAIRS) for n in (
        "pre_norm_g", "post_norm_g", "even_sc_conv_w", "even_sc_conv_b", "even_q_norm_g", "even_kv_norm_g",
        "odd_conv_w", "odd_conv_b", "odd_ln_g", "odd_ln_b")}
    dmod = [None] * DEPTH
    dep = None
    for layer in reversed(range(DEPTH)):
        i = layer // 2
        mod_l = row1(mod, layer)
        x_in, h, z, y, yo, wl, extra = saved[layer]
        dyo, dgate, g["post_norm_g"][layer] = _post_bwd(dx, yo, row1(p["post_norm_g"], layer), mod_l, tsb, dep)
        bufs = {}
        if layer % 2 == 0:
            q, k, v, o, lse = extra
            dy = _mm(dyo, wl["w_out"], "nt", F32, 512, 1024, "even_out_bwd_x")
            bufs["even_w_out"] = _mm_tn_shards(y, dyo, "rows", "even_out_bwd_w")
            dz, do, g["even_sc_conv_w"][i], g["even_sc_conv_b"][i] = _even_gate_bwd(
                dy, z, o, wl["sc_conv_w"], row1(p["even_sc_conv_b"], i), tsb)
            dq, dk, dv = _attn_bwd(q, k, v, do, o, lse, tq)
            dz, bufs["even_mla"], g["even_q_norm_g"][i], g["even_kv_norm_g"][i] = _mla_prep_bwd(
                dz, dq, dk, dv, z, cos, sin, row1(p["even_q_norm_g"], i), row1(p["even_kv_norm_g"], i),
                wl["wq"], wl["wuk"], wl["wuv"], tsb)
            bufs["even_w_in"] = _ein_to_shards(_mm(h, dz, "tn", F32, D_MODEL, 512, "even_in_bwd_w"))
        else:
            uc = extra
            dy = _mm(dyo, wl["w_out"], "nt", F32, 512, 1024, "odd_out_bwd_x")
            bufs["odd_w_out"] = _mm_tn_shards(y, dyo, "rows", "odd_out_bwd_w")
            dz, g["odd_conv_w"][i], g["odd_conv_b"][i], g["odd_ln_g"][i], g["odd_ln_b"][i] = _odd_bwd(
                dy, z, uc, wl["conv_w"], wl["ln_g"], wl["ln_b"], tsb)
            bufs["odd_w_in"] = _mm_tn_shards(h, dz, "cols", "odd_in_bwd_w")
        dx, dshift, dscale, g["pre_norm_g"][layer] = _pre_bwd(
            dz, wl["w_in"], dx, x_in, row1(p["pre_norm_g"], layer), mod_l, tsb)
        dmod[layer] = jnp.concatenate([dshift, dscale, dgate], axis=-1)
        dep = grads_done(layer, bufs, dx) if grads_done is not None else None
    stack = lambda parts: jnp.stack([a[0] if a.shape[0] == 1 and a.ndim == 2 else a for a in parts])
    small = {n: stack(parts) for n, parts in g.items()}
    small["dmod"] = jnp.concatenate(dmod, axis=0)
    return loss, dx, small


def _uq_to_heads(w):
    w = w.reshape(N_CHIPS, Q_LORA, 2, QK_NOPE + QK_ROPE).transpose(0, 2, 1, 3).reshape(HEADS, Q_LORA, QK_NOPE + QK_ROPE)
    half = QK_ROPE // 2
    rotated = jnp.concatenate([jnp.zeros_like(w[..., :QK_NOPE]), -w[..., QK_NOPE + half:], w[..., QK_NOPE:QK_NOPE + half]],
                              axis=-1)
    pad = ((0, 0), (0, 0), (0, HEAD_PAD - QK_NOPE - QK_ROPE))
    return _side_by_side(jnp.pad(w, pad)), _side_by_side(jnp.pad(rotated, pad))


def _side_by_side(w):
    return w.transpose(1, 0, 2).reshape(w.shape[1], HEADS * HEAD_PAD)


def _ukv_to_heads(w):
    w = w.reshape(N_CHIPS, KV_LORA, 2, QK_NOPE + V_HEAD).transpose(0, 2, 1, 3).reshape(HEADS, KV_LORA, QK_NOPE + V_HEAD)
    wk = jnp.pad(w[..., :QK_NOPE], ((0, 0), (0, 0), (0, HEAD_PAD - QK_NOPE)))
    wv = w[..., QK_NOPE:]
    zero = jnp.zeros_like(wv)
    odd = (jnp.arange(HEADS) % 2 == 1)[:, None, None]
    wv = jnp.concatenate([jnp.where(odd, zero, wv), jnp.where(odd, wv, zero)], axis=-1)
    return _side_by_side(wk), _side_by_side(wv)


def _mla_local(q):
    blocks = q.reshape(2, MLA_ROWS, HEAD_PAD)
    uq = jnp.concatenate([blocks[r, :Q_LORA, :QK_NOPE + QK_ROPE] for r in range(2)], axis=-1)
    ukv = jnp.concatenate(
        [jnp.concatenate([blocks[r, Q_LORA:Q_LORA + KV_LORA, :QK_NOPE],
                          blocks[r, Q_LORA + KV_LORA:, V_HEAD * r:V_HEAD * (r + 1)]], axis=-1) for r in range(2)], axis=-1)
    return uq, ukv


def _place():
    return lax.axis_index("x"), lax.axis_index("y"), lax.axis_index("c")


def _flip(v, bit):
    return 1 - v if bit else v


def _sem(a, k):
    return a * (N_CHIPS - 1) + k - 1


def _remote(src, dst, send_sem, recv_sem, peer):
    return pltpu.make_async_remote_copy(src_ref=src, dst_ref=dst, send_sem=send_sem, recv_sem=recv_sem,
                                        device_id=peer, device_id_type=MESH)


_VMEM_SPEC = pl.BlockSpec(memory_space=pltpu.VMEM)
_HBM_SPEC = pl.BlockSpec(memory_space=pl.ANY)


def _ada_fwd(c8, ada_w, ada_b_sh):
    depth, d, cols = ada_w.shape

    def body(c_ref, w_ref, b_ref, call_ref, mod_ref, s1, r1, s2, r2):
        x, y, c = _place()
        chip = 2 * x + y
        me = 2 * chip + c
        call_ref[me] = c_ref[...]
        sends = []
        for k in range(1, N_DEV):
            peer = (_flip(x, k & 4), _flip(y, k & 2), _flip(c, k & 1))
            cp = _remote(c_ref, call_ref.at[me], s1.at[k - 1], r1.at[k - 1], peer)
            cp.start()
            sends.append(cp)
        for k in range(1, N_DEV):
            src = 4 * _flip(x, k & 4) + 2 * _flip(y, k & 2) + _flip(c, k & 1)
            _remote(c_ref, call_ref.at[src], s1.at[k - 1], r1.at[k - 1], (x, y, c)).wait_recv()
        act = _silu(jnp.concatenate([call_ref[e, 0:1, :] for e in range(N_DEV)], axis=0))
        for l in range(depth):
            mod_ref[chip, l] = _dot_nn(act, w_ref[l]) + b_ref[l:l + 1, :]
        for k in range(1, N_CHIPS):
            peer = (_flip(x, k & 2), _flip(y, k & 1), c)
            cp = _remote(mod_ref.at[chip], mod_ref.at[chip], s2.at[k - 1], r2.at[k - 1], peer)
            cp.start()
            sends.append(cp)
        for k in range(1, N_CHIPS):
            src = 2 * _flip(x, k & 2) + _flip(y, k & 1)
            _remote(mod_ref.at[src], mod_ref.at[src], s2.at[k - 1], r2.at[k - 1], (x, y, c)).wait_recv()
        for cp in sends:
            cp.wait_send()

    return pl.pallas_call(
        body, name="ada_fwd", in_specs=[_VMEM_SPEC] * 3, out_specs=[_VMEM_SPEC] * 2,
        out_shape=[_sds((N_DEV, 8, d)), _sds((N_CHIPS, depth, N_DEV, cols))],
        scratch_shapes=[pltpu.SemaphoreType.DMA((N_DEV - 1,)), pltpu.SemaphoreType.DMA((N_DEV - 1,)),
                        pltpu.SemaphoreType.DMA((N_CHIPS - 1,)), pltpu.SemaphoreType.DMA((N_CHIPS - 1,))],
        compiler_params=pltpu.CompilerParams(vmem_limit_bytes=VMEM_LIMIT_V7X))(c8, ada_w, ada_b_sh)


def _ada_bwd(c_t, dmod_sh):
    depth, n, cols = dmod_sh.shape
    d = c_t.shape[0]
    tr = 256

    def body(c_ref, dm_ref, o_ref):
        act = _silu(c_ref[...])
        acc = act[:, 0:1] * dm_ref[0, 0:1, :]
        for e in range(1, n):
            acc = acc + act[:, e:e + 1] * dm_ref[0, e:e + 1, :]
        o_ref[0] = acc

    return pl.pallas_call(
        body, name="ada_bwd", grid=(depth, d // tr),
        in_specs=[pl.BlockSpec((tr, n), lambda l, i: (i, 0)), pl.BlockSpec((1, n, cols), lambda l, i: (l, 0, 0))],
        out_specs=pl.BlockSpec((1, tr, cols), lambda l, i: (l, i, 0)), out_shape=_sds((depth, d, cols)),
        compiler_params=_cp(2))(c_t, dmod_sh)


def _gathered_shape(shape, how):
    if how == "slot":
        return (N_CHIPS,) + shape
    r, cc = shape
    return (r, N_CHIPS * cc) if how == "cols" else (N_CHIPS * r, cc)


def _gathered_part(ref, shape, how, chip):
    if how == "slot":
        return ref.at[chip]
    if how == "cols":
        return ref.at[:, pl.ds(pl.multiple_of(chip * shape[1], 128), shape[1])]
    return ref.at[pl.ds(pl.multiple_of(chip * shape[0], 8), shape[0]), :]


_SEM_SPEC = pl.BlockSpec(memory_space=pltpu.SEMAPHORE)
_TOKEN = jax.ShapeDtypeStruct((8, 128), F32)
_SPLIT_COPY = pltpu.CompilerParams(has_side_effects=pltpu.SideEffectType.DATAFLOW_SIDE_EFFECTING)


def _in_hbm(a):
    return pltpu.with_memory_space_constraint(a, pltpu.HBM)


def _gather_start(items, gathered, name, after=()):
    n = len(items)

    def body(*refs):
        ins, outs = refs[:n], refs[n:2 * n]
        send_sems, recv_sems = refs[2 * n + len(after)], refs[2 * n + len(after) + 1]
        x, y, c = _place()
        for a in range(n):
            for k in range(1, N_CHIPS):
                part = _gathered_part(outs[a], items[a][0].shape, items[a][1], 2 * x + y)
                _remote(ins[a], part, send_sems.at[_sem(a, k)], recv_sems.at[_sem(a, k)],
                        (_flip(x, k & 2), _flip(y, k & 1), c)).start()
        refs[-1][...] = jnp.zeros(_TOKEN.shape, _TOKEN.dtype)

    arrays = [_in_hbm(a) for a, _ in items] + [_in_hbm(a) for a in gathered]
    res = pl.pallas_call(
        body, name=name, in_specs=[_HBM_SPEC] * (2 * n + len(after)),
        out_specs=[_SEM_SPEC, _SEM_SPEC] + [_HBM_SPEC] * (2 * n) + [_VMEM_SPEC],
        out_shape=[pltpu.SemaphoreType.DMA((n * (N_CHIPS - 1),)), pltpu.SemaphoreType.DMA((n * (N_CHIPS - 1),))]
        + [pltpu.HBM(a.shape, a.dtype) for a in arrays] + [_TOKEN],
        input_output_aliases={a: 2 + a for a in range(2 * n)}, compiler_params=_SPLIT_COPY)(*arrays, *after)
    return res[0], res[1], res[2:2 + n], res[2 + n:2 + 2 * n], res[-1]


def _gather_wait(items, started, after, name):
    n = len(items)
    send_sems, recv_sems, shards, gathered, _ = started

    def body(*refs):
        ins, outs, send_sems, recv_sems = refs[:n], refs[n:2 * n], refs[2 * n], refs[2 * n + 1]
        x, y, c = _place()
        for a in range(n):
            for k in range(1, N_CHIPS):
                part = _gathered_part(outs[a], items[a][0].shape, items[a][1], 2 * _flip(x, k & 2) + _flip(y, k & 1))
                cp = _remote(ins[a], part, send_sems.at[_sem(a, k)], recv_sems.at[_sem(a, k)], (x, y, c))
                cp.wait_send()
                cp.wait_recv()

    res = pl.pallas_call(
        body, name=name, in_specs=[_HBM_SPEC] * (2 * n) + [_SEM_SPEC, _SEM_SPEC] + [_HBM_SPEC] * len(after),
        out_specs=[_HBM_SPEC] * (2 * n), out_shape=[pltpu.HBM(a.shape, a.dtype) for a in (*shards, *gathered)],
        input_output_aliases={a: a for a in range(2 * n)}, compiler_params=_SPLIT_COPY)(
            *shards, *gathered, send_sems, recv_sems, *after)
    return res[n:]


def _rs_start(bufs, name, after=()):
    n = len(bufs)

    def body(*refs):
        srcs, lands = refs[:n], refs[n:2 * n]
        send_sems, recv_sems = refs[2 * n + len(after)], refs[2 * n + len(after) + 1]
        x, y, c = _place()
        for a in range(n):
            for k in range(1, N_CHIPS):
                tx, ty = _flip(x, k & 2), _flip(y, k & 1)
                _remote(srcs[a].at[2 * tx + ty], lands[a].at[k - 1], send_sems.at[_sem(a, k)], recv_sems.at[_sem(a, k)],
                        (tx, ty, c)).start()
        refs[-1][...] = jnp.zeros(_TOKEN.shape, _TOKEN.dtype)

    arrays = [_in_hbm(b) for b in bufs] + [_in_hbm(lax.empty((N_CHIPS - 1,) + b.shape[1:], b.dtype)) for b in bufs]
    res = pl.pallas_call(
        body, name=name, in_specs=[_HBM_SPEC] * (2 * n + len(after)),
        out_specs=[_SEM_SPEC, _SEM_SPEC] + [_HBM_SPEC] * (2 * n) + [_VMEM_SPEC],
        out_shape=[pltpu.SemaphoreType.DMA((n * (N_CHIPS - 1),)), pltpu.SemaphoreType.DMA((n * (N_CHIPS - 1),))]
        + [pltpu.HBM(a.shape, a.dtype) for a in arrays] + [_TOKEN],
        input_output_aliases={a: 2 + a for a in range(2 * n)}, compiler_params=_SPLIT_COPY)(*arrays, *after)
    return res[0], res[1], res[2:2 + n], res[2 + n:2 + 2 * n], res[-1]


def _rs_wait(started, after, name):
    send_sems, recv_sems, bufs, lands, _ = started
    n = len(bufs)

    def body(*refs):
        srcs, lnds, send_sems, recv_sems = refs[:n], refs[n:2 * n], refs[2 * n], refs[2 * n + 1]
        x, y, c = _place()
        for a in range(n):
            for k in range(1, N_CHIPS):
                cp = _remote(srcs[a].at[0], lnds[a].at[k - 1], send_sems.at[_sem(a, k)], recv_sems.at[_sem(a, k)], (x, y, c))
                cp.wait_send()
                cp.wait_recv()

    res = pl.pallas_call(
        body, name=name, in_specs=[_HBM_SPEC] * (2 * n) + [_SEM_SPEC, _SEM_SPEC] + [_HBM_SPEC] * len(after),
        out_specs=[_HBM_SPEC] * (2 * n), out_shape=[pltpu.HBM(a.shape, a.dtype) for a in (*bufs, *lands)],
        input_output_aliases={a: a for a in range(2 * n)}, compiler_params=_SPLIT_COPY)(
            *bufs, *lands, send_sems, recv_sems, *after)
    return res[:n], res[n:]


def _place_own(shard, how, chip_idx):
    r, cc = shard.shape
    block, index = {"slot": ((1, r, cc), lambda i, c: (c[0], 0, 0)), "cols": ((r, cc), lambda i, c: (0, c[0])),
                    "rows": ((r, cc), lambda i, c: (c[0], 0))}[how]

    def body(c_ref, in_ref, o_ref):
        del c_ref
        o_ref[...] = in_ref[...].reshape(o_ref.shape)

    return pl.pallas_call(
        body, name="place_own", out_shape=_sds(_gathered_shape(shard.shape, how), shard.dtype),
        grid_spec=pltpu.PrefetchScalarGridSpec(
            num_scalar_prefetch=1, grid=(1,), in_specs=[pl.BlockSpec((r, cc), lambda i, c: (0, 0))],
            out_specs=pl.BlockSpec(block, index)),
        compiler_params=_cp(1))(chip_idx, shard)


def _gather_chips(items, after=()):
    n = n_all = len(items)
    arrays = [a for a, _ in items]

    def body(*refs):
        ins, outs = refs[:n_all], refs[-n_all - 3:-3]
        send_sems, recv_sems, local_sems = refs[-3:]
        x, y, c = _place()
        chip = 2 * x + y
        part = lambda a, j: _gathered_part(outs[a], items[a][0].shape, items[a][1], j)
        local = [pltpu.make_async_copy(ins[a], part(a, chip), local_sems.at[a]) for a in range(n_all)]
        for cp in local[:n]:
            cp.start()
        sends = []
        for a in range(n):
            for k in range(1, N_CHIPS):
                peer = (_flip(x, k & 2), _flip(y, k & 1), c)
                cp = _remote(ins[a], part(a, chip), send_sems.at[_sem(a, k)], recv_sems.at[_sem(a, k)], peer)
                cp.start()
                sends.append(cp)
        for cp in local[n:]:
            cp.start()
        for a in range(n):
            for k in range(1, N_CHIPS):
                src = 2 * _flip(x, k & 2) + _flip(y, k & 1)
                _remote(ins[a], part(a, src), send_sems.at[_sem(a, k)], recv_sems.at[_sem(a, k)], (x, y, c)).wait_recv()
        for cp in sends:
            cp.wait_send()
        for cp in local:
            cp.wait()

    return pl.pallas_call(
        body, name="gather_chips", in_specs=[_HBM_SPEC] * (n_all + len(after)), out_specs=[_HBM_SPEC] * n_all,
        out_shape=[_sds(_gathered_shape(a.shape, how), a.dtype) for a, how in items],
        scratch_shapes=[pltpu.SemaphoreType.DMA((n * (N_CHIPS - 1),)), pltpu.SemaphoreType.DMA((n * (N_CHIPS - 1),)),
                        pltpu.SemaphoreType.DMA((n_all,))])(*arrays, *after)


def _gather_sum_all(small):
    r, w = small.shape

    def body(in_ref, all_ref, sum_ref, send_sems, recv_sems):
        x, y, c = _place()
        me = 4 * x + 2 * y + c
        all_ref[me] = in_ref[...]
        sends = []
        for k in range(1, N_DEV):
            peer = (_flip(x, k & 4), _flip(y, k & 2), _flip(c, k & 1))
            cp = _remote(in_ref, all_ref.at[me], send_sems.at[k - 1], recv_sems.at[k - 1], peer)
            cp.start()
            sends.append(cp)
        for k in range(1, N_DEV):
            src = 4 * _flip(x, k & 4) + 2 * _flip(y, k & 2) + _flip(c, k & 1)
            _remote(in_ref, all_ref.at[src], send_sems.at[k - 1], recv_sems.at[k - 1], (x, y, c)).wait_recv()
        acc = all_ref[0]
        for e in range(1, N_DEV):
            acc = acc + all_ref[e]
        sum_ref[...] = acc
        for cp in sends:
            cp.wait_send()

    return pl.pallas_call(
        body, name="gather_sum_all", in_specs=[_VMEM_SPEC], out_specs=[_VMEM_SPEC] * 2,
        out_shape=[_sds((N_DEV, r, w)), _sds((r, w))],
        scratch_shapes=[pltpu.SemaphoreType.DMA((N_DEV - 1,)), pltpu.SemaphoreType.DMA((N_DEV - 1,))],
        compiler_params=pltpu.CompilerParams(vmem_limit_bytes=VMEM_LIMIT_V7X))(small)


def _add_chips(buf, t, chip_idx):
    r, cc = buf.shape[1:]
    tr = min(256, r)

    def body(c_ref, p_ref, t_ref, o_ref):
        del c_ref
        o_ref[...] = p_ref[0] + t_ref[0].astype(F32) + t_ref[1].astype(F32) + t_ref[2].astype(F32)

    return pl.pallas_call(
        body, name="add_chips", out_shape=_sds((r, cc)),
        grid_spec=pltpu.PrefetchScalarGridSpec(
            num_scalar_prefetch=1, grid=(r // tr,),
            in_specs=[pl.BlockSpec((1, tr, cc), lambda i, c: (c[0], i, 0)),
                      pl.BlockSpec((N_CHIPS - 1, tr, cc), lambda i, c: (0, i, 0))],
            out_specs=pl.BlockSpec((tr, cc), lambda i, c: (i, 0))),
        compiler_params=_cp(1))(chip_idx, buf, t)


def _rs_sibling(qs):
    n = len(qs)

    def body(*refs):
        ins, outs = refs[:n], refs[n:2 * n]
        send_sems, recv_sems = refs[2 * n:]
        x, y, c = _place()
        copies = [_remote(ins[a], outs[a], send_sems.at[a], recv_sems.at[a], (x, y, 1 - c)) for a in range(n)]
        for cp in copies:
            cp.start()
        for cp in copies:
            cp.wait()

    return pl.pallas_call(
        body, name="rs_sibling", in_specs=[_HBM_SPEC] * n, out_specs=[_HBM_SPEC] * n,
        out_shape=[_sds(q.shape) for q in qs],
        scratch_shapes=[pltpu.SemaphoreType.DMA((n,)), pltpu.SemaphoreType.DMA((n,))])(*qs)


def _adamw_update(w, g, m, v):
    m = ADAM_B1 * m + (1.0 - ADAM_B1) * g
    v = ADAM_B2 * v + (1.0 - ADAM_B2) * jnp.square(g)
    m_hat = m / (1.0 - ADAM_B1 ** ADAM_STEP)
    v_hat = v / (1.0 - ADAM_B2 ** ADAM_STEP)
    return -ADAM_LR * (m_hat / (jnp.sqrt(v_hat) + ADAM_EPS) + ADAM_WD * w), m, v


def _adamw(w, g_parts, m, v, name):
    shape = w.shape
    cols = shape[-1]
    rows = _size(shape[:-1])
    tr = 512 if rows % 512 == 0 else rows
    spec = pl.BlockSpec((tr, cols), lambda i: (i, 0))
    n = len(g_parts)

    def body(*refs):
        w_ref, m_ref, v_ref = refs[:3]
        g_ref, d_ref, nm_ref, nv_ref = refs[3 + n:]
        g = refs[3][...]
        for r in refs[4:3 + n]:
            g = g + r[...]
        g_ref[...] = g
        d_ref[...], nm_ref[...], nv_ref[...] = _adamw_update(w_ref[...], g, m_ref[...], v_ref[...])

    outs = pl.pallas_call(
        body, name="adamw_" + name, grid=(rows // tr,), in_specs=[spec] * (3 + n), out_specs=[spec] * 4,
        out_shape=[_sds((rows, cols))] * 4, compiler_params=_cp(1))(
            *[a.reshape(rows, cols) for a in (w, m, v, *g_parts)])
    return tuple(o.reshape(shape) for o in outs)


def _adamw_layer(w, g_parts, m, v, layer, prev, name):
    _, r, cc = w.shape
    tr = 512 if r % 512 == 0 else r
    spec = pl.BlockSpec((1, tr, cc), lambda i: (layer, i, 0))
    n = len(g_parts)

    def body(*refs):
        w_ref, m_ref, v_ref = refs[:3]
        g_ref, d_ref, nm_ref, nv_ref = refs[-4:]
        g = refs[3][...]
        for q in refs[4:3 + n]:
            g = g + q[...]
        g = g[:, :cc]
        g_ref[0] = g
        d_ref[0], nm_ref[0], nv_ref[0] = _adamw_update(w_ref[0], g, m_ref[0], v_ref[0])

    g_specs = [pl.BlockSpec((tr, q.shape[1]), lambda i: (i, 0)) for q in g_parts]
    passed = () if prev is None else tuple(prev)
    return pl.pallas_call(
        body, name="adamw_" + name, grid=(r // tr,),
        in_specs=[spec] * 3 + g_specs + [_HBM_SPEC] * len(passed), out_specs=[spec] * 4,
        out_shape=[_sds(w.shape)] * 4, input_output_aliases={3 + n + k: k for k in range(len(passed))},
        compiler_params=_cp(1))(w, m, v, *g_parts, *passed)


def _size(shape):
    n = 1
    for s in shape:
        n *= s
    return n


_SMALL = (("dmod", (DEPTH, 3 * D_MODEL)), ("pre_norm_g", (DEPTH, D_MODEL)), ("post_norm_g", (DEPTH, D_MODEL)),
          ("even_sc_conv_w", (2, SC_KERNEL, SC_WIDTH)), ("even_sc_conv_b", (2, SC_WIDTH)),
          ("even_q_norm_g", (2, Q_LORA)), ("even_kv_norm_g", (2, KV_LORA)),
          ("odd_conv_w", (2, CONF_KERNEL, D_MODEL)), ("odd_conv_b", (2, D_MODEL)), ("odd_ln_g", (2, D_MODEL)),
          ("odd_ln_b", (2, D_MODEL)))
SMALL_ROWS = -(-sum(_size(s) for _, s in _SMALL) // (8 * 128)) * 8

_SMALL_W = (("even_sc_conv_w", (2, SC_KERNEL, SC_WIDTH // N_CHIPS)), ("odd_conv_w", (2, CONF_KERNEL, D_MODEL // N_CHIPS)),
            ("odd_conv_b", (2, D_MODEL // N_CHIPS)), ("odd_ln_g", (2, D_MODEL // N_CHIPS)),
            ("odd_ln_b", (2, D_MODEL // N_CHIPS)))
SMALL_W_ROWS = -(-sum(_size(s) for _, s in _SMALL_W) // (8 * 128)) * 8


def _pack_rows(arrays, layout, rows):
    flat = jnp.concatenate([arrays[n].reshape(-1) for n, _ in layout])
    return jnp.pad(flat, (0, rows * 128 - flat.shape[0])).reshape(rows, 128)


def _unpack_small(t):
    flat = t.reshape(-1)
    out, at = {}, 0
    for n, shape in _SMALL:
        out[n] = flat[at:at + _size(shape)].reshape(shape)
        at += _size(shape)
    return out


def _unpack_small_w(t):
    flat = t.reshape(N_CHIPS, -1)
    out, at = {}, 0
    for n, shape in _SMALL_W:
        a = flat[:, at:at + _size(shape)].reshape((N_CHIPS,) + shape)
        out[n] = jnp.moveaxis(a, 0, -2).reshape(shape[:-1] + (N_CHIPS * shape[-1],))
        at += _size(shape)
    return out


def _chip_cols(a, chip):
    n = a.shape[-1] // N_CHIPS
    return lax.dynamic_slice_in_dim(a, chip * n, n, axis=a.ndim - 1)


def _join_cols(a):
    _, l, r, cc = a.shape
    return a.transpose(1, 2, 0, 3).reshape(l, r, N_CHIPS * cc)


WEIGHT_NAMES = ("ada_w", "ada_b", "pre_norm_g", "post_norm_g", "even_w_in", "even_sc_conv_w", "even_sc_conv_b",
                "even_q_norm_g", "even_kv_norm_g", "even_w_uq", "even_w_ukv", "even_w_out", "odd_w_in", "odd_conv_w",
                "odd_conv_b", "odd_ln_g", "odd_ln_b", "odd_w_out")
GATHER_HOW = ((("even_w_in", "slot"), ("even_w_uq", "slot"), ("even_w_ukv", "slot"), ("even_w_out", "rows")),
              (("odd_w_in", "cols"), ("odd_w_out", "rows")))


def kernel(x, c, positions, ada_w, ada_b, pre_norm_g, post_norm_g, even_w_in, even_sc_conv_w, even_sc_conv_b, even_q_norm_g, even_kv_norm_g, even_w_uq, even_w_ukv, even_w_out, odd_w_in, odd_conv_w, odd_conv_b, odd_ln_g, odd_ln_b, odd_w_out, loss_target, m_ada_w, m_ada_b, m_pre_norm_g, m_post_norm_g, m_even_w_in, m_even_sc_conv_w, m_even_sc_conv_b, m_even_q_norm_g, m_even_kv_norm_g, m_even_w_uq, m_even_w_ukv, m_even_w_out, m_odd_w_in, m_odd_conv_w, m_odd_conv_b, m_odd_ln_g, m_odd_ln_b, m_odd_w_out, v_ada_w, v_ada_b, v_pre_norm_g, v_post_norm_g, v_even_w_in, v_even_sc_conv_w, v_even_sc_conv_b, v_even_q_norm_g, v_even_kv_norm_g, v_even_w_uq, v_even_w_ukv, v_even_w_out, v_odd_w_in, v_odd_conv_w, v_odd_conv_b, v_odd_ln_g, v_odd_ln_b, v_odd_w_out):
    w = dict(zip(WEIGHT_NAMES, (ada_w, ada_b, pre_norm_g, post_norm_g, even_w_in, even_sc_conv_w, even_sc_conv_b,
                                even_q_norm_g, even_kv_norm_g, even_w_uq, even_w_ukv, even_w_out, odd_w_in, odd_conv_w,
                                odd_conv_b, odd_ln_g, odd_ln_b, odd_w_out)))
    m = dict(zip(WEIGHT_NAMES, (m_ada_w, m_ada_b, m_pre_norm_g, m_post_norm_g, m_even_w_in, m_even_sc_conv_w,
                                m_even_sc_conv_b, m_even_q_norm_g, m_even_kv_norm_g, m_even_w_uq, m_even_w_ukv,
                                m_even_w_out, m_odd_w_in, m_odd_conv_w, m_odd_conv_b, m_odd_ln_g, m_odd_ln_b, m_odd_w_out)))
    v = dict(zip(WEIGHT_NAMES, (v_ada_w, v_ada_b, v_pre_norm_g, v_post_norm_g, v_even_w_in, v_even_sc_conv_w,
                                v_even_sc_conv_b, v_even_q_norm_g, v_even_kv_norm_g, v_even_w_uq, v_even_w_ukv,
                                v_even_w_out, v_odd_w_in, v_odd_conv_w, v_odd_conv_b, v_odd_ln_g, v_odd_ln_b, v_odd_w_out)))
    ix, iy, ic = _place()
    chip = 2 * ix + iy
    me = 2 * chip + ic
    s = x.shape[1]

    c_all, mod_all = _ada_fwd(jnp.broadcast_to(c, (8, D_MODEL)), ada_w, _chip_cols(ada_b, chip))
    mod = lax.dynamic_index_in_dim(mod_all, me, axis=2, keepdims=False)
    mod = mod.transpose(1, 0, 2).reshape(DEPTH, 3 * D_MODEL)

    items = [[(w[n][layer // 2].astype(MXU_DTYPE), how) for n, how in GATHER_HOW[layer % 2]] for layer in range(DEPTH)]
    later_items = [item for layer_items in items[1:] for item in layer_items]
    first = _gather_chips(items[0] + [(_pack_rows(w, _SMALL_W, SMALL_W_ROWS), "slot")], [mod_all])
    small_w = _unpack_small_w(first[len(items[0])])
    weights_sent = _gather_start(later_items, [_place_own(a, how, chip.reshape(1)) for a, how in later_items],
                                 "gather_start", [first[0]])
    later = []

    def layer_weights(layer, x_in):
        i = layer // 2
        if layer == 0:
            arrays = first[:len(items[0])]
        else:
            if not later:
                later.extend(_gather_wait(later_items, weights_sent, [x_in], "gather_wait"))
            at = sum(len(layer_items) for layer_items in items[1:layer])
            arrays = later[at:at + len(items[layer])]
        if layer % 2 == 0:
            ein, uq, ukv, eout = arrays
            wuk, wuv = _ukv_to_heads(ukv)
            wq, wq_rot = _uq_to_heads(uq)
            return {"w_in": _ein_from_shards(ein), "wq": wq, "wq_rot": wq_rot, "wuk": wuk, "wuv": wuv, "w_out": eout,
                    "sc_conv_w": small_w["even_sc_conv_w"][i]}
        oin, oout = arrays
        return {"w_in": oin, "w_out": oout, "conv_w": small_w["odd_conv_w"][i], "conv_b": small_w["odd_conv_b"][i:i + 1],
                "ln_g": small_w["odd_ln_g"][i:i + 1], "ln_b": small_w["odd_ln_b"][i:i + 1]}

    in_flight, own, sib, last = {}, {}, {}, {}

    def land(layer, after):
        names, started, kept = in_flight.pop(layer)
        bufs, arrived = _rs_wait(started, after, "rs_wait_%d" % layer)
        sums = [_add_chips(b, t, chip.reshape(1)) for b, t in zip(bufs if kept is None else kept, arrived)]
        for n, mine, theirs in zip(names, sums, _rs_sibling(sums)):
            own[n, layer // 2], sib[n, layer // 2] = mine, theirs

    def grads_done(layer, bufs, dx_in):
        if layer + 1 in in_flight:
            land(layer + 1, [dx_in])
        if layer == 0:
            last.update(bufs)
            return None
        names = sorted(bufs)
        in_flight[layer] = (names, _rs_start([bufs[n] for n in names], "rs_start_%d" % layer), None)
        return in_flight[layer][1][-1]

    p = {"pre_norm_g": pre_norm_g, "post_norm_g": post_norm_g, "even_sc_conv_b": even_sc_conv_b,
         "even_q_norm_g": even_q_norm_g, "even_kv_norm_g": even_kv_norm_g}
    inv_freq = 1.0 / (ROPE_THETA ** (jnp.arange(0, QK_ROPE, 2, dtype=F32) / QK_ROPE))
    inv_freq = jnp.zeros((1, HEAD_PAD), F32).at[0, QK_NOPE:QK_NOPE + QK_ROPE].set(jnp.tile(inv_freq, 2))
    cos, sin = _rope_tables(positions.reshape(s, 1), inv_freq)

    loss, dx, g = _local_step(x[0], loss_target[0], cos, sin, mod, p, layer_weights, weights_sent[-1], grads_done)

    grads, deltas, new_m, new_v = {}, {}, {}, {}

    def update_layers(n, results, pairs):
        for i in pairs:
            results = _adamw_layer(w[n], [own[n, i], sib[n, i]], m[n], v[n], i, results, n)
        return results

    small_all, small_sum = _gather_sum_all(_pack_rows(g, _SMALL, SMALL_ROWS))
    names = sorted(last)
    kept = [last[n] for n in names]
    in_flight[0] = (names, _rs_start([b.astype(jnp.bfloat16) for b in kept], "rs_start_0", [small_sum]), kept)
    tot = _unpack_small(small_sum)
    dmod_all = small_all[:, :DEPTH * 3 * D_MODEL // 128].reshape(N_DEV, DEPTH, 3 * D_MODEL)
    grads["ada_w"] = _ada_bwd(c_all[:, 0, :].T, _chip_cols(dmod_all, chip).transpose(1, 0, 2))
    grads["ada_b"] = tot["dmod"]
    for n in ("pre_norm_g", "post_norm_g", "even_sc_conv_b", "even_q_norm_g", "even_kv_norm_g"):
        grads[n] = tot[n]
    for n in ("even_sc_conv_w", "odd_conv_w", "odd_conv_b", "odd_ln_g", "odd_ln_b"):
        grads[n] = _chip_cols(tot[n], chip)
    for n in list(grads):
        _, deltas[n], new_m[n], new_v[n] = _adamw(w[n], [grads[n]], m[n], v[n], n)

    for n in ("odd_w_in", "odd_w_out"):
        grads[n], deltas[n], new_m[n], new_v[n] = update_layers(n, None, (1, 0))
    partly = {n: update_layers(n, None, (1,)) for n in ("even_w_in", "even_w_out")}
    land(0, [deltas["ada_w"], deltas["odd_w_in"], partly["even_w_in"][1]])
    for n in ("even_w_in", "even_w_out"):
        grads[n], deltas[n], new_m[n], new_v[n] = update_layers(n, partly[n], (0,))
    uq_parts, ukv_parts = zip(*[[jnp.stack(part) for part in zip(*[_mla_local(q["even_mla", i]) for i in range(N_PAIRS)])]
                                for q in (own, sib)])
    for n, parts in (("even_w_uq", uq_parts), ("even_w_ukv", ukv_parts)):
        grads[n], deltas[n], new_m[n], new_v[n] = _adamw(w[n], list(parts), m[n], v[n], n)

    total_loss = lax.psum(loss[0, 0], ("x", "y", "c"))
    return (total_loss, dx[None], *[grads[n] for n in WEIGHT_NAMES], *[deltas[n] for n in WEIGHT_NAMES],
            *[new_m[n] for n in WEIGHT_NAMES], *[new_v[n] for n in WEIGHT_NAMES])
```

```python
import functools

import jax
import jax.numpy as jnp
from jax import lax
from jax.experimental import pallas as pl
from jax.experimental.pallas import tpu as pltpu

F32 = jnp.float32
MXU_DTYPE = jnp.bfloat16
MESH = pl.DeviceIdType.MESH
VMEM_LIMIT_V7X = 56 * 2 ** 20

EPS = 1e-6
D_MODEL = 1024
DEPTH = 4
CHUNK = 64
SC_WIDTH = 512
SC_KERNEL = 3
SC_HALO = 8
HEADS = 8
QK_NOPE = 64
QK_ROPE = 32
V_HEAD = 64
HEAD_PAD = 128
Q_LORA = 256
KV_LORA = 128
ROPE_THETA = 10000.0
CONF_KERNEL = 31
CONF_HALO = 32
CONV_ROWS = 32
SUBLANES = 8
EVEN_IN = 2976
EVEN_PAD = 3072
ODD_IN = 3072
N_CHIPS = 4
N_DEV = 8
NEG = -1e30

ADAM_LR = 0.001
ADAM_B1 = 0.9
ADAM_B2 = 0.999
ADAM_EPS = 1e-08
ADAM_WD = 0.01
ADAM_STEP = 10

N_PAIRS = DEPTH // 2
EVEN_SHARD = EVEN_IN // N_CHIPS
EVEN_SHARD_PAD = 768
MLA_ROWS = Q_LORA + 2 * KV_LORA


def _cp(n_grid=0, **kw):
    return pltpu.CompilerParams(dimension_semantics=("arbitrary",) * n_grid,
                                vmem_limit_bytes=VMEM_LIMIT_V7X, **kw)


def _sigmoid(x):
    return 1.0 / (1.0 + jnp.exp(-x))


def _silu(x):
    return x * _sigmoid(x)


def _dsilu(x):
    s = _sigmoid(x)
    return s * (1.0 + x * (1.0 - s))


def _rms(x, g):
    return x * lax.rsqrt(jnp.mean(x * x, axis=-1, keepdims=True) + EPS) * g


def _dot(a, b, dims):
    return lax.dot_general(a.astype(MXU_DTYPE), b.astype(MXU_DTYPE), (dims, ((), ())),
                           preferred_element_type=F32)


def _dot_nn(a, b):
    return _dot(a, b, ((1,), (0,)))


def _dot_nt(a, b):
    return _dot(a, b, ((1,), (1,)))


def _dot_tn(a, b):
    return _dot(a, b, ((0,), (0,)))


def _rows(ts, w, cb=0):
    return pl.BlockSpec((ts, w), lambda i: (i, cb))


def _vec(w, cb=0, r=1):
    return pl.BlockSpec((r, w), lambda i: (0, cb))


def _prev_halo(ts, hr, w, cb):
    return pl.BlockSpec((hr, w), lambda i: (jnp.maximum(i * (ts // hr) - 1, 0), cb))


def _next_halo(ts, hr, w, cb, s):
    return pl.BlockSpec((hr, w), lambda i: (jnp.minimum((i + 1) * (ts // hr), s // hr - 1), cb))


def _sds(shape, dtype=F32):
    return jax.ShapeDtypeStruct(shape, dtype)


def _mm(a, b, mode, out_dtype, tm, tn, name):
    tm = min(tm, a.shape[1] if mode == "tn" else a.shape[0])
    tn = min(tn, b.shape[0] if mode == "nt" else b.shape[1])
    if mode == "nn":
        (m, k), n = a.shape, b.shape[1]
        a_spec = pl.BlockSpec((tm, k), lambda i, j: (i, 0))
        b_spec = pl.BlockSpec((k, tn), lambda i, j: (0, j))
        dot = _dot_nn
    elif mode == "nt":
        (m, k), n = a.shape, b.shape[0]
        a_spec = pl.BlockSpec((tm, k), lambda i, j: (i, 0))
        b_spec = pl.BlockSpec((tn, k), lambda i, j: (j, 0))
        dot = _dot_nt
    else:
        (k, m), n = a.shape, b.shape[1]
        a_spec = pl.BlockSpec((k, tm), lambda i, j: (0, i))
        b_spec = pl.BlockSpec((k, tn), lambda i, j: (0, j))
        dot = _dot_tn
    assert m % tm == 0 and n % tn == 0, (name, m, n, tm, tn)

    def body(a_ref, b_ref, o_ref):
        o_ref[...] = dot(a_ref[...], b_ref[...]).astype(o_ref.dtype)

    return pl.pallas_call(
        body, name=name, grid=(m // tm, n // tn), in_specs=[a_spec, b_spec],
        out_specs=pl.BlockSpec((tm, tn), lambda i, j: (i, j)), out_shape=_sds((m, n), out_dtype),
        compiler_params=_cp(2))(a, b)


def _mm_tn_shards(a, b, by, name):
    k, m = a.shape
    n = b.shape[1]
    if by == "cols":
        tm, tn = m, n // N_CHIPS
        shape, grid = (N_CHIPS, m, tn), (1, N_CHIPS)
        out_spec = pl.BlockSpec((1, tm, tn), lambda i, j: (j, i, 0))
    else:
        tm, tn = m // N_CHIPS, n
        shape, grid = (N_CHIPS, tm, n), (N_CHIPS, 1)
        out_spec = pl.BlockSpec((1, tm, tn), lambda i, j: (i, 0, j))

    def body(a_ref, b_ref, o_ref):
        o_ref[0] = _dot_tn(a_ref[...], b_ref[...])

    return pl.pallas_call(
        body, name=name, grid=grid,
        in_specs=[pl.BlockSpec((k, tm), lambda i, j: (0, i)), pl.BlockSpec((k, tn), lambda i, j: (0, j))],
        out_specs=out_spec, out_shape=_sds(shape), compiler_params=_cp(2))(a, b)


def _even_col(q):
    return q if q < 2432 else (q + 64 if q < 2464 else q + 96)


def _shard_pieces(j):
    lo, hi = EVEN_SHARD * j, EVEN_SHARD * (j + 1)
    cuts = [lo] + [b for b in (2432, 2464) if lo < b < hi] + [hi]
    return [(a - lo, _even_col(a), b - a) for a, b in zip(cuts[:-1], cuts[1:])]


def _ein_from_shards(w):
    _, d, _ = w.shape
    tr = 256

    def body(w_ref, o_ref):
        parts, at = [], 0
        for j in range(N_CHIPS):
            for d0, s0, n in _shard_pieces(j):
                if s0 > at:
                    parts.append(jnp.zeros((tr, s0 - at), F32))
                parts.append(w_ref[j, :, d0:d0 + n].astype(F32))
                at = s0 + n
        o_ref[...] = jnp.concatenate(parts, axis=1).astype(o_ref.dtype)

    return pl.pallas_call(
        body, name="ein_from_shards", grid=(d // tr,),
        in_specs=[pl.BlockSpec((N_CHIPS, tr, EVEN_SHARD), lambda i: (0, i, 0))],
        out_specs=_rows(tr, EVEN_PAD), out_shape=_sds((d, EVEN_PAD), w.dtype), compiler_params=_cp(1))(w)


def _ein_to_shards(dw):
    d = dw.shape[0]
    tr = 256

    def body(dw_ref, o_ref):
        for j in range(N_CHIPS):
            parts = [dw_ref[:, s0:s0 + n] for _, s0, n in _shard_pieces(j)]
            o_ref[j] = jnp.concatenate(parts + [jnp.zeros((tr, EVEN_SHARD_PAD - EVEN_SHARD), F32)], axis=1)

    return pl.pallas_call(
        body, name="ein_to_shards", grid=(d // tr,), in_specs=[_rows(tr, EVEN_PAD)],
        out_specs=pl.BlockSpec((N_CHIPS, tr, EVEN_SHARD_PAD), lambda i: (0, i, 0)),
        out_shape=_sds((N_CHIPS, d, EVEN_SHARD_PAD)), compiler_params=_cp(1))(dw)


def _rope_tables(pos_col, invf):
    s = pos_col.shape[0]
    ts = min(512, s)

    def body(p_ref, f_ref, c_ref, s_ref):
        ang = p_ref[...].astype(F32) * f_ref[...]
        lane = lax.broadcasted_iota(jnp.int32, ang.shape, 1)
        rope = (lane >= QK_NOPE) & (lane < QK_NOPE + QK_ROPE)
        c_ref[...] = jnp.where(lane < QK_NOPE, 1.0, jnp.where(rope, jnp.cos(ang), 0.0))
        s_ref[...] = jnp.where(rope, jnp.sin(ang), 0.0)

    return pl.pallas_call(
        body, name="rope_tables", grid=(s // ts,), in_specs=[_rows(ts, 1), _vec(HEAD_PAD)],
        out_specs=[_rows(ts, HEAD_PAD)] * 2, out_shape=[_sds((s, HEAD_PAD))] * 2,
        compiler_params=_cp(1))(pos_col, invf)


def _after(dep):
    return () if dep is None else (dep,)


def _pre_fwd(x, g, mod_l, ts, dep=None):
    s, d = x.shape

    def body(x_ref, g_ref, sh_ref, sc_ref, *rest):
        h = _rms(x_ref[...], g_ref[...]) * (1.0 + sc_ref[...]) + sh_ref[...]
        rest[-1][...] = h.astype(rest[-1].dtype)

    return pl.pallas_call(
        body, name="pre_fwd", grid=(s // ts,),
        in_specs=[_rows(ts, d), _vec(d), _vec(d, 0), _vec(d, 1)] + [_HBM_SPEC] * len(_after(dep)),
        out_specs=_rows(ts, d), out_shape=_sds((s, d), MXU_DTYPE), compiler_params=_cp(1))(
            x, g, mod_l, mod_l, *_after(dep))


def _pre_bwd(dz, w_in, dx_out, x, g, mod_l, ts):
    s, d = x.shape
    n_in = dz.shape[1]

    def f(xv, gv, sh, sc):
        return _rms(xv, gv) * (1.0 + sc) + sh

    def body(dz_ref, w_ref, dxo_ref, x_ref, g_ref, sh_ref, sc_ref, dx_ref, dsh_ref, dsc_ref, dg_ref):
        i = pl.program_id(0)
        _, vjp = jax.vjp(f, x_ref[...], g_ref[...], sh_ref[...], sc_ref[...])
        dx, dg, dsh, dsc = vjp(_dot_nt(dz_ref[...], w_ref[...]))
        dx_ref[...] = dxo_ref[...] + dx

        @pl.when(i == 0)
        def _():
            dsh_ref[...] = jnp.zeros_like(dsh_ref)
            dsc_ref[...] = jnp.zeros_like(dsc_ref)
            dg_ref[...] = jnp.zeros_like(dg_ref)

        dsh_ref[...] += dsh
        dsc_ref[...] += dsc
        dg_ref[...] += dg

    return pl.pallas_call(
        body, name="pre_bwd", grid=(s // ts,),
        in_specs=[_rows(ts, n_in), _vec(n_in, 0, d), _rows(ts, d), _rows(ts, d), _vec(d), _vec(d, 0), _vec(d, 1)],
        out_specs=[_rows(ts, d), _vec(d), _vec(d), _vec(d)],
        out_shape=[_sds((s, d)), _sds((1, d)), _sds((1, d)), _sds((1, d))],
        compiler_params=_cp(1))(dz, w_in, dx_out, x, g, mod_l, mod_l)


def _post_fwd(x, yo, g, mod_l, ts):
    s, d = x.shape

    def body(x_ref, yo_ref, g_ref, gate_ref, o_ref):
        o_ref[...] = x_ref[...] + gate_ref[...] * _rms(yo_ref[...], g_ref[...])

    return pl.pallas_call(
        body, name="post_fwd", grid=(s // ts,),
        in_specs=[_rows(ts, d), _rows(ts, d), _vec(d), _vec(d, 2)],
        out_specs=_rows(ts, d), out_shape=_sds((s, d)), compiler_params=_cp(1))(x, yo, g, mod_l)


def _post_bwd(dx_out, yo, g, mod_l, ts, dep=None):
    s, d = yo.shape

    def f(yov, gv, gate):
        return gate * _rms(yov, gv)

    def body(dx_ref, yo_ref, g_ref, gate_ref, *rest):
        dyo_ref, dgate_ref, dg_ref = rest[-3:]
        i = pl.program_id(0)
        _, vjp = jax.vjp(f, yo_ref[...], g_ref[...], gate_ref[...])
        dyo, dg, dgate = vjp(dx_ref[...])
        dyo_ref[...] = dyo.astype(dyo_ref.dtype)

        @pl.when(i == 0)
        def _():
            dgate_ref[...] = jnp.zeros_like(dgate_ref)
            dg_ref[...] = jnp.zeros_like(dg_ref)

        dgate_ref[...] += dgate
        dg_ref[...] += dg

    return pl.pallas_call(
        body, name="post_bwd", grid=(s // ts,),
        in_specs=[_rows(ts, d), _rows(ts, d), _vec(d), _vec(d, 2)] + [_HBM_SPEC] * len(_after(dep)),
        out_specs=[_rows(ts, d), _vec(d), _vec(d)],
        out_shape=[_sds((s, d), MXU_DTYPE), _sds((1, d)), _sds((1, d))],
        compiler_params=_cp(1))(dx_out, yo, g, mod_l, *_after(dep))


def _loss_fwd_bwd(x, target, ts):
    s, d = x.shape

    def body(x_ref, t_ref, loss_ref, dx_ref):
        i = pl.program_id(0)
        err = x_ref[...] - t_ref[...]
        dx_ref[...] = err * (1.0 / d)

        @pl.when(i == 0)
        def _():
            loss_ref[...] = jnp.zeros_like(loss_ref)

        loss_ref[...] += 0.5 * jnp.sum(jnp.sum(err * err, axis=-1, keepdims=True) * (1.0 / d), axis=0, keepdims=True)

    return pl.pallas_call(
        body, name="loss", grid=(s // ts,), in_specs=[_rows(ts, d), _rows(ts, d)],
        out_specs=[_vec(1), _rows(ts, d)], out_shape=[_sds((1, 1)), _sds((s, d))],
        compiler_params=_cp(1))(x, target)


def _rope(t, cos, sin):
    lane = lax.broadcasted_iota(jnp.int32, t.shape, 1)
    first = (lane >= QK_NOPE) & (lane < QK_NOPE + QK_ROPE // 2)
    second = (lane >= QK_NOPE + QK_ROPE // 2) & (lane < QK_NOPE + QK_ROPE)
    up = pltpu.roll(t, QK_ROPE // 2, 1)
    down = pltpu.roll(t, HEAD_PAD - QK_ROPE // 2, 1)
    return t * cos + jnp.where(first, -down, jnp.where(second, up, 0.0)) * sin


def _rope_transposed(g, cos, sin):
    lane = lax.broadcasted_iota(jnp.int32, g.shape, 1)
    first = (lane >= QK_NOPE) & (lane < QK_NOPE + QK_ROPE // 2)
    second = (lane >= QK_NOPE + QK_ROPE // 2) & (lane < QK_NOPE + QK_ROPE)
    u = g * sin
    up = pltpu.roll(u, QK_ROPE // 2, 1)
    down = pltpu.roll(u, HEAD_PAD - QK_ROPE // 2, 1)
    return g * cos + jnp.where(first, down, jnp.where(second, -up, 0.0))


def _mla_prep_fwd(z, cos, sin, qg, kvg, wq, wq_rot, wuk, wuv, ts):
    s = z.shape[0]
    wide = HEADS * HEAD_PAD

    def body(cq_ref, ckv_ref, kr_ref, cos_ref, sin_ref, qg_ref, kvg_ref, wq_ref, wqr_ref, wuk_ref, wuv_ref,
             q_ref, qt_ref, k_ref, v_ref):
        cos_v, sin_v = cos_ref[...], sin_ref[...]
        cqn = _rms(cq_ref[...], qg_ref[...])
        ckvn = _rms(ckv_ref[...], kvg_ref[...])
        kr = _rope(kr_ref[...], cos_v, sin_v)
        q_lin, q_rot = _dot_nn(cqn, wq_ref[...]), _dot_nn(cqn, wqr_ref[...])
        k_lin, v_all = _dot_nn(ckvn, wuk_ref[...]), _dot_nn(ckvn, wuv_ref[...])
        for h in range(HEADS):
            lanes = slice(h * HEAD_PAD, (h + 1) * HEAD_PAD)
            qh = q_lin[:, lanes] * cos_v + q_rot[:, lanes] * sin_v
            q_ref[h] = qh.astype(q_ref.dtype)
            qt_ref[h, 0] = qh.T.astype(qt_ref.dtype)
            k_ref[h] = (k_lin[:, lanes] + kr).astype(k_ref.dtype)
            v_ref[h] = v_all[:, lanes].astype(v_ref.dtype)

    out = pl.BlockSpec((HEADS, ts, HEAD_PAD), lambda i: (0, i, 0))
    return pl.pallas_call(
        body, name="mla_prep_fwd", grid=(s // ts,),
        in_specs=[_rows(ts, Q_LORA, 8), _rows(ts, KV_LORA, 18), _rows(ts, HEAD_PAD, 19), _rows(ts, HEAD_PAD), _rows(ts, HEAD_PAD),
                  _vec(Q_LORA), _vec(KV_LORA), _vec(wide, 0, Q_LORA), _vec(wide, 0, Q_LORA), _vec(wide, 0, KV_LORA),
                  _vec(wide, 0, KV_LORA)],
        out_specs=[out, pl.BlockSpec((HEADS, 1, HEAD_PAD, ts), lambda i: (0, i, 0, 0)), out, out],
        out_shape=[_sds((HEADS, s, HEAD_PAD), MXU_DTYPE), _sds((HEADS, s // ts, HEAD_PAD, ts), MXU_DTYPE)]
        + [_sds((HEADS, s, HEAD_PAD), MXU_DTYPE)] * 2,
        compiler_params=_cp(1))(z, z, z, cos, sin, qg, kvg, wq, wq_rot, wuk, wuv)


def _mla_prep_bwd(dz, dq, dk, dv, z, cos, sin, qg, kvg, wq, wuk, wuv, ts):
    s = z.shape[0]

    def fq(cq, g):
        return _rms(cq, g)

    def body(dz_in_ref, dq_ref, dk_ref, dv_ref, cq_ref, ckv_ref, cos_ref, sin_ref, qg_ref, kvg_ref, wq_ref, wuk_ref,
             wuv_ref, dz_ref, dw_ref, dqg_ref, dkvg_ref):
        del dz_in_ref
        cos_v, sin_v = cos_ref[...], sin_ref[...]

        @pl.when(pl.program_id(0) == 0)
        def _():
            dw_ref[...] = jnp.zeros_like(dw_ref)
            dqg_ref[...] = jnp.zeros_like(dqg_ref)
            dkvg_ref[...] = jnp.zeros_like(dkvg_ref)

        cqn, vjp_q = jax.vjp(fq, cq_ref[...], qg_ref[...])
        ckvn, vjp_kv = jax.vjp(fq, ckv_ref[...], kvg_ref[...])
        lane = lax.broadcasted_iota(jnp.int32, (ts, HEAD_PAD), 1)
        rope_lanes = (lane >= QK_NOPE) & (lane < QK_NOPE + QK_ROPE)
        dq_lin = jnp.concatenate([_rope_transposed(dq_ref[h], cos_v, sin_v).astype(MXU_DTYPE) for h in range(HEADS)], axis=1)
        dk_all = jnp.concatenate([dk_ref[h].astype(MXU_DTYPE) for h in range(HEADS)], axis=1)
        dv_all = jnp.concatenate([dv_ref[h].astype(MXU_DTYPE) for h in range(HEADS)], axis=1)
        dkr = jnp.where(rope_lanes, dk_ref[0], 0.0)
        for h in range(1, HEADS):
            dkr = dkr + jnp.where(rope_lanes, dk_ref[h], 0.0)
        dcq, dqg = vjp_q(_dot_nt(dq_lin, wq_ref[...]))
        dckv, dkvg = vjp_kv(_dot_nt(dk_all, wuk_ref[...]) + _dot_nt(dv_all, wuv_ref[...]))
        dz_ref[:, 0:Q_LORA] = dcq.astype(dz_ref.dtype)
        dz_ref[:, Q_LORA:Q_LORA + KV_LORA] = dckv.astype(dz_ref.dtype)
        dz_ref[:, Q_LORA + KV_LORA:] = _rope_transposed(dkr, cos_v, sin_v).astype(dz_ref.dtype)
        dqg_ref[...] += dqg
        dkvg_ref[...] += dkvg
        dwq, dwuk, dwuv = _dot_tn(cqn, dq_lin), _dot_tn(ckvn, dk_all), _dot_tn(ckvn, dv_all)
        for h in range(HEADS):
            lanes = slice(h * HEAD_PAD, (h + 1) * HEAD_PAD)
            row0 = (h % 2) * MLA_ROWS
            dw_ref[h // 2, row0:row0 + Q_LORA, :] += dwq[:, lanes]
            dw_ref[h // 2, row0 + Q_LORA:row0 + Q_LORA + KV_LORA, :] += dwuk[:, lanes]
            dw_ref[h // 2, row0 + Q_LORA + KV_LORA:row0 + MLA_ROWS, :] += dwuv[:, lanes]

    wide = HEADS * HEAD_PAD
    heads = pl.BlockSpec((HEADS, ts, HEAD_PAD), lambda i: (0, i, 0))
    whole = pl.BlockSpec((N_CHIPS, 2 * MLA_ROWS, HEAD_PAD), lambda i: (0, 0, 0))
    return pl.pallas_call(
        body, name="mla_prep_bwd", grid=(s // ts,),
        in_specs=[_HBM_SPEC, heads, heads, heads, _rows(ts, Q_LORA, 8), _rows(ts, KV_LORA, 18),
                  _rows(ts, HEAD_PAD), _rows(ts, HEAD_PAD), _vec(Q_LORA), _vec(KV_LORA), _vec(wide, 0, Q_LORA),
                  _vec(wide, 0, KV_LORA), _vec(wide, 0, KV_LORA)],
        out_specs=[_rows(ts, 512, 4), whole, _vec(Q_LORA), _vec(KV_LORA)],
        out_shape=[_sds(dz.shape, dz.dtype), _sds((N_CHIPS, 2 * MLA_ROWS, HEAD_PAD)), _sds((1, Q_LORA)), _sds((1, KV_LORA))],
        input_output_aliases={0: 0}, compiler_params=_cp(1))(dz, dq, dk, dv, z, z, cos, sin, qg, kvg, wq, wuk, wuv)


def _chunk_mask(q0, k0, tq, tk):
    rows = q0 + lax.broadcasted_iota(jnp.int32, (tq, tk), 0)
    cols = k0 + lax.broadcasted_iota(jnp.int32, (tq, tk), 1)
    return lax.shift_right_logical(cols, 6) <= lax.shift_right_logical(rows, 6)


def _attn_fwd(q, k, v, tq):
    s = q.shape[1]
    nq = s // tq
    scale = 1.0 / float(QK_NOPE + QK_ROPE) ** 0.5

    def body(q_ref, k_ref, v_ref, o_ref, lse_ref, m_s, l_s, acc_s):
        qi, hh = pl.program_id(1), pl.program_id(2)
        qv = q_ref[0]
        m_s[...] = jnp.full((tq, 1), NEG, F32)
        l_s[...] = jnp.zeros((tq, 1), F32)
        acc_s[...] = jnp.zeros((tq, HEAD_PAD), F32)

        def step(kj, masked):
            k0 = pl.multiple_of(kj * tq, tq)
            sc = _dot_nt(qv, k_ref[0, pl.ds(k0, tq), :]) * scale
            if masked:
                sc = jnp.where(_chunk_mask(qi * tq, k0, tq, tq), sc, NEG)
            m = m_s[...]
            m_new = jnp.maximum(m, jnp.max(sc, axis=-1, keepdims=True))
            alpha = jnp.exp(m - m_new)
            p = jnp.exp(sc - m_new)
            m_s[...] = m_new
            l_s[...] = alpha * l_s[...] + jnp.sum(p, axis=-1, keepdims=True)
            acc_s[...] = alpha * acc_s[...] + _dot_nn(p, v_ref[0, pl.ds(k0, tq), :])

        @pl.when(qi % 2 == 1)
        def _():
            step(0, False)

        def two(i, c):
            step(qi % 2 + 2 * i, False)
            step(qi % 2 + 2 * i + 1, False)
            return c

        lax.fori_loop(0, qi // 2, two, 0)
        step(qi, True)
        o = acc_s[...] / l_s[...]
        lse_ref[0] = m_s[...] + jnp.log(l_s[...])

        @pl.when(hh == 0)
        def _():
            o_ref[...] = o

        @pl.when(hh == 1)
        def _():
            o_ref[...] += o

    head = lambda hp, qi, hh: 2 * hp + hh
    return pl.pallas_call(
        body, name="attn_fwd", grid=(HEADS // 2, nq, 2),
        in_specs=[pl.BlockSpec((1, tq, HEAD_PAD), lambda hp, qi, hh: (head(hp, qi, hh), qi, 0)),
                  pl.BlockSpec((1, s, HEAD_PAD), lambda hp, qi, hh: (head(hp, qi, hh), 0, 0)),
                  pl.BlockSpec((1, s, HEAD_PAD), lambda hp, qi, hh: (head(hp, qi, hh), 0, 0))],
        out_specs=[pl.BlockSpec((tq, HEAD_PAD), lambda hp, qi, hh: (qi, hp)),
                   pl.BlockSpec((1, tq, 1), lambda hp, qi, hh: (head(hp, qi, hh), qi, 0))],
        out_shape=[_sds((s, HEADS * V_HEAD)), _sds((HEADS, s, 1))],
        scratch_shapes=[pltpu.VMEM((tq, 1), F32), pltpu.VMEM((tq, 1), F32), pltpu.VMEM((tq, HEAD_PAD), F32)],
        compiler_params=_cp(3))(q, k, v)


def _attn_bwd(q, q_t, k, v, do, do_t, o, lse, tq):
    s = q.shape[1]
    nq = s // tq
    per_q = tq // do_t.shape[3]
    scale = 1.0 / float(QK_NOPE + QK_ROPE) ** 0.5

    def body(q_ref, qt_ref, k_ref, v_ref, do_ref, dot_ref, o_ref, lse_ref, dq_ref, dk_ref, dv_ref, dk_t, dv_t):
        hh, kj = pl.program_id(1), pl.program_id(2)

        @pl.when(kj == 0)
        def _():
            dq_ref[...] = jnp.zeros_like(dq_ref)

        kv, vv = k_ref[0], v_ref[0]
        lane = lax.broadcasted_iota(jnp.int32, (tq, HEAD_PAD), 1)
        mine = lax.shift_right_logical(lane, 6) == hh
        dk_t[...] = jnp.zeros_like(dk_t)
        dv_t[...] = jnp.zeros_like(dv_t)

        def step(qi, masked):
            q0 = pl.multiple_of(qi * tq, tq)
            qv = q_ref[0, pl.ds(q0, tq), :]
            dov = do_ref[pl.ds(q0, tq), :]
            delta = jnp.sum(jnp.where(mine, dov * o_ref[pl.ds(q0, tq), :], 0.0), axis=-1, keepdims=True)
            sc = _dot_nt(qv, kv) * scale
            if masked:
                sc = jnp.where(_chunk_mask(q0, kj * tq, tq, tq), sc, NEG)
            p = jnp.exp(sc - lse_ref[0, pl.ds(q0, tq), :])
            ds = (p * (_dot_nt(dov, vv) - delta) * scale).astype(MXU_DTYPE)
            do_tv = jnp.concatenate([dot_ref[0, qi * per_q + r] for r in range(per_q)], axis=1)
            dv_t[...] += _dot_nn(do_tv, p)
            dk_t[...] += _dot_nn(qt_ref[0, qi], ds)
            dq_ref[0, pl.ds(q0, tq), :] += _dot_nn(ds, kv)

        step(kj, True)
        odd = (nq - 1 - kj) % 2

        @pl.when(odd == 1)
        def _():
            step(kj + 1, False)

        def two(i, c):
            step(kj + 1 + odd + 2 * i, False)
            step(kj + 2 + odd + 2 * i, False)
            return c

        lax.fori_loop(0, (nq - 1 - kj) // 2, two, 0)
        dk_ref[0] = dk_t[...].T
        dv_ref[0] = dv_t[...].T

    head = lambda hp, hh, kj: 2 * hp + hh
    full = pl.BlockSpec((1, s, HEAD_PAD), lambda hp, hh, kj: (head(hp, hh, kj), 0, 0))
    blk = pl.BlockSpec((1, tq, HEAD_PAD), lambda hp, hh, kj: (head(hp, hh, kj), kj, 0))
    pair = pl.BlockSpec((s, HEAD_PAD), lambda hp, hh, kj: (0, hp))
    return pl.pallas_call(
        body, name="attn_bwd", grid=(HEADS // 2, 2, nq),
        in_specs=[full, pl.BlockSpec((1,) + q_t.shape[1:], lambda hp, hh, kj: (head(hp, hh, kj), 0, 0, 0)), blk, blk,
                  pair, pl.BlockSpec((1,) + do_t.shape[1:], lambda hp, hh, kj: (hp, 0, 0, 0)), pair,
                  pl.BlockSpec((1, s, 1), lambda hp, hh, kj: (head(hp, hh, kj), 0, 0))],
        out_specs=[full, blk, blk], out_shape=[_sds((HEADS, s, HEAD_PAD))] * 3,
        scratch_shapes=[pltpu.VMEM((HEAD_PAD, tq), F32), pltpu.VMEM((HEAD_PAD, tq), F32)],
        compiler_params=_cp(3))(q, q_t, k, v, do, do_t, o, lse)


def _sc_conv(u, ubuf, w_ref, b_ref, ts):
    return (w_ref[2:3, :] * u + w_ref[1:2, :] * ubuf[pl.ds(SC_HALO - 1, ts), :]
            + w_ref[0:1, :] * ubuf[pl.ds(SC_HALO - 2, ts), :] + b_ref[...])


def _even_gate_fwd(z, o, sc_w, sc_b, ts):
    s = z.shape[0]
    w = SC_WIDTH

    def body(ab_ref, ac_ref, ax_ref, ag_ref, bg_ref, hc_ref, hx_ref, o_ref, w_ref, b_ref, y_ref, ubuf):
        i = pl.program_id(0)
        u = ac_ref[...] * ax_ref[...]
        ubuf[0:SC_HALO, :] = jnp.where(i > 0, hc_ref[...] * hx_ref[...], 0.0)
        ubuf[SC_HALO:, :] = u
        conv = _sc_conv(u, ubuf, w_ref, b_ref, ts)
        y_ref[:, 0:w] = (ab_ref[...] * conv * _silu(ag_ref[...])).astype(y_ref.dtype)
        y_ref[:, w:] = (o_ref[...] * _silu(bg_ref[...])).astype(y_ref.dtype)

    return pl.pallas_call(
        body, name="even_gate_fwd", grid=(s // ts,),
        in_specs=[_rows(ts, w, 0), _rows(ts, w, 1), _rows(ts, w, 2), _rows(ts, w, 3), _rows(ts, w, 5),
                  _prev_halo(ts, SC_HALO, w, 1), _prev_halo(ts, SC_HALO, w, 2), _rows(ts, w),
                  _vec(w, 0, SC_KERNEL), _vec(w)],
        out_specs=_rows(ts, 2 * w), out_shape=_sds((s, 2 * w), MXU_DTYPE),
        scratch_shapes=[pltpu.VMEM((ts + SC_HALO, w), F32)],
        compiler_params=_cp(1))(z, z, z, z, z, z, z, o, sc_w, sc_b)


def _even_gate_bwd(dy, z, o, sc_w, sc_b, ts):
    s = z.shape[0]
    w = SC_WIDTH
    n = s // ts

    def body(dya_ref, dyb_ref, dyan_ref, ab_ref, ac_ref, ax_ref, ag_ref, bg_ref, hc_ref, hx_ref, abn_ref, agn_ref,
             o_ref, w_ref, b_ref, dz_ref, do_ref, dot_ref, dw_ref, db_ref, ubuf, dbuf):
        i = pl.program_id(0)
        ab, ac, ax, ag, bg = ab_ref[...], ac_ref[...], ax_ref[...], ag_ref[...], bg_ref[...]
        dya, dyb = dya_ref[...], dyb_ref[...]
        u = ac * ax
        ubuf[0:SC_HALO, :] = jnp.where(i > 0, hc_ref[...] * hx_ref[...], 0.0)
        ubuf[SC_HALO:, :] = u
        conv = _sc_conv(u, ubuf, w_ref, b_ref, ts)
        sg = _silu(ag)
        dconv = dya * ab * sg
        dbuf[0:ts, :] = dconv
        dbuf[ts:, :] = jnp.where(i < n - 1, dyan_ref[...] * abn_ref[...] * _silu(agn_ref[...]), 0.0)
        du = w_ref[2:3, :] * dconv + w_ref[1:2, :] * dbuf[pl.ds(1, ts), :] + w_ref[0:1, :] * dbuf[pl.ds(2, ts), :]
        dz_ref[:, 0:w] = (dya * conv * sg).astype(dz_ref.dtype)
        dz_ref[:, w:2 * w] = (du * ax).astype(dz_ref.dtype)
        dz_ref[:, 2 * w:3 * w] = (du * ac).astype(dz_ref.dtype)
        dz_ref[:, 3 * w:4 * w] = (dya * ab * conv * _dsilu(ag)).astype(dz_ref.dtype)
        dz_ref[:, 4 * w:5 * w] = jnp.zeros((ts, w), dz_ref.dtype)
        dz_ref[:, 5 * w:] = (dyb * o_ref[...] * _dsilu(bg)).astype(dz_ref.dtype)
        do = dyb * _silu(bg)
        do_ref[...] = do
        for pair in range(HEADS // 2):
            dot_ref[pair, 0] = do[:, pair * HEAD_PAD:(pair + 1) * HEAD_PAD].T.astype(dot_ref.dtype)

        @pl.when(i == 0)
        def _():
            dw_ref[...] = jnp.zeros_like(dw_ref)
            db_ref[...] = jnp.zeros_like(db_ref)

        dw_ref[0:1, :] += jnp.sum(dconv * ubuf[pl.ds(SC_HALO - 2, ts), :], axis=0, keepdims=True)
        dw_ref[1:2, :] += jnp.sum(dconv * ubuf[pl.ds(SC_HALO - 1, ts), :], axis=0, keepdims=True)
        dw_ref[2:3, :] += jnp.sum(dconv * u, axis=0, keepdims=True)
        db_ref[...] += jnp.sum(dconv, axis=0, keepdims=True)

    return pl.pallas_call(
        body, name="even_gate_bwd", grid=(n,),
        in_specs=[_rows(ts, w, 0), _rows(ts, w, 1), _next_halo(ts, SC_HALO, w, 0, s),
                  _rows(ts, w, 0), _rows(ts, w, 1), _rows(ts, w, 2), _rows(ts, w, 3), _rows(ts, w, 5),
                  _prev_halo(ts, SC_HALO, w, 1), _prev_halo(ts, SC_HALO, w, 2),
                  _next_halo(ts, SC_HALO, w, 0, s), _next_halo(ts, SC_HALO, w, 3, s),
                  _rows(ts, w), _vec(w, 0, SC_KERNEL), _vec(w)],
        out_specs=[_rows(ts, EVEN_PAD), _rows(ts, w), pl.BlockSpec((HEADS // 2, 1, HEAD_PAD, ts), lambda i: (0, i, 0, 0)),
                   _vec(w, 0, SC_KERNEL), _vec(w)],
        out_shape=[_sds((s, EVEN_PAD), MXU_DTYPE), _sds((s, w)), _sds((HEADS // 2, n, HEAD_PAD, ts), MXU_DTYPE),
                   _sds((SC_KERNEL, w)), _sds((1, w))],
        scratch_shapes=[pltpu.VMEM((ts + SC_HALO, w), F32), pltpu.VMEM((ts + SC_HALO, w), F32)],
        compiler_params=_cp(1))(dy, dy, dy, z, z, z, z, z, z, z, z, z, o, sc_w, sc_b)


def _ln_act(uc, sg, g, b):
    mu = jnp.mean(uc, axis=-1, keepdims=True)
    var = jnp.mean(jnp.square(uc - mu), axis=-1, keepdims=True)
    return _silu((uc - mu) * lax.rsqrt(var + EPS) * g + b) * _silu(sg)


def _shifted_copies(buf, shifted, rows):
    for b in range(1, SUBLANES):
        shifted[b - 1, 0:rows, :] = buf[pl.ds(b, rows), :]


def _rows_at(buf, shifted, start, n):
    a, b = divmod(start, SUBLANES)
    return buf[pl.ds(SUBLANES * a, n), :] if b == 0 else shifted[b - 1, pl.ds(SUBLANES * a, n), :]


def _odd_fwd(z, conv_w, conv_b, ln_g, ln_b, ts):
    s = z.shape[0]
    d = D_MODEL
    k = CONF_KERNEL

    def body(val_ref, glu_ref, sg_ref, hval_ref, hglu_ref, w_ref, b_ref, g_ref, beta_ref, y_ref, uc_ref, ubuf, ush):
        i = pl.program_id(0)
        ubuf[0:CONF_HALO, :] = jnp.where(i > 0, hval_ref[...] * _sigmoid(hglu_ref[...]), 0.0)
        ubuf[CONF_HALO:, :] = val_ref[...] * _sigmoid(glu_ref[...])
        _shifted_copies(ubuf, ush, ts + CONF_HALO - SUBLANES)
        for r0 in range(0, ts, CONV_ROWS):
            acc = jnp.broadcast_to(b_ref[...], (CONV_ROWS, d))
            for j in range(k):
                acc = acc + w_ref[j:j + 1, :] * _rows_at(ubuf, ush, r0 + CONF_HALO - (k - 1) + j, CONV_ROWS)
            uc_ref[r0:r0 + CONV_ROWS, :] = acc
        y_ref[...] = _ln_act(uc_ref[...], sg_ref[...], g_ref[...], beta_ref[...]).astype(y_ref.dtype)

    return pl.pallas_call(
        body, name="odd_fwd", grid=(s // ts,),
        in_specs=[_rows(ts, d, 0), _rows(ts, d, 1), _rows(ts, d, 2),
                  _prev_halo(ts, CONF_HALO, d, 0), _prev_halo(ts, CONF_HALO, d, 1),
                  _vec(d, 0, k), _vec(d), _vec(d), _vec(d)],
        out_specs=[_rows(ts, d), _rows(ts, d)], out_shape=[_sds((s, d), MXU_DTYPE), _sds((s, d))],
        scratch_shapes=[pltpu.VMEM((ts + CONF_HALO, d), F32),
                        pltpu.VMEM((SUBLANES - 1, ts + CONF_HALO - SUBLANES, d), F32)],
        compiler_params=_cp(1))(z, z, z, z, z, conv_w, conv_b, ln_g, ln_b)


def _odd_bwd(dy, z, uc, conv_w, ln_g, ln_b, ts):
    s = z.shape[0]
    d = D_MODEL
    k = CONF_KERNEL
    n = s // ts

    def body(dy_ref, dyn_ref, val_ref, glu_ref, sg_ref, sgn_ref, uc_ref, ucn_ref,
             w_ref, g_ref, beta_ref, dz_ref, dw_ref, db_ref, dg_ref, dbeta_ref, dbuf, dsh, dw_acc):
        i = pl.program_id(0)
        val, glu = val_ref[...], glu_ref[...]
        sig = _sigmoid(glu)
        u = val * sig
        _, vjp = jax.vjp(_ln_act, uc_ref[...], sg_ref[...], g_ref[...], beta_ref[...])
        duc, dsg, dg, dbeta = vjp(dy_ref[...])
        _, vjp_n = jax.vjp(_ln_act, ucn_ref[...], sgn_ref[...], g_ref[...], beta_ref[...])
        dbuf[0:ts, :] = duc
        dbuf[ts:, :] = jnp.where(i < n - 1, vjp_n(dyn_ref[...])[0], 0.0)
        dz_ref[:, 2 * d:] = dsg.astype(dz_ref.dtype)
        _shifted_copies(dbuf, dsh, ts + CONF_HALO - SUBLANES)

        @pl.when(i == 0)
        def _():
            dw_acc[...] = jnp.zeros_like(dw_acc)
            db_ref[...] = jnp.zeros_like(db_ref)
            dg_ref[...] = jnp.zeros_like(dg_ref)
            dbeta_ref[...] = jnp.zeros_like(dbeta_ref)

        db_ref[...] += jnp.sum(duc, axis=0, keepdims=True)
        dg_ref[...] += dg
        dbeta_ref[...] += dbeta
        for r0 in range(0, ts, CONV_ROWS):
            acc = jnp.zeros((CONV_ROWS, d), F32)
            for j in range(k):
                acc = acc + w_ref[j:j + 1, :] * _rows_at(dbuf, dsh, r0 + (k - 1) - j, CONV_ROWS)
            sig_r = sig[r0:r0 + CONV_ROWS, :]
            dz_ref[r0:r0 + CONV_ROWS, 0:d] = (acc * sig_r).astype(dz_ref.dtype)
            dz_ref[r0:r0 + CONV_ROWS, d:2 * d] = (acc * val[r0:r0 + CONV_ROWS, :] * sig_r * (1.0 - sig_r)).astype(dz_ref.dtype)
        for j in range(k):
            prod = _rows_at(dbuf, dsh, (k - 1) - j, ts) * u
            dw_acc[j] += jnp.sum(prod.reshape(ts // SUBLANES, SUBLANES, d), axis=0)

        @pl.when(i == n - 1)
        def _():
            dw_ref[...] = jnp.sum(dw_acc[...], axis=1)

    return pl.pallas_call(
        body, name="odd_bwd", grid=(n,),
        in_specs=[_rows(ts, d), _next_halo(ts, CONF_HALO, d, 0, s),
                  _rows(ts, d, 0), _rows(ts, d, 1), _rows(ts, d, 2), _next_halo(ts, CONF_HALO, d, 2, s),
                  _rows(ts, d), _next_halo(ts, CONF_HALO, d, 0, s),
                  _vec(d, 0, k), _vec(d), _vec(d)],
        out_specs=[_rows(ts, ODD_IN), _vec(d, 0, k), _vec(d), _vec(d), _vec(d)],
        out_shape=[_sds((s, ODD_IN), MXU_DTYPE), _sds((k, d)), _sds((1, d)), _sds((1, d)), _sds((1, d))],
        scratch_shapes=[pltpu.VMEM((ts + CONF_HALO, d), F32),
                        pltpu.VMEM((SUBLANES - 1, ts + CONF_HALO - SUBLANES, d), F32), pltpu.VMEM((k, SUBLANES, d), F32)],
        compiler_params=_cp(1))(dy, dy, z, z, z, z, uc, uc, conv_w, ln_g, ln_b)


def _local_step(x, target, cos, sin, mod, p, layer_weights, fwd_dep=None, grads_done=None):
    s = x.shape[0]
    tsf, tsb = min(512, s // 2), min(256, s // 2)
    tq = min(512, s // 2)
    row1 = lambda a, i: a[i:i + 1]
    saved = []
    for layer in range(DEPTH):
        i = layer // 2
        mod_l = row1(mod, layer)
        wl = layer_weights(layer, x)
        h = _pre_fwd(x, row1(p["pre_norm_g"], layer), mod_l, tsf, fwd_dep if layer == 0 else None)
        if layer % 2 == 0:
            z = _mm(h, wl["w_in"], "nn", F32, 256, EVEN_PAD, "even_in_fwd")
            q, q_t, k, v = _mla_prep_fwd(z, cos, sin, row1(p["even_q_norm_g"], i), row1(p["even_kv_norm_g"], i),
                                    wl["wq"], wl["wq_rot"], wl["wuk"], wl["wuv"], tsf)
            o, lse = _attn_fwd(q, k, v, tq)
            y = _even_gate_fwd(z, o, wl["sc_conv_w"], row1(p["even_sc_conv_b"], i), tsf)
            yo = _mm(y, wl["w_out"], "nn", F32, 512, 1024, "even_out_fwd")
            saved.append((x, h, z, y, yo, wl, (q, q_t, k, v, o, lse)))
        else:
            z = _mm(h, wl["w_in"], "nn", F32, 256, ODD_IN, "odd_in_fwd")
            y, uc = _odd_fwd(z, wl["conv_w"], wl["conv_b"], wl["ln_g"], wl["ln_b"], tsf)
            yo = _mm(y, wl["w_out"], "nn", F32, 512, 1024, "odd_out_fwd")
            saved.append((x, h, z, y, yo, wl, uc))
        x = _post_fwd(x, yo, row1(p["post_norm_g"], layer), mod_l, tsf)

    loss, dx = _loss_fwd_bwd(x, target, tsf)

    g = {n: [None] * (DEPTH if n in ("pre_norm_g", "post_norm_g") else N_PAIRS) for n in (
        "pre_norm_g", "post_norm_g", "even_sc_conv_w", "even_sc_conv_b", "even_q_norm_g", "even_kv_norm_g",
        "odd_conv_w", "odd_conv_b", "odd_ln_g", "odd_ln_b")}
    dmod = [None] * DEPTH
    dep = None
    for layer in reversed(range(DEPTH)):
        i = layer // 2
        mod_l = row1(mod, layer)
        x_in, h, z, y, yo, wl, extra = saved[layer]
        dyo, dgate, g["post_norm_g"][layer] = _post_bwd(dx, yo, row1(p["post_norm_g"], layer), mod_l, tsb, dep)
        bufs = {}
        if layer % 2 == 0:
            q, q_t, k, v, o, lse = extra
            dy = _mm(dyo, wl["w_out"], "nt", F32, 512, 1024, "even_out_bwd_x")
            bufs["even_w_out"] = _mm_tn_shards(y, dyo, "rows", "even_out_bwd_w")
            dz, do, do_t, g["even_sc_conv_w"][i], g["even_sc_conv_b"][i] = _even_gate_bwd(
                dy, z, o, wl["sc_conv_w"], row1(p["even_sc_conv_b"], i), tsb)
            dq, dk, dv = _attn_bwd(q, q_t, k, v, do, do_t, o, lse, tq)
            dz, bufs["even_mla"], g["even_q_norm_g"][i], g["even_kv_norm_g"][i] = _mla_prep_bwd(
                dz, dq, dk, dv, z, cos, sin, row1(p["even_q_norm_g"], i), row1(p["even_kv_norm_g"], i),
                wl["wq"], wl["wuk"], wl["wuv"], tsb)
            bufs["even_w_in"] = _ein_to_shards(_mm(h, dz, "tn", F32, D_MODEL, 512, "even_in_bwd_w"))
        else:
            uc = extra
            dy = _mm(dyo, wl["w_out"], "nt", F32, 512, 1024, "odd_out_bwd_x")
            bufs["odd_w_out"] = _mm_tn_shards(y, dyo, "rows", "odd_out_bwd_w")
            dz, g["odd_conv_w"][i], g["odd_conv_b"][i], g["odd_ln_g"][i], g["odd_ln_b"][i] = _odd_bwd(
                dy, z, uc, wl["conv_w"], wl["ln_g"], wl["ln_b"], tsb)
            bufs["odd_w_in"] = _mm_tn_shards(h, dz, "cols", "odd_in_bwd_w")
        dx, dshift, dscale, g["pre_norm_g"][layer] = _pre_bwd(
            dz, wl["w_in"], dx, x_in, row1(p["pre_norm_g"], layer), mod_l, tsb)
        dmod[layer] = jnp.concatenate([dshift, dscale, dgate], axis=-1)
        dep = grads_done(layer, bufs, dx) if grads_done is not None else None
    stack = lambda parts: jnp.stack([a[0] if a.shape[0] == 1 and a.ndim == 2 else a for a in parts])
    small = {n: stack(parts) for n, parts in g.items()}
    small["dmod"] = jnp.concatenate(dmod, axis=0)
    return loss, dx, small


def _uq_to_heads(w):
    w = w.reshape(N_CHIPS, Q_LORA, 2, QK_NOPE + QK_ROPE).transpose(0, 2, 1, 3).reshape(HEADS, Q_LORA, QK_NOPE + QK_ROPE)
    half = QK_ROPE // 2
    rotated = jnp.concatenate([jnp.zeros_like(w[..., :QK_NOPE]), -w[..., QK_NOPE + half:], w[..., QK_NOPE:QK_NOPE + half]],
                              axis=-1)
    pad = ((0, 0), (0, 0), (0, HEAD_PAD - QK_NOPE - QK_ROPE))
    return _side_by_side(jnp.pad(w, pad)), _side_by_side(jnp.pad(rotated, pad))


def _side_by_side(w):
    return w.transpose(1, 0, 2).reshape(w.shape[1], HEADS * HEAD_PAD)


def _ukv_to_heads(w):
    w = w.reshape(N_CHIPS, KV_LORA, 2, QK_NOPE + V_HEAD).transpose(0, 2, 1, 3).reshape(HEADS, KV_LORA, QK_NOPE + V_HEAD)
    wk = jnp.pad(w[..., :QK_NOPE], ((0, 0), (0, 0), (0, HEAD_PAD - QK_NOPE)))
    wv = w[..., QK_NOPE:]
    zero = jnp.zeros_like(wv)
    odd = (jnp.arange(HEADS) % 2 == 1)[:, None, None]
    wv = jnp.concatenate([jnp.where(odd, zero, wv), jnp.where(odd, wv, zero)], axis=-1)
    return _side_by_side(wk), _side_by_side(wv)


def _mla_local(q):
    blocks = q.reshape(2, MLA_ROWS, HEAD_PAD)
    uq = jnp.concatenate([blocks[r, :Q_LORA, :QK_NOPE + QK_ROPE] for r in range(2)], axis=-1)
    ukv = jnp.concatenate(
        [jnp.concatenate([blocks[r, Q_LORA:Q_LORA + KV_LORA, :QK_NOPE],
                          blocks[r, Q_LORA + KV_LORA:, V_HEAD * r:V_HEAD * (r + 1)]], axis=-1) for r in range(2)], axis=-1)
    return uq, ukv


def _place():
    return lax.axis_index("x"), lax.axis_index("y"), lax.axis_index("c")


def _flip(v, bit):
    return 1 - v if bit else v


def _sem(a, k):
    return a * (N_CHIPS - 1) + k - 1


def _remote(src, dst, send_sem, recv_sem, peer):
    return pltpu.make_async_remote_copy(src_ref=src, dst_ref=dst, send_sem=send_sem, recv_sem=recv_sem,
                                        device_id=peer, device_id_type=MESH)


_VMEM_SPEC = pl.BlockSpec(memory_space=pltpu.VMEM)
_HBM_SPEC = pl.BlockSpec(memory_space=pl.ANY)


def _ada_fwd(c8, ada_w, ada_b_sh):
    depth, d, cols = ada_w.shape

    def body(c_ref, w_ref, b_ref, call_ref, mod_ref, s1, r1, s2, r2):
        x, y, c = _place()
        chip = 2 * x + y
        me = 2 * chip + c
        call_ref[me] = c_ref[...]
        sends = []
        for k in range(1, N_DEV):
            peer = (_flip(x, k & 4), _flip(y, k & 2), _flip(c, k & 1))
            cp = _remote(c_ref, call_ref.at[me], s1.at[k - 1], r1.at[k - 1], peer)
            cp.start()
            sends.append(cp)
        for k in range(1, N_DEV):
            src = 4 * _flip(x, k & 4) + 2 * _flip(y, k & 2) + _flip(c, k & 1)
            _remote(c_ref, call_ref.at[src], s1.at[k - 1], r1.at[k - 1], (x, y, c)).wait_recv()
        act = _silu(jnp.concatenate([call_ref[e, 0:1, :] for e in range(N_DEV)], axis=0))
        for l in range(depth):
            mod_ref[chip, l] = _dot_nn(act, w_ref[l]) + b_ref[l:l + 1, :]
        for k in range(1, N_CHIPS):
            peer = (_flip(x, k & 2), _flip(y, k & 1), c)
            cp = _remote(mod_ref.at[chip], mod_ref.at[chip], s2.at[k - 1], r2.at[k - 1], peer)
            cp.start()
            sends.append(cp)
        for k in range(1, N_CHIPS):
            src = 2 * _flip(x, k & 2) + _flip(y, k & 1)
            _remote(mod_ref.at[src], mod_ref.at[src], s2.at[k - 1], r2.at[k - 1], (x, y, c)).wait_recv()
        for cp in sends:
            cp.wait_send()

    return pl.pallas_call(
        body, name="ada_fwd", in_specs=[_VMEM_SPEC] * 3, out_specs=[_VMEM_SPEC] * 2,
        out_shape=[_sds((N_DEV, 8, d)), _sds((N_CHIPS, depth, N_DEV, cols))],
        scratch_shapes=[pltpu.SemaphoreType.DMA((N_DEV - 1,)), pltpu.SemaphoreType.DMA((N_DEV - 1,)),
                        pltpu.SemaphoreType.DMA((N_CHIPS - 1,)), pltpu.SemaphoreType.DMA((N_CHIPS - 1,))],
        compiler_params=pltpu.CompilerParams(vmem_limit_bytes=VMEM_LIMIT_V7X))(c8, ada_w, ada_b_sh)


def _ada_bwd(c_t, dmod_sh):
    depth, n, cols = dmod_sh.shape
    d = c_t.shape[0]
    tr = 256

    def body(c_ref, dm_ref, o_ref):
        act = _silu(c_ref[...])
        acc = act[:, 0:1] * dm_ref[0, 0:1, :]
        for e in range(1, n):
            acc = acc + act[:, e:e + 1] * dm_ref[0, e:e + 1, :]
        o_ref[0] = acc

    return pl.pallas_call(
        body, name="ada_bwd", grid=(depth, d // tr),
        in_specs=[pl.BlockSpec((tr, n), lambda l, i: (i, 0)), pl.BlockSpec((1, n, cols), lambda l, i: (l, 0, 0))],
        out_specs=pl.BlockSpec((1, tr, cols), lambda l, i: (l, i, 0)), out_shape=_sds((depth, d, cols)),
        compiler_params=_cp(2))(c_t, dmod_sh)


def _gathered_shape(shape, how):
    if how == "slot":
        return (N_CHIPS,) + shape
    r, cc = shape
    return (r, N_CHIPS * cc) if how == "cols" else (N_CHIPS * r, cc)


def _gathered_part(ref, shape, how, chip):
    if how == "slot":
        return ref.at[chip]
    if how == "cols":
        return ref.at[:, pl.ds(pl.multiple_of(chip * shape[1], 128), shape[1])]
    return ref.at[pl.ds(pl.multiple_of(chip * shape[0], 8), shape[0]), :]


_SEM_SPEC = pl.BlockSpec(memory_space=pltpu.SEMAPHORE)
_TOKEN = jax.ShapeDtypeStruct((8, 128), F32)
_SPLIT_COPY = pltpu.CompilerParams(has_side_effects=pltpu.SideEffectType.DATAFLOW_SIDE_EFFECTING)


def _in_hbm(a):
    return pltpu.with_memory_space_constraint(a, pltpu.HBM)


def _gather_start(items, gathered, name, after=()):
    n = len(items)

    def body(*refs):
        ins, outs = refs[:n], refs[n:2 * n]
        send_sems, recv_sems = refs[2 * n + len(after)], refs[2 * n + len(after) + 1]
        x, y, c = _place()
        for a in range(n):
            for k in range(1, N_CHIPS):
                part = _gathered_part(outs[a], items[a][0].shape, items[a][1], 2 * x + y)
                _remote(ins[a], part, send_sems.at[_sem(a, k)], recv_sems.at[_sem(a, k)],
                        (_flip(x, k & 2), _flip(y, k & 1), c)).start()
        refs[-1][...] = jnp.zeros(_TOKEN.shape, _TOKEN.dtype)

    arrays = [_in_hbm(a) for a, _ in items] + [_in_hbm(a) for a in gathered]
    res = pl.pallas_call(
        body, name=name, in_specs=[_HBM_SPEC] * (2 * n + len(after)),
        out_specs=[_SEM_SPEC, _SEM_SPEC] + [_HBM_SPEC] * (2 * n) + [_VMEM_SPEC],
        out_shape=[pltpu.SemaphoreType.DMA((n * (N_CHIPS - 1),)), pltpu.SemaphoreType.DMA((n * (N_CHIPS - 1),))]
        + [pltpu.HBM(a.shape, a.dtype) for a in arrays] + [_TOKEN],
        input_output_aliases={a: 2 + a for a in range(2 * n)}, compiler_params=_SPLIT_COPY)(*arrays, *after)
    return res[0], res[1], res[2:2 + n], res[2 + n:2 + 2 * n], res[-1]


def _gather_wait(items, started, after, name):
    n = len(items)
    send_sems, recv_sems, shards, gathered, _ = started

    def body(*refs):
        ins, outs, send_sems, recv_sems = refs[:n], refs[n:2 * n], refs[2 * n], refs[2 * n + 1]
        x, y, c = _place()
        for a in range(n):
            for k in range(1, N_CHIPS):
                part = _gathered_part(outs[a], items[a][0].shape, items[a][1], 2 * _flip(x, k & 2) + _flip(y, k & 1))
                cp = _remote(ins[a], part, send_sems.at[_sem(a, k)], recv_sems.at[_sem(a, k)], (x, y, c))
                cp.wait_send()
                cp.wait_recv()

    res = pl.pallas_call(
        body, name=name, in_specs=[_HBM_SPEC] * (2 * n) + [_SEM_SPEC, _SEM_SPEC] + [_HBM_SPEC] * len(after),
        out_specs=[_HBM_SPEC] * (2 * n), out_shape=[pltpu.HBM(a.shape, a.dtype) for a in (*shards, *gathered)],
        input_output_aliases={a: a for a in range(2 * n)}, compiler_params=_SPLIT_COPY)(
            *shards, *gathered, send_sems, recv_sems, *after)
    return res[n:]


def _rs_start(bufs, name, after=()):
    n = len(bufs)

    def body(*refs):
        srcs, lands = refs[:n], refs[n:2 * n]
        send_sems, recv_sems = refs[2 * n + len(after)], refs[2 * n + len(after) + 1]
        x, y, c = _place()
        for a in range(n):
            for k in range(1, N_CHIPS):
                tx, ty = _flip(x, k & 2), _flip(y, k & 1)
                _remote(srcs[a].at[2 * tx + ty], lands[a].at[k - 1], send_sems.at[_sem(a, k)], recv_sems.at[_sem(a, k)],
                        (tx, ty, c)).start()
        refs[-1][...] = jnp.zeros(_TOKEN.shape, _TOKEN.dtype)

    arrays = [_in_hbm(b) for b in bufs] + [_in_hbm(lax.empty((N_CHIPS - 1,) + b.shape[1:], b.dtype)) for b in bufs]
    res = pl.pallas_call(
        body, name=name, in_specs=[_HBM_SPEC] * (2 * n + len(after)),
        out_specs=[_SEM_SPEC, _SEM_SPEC] + [_HBM_SPEC] * (2 * n) + [_VMEM_SPEC],
        out_shape=[pltpu.SemaphoreType.DMA((n * (N_CHIPS - 1),)), pltpu.SemaphoreType.DMA((n * (N_CHIPS - 1),))]
        + [pltpu.HBM(a.shape, a.dtype) for a in arrays] + [_TOKEN],
        input_output_aliases={a: 2 + a for a in range(2 * n)}, compiler_params=_SPLIT_COPY)(*arrays, *after)
    return res[0], res[1], res[2:2 + n], res[2 + n:2 + 2 * n], res[-1]


def _rs_wait(started, after, name):
    send_sems, recv_sems, bufs, lands, _ = started
    n = len(bufs)

    def body(*refs):
        srcs, lnds, send_sems, recv_sems = refs[:n], refs[n:2 * n], refs[2 * n], refs[2 * n + 1]
        x, y, c = _place()
        for a in range(n):
            for k in range(1, N_CHIPS):
                cp = _remote(srcs[a].at[0], lnds[a].at[k - 1], send_sems.at[_sem(a, k)], recv_sems.at[_sem(a, k)], (x, y, c))
                cp.wait_send()
                cp.wait_recv()

    res = pl.pallas_call(
        body, name=name, in_specs=[_HBM_SPEC] * (2 * n) + [_SEM_SPEC, _SEM_SPEC] + [_HBM_SPEC] * len(after),
        out_specs=[_HBM_SPEC] * (2 * n), out_shape=[pltpu.HBM(a.shape, a.dtype) for a in (*bufs, *lands)],
        input_output_aliases={a: a for a in range(2 * n)}, compiler_params=_SPLIT_COPY)(
            *bufs, *lands, send_sems, recv_sems, *after)
    return res[:n], res[n:]


def _place_own(shard, how, chip_idx):
    r, cc = shard.shape
    block, index = {"slot": ((1, r, cc), lambda i, c: (c[0], 0, 0)), "cols": ((r, cc), lambda i, c: (0, c[0])),
                    "rows": ((r, cc), lambda i, c: (c[0], 0))}[how]

    def body(c_ref, in_ref, o_ref):
        del c_ref
        o_ref[...] = in_ref[...].reshape(o_ref.shape)

    return pl.pallas_call(
        body, name="place_own", out_shape=_sds(_gathered_shape(shard.shape, how), shard.dtype),
        grid_spec=pltpu.PrefetchScalarGridSpec(
            num_scalar_prefetch=1, grid=(1,), in_specs=[pl.BlockSpec((r, cc), lambda i, c: (0, 0))],
            out_specs=pl.BlockSpec(block, index)),
        compiler_params=_cp(1))(chip_idx, shard)


def _gather_chips(items, after=()):
    n = n_all = len(items)
    arrays = [a for a, _ in items]

    def body(*refs):
        ins, outs = refs[:n_all], refs[-n_all - 3:-3]
        send_sems, recv_sems, local_sems = refs[-3:]
        x, y, c = _place()
        chip = 2 * x + y
        part = lambda a, j: _gathered_part(outs[a], items[a][0].shape, items[a][1], j)
        local = [pltpu.make_async_copy(ins[a], part(a, chip), local_sems.at[a]) for a in range(n_all)]
        for cp in local[:n]:
            cp.start()
        sends = []
        for a in range(n):
            for k in range(1, N_CHIPS):
                peer = (_flip(x, k & 2), _flip(y, k & 1), c)
                cp = _remote(ins[a], part(a, chip), send_sems.at[_sem(a, k)], recv_sems.at[_sem(a, k)], peer)
                cp.start()
                sends.append(cp)
        for cp in local[n:]:
            cp.start()
        for a in range(n):
            for k in range(1, N_CHIPS):
                src = 2 * _flip(x, k & 2) + _flip(y, k & 1)
                _remote(ins[a], part(a, src), send_sems.at[_sem(a, k)], recv_sems.at[_sem(a, k)], (x, y, c)).wait_recv()
        for cp in sends:
            cp.wait_send()
        for cp in local:
            cp.wait()

    return pl.pallas_call(
        body, name="gather_chips", in_specs=[_HBM_SPEC] * (n_all + len(after)), out_specs=[_HBM_SPEC] * n_all,
        out_shape=[_sds(_gathered_shape(a.shape, how), a.dtype) for a, how in items],
        scratch_shapes=[pltpu.SemaphoreType.DMA((n * (N_CHIPS - 1),)), pltpu.SemaphoreType.DMA((n * (N_CHIPS - 1),)),
                        pltpu.SemaphoreType.DMA((n_all,))])(*arrays, *after)


def _gather_sum_all(small):
    r, w = small.shape

    def body(in_ref, all_ref, sum_ref, send_sems, recv_sems):
        x, y, c = _place()
        me = 4 * x + 2 * y + c
        all_ref[me] = in_ref[...]
        sends = []
        for k in range(1, N_DEV):
            peer = (_flip(x, k & 4), _flip(y, k & 2), _flip(c, k & 1))
            cp = _remote(in_ref, all_ref.at[me], send_sems.at[k - 1], recv_sems.at[k - 1], peer)
            cp.start()
            sends.append(cp)
        for k in range(1, N_DEV):
            src = 4 * _flip(x, k & 4) + 2 * _flip(y, k & 2) + _flip(c, k & 1)
            _remote(in_ref, all_ref.at[src], send_sems.at[k - 1], recv_sems.at[k - 1], (x, y, c)).wait_recv()
        acc = all_ref[0]
        for e in range(1, N_DEV):
            acc = acc + all_ref[e]
        sum_ref[...] = acc
        for cp in sends:
            cp.wait_send()

    return pl.pallas_call(
        body, name="gather_sum_all", in_specs=[_VMEM_SPEC], out_specs=[_VMEM_SPEC] * 2,
        out_shape=[_sds((N_DEV, r, w)), _sds((r, w))],
        scratch_shapes=[pltpu.SemaphoreType.DMA((N_DEV - 1,)), pltpu.SemaphoreType.DMA((N_DEV - 1,))],
        compiler_params=pltpu.CompilerParams(vmem_limit_bytes=VMEM_LIMIT_V7X))(small)


def _add_chips(buf, t, chip_idx):
    r, cc = buf.shape[1:]
    tr = min(256, r)

    def body(c_ref, p_ref, t_ref, o_ref):
        del c_ref
        o_ref[...] = p_ref[0] + t_ref[0].astype(F32) + t_ref[1].astype(F32) + t_ref[2].astype(F32)

    return pl.pallas_call(
        body, name="add_chips", out_shape=_sds((r, cc)),
        grid_spec=pltpu.PrefetchScalarGridSpec(
            num_scalar_prefetch=1, grid=(r // tr,),
            in_specs=[pl.BlockSpec((1, tr, cc), lambda i, c: (c[0], i, 0)),
                      pl.BlockSpec((N_CHIPS - 1, tr, cc), lambda i, c: (0, i, 0))],
            out_specs=pl.BlockSpec((tr, cc), lambda i, c: (i, 0))),
        compiler_params=_cp(1))(chip_idx, buf, t)


def _rs_sibling(qs):
    n = len(qs)

    def body(*refs):
        ins, outs = refs[:n], refs[n:2 * n]
        send_sems, recv_sems = refs[2 * n:]
        x, y, c = _place()
        copies = [_remote(ins[a], outs[a], send_sems.at[a], recv_sems.at[a], (x, y, 1 - c)) for a in range(n)]
        for cp in copies:
            cp.start()
        for cp in copies:
            cp.wait()

    return pl.pallas_call(
        body, name="rs_sibling", in_specs=[_HBM_SPEC] * n, out_specs=[_HBM_SPEC] * n,
        out_shape=[_sds(q.shape) for q in qs],
        scratch_shapes=[pltpu.SemaphoreType.DMA((n,)), pltpu.SemaphoreType.DMA((n,))])(*qs)


def _adamw_update(w, g, m, v):
    m = ADAM_B1 * m + (1.0 - ADAM_B1) * g
    v = ADAM_B2 * v + (1.0 - ADAM_B2) * jnp.square(g)
    m_hat = m / (1.0 - ADAM_B1 ** ADAM_STEP)
    v_hat = v / (1.0 - ADAM_B2 ** ADAM_STEP)
    return -ADAM_LR * (m_hat / (jnp.sqrt(v_hat) + ADAM_EPS) + ADAM_WD * w), m, v


def _adamw(w, g_parts, m, v, name):
    shape = w.shape
    cols = shape[-1]
    rows = _size(shape[:-1])
    tr = 512 if rows % 512 == 0 else rows
    spec = pl.BlockSpec((tr, cols), lambda i: (i, 0))
    n = len(g_parts)

    def body(*refs):
        w_ref, m_ref, v_ref = refs[:3]
        g_ref, d_ref, nm_ref, nv_ref = refs[3 + n:]
        g = refs[3][...]
        for r in refs[4:3 + n]:
            g = g + r[...]
        g_ref[...] = g
        d_ref[...], nm_ref[...], nv_ref[...] = _adamw_update(w_ref[...], g, m_ref[...], v_ref[...])

    outs = pl.pallas_call(
        body, name="adamw_" + name, grid=(rows // tr,), in_specs=[spec] * (3 + n), out_specs=[spec] * 4,
        out_shape=[_sds((rows, cols))] * 4, compiler_params=_cp(1))(
            *[a.reshape(rows, cols) for a in (w, m, v, *g_parts)])
    return tuple(o.reshape(shape) for o in outs)


def _adamw_layer(w, g_parts, m, v, layer, prev, name):
    _, r, cc = w.shape
    tr = 512 if r % 512 == 0 else r
    spec = pl.BlockSpec((1, tr, cc), lambda i: (layer, i, 0))
    n = len(g_parts)

    def body(*refs):
        w_ref, m_ref, v_ref = refs[:3]
        g_ref, d_ref, nm_ref, nv_ref = refs[-4:]
        g = refs[3][...]
        for q in refs[4:3 + n]:
            g = g + q[...]
        g = g[:, :cc]
        g_ref[0] = g
        d_ref[0], nm_ref[0], nv_ref[0] = _adamw_update(w_ref[0], g, m_ref[0], v_ref[0])

    g_specs = [pl.BlockSpec((tr, q.shape[1]), lambda i: (i, 0)) for q in g_parts]
    passed = () if prev is None else tuple(prev)
    return pl.pallas_call(
        body, name="adamw_" + name, grid=(r // tr,),
        in_specs=[spec] * 3 + g_specs + [_HBM_SPEC] * len(passed), out_specs=[spec] * 4,
        out_shape=[_sds(w.shape)] * 4, input_output_aliases={3 + n + k: k for k in range(len(passed))},
        compiler_params=_cp(1))(w, m, v, *g_parts, *passed)


def _size(shape):
    n = 1
    for s in shape:
        n *= s
    return n


_SMALL = (("dmod", (DEPTH, 3 * D_MODEL)), ("pre_norm_g", (DEPTH, D_MODEL)), ("post_norm_g", (DEPTH, D_MODEL)),
          ("even_sc_conv_w", (2, SC_KERNEL, SC_WIDTH)), ("even_sc_conv_b", (2, SC_WIDTH)),
          ("even_q_norm_g", (2, Q_LORA)), ("even_kv_norm_g", (2, KV_LORA)),
          ("odd_conv_w", (2, CONF_KERNEL, D_MODEL)), ("odd_conv_b", (2, D_MODEL)), ("odd_ln_g", (2, D_MODEL)),
          ("odd_ln_b", (2, D_MODEL)))
SMALL_ROWS = -(-sum(_size(s) for _, s in _SMALL) // (8 * 128)) * 8

_SMALL_W = (("even_sc_conv_w", (2, SC_KERNEL, SC_WIDTH // N_CHIPS)), ("odd_conv_w", (2, CONF_KERNEL, D_MODEL // N_CHIPS)),
            ("odd_conv_b", (2, D_MODEL // N_CHIPS)), ("odd_ln_g", (2, D_MODEL // N_CHIPS)),
            ("odd_ln_b", (2, D_MODEL // N_CHIPS)))
SMALL_W_ROWS = -(-sum(_size(s) for _, s in _SMALL_W) // (8 * 128)) * 8


def _pack_rows(arrays, layout, rows):
    flat = jnp.concatenate([arrays[n].reshape(-1) for n, _ in layout])
    return jnp.pad(flat, (0, rows * 128 - flat.shape[0])).reshape(rows, 128)


def _unpack_small(t):
    flat = t.reshape(-1)
    out, at = {}, 0
    for n, shape in _SMALL:
        out[n] = flat[at:at + _size(shape)].reshape(shape)
        at += _size(shape)
    return out


def _unpack_small_w(t):
    flat = t.reshape(N_CHIPS, -1)
    out, at = {}, 0
    for n, shape in _SMALL_W:
        a = flat[:, at:at + _size(shape)].reshape((N_CHIPS,) + shape)
        out[n] = jnp.moveaxis(a, 0, -2).reshape(shape[:-1] + (N_CHIPS * shape[-1],))
        at += _size(shape)
    return out


def _chip_cols(a, chip):
    n = a.shape[-1] // N_CHIPS
    return lax.dynamic_slice_in_dim(a, chip * n, n, axis=a.ndim - 1)


def _join_cols(a):
    _, l, r, cc = a.shape
    return a.transpose(1, 2, 0, 3).reshape(l, r, N_CHIPS * cc)


WEIGHT_NAMES = ("ada_w", "ada_b", "pre_norm_g", "post_norm_g", "even_w_in", "even_sc_conv_w", "even_sc_conv_b",
                "even_q_norm_g", "even_kv_norm_g", "even_w_uq", "even_w_ukv", "even_w_out", "odd_w_in", "odd_conv_w",
                "odd_conv_b", "odd_ln_g", "odd_ln_b", "odd_w_out")
GATHER_HOW = ((("even_w_in", "slot"), ("even_w_uq", "slot"), ("even_w_ukv", "slot"), ("even_w_out", "rows")),
              (("odd_w_in", "cols"), ("odd_w_out", "rows")))


def kernel(x, c, positions, ada_w, ada_b, pre_norm_g, post_norm_g, even_w_in, even_sc_conv_w, even_sc_conv_b, even_q_norm_g, even_kv_norm_g, even_w_uq, even_w_ukv, even_w_out, odd_w_in, odd_conv_w, odd_conv_b, odd_ln_g, odd_ln_b, odd_w_out, loss_target, m_ada_w, m_ada_b, m_pre_norm_g, m_post_norm_g, m_even_w_in, m_even_sc_conv_w, m_even_sc_conv_b, m_even_q_norm_g, m_even_kv_norm_g, m_even_w_uq, m_even_w_ukv, m_even_w_out, m_odd_w_in, m_odd_conv_w, m_odd_conv_b, m_odd_ln_g, m_odd_ln_b, m_odd_w_out, v_ada_w, v_ada_b, v_pre_norm_g, v_post_norm_g, v_even_w_in, v_even_sc_conv_w, v_even_sc_conv_b, v_even_q_norm_g, v_even_kv_norm_g, v_even_w_uq, v_even_w_ukv, v_even_w_out, v_odd_w_in, v_odd_conv_w, v_odd_conv_b, v_odd_ln_g, v_odd_ln_b, v_odd_w_out):
    w = dict(zip(WEIGHT_NAMES, (ada_w, ada_b, pre_norm_g, post_norm_g, even_w_in, even_sc_conv_w, even_sc_conv_b,
                                even_q_norm_g, even_kv_norm_g, even_w_uq, even_w_ukv, even_w_out, odd_w_in, odd_conv_w,
                                odd_conv_b, odd_ln_g, odd_ln_b, odd_w_out)))
    m = dict(zip(WEIGHT_NAMES, (m_ada_w, m_ada_b, m_pre_norm_g, m_post_norm_g, m_even_w_in, m_even_sc_conv_w,
                                m_even_sc_conv_b, m_even_q_norm_g, m_even_kv_norm_g, m_even_w_uq, m_even_w_ukv,
                                m_even_w_out, m_odd_w_in, m_odd_conv_w, m_odd_conv_b, m_odd_ln_g, m_odd_ln_b, m_odd_w_out)))
    v = dict(zip(WEIGHT_NAMES, (v_ada_w, v_ada_b, v_pre_norm_g, v_post_norm_g, v_even_w_in, v_even_sc_conv_w,
                                v_even_sc_conv_b, v_even_q_norm_g, v_even_kv_norm_g, v_even_w_uq, v_even_w_ukv,
                                v_even_w_out, v_odd_w_in, v_odd_conv_w, v_odd_conv_b, v_odd_ln_g, v_odd_ln_b, v_odd_w_out)))
    ix, iy, ic = _place()
    chip = 2 * ix + iy
    me = 2 * chip + ic
    s = x.shape[1]

    c_all, mod_all = _ada_fwd(jnp.broadcast_to(c, (8, D_MODEL)), ada_w, _chip_cols(ada_b, chip))
    mod = lax.dynamic_index_in_dim(mod_all, me, axis=2, keepdims=False)
    mod = mod.transpose(1, 0, 2).reshape(DEPTH, 3 * D_MODEL)

    items = [[(w[n][layer // 2].astype(MXU_DTYPE), how) for n, how in GATHER_HOW[layer % 2]] for layer in range(DEPTH)]
    later_items = [item for layer_items in items[1:] for item in layer_items]
    first = _gather_chips(items[0] + [(_pack_rows(w, _SMALL_W, SMALL_W_ROWS), "slot")], [mod_all])
    small_w = _unpack_small_w(first[len(items[0])])
    weights_sent = _gather_start(later_items, [_place_own(a, how, chip.reshape(1)) for a, how in later_items],
                                 "gather_start", [first[0]])
    later = []

    def layer_weights(layer, x_in):
        i = layer // 2
        if layer == 0:
            arrays = first[:len(items[0])]
        else:
            if not later:
                later.extend(_gather_wait(later_items, weights_sent, [x_in], "gather_wait"))
            at = sum(len(layer_items) for layer_items in items[1:layer])
            arrays = later[at:at + len(items[layer])]
        if layer % 2 == 0:
            ein, uq, ukv, eout = arrays
            wuk, wuv = _ukv_to_heads(ukv)
            wq, wq_rot = _uq_to_heads(uq)
            return {"w_in": _ein_from_shards(ein), "wq": wq, "wq_rot": wq_rot, "wuk": wuk, "wuv": wuv, "w_out": eout,
                    "sc_conv_w": small_w["even_sc_conv_w"][i]}
        oin, oout = arrays
        return {"w_in": oin, "w_out": oout, "conv_w": small_w["odd_conv_w"][i], "conv_b": small_w["odd_conv_b"][i:i + 1],
                "ln_g": small_w["odd_ln_g"][i:i + 1], "ln_b": small_w["odd_ln_b"][i:i + 1]}

    in_flight, own, sib, last = {}, {}, {}, {}

    def land(layer, after):
        names, started, kept = in_flight.pop(layer)
        bufs, arrived = _rs_wait(started, after, "rs_wait_%d" % layer)
        sums = [_add_chips(b, t, chip.reshape(1)) for b, t in zip(bufs if kept is None else kept, arrived)]
        for n, mine, theirs in zip(names, sums, _rs_sibling(sums)):
            own[n, layer // 2], sib[n, layer // 2] = mine, theirs

    def grads_done(layer, bufs, dx_in):
        if layer + 1 in in_flight:
            land(layer + 1, [dx_in])
        if layer == 0:
            last.update(bufs)
            return None
        names = sorted(bufs)
        in_flight[layer] = (names, _rs_start([bufs[n] for n in names], "rs_start_%d" % layer), None)
        return in_flight[layer][1][-1]

    p = {"pre_norm_g": pre_norm_g, "post_norm_g": post_norm_g, "even_sc_conv_b": even_sc_conv_b,
         "even_q_norm_g": even_q_norm_g, "even_kv_norm_g": even_kv_norm_g}
    inv_freq = 1.0 / (ROPE_THETA ** (jnp.arange(0, QK_ROPE, 2, dtype=F32) / QK_ROPE))
    inv_freq = jnp.zeros((1, HEAD_PAD), F32).at[0, QK_NOPE:QK_NOPE + QK_ROPE].set(jnp.tile(inv_freq, 2))
    cos, sin = _rope_tables(positions.reshape(s, 1), inv_freq)

    loss, dx, g = _local_step(x[0], loss_target[0], cos, sin, mod, p, layer_weights, weights_sent[-1], grads_done)

    grads, deltas, new_m, new_v = {}, {}, {}, {}

    def update_layers(n, results, pairs):
        for i in pairs:
            results = _adamw_layer(w[n], [own[n, i], sib[n, i]], m[n], v[n], i, results, n)
        return results

    small_all, small_sum = _gather_sum_all(_pack_rows(g, _SMALL, SMALL_ROWS))
    names = sorted(last)
    kept = [last[n] for n in names]
    in_flight[0] = (names, _rs_start([b.astype(jnp.bfloat16) for b in kept], "rs_start_0", [small_sum]), kept)
    tot = _unpack_small(small_sum)
    dmod_all = small_all[:, :DEPTH * 3 * D_MODEL // 128].reshape(N_DEV, DEPTH, 3 * D_MODEL)
    grads["ada_w"] = _ada_bwd(c_all[:, 0, :].T, _chip_cols(dmod_all, chip).transpose(1, 0, 2))
    grads["ada_b"] = tot["dmod"]
    for n in ("pre_norm_g", "post_norm_g", "even_sc_conv_b", "even_q_norm_g", "even_kv_norm_g"):
        grads[n] = tot[n]
    for n in ("even_sc_conv_w", "odd_conv_w", "odd_conv_b", "odd_ln_g", "odd_ln_b"):
        grads[n] = _chip_cols(tot[n], chip)
    for n in list(grads):
        _, deltas[n], new_m[n], new_v[n] = _adamw(w[n], [grads[n]], m[n], v[n], n)

    for n in ("odd_w_in", "odd_w_out"):
        grads[n], deltas[n], new_m[n], new_v[n] = update_layers(n, None, (1, 0))
    partly = {n: update_layers(n, None, (1,)) for n in ("even_w_in", "even_w_out")}
    land(0, [deltas["ada_w"], deltas["odd_w_in"], partly["even_w_in"][1]])
    for n in ("even_w_in", "even_w_out"):
        grads[n], deltas[n], new_m[n], new_v[n] = update_layers(n, partly[n], (0,))
    uq_parts, ukv_parts = zip(*[[jnp.stack(part) for part in zip(*[_mla_local(q["even_mla", i]) for i in range(N_PAIRS)])]
                                for q in (own, sib)])
    for n, parts in (("even_w_uq", uq_parts), ("even_w_ukv", ukv_parts)):
        grads[n], deltas[n], new_m[n], new_v[n] = _adamw(w[n], list(parts), m[n], v[n], n)

    total_loss = lax.psum(loss[0, 0], ("x", "y", "c"))
    return (total_loss, dx[None], *[grads[n] for n in WEIGHT_NAMES], *[deltas[n] for n in WEIGHT_NAMES],
            *[new_m[n] for n in WEIGHT_NAMES], *[new_v[n] for n in WEIGHT_NAMES])
```

```python
import functools

import jax
import jax.numpy as jnp
from jax import lax
from jax.experimental import pallas as pl
from jax.experimental.pallas import tpu as pltpu

F32 = jnp.float32
MXU_DTYPE = jnp.bfloat16
MESH = pl.DeviceIdType.MESH
VMEM_LIMIT_V7X = 56 * 2 ** 20

EPS = 1e-6
D_MODEL = 1024
DEPTH = 4
CHUNK = 64
SC_WIDTH = 512
SC_KERNEL = 3
SC_HALO = 8
HEADS = 8
QK_NOPE = 64
QK_ROPE = 32
V_HEAD = 64
HEAD_PAD = 128
Q_LORA = 256
KV_LORA = 128
ROPE_THETA = 10000.0
CONF_KERNEL = 31
CONF_HALO = 32
CONV_ROWS = 32
SUBLANES = 8
EVEN_IN = 2976
EVEN_PAD = 3072
ODD_IN = 3072
N_CHIPS = 4
N_DEV = 8
NEG = -1e30

ADAM_LR = 0.001
ADAM_B1 = 0.9
ADAM_B2 = 0.999
ADAM_EPS = 1e-08
ADAM_WD = 0.01
ADAM_STEP = 10

N_PAIRS = DEPTH // 2
EVEN_SHARD = EVEN_IN // N_CHIPS
EVEN_SHARD_PAD = 768
MLA_ROWS = Q_LORA + 2 * KV_LORA


def _cp(n_grid=0, **kw):
    return pltpu.CompilerParams(dimension_semantics=("arbitrary",) * n_grid,
                                vmem_limit_bytes=VMEM_LIMIT_V7X, **kw)


def _sigmoid(x):
    return 1.0 / (1.0 + jnp.exp(-x))


def _silu(x):
    return x * _sigmoid(x)


def _dsilu(x):
    s = _sigmoid(x)
    return s * (1.0 + x * (1.0 - s))


def _rms(x, g):
    return x * lax.rsqrt(jnp.mean(x * x, axis=-1, keepdims=True) + EPS) * g


def _dot(a, b, dims):
    return lax.dot_general(a.astype(MXU_DTYPE), b.astype(MXU_DTYPE), (dims, ((), ())),
                           preferred_element_type=F32)


def _dot_nn(a, b):
    return _dot(a, b, ((1,), (0,)))


def _dot_nt(a, b):
    return _dot(a, b, ((1,), (1,)))


def _dot_tn(a, b):
    return _dot(a, b, ((0,), (0,)))


def _rows(ts, w, cb=0):
    return pl.BlockSpec((ts, w), lambda i: (i, cb))


def _vec(w, cb=0, r=1):
    return pl.BlockSpec((r, w), lambda i: (0, cb))


def _prev_halo(ts, hr, w, cb):
    return pl.BlockSpec((hr, w), lambda i: (jnp.maximum(i * (ts // hr) - 1, 0), cb))


def _next_halo(ts, hr, w, cb, s):
    return pl.BlockSpec((hr, w), lambda i: (jnp.minimum((i + 1) * (ts // hr), s // hr - 1), cb))


def _sds(shape, dtype=F32):
    return jax.ShapeDtypeStruct(shape, dtype)


def _mm(a, b, mode, out_dtype, tm, tn, name):
    tm = min(tm, a.shape[1] if mode == "tn" else a.shape[0])
    tn = min(tn, b.shape[0] if mode == "nt" else b.shape[1])
    if mode == "nn":
        (m, k), n = a.shape, b.shape[1]
        a_spec = pl.BlockSpec((tm, k), lambda i, j: (i, 0))
        b_spec = pl.BlockSpec((k, tn), lambda i, j: (0, j))
        dot = _dot_nn
    elif mode == "nt":
        (m, k), n = a.shape, b.shape[0]
        a_spec = pl.BlockSpec((tm, k), lambda i, j: (i, 0))
        b_spec = pl.BlockSpec((tn, k), lambda i, j: (j, 0))
        dot = _dot_nt
    else:
        (k, m), n = a.shape, b.shape[1]
        a_spec = pl.BlockSpec((k, tm), lambda i, j: (0, i))
        b_spec = pl.BlockSpec((k, tn), lambda i, j: (0, j))
        dot = _dot_tn
    assert m % tm == 0 and n % tn == 0, (name, m, n, tm, tn)

    def body(a_ref, b_ref, o_ref):
        o_ref[...] = dot(a_ref[...], b_ref[...]).astype(o_ref.dtype)

    return pl.pallas_call(
        body, name=name, grid=(m // tm, n // tn), in_specs=[a_spec, b_spec],
        out_specs=pl.BlockSpec((tm, tn), lambda i, j: (i, j)), out_shape=_sds((m, n), out_dtype),
        compiler_params=_cp(2))(a, b)


def _mm_tn_shards(a, b, by, name):
    k, m = a.shape
    n = b.shape[1]
    if by == "cols":
        tm, tn = m, n // N_CHIPS
        shape, grid = (N_CHIPS, m, tn), (1, N_CHIPS)
        out_spec = pl.BlockSpec((1, tm, tn), lambda i, j: (j, i, 0))
    else:
        tm, tn = m // N_CHIPS, n
        shape, grid = (N_CHIPS, tm, n), (N_CHIPS, 1)
        out_spec = pl.BlockSpec((1, tm, tn), lambda i, j: (i, 0, j))

    def body(a_ref, b_ref, o_ref):
        o_ref[0] = _dot_tn(a_ref[...], b_ref[...])

    return pl.pallas_call(
        body, name=name, grid=grid,
        in_specs=[pl.BlockSpec((k, tm), lambda i, j: (0, i)), pl.BlockSpec((k, tn), lambda i, j: (0, j))],
        out_specs=out_spec, out_shape=_sds(shape), compiler_params=_cp(2))(a, b)


def _even_col(q):
    return q if q < 2432 else (q + 64 if q < 2464 else q + 96)


def _shard_pieces(j):
    lo, hi = EVEN_SHARD * j, EVEN_SHARD * (j + 1)
    cuts = [lo] + [b for b in (2432, 2464) if lo < b < hi] + [hi]
    return [(a - lo, _even_col(a), b - a) for a, b in zip(cuts[:-1], cuts[1:])]


def _ein_from_shards(w):
    _, d, _ = w.shape
    tr = 256

    def body(w_ref, o_ref):
        parts, at = [], 0
        for j in range(N_CHIPS):
            for d0, s0, n in _shard_pieces(j):
                if s0 > at:
                    parts.append(jnp.zeros((tr, s0 - at), F32))
                parts.append(w_ref[j, :, d0:d0 + n].astype(F32))
                at = s0 + n
        o_ref[...] = jnp.concatenate(parts, axis=1).astype(o_ref.dtype)

    return pl.pallas_call(
        body, name="ein_from_shards", grid=(d // tr,),
        in_specs=[pl.BlockSpec((N_CHIPS, tr, EVEN_SHARD), lambda i: (0, i, 0))],
        out_specs=_rows(tr, EVEN_PAD), out_shape=_sds((d, EVEN_PAD), w.dtype), compiler_params=_cp(1))(w)


def _ein_to_shards(dw):
    d = dw.shape[0]
    tr = 256

    def body(dw_ref, o_ref):
        for j in range(N_CHIPS):
            parts = [dw_ref[:, s0:s0 + n] for _, s0, n in _shard_pieces(j)]
            o_ref[j] = jnp.concatenate(parts + [jnp.zeros((tr, EVEN_SHARD_PAD - EVEN_SHARD), F32)], axis=1)

    return pl.pallas_call(
        body, name="ein_to_shards", grid=(d // tr,), in_specs=[_rows(tr, EVEN_PAD)],
        out_specs=pl.BlockSpec((N_CHIPS, tr, EVEN_SHARD_PAD), lambda i: (0, i, 0)),
        out_shape=_sds((N_CHIPS, d, EVEN_SHARD_PAD)), compiler_params=_cp(1))(dw)


def _rope_tables(pos_col, invf):
    s = pos_col.shape[0]
    ts = min(512, s)

    def body(p_ref, f_ref, c_ref, s_ref):
        ang = p_ref[...].astype(F32) * f_ref[...]
        lane = lax.broadcasted_iota(jnp.int32, ang.shape, 1)
        rope = (lane >= QK_NOPE) & (lane < QK_NOPE + QK_ROPE)
        c_ref[...] = jnp.where(lane < QK_NOPE, 1.0, jnp.where(rope, jnp.cos(ang), 0.0))
        s_ref[...] = jnp.where(rope, jnp.sin(ang), 0.0)

    return pl.pallas_call(
        body, name="rope_tables", grid=(s // ts,), in_specs=[_rows(ts, 1), _vec(HEAD_PAD)],
        out_specs=[_rows(ts, HEAD_PAD)] * 2, out_shape=[_sds((s, HEAD_PAD))] * 2,
        compiler_params=_cp(1))(pos_col, invf)


def _after(dep):
    return () if dep is None else (dep,)


def _pre_fwd(x, g, mod_l, ts, dep=None):
    s, d = x.shape

    def body(x_ref, g_ref, sh_ref, sc_ref, *rest):
        h = _rms(x_ref[...], g_ref[...]) * (1.0 + sc_ref[...]) + sh_ref[...]
        rest[-1][...] = h.astype(rest[-1].dtype)

    return pl.pallas_call(
        body, name="pre_fwd", grid=(s // ts,),
        in_specs=[_rows(ts, d), _vec(d), _vec(d, 0), _vec(d, 1)] + [_HBM_SPEC] * len(_after(dep)),
        out_specs=_rows(ts, d), out_shape=_sds((s, d), MXU_DTYPE), compiler_params=_cp(1))(
            x, g, mod_l, mod_l, *_after(dep))


def _pre_bwd(dz, w_in, dx_out, x, g, mod_l, ts):
    s, d = x.shape
    n_in = dz.shape[1]

    def f(xv, gv, sh, sc):
        return _rms(xv, gv) * (1.0 + sc) + sh

    def body(dz_ref, w_ref, dxo_ref, x_ref, g_ref, sh_ref, sc_ref, dx_ref, dsh_ref, dsc_ref, dg_ref):
        i = pl.program_id(0)
        _, vjp = jax.vjp(f, x_ref[...], g_ref[...], sh_ref[...], sc_ref[...])
        dx, dg, dsh, dsc = vjp(_dot_nt(dz_ref[...], w_ref[...]))
        dx_ref[...] = dxo_ref[...] + dx

        @pl.when(i == 0)
        def _():
            dsh_ref[...] = jnp.zeros_like(dsh_ref)
            dsc_ref[...] = jnp.zeros_like(dsc_ref)
            dg_ref[...] = jnp.zeros_like(dg_ref)

        dsh_ref[...] += dsh
        dsc_ref[...] += dsc
        dg_ref[...] += dg

    return pl.pallas_call(
        body, name="pre_bwd", grid=(s // ts,),
        in_specs=[_rows(ts, n_in), _vec(n_in, 0, d), _rows(ts, d), _rows(ts, d), _vec(d), _vec(d, 0), _vec(d, 1)],
        out_specs=[_rows(ts, d), _vec(d), _vec(d), _vec(d)],
        out_shape=[_sds((s, d)), _sds((1, d)), _sds((1, d)), _sds((1, d))],
        compiler_params=_cp(1))(dz, w_in, dx_out, x, g, mod_l, mod_l)


def _post_fwd(x, yo, g, mod_l, ts):
    s, d = x.shape

    def body(x_ref, yo_ref, g_ref, gate_ref, o_ref):
        o_ref[...] = x_ref[...] + gate_ref[...] * _rms(yo_ref[...], g_ref[...])

    return pl.pallas_call(
        body, name="post_fwd", grid=(s // ts,),
        in_specs=[_rows(ts, d), _rows(ts, d), _vec(d), _vec(d, 2)],
        out_specs=_rows(ts, d), out_shape=_sds((s, d)), compiler_params=_cp(1))(x, yo, g, mod_l)


def _post_bwd(dx_out, yo, g, mod_l, ts, dep=None):
    s, d = yo.shape

    def f(yov, gv, gate):
        return gate * _rms(yov, gv)

    def body(dx_ref, yo_ref, g_ref, gate_ref, *rest):
        dyo_ref, dgate_ref, dg_ref = rest[-3:]
        i = pl.program_id(0)
        _, vjp = jax.vjp(f, yo_ref[...], g_ref[...], gate_ref[...])
        dyo, dg, dgate = vjp(dx_ref[...])
        dyo_ref[...] = dyo.astype(dyo_ref.dtype)

        @pl.when(i == 0)
        def _():
            dgate_ref[...] = jnp.zeros_like(dgate_ref)
            dg_ref[...] = jnp.zeros_like(dg_ref)

        dgate_ref[...] += dgate
        dg_ref[...] += dg

    return pl.pallas_call(
        body, name="post_bwd", grid=(s // ts,),
        in_specs=[_rows(ts, d), _rows(ts, d), _vec(d), _vec(d, 2)] + [_HBM_SPEC] * len(_after(dep)),
        out_specs=[_rows(ts, d), _vec(d), _vec(d)],
        out_shape=[_sds((s, d), MXU_DTYPE), _sds((1, d)), _sds((1, d))],
        compiler_params=_cp(1))(dx_out, yo, g, mod_l, *_after(dep))


def _loss_fwd_bwd(x, target, ts):
    s, d = x.shape

    def body(x_ref, t_ref, loss_ref, dx_ref):
        i = pl.program_id(0)
        err = x_ref[...] - t_ref[...]
        dx_ref[...] = err * (1.0 / d)

        @pl.when(i == 0)
        def _():
            loss_ref[...] = jnp.zeros_like(loss_ref)

        loss_ref[...] += 0.5 * jnp.sum(jnp.sum(err * err, axis=-1, keepdims=True) * (1.0 / d), axis=0, keepdims=True)

    return pl.pallas_call(
        body, name="loss", grid=(s // ts,), in_specs=[_rows(ts, d), _rows(ts, d)],
        out_specs=[_vec(1), _rows(ts, d)], out_shape=[_sds((1, 1)), _sds((s, d))],
        compiler_params=_cp(1))(x, target)


def _rope(t, cos, sin):
    lane = lax.broadcasted_iota(jnp.int32, t.shape, 1)
    first = (lane >= QK_NOPE) & (lane < QK_NOPE + QK_ROPE // 2)
    second = (lane >= QK_NOPE + QK_ROPE // 2) & (lane < QK_NOPE + QK_ROPE)
    up = pltpu.roll(t, QK_ROPE // 2, 1)
    down = pltpu.roll(t, HEAD_PAD - QK_ROPE // 2, 1)
    return t * cos + jnp.where(first, -down, jnp.where(second, up, 0.0)) * sin


def _rope_transposed(g, cos, sin):
    lane = lax.broadcasted_iota(jnp.int32, g.shape, 1)
    first = (lane >= QK_NOPE) & (lane < QK_NOPE + QK_ROPE // 2)
    second = (lane >= QK_NOPE + QK_ROPE // 2) & (lane < QK_NOPE + QK_ROPE)
    u = g * sin
    up = pltpu.roll(u, QK_ROPE // 2, 1)
    down = pltpu.roll(u, HEAD_PAD - QK_ROPE // 2, 1)
    return g * cos + jnp.where(first, down, jnp.where(second, -up, 0.0))


def _mla_prep_fwd(z, cos, sin, qg, kvg, wq, wq_rot, wuk, wuv, ts):
    s = z.shape[0]
    wide = HEADS * HEAD_PAD

    def body(cq_ref, ckv_ref, kr_ref, cos_ref, sin_ref, qg_ref, kvg_ref, wq_ref, wqr_ref, wuk_ref, wuv_ref,
             q_ref, qt_ref, k_ref, v_ref):
        cos_v, sin_v = cos_ref[...], sin_ref[...]
        cqn = _rms(cq_ref[...], qg_ref[...])
        ckvn = _rms(ckv_ref[...], kvg_ref[...])
        kr = _rope(kr_ref[...], cos_v, sin_v)
        q_lin, q_rot = _dot_nn(cqn, wq_ref[...]), _dot_nn(cqn, wqr_ref[...])
        k_lin, v_all = _dot_nn(ckvn, wuk_ref[...]), _dot_nn(ckvn, wuv_ref[...])
        for h in range(HEADS):
            lanes = slice(h * HEAD_PAD, (h + 1) * HEAD_PAD)
            qh = q_lin[:, lanes] * cos_v + q_rot[:, lanes] * sin_v
            q_ref[h] = qh.astype(q_ref.dtype)
            qt_ref[h, 0] = qh.T.astype(qt_ref.dtype)
            k_ref[h] = (k_lin[:, lanes] + kr).astype(k_ref.dtype)
            v_ref[h] = v_all[:, lanes].astype(v_ref.dtype)

    out = pl.BlockSpec((HEADS, ts, HEAD_PAD), lambda i: (0, i, 0))
    return pl.pallas_call(
        body, name="mla_prep_fwd", grid=(s // ts,),
        in_specs=[_rows(ts, Q_LORA, 8), _rows(ts, KV_LORA, 18), _rows(ts, HEAD_PAD, 19), _rows(ts, HEAD_PAD), _rows(ts, HEAD_PAD),
                  _vec(Q_LORA), _vec(KV_LORA), _vec(wide, 0, Q_LORA), _vec(wide, 0, Q_LORA), _vec(wide, 0, KV_LORA),
                  _vec(wide, 0, KV_LORA)],
        out_specs=[out, pl.BlockSpec((HEADS, 1, HEAD_PAD, ts), lambda i: (0, i, 0, 0)), out, out],
        out_shape=[_sds((HEADS, s, HEAD_PAD), MXU_DTYPE), _sds((HEADS, s // ts, HEAD_PAD, ts), MXU_DTYPE)]
        + [_sds((HEADS, s, HEAD_PAD), MXU_DTYPE)] * 2,
        compiler_params=_cp(1))(z, z, z, cos, sin, qg, kvg, wq, wq_rot, wuk, wuv)


def _mla_prep_bwd(dz, dq, dk, dv, z, cos, sin, qg, kvg, wq, wuk, wuv, ts):
    s = z.shape[0]

    def fq(cq, g):
        return _rms(cq, g)

    def body(dz_in_ref, dq_ref, dk_ref, dv_ref, cq_ref, ckv_ref, cos_ref, sin_ref, qg_ref, kvg_ref, wq_ref, wuk_ref,
             wuv_ref, dz_ref, dw_ref, dqg_ref, dkvg_ref):
        del dz_in_ref
        cos_v, sin_v = cos_ref[...], sin_ref[...]

        @pl.when(pl.program_id(0) == 0)
        def _():
            dw_ref[...] = jnp.zeros_like(dw_ref)
            dqg_ref[...] = jnp.zeros_like(dqg_ref)
            dkvg_ref[...] = jnp.zeros_like(dkvg_ref)

        cqn, vjp_q = jax.vjp(fq, cq_ref[...], qg_ref[...])
        ckvn, vjp_kv = jax.vjp(fq, ckv_ref[...], kvg_ref[...])
        lane = lax.broadcasted_iota(jnp.int32, (ts, HEAD_PAD), 1)
        rope_lanes = (lane >= QK_NOPE) & (lane < QK_NOPE + QK_ROPE)
        dq_lin = jnp.concatenate([_rope_transposed(dq_ref[h], cos_v, sin_v).astype(MXU_DTYPE) for h in range(HEADS)], axis=1)
        dk_all = jnp.concatenate([dk_ref[h].astype(MXU_DTYPE) for h in range(HEADS)], axis=1)
        dv_all = jnp.concatenate([dv_ref[h].astype(MXU_DTYPE) for h in range(HEADS)], axis=1)
        dkr = jnp.where(rope_lanes, dk_ref[0], 0.0)
        for h in range(1, HEADS):
            dkr = dkr + jnp.where(rope_lanes, dk_ref[h], 0.0)
        dcq, dqg = vjp_q(_dot_nt(dq_lin, wq_ref[...]))
        dckv, dkvg = vjp_kv(_dot_nt(dk_all, wuk_ref[...]) + _dot_nt(dv_all, wuv_ref[...]))
        dz_ref[:, 0:Q_LORA] = dcq.astype(dz_ref.dtype)
        dz_ref[:, Q_LORA:Q_LORA + KV_LORA] = dckv.astype(dz_ref.dtype)
        dz_ref[:, Q_LORA + KV_LORA:] = _rope_transposed(dkr, cos_v, sin_v).astype(dz_ref.dtype)
        dqg_ref[...] += dqg
        dkvg_ref[...] += dkvg
        dwq, dwuk, dwuv = _dot_tn(cqn, dq_lin), _dot_tn(ckvn, dk_all), _dot_tn(ckvn, dv_all)
        for h in range(HEADS):
            lanes = slice(h * HEAD_PAD, (h + 1) * HEAD_PAD)
            row0 = (h % 2) * MLA_ROWS
            dw_ref[h // 2, row0:row0 + Q_LORA, :] += dwq[:, lanes]
            dw_ref[h // 2, row0 + Q_LORA:row0 + Q_LORA + KV_LORA, :] += dwuk[:, lanes]
            dw_ref[h // 2, row0 + Q_LORA + KV_LORA:row0 + MLA_ROWS, :] += dwuv[:, lanes]

    wide = HEADS * HEAD_PAD
    heads = pl.BlockSpec((HEADS, ts, HEAD_PAD), lambda i: (0, i, 0))
    whole = pl.BlockSpec((N_CHIPS, 2 * MLA_ROWS, HEAD_PAD), lambda i: (0, 0, 0))
    return pl.pallas_call(
        body, name="mla_prep_bwd", grid=(s // ts,),
        in_specs=[_HBM_SPEC, heads, heads, heads, _rows(ts, Q_LORA, 8), _rows(ts, KV_LORA, 18),
                  _rows(ts, HEAD_PAD), _rows(ts, HEAD_PAD), _vec(Q_LORA), _vec(KV_LORA), _vec(wide, 0, Q_LORA),
                  _vec(wide, 0, KV_LORA), _vec(wide, 0, KV_LORA)],
        out_specs=[_rows(ts, 512, 4), whole, _vec(Q_LORA), _vec(KV_LORA)],
        out_shape=[_sds(dz.shape, dz.dtype), _sds((N_CHIPS, 2 * MLA_ROWS, HEAD_PAD)), _sds((1, Q_LORA)), _sds((1, KV_LORA))],
        input_output_aliases={0: 0}, compiler_params=_cp(1))(dz, dq, dk, dv, z, z, cos, sin, qg, kvg, wq, wuk, wuv)


def _chunk_mask(q0, k0, tq, tk):
    rows = q0 + lax.broadcasted_iota(jnp.int32, (tq, tk), 0)
    cols = k0 + lax.broadcasted_iota(jnp.int32, (tq, tk), 1)
    return lax.shift_right_logical(cols, 6) <= lax.shift_right_logical(rows, 6)


def _attn_fwd(q, k, v, tq):
    s = q.shape[1]
    nq = s // tq
    scale = 1.0 / float(QK_NOPE + QK_ROPE) ** 0.5

    def body(q_ref, k_ref, v_ref, o_ref, lse_ref):
        qi, hh = pl.program_id(1), pl.program_id(2)
        qv = q_ref[0]

        def step(kj, carry, masked):
            m, l, acc = carry
            k0 = pl.multiple_of(kj * tq, tq)
            sc = _dot_nt(qv, k_ref[0, pl.ds(k0, tq), :]) * scale
            if masked:
                sc = jnp.where(_chunk_mask(qi * tq, k0, tq, tq), sc, NEG)
            m_new = jnp.maximum(m, jnp.max(sc, axis=-1, keepdims=True))
            alpha = jnp.exp(m - m_new)
            p = jnp.exp(sc - m_new)
            l = alpha * l + jnp.sum(p, axis=-1, keepdims=True)
            acc = alpha * acc + _dot_nn(p, v_ref[0, pl.ds(k0, tq), :])
            return m_new, l, acc

        init = (jnp.full((tq, 1), NEG, F32), jnp.zeros((tq, 1), F32), jnp.zeros((tq, HEAD_PAD), F32))
        carry = lax.fori_loop(0, qi, lambda kj, c: step(kj, c, False), init)
        m, l, acc = step(qi, carry, True)
        o = acc / l
        lse_ref[0] = m + jnp.log(l)

        @pl.when(hh == 0)
        def _():
            o_ref[...] = o

        @pl.when(hh == 1)
        def _():
            o_ref[...] += o

    head = lambda hp, qi, hh: 2 * hp + hh
    return pl.pallas_call(
        body, name="attn_fwd", grid=(HEADS // 2, nq, 2),
        in_specs=[pl.BlockSpec((1, tq, HEAD_PAD), lambda hp, qi, hh: (head(hp, qi, hh), qi, 0)),
                  pl.BlockSpec((1, s, HEAD_PAD), lambda hp, qi, hh: (head(hp, qi, hh), 0, 0)),
                  pl.BlockSpec((1, s, HEAD_PAD), lambda hp, qi, hh: (head(hp, qi, hh), 0, 0))],
        out_specs=[pl.BlockSpec((tq, HEAD_PAD), lambda hp, qi, hh: (qi, hp)),
                   pl.BlockSpec((1, tq, 1), lambda hp, qi, hh: (head(hp, qi, hh), qi, 0))],
        out_shape=[_sds((s, HEADS * V_HEAD)), _sds((HEADS, s, 1))],
        compiler_params=_cp(3))(q, k, v)


def _attn_bwd(q, q_t, k, v, do, do_t, o, lse, tq):
    s = q.shape[1]
    nq = s // tq
    per_q = tq // do_t.shape[3]
    scale = 1.0 / float(QK_NOPE + QK_ROPE) ** 0.5

    def body(q_ref, qt_ref, k_ref, v_ref, do_ref, dot_ref, o_ref, lse_ref, dq_ref, dk_ref, dv_ref, dk_t, dv_t):
        hh, kj = pl.program_id(1), pl.program_id(2)

        @pl.when(kj == 0)
        def _():
            dq_ref[...] = jnp.zeros_like(dq_ref)

        kv, vv = k_ref[0], v_ref[0]
        lane = lax.broadcasted_iota(jnp.int32, (tq, HEAD_PAD), 1)
        mine = lax.shift_right_logical(lane, 6) == hh
        dk_t[...] = jnp.zeros_like(dk_t)
        dv_t[...] = jnp.zeros_like(dv_t)

        def step(qi, masked):
            q0 = pl.multiple_of(qi * tq, tq)
            qv = q_ref[0, pl.ds(q0, tq), :]
            dov = do_ref[pl.ds(q0, tq), :]
            delta = jnp.sum(jnp.where(mine, dov * o_ref[pl.ds(q0, tq), :], 0.0), axis=-1, keepdims=True)
            sc = _dot_nt(qv, kv) * scale
            if masked:
                sc = jnp.where(_chunk_mask(q0, kj * tq, tq, tq), sc, NEG)
            p = jnp.exp(sc - lse_ref[0, pl.ds(q0, tq), :])
            ds = (p * (_dot_nt(dov, vv) - delta) * scale).astype(MXU_DTYPE)
            do_tv = jnp.concatenate([dot_ref[0, qi * per_q + r] for r in range(per_q)], axis=1)
            dv_t[...] += _dot_nn(do_tv, p)
            dk_t[...] += _dot_nn(qt_ref[0, qi], ds)
            dq_ref[0, pl.ds(q0, tq), :] += _dot_nn(ds, kv)

        step(kj, True)
        odd = (nq - 1 - kj) % 2

        @pl.when(odd == 1)
        def _():
            step(kj + 1, False)

        def two(i, c):
            step(kj + 1 + odd + 2 * i, False)
            step(kj + 2 + odd + 2 * i, False)
            return c

        lax.fori_loop(0, (nq - 1 - kj) // 2, two, 0)
        dk_ref[0] = dk_t[...].T
        dv_ref[0] = dv_t[...].T

    head = lambda hp, hh, kj: 2 * hp + hh
    full = pl.BlockSpec((1, s, HEAD_PAD), lambda hp, hh, kj: (head(hp, hh, kj), 0, 0))
    blk = pl.BlockSpec((1, tq, HEAD_PAD), lambda hp, hh, kj: (head(hp, hh, kj), kj, 0))
    pair = pl.BlockSpec((s, HEAD_PAD), lambda hp, hh, kj: (0, hp))
    return pl.pallas_call(
        body, name="attn_bwd", grid=(HEADS // 2, 2, nq),
        in_specs=[full, pl.BlockSpec((1,) + q_t.shape[1:], lambda hp, hh, kj: (head(hp, hh, kj), 0, 0, 0)), blk, blk,
                  pair, pl.BlockSpec((1,) + do_t.shape[1:], lambda hp, hh, kj: (hp, 0, 0, 0)), pair,
                  pl.BlockSpec((1, s, 1), lambda hp, hh, kj: (head(hp, hh, kj), 0, 0))],
        out_specs=[full, blk, blk], out_shape=[_sds((HEADS, s, HEAD_PAD))] * 3,
        scratch_shapes=[pltpu.VMEM((HEAD_PAD, tq), F32), pltpu.VMEM((HEAD_PAD, tq), F32)],
        compiler_params=_cp(3))(q, q_t, k, v, do, do_t, o, lse)


def _sc_conv(u, ubuf, w_ref, b_ref, ts):
    return (w_ref[2:3, :] * u + w_ref[1:2, :] * ubuf[pl.ds(SC_HALO - 1, ts), :]
            + w_ref[0:1, :] * ubuf[pl.ds(SC_HALO - 2, ts), :] + b_ref[...])


def _even_gate_fwd(z, o, sc_w, sc_b, ts):
    s = z.shape[0]
    w = SC_WIDTH

    def body(ab_ref, ac_ref, ax_ref, ag_ref, bg_ref, hc_ref, hx_ref, o_ref, w_ref, b_ref, y_ref, ubuf):
        i = pl.program_id(0)
        u = ac_ref[...] * ax_ref[...]
        ubuf[0:SC_HALO, :] = jnp.where(i > 0, hc_ref[...] * hx_ref[...], 0.0)
        ubuf[SC_HALO:, :] = u
        conv = _sc_conv(u, ubuf, w_ref, b_ref, ts)
        y_ref[:, 0:w] = (ab_ref[...] * conv * _silu(ag_ref[...])).astype(y_ref.dtype)
        y_ref[:, w:] = (o_ref[...] * _silu(bg_ref[...])).astype(y_ref.dtype)

    return pl.pallas_call(
        body, name="even_gate_fwd", grid=(s // ts,),
        in_specs=[_rows(ts, w, 0), _rows(ts, w, 1), _rows(ts, w, 2), _rows(ts, w, 3), _rows(ts, w, 5),
                  _prev_halo(ts, SC_HALO, w, 1), _prev_halo(ts, SC_HALO, w, 2), _rows(ts, w),
                  _vec(w, 0, SC_KERNEL), _vec(w)],
        out_specs=_rows(ts, 2 * w), out_shape=_sds((s, 2 * w), MXU_DTYPE),
        scratch_shapes=[pltpu.VMEM((ts + SC_HALO, w), F32)],
        compiler_params=_cp(1))(z, z, z, z, z, z, z, o, sc_w, sc_b)


def _even_gate_bwd(dy, z, o, sc_w, sc_b, ts):
    s = z.shape[0]
    w = SC_WIDTH
    n = s // ts

    def body(dya_ref, dyb_ref, dyan_ref, ab_ref, ac_ref, ax_ref, ag_ref, bg_ref, hc_ref, hx_ref, abn_ref, agn_ref,
             o_ref, w_ref, b_ref, dz_ref, do_ref, dot_ref, dw_ref, db_ref, ubuf, dbuf):
        i = pl.program_id(0)
        ab, ac, ax, ag, bg = ab_ref[...], ac_ref[...], ax_ref[...], ag_ref[...], bg_ref[...]
        dya, dyb = dya_ref[...], dyb_ref[...]
        u = ac * ax
        ubuf[0:SC_HALO, :] = jnp.where(i > 0, hc_ref[...] * hx_ref[...], 0.0)
        ubuf[SC_HALO:, :] = u
        conv = _sc_conv(u, ubuf, w_ref, b_ref, ts)
        sg = _silu(ag)
        dconv = dya * ab * sg
        dbuf[0:ts, :] = dconv
        dbuf[ts:, :] = jnp.where(i < n - 1, dyan_ref[...] * abn_ref[...] * _silu(agn_ref[...]), 0.0)
        du = w_ref[2:3, :] * dconv + w_ref[1:2, :] * dbuf[pl.ds(1, ts), :] + w_ref[0:1, :] * dbuf[pl.ds(2, ts), :]
        dz_ref[:, 0:w] = (dya * conv * sg).astype(dz_ref.dtype)
        dz_ref[:, w:2 * w] = (du * ax).astype(dz_ref.dtype)
        dz_ref[:, 2 * w:3 * w] = (du * ac).astype(dz_ref.dtype)
        dz_ref[:, 3 * w:4 * w] = (dya * ab * conv * _dsilu(ag)).astype(dz_ref.dtype)
        dz_ref[:, 4 * w:5 * w] = jnp.zeros((ts, w), dz_ref.dtype)
        dz_ref[:, 5 * w:] = (dyb * o_ref[...] * _dsilu(bg)).astype(dz_ref.dtype)
        do = dyb * _silu(bg)
        do_ref[...] = do
        for pair in range(HEADS // 2):
            dot_ref[pair, 0] = do[:, pair * HEAD_PAD:(pair + 1) * HEAD_PAD].T.astype(dot_ref.dtype)

        @pl.when(i == 0)
        def _():
            dw_ref[...] = jnp.zeros_like(dw_ref)
            db_ref[...] = jnp.zeros_like(db_ref)

        dw_ref[0:1, :] += jnp.sum(dconv * ubuf[pl.ds(SC_HALO - 2, ts), :], axis=0, keepdims=True)
        dw_ref[1:2, :] += jnp.sum(dconv * ubuf[pl.ds(SC_HALO - 1, ts), :], axis=0, keepdims=True)
        dw_ref[2:3, :] += jnp.sum(dconv * u, axis=0, keepdims=True)
        db_ref[...] += jnp.sum(dconv, axis=0, keepdims=True)

    return pl.pallas_call(
        body, name="even_gate_bwd", grid=(n,),
        in_specs=[_rows(ts, w, 0), _rows(ts, w, 1), _next_halo(ts, SC_HALO, w, 0, s),
                  _rows(ts, w, 0), _rows(ts, w, 1), _rows(ts, w, 2), _rows(ts, w, 3), _rows(ts, w, 5),
                  _prev_halo(ts, SC_HALO, w, 1), _prev_halo(ts, SC_HALO, w, 2),
                  _next_halo(ts, SC_HALO, w, 0, s), _next_halo(ts, SC_HALO, w, 3, s),
                  _rows(ts, w), _vec(w, 0, SC_KERNEL), _vec(w)],
        out_specs=[_rows(ts, EVEN_PAD), _rows(ts, w), pl.BlockSpec((HEADS // 2, 1, HEAD_PAD, ts), lambda i: (0, i, 0, 0)),
                   _vec(w, 0, SC_KERNEL), _vec(w)],
        out_shape=[_sds((s, EVEN_PAD), MXU_DTYPE), _sds((s, w)), _sds((HEADS // 2, n, HEAD_PAD, ts), MXU_DTYPE),
                   _sds((SC_KERNEL, w)), _sds((1, w))],
        scratch_shapes=[pltpu.VMEM((ts + SC_HALO, w), F32), pltpu.VMEM((ts + SC_HALO, w), F32)],
        compiler_params=_cp(1))(dy, dy, dy, z, z, z, z, z, z, z, z, z, o, sc_w, sc_b)


def _ln_act(uc, sg, g, b):
    mu = jnp.mean(uc, axis=-1, keepdims=True)
    var = jnp.mean(jnp.square(uc - mu), axis=-1, keepdims=True)
    return _silu((uc - mu) * lax.rsqrt(var + EPS) * g + b) * _silu(sg)


def _shifted_copies(buf, shifted, rows):
    for b in range(1, SUBLANES):
        shifted[b - 1, 0:rows, :] = buf[pl.ds(b, rows), :]


def _rows_at(buf, shifted, start, n):
    a, b = divmod(start, SUBLANES)
    return buf[pl.ds(SUBLANES * a, n), :] if b == 0 else shifted[b - 1, pl.ds(SUBLANES * a, n), :]


def _odd_fwd(z, conv_w, conv_b, ln_g, ln_b, ts):
    s = z.shape[0]
    d = D_MODEL
    k = CONF_KERNEL

    def body(val_ref, glu_ref, sg_ref, hval_ref, hglu_ref, w_ref, b_ref, g_ref, beta_ref, y_ref, uc_ref, ubuf, ush):
        i = pl.program_id(0)
        ubuf[0:CONF_HALO, :] = jnp.where(i > 0, hval_ref[...] * _sigmoid(hglu_ref[...]), 0.0)
        ubuf[CONF_HALO:, :] = val_ref[...] * _sigmoid(glu_ref[...])
        _shifted_copies(ubuf, ush, ts + CONF_HALO - SUBLANES)
        for r0 in range(0, ts, CONV_ROWS):
            acc = jnp.broadcast_to(b_ref[...], (CONV_ROWS, d))
            for j in range(k):
                acc = acc + w_ref[j:j + 1, :] * _rows_at(ubuf, ush, r0 + CONF_HALO - (k - 1) + j, CONV_ROWS)
            uc_ref[r0:r0 + CONV_ROWS, :] = acc
        y_ref[...] = _ln_act(uc_ref[...], sg_ref[...], g_ref[...], beta_ref[...]).astype(y_ref.dtype)

    return pl.pallas_call(
        body, name="odd_fwd", grid=(s // ts,),
        in_specs=[_rows(ts, d, 0), _rows(ts, d, 1), _rows(ts, d, 2),
                  _prev_halo(ts, CONF_HALO, d, 0), _prev_halo(ts, CONF_HALO, d, 1),
                  _vec(d, 0, k), _vec(d), _vec(d), _vec(d)],
        out_specs=[_rows(ts, d), _rows(ts, d)], out_shape=[_sds((s, d), MXU_DTYPE), _sds((s, d))],
        scratch_shapes=[pltpu.VMEM((ts + CONF_HALO, d), F32),
                        pltpu.VMEM((SUBLANES - 1, ts + CONF_HALO - SUBLANES, d), F32)],
        compiler_params=_cp(1))(z, z, z, z, z, conv_w, conv_b, ln_g, ln_b)


def _odd_bwd(dy, z, uc, conv_w, ln_g, ln_b, ts):
    s = z.shape[0]
    d = D_MODEL
    k = CONF_KERNEL
    n = s // ts

    def body(dy_ref, dyn_ref, val_ref, glu_ref, sg_ref, sgn_ref, uc_ref, ucn_ref,
             w_ref, g_ref, beta_ref, dz_ref, dw_ref, db_ref, dg_ref, dbeta_ref, dbuf, dsh, dw_acc):
        i = pl.program_id(0)
        val, glu = val_ref[...], glu_ref[...]
        sig = _sigmoid(glu)
        u = val * sig
        _, vjp = jax.vjp(_ln_act, uc_ref[...], sg_ref[...], g_ref[...], beta_ref[...])
        duc, dsg, dg, dbeta = vjp(dy_ref[...])
        _, vjp_n = jax.vjp(_ln_act, ucn_ref[...], sgn_ref[...], g_ref[...], beta_ref[...])
        dbuf[0:ts, :] = duc
        dbuf[ts:, :] = jnp.where(i < n - 1, vjp_n(dyn_ref[...])[0], 0.0)
        dz_ref[:, 2 * d:] = dsg.astype(dz_ref.dtype)
        _shifted_copies(dbuf, dsh, ts + CONF_HALO - SUBLANES)

        @pl.when(i == 0)
        def _():
            dw_acc[...] = jnp.zeros_like(dw_acc)
            db_ref[...] = jnp.zeros_like(db_ref)
            dg_ref[...] = jnp.zeros_like(dg_ref)
            dbeta_ref[...] = jnp.zeros_like(dbeta_ref)

        db_ref[...] += jnp.sum(duc, axis=0, keepdims=True)
        dg_ref[...] += dg
        dbeta_ref[...] += dbeta
        for r0 in range(0, ts, CONV_ROWS):
            acc = jnp.zeros((CONV_ROWS, d), F32)
            for j in range(k):
                acc = acc + w_ref[j:j + 1, :] * _rows_at(dbuf, dsh, r0 + (k - 1) - j, CONV_ROWS)
            sig_r = sig[r0:r0 + CONV_ROWS, :]
            dz_ref[r0:r0 + CONV_ROWS, 0:d] = (acc * sig_r).astype(dz_ref.dtype)
            dz_ref[r0:r0 + CONV_ROWS, d:2 * d] = (acc * val[r0:r0 + CONV_ROWS, :] * sig_r * (1.0 - sig_r)).astype(dz_ref.dtype)
        for j in range(k):
            prod = _rows_at(dbuf, dsh, (k - 1) - j, ts) * u
            dw_acc[j] += jnp.sum(prod.reshape(ts // SUBLANES, SUBLANES, d), axis=0)

        @pl.when(i == n - 1)
        def _():
            dw_ref[...] = jnp.sum(dw_acc[...], axis=1)

    return pl.pallas_call(
        body, name="odd_bwd", grid=(n,),
        in_specs=[_rows(ts, d), _next_halo(ts, CONF_HALO, d, 0, s),
                  _rows(ts, d, 0), _rows(ts, d, 1), _rows(ts, d, 2), _next_halo(ts, CONF_HALO, d, 2, s),
                  _rows(ts, d), _next_halo(ts, CONF_HALO, d, 0, s),
                  _vec(d, 0, k), _vec(d), _vec(d)],
        out_specs=[_rows(ts, ODD_IN), _vec(d, 0, k), _vec(d), _vec(d), _vec(d)],
        out_shape=[_sds((s, ODD_IN), MXU_DTYPE), _sds((k, d)), _sds((1, d)), _sds((1, d)), _sds((1, d))],
        scratch_shapes=[pltpu.VMEM((ts + CONF_HALO, d), F32),
                        pltpu.VMEM((SUBLANES - 1, ts + CONF_HALO - SUBLANES, d), F32), pltpu.VMEM((k, SUBLANES, d), F32)],
        compiler_params=_cp(1))(dy, dy, z, z, z, z, uc, uc, conv_w, ln_g, ln_b)


def _local_step(x, target, cos, sin, mod, p, layer_weights, fwd_dep=None, grads_done=None):
    s = x.shape[0]
    tsf, tsb = min(512, s // 2), min(256, s // 2)
    tq = min(512, s // 2)
    row1 = lambda a, i: a[i:i + 1]
    saved = []
    for layer in range(DEPTH):
        i = layer // 2
        mod_l = row1(mod, layer)
        wl = layer_weights(layer, x)
        h = _pre_fwd(x, row1(p["pre_norm_g"], layer), mod_l, tsf, fwd_dep if layer == 0 else None)
        if layer % 2 == 0:
            z = _mm(h, wl["w_in"], "nn", F32, 256, EVEN_PAD, "even_in_fwd")
            q, q_t, k, v = _mla_prep_fwd(z, cos, sin, row1(p["even_q_norm_g"], i), row1(p["even_kv_norm_g"], i),
                                    wl["wq"], wl["wq_rot"], wl["wuk"], wl["wuv"], tsf)
            o, lse = _attn_fwd(q, k, v, tq)
            y = _even_gate_fwd(z, o, wl["sc_conv_w"], row1(p["even_sc_conv_b"], i), tsf)
            yo = _mm(y, wl["w_out"], "nn", F32, 512, 1024, "even_out_fwd")
            saved.append((x, h, z, y, yo, wl, (q, q_t, k, v, o, lse)))
        else:
            z = _mm(h, wl["w_in"], "nn", F32, 256, ODD_IN, "odd_in_fwd")
            y, uc = _odd_fwd(z, wl["conv_w"], wl["conv_b"], wl["ln_g"], wl["ln_b"], tsf)
            yo = _mm(y, wl["w_out"], "nn", F32, 512, 1024, "odd_out_fwd")
            saved.append((x, h, z, y, yo, wl, uc))
        x = _post_fwd(x, yo, row1(p["post_norm_g"], layer), mod_l, tsf)

    loss, dx = _loss_fwd_bwd(x, target, tsf)

    g = {n: [None] * (DEPTH if n in ("pre_norm_g", "post_norm_g") else N_PAIRS) for n in (
        "pre_norm_g", "post_norm_g", "even_sc_conv_w", "even_sc_conv_b", "even_q_norm_g", "even_kv_norm_g",
        "odd_conv_w", "odd_conv_b", "odd_ln_g", "odd_ln_b")}
    dmod = [None] * DEPTH
    dep = None
    for layer in reversed(range(DEPTH)):
        i = layer // 2
        mod_l = row1(mod, layer)
        x_in, h, z, y, yo, wl, extra = saved[layer]
        dyo, dgate, g["post_norm_g"][layer] = _post_bwd(dx, yo, row1(p["post_norm_g"], layer), mod_l, tsb, dep)
        bufs = {}
        if layer % 2 == 0:
            q, q_t, k, v, o, lse = extra
            dy = _mm(dyo, wl["w_out"], "nt", F32, 512, 1024, "even_out_bwd_x")
            bufs["even_w_out"] = _mm_tn_shards(y, dyo, "rows", "even_out_bwd_w")
            dz, do, do_t, g["even_sc_conv_w"][i], g["even_sc_conv_b"][i] = _even_gate_bwd(
                dy, z, o, wl["sc_conv_w"], row1(p["even_sc_conv_b"], i), tsb)
            dq, dk, dv = _attn_bwd(q, q_t, k, v, do, do_t, o, lse, tq)
            dz, bufs["even_mla"], g["even_q_norm_g"][i], g["even_kv_norm_g"][i] = _mla_prep_bwd(
                dz, dq, dk, dv, z, cos, sin, row1(p["even_q_norm_g"], i), row1(p["even_kv_norm_g"], i),
                wl["wq"], wl["wuk"], wl["wuv"], tsb)
            bufs["even_w_in"] = _ein_to_shards(_mm(h, dz, "tn", F32, D_MODEL, 512, "even_in_bwd_w"))
        else:
            uc = extra
            dy = _mm(dyo, wl["w_out"], "nt", F32, 512, 1024, "odd_out_bwd_x")
            bufs["odd_w_out"] = _mm_tn_shards(y, dyo, "rows", "odd_out_bwd_w")
            dz, g["odd_conv_w"][i], g["odd_conv_b"][i], g["odd_ln_g"][i], g["odd_ln_b"][i] = _odd_bwd(
                dy, z, uc, wl["conv_w"], wl["ln_g"], wl["ln_b"], tsb)
            bufs["odd_w_in"] = _mm_tn_shards(h, dz, "cols", "odd_in_bwd_w")
        dx, dshift, dscale, g["pre_norm_g"][layer] = _pre_bwd(
            dz, wl["w_in"], dx, x_in, row1(p["pre_norm_g"], layer), mod_l, tsb)
        dmod[layer] = jnp.concatenate([dshift, dscale, dgate], axis=-1)
        dep = grads_done(layer, bufs, dx) if grads_done is not None else None
    stack = lambda parts: jnp.stack([a[0] if a.shape[0] == 1 and a.ndim == 2 else a for a in parts])
    small = {n: stack(parts) for n, parts in g.items()}
    small["dmod"] = jnp.concatenate(dmod, axis=0)
    return loss, dx, small


def _uq_to_heads(w):
    w = w.reshape(N_CHIPS, Q_LORA, 2, QK_NOPE + QK_ROPE).transpose(0, 2, 1, 3).reshape(HEADS, Q_LORA, QK_NOPE + QK_ROPE)
    half = QK_ROPE // 2
    rotated = jnp.concatenate([jnp.zeros_like(w[..., :QK_NOPE]), -w[..., QK_NOPE + half:], w[..., QK_NOPE:QK_NOPE + half]],
                              axis=-1)
    pad = ((0, 0), (0, 0), (0, HEAD_PAD - QK_NOPE - QK_ROPE))
    return _side_by_side(jnp.pad(w, pad)), _side_by_side(jnp.pad(rotated, pad))


def _side_by_side(w):
    return w.transpose(1, 0, 2).reshape(w.shape[1], HEADS * HEAD_PAD)


def _ukv_to_heads(w):
    w = w.reshape(N_CHIPS, KV_LORA, 2, QK_NOPE + V_HEAD).transpose(0, 2, 1, 3).reshape(HEADS, KV_LORA, QK_NOPE + V_HEAD)
    wk = jnp.pad(w[..., :QK_NOPE], ((0, 0), (0, 0), (0, HEAD_PAD - QK_NOPE)))
    wv = w[..., QK_NOPE:]
    zero = jnp.zeros_like(wv)
    odd = (jnp.arange(HEADS) % 2 == 1)[:, None, None]
    wv = jnp.concatenate([jnp.where(odd, zero, wv), jnp.where(odd, wv, zero)], axis=-1)
    return _side_by_side(wk), _side_by_side(wv)


def _mla_local(q):
    blocks = q.reshape(2, MLA_ROWS, HEAD_PAD)
    uq = jnp.concatenate([blocks[r, :Q_LORA, :QK_NOPE + QK_ROPE] for r in range(2)], axis=-1)
    ukv = jnp.concatenate(
        [jnp.concatenate([blocks[r, Q_LORA:Q_LORA + KV_LORA, :QK_NOPE],
                          blocks[r, Q_LORA + KV_LORA:, V_HEAD * r:V_HEAD * (r + 1)]], axis=-1) for r in range(2)], axis=-1)
    return uq, ukv


def _place():
    return lax.axis_index("x"), lax.axis_index("y"), lax.axis_index("c")


def _flip(v, bit):
    return 1 - v if bit else v


def _sem(a, k):
    return a * (N_CHIPS - 1) + k - 1


def _remote(src, dst, send_sem, recv_sem, peer):
    return pltpu.make_async_remote_copy(src_ref=src, dst_ref=dst, send_sem=send_sem, recv_sem=recv_sem,
                                        device_id=peer, device_id_type=MESH)


_VMEM_SPEC = pl.BlockSpec(memory_space=pltpu.VMEM)
_HBM_SPEC = pl.BlockSpec(memory_space=pl.ANY)


def _ada_fwd(c8, ada_w, ada_b_sh):
    depth, d, cols = ada_w.shape

    def body(c_ref, w_ref, b_ref, call_ref, mod_ref, s1, r1, s2, r2):
        x, y, c = _place()
        chip = 2 * x + y
        me = 2 * chip + c
        call_ref[me] = c_ref[...]
        sends = []
        for k in range(1, N_DEV):
            peer = (_flip(x, k & 4), _flip(y, k & 2), _flip(c, k & 1))
            cp = _remote(c_ref, call_ref.at[me], s1.at[k - 1], r1.at[k - 1], peer)
            cp.start()
            sends.append(cp)
        for k in range(1, N_DEV):
            src = 4 * _flip(x, k & 4) + 2 * _flip(y, k & 2) + _flip(c, k & 1)
            _remote(c_ref, call_ref.at[src], s1.at[k - 1], r1.at[k - 1], (x, y, c)).wait_recv()
        act = _silu(jnp.concatenate([call_ref[e, 0:1, :] for e in range(N_DEV)], axis=0))
        for l in range(depth):
            mod_ref[chip, l] = _dot_nn(act, w_ref[l]) + b_ref[l:l + 1, :]
        for k in range(1, N_CHIPS):
            peer = (_flip(x, k & 2), _flip(y, k & 1), c)
            cp = _remote(mod_ref.at[chip], mod_ref.at[chip], s2.at[k - 1], r2.at[k - 1], peer)
            cp.start()
            sends.append(cp)
        for k in range(1, N_CHIPS):
            src = 2 * _flip(x, k & 2) + _flip(y, k & 1)
            _remote(mod_ref.at[src], mod_ref.at[src], s2.at[k - 1], r2.at[k - 1], (x, y, c)).wait_recv()
        for cp in sends:
            cp.wait_send()

    return pl.pallas_call(
        body, name="ada_fwd", in_specs=[_VMEM_SPEC] * 3, out_specs=[_VMEM_SPEC] * 2,
        out_shape=[_sds((N_DEV, 8, d)), _sds((N_CHIPS, depth, N_DEV, cols))],
        scratch_shapes=[pltpu.SemaphoreType.DMA((N_DEV - 1,)), pltpu.SemaphoreType.DMA((N_DEV - 1,)),
                        pltpu.SemaphoreType.DMA((N_CHIPS - 1,)), pltpu.SemaphoreType.DMA((N_CHIPS - 1,))],
        compiler_params=pltpu.CompilerParams(vmem_limit_bytes=VMEM_LIMIT_V7X))(c8, ada_w, ada_b_sh)


def _ada_bwd(c_t, dmod_sh):
    depth, n, cols = dmod_sh.shape
    d = c_t.shape[0]
    tr = 256

    def body(c_ref, dm_ref, o_ref):
        act = _silu(c_ref[...])
        acc = act[:, 0:1] * dm_ref[0, 0:1, :]
        for e in range(1, n):
            acc = acc + act[:, e:e + 1] * dm_ref[0, e:e + 1, :]
        o_ref[0] = acc

    return pl.pallas_call(
        body, name="ada_bwd", grid=(depth, d // tr),
        in_specs=[pl.BlockSpec((tr, n), lambda l, i: (i, 0)), pl.BlockSpec((1, n, cols), lambda l, i: (l, 0, 0))],
        out_specs=pl.BlockSpec((1, tr, cols), lambda l, i: (l, i, 0)), out_shape=_sds((depth, d, cols)),
        compiler_params=_cp(2))(c_t, dmod_sh)


def _gathered_shape(shape, how):
    if how == "slot":
        return (N_CHIPS,) + shape
    r, cc = shape
    return (r, N_CHIPS * cc) if how == "cols" else (N_CHIPS * r, cc)


def _gathered_part(ref, shape, how, chip):
    if how == "slot":
        return ref.at[chip]
    if how == "cols":
        return ref.at[:, pl.ds(pl.multiple_of(chip * shape[1], 128), shape[1])]
    return ref.at[pl.ds(pl.multiple_of(chip * shape[0], 8), shape[0]), :]


_SEM_SPEC = pl.BlockSpec(memory_space=pltpu.SEMAPHORE)
_TOKEN = jax.ShapeDtypeStruct((8, 128), F32)
_SPLIT_COPY = pltpu.CompilerParams(has_side_effects=pltpu.SideEffectType.DATAFLOW_SIDE_EFFECTING)


def _in_hbm(a):
    return pltpu.with_memory_space_constraint(a, pltpu.HBM)


def _gather_start(items, gathered, name, after=()):
    n = len(items)

    def body(*refs):
        ins, outs = refs[:n], refs[n:2 * n]
        send_sems, recv_sems = refs[2 * n + len(after)], refs[2 * n + len(after) + 1]
        x, y, c = _place()
        for a in range(n):
            for k in range(1, N_CHIPS):
                part = _gathered_part(outs[a], items[a][0].shape, items[a][1], 2 * x + y)
                _remote(ins[a], part, send_sems.at[_sem(a, k)], recv_sems.at[_sem(a, k)],
                        (_flip(x, k & 2), _flip(y, k & 1), c)).start()
        refs[-1][...] = jnp.zeros(_TOKEN.shape, _TOKEN.dtype)

    arrays = [_in_hbm(a) for a, _ in items] + [_in_hbm(a) for a in gathered]
    res = pl.pallas_call(
        body, name=name, in_specs=[_HBM_SPEC] * (2 * n + len(after)),
        out_specs=[_SEM_SPEC, _SEM_SPEC] + [_HBM_SPEC] * (2 * n) + [_VMEM_SPEC],
        out_shape=[pltpu.SemaphoreType.DMA((n * (N_CHIPS - 1),)), pltpu.SemaphoreType.DMA((n * (N_CHIPS - 1),))]
        + [pltpu.HBM(a.shape, a.dtype) for a in arrays] + [_TOKEN],
        input_output_aliases={a: 2 + a for a in range(2 * n)}, compiler_params=_SPLIT_COPY)(*arrays, *after)
    return res[0], res[1], res[2:2 + n], res[2 + n:2 + 2 * n], res[-1]


def _gather_wait(items, started, after, name):
    n = len(items)
    send_sems, recv_sems, shards, gathered, _ = started

    def body(*refs):
        ins, outs, send_sems, recv_sems = refs[:n], refs[n:2 * n], refs[2 * n], refs[2 * n + 1]
        x, y, c = _place()
        for a in range(n):
            for k in range(1, N_CHIPS):
                part = _gathered_part(outs[a], items[a][0].shape, items[a][1], 2 * _flip(x, k & 2) + _flip(y, k & 1))
                cp = _remote(ins[a], part, send_sems.at[_sem(a, k)], recv_sems.at[_sem(a, k)], (x, y, c))
                cp.wait_send()
                cp.wait_recv()

    res = pl.pallas_call(
        body, name=name, in_specs=[_HBM_SPEC] * (2 * n) + [_SEM_SPEC, _SEM_SPEC] + [_HBM_SPEC] * len(after),
        out_specs=[_HBM_SPEC] * (2 * n), out_shape=[pltpu.HBM(a.shape, a.dtype) for a in (*shards, *gathered)],
        input_output_aliases={a: a for a in range(2 * n)}, compiler_params=_SPLIT_COPY)(
            *shards, *gathered, send_sems, recv_sems, *after)
    return res[n:]


def _rs_start(bufs, name, after=()):
    n = len(bufs)

    def body(*refs):
        srcs, lands = refs[:n], refs[n:2 * n]
        send_sems, recv_sems = refs[2 * n + len(after)], refs[2 * n + len(after) + 1]
        x, y, c = _place()
        for a in range(n):
            for k in range(1, N_CHIPS):
                tx, ty = _flip(x, k & 2), _flip(y, k & 1)
                _remote(srcs[a].at[2 * tx + ty], lands[a].at[k - 1], send_sems.at[_sem(a, k)], recv_sems.at[_sem(a, k)],
                        (tx, ty, c)).start()
        refs[-1][...] = jnp.zeros(_TOKEN.shape, _TOKEN.dtype)

    arrays = [_in_hbm(b) for b in bufs] + [_in_hbm(lax.empty((N_CHIPS - 1,) + b.shape[1:], b.dtype)) for b in bufs]
    res = pl.pallas_call(
        body, name=name, in_specs=[_HBM_SPEC] * (2 * n + len(after)),
        out_specs=[_SEM_SPEC, _SEM_SPEC] + [_HBM_SPEC] * (2 * n) + [_VMEM_SPEC],
        out_shape=[pltpu.SemaphoreType.DMA((n * (N_CHIPS - 1),)), pltpu.SemaphoreType.DMA((n * (N_CHIPS - 1),))]
        + [pltpu.HBM(a.shape, a.dtype) for a in arrays] + [_TOKEN],
        input_output_aliases={a: 2 + a for a in range(2 * n)}, compiler_params=_SPLIT_COPY)(*arrays, *after)
    return res[0], res[1], res[2:2 + n], res[2 + n:2 + 2 * n], res[-1]


def _rs_wait(started, after, name):
    send_sems, recv_sems, bufs, lands, _ = started
    n = len(bufs)

    def body(*refs):
        srcs, lnds, send_sems, recv_sems = refs[:n], refs[n:2 * n], refs[2 * n], refs[2 * n + 1]
        x, y, c = _place()
        for a in range(n):
            for k in range(1, N_CHIPS):
                cp = _remote(srcs[a].at[0], lnds[a].at[k - 1], send_sems.at[_sem(a, k)], recv_sems.at[_sem(a, k)], (x, y, c))
                cp.wait_send()
                cp.wait_recv()

    res = pl.pallas_call(
        body, name=name, in_specs=[_HBM_SPEC] * (2 * n) + [_SEM_SPEC, _SEM_SPEC] + [_HBM_SPEC] * len(after),
        out_specs=[_HBM_SPEC] * (2 * n), out_shape=[pltpu.HBM(a.shape, a.dtype) for a in (*bufs, *lands)],
        input_output_aliases={a: a for a in range(2 * n)}, compiler_params=_SPLIT_COPY)(
            *bufs, *lands, send_sems, recv_sems, *after)
    return res[:n], res[n:]


def _place_own(shard, how, chip_idx):
    r, cc = shard.shape
    block, index = {"slot": ((1, r, cc), lambda i, c: (c[0], 0, 0)), "cols": ((r, cc), lambda i, c: (0, c[0])),
                    "rows": ((r, cc), lambda i, c: (c[0], 0))}[how]

    def body(c_ref, in_ref, o_ref):
        del c_ref
        o_ref[...] = in_ref[...].reshape(o_ref.shape)

    return pl.pallas_call(
        body, name="place_own", out_shape=_sds(_gathered_shape(shard.shape, how), shard.dtype),
        grid_spec=pltpu.PrefetchScalarGridSpec(
            num_scalar_prefetch=1, grid=(1,), in_specs=[pl.BlockSpec((r, cc), lambda i, c: (0, 0))],
            out_specs=pl.BlockSpec(block, index)),
        compiler_params=_cp(1))(chip_idx, shard)


def _gather_chips(items, after=()):
    n = n_all = len(items)
    arrays = [a for a, _ in items]

    def body(*refs):
        ins, outs = refs[:n_all], refs[-n_all - 3:-3]
        send_sems, recv_sems, local_sems = refs[-3:]
        x, y, c = _place()
        chip = 2 * x + y
        part = lambda a, j: _gathered_part(outs[a], items[a][0].shape, items[a][1], j)
        local = [pltpu.make_async_copy(ins[a], part(a, chip), local_sems.at[a]) for a in range(n_all)]
        for cp in local[:n]:
            cp.start()
        sends = []
        for a in range(n):
            for k in range(1, N_CHIPS):
                peer = (_flip(x, k & 2), _flip(y, k & 1), c)
                cp = _remote(ins[a], part(a, chip), send_sems.at[_sem(a, k)], recv_sems.at[_sem(a, k)], peer)
                cp.start()
                sends.append(cp)
        for cp in local[n:]:
            cp.start()
        for a in range(n):
            for k in range(1, N_CHIPS):
                src = 2 * _flip(x, k & 2) + _flip(y, k & 1)
                _remote(ins[a], part(a, src), send_sems.at[_sem(a, k)], recv_sems.at[_sem(a, k)], (x, y, c)).wait_recv()
        for cp in sends:
            cp.wait_send()
        for cp in local:
            cp.wait()

    return pl.pallas_call(
        body, name="gather_chips", in_specs=[_HBM_SPEC] * (n_all + len(after)), out_specs=[_HBM_SPEC] * n_all,
        out_shape=[_sds(_gathered_shape(a.shape, how), a.dtype) for a, how in items],
        scratch_shapes=[pltpu.SemaphoreType.DMA((n * (N_CHIPS - 1),)), pltpu.SemaphoreType.DMA((n * (N_CHIPS - 1),)),
                        pltpu.SemaphoreType.DMA((n_all,))])(*arrays, *after)


def _gather_sum_all(small):
    r, w = small.shape

    def body(in_ref, all_ref, sum_ref, send_sems, recv_sems):
        x, y, c = _place()
        me = 4 * x + 2 * y + c
        all_ref[me] = in_ref[...]
        sends = []
        for k in range(1, N_DEV):
            peer = (_flip(x, k & 4), _flip(y, k & 2), _flip(c, k & 1))
            cp = _remote(in_ref, all_ref.at[me], send_sems.at[k - 1], recv_sems.at[k - 1], peer)
            cp.start()
            sends.append(cp)
        for k in range(1, N_DEV):
            src = 4 * _flip(x, k & 4) + 2 * _flip(y, k & 2) + _flip(c, k & 1)
            _remote(in_ref, all_ref.at[src], send_sems.at[k - 1], recv_sems.at[k - 1], (x, y, c)).wait_recv()
        acc = all_ref[0]
        for e in range(1, N_DEV):
            acc = acc + all_ref[e]
        sum_ref[...] = acc
        for cp in sends:
            cp.wait_send()

    return pl.pallas_call(
        body, name="gather_sum_all", in_specs=[_VMEM_SPEC], out_specs=[_VMEM_SPEC] * 2,
        out_shape=[_sds((N_DEV, r, w)), _sds((r, w))],
        scratch_shapes=[pltpu.SemaphoreType.DMA((N_DEV - 1,)), pltpu.SemaphoreType.DMA((N_DEV - 1,))],
        compiler_params=pltpu.CompilerParams(vmem_limit_bytes=VMEM_LIMIT_V7X))(small)


def _add_chips(buf, t, chip_idx):
    r, cc = buf.shape[1:]
    tr = min(256, r)

    def body(c_ref, p_ref, t_ref, o_ref):
        del c_ref
        o_ref[...] = p_ref[0] + t_ref[0].astype(F32) + t_ref[1].astype(F32) + t_ref[2].astype(F32)

    return pl.pallas_call(
        body, name="add_chips", out_shape=_sds((r, cc)),
        grid_spec=pltpu.PrefetchScalarGridSpec(
            num_scalar_prefetch=1, grid=(r // tr,),
            in_specs=[pl.BlockSpec((1, tr, cc), lambda i, c: (c[0], i, 0)),
                      pl.BlockSpec((N_CHIPS - 1, tr, cc), lambda i, c: (0, i, 0))],
            out_specs=pl.BlockSpec((tr, cc), lambda i, c: (i, 0))),
        compiler_params=_cp(1))(chip_idx, buf, t)


def _rs_sibling(qs):
    n = len(qs)

    def body(*refs):
        ins, outs = refs[:n], refs[n:2 * n]
        send_sems, recv_sems = refs[2 * n:]
        x, y, c = _place()
        copies = [_remote(ins[a], outs[a], send_sems.at[a], recv_sems.at[a], (x, y, 1 - c)) for a in range(n)]
        for cp in copies:
            cp.start()
        for cp in copies:
            cp.wait()

    return pl.pallas_call(
        body, name="rs_sibling", in_specs=[_HBM_SPEC] * n, out_specs=[_HBM_SPEC] * n,
        out_shape=[_sds(q.shape) for q in qs],
        scratch_shapes=[pltpu.SemaphoreType.DMA((n,)), pltpu.SemaphoreType.DMA((n,))])(*qs)


def _adamw_update(w, g, m, v):
    m = ADAM_B1 * m + (1.0 - ADAM_B1) * g
    v = ADAM_B2 * v + (1.0 - ADAM_B2) * jnp.square(g)
    m_hat = m / (1.0 - ADAM_B1 ** ADAM_STEP)
    v_hat = v / (1.0 - ADAM_B2 ** ADAM_STEP)
    return -ADAM_LR * (m_hat / (jnp.sqrt(v_hat) + ADAM_EPS) + ADAM_WD * w), m, v


def _adamw(w, g_parts, m, v, name):
    shape = w.shape
    cols = shape[-1]
    rows = _size(shape[:-1])
    tr = 512 if rows % 512 == 0 else rows
    spec = pl.BlockSpec((tr, cols), lambda i: (i, 0))
    n = len(g_parts)

    def body(*refs):
        w_ref, m_ref, v_ref = refs[:3]
        g_ref, d_ref, nm_ref, nv_ref = refs[3 + n:]
        g = refs[3][...]
        for r in refs[4:3 + n]:
            g = g + r[...]
        g_ref[...] = g
        d_ref[...], nm_ref[...], nv_ref[...] = _adamw_update(w_ref[...], g, m_ref[...], v_ref[...])

    outs = pl.pallas_call(
        body, name="adamw_" + name, grid=(rows // tr,), in_specs=[spec] * (3 + n), out_specs=[spec] * 4,
        out_shape=[_sds((rows, cols))] * 4, compiler_params=_cp(1))(
            *[a.reshape(rows, cols) for a in (w, m, v, *g_parts)])
    return tuple(o.reshape(shape) for o in outs)


def _adamw_layer(w, g_parts, m, v, layer, prev, name):
    _, r, cc = w.shape
    tr = 512 if r % 512 == 0 else r
    spec = pl.BlockSpec((1, tr, cc), lambda i: (layer, i, 0))
    n = len(g_parts)

    def body(*refs):
        w_ref, m_ref, v_ref = refs[:3]
        g_ref, d_ref, nm_ref, nv_ref = refs[-4:]
        g = refs[3][...]
        for q in refs[4:3 + n]:
            g = g + q[...]
        g = g[:, :cc]
        g_ref[0] = g
        d_ref[0], nm_ref[0], nv_ref[0] = _adamw_update(w_ref[0], g, m_ref[0], v_ref[0])

    g_specs = [pl.BlockSpec((tr, q.shape[1]), lambda i: (i, 0)) for q in g_parts]
    passed = () if prev is None else tuple(prev)
    return pl.pallas_call(
        body, name="adamw_" + name, grid=(r // tr,),
        in_specs=[spec] * 3 + g_specs + [_HBM_SPEC] * len(passed), out_specs=[spec] * 4,
        out_shape=[_sds(w.shape)] * 4, input_output_aliases={3 + n + k: k for k in range(len(passed))},
        compiler_params=_cp(1))(w, m, v, *g_parts, *passed)


def _size(shape):
    n = 1
    for s in shape:
        n *= s
    return n


_SMALL = (("dmod", (DEPTH, 3 * D_MODEL)), ("pre_norm_g", (DEPTH, D_MODEL)), ("post_norm_g", (DEPTH, D_MODEL)),
          ("even_sc_conv_w", (2, SC_KERNEL, SC_WIDTH)), ("even_sc_conv_b", (2, SC_WIDTH)),
          ("even_q_norm_g", (2, Q_LORA)), ("even_kv_norm_g", (2, KV_LORA)),
          ("odd_conv_w", (2, CONF_KERNEL, D_MODEL)), ("odd_conv_b", (2, D_MODEL)), ("odd_ln_g", (2, D_MODEL)),
          ("odd_ln_b", (2, D_MODEL)))
SMALL_ROWS = -(-sum(_size(s) for _, s in _SMALL) // (8 * 128)) * 8

_SMALL_W = (("even_sc_conv_w", (2, SC_KERNEL, SC_WIDTH // N_CHIPS)), ("odd_conv_w", (2, CONF_KERNEL, D_MODEL // N_CHIPS)),
            ("odd_conv_b", (2, D_MODEL // N_CHIPS)), ("odd_ln_g", (2, D_MODEL // N_CHIPS)),
            ("odd_ln_b", (2, D_MODEL // N_CHIPS)))
SMALL_W_ROWS = -(-sum(_size(s) for _, s in _SMALL_W) // (8 * 128)) * 8


def _pack_rows(arrays, layout, rows):
    flat = jnp.concatenate([arrays[n].reshape(-1) for n, _ in layout])
    return jnp.pad(flat, (0, rows * 128 - flat.shape[0])).reshape(rows, 128)


def _unpack_small(t):
    flat = t.reshape(-1)
    out, at = {}, 0
    for n, shape in _SMALL:
        out[n] = flat[at:at + _size(shape)].reshape(shape)
        at += _size(shape)
    return out


def _unpack_small_w(t):
    flat = t.reshape(N_CHIPS, -1)
    out, at = {}, 0
    for n, shape in _SMALL_W:
        a = flat[:, at:at + _size(shape)].reshape((N_CHIPS,) + shape)
        out[n] = jnp.moveaxis(a, 0, -2).reshape(shape[:-1] + (N_CHIPS * shape[-1],))
        at += _size(shape)
    return out


def _chip_cols(a, chip):
    n = a.shape[-1] // N_CHIPS
    return lax.dynamic_slice_in_dim(a, chip * n, n, axis=a.ndim - 1)


def _join_cols(a):
    _, l, r, cc = a.shape
    return a.transpose(1, 2, 0, 3).reshape(l, r, N_CHIPS * cc)


WEIGHT_NAMES = ("ada_w", "ada_b", "pre_norm_g", "post_norm_g", "even_w_in", "even_sc_conv_w", "even_sc_conv_b",
                "even_q_norm_g", "even_kv_norm_g", "even_w_uq", "even_w_ukv", "even_w_out", "odd_w_in", "odd_conv_w",
                "odd_conv_b", "odd_ln_g", "odd_ln_b", "odd_w_out")
GATHER_HOW = ((("even_w_in", "slot"), ("even_w_uq", "slot"), ("even_w_ukv", "slot"), ("even_w_out", "rows")),
              (("odd_w_in", "cols"), ("odd_w_out", "rows")))


def kernel(x, c, positions, ada_w, ada_b, pre_norm_g, post_norm_g, even_w_in, even_sc_conv_w, even_sc_conv_b, even_q_norm_g, even_kv_norm_g, even_w_uq, even_w_ukv, even_w_out, odd_w_in, odd_conv_w, odd_conv_b, odd_ln_g, odd_ln_b, odd_w_out, loss_target, m_ada_w, m_ada_b, m_pre_norm_g, m_post_norm_g, m_even_w_in, m_even_sc_conv_w, m_even_sc_conv_b, m_even_q_norm_g, m_even_kv_norm_g, m_even_w_uq, m_even_w_ukv, m_even_w_out, m_odd_w_in, m_odd_conv_w, m_odd_conv_b, m_odd_ln_g, m_odd_ln_b, m_odd_w_out, v_ada_w, v_ada_b, v_pre_norm_g, v_post_norm_g, v_even_w_in, v_even_sc_conv_w, v_even_sc_conv_b, v_even_q_norm_g, v_even_kv_norm_g, v_even_w_uq, v_even_w_ukv, v_even_w_out, v_odd_w_in, v_odd_conv_w, v_odd_conv_b, v_odd_ln_g, v_odd_ln_b, v_odd_w_out):
    w = dict(zip(WEIGHT_NAMES, (ada_w, ada_b, pre_norm_g, post_norm_g, even_w_in, even_sc_conv_w, even_sc_conv_b,
                                even_q_norm_g, even_kv_norm_g, even_w_uq, even_w_ukv, even_w_out, odd_w_in, odd_conv_w,
                                odd_conv_b, odd_ln_g, odd_ln_b, odd_w_out)))
    m = dict(zip(WEIGHT_NAMES, (m_ada_w, m_ada_b, m_pre_norm_g, m_post_norm_g, m_even_w_in, m_even_sc_conv_w,
                                m_even_sc_conv_b, m_even_q_norm_g, m_even_kv_norm_g, m_even_w_uq, m_even_w_ukv,
                                m_even_w_out, m_odd_w_in, m_odd_conv_w, m_odd_conv_b, m_odd_ln_g, m_odd_ln_b, m_odd_w_out)))
    v = dict(zip(WEIGHT_NAMES, (v_ada_w, v_ada_b, v_pre_norm_g, v_post_norm_g, v_even_w_in, v_even_sc_conv_w,
                                v_even_sc_conv_b, v_even_q_norm_g, v_even_kv_norm_g, v_even_w_uq, v_even_w_ukv,
                                v_even_w_out, v_odd_w_in, v_odd_conv_w, v_odd_conv_b, v_odd_ln_g, v_odd_ln_b, v_odd_w_out)))
    ix, iy, ic = _place()
    chip = 2 * ix + iy
    me = 2 * chip + ic
    s = x.shape[1]

    c_all, mod_all = _ada_fwd(jnp.broadcast_to(c, (8, D_MODEL)), ada_w, _chip_cols(ada_b, chip))
    mod = lax.dynamic_index_in_dim(mod_all, me, axis=2, keepdims=False)
    mod = mod.transpose(1, 0, 2).reshape(DEPTH, 3 * D_MODEL)

    items = [[(w[n][layer // 2].astype(MXU_DTYPE), how) for n, how in GATHER_HOW[layer % 2]] for layer in range(DEPTH)]
    later_items = [item for layer_items in items[1:] for item in layer_items]
    first = _gather_chips(items[0] + [(_pack_rows(w, _SMALL_W, SMALL_W_ROWS), "slot")], [mod_all])
    small_w = _unpack_small_w(first[len(items[0])])
    weights_sent = _gather_start(later_items, [_place_own(a, how, chip.reshape(1)) for a, how in later_items],
                                 "gather_start", [first[0]])
    later = []

    def layer_weights(layer, x_in):
        i = layer // 2
        if layer == 0:
            arrays = first[:len(items[0])]
        else:
            if not later:
                later.extend(_gather_wait(later_items, weights_sent, [x_in], "gather_wait"))
            at = sum(len(layer_items) for layer_items in items[1:layer])
            arrays = later[at:at + len(items[layer])]
        if layer % 2 == 0:
            ein, uq, ukv, eout = arrays
            wuk, wuv = _ukv_to_heads(ukv)
            wq, wq_rot = _uq_to_heads(uq)
            return {"w_in": _ein_from_shards(ein), "wq": wq, "wq_rot": wq_rot, "wuk": wuk, "wuv": wuv, "w_out": eout,
                    "sc_conv_w": small_w["even_sc_conv_w"][i]}
        oin, oout = arrays
        return {"w_in": oin, "w_out": oout, "conv_w": small_w["odd_conv_w"][i], "conv_b": small_w["odd_conv_b"][i:i + 1],
                "ln_g": small_w["odd_ln_g"][i:i + 1], "ln_b": small_w["odd_ln_b"][i:i + 1]}

    in_flight, own, sib, last = {}, {}, {}, {}

    def land(layer, after):
        names, started, kept = in_flight.pop(layer)
        bufs, arrived = _rs_wait(started, after, "rs_wait_%d" % layer)
        sums = [_add_chips(b, t, chip.reshape(1)) for b, t in zip(bufs if kept is None else kept, arrived)]
        for n, mine, theirs in zip(names, sums, _rs_sibling(sums)):
            own[n, layer // 2], sib[n, layer // 2] = mine, theirs

    def grads_done(layer, bufs, dx_in):
        if layer + 1 in in_flight:
            land(layer + 1, [dx_in])
        if layer == 0:
            last.update(bufs)
            return None
        names = sorted(bufs)
        in_flight[layer] = (names, _rs_start([bufs[n] for n in names], "rs_start_%d" % layer), None)
        return in_flight[layer][1][-1]

    p = {"pre_norm_g": pre_norm_g, "post_norm_g": post_norm_g, "even_sc_conv_b": even_sc_conv_b,
         "even_q_norm_g": even_q_norm_g, "even_kv_norm_g": even_kv_norm_g}
    inv_freq = 1.0 / (ROPE_THETA ** (jnp.arange(0, QK_ROPE, 2, dtype=F32) / QK_ROPE))
    inv_freq = jnp.zeros((1, HEAD_PAD), F32).at[0, QK_NOPE:QK_NOPE + QK_ROPE].set(jnp.tile(inv_freq, 2))
    cos, sin = _rope_tables(positions.reshape(s, 1), inv_freq)

    loss, dx, g = _local_step(x[0], loss_target[0], cos, sin, mod, p, layer_weights, weights_sent[-1], grads_done)

    grads, deltas, new_m, new_v = {}, {}, {}, {}

    def update_layers(n, results, pairs):
        for i in pairs:
            results = _adamw_layer(w[n], [own[n, i], sib[n, i]], m[n], v[n], i, results, n)
        return results

    small_all, small_sum = _gather_sum_all(_pack_rows(g, _SMALL, SMALL_ROWS))
    names = sorted(last)
    kept = [last[n] for n in names]
    in_flight[0] = (names, _rs_start([b.astype(jnp.bfloat16) for b in kept], "rs_start_0", [small_sum]), kept)
    tot = _unpack_small(small_sum)
    dmod_all = small_all[:, :DEPTH * 3 * D_MODEL // 128].reshape(N_DEV, DEPTH, 3 * D_MODEL)
    grads["ada_w"] = _ada_bwd(c_all[:, 0, :].T, _chip_cols(dmod_all, chip).transpose(1, 0, 2))
    grads["ada_b"] = tot["dmod"]
    for n in ("pre_norm_g", "post_norm_g", "even_sc_conv_b", "even_q_norm_g", "even_kv_norm_g"):
        grads[n] = tot[n]
    for n in ("even_sc_conv_w", "odd_conv_w", "odd_conv_b", "odd_ln_g", "odd_ln_b"):
        grads[n] = _chip_cols(tot[n], chip)
    for n in list(grads):
        _, deltas[n], new_m[n], new_v[n] = _adamw(w[n], [grads[n]], m[n], v[n], n)

    for n in ("odd_w_in", "odd_w_out"):
        grads[n], deltas[n], new_m[n], new_v[n] = update_layers(n, None, (1, 0))
    partly = {n: update_layers(n, None, (1,)) for n in ("even_w_in", "even_w_out")}
    land(0, [deltas["ada_w"], deltas["odd_w_in"], partly["even_w_in"][1]])
    for n in ("even_w_in", "even_w_out"):
        grads[n], deltas[n], new_m[n], new_v[n] = update_layers(n, partly[n], (0,))
    uq_parts, ukv_parts = zip(*[[jnp.stack(part) for part in zip(*[_mla_local(q["even_mla", i]) for i in range(N_PAIRS)])]
                                for q in (own, sib)])
    for n, parts in (("even_w_uq", uq_parts), ("even_w_ukv", ukv_parts)):
        grads[n], deltas[n], new_m[n], new_v[n] = _adamw(w[n], list(parts), m[n], v[n], n)

    total_loss = lax.psum(loss[0, 0], ("x", "y", "c"))
    return (total_loss, dx[None], *[grads[n] for n in WEIGHT_NAMES], *[deltas[n] for n in WEIGHT_NAMES],
            *[new_m[n] for n in WEIGHT_NAMES], *[new_v[n] for n in WEIGHT_NAMES])
```

```python
import functools

import jax
import jax.numpy as jnp
from jax import lax
from jax.experimental import pallas as pl
from jax.experimental.pallas import tpu as pltpu

F32 = jnp.float32
MXU_DTYPE = jnp.bfloat16
MESH = pl.DeviceIdType.MESH
VMEM_LIMIT_V7X = 56 * 2 ** 20

EPS = 1e-6
D_MODEL = 1024
DEPTH = 4
CHUNK = 64
SC_WIDTH = 512
SC_KERNEL = 3
SC_HALO = 8
HEADS = 8
QK_NOPE = 64
QK_ROPE = 32
V_HEAD = 64
HEAD_PAD = 128
Q_LORA = 256
KV_LORA = 128
ROPE_THETA = 10000.0
CONF_KERNEL = 31
CONF_HALO = 32
CONV_ROWS = 32
SUBLANES = 8
EVEN_IN = 2976
EVEN_PAD = 3072
ODD_IN = 3072
N_CHIPS = 4
N_DEV = 8
NEG = -1e30

ADAM_LR = 0.001
ADAM_B1 = 0.9
ADAM_B2 = 0.999
ADAM_EPS = 1e-08
ADAM_WD = 0.01
ADAM_STEP = 10

N_PAIRS = DEPTH // 2
EVEN_SHARD = EVEN_IN // N_CHIPS
EVEN_SHARD_PAD = 768
MLA_ROWS = Q_LORA + 2 * KV_LORA


def _cp(n_grid=0, **kw):
    return pltpu.CompilerParams(dimension_semantics=("arbitrary",) * n_grid,
                                vmem_limit_bytes=VMEM_LIMIT_V7X, **kw)


def _sigmoid(x):
    return 1.0 / (1.0 + jnp.exp(-x))


def _silu(x):
    return x * _sigmoid(x)


def _dsilu(x):
    s = _sigmoid(x)
    return s * (1.0 + x * (1.0 - s))


def _rms(x, g):
    return x * lax.rsqrt(jnp.mean(x * x, axis=-1, keepdims=True) + EPS) * g


def _dot(a, b, dims):
    return lax.dot_general(a.astype(MXU_DTYPE), b.astype(MXU_DTYPE), (dims, ((), ())),
                           preferred_element_type=F32)


def _dot_nn(a, b):
    return _dot(a, b, ((1,), (0,)))


def _dot_nt(a, b):
    return _dot(a, b, ((1,), (1,)))


def _dot_tn(a, b):
    return _dot(a, b, ((0,), (0,)))


def _rows(ts, w, cb=0):
    return pl.BlockSpec((ts, w), lambda i: (i, cb))


def _vec(w, cb=0, r=1):
    return pl.BlockSpec((r, w), lambda i: (0, cb))


def _prev_halo(ts, hr, w, cb):
    return pl.BlockSpec((hr, w), lambda i: (jnp.maximum(i * (ts // hr) - 1, 0), cb))


def _next_halo(ts, hr, w, cb, s):
    return pl.BlockSpec((hr, w), lambda i: (jnp.minimum((i + 1) * (ts // hr), s // hr - 1), cb))


def _sds(shape, dtype=F32):
    return jax.ShapeDtypeStruct(shape, dtype)


def _mm(a, b, mode, out_dtype, tm, tn, name):
    tm = min(tm, a.shape[1] if mode == "tn" else a.shape[0])
    tn = min(tn, b.shape[0] if mode == "nt" else b.shape[1])
    if mode == "nn":
        (m, k), n = a.shape, b.shape[1]
        a_spec = pl.BlockSpec((tm, k), lambda i, j: (i, 0))
        b_spec = pl.BlockSpec((k, tn), lambda i, j: (0, j))
        dot = _dot_nn
    elif mode == "nt":
        (m, k), n = a.shape, b.shape[0]
        a_spec = pl.BlockSpec((tm, k), lambda i, j: (i, 0))
        b_spec = pl.BlockSpec((tn, k), lambda i, j: (j, 0))
        dot = _dot_nt
    else:
        (k, m), n = a.shape, b.shape[1]
        a_spec = pl.BlockSpec((k, tm), lambda i, j: (0, i))
        b_spec = pl.BlockSpec((k, tn), lambda i, j: (0, j))
        dot = _dot_tn
    assert m % tm == 0 and n % tn == 0, (name, m, n, tm, tn)

    def body(a_ref, b_ref, o_ref):
        o_ref[...] = dot(a_ref[...], b_ref[...]).astype(o_ref.dtype)

    return pl.pallas_call(
        body, name=name, grid=(m // tm, n // tn), in_specs=[a_spec, b_spec],
        out_specs=pl.BlockSpec((tm, tn), lambda i, j: (i, j)), out_shape=_sds((m, n), out_dtype),
        compiler_params=_cp(2))(a, b)


def _mm_tn_shards(a, b, by, name):
    k, m = a.shape
    n = b.shape[1]
    if by == "cols":
        tm, tn = m, n // N_CHIPS
        shape, grid = (N_CHIPS, m, tn), (1, N_CHIPS)
        out_spec = pl.BlockSpec((1, tm, tn), lambda i, j: (j, i, 0))
    else:
        tm, tn = m // N_CHIPS, n
        shape, grid = (N_CHIPS, tm, n), (N_CHIPS, 1)
        out_spec = pl.BlockSpec((1, tm, tn), lambda i, j: (i, 0, j))

    def body(a_ref, b_ref, o_ref):
        o_ref[0] = _dot_tn(a_ref[...], b_ref[...])

    return pl.pallas_call(
        body, name=name, grid=grid,
        in_specs=[pl.BlockSpec((k, tm), lambda i, j: (0, i)), pl.BlockSpec((k, tn), lambda i, j: (0, j))],
        out_specs=out_spec, out_shape=_sds(shape), compiler_params=_cp(2))(a, b)


def _even_col(q):
    return q if q < 2432 else (q + 64 if q < 2464 else q + 96)


def _shard_pieces(j):
    lo, hi = EVEN_SHARD * j, EVEN_SHARD * (j + 1)
    cuts = [lo] + [b for b in (2432, 2464) if lo < b < hi] + [hi]
    return [(a - lo, _even_col(a), b - a) for a, b in zip(cuts[:-1], cuts[1:])]


def _ein_from_shards(w):
    _, d, _ = w.shape
    tr = 256

    def body(w_ref, o_ref):
        parts, at = [], 0
        for j in range(N_CHIPS):
            for d0, s0, n in _shard_pieces(j):
                if s0 > at:
                    parts.append(jnp.zeros((tr, s0 - at), F32))
                parts.append(w_ref[j, :, d0:d0 + n].astype(F32))
                at = s0 + n
        o_ref[...] = jnp.concatenate(parts, axis=1).astype(o_ref.dtype)

    return pl.pallas_call(
        body, name="ein_from_shards", grid=(d // tr,),
        in_specs=[pl.BlockSpec((N_CHIPS, tr, EVEN_SHARD), lambda i: (0, i, 0))],
        out_specs=_rows(tr, EVEN_PAD), out_shape=_sds((d, EVEN_PAD), w.dtype), compiler_params=_cp(1))(w)


def _ein_to_shards(dw):
    d = dw.shape[0]
    tr = 256

    def body(dw_ref, o_ref):
        for j in range(N_CHIPS):
            parts = [dw_ref[:, s0:s0 + n] for _, s0, n in _shard_pieces(j)]
            o_ref[j] = jnp.concatenate(parts + [jnp.zeros((tr, EVEN_SHARD_PAD - EVEN_SHARD), F32)], axis=1)

    return pl.pallas_call(
        body, name="ein_to_shards", grid=(d // tr,), in_specs=[_rows(tr, EVEN_PAD)],
        out_specs=pl.BlockSpec((N_CHIPS, tr, EVEN_SHARD_PAD), lambda i: (0, i, 0)),
        out_shape=_sds((N_CHIPS, d, EVEN_SHARD_PAD)), compiler_params=_cp(1))(dw)


def _rope_tables(pos_col, invf):
    s = pos_col.shape[0]
    ts = min(512, s)

    def body(p_ref, f_ref, c_ref, s_ref):
        ang = p_ref[...].astype(F32) * f_ref[...]
        lane = lax.broadcasted_iota(jnp.int32, ang.shape, 1)
        rope = (lane >= QK_NOPE) & (lane < QK_NOPE + QK_ROPE)
        c_ref[...] = jnp.where(lane < QK_NOPE, 1.0, jnp.where(rope, jnp.cos(ang), 0.0))
        s_ref[...] = jnp.where(rope, jnp.sin(ang), 0.0)

    return pl.pallas_call(
        body, name="rope_tables", grid=(s // ts,), in_specs=[_rows(ts, 1), _vec(HEAD_PAD)],
        out_specs=[_rows(ts, HEAD_PAD)] * 2, out_shape=[_sds((s, HEAD_PAD))] * 2,
        compiler_params=_cp(1))(pos_col, invf)


def _after(dep):
    return () if dep is None else (dep,)


def _pre_fwd(x, g, mod_l, ts, dep=None):
    s, d = x.shape

    def body(x_ref, g_ref, sh_ref, sc_ref, *rest):
        h = _rms(x_ref[...], g_ref[...]) * (1.0 + sc_ref[...]) + sh_ref[...]
        rest[-1][...] = h.astype(rest[-1].dtype)

    return pl.pallas_call(
        body, name="pre_fwd", grid=(s // ts,),
        in_specs=[_rows(ts, d), _vec(d), _vec(d, 0), _vec(d, 1)] + [_HBM_SPEC] * len(_after(dep)),
        out_specs=_rows(ts, d), out_shape=_sds((s, d), MXU_DTYPE), compiler_params=_cp(1))(
            x, g, mod_l, mod_l, *_after(dep))


def _pre_bwd(dz, w_in, dx_out, x, g, mod_l, ts):
    s, d = x.shape
    n_in = dz.shape[1]

    def f(xv, gv, sh, sc):
        return _rms(xv, gv) * (1.0 + sc) + sh

    def body(dz_ref, w_ref, dxo_ref, x_ref, g_ref, sh_ref, sc_ref, dx_ref, dsh_ref, dsc_ref, dg_ref):
        i = pl.program_id(0)
        _, vjp = jax.vjp(f, x_ref[...], g_ref[...], sh_ref[...], sc_ref[...])
        dx, dg, dsh, dsc = vjp(_dot_nt(dz_ref[...], w_ref[...]))
        dx_ref[...] = dxo_ref[...] + dx

        @pl.when(i == 0)
        def _():
            dsh_ref[...] = jnp.zeros_like(dsh_ref)
            dsc_ref[...] = jnp.zeros_like(dsc_ref)
            dg_ref[...] = jnp.zeros_like(dg_ref)

        dsh_ref[...] += dsh
        dsc_ref[...] += dsc
        dg_ref[...] += dg

    return pl.pallas_call(
        body, name="pre_bwd", grid=(s // ts,),
        in_specs=[_rows(ts, n_in), _vec(n_in, 0, d), _rows(ts, d), _rows(ts, d), _vec(d), _vec(d, 0), _vec(d, 1)],
        out_specs=[_rows(ts, d), _vec(d), _vec(d), _vec(d)],
        out_shape=[_sds((s, d)), _sds((1, d)), _sds((1, d)), _sds((1, d))],
        compiler_params=_cp(1))(dz, w_in, dx_out, x, g, mod_l, mod_l)


def _post_fwd(x, yo, g, mod_l, ts):
    s, d = x.shape

    def body(x_ref, yo_ref, g_ref, gate_ref, o_ref):
        o_ref[...] = x_ref[...] + gate_ref[...] * _rms(yo_ref[...], g_ref[...])

    return pl.pallas_call(
        body, name="post_fwd", grid=(s // ts,),
        in_specs=[_rows(ts, d), _rows(ts, d), _vec(d), _vec(d, 2)],
        out_specs=_rows(ts, d), out_shape=_sds((s, d)), compiler_params=_cp(1))(x, yo, g, mod_l)


def _post_bwd(dx_out, yo, g, mod_l, ts, dep=None):
    s, d = yo.shape

    def f(yov, gv, gate):
        return gate * _rms(yov, gv)

    def body(dx_ref, yo_ref, g_ref, gate_ref, *rest):
        dyo_ref, dgate_ref, dg_ref = rest[-3:]
        i = pl.program_id(0)
        _, vjp = jax.vjp(f, yo_ref[...], g_ref[...], gate_ref[...])
        dyo, dg, dgate = vjp(dx_ref[...])
        dyo_ref[...] = dyo.astype(dyo_ref.dtype)

        @pl.when(i == 0)
        def _():
            dgate_ref[...] = jnp.zeros_like(dgate_ref)
            dg_ref[...] = jnp.zeros_like(dg_ref)

        dgate_ref[...] += dgate
        dg_ref[...] += dg

    return pl.pallas_call(
        body, name="post_bwd", grid=(s // ts,),
        in_specs=[_rows(ts, d), _rows(ts, d), _vec(d), _vec(d, 2)] + [_HBM_SPEC] * len(_after(dep)),
        out_specs=[_rows(ts, d), _vec(d), _vec(d)],
        out_shape=[_sds((s, d), MXU_DTYPE), _sds((1, d)), _sds((1, d))],
        compiler_params=_cp(1))(dx_out, yo, g, mod_l, *_after(dep))


def _loss_fwd_bwd(x, target, ts):
    s, d = x.shape

    def body(x_ref, t_ref, loss_ref, dx_ref):
        i = pl.program_id(0)
        err = x_ref[...] - t_ref[...]
        dx_ref[...] = err * (1.0 / d)

        @pl.when(i == 0)
        def _():
            loss_ref[...] = jnp.zeros_like(loss_ref)

        loss_ref[...] += 0.5 * jnp.sum(jnp.sum(err * err, axis=-1, keepdims=True) * (1.0 / d), axis=0, keepdims=True)

    return pl.pallas_call(
        body, name="loss", grid=(s // ts,), in_specs=[_rows(ts, d), _rows(ts, d)],
        out_specs=[_vec(1), _rows(ts, d)], out_shape=[_sds((1, 1)), _sds((s, d))],
        compiler_params=_cp(1))(x, target)


def _rope(t, cos, sin):
    lane = lax.broadcasted_iota(jnp.int32, t.shape, 1)
    first = (lane >= QK_NOPE) & (lane < QK_NOPE + QK_ROPE // 2)
    second = (lane >= QK_NOPE + QK_ROPE // 2) & (lane < QK_NOPE + QK_ROPE)
    up = pltpu.roll(t, QK_ROPE // 2, 1)
    down = pltpu.roll(t, HEAD_PAD - QK_ROPE // 2, 1)
    return t * cos + jnp.where(first, -down, jnp.where(second, up, 0.0)) * sin


def _rope_transposed(g, cos, sin):
    lane = lax.broadcasted_iota(jnp.int32, g.shape, 1)
    first = (lane >= QK_NOPE) & (lane < QK_NOPE + QK_ROPE // 2)
    second = (lane >= QK_NOPE + QK_ROPE // 2) & (lane < QK_NOPE + QK_ROPE)
    u = g * sin
    up = pltpu.roll(u, QK_ROPE // 2, 1)
    down = pltpu.roll(u, HEAD_PAD - QK_ROPE // 2, 1)
    return g * cos + jnp.where(first, down, jnp.where(second, -up, 0.0))


def _mla_prep_fwd(z, cos, sin, qg, kvg, wq, wq_rot, wuk, wuv, ts):
    s = z.shape[0]
    wide = HEADS * HEAD_PAD

    def body(cq_ref, ckv_ref, kr_ref, cos_ref, sin_ref, qg_ref, kvg_ref, wq_ref, wqr_ref, wuk_ref, wuv_ref,
             q_ref, qt_ref, k_ref, v_ref):
        cos_v, sin_v = cos_ref[...], sin_ref[...]
        cqn = _rms(cq_ref[...], qg_ref[...])
        ckvn = _rms(ckv_ref[...], kvg_ref[...])
        kr = _rope(kr_ref[...], cos_v, sin_v)
        q_lin, q_rot = _dot_nn(cqn, wq_ref[...]), _dot_nn(cqn, wqr_ref[...])
        k_lin, v_all = _dot_nn(ckvn, wuk_ref[...]), _dot_nn(ckvn, wuv_ref[...])
        for h in range(HEADS):
            lanes = slice(h * HEAD_PAD, (h + 1) * HEAD_PAD)
            qh = q_lin[:, lanes] * cos_v + q_rot[:, lanes] * sin_v
            q_ref[h] = qh.astype(q_ref.dtype)
            qt_ref[h, 0] = qh.T.astype(qt_ref.dtype)
            k_ref[h] = (k_lin[:, lanes] + kr).astype(k_ref.dtype)
            v_ref[h] = v_all[:, lanes].astype(v_ref.dtype)

    out = pl.BlockSpec((HEADS, ts, HEAD_PAD), lambda i: (0, i, 0))
    return pl.pallas_call(
        body, name="mla_prep_fwd", grid=(s // ts,),
        in_specs=[_rows(ts, Q_LORA, 8), _rows(ts, KV_LORA, 18), _rows(ts, HEAD_PAD, 19), _rows(ts, HEAD_PAD), _rows(ts, HEAD_PAD),
                  _vec(Q_LORA), _vec(KV_LORA), _vec(wide, 0, Q_LORA), _vec(wide, 0, Q_LORA), _vec(wide, 0, KV_LORA),
                  _vec(wide, 0, KV_LORA)],
        out_specs=[out, pl.BlockSpec((HEADS, 1, HEAD_PAD, ts), lambda i: (0, i, 0, 0)), out, out],
        out_shape=[_sds((HEADS, s, HEAD_PAD), MXU_DTYPE), _sds((HEADS, s // ts, HEAD_PAD, ts), MXU_DTYPE)]
        + [_sds((HEADS, s, HEAD_PAD), MXU_DTYPE)] * 2,
        compiler_params=_cp(1))(z, z, z, cos, sin, qg, kvg, wq, wq_rot, wuk, wuv)


def _mla_prep_bwd(dz, dq, dk, dv, z, cos, sin, qg, kvg, wq, wuk, wuv, ts):
    s = z.shape[0]

    def fq(cq, g):
        return _rms(cq, g)

    def body(dz_in_ref, dq_ref, dk_ref, dv_ref, cq_ref, ckv_ref, cos_ref, sin_ref, qg_ref, kvg_ref, wq_ref, wuk_ref,
             wuv_ref, dz_ref, dw_ref, dqg_ref, dkvg_ref):
        del dz_in_ref
        cos_v, sin_v = cos_ref[...], sin_ref[...]

        @pl.when(pl.program_id(0) == 0)
        def _():
            dw_ref[...] = jnp.zeros_like(dw_ref)
            dqg_ref[...] = jnp.zeros_like(dqg_ref)
            dkvg_ref[...] = jnp.zeros_like(dkvg_ref)

        cqn, vjp_q = jax.vjp(fq, cq_ref[...], qg_ref[...])
        ckvn, vjp_kv = jax.vjp(fq, ckv_ref[...], kvg_ref[...])
        lane = lax.broadcasted_iota(jnp.int32, (ts, HEAD_PAD), 1)
        rope_lanes = (lane >= QK_NOPE) & (lane < QK_NOPE + QK_ROPE)
        dq_lin = jnp.concatenate([_rope_transposed(dq_ref[h], cos_v, sin_v).astype(MXU_DTYPE) for h in range(HEADS)], axis=1)
        dk_all = jnp.concatenate([dk_ref[h].astype(MXU_DTYPE) for h in range(HEADS)], axis=1)
        dv_all = jnp.concatenate([dv_ref[h].astype(MXU_DTYPE) for h in range(HEADS)], axis=1)
        dkr = jnp.where(rope_lanes, dk_ref[0], 0.0)
        for h in range(1, HEADS):
            dkr = dkr + jnp.where(rope_lanes, dk_ref[h], 0.0)
        dcq, dqg = vjp_q(_dot_nt(dq_lin, wq_ref[...]))
        dckv, dkvg = vjp_kv(_dot_nt(dk_all, wuk_ref[...]) + _dot_nt(dv_all, wuv_ref[...]))
        dz_ref[:, 0:Q_LORA] = dcq.astype(dz_ref.dtype)
        dz_ref[:, Q_LORA:Q_LORA + KV_LORA] = dckv.astype(dz_ref.dtype)
        dz_ref[:, Q_LORA + KV_LORA:] = _rope_transposed(dkr, cos_v, sin_v).astype(dz_ref.dtype)
        dqg_ref[...] += dqg
        dkvg_ref[...] += dkvg
        dwq, dwuk, dwuv = _dot_tn(cqn, dq_lin), _dot_tn(ckvn, dk_all), _dot_tn(ckvn, dv_all)
        for h in range(HEADS):
            lanes = slice(h * HEAD_PAD, (h + 1) * HEAD_PAD)
            row0 = (h % 2) * MLA_ROWS
            dw_ref[h // 2, row0:row0 + Q_LORA, :] += dwq[:, lanes]
            dw_ref[h // 2, row0 + Q_LORA:row0 + Q_LORA + KV_LORA, :] += dwuk[:, lanes]
            dw_ref[h // 2, row0 + Q_LORA + KV_LORA:row0 + MLA_ROWS, :] += dwuv[:, lanes]

    wide = HEADS * HEAD_PAD
    heads = pl.BlockSpec((HEADS, ts, HEAD_PAD), lambda i: (0, i, 0))
    whole = pl.BlockSpec((N_CHIPS, 2 * MLA_ROWS, HEAD_PAD), lambda i: (0, 0, 0))
    return pl.pallas_call(
        body, name="mla_prep_bwd", grid=(s // ts,),
        in_specs=[_HBM_SPEC, heads, heads, heads, _rows(ts, Q_LORA, 8), _rows(ts, KV_LORA, 18),
                  _rows(ts, HEAD_PAD), _rows(ts, HEAD_PAD), _vec(Q_LORA), _vec(KV_LORA), _vec(wide, 0, Q_LORA),
                  _vec(wide, 0, KV_LORA), _vec(wide, 0, KV_LORA)],
        out_specs=[_rows(ts, 512, 4), whole, _vec(Q_LORA), _vec(KV_LORA)],
        out_shape=[_sds(dz.shape, dz.dtype), _sds((N_CHIPS, 2 * MLA_ROWS, HEAD_PAD)), _sds((1, Q_LORA)), _sds((1, KV_LORA))],
        input_output_aliases={0: 0}, compiler_params=_cp(1))(dz, dq, dk, dv, z, z, cos, sin, qg, kvg, wq, wuk, wuv)


def _chunk_mask(q0, k0, tq, tk):
    rows = q0 + lax.broadcasted_iota(jnp.int32, (tq, tk), 0)
    cols = k0 + lax.broadcasted_iota(jnp.int32, (tq, tk), 1)
    return lax.shift_right_logical(cols, 6) <= lax.shift_right_logical(rows, 6)


def _attn_fwd(q, k, v, tq):
    s = q.shape[1]
    nq = s // tq
    scale = 1.0 / float(QK_NOPE + QK_ROPE) ** 0.5

    def body(q_ref, k_ref, v_ref, o_ref, lse_ref):
        qi, hh = pl.program_id(1), pl.program_id(2)
        qv = q_ref[0]

        def step(kj, carry, masked):
            m, l, acc = carry
            k0 = pl.multiple_of(kj * tq, tq)
            sc = _dot_nt(qv, k_ref[0, pl.ds(k0, tq), :]) * scale
            if masked:
                sc = jnp.where(_chunk_mask(qi * tq, k0, tq, tq), sc, NEG)
            m_new = jnp.maximum(m, jnp.max(sc, axis=-1, keepdims=True))
            alpha = jnp.exp(m - m_new)
            p = jnp.exp(sc - m_new)
            l = alpha * l + jnp.sum(p, axis=-1, keepdims=True)
            acc = alpha * acc + _dot_nn(p, v_ref[0, pl.ds(k0, tq), :])
            return m_new, l, acc

        init = (jnp.full((tq, 1), NEG, F32), jnp.zeros((tq, 1), F32), jnp.zeros((tq, HEAD_PAD), F32))
        carry = lax.fori_loop(0, qi, lambda kj, c: step(kj, c, False), init)
        m, l, acc = step(qi, carry, True)
        o = acc / l
        lse_ref[0] = m + jnp.log(l)

        @pl.when(hh == 0)
        def _():
            o_ref[...] = o

        @pl.when(hh == 1)
        def _():
            o_ref[...] += o

    head = lambda hp, qi, hh: 2 * hp + hh
    return pl.pallas_call(
        body, name="attn_fwd", grid=(HEADS // 2, nq, 2),
        in_specs=[pl.BlockSpec((1, tq, HEAD_PAD), lambda hp, qi, hh: (head(hp, qi, hh), qi, 0)),
                  pl.BlockSpec((1, s, HEAD_PAD), lambda hp, qi, hh: (head(hp, qi, hh), 0, 0)),
                  pl.BlockSpec((1, s, HEAD_PAD), lambda hp, qi, hh: (head(hp, qi, hh), 0, 0))],
        out_specs=[pl.BlockSpec((tq, HEAD_PAD), lambda hp, qi, hh: (qi, hp)),
                   pl.BlockSpec((1, tq, 1), lambda hp, qi, hh: (head(hp, qi, hh), qi, 0))],
        out_shape=[_sds((s, HEADS * V_HEAD)), _sds((HEADS, s, 1))],
        compiler_params=_cp(3))(q, k, v)


def _attn_bwd(q, q_t, k, v, do, do_t, o, lse, tq):
    s = q.shape[1]
    nq = s // tq
    per_q = tq // do_t.shape[3]
    scale = 1.0 / float(QK_NOPE + QK_ROPE) ** 0.5

    def body(q_ref, qt_ref, k_ref, v_ref, do_ref, dot_ref, o_ref, lse_ref, dq_ref, dk_ref, dv_ref, dk_t, dv_t):
        hh, kj = pl.program_id(1), pl.program_id(2)

        @pl.when(kj == 0)
        def _():
            dq_ref[...] = jnp.zeros_like(dq_ref)

        kv, vv = k_ref[0], v_ref[0]
        lane = lax.broadcasted_iota(jnp.int32, (tq, HEAD_PAD), 1)
        mine = lax.shift_right_logical(lane, 6) == hh
        dk_t[...] = jnp.zeros_like(dk_t)
        dv_t[...] = jnp.zeros_like(dv_t)

        def step(qi, masked):
            q0 = pl.multiple_of(qi * tq, tq)
            qv = q_ref[0, pl.ds(q0, tq), :]
            dov = do_ref[pl.ds(q0, tq), :]
            delta = jnp.sum(jnp.where(mine, dov * o_ref[pl.ds(q0, tq), :], 0.0), axis=-1, keepdims=True)
            sc = _dot_nt(qv, kv) * scale
            if masked:
                sc = jnp.where(_chunk_mask(q0, kj * tq, tq, tq), sc, NEG)
            p = jnp.exp(sc - lse_ref[0, pl.ds(q0, tq), :])
            ds = (p * (_dot_nt(dov, vv) - delta) * scale).astype(MXU_DTYPE)
            do_tv = jnp.concatenate([dot_ref[0, qi * per_q + r] for r in range(per_q)], axis=1)
            dv_t[...] += _dot_nn(do_tv, p)
            dk_t[...] += _dot_nn(qt_ref[0, qi], ds)
            dq_ref[0, pl.ds(q0, tq), :] += _dot_nn(ds, kv)

        step(kj, True)
        odd = (nq - 1 - kj) % 2

        @pl.when(odd == 1)
        def _():
            step(kj + 1, False)

        def two(i, c):
            step(kj + 1 + odd + 2 * i, False)
            step(kj + 2 + odd + 2 * i, False)
            return c

        lax.fori_loop(0, (nq - 1 - kj) // 2, two, 0)
        dk_ref[0] = dk_t[...].T
        dv_ref[0] = dv_t[...].T

    head = lambda hp, hh, kj: 2 * hp + hh
    full = pl.BlockSpec((1, s, HEAD_PAD), lambda hp, hh, kj: (head(hp, hh, kj), 0, 0))
    blk = pl.BlockSpec((1, tq, HEAD_PAD), lambda hp, hh, kj: (head(hp, hh, kj), kj, 0))
    pair = pl.BlockSpec((s, HEAD_PAD), lambda hp, hh, kj: (0, hp))
    return pl.pallas_call(
        body, name="attn_bwd", grid=(HEADS // 2, 2, nq),
        in_specs=[full, pl.BlockSpec((1,) + q_t.shape[1:], lambda hp, hh, kj: (head(hp, hh, kj), 0, 0, 0)), blk, blk,
                  pair, pl.BlockSpec((1,) + do_t.shape[1:], lambda hp, hh, kj: (hp, 0, 0, 0)), pair,
                  pl.BlockSpec((1, s, 1), lambda hp, hh, kj: (head(hp, hh, kj), 0, 0))],
        out_specs=[full, blk, blk], out_shape=[_sds((HEADS, s, HEAD_PAD))] * 3,
        scratch_shapes=[pltpu.VMEM((HEAD_PAD, tq), F32), pltpu.VMEM((HEAD_PAD, tq), F32)],
        compiler_params=_cp(3))(q, q_t, k, v, do, do_t, o, lse)


def _sc_conv(u, ubuf, w_ref, b_ref, ts):
    return (w_ref[2:3, :] * u + w_ref[1:2, :] * ubuf[pl.ds(SC_HALO - 1, ts), :]
            + w_ref[0:1, :] * ubuf[pl.ds(SC_HALO - 2, ts), :] + b_ref[...])


def _even_gate_fwd(z, o, sc_w, sc_b, ts):
    s = z.shape[0]
    w = SC_WIDTH

    def body(ab_ref, ac_ref, ax_ref, ag_ref, bg_ref, hc_ref, hx_ref, o_ref, w_ref, b_ref, y_ref, ubuf):
        i = pl.program_id(0)
        u = ac_ref[...] * ax_ref[...]
        ubuf[0:SC_HALO, :] = jnp.where(i > 0, hc_ref[...] * hx_ref[...], 0.0)
        ubuf[SC_HALO:, :] = u
        conv = _sc_conv(u, ubuf, w_ref, b_ref, ts)
        y_ref[:, 0:w] = (ab_ref[...] * conv * _silu(ag_ref[...])).astype(y_ref.dtype)
        y_ref[:, w:] = (o_ref[...] * _silu(bg_ref[...])).astype(y_ref.dtype)

    return pl.pallas_call(
        body, name="even_gate_fwd", grid=(s // ts,),
        in_specs=[_rows(ts, w, 0), _rows(ts, w, 1), _rows(ts, w, 2), _rows(ts, w, 3), _rows(ts, w, 5),
                  _prev_halo(ts, SC_HALO, w, 1), _prev_halo(ts, SC_HALO, w, 2), _rows(ts, w),
                  _vec(w, 0, SC_KERNEL), _vec(w)],
        out_specs=_rows(ts, 2 * w), out_shape=_sds((s, 2 * w), MXU_DTYPE),
        scratch_shapes=[pltpu.VMEM((ts + SC_HALO, w), F32)],
        compiler_params=_cp(1))(z, z, z, z, z, z, z, o, sc_w, sc_b)


def _even_gate_bwd(dy, z, o, sc_w, sc_b, ts):
    s = z.shape[0]
    w = SC_WIDTH
    n = s // ts

    def body(dya_ref, dyb_ref, dyan_ref, ab_ref, ac_ref, ax_ref, ag_ref, bg_ref, hc_ref, hx_ref, abn_ref, agn_ref,
             o_ref, w_ref, b_ref, dz_ref, do_ref, dot_ref, dw_ref, db_ref, ubuf, dbuf):
        i = pl.program_id(0)
        ab, ac, ax, ag, bg = ab_ref[...], ac_ref[...], ax_ref[...], ag_ref[...], bg_ref[...]
        dya, dyb = dya_ref[...], dyb_ref[...]
        u = ac * ax
        ubuf[0:SC_HALO, :] = jnp.where(i > 0, hc_ref[...] * hx_ref[...], 0.0)
        ubuf[SC_HALO:, :] = u
        conv = _sc_conv(u, ubuf, w_ref, b_ref, ts)
        sg = _silu(ag)
        dconv = dya * ab * sg
        dbuf[0:ts, :] = dconv
        dbuf[ts:, :] = jnp.where(i < n - 1, dyan_ref[...] * abn_ref[...] * _silu(agn_ref[...]), 0.0)
        du = w_ref[2:3, :] * dconv + w_ref[1:2, :] * dbuf[pl.ds(1, ts), :] + w_ref[0:1, :] * dbuf[pl.ds(2, ts), :]
        dz_ref[:, 0:w] = (dya * conv * sg).astype(dz_ref.dtype)
        dz_ref[:, w:2 * w] = (du * ax).astype(dz_ref.dtype)
        dz_ref[:, 2 * w:3 * w] = (du * ac).astype(dz_ref.dtype)
        dz_ref[:, 3 * w:4 * w] = (dya * ab * conv * _dsilu(ag)).astype(dz_ref.dtype)
        dz_ref[:, 4 * w:5 * w] = jnp.zeros((ts, w), dz_ref.dtype)
        dz_ref[:, 5 * w:] = (dyb * o_ref[...] * _dsilu(bg)).astype(dz_ref.dtype)
        do = dyb * _silu(bg)
        do_ref[...] = do
        for pair in range(HEADS // 2):
            dot_ref[pair, 0] = do[:, pair * HEAD_PAD:(pair + 1) * HEAD_PAD].T.astype(dot_ref.dtype)

        @pl.when(i == 0)
        def _():
            dw_ref[...] = jnp.zeros_like(dw_ref)
            db_ref[...] = jnp.zeros_like(db_ref)

        dw_ref[0:1, :] += jnp.sum(dconv * ubuf[pl.ds(SC_HALO - 2, ts), :], axis=0, keepdims=True)
        dw_ref[1:2, :] += jnp.sum(dconv * ubuf[pl.ds(SC_HALO - 1, ts), :], axis=0, keepdims=True)
        dw_ref[2:3, :] += jnp.sum(dconv * u, axis=0, keepdims=True)
        db_ref[...] += jnp.sum(dconv, axis=0, keepdims=True)

    return pl.pallas_call(
        body, name="even_gate_bwd", grid=(n,),
        in_specs=[_rows(ts, w, 0), _rows(ts, w, 1), _next_halo(ts, SC_HALO, w, 0, s),
                  _rows(ts, w, 0), _rows(ts, w, 1), _rows(ts, w, 2), _rows(ts, w, 3), _rows(ts, w, 5),
                  _prev_halo(ts, SC_HALO, w, 1), _prev_halo(ts, SC_HALO, w, 2),
                  _next_halo(ts, SC_HALO, w, 0, s), _next_halo(ts, SC_HALO, w, 3, s),
                  _rows(ts, w), _vec(w, 0, SC_KERNEL), _vec(w)],
        out_specs=[_rows(ts, EVEN_PAD), _rows(ts, w), pl.BlockSpec((HEADS // 2, 1, HEAD_PAD, ts), lambda i: (0, i, 0, 0)),
                   _vec(w, 0, SC_KERNEL), _vec(w)],
        out_shape=[_sds((s, EVEN_PAD), MXU_DTYPE), _sds((s, w)), _sds((HEADS // 2, n, HEAD_PAD, ts), MXU_DTYPE),
                   _sds((SC_KERNEL, w)), _sds((1, w))],
        scratch_shapes=[pltpu.VMEM((ts + SC_HALO, w), F32), pltpu.VMEM((ts + SC_HALO, w), F32)],
        compiler_params=_cp(1))(dy, dy, dy, z, z, z, z, z, z, z, z, z, o, sc_w, sc_b)


def _ln_act(uc, sg, g, b):
    mu = jnp.mean(uc, axis=-1, keepdims=True)
    var = jnp.mean(jnp.square(uc - mu), axis=-1, keepdims=True)
    return _silu((uc - mu) * lax.rsqrt(var + EPS) * g + b) * _silu(sg)


def _shifted_copies(buf, shifted, rows):
    for b in range(1, SUBLANES):
        shifted[b - 1, 0:rows, :] = buf[pl.ds(b, rows), :]


def _rows_at(buf, shifted, start, n):
    a, b = divmod(start, SUBLANES)
    return buf[pl.ds(SUBLANES * a, n), :] if b == 0 else shifted[b - 1, pl.ds(SUBLANES * a, n), :]


def _odd_fwd(z, conv_w, conv_b, ln_g, ln_b, ts):
    s = z.shape[0]
    d = D_MODEL
    k = CONF_KERNEL

    def body(val_ref, glu_ref, sg_ref, hval_ref, hglu_ref, w_ref, b_ref, g_ref, beta_ref, y_ref, uc_ref, ubuf, ush):
        i = pl.program_id(0)
        ubuf[0:CONF_HALO, :] = jnp.where(i > 0, hval_ref[...] * _sigmoid(hglu_ref[...]), 0.0)
        ubuf[CONF_HALO:, :] = val_ref[...] * _sigmoid(glu_ref[...])
        _shifted_copies(ubuf, ush, ts + CONF_HALO - SUBLANES)
        for r0 in range(0, ts, CONV_ROWS):
            acc = jnp.broadcast_to(b_ref[...], (CONV_ROWS, d))
            for j in range(k):
                acc = acc + w_ref[j:j + 1, :] * _rows_at(ubuf, ush, r0 + CONF_HALO - (k - 1) + j, CONV_ROWS)
            uc_ref[r0:r0 + CONV_ROWS, :] = acc
        y_ref[...] = _ln_act(uc_ref[...], sg_ref[...], g_ref[...], beta_ref[...]).astype(y_ref.dtype)

    return pl.pallas_call(
        body, name="odd_fwd", grid=(s // ts,),
        in_specs=[_rows(ts, d, 0), _rows(ts, d, 1), _rows(ts, d, 2),
                  _prev_halo(ts, CONF_HALO, d, 0), _prev_halo(ts, CONF_HALO, d, 1),
                  _vec(d, 0, k), _vec(d), _vec(d), _vec(d)],
        out_specs=[_rows(ts, d), _rows(ts, d)], out_shape=[_sds((s, d), MXU_DTYPE), _sds((s, d))],
        scratch_shapes=[pltpu.VMEM((ts + CONF_HALO, d), F32),
                        pltpu.VMEM((SUBLANES - 1, ts + CONF_HALO - SUBLANES, d), F32)],
        compiler_params=_cp(1))(z, z, z, z, z, conv_w, conv_b, ln_g, ln_b)


def _odd_bwd(dy, z, uc, conv_w, ln_g, ln_b, ts):
    s = z.shape[0]
    d = D_MODEL
    k = CONF_KERNEL
    n = s // ts

    def body(dy_ref, dyn_ref, val_ref, glu_ref, sg_ref, sgn_ref, uc_ref, ucn_ref,
             w_ref, g_ref, beta_ref, dz_ref, dw_ref, db_ref, dg_ref, dbeta_ref, dbuf, dsh, dw_acc):
        i = pl.program_id(0)
        val, glu = val_ref[...], glu_ref[...]
        sig = _sigmoid(glu)
        u = val * sig
        _, vjp = jax.vjp(_ln_act, uc_ref[...], sg_ref[...], g_ref[...], beta_ref[...])
        duc, dsg, dg, dbeta = vjp(dy_ref[...])
        _, vjp_n = jax.vjp(_ln_act, ucn_ref[...], sgn_ref[...], g_ref[...], beta_ref[...])
        dbuf[0:ts, :] = duc
        dbuf[ts:, :] = jnp.where(i < n - 1, vjp_n(dyn_ref[...])[0], 0.0)
        dz_ref[:, 2 * d:] = dsg.astype(dz_ref.dtype)
        _shifted_copies(dbuf, dsh, ts + CONF_HALO - SUBLANES)

        @pl.when(i == 0)
        def _():
            dw_acc[...] = jnp.zeros_like(dw_acc)
            db_ref[...] = jnp.zeros_like(db_ref)
            dg_ref[...] = jnp.zeros_like(dg_ref)
            dbeta_ref[...] = jnp.zeros_like(dbeta_ref)

        db_ref[...] += jnp.sum(duc, axis=0, keepdims=True)
        dg_ref[...] += dg
        dbeta_ref[...] += dbeta
        for r0 in range(0, ts, CONV_ROWS):
            acc = jnp.zeros((CONV_ROWS, d), F32)
            for j in range(k):
                acc = acc + w_ref[j:j + 1, :] * _rows_at(dbuf, dsh, r0 + (k - 1) - j, CONV_ROWS)
            sig_r = sig[r0:r0 + CONV_ROWS, :]
            dz_ref[r0:r0 + CONV_ROWS, 0:d] = (acc * sig_r).astype(dz_ref.dtype)
            dz_ref[r0:r0 + CONV_ROWS, d:2 * d] = (acc * val[r0:r0 + CONV_ROWS, :] * sig_r * (1.0 - sig_r)).astype(dz_ref.dtype)
        for j in range(k):
            prod = _rows_at(dbuf, dsh, (k - 1) - j, ts) * u
            dw_acc[j] += jnp.sum(prod.reshape(ts // SUBLANES, SUBLANES, d), axis=0)

        @pl.when(i == n - 1)
        def _():
            dw_ref[...] = jnp.sum(dw_acc[...], axis=1)

    return pl.pallas_call(
        body, name="odd_bwd", grid=(n,),
        in_specs=[_rows(ts, d), _next_halo(ts, CONF_HALO, d, 0, s),
                  _rows(ts, d, 0), _rows(ts, d, 1), _rows(ts, d, 2), _next_halo(ts, CONF_HALO, d, 2, s),
                  _rows(ts, d), _next_halo(ts, CONF_HALO, d, 0, s),
                  _vec(d, 0, k), _vec(d), _vec(d)],
        out_specs=[_rows(ts, ODD_IN), _vec(d, 0, k), _vec(d), _vec(d), _vec(d)],
        out_shape=[_sds((s, ODD_IN), MXU_DTYPE), _sds((k, d)), _sds((1, d)), _sds((1, d)), _sds((1, d))],
        scratch_shapes=[pltpu.VMEM((ts + CONF_HALO, d), F32),
                        pltpu.VMEM((SUBLANES - 1, ts + CONF_HALO - SUBLANES, d), F32), pltpu.VMEM((k, SUBLANES, d), F32)],
        compiler_params=_cp(1))(dy, dy, z, z, z, z, uc, uc, conv_w, ln_g, ln_b)


def _local_step(x, target, cos, sin, mod, p, layer_weights, fwd_dep=None, grads_done=None):
    s = x.shape[0]
    tsf, tsb = min(512, s // 2), min(256, s // 2)
    tq = min(512, s // 2)
    row1 = lambda a, i: a[i:i + 1]
    saved = []
    for layer in range(DEPTH):
        i = layer // 2
        mod_l = row1(mod, layer)
        h = _pre_fwd(x, row1(p["pre_norm_g"], layer), mod_l, tsf, fwd_dep if layer == 0 else None)
        wl = layer_weights(layer, h)
        if layer % 2 == 0:
            z = _mm(h, wl["w_in"], "nn", F32, 256, EVEN_PAD, "even_in_fwd")
            if "late" in wl:
                wl.update(wl.pop("late")(z))
            q, q_t, k, v = _mla_prep_fwd(z, cos, sin, row1(p["even_q_norm_g"], i), row1(p["even_kv_norm_g"], i),
                                    wl["wq"], wl["wq_rot"], wl["wuk"], wl["wuv"], tsf)
            o, lse = _attn_fwd(q, k, v, tq)
            y = _even_gate_fwd(z, o, wl["sc_conv_w"], row1(p["even_sc_conv_b"], i), tsf)
            yo = _mm(y, wl["w_out"], "nn", F32, 512, 1024, "even_out_fwd")
            saved.append((x, h, z, y, yo, wl, (q, q_t, k, v, o, lse)))
        else:
            z = _mm(h, wl["w_in"], "nn", F32, 256, ODD_IN, "odd_in_fwd")
            y, uc = _odd_fwd(z, wl["conv_w"], wl["conv_b"], wl["ln_g"], wl["ln_b"], tsf)
            yo = _mm(y, wl["w_out"], "nn", F32, 512, 1024, "odd_out_fwd")
            saved.append((x, h, z, y, yo, wl, uc))
        x = _post_fwd(x, yo, row1(p["post_norm_g"], layer), mod_l, tsf)

    loss, dx = _loss_fwd_bwd(x, target, tsf)

    g = {n: [None] * (DEPTH if n in ("pre_norm_g", "post_norm_g") else N_PAIRS) for n in (
        "pre_norm_g", "post_norm_g", "even_sc_conv_w", "even_sc_conv_b", "even_q_norm_g", "even_kv_norm_g",
        "odd_conv_w", "odd_conv_b", "odd_ln_g", "odd_ln_b")}
    dmod = [None] * DEPTH
    dep = None
    for layer in reversed(range(DEPTH)):
        i = layer // 2
        mod_l = row1(mod, layer)
        x_in, h, z, y, yo, wl, extra = saved[layer]
        dyo, dgate, g["post_norm_g"][layer] = _post_bwd(dx, yo, row1(p["post_norm_g"], layer), mod_l, tsb, dep)
        bufs = {}
        if layer % 2 == 0:
            q, q_t, k, v, o, lse = extra
            dy = _mm(dyo, wl["w_out"], "nt", F32, 512, 1024, "even_out_bwd_x")
            bufs["even_w_out"] = _mm_tn_shards(y, dyo, "rows", "even_out_bwd_w")
            dz, do, do_t, g["even_sc_conv_w"][i], g["even_sc_conv_b"][i] = _even_gate_bwd(
                dy, z, o, wl["sc_conv_w"], row1(p["even_sc_conv_b"], i), tsb)
            dq, dk, dv = _attn_bwd(q, q_t, k, v, do, do_t, o, lse, tq)
            dz, bufs["even_mla"], g["even_q_norm_g"][i], g["even_kv_norm_g"][i] = _mla_prep_bwd(
                dz, dq, dk, dv, z, cos, sin, row1(p["even_q_norm_g"], i), row1(p["even_kv_norm_g"], i),
                wl["wq"], wl["wuk"], wl["wuv"], tsb)
            bufs["even_w_in"] = _ein_to_shards(_mm(h, dz, "tn", F32, D_MODEL, 512, "even_in_bwd_w"))
        else:
            uc = extra
            dy = _mm(dyo, wl["w_out"], "nt", F32, 512, 1024, "odd_out_bwd_x")
            bufs["odd_w_out"] = _mm_tn_shards(y, dyo, "rows", "odd_out_bwd_w")
            dz, g["odd_conv_w"][i], g["odd_conv_b"][i], g["odd_ln_g"][i], g["odd_ln_b"][i] = _odd_bwd(
                dy, z, uc, wl["conv_w"], wl["ln_g"], wl["ln_b"], tsb)
            bufs["odd_w_in"] = _mm_tn_shards(h, dz, "cols", "odd_in_bwd_w")
        dx, dshift, dscale, g["pre_norm_g"][layer] = _pre_bwd(
            dz, wl["w_in"], dx, x_in, row1(p["pre_norm_g"], layer), mod_l, tsb)
        dmod[layer] = jnp.concatenate([dshift, dscale, dgate], axis=-1)
        dep = grads_done(layer, bufs, dx) if grads_done is not None else None
    stack = lambda parts: jnp.stack([a[0] if a.shape[0] == 1 and a.ndim == 2 else a for a in parts])
    small = {n: stack(parts) for n, parts in g.items()}
    small["dmod"] = jnp.concatenate(dmod, axis=0)
    return loss, dx, small


def _uq_to_heads(w):
    w = w.reshape(N_CHIPS, Q_LORA, 2, QK_NOPE + QK_ROPE).transpose(0, 2, 1, 3).reshape(HEADS, Q_LORA, QK_NOPE + QK_ROPE)
    half = QK_ROPE // 2
    rotated = jnp.concatenate([jnp.zeros_like(w[..., :QK_NOPE]), -w[..., QK_NOPE + half:], w[..., QK_NOPE:QK_NOPE + half]],
                              axis=-1)
    pad = ((0, 0), (0, 0), (0, HEAD_PAD - QK_NOPE - QK_ROPE))
    return _side_by_side(jnp.pad(w, pad)), _side_by_side(jnp.pad(rotated, pad))


def _side_by_side(w):
    return w.transpose(1, 0, 2).reshape(w.shape[1], HEADS * HEAD_PAD)


def _ukv_to_heads(w):
    w = w.reshape(N_CHIPS, KV_LORA, 2, QK_NOPE + V_HEAD).transpose(0, 2, 1, 3).reshape(HEADS, KV_LORA, QK_NOPE + V_HEAD)
    wk = jnp.pad(w[..., :QK_NOPE], ((0, 0), (0, 0), (0, HEAD_PAD - QK_NOPE)))
    wv = w[..., QK_NOPE:]
    zero = jnp.zeros_like(wv)
    odd = (jnp.arange(HEADS) % 2 == 1)[:, None, None]
    wv = jnp.concatenate([jnp.where(odd, zero, wv), jnp.where(odd, wv, zero)], axis=-1)
    return _side_by_side(wk), _side_by_side(wv)


def _mla_local(q):
    blocks = q.reshape(2, MLA_ROWS, HEAD_PAD)
    uq = jnp.concatenate([blocks[r, :Q_LORA, :QK_NOPE + QK_ROPE] for r in range(2)], axis=-1)
    ukv = jnp.concatenate(
        [jnp.concatenate([blocks[r, Q_LORA:Q_LORA + KV_LORA, :QK_NOPE],
                          blocks[r, Q_LORA + KV_LORA:, V_HEAD * r:V_HEAD * (r + 1)]], axis=-1) for r in range(2)], axis=-1)
    return uq, ukv


def _place():
    return lax.axis_index("x"), lax.axis_index("y"), lax.axis_index("c")


def _flip(v, bit):
    return 1 - v if bit else v


def _sem(a, k):
    return a * (N_CHIPS - 1) + k - 1


def _remote(src, dst, send_sem, recv_sem, peer):
    return pltpu.make_async_remote_copy(src_ref=src, dst_ref=dst, send_sem=send_sem, recv_sem=recv_sem,
                                        device_id=peer, device_id_type=MESH)


_VMEM_SPEC = pl.BlockSpec(memory_space=pltpu.VMEM)
_HBM_SPEC = pl.BlockSpec(memory_space=pl.ANY)


def _ada_fwd(c8, ada_w, ada_b_sh):
    depth, d, cols = ada_w.shape

    def body(c_ref, w_ref, b_ref, call_ref, mod_ref, s1, r1, s2, r2):
        x, y, c = _place()
        chip = 2 * x + y
        me = 2 * chip + c
        call_ref[me] = c_ref[...]
        sends = []
        for k in range(1, N_DEV):
            peer = (_flip(x, k & 4), _flip(y, k & 2), _flip(c, k & 1))
            cp = _remote(c_ref, call_ref.at[me], s1.at[k - 1], r1.at[k - 1], peer)
            cp.start()
            sends.append(cp)
        for k in range(1, N_DEV):
            src = 4 * _flip(x, k & 4) + 2 * _flip(y, k & 2) + _flip(c, k & 1)
            _remote(c_ref, call_ref.at[src], s1.at[k - 1], r1.at[k - 1], (x, y, c)).wait_recv()
        act = _silu(jnp.concatenate([call_ref[e, 0:1, :] for e in range(N_DEV)], axis=0))
        for l in range(depth):
            mod_ref[chip, l] = _dot_nn(act, w_ref[l]) + b_ref[l:l + 1, :]
        for k in range(1, N_CHIPS):
            peer = (_flip(x, k & 2), _flip(y, k & 1), c)
            cp = _remote(mod_ref.at[chip], mod_ref.at[chip], s2.at[k - 1], r2.at[k - 1], peer)
            cp.start()
            sends.append(cp)
        for k in range(1, N_CHIPS):
            src = 2 * _flip(x, k & 2) + _flip(y, k & 1)
            _remote(mod_ref.at[src], mod_ref.at[src], s2.at[k - 1], r2.at[k - 1], (x, y, c)).wait_recv()
        for cp in sends:
            cp.wait_send()

    return pl.pallas_call(
        body, name="ada_fwd", in_specs=[_VMEM_SPEC] * 3, out_specs=[_VMEM_SPEC] * 2,
        out_shape=[_sds((N_DEV, 8, d)), _sds((N_CHIPS, depth, N_DEV, cols))],
        scratch_shapes=[pltpu.SemaphoreType.DMA((N_DEV - 1,)), pltpu.SemaphoreType.DMA((N_DEV - 1,)),
                        pltpu.SemaphoreType.DMA((N_CHIPS - 1,)), pltpu.SemaphoreType.DMA((N_CHIPS - 1,))],
        compiler_params=pltpu.CompilerParams(vmem_limit_bytes=VMEM_LIMIT_V7X))(c8, ada_w, ada_b_sh)


def _ada_bwd(c_t, dmod_sh):
    depth, n, cols = dmod_sh.shape
    d = c_t.shape[0]
    tr = 256

    def body(c_ref, dm_ref, o_ref):
        act = _silu(c_ref[...])
        acc = act[:, 0:1] * dm_ref[0, 0:1, :]
        for e in range(1, n):
            acc = acc + act[:, e:e + 1] * dm_ref[0, e:e + 1, :]
        o_ref[0] = acc

    return pl.pallas_call(
        body, name="ada_bwd", grid=(depth, d // tr),
        in_specs=[pl.BlockSpec((tr, n), lambda l, i: (i, 0)), pl.BlockSpec((1, n, cols), lambda l, i: (l, 0, 0))],
        out_specs=pl.BlockSpec((1, tr, cols), lambda l, i: (l, i, 0)), out_shape=_sds((depth, d, cols)),
        compiler_params=_cp(2))(c_t, dmod_sh)


def _gathered_shape(shape, how):
    if how == "slot":
        return (N_CHIPS,) + shape
    r, cc = shape
    return (r, N_CHIPS * cc) if how == "cols" else (N_CHIPS * r, cc)


def _gathered_part(ref, shape, how, chip):
    if how == "slot":
        return ref.at[chip]
    if how == "cols":
        return ref.at[:, pl.ds(pl.multiple_of(chip * shape[1], 128), shape[1])]
    return ref.at[pl.ds(pl.multiple_of(chip * shape[0], 8), shape[0]), :]


_SEM_SPEC = pl.BlockSpec(memory_space=pltpu.SEMAPHORE)
_TOKEN = jax.ShapeDtypeStruct((8, 128), F32)
_SPLIT_COPY = pltpu.CompilerParams(has_side_effects=pltpu.SideEffectType.DATAFLOW_SIDE_EFFECTING)


def _in_hbm(a):
    return pltpu.with_memory_space_constraint(a, pltpu.HBM)


def _gather_start(items, gathered, name, after=()):
    n = len(items)

    def body(*refs):
        ins, outs = refs[:n], refs[n:2 * n]
        send_sems, recv_sems = refs[2 * n + len(after)], refs[2 * n + len(after) + 1]
        x, y, c = _place()
        for a in range(n):
            for k in range(1, N_CHIPS):
                part = _gathered_part(outs[a], items[a][0].shape, items[a][1], 2 * x + y)
                _remote(ins[a], part, send_sems.at[_sem(a, k)], recv_sems.at[_sem(a, k)],
                        (_flip(x, k & 2), _flip(y, k & 1), c)).start()
        refs[-1][...] = jnp.zeros(_TOKEN.shape, _TOKEN.dtype)

    arrays = [_in_hbm(a) for a, _ in items] + [_in_hbm(a) for a in gathered]
    res = pl.pallas_call(
        body, name=name, in_specs=[_HBM_SPEC] * (2 * n + len(after)),
        out_specs=[_SEM_SPEC, _SEM_SPEC] + [_HBM_SPEC] * (2 * n) + [_VMEM_SPEC],
        out_shape=[pltpu.SemaphoreType.DMA((n * (N_CHIPS - 1),)), pltpu.SemaphoreType.DMA((n * (N_CHIPS - 1),))]
        + [pltpu.HBM(a.shape, a.dtype) for a in arrays] + [_TOKEN],
        input_output_aliases={a: 2 + a for a in range(2 * n)}, compiler_params=_SPLIT_COPY)(*arrays, *after)
    return res[0], res[1], res[2:2 + n], res[2 + n:2 + 2 * n], res[-1]


def _gather_wait(items, started, after, name):
    n = len(items)
    send_sems, recv_sems, shards, gathered, _ = started

    def body(*refs):
        ins, outs, send_sems, recv_sems = refs[:n], refs[n:2 * n], refs[2 * n], refs[2 * n + 1]
        x, y, c = _place()
        for a in range(n):
            for k in range(1, N_CHIPS):
                part = _gathered_part(outs[a], items[a][0].shape, items[a][1], 2 * _flip(x, k & 2) + _flip(y, k & 1))
                cp = _remote(ins[a], part, send_sems.at[_sem(a, k)], recv_sems.at[_sem(a, k)], (x, y, c))
                cp.wait_send()
                cp.wait_recv()

    res = pl.pallas_call(
        body, name=name, in_specs=[_HBM_SPEC] * (2 * n) + [_SEM_SPEC, _SEM_SPEC] + [_HBM_SPEC] * len(after),
        out_specs=[_HBM_SPEC] * (2 * n), out_shape=[pltpu.HBM(a.shape, a.dtype) for a in (*shards, *gathered)],
        input_output_aliases={a: a for a in range(2 * n)}, compiler_params=_SPLIT_COPY)(
            *shards, *gathered, send_sems, recv_sems, *after)
    return res[n:]


def _rs_start(bufs, name, after=()):
    n = len(bufs)

    def body(*refs):
        srcs, lands = refs[:n], refs[n:2 * n]
        send_sems, recv_sems = refs[2 * n + len(after)], refs[2 * n + len(after) + 1]
        x, y, c = _place()
        for a in range(n):
            for k in range(1, N_CHIPS):
                tx, ty = _flip(x, k & 2), _flip(y, k & 1)
                _remote(srcs[a].at[2 * tx + ty], lands[a].at[k - 1], send_sems.at[_sem(a, k)], recv_sems.at[_sem(a, k)],
                        (tx, ty, c)).start()
        refs[-1][...] = jnp.zeros(_TOKEN.shape, _TOKEN.dtype)

    arrays = [_in_hbm(b) for b in bufs] + [_in_hbm(lax.empty((N_CHIPS - 1,) + b.shape[1:], b.dtype)) for b in bufs]
    res = pl.pallas_call(
        body, name=name, in_specs=[_HBM_SPEC] * (2 * n + len(after)),
        out_specs=[_SEM_SPEC, _SEM_SPEC] + [_HBM_SPEC] * (2 * n) + [_VMEM_SPEC],
        out_shape=[pltpu.SemaphoreType.DMA((n * (N_CHIPS - 1),)), pltpu.SemaphoreType.DMA((n * (N_CHIPS - 1),))]
        + [pltpu.HBM(a.shape, a.dtype) for a in arrays] + [_TOKEN],
        input_output_aliases={a: 2 + a for a in range(2 * n)}, compiler_params=_SPLIT_COPY)(*arrays, *after)
    return res[0], res[1], res[2:2 + n], res[2 + n:2 + 2 * n], res[-1]


def _rs_wait(started, after, name):
    send_sems, recv_sems, bufs, lands, _ = started
    n = len(bufs)

    def body(*refs):
        srcs, lnds, send_sems, recv_sems = refs[:n], refs[n:2 * n], refs[2 * n], refs[2 * n + 1]
        x, y, c = _place()
        for a in range(n):
            for k in range(1, N_CHIPS):
                cp = _remote(srcs[a].at[0], lnds[a].at[k - 1], send_sems.at[_sem(a, k)], recv_sems.at[_sem(a, k)], (x, y, c))
                cp.wait_send()
                cp.wait_recv()

    res = pl.pallas_call(
        body, name=name, in_specs=[_HBM_SPEC] * (2 * n) + [_SEM_SPEC, _SEM_SPEC] + [_HBM_SPEC] * len(after),
        out_specs=[_HBM_SPEC] * (2 * n), out_shape=[pltpu.HBM(a.shape, a.dtype) for a in (*bufs, *lands)],
        input_output_aliases={a: a for a in range(2 * n)}, compiler_params=_SPLIT_COPY)(
            *bufs, *lands, send_sems, recv_sems, *after)
    return res[:n], res[n:]


def _place_own(shard, how, chip_idx):
    r, cc = shard.shape
    block, index = {"slot": ((1, r, cc), lambda i, c: (c[0], 0, 0)), "cols": ((r, cc), lambda i, c: (0, c[0])),
                    "rows": ((r, cc), lambda i, c: (c[0], 0))}[how]

    def body(c_ref, in_ref, o_ref):
        del c_ref
        o_ref[...] = in_ref[...].reshape(o_ref.shape)

    return pl.pallas_call(
        body, name="place_own", out_shape=_sds(_gathered_shape(shard.shape, how), shard.dtype),
        grid_spec=pltpu.PrefetchScalarGridSpec(
            num_scalar_prefetch=1, grid=(1,), in_specs=[pl.BlockSpec((r, cc), lambda i, c: (0, 0))],
            out_specs=pl.BlockSpec(block, index)),
        compiler_params=_cp(1))(chip_idx, shard)


def _gather_chips(items, after=()):
    n = n_all = len(items)
    arrays = [a for a, _ in items]

    def body(*refs):
        ins, outs = refs[:n_all], refs[-n_all - 3:-3]
        send_sems, recv_sems, local_sems = refs[-3:]
        x, y, c = _place()
        chip = 2 * x + y
        part = lambda a, j: _gathered_part(outs[a], items[a][0].shape, items[a][1], j)
        local = [pltpu.make_async_copy(ins[a], part(a, chip), local_sems.at[a]) for a in range(n_all)]
        for cp in local[:n]:
            cp.start()
        sends = []
        for a in range(n):
            for k in range(1, N_CHIPS):
                peer = (_flip(x, k & 2), _flip(y, k & 1), c)
                cp = _remote(ins[a], part(a, chip), send_sems.at[_sem(a, k)], recv_sems.at[_sem(a, k)], peer)
                cp.start()
                sends.append(cp)
        for cp in local[n:]:
            cp.start()
        for a in range(n):
            for k in range(1, N_CHIPS):
                src = 2 * _flip(x, k & 2) + _flip(y, k & 1)
                _remote(ins[a], part(a, src), send_sems.at[_sem(a, k)], recv_sems.at[_sem(a, k)], (x, y, c)).wait_recv()
        for cp in sends:
            cp.wait_send()
        for cp in local:
            cp.wait()

    return pl.pallas_call(
        body, name="gather_chips", in_specs=[_HBM_SPEC] * (n_all + len(after)), out_specs=[_HBM_SPEC] * n_all,
        out_shape=[_sds(_gathered_shape(a.shape, how), a.dtype) for a, how in items],
        scratch_shapes=[pltpu.SemaphoreType.DMA((n * (N_CHIPS - 1),)), pltpu.SemaphoreType.DMA((n * (N_CHIPS - 1),)),
                        pltpu.SemaphoreType.DMA((n_all,))])(*arrays, *after)


def _gather_sum_all(small):
    r, w = small.shape

    def body(in_ref, all_ref, sum_ref, send_sems, recv_sems):
        x, y, c = _place()
        me = 4 * x + 2 * y + c
        all_ref[me] = in_ref[...]
        sends = []
        for k in range(1, N_DEV):
            peer = (_flip(x, k & 4), _flip(y, k & 2), _flip(c, k & 1))
            cp = _remote(in_ref, all_ref.at[me], send_sems.at[k - 1], recv_sems.at[k - 1], peer)
            cp.start()
            sends.append(cp)
        for k in range(1, N_DEV):
            src = 4 * _flip(x, k & 4) + 2 * _flip(y, k & 2) + _flip(c, k & 1)
            _remote(in_ref, all_ref.at[src], send_sems.at[k - 1], recv_sems.at[k - 1], (x, y, c)).wait_recv()
        acc = all_ref[0]
        for e in range(1, N_DEV):
            acc = acc + all_ref[e]
        sum_ref[...] = acc
        for cp in sends:
            cp.wait_send()

    return pl.pallas_call(
        body, name="gather_sum_all", in_specs=[_VMEM_SPEC], out_specs=[_VMEM_SPEC] * 2,
        out_shape=[_sds((N_DEV, r, w)), _sds((r, w))],
        scratch_shapes=[pltpu.SemaphoreType.DMA((N_DEV - 1,)), pltpu.SemaphoreType.DMA((N_DEV - 1,))],
        compiler_params=pltpu.CompilerParams(vmem_limit_bytes=VMEM_LIMIT_V7X))(small)


def _add_chips(buf, t, chip_idx):
    r, cc = buf.shape[1:]
    tr = min(256, r)

    def body(c_ref, p_ref, t_ref, o_ref):
        del c_ref
        o_ref[...] = p_ref[0] + t_ref[0].astype(F32) + t_ref[1].astype(F32) + t_ref[2].astype(F32)

    return pl.pallas_call(
        body, name="add_chips", out_shape=_sds((r, cc)),
        grid_spec=pltpu.PrefetchScalarGridSpec(
            num_scalar_prefetch=1, grid=(r // tr,),
            in_specs=[pl.BlockSpec((1, tr, cc), lambda i, c: (c[0], i, 0)),
                      pl.BlockSpec((N_CHIPS - 1, tr, cc), lambda i, c: (0, i, 0))],
            out_specs=pl.BlockSpec((tr, cc), lambda i, c: (i, 0))),
        compiler_params=_cp(1))(chip_idx, buf, t)


def _rs_sibling(qs):
    n = len(qs)

    def body(*refs):
        ins, outs = refs[:n], refs[n:2 * n]
        send_sems, recv_sems = refs[2 * n:]
        x, y, c = _place()
        copies = [_remote(ins[a], outs[a], send_sems.at[a], recv_sems.at[a], (x, y, 1 - c)) for a in range(n)]
        for cp in copies:
            cp.start()
        for cp in copies:
            cp.wait()

    return pl.pallas_call(
        body, name="rs_sibling", in_specs=[_HBM_SPEC] * n, out_specs=[_HBM_SPEC] * n,
        out_shape=[_sds(q.shape) for q in qs],
        scratch_shapes=[pltpu.SemaphoreType.DMA((n,)), pltpu.SemaphoreType.DMA((n,))])(*qs)


def _adamw_update(w, g, m, v):
    m = ADAM_B1 * m + (1.0 - ADAM_B1) * g
    v = ADAM_B2 * v + (1.0 - ADAM_B2) * jnp.square(g)
    m_hat = m / (1.0 - ADAM_B1 ** ADAM_STEP)
    v_hat = v / (1.0 - ADAM_B2 ** ADAM_STEP)
    return -ADAM_LR * (m_hat / (jnp.sqrt(v_hat) + ADAM_EPS) + ADAM_WD * w), m, v


def _adamw(w, g_parts, m, v, name):
    shape = w.shape
    cols = shape[-1]
    rows = _size(shape[:-1])
    tr = 512 if rows % 512 == 0 else rows
    spec = pl.BlockSpec((tr, cols), lambda i: (i, 0))
    n = len(g_parts)
    n_out = 4 if n > 1 else 3

    def body(*refs):
        w_ref, m_ref, v_ref = refs[:3]
        d_ref, nm_ref, nv_ref = refs[-3:]
        g = refs[3][...]
        for r in refs[4:3 + n]:
            g = g + r[...]
        if n > 1:
            refs[3 + n][...] = g
        d_ref[...], nm_ref[...], nv_ref[...] = _adamw_update(w_ref[...], g, m_ref[...], v_ref[...])

    outs = pl.pallas_call(
        body, name="adamw_" + name, grid=(rows // tr,), in_specs=[spec] * (3 + n), out_specs=[spec] * n_out,
        out_shape=[_sds((rows, cols))] * n_out, compiler_params=_cp(1))(
            *[a.reshape(rows, cols) for a in (w, m, v, *g_parts)])
    outs = tuple(o.reshape(shape) for o in outs)
    return outs if n > 1 else (g_parts[0],) + outs


def _adamw_layer(w, g_parts, m, v, layer, prev, name):
    _, r, cc = w.shape
    tr = 512 if r % 512 == 0 else r
    spec = pl.BlockSpec((1, tr, cc), lambda i: (layer, i, 0))
    n = len(g_parts)

    def body(*refs):
        w_ref, m_ref, v_ref = refs[:3]
        g_ref, d_ref, nm_ref, nv_ref = refs[-4:]
        g = refs[3][...]
        for q in refs[4:3 + n]:
            g = g + q[...]
        g = g[:, :cc]
        g_ref[0] = g
        d_ref[0], nm_ref[0], nv_ref[0] = _adamw_update(w_ref[0], g, m_ref[0], v_ref[0])

    g_specs = [pl.BlockSpec((tr, q.shape[1]), lambda i: (i, 0)) for q in g_parts]
    passed = () if prev is None else tuple(prev)
    return pl.pallas_call(
        body, name="adamw_" + name, grid=(r // tr,),
        in_specs=[spec] * 3 + g_specs + [_HBM_SPEC] * len(passed), out_specs=[spec] * 4,
        out_shape=[_sds(w.shape)] * 4, input_output_aliases={3 + n + k: k for k in range(len(passed))},
        compiler_params=_cp(1))(w, m, v, *g_parts, *passed)


def _size(shape):
    n = 1
    for s in shape:
        n *= s
    return n


_SMALL = (("dmod", (DEPTH, 3 * D_MODEL)), ("pre_norm_g", (DEPTH, D_MODEL)), ("post_norm_g", (DEPTH, D_MODEL)),
          ("even_sc_conv_w", (2, SC_KERNEL, SC_WIDTH)), ("even_sc_conv_b", (2, SC_WIDTH)),
          ("even_q_norm_g", (2, Q_LORA)), ("even_kv_norm_g", (2, KV_LORA)),
          ("odd_conv_w", (2, CONF_KERNEL, D_MODEL)), ("odd_conv_b", (2, D_MODEL)), ("odd_ln_g", (2, D_MODEL)),
          ("odd_ln_b", (2, D_MODEL)))
SMALL_ROWS = -(-sum(_size(s) for _, s in _SMALL) // (8 * 128)) * 8

_SMALL_W = (("even_sc_conv_w", (2, SC_KERNEL, SC_WIDTH // N_CHIPS)), ("odd_conv_w", (2, CONF_KERNEL, D_MODEL // N_CHIPS)),
            ("odd_conv_b", (2, D_MODEL // N_CHIPS)), ("odd_ln_g", (2, D_MODEL // N_CHIPS)),
            ("odd_ln_b", (2, D_MODEL // N_CHIPS)))
SMALL_W_ROWS = -(-sum(_size(s) for _, s in _SMALL_W) // (8 * 128)) * 8


def _pack_rows(arrays, layout, rows):
    flat = jnp.concatenate([arrays[n].reshape(-1) for n, _ in layout])
    return jnp.pad(flat, (0, rows * 128 - flat.shape[0])).reshape(rows, 128)


def _unpack_small(t):
    flat = t.reshape(-1)
    out, at = {}, 0
    for n, shape in _SMALL:
        out[n] = flat[at:at + _size(shape)].reshape(shape)
        at += _size(shape)
    return out


def _unpack_small_w(t):
    flat = t.reshape(N_CHIPS, -1)
    out, at = {}, 0
    for n, shape in _SMALL_W:
        a = flat[:, at:at + _size(shape)].reshape((N_CHIPS,) + shape)
        out[n] = jnp.moveaxis(a, 0, -2).reshape(shape[:-1] + (N_CHIPS * shape[-1],))
        at += _size(shape)
    return out


def _chip_cols(a, chip):
    n = a.shape[-1] // N_CHIPS
    return lax.dynamic_slice_in_dim(a, chip * n, n, axis=a.ndim - 1)


def _join_cols(a):
    _, l, r, cc = a.shape
    return a.transpose(1, 2, 0, 3).reshape(l, r, N_CHIPS * cc)


WEIGHT_NAMES = ("ada_w", "ada_b", "pre_norm_g", "post_norm_g", "even_w_in", "even_sc_conv_w", "even_sc_conv_b",
                "even_q_norm_g", "even_kv_norm_g", "even_w_uq", "even_w_ukv", "even_w_out", "odd_w_in", "odd_conv_w",
                "odd_conv_b", "odd_ln_g", "odd_ln_b", "odd_w_out")
GATHER_HOW = ((("even_w_in", "slot"), ("even_w_uq", "slot"), ("even_w_ukv", "slot"), ("even_w_out", "rows")),
              (("odd_w_in", "cols"), ("odd_w_out", "rows")))


def kernel(x, c, positions, ada_w, ada_b, pre_norm_g, post_norm_g, even_w_in, even_sc_conv_w, even_sc_conv_b, even_q_norm_g, even_kv_norm_g, even_w_uq, even_w_ukv, even_w_out, odd_w_in, odd_conv_w, odd_conv_b, odd_ln_g, odd_ln_b, odd_w_out, loss_target, m_ada_w, m_ada_b, m_pre_norm_g, m_post_norm_g, m_even_w_in, m_even_sc_conv_w, m_even_sc_conv_b, m_even_q_norm_g, m_even_kv_norm_g, m_even_w_uq, m_even_w_ukv, m_even_w_out, m_odd_w_in, m_odd_conv_w, m_odd_conv_b, m_odd_ln_g, m_odd_ln_b, m_odd_w_out, v_ada_w, v_ada_b, v_pre_norm_g, v_post_norm_g, v_even_w_in, v_even_sc_conv_w, v_even_sc_conv_b, v_even_q_norm_g, v_even_kv_norm_g, v_even_w_uq, v_even_w_ukv, v_even_w_out, v_odd_w_in, v_odd_conv_w, v_odd_conv_b, v_odd_ln_g, v_odd_ln_b, v_odd_w_out):
    w = dict(zip(WEIGHT_NAMES, (ada_w, ada_b, pre_norm_g, post_norm_g, even_w_in, even_sc_conv_w, even_sc_conv_b,
                                even_q_norm_g, even_kv_norm_g, even_w_uq, even_w_ukv, even_w_out, odd_w_in, odd_conv_w,
                                odd_conv_b, odd_ln_g, odd_ln_b, odd_w_out)))
    m = dict(zip(WEIGHT_NAMES, (m_ada_w, m_ada_b, m_pre_norm_g, m_post_norm_g, m_even_w_in, m_even_sc_conv_w,
                                m_even_sc_conv_b, m_even_q_norm_g, m_even_kv_norm_g, m_even_w_uq, m_even_w_ukv,
                                m_even_w_out, m_odd_w_in, m_odd_conv_w, m_odd_conv_b, m_odd_ln_g, m_odd_ln_b, m_odd_w_out)))
    v = dict(zip(WEIGHT_NAMES, (v_ada_w, v_ada_b, v_pre_norm_g, v_post_norm_g, v_even_w_in, v_even_sc_conv_w,
                                v_even_sc_conv_b, v_even_q_norm_g, v_even_kv_norm_g, v_even_w_uq, v_even_w_ukv,
                                v_even_w_out, v_odd_w_in, v_odd_conv_w, v_odd_conv_b, v_odd_ln_g, v_odd_ln_b, v_odd_w_out)))
    ix, iy, ic = _place()
    chip = 2 * ix + iy
    me = 2 * chip + ic
    s = x.shape[1]

    c_all, mod_all = _ada_fwd(jnp.broadcast_to(c, (8, D_MODEL)), ada_w, _chip_cols(ada_b, chip))
    mod = lax.dynamic_index_in_dim(mod_all, me, axis=2, keepdims=False)
    mod = mod.transpose(1, 0, 2).reshape(DEPTH, 3 * D_MODEL)

    items = [[(w[n][layer // 2].astype(MXU_DTYPE), how) for n, how in GATHER_HOW[layer % 2]] for layer in range(DEPTH)]
    groups = [items[0][:1], items[0][1:] + [(_pack_rows(w, _SMALL_W, SMALL_W_ROWS), "slot")],
              [item for layer_items in items[1:] for item in layer_items]]
    sent, dep = [], mod_all
    for number, group in enumerate(groups):
        sent.append(_gather_start(group, [_place_own(a, how, chip.reshape(1)) for a, how in group],
                                  "gather_start_%d" % number, [dep]))
        dep = sent[-1][-1]
    arrived = {}

    def group(number, after):
        if number not in arrived:
            arrived[number] = _gather_wait(groups[number], sent[number], after, "gather_wait_%d" % number)
        return arrived[number]

    def even_rest(i, uq, ukv, eout, small_w):
        wuk, wuv = _ukv_to_heads(ukv)
        wq, wq_rot = _uq_to_heads(uq)
        return {"wq": wq, "wq_rot": wq_rot, "wuk": wuk, "wuv": wuv, "w_out": eout, "sc_conv_w": small_w["even_sc_conv_w"][i]}

    def layer_weights(layer, h):
        i = layer // 2
        if layer == 0:
            def late(z):
                uq, ukv, eout, small = group(1, [z])
                return even_rest(i, uq, ukv, eout, _unpack_small_w(small))
            return {"w_in": _ein_from_shards(group(0, [h])[0]), "late": late}
        small_w = _unpack_small_w(group(1, [h])[-1])
        at = sum(len(layer_items) for layer_items in items[1:layer])
        arrays = group(2, [h])[at:at + len(items[layer])]
        if layer % 2 == 0:
            return {"w_in": _ein_from_shards(arrays[0]), **even_rest(i, *arrays[1:], small_w)}
        oin, oout = arrays
        return {"w_in": oin, "w_out": oout, "conv_w": small_w["odd_conv_w"][i], "conv_b": small_w["odd_conv_b"][i:i + 1],
                "ln_g": small_w["odd_ln_g"][i:i + 1], "ln_b": small_w["odd_ln_b"][i:i + 1]}

    in_flight, own, sib, last = {}, {}, {}, {}

    def land(layer, after):
        names, started, kept = in_flight.pop(layer)
        bufs, arrived = _rs_wait(started, after, "rs_wait_%d" % layer)
        sums = [_add_chips(b, t, chip.reshape(1)) for b, t in zip(bufs if kept is None else kept, arrived)]
        for n, mine, theirs in zip(names, sums, _rs_sibling(sums)):
            own[n, layer // 2], sib[n, layer // 2] = mine, theirs

    def grads_done(layer, bufs, dx_in):
        if layer + 1 in in_flight:
            land(layer + 1, [dx_in])
        if layer == 0:
            last.update(bufs)
            return None
        names = sorted(bufs)
        in_flight[layer] = (names, _rs_start([bufs[n] for n in names], "rs_start_%d" % layer), None)
        return in_flight[layer][1][-1]

    p = {"pre_norm_g": pre_norm_g, "post_norm_g": post_norm_g, "even_sc_conv_b": even_sc_conv_b,
         "even_q_norm_g": even_q_norm_g, "even_kv_norm_g": even_kv_norm_g}
    inv_freq = 1.0 / (ROPE_THETA ** (jnp.arange(0, QK_ROPE, 2, dtype=F32) / QK_ROPE))
    inv_freq = jnp.zeros((1, HEAD_PAD), F32).at[0, QK_NOPE:QK_NOPE + QK_ROPE].set(jnp.tile(inv_freq, 2))
    cos, sin = _rope_tables(positions.reshape(s, 1), inv_freq)

    loss, dx, g = _local_step(x[0], loss_target[0], cos, sin, mod, p, layer_weights, dep, grads_done)

    grads, deltas, new_m, new_v = {}, {}, {}, {}

    def update_layers(n, results, pairs):
        for i in pairs:
            results = _adamw_layer(w[n], [own[n, i], sib[n, i]], m[n], v[n], i, results, n)
        return results

    small_all, small_sum = _gather_sum_all(_pack_rows(g, _SMALL, SMALL_ROWS))
    names = sorted(last)
    kept = [last[n] for n in names]
    in_flight[0] = (names, _rs_start([b.astype(jnp.bfloat16) for b in kept], "rs_start_0", [small_sum]), kept)
    tot = _unpack_small(small_sum)
    dmod_all = small_all[:, :DEPTH * 3 * D_MODEL // 128].reshape(N_DEV, DEPTH, 3 * D_MODEL)
    grads["ada_w"] = _ada_bwd(c_all[:, 0, :].T, _chip_cols(dmod_all, chip).transpose(1, 0, 2))
    grads["ada_b"] = tot["dmod"]
    for n in ("pre_norm_g", "post_norm_g", "even_sc_conv_b", "even_q_norm_g", "even_kv_norm_g"):
        grads[n] = tot[n]
    for n in ("even_sc_conv_w", "odd_conv_w", "odd_conv_b", "odd_ln_g", "odd_ln_b"):
        grads[n] = _chip_cols(tot[n], chip)
    for n in list(grads):
        _, deltas[n], new_m[n], new_v[n] = _adamw(w[n], [grads[n]], m[n], v[n], n)

    for n in ("odd_w_in", "odd_w_out"):
        grads[n], deltas[n], new_m[n], new_v[n] = update_layers(n, None, (1, 0))
    partly = {n: update_layers(n, None, (1,)) for n in ("even_w_in", "even_w_out")}
    land(0, [deltas["ada_w"], deltas["odd_w_in"], partly["even_w_in"][1]])
    for n in ("even_w_in", "even_w_out"):
        grads[n], deltas[n], new_m[n], new_v[n] = update_layers(n, partly[n], (0,))
    uq_parts, ukv_parts = zip(*[[jnp.stack(part) for part in zip(*[_mla_local(q["even_mla", i]) for i in range(N_PAIRS)])]
                                for q in (own, sib)])
    for n, parts in (("even_w_uq", uq_parts), ("even_w_ukv", ukv_parts)):
        grads[n], deltas[n], new_m[n], new_v[n] = _adamw(w[n], list(parts), m[n], v[n], n)

    total_loss = lax.psum(loss[0, 0], ("x", "y", "c"))
    return (total_loss, dx[None], *[grads[n] for n in WEIGHT_NAMES], *[deltas[n] for n in WEIGHT_NAMES],
            *[new_m[n] for n in WEIGHT_NAMES], *[new_v[n] for n in WEIGHT_NAMES])
```

```python
import jax
import jax.numpy as jnp
from jax import lax
from jax.experimental import pallas as pl
from jax.experimental.pallas import tpu as pltpu

F32 = jnp.float32
MXU_DTYPE = jnp.bfloat16
MESH = pl.DeviceIdType.MESH
VMEM_LIMIT_V7X = 56 * 2 ** 20

EPS = 1e-6
D_MODEL = 1024
DEPTH = 4
CHUNK = 64
SC_WIDTH = 512
SC_KERNEL = 3
SC_HALO = 8
HEADS = 8
QK_NOPE = 64
QK_ROPE = 32
V_HEAD = 64
HEAD_PAD = 128
Q_LORA = 256
KV_LORA = 128
ROPE_THETA = 10000.0
CONF_KERNEL = 31
CONF_HALO = 32
CONV_ROWS = 32
SUBLANES = 8
EVEN_IN = 2976
EVEN_PAD = 3072
ODD_IN = 3072
N_CHIPS = 4
N_DEV = 8
NEG = -1e30

ADAM_LR = 0.001
ADAM_B1 = 0.9
ADAM_B2 = 0.999
ADAM_EPS = 1e-08
ADAM_WD = 0.01
ADAM_STEP = 10

N_PAIRS = DEPTH // 2
EVEN_SHARD = EVEN_IN // N_CHIPS
EVEN_SHARD_PAD = 768
MLA_ROWS = Q_LORA + 2 * KV_LORA


def _cp(n_grid=0, **kw):
    return pltpu.CompilerParams(dimension_semantics=("arbitrary",) * n_grid,
                                vmem_limit_bytes=VMEM_LIMIT_V7X, **kw)


def _sigmoid(x):
    return 1.0 / (1.0 + jnp.exp(-x))


def _silu(x):
    return x * _sigmoid(x)


def _dsilu(x):
    s = _sigmoid(x)
    return s * (1.0 + x * (1.0 - s))


def _rms(x, g):
    return x * lax.rsqrt(jnp.mean(x * x, axis=-1, keepdims=True) + EPS) * g


def _dot(a, b, dims):
    return lax.dot_general(a.astype(MXU_DTYPE), b.astype(MXU_DTYPE), (dims, ((), ())),
                           preferred_element_type=F32)


def _dot_nn(a, b):
    return _dot(a, b, ((1,), (0,)))


def _dot_nt(a, b):
    return _dot(a, b, ((1,), (1,)))


def _dot_tn(a, b):
    return _dot(a, b, ((0,), (0,)))


def _rows(ts, w, cb=0):
    return pl.BlockSpec((ts, w), lambda i: (i, cb))


def _vec(w, cb=0, r=1):
    return pl.BlockSpec((r, w), lambda i: (0, cb))


def _prev_halo(ts, hr, w, cb):
    return pl.BlockSpec((hr, w), lambda i: (jnp.maximum(i * (ts // hr) - 1, 0), cb))


def _next_halo(ts, hr, w, cb, s):
    return pl.BlockSpec((hr, w), lambda i: (jnp.minimum((i + 1) * (ts // hr), s // hr - 1), cb))


def _sds(shape, dtype=F32):
    return jax.ShapeDtypeStruct(shape, dtype)


def _mm(a, b, mode, out_dtype, tm, tn, name):
    tm = min(tm, a.shape[1] if mode == "tn" else a.shape[0])
    tn = min(tn, b.shape[0] if mode == "nt" else b.shape[1])
    if mode == "nn":
        (m, k), n = a.shape, b.shape[1]
        a_spec = pl.BlockSpec((tm, k), lambda i, j: (i, 0))
        b_spec = pl.BlockSpec((k, tn), lambda i, j: (0, j))
        dot = _dot_nn
    elif mode == "nt":
        (m, k), n = a.shape, b.shape[0]
        a_spec = pl.BlockSpec((tm, k), lambda i, j: (i, 0))
        b_spec = pl.BlockSpec((tn, k), lambda i, j: (j, 0))
        dot = _dot_nt
    else:
        (k, m), n = a.shape, b.shape[1]
        a_spec = pl.BlockSpec((k, tm), lambda i, j: (0, i))
        b_spec = pl.BlockSpec((k, tn), lambda i, j: (0, j))
        dot = _dot_tn
    assert m % tm == 0 and n % tn == 0, (name, m, n, tm, tn)

    def body(a_ref, b_ref, o_ref):
        o_ref[...] = dot(a_ref[...], b_ref[...]).astype(o_ref.dtype)

    return pl.pallas_call(
        body, name=name, grid=(m // tm, n // tn), in_specs=[a_spec, b_spec],
        out_specs=pl.BlockSpec((tm, tn), lambda i, j: (i, j)), out_shape=_sds((m, n), out_dtype),
        compiler_params=_cp(2))(a, b)


def _mm_tn_shards(a, b, by, name):
    k, m = a.shape
    n = b.shape[1]
    if by == "cols":
        tm, tn = m, n // N_CHIPS
        shape, grid = (N_CHIPS, m, tn), (1, N_CHIPS)
        out_spec = pl.BlockSpec((1, tm, tn), lambda i, j: (j, i, 0))
    else:
        tm, tn = m // N_CHIPS, n
        shape, grid = (N_CHIPS, tm, n), (N_CHIPS, 1)
        out_spec = pl.BlockSpec((1, tm, tn), lambda i, j: (i, 0, j))

    def body(a_ref, b_ref, o_ref):
        o_ref[0] = _dot_tn(a_ref[...], b_ref[...])

    return pl.pallas_call(
        body, name=name, grid=grid,
        in_specs=[pl.BlockSpec((k, tm), lambda i, j: (0, i)), pl.BlockSpec((k, tn), lambda i, j: (0, j))],
        out_specs=out_spec, out_shape=_sds(shape), compiler_params=_cp(2))(a, b)


def _even_col(q):
    return q if q < 2432 else (q + 64 if q < 2464 else q + 96)


def _shard_pieces(j):
    lo, hi = EVEN_SHARD * j, EVEN_SHARD * (j + 1)
    cuts = [lo] + [b for b in (2432, 2464) if lo < b < hi] + [hi]
    return [(a - lo, _even_col(a), b - a) for a, b in zip(cuts[:-1], cuts[1:])]


def _ein_from_shards(w):
    _, d, _ = w.shape
    tr = 256

    def body(w_ref, o_ref):
        parts, at = [], 0
        for j in range(N_CHIPS):
            for d0, s0, n in _shard_pieces(j):
                if s0 > at:
                    parts.append(jnp.zeros((tr, s0 - at), F32))
                parts.append(w_ref[j, :, d0:d0 + n].astype(F32))
                at = s0 + n
        o_ref[...] = jnp.concatenate(parts, axis=1).astype(o_ref.dtype)

    return pl.pallas_call(
        body, name="ein_from_shards", grid=(d // tr,),
        in_specs=[pl.BlockSpec((N_CHIPS, tr, EVEN_SHARD), lambda i: (0, i, 0))],
        out_specs=_rows(tr, EVEN_PAD), out_shape=_sds((d, EVEN_PAD), w.dtype), compiler_params=_cp(1))(w)


def _ein_to_shards(dw):
    d = dw.shape[0]
    tr = 256

    def body(dw_ref, o_ref):
        for j in range(N_CHIPS):
            parts = [dw_ref[:, s0:s0 + n] for _, s0, n in _shard_pieces(j)]
            o_ref[j] = jnp.concatenate(parts + [jnp.zeros((tr, EVEN_SHARD_PAD - EVEN_SHARD), F32)], axis=1)

    return pl.pallas_call(
        body, name="ein_to_shards", grid=(d // tr,), in_specs=[_rows(tr, EVEN_PAD)],
        out_specs=pl.BlockSpec((N_CHIPS, tr, EVEN_SHARD_PAD), lambda i: (0, i, 0)),
        out_shape=_sds((N_CHIPS, d, EVEN_SHARD_PAD)), compiler_params=_cp(1))(dw)


def _rope_tables(pos_col, invf):
    s = pos_col.shape[0]
    ts = min(512, s)

    def body(p_ref, f_ref, c_ref, s_ref):
        ang = p_ref[...].astype(F32) * f_ref[...]
        lane = lax.broadcasted_iota(jnp.int32, ang.shape, 1)
        rope = (lane >= QK_NOPE) & (lane < QK_NOPE + QK_ROPE)
        c_ref[...] = jnp.where(lane < QK_NOPE, 1.0, jnp.where(rope, jnp.cos(ang), 0.0))
        s_ref[...] = jnp.where(rope, jnp.sin(ang), 0.0)

    return pl.pallas_call(
        body, name="rope_tables", grid=(s // ts,), in_specs=[_rows(ts, 1), _vec(HEAD_PAD)],
        out_specs=[_rows(ts, HEAD_PAD)] * 2, out_shape=[_sds((s, HEAD_PAD))] * 2,
        compiler_params=_cp(1))(pos_col, invf)


def _after(dep):
    return () if dep is None else (dep,)


def _pre_fwd(x, g, mod_l, ts, dep=None):
    s, d = x.shape

    def body(x_ref, g_ref, sh_ref, sc_ref, *rest):
        h = _rms(x_ref[...], g_ref[...]) * (1.0 + sc_ref[...]) + sh_ref[...]
        rest[-1][...] = h.astype(rest[-1].dtype)

    return pl.pallas_call(
        body, name="pre_fwd", grid=(s // ts,),
        in_specs=[_rows(ts, d), _vec(d), _vec(d, 0), _vec(d, 1)] + [_HBM_SPEC] * len(_after(dep)),
        out_specs=_rows(ts, d), out_shape=_sds((s, d), MXU_DTYPE), compiler_params=_cp(1))(
            x, g, mod_l, mod_l, *_after(dep))


def _pre_bwd(dz, w_in, dx_out, x, g, mod_l, ts):
    s, d = x.shape
    n_in = dz.shape[1]

    def f(xv, gv, sh, sc):
        return _rms(xv, gv) * (1.0 + sc) + sh

    def body(dz_ref, w_ref, dxo_ref, x_ref, g_ref, sh_ref, sc_ref, dx_ref, dsh_ref, dsc_ref, dg_ref):
        @pl.when(pl.program_id(0) == 0)
        def _():
            dsh_ref[...] = jnp.zeros_like(dsh_ref)
            dsc_ref[...] = jnp.zeros_like(dsc_ref)
            dg_ref[...] = jnp.zeros_like(dg_ref)

        for rows in (slice(0, ts // 2), slice(ts // 2, ts)):
            _, vjp = jax.vjp(f, x_ref[rows, :], g_ref[...], sh_ref[...], sc_ref[...])
            dx, dg, dsh, dsc = vjp(_dot_nt(dz_ref[rows, :], w_ref[...]))
            dx_ref[rows, :] = dxo_ref[rows, :] + dx
            dsh_ref[...] += dsh
            dsc_ref[...] += dsc
            dg_ref[...] += dg

    return pl.pallas_call(
        body, name="pre_bwd", grid=(s // ts,),
        in_specs=[_rows(ts, n_in), _vec(n_in, 0, d), _rows(ts, d), _rows(ts, d), _vec(d), _vec(d, 0), _vec(d, 1)],
        out_specs=[_rows(ts, d), _vec(d), _vec(d), _vec(d)],
        out_shape=[_sds((s, d)), _sds((1, d)), _sds((1, d)), _sds((1, d))],
        compiler_params=_cp(1))(dz, w_in, dx_out, x, g, mod_l, mod_l)


def _post_fwd(x, yo, g, mod_l, ts):
    s, d = x.shape

    def body(x_ref, yo_ref, g_ref, gate_ref, o_ref):
        o_ref[...] = x_ref[...] + gate_ref[...] * _rms(yo_ref[...], g_ref[...])

    return pl.pallas_call(
        body, name="post_fwd", grid=(s // ts,),
        in_specs=[_rows(ts, d), _rows(ts, d), _vec(d), _vec(d, 2)],
        out_specs=_rows(ts, d), out_shape=_sds((s, d)), compiler_params=_cp(1))(x, yo, g, mod_l)


def _post_bwd(dx_out, yo, g, mod_l, ts, dep=None):
    s, d = yo.shape

    def f(yov, gv, gate):
        return gate * _rms(yov, gv)

    def body(dx_ref, yo_ref, g_ref, gate_ref, *rest):
        dyo_ref, dgate_ref, dg_ref = rest[-3:]
        i = pl.program_id(0)
        _, vjp = jax.vjp(f, yo_ref[...], g_ref[...], gate_ref[...])
        dyo, dg, dgate = vjp(dx_ref[...])
        dyo_ref[...] = dyo.astype(dyo_ref.dtype)

        @pl.when(i == 0)
        def _():
            dgate_ref[...] = jnp.zeros_like(dgate_ref)
            dg_ref[...] = jnp.zeros_like(dg_ref)

        dgate_ref[...] += dgate
        dg_ref[...] += dg

    return pl.pallas_call(
        body, name="post_bwd", grid=(s // ts,),
        in_specs=[_rows(ts, d), _rows(ts, d), _vec(d), _vec(d, 2)] + [_HBM_SPEC] * len(_after(dep)),
        out_specs=[_rows(ts, d), _vec(d), _vec(d)],
        out_shape=[_sds((s, d), MXU_DTYPE), _sds((1, d)), _sds((1, d))],
        compiler_params=_cp(1))(dx_out, yo, g, mod_l, *_after(dep))


def _loss_fwd_bwd(x, target, ts):
    s, d = x.shape

    def body(x_ref, t_ref, loss_ref, dx_ref):
        i = pl.program_id(0)
        err = x_ref[...] - t_ref[...]
        dx_ref[...] = err * (1.0 / d)

        @pl.when(i == 0)
        def _():
            loss_ref[...] = jnp.zeros_like(loss_ref)

        loss_ref[...] += 0.5 * jnp.sum(jnp.sum(err * err, axis=-1, keepdims=True) * (1.0 / d), axis=0, keepdims=True)

    return pl.pallas_call(
        body, name="loss", grid=(s // ts,), in_specs=[_rows(ts, d), _rows(ts, d)],
        out_specs=[_vec(1), _rows(ts, d)], out_shape=[_sds((1, 1)), _sds((s, d))],
        compiler_params=_cp(1))(x, target)


def _rope(t, cos, sin):
    lane = lax.broadcasted_iota(jnp.int32, t.shape, 1)
    first = (lane >= QK_NOPE) & (lane < QK_NOPE + QK_ROPE // 2)
    second = (lane >= QK_NOPE + QK_ROPE // 2) & (lane < QK_NOPE + QK_ROPE)
    up = pltpu.roll(t, QK_ROPE // 2, 1)
    down = pltpu.roll(t, HEAD_PAD - QK_ROPE // 2, 1)
    return t * cos + jnp.where(first, -down, jnp.where(second, up, 0.0)) * sin


def _rope_transposed(g, cos, sin):
    lane = lax.broadcasted_iota(jnp.int32, g.shape, 1)
    first = (lane >= QK_NOPE) & (lane < QK_NOPE + QK_ROPE // 2)
    second = (lane >= QK_NOPE + QK_ROPE // 2) & (lane < QK_NOPE + QK_ROPE)
    u = g * sin
    up = pltpu.roll(u, QK_ROPE // 2, 1)
    down = pltpu.roll(u, HEAD_PAD - QK_ROPE // 2, 1)
    return g * cos + jnp.where(first, down, jnp.where(second, -up, 0.0))


def _mla_prep_fwd(z, cos, sin, qg, kvg, wq, wq_rot, wuk, wuv, ts):
    s = z.shape[0]
    wide = HEADS * HEAD_PAD

    def body(cq_ref, ckv_ref, kr_ref, cos_ref, sin_ref, qg_ref, kvg_ref, wq_ref, wqr_ref, wuk_ref, wuv_ref,
             q_ref, qt_ref, k_ref, v_ref):
        cos_v, sin_v = cos_ref[...], sin_ref[...]
        cqn = _rms(cq_ref[...], qg_ref[...])
        ckvn = _rms(ckv_ref[...], kvg_ref[...])
        kr = _rope(kr_ref[...], cos_v, sin_v)
        q_lin, q_rot = _dot_nn(cqn, wq_ref[...]), _dot_nn(cqn, wqr_ref[...])
        k_lin, v_all = _dot_nn(ckvn, wuk_ref[...]), _dot_nn(ckvn, wuv_ref[...])
        for h in range(HEADS):
            lanes = slice(h * HEAD_PAD, (h + 1) * HEAD_PAD)
            qh = q_lin[:, lanes] * cos_v + q_rot[:, lanes] * sin_v
            q_ref[h] = qh.astype(q_ref.dtype)
            qt_ref[h, 0] = qh.T.astype(qt_ref.dtype)
            k_ref[h] = (k_lin[:, lanes] + kr).astype(k_ref.dtype)
            v_ref[h] = v_all[:, lanes].astype(v_ref.dtype)

    out = pl.BlockSpec((HEADS, ts, HEAD_PAD), lambda i: (0, i, 0))
    return pl.pallas_call(
        body, name="mla_prep_fwd", grid=(s // ts,),
        in_specs=[_rows(ts, Q_LORA, 8), _rows(ts, KV_LORA, 18), _rows(ts, HEAD_PAD, 19), _rows(ts, HEAD_PAD), _rows(ts, HEAD_PAD),
                  _vec(Q_LORA), _vec(KV_LORA), _vec(wide, 0, Q_LORA), _vec(wide, 0, Q_LORA), _vec(wide, 0, KV_LORA),
                  _vec(wide, 0, KV_LORA)],
        out_specs=[out, pl.BlockSpec((HEADS, 1, HEAD_PAD, ts), lambda i: (0, i, 0, 0)), out, out],
        out_shape=[_sds((HEADS, s, HEAD_PAD), MXU_DTYPE), _sds((HEADS, s // ts, HEAD_PAD, ts), MXU_DTYPE)]
        + [_sds((HEADS, s, HEAD_PAD), MXU_DTYPE)] * 2,
        compiler_params=_cp(1))(z, z, z, cos, sin, qg, kvg, wq, wq_rot, wuk, wuv)


def _mla_prep_bwd(dz, dq, dk, dv, z, cos, sin, qg, kvg, wq, wuk, wuv, ts):
    s = z.shape[0]

    def fq(cq, g):
        return _rms(cq, g)

    def body(dz_in_ref, dq_ref, dk_ref, dv_ref, cq_ref, ckv_ref, cos_ref, sin_ref, qg_ref, kvg_ref, wq_ref, wuk_ref,
             wuv_ref, dz_ref, dw_ref, dqg_ref, dkvg_ref):
        del dz_in_ref
        cos_v, sin_v = cos_ref[...], sin_ref[...]

        @pl.when(pl.program_id(0) == 0)
        def _():
            dw_ref[...] = jnp.zeros_like(dw_ref)
            dqg_ref[...] = jnp.zeros_like(dqg_ref)
            dkvg_ref[...] = jnp.zeros_like(dkvg_ref)

        cqn, vjp_q = jax.vjp(fq, cq_ref[...], qg_ref[...])
        ckvn, vjp_kv = jax.vjp(fq, ckv_ref[...], kvg_ref[...])
        lane = lax.broadcasted_iota(jnp.int32, (ts, HEAD_PAD), 1)
        rope_lanes = (lane >= QK_NOPE) & (lane < QK_NOPE + QK_ROPE)
        dq_lin = jnp.concatenate([_rope_transposed(dq_ref[h], cos_v, sin_v).astype(MXU_DTYPE) for h in range(HEADS)], axis=1)
        dk_all = jnp.concatenate([dk_ref[h].astype(MXU_DTYPE) for h in range(HEADS)], axis=1)
        dv_all = jnp.concatenate([dv_ref[h].astype(MXU_DTYPE) for h in range(HEADS)], axis=1)
        dkr = jnp.where(rope_lanes, dk_ref[0], 0.0)
        for h in range(1, HEADS):
            dkr = dkr + jnp.where(rope_lanes, dk_ref[h], 0.0)
        dcq, dqg = vjp_q(_dot_nt(dq_lin, wq_ref[...]))
        dckv, dkvg = vjp_kv(_dot_nt(dk_all, wuk_ref[...]) + _dot_nt(dv_all, wuv_ref[...]))
        dz_ref[:, 0:Q_LORA] = dcq.astype(dz_ref.dtype)
        dz_ref[:, Q_LORA:Q_LORA + KV_LORA] = dckv.astype(dz_ref.dtype)
        dz_ref[:, Q_LORA + KV_LORA:] = _rope_transposed(dkr, cos_v, sin_v).astype(dz_ref.dtype)
        dqg_ref[...] += dqg
        dkvg_ref[...] += dkvg
        dwq, dwuk, dwuv = _dot_tn(cqn, dq_lin), _dot_tn(ckvn, dk_all), _dot_tn(ckvn, dv_all)
        for h in range(HEADS):
            lanes = slice(h * HEAD_PAD, (h + 1) * HEAD_PAD)
            row0 = (h % 2) * MLA_ROWS
            dw_ref[h // 2, row0:row0 + Q_LORA, :] += dwq[:, lanes]
            dw_ref[h // 2, row0 + Q_LORA:row0 + Q_LORA + KV_LORA, :] += dwuk[:, lanes]
            dw_ref[h // 2, row0 + Q_LORA + KV_LORA:row0 + MLA_ROWS, :] += dwuv[:, lanes]

    wide = HEADS * HEAD_PAD
    heads = pl.BlockSpec((HEADS, ts, HEAD_PAD), lambda i: (0, i, 0))
    whole = pl.BlockSpec((N_CHIPS, 2 * MLA_ROWS, HEAD_PAD), lambda i: (0, 0, 0))
    return pl.pallas_call(
        body, name="mla_prep_bwd", grid=(s // ts,),
        in_specs=[_HBM_SPEC, heads, heads, heads, _rows(ts, Q_LORA, 8), _rows(ts, KV_LORA, 18),
                  _rows(ts, HEAD_PAD), _rows(ts, HEAD_PAD), _vec(Q_LORA), _vec(KV_LORA), _vec(wide, 0, Q_LORA),
                  _vec(wide, 0, KV_LORA), _vec(wide, 0, KV_LORA)],
        out_specs=[_rows(ts, 512, 4), whole, _vec(Q_LORA), _vec(KV_LORA)],
        out_shape=[_sds(dz.shape, dz.dtype), _sds((N_CHIPS, 2 * MLA_ROWS, HEAD_PAD)), _sds((1, Q_LORA)), _sds((1, KV_LORA))],
        input_output_aliases={0: 0}, compiler_params=_cp(1))(dz, dq, dk, dv, z, z, cos, sin, qg, kvg, wq, wuk, wuv)


def _chunk_mask(q0, k0, tq, tk):
    rows = q0 + lax.broadcasted_iota(jnp.int32, (tq, tk), 0)
    cols = k0 + lax.broadcasted_iota(jnp.int32, (tq, tk), 1)
    shift = CHUNK.bit_length() - 1
    return lax.shift_right_logical(cols, shift) <= lax.shift_right_logical(rows, shift)


def _attn_fwd(q, k, v, tq):
    s = q.shape[1]
    nq = s // tq
    scale = 1.0 / float(QK_NOPE + QK_ROPE) ** 0.5

    assert nq % 2 == 0, (s, tq)

    def body(q_ref, k_ref, v_ref, o_ref, lse_ref):
        pair, hh = pl.program_id(1), pl.program_id(2)

        def step(qv, q0, kj, carry, masked):
            m, l, acc = carry
            k0 = pl.multiple_of(kj * tq, tq)
            sc = _dot_nt(qv, k_ref[0, pl.ds(k0, tq), :]) * scale
            if masked:
                sc = jnp.where(_chunk_mask(q0, k0, tq, tq), sc, NEG)
            m_new = jnp.maximum(m, jnp.max(sc, axis=-1, keepdims=True))
            alpha = jnp.exp(m - m_new)
            p = jnp.exp(sc - m_new)
            l = alpha * l + jnp.sum(p, axis=-1, keepdims=True)
            acc = alpha * acc + _dot_nn(p, v_ref[0, pl.ds(k0, tq), :])
            return m_new, l, acc

        for half in range(2):
            rows = slice(half * tq, (half + 1) * tq)
            qv = q_ref[0, rows, :]
            q0 = (2 * pair + half) * tq
            two = lambda i, c: step(qv, q0, 2 * i + 1, step(qv, q0, 2 * i, c, False), False)
            init = (jnp.full((tq, 1), NEG, F32), jnp.zeros((tq, 1), F32), jnp.zeros((tq, HEAD_PAD), F32))
            carry = lax.fori_loop(0, pair, two, init)
            if half == 1:
                carry = step(qv, q0, 2 * pair, carry, False)
            m, l, acc = step(qv, q0, 2 * pair + half, carry, True)
            o = acc / l
            lse_ref[0, rows, :] = m + jnp.log(l)

            @pl.when(hh == 0)
            def _():
                o_ref[rows, :] = o

            @pl.when(hh == 1)
            def _():
                o_ref[rows, :] += o

    head = lambda hp, pair, hh: 2 * hp + hh
    return pl.pallas_call(
        body, name="attn_fwd", grid=(HEADS // 2, nq // 2, 2),
        in_specs=[pl.BlockSpec((1, 2 * tq, HEAD_PAD), lambda hp, pair, hh: (head(hp, pair, hh), pair, 0)),
                  pl.BlockSpec((1, s, HEAD_PAD), lambda hp, pair, hh: (head(hp, pair, hh), 0, 0)),
                  pl.BlockSpec((1, s, HEAD_PAD), lambda hp, pair, hh: (head(hp, pair, hh), 0, 0))],
        out_specs=[pl.BlockSpec((2 * tq, HEAD_PAD), lambda hp, pair, hh: (pair, hp)),
                   pl.BlockSpec((1, 2 * tq, 1), lambda hp, pair, hh: (head(hp, pair, hh), pair, 0))],
        out_shape=[_sds((s, HEADS * V_HEAD)), _sds((HEADS, s, 1))],
        compiler_params=_cp(3))(q, k, v)


def _attn_bwd(q, q_t, k, v, do, do_t, o, lse, tq):
    s = q.shape[1]
    nq = s // tq
    per_q = tq // do_t.shape[3]
    scale = 1.0 / float(QK_NOPE + QK_ROPE) ** 0.5

    def body(q_ref, qt_ref, k_ref, v_ref, do_ref, dot_ref, o_ref, lse_ref, dq_ref, dk_ref, dv_ref, dk_t, dv_t):
        hh, kj = pl.program_id(1), pl.program_id(2)

        @pl.when(kj == 0)
        def _():
            dq_ref[...] = jnp.zeros_like(dq_ref)

        kv, vv = k_ref[0], v_ref[0]
        lane = lax.broadcasted_iota(jnp.int32, (tq, HEAD_PAD), 1)
        mine = lax.shift_right_logical(lane, 6) == hh
        dk_t[...] = jnp.zeros_like(dk_t)
        dv_t[...] = jnp.zeros_like(dv_t)

        def step(qi, masked):
            q0 = pl.multiple_of(qi * tq, tq)
            qv = q_ref[0, pl.ds(q0, tq), :]
            dov = do_ref[pl.ds(q0, tq), :]
            delta = jnp.sum(jnp.where(mine, dov * o_ref[pl.ds(q0, tq), :], 0.0), axis=-1, keepdims=True)
            sc = _dot_nt(qv, kv) * scale
            if masked:
                sc = jnp.where(_chunk_mask(q0, kj * tq, tq, tq), sc, NEG)
            p = jnp.exp(sc - lse_ref[0, pl.ds(q0, tq), :])
            ds = (p * (_dot_nt(dov, vv) - delta) * scale).astype(MXU_DTYPE)
            do_tv = jnp.concatenate([dot_ref[0, qi * per_q + r] for r in range(per_q)], axis=1)
            dv_t[...] += _dot_nn(do_tv, p)
            dk_t[...] += _dot_nn(qt_ref[0, qi], ds)
            dq_ref[0, pl.ds(q0, tq), :] += _dot_nn(ds, kv)

        step(kj, True)
        odd = (nq - 1 - kj) % 2

        @pl.when(odd == 1)
        def _():
            step(kj + 1, False)

        def two(i, c):
            step(kj + 1 + odd + 2 * i, False)
            step(kj + 2 + odd + 2 * i, False)
            return c

        lax.fori_loop(0, (nq - 1 - kj) // 2, two, 0)
        dk_ref[0] = dk_t[...].T
        dv_ref[0] = dv_t[...].T

    head = lambda hp, hh, kj: 2 * hp + hh
    full = pl.BlockSpec((1, s, HEAD_PAD), lambda hp, hh, kj: (head(hp, hh, kj), 0, 0))
    blk = pl.BlockSpec((1, tq, HEAD_PAD), lambda hp, hh, kj: (head(hp, hh, kj), kj, 0))
    pair = pl.BlockSpec((s, HEAD_PAD), lambda hp, hh, kj: (0, hp))
    return pl.pallas_call(
        body, name="attn_bwd", grid=(HEADS // 2, 2, nq),
        in_specs=[full, pl.BlockSpec((1,) + q_t.shape[1:], lambda hp, hh, kj: (head(hp, hh, kj), 0, 0, 0)), blk, blk,
                  pair, pl.BlockSpec((1,) + do_t.shape[1:], lambda hp, hh, kj: (hp, 0, 0, 0)), pair,
                  pl.BlockSpec((1, s, 1), lambda hp, hh, kj: (head(hp, hh, kj), 0, 0))],
        out_specs=[full, blk, blk], out_shape=[_sds((HEADS, s, HEAD_PAD))] * 3,
        scratch_shapes=[pltpu.VMEM((HEAD_PAD, tq), F32), pltpu.VMEM((HEAD_PAD, tq), F32)],
        compiler_params=_cp(3))(q, q_t, k, v, do, do_t, o, lse)


def _sc_conv(u, ubuf, w_ref, b_ref, ts):
    return (w_ref[2:3, :] * u + w_ref[1:2, :] * ubuf[pl.ds(SC_HALO - 1, ts), :]
            + w_ref[0:1, :] * ubuf[pl.ds(SC_HALO - 2, ts), :] + b_ref[...])


def _even_gate_fwd(z, o, sc_w, sc_b, ts):
    s = z.shape[0]
    w = SC_WIDTH

    def body(ab_ref, ac_ref, ax_ref, ag_ref, bg_ref, hc_ref, hx_ref, o_ref, w_ref, b_ref, y_ref, ubuf):
        i = pl.program_id(0)
        u = ac_ref[...] * ax_ref[...]
        ubuf[0:SC_HALO, :] = jnp.where(i > 0, hc_ref[...] * hx_ref[...], 0.0)
        ubuf[SC_HALO:, :] = u
        conv = _sc_conv(u, ubuf, w_ref, b_ref, ts)
        y_ref[:, 0:w] = (ab_ref[...] * conv * _silu(ag_ref[...])).astype(y_ref.dtype)
        y_ref[:, w:] = (o_ref[...] * _silu(bg_ref[...])).astype(y_ref.dtype)

    return pl.pallas_call(
        body, name="even_gate_fwd", grid=(s // ts,),
        in_specs=[_rows(ts, w, 0), _rows(ts, w, 1), _rows(ts, w, 2), _rows(ts, w, 3), _rows(ts, w, 5),
                  _prev_halo(ts, SC_HALO, w, 1), _prev_halo(ts, SC_HALO, w, 2), _rows(ts, w),
                  _vec(w, 0, SC_KERNEL), _vec(w)],
        out_specs=_rows(ts, 2 * w), out_shape=_sds((s, 2 * w), MXU_DTYPE),
        scratch_shapes=[pltpu.VMEM((ts + SC_HALO, w), F32)],
        compiler_params=_cp(1))(z, z, z, z, z, z, z, o, sc_w, sc_b)


def _even_gate_bwd(dy, z, o, sc_w, sc_b, ts):
    s = z.shape[0]
    w = SC_WIDTH
    n = s // ts

    def body(dya_ref, dyb_ref, dyan_ref, ab_ref, ac_ref, ax_ref, ag_ref, bg_ref, hc_ref, hx_ref, abn_ref, agn_ref,
             o_ref, w_ref, b_ref, dz_ref, do_ref, dot_ref, dw_ref, db_ref, ubuf, dbuf):
        i = pl.program_id(0)
        ab, ac, ax, ag, bg = ab_ref[...], ac_ref[...], ax_ref[...], ag_ref[...], bg_ref[...]
        dya, dyb = dya_ref[...], dyb_ref[...]
        u = ac * ax
        ubuf[0:SC_HALO, :] = jnp.where(i > 0, hc_ref[...] * hx_ref[...], 0.0)
        ubuf[SC_HALO:, :] = u
        conv = _sc_conv(u, ubuf, w_ref, b_ref, ts)
        sg = _silu(ag)
        dconv = dya * ab * sg
        dbuf[0:ts, :] = dconv
        dbuf[ts:, :] = jnp.where(i < n - 1, dyan_ref[...] * abn_ref[...] * _silu(agn_ref[...]), 0.0)
        du = w_ref[2:3, :] * dconv + w_ref[1:2, :] * dbuf[pl.ds(1, ts), :] + w_ref[0:1, :] * dbuf[pl.ds(2, ts), :]
        dz_ref[:, 0:w] = (dya * conv * sg).astype(dz_ref.dtype)
        dz_ref[:, w:2 * w] = (du * ax).astype(dz_ref.dtype)
        dz_ref[:, 2 * w:3 * w] = (du * ac).astype(dz_ref.dtype)
        dz_ref[:, 3 * w:4 * w] = (dya * ab * conv * _dsilu(ag)).astype(dz_ref.dtype)
        dz_ref[:, 4 * w:5 * w] = jnp.zeros((ts, w), dz_ref.dtype)
        dz_ref[:, 5 * w:] = (dyb * o_ref[...] * _dsilu(bg)).astype(dz_ref.dtype)
        do = dyb * _silu(bg)
        do_ref[...] = do
        for pair in range(HEADS // 2):
            dot_ref[pair, 0] = do[:, pair * HEAD_PAD:(pair + 1) * HEAD_PAD].T.astype(dot_ref.dtype)

        @pl.when(i == 0)
        def _():
            dw_ref[...] = jnp.zeros_like(dw_ref)
            db_ref[...] = jnp.zeros_like(db_ref)

        dw_ref[0:1, :] += jnp.sum(dconv * ubuf[pl.ds(SC_HALO - 2, ts), :], axis=0, keepdims=True)
        dw_ref[1:2, :] += jnp.sum(dconv * ubuf[pl.ds(SC_HALO - 1, ts), :], axis=0, keepdims=True)
        dw_ref[2:3, :] += jnp.sum(dconv * u, axis=0, keepdims=True)
        db_ref[...] += jnp.sum(dconv, axis=0, keepdims=True)

    return pl.pallas_call(
        body, name="even_gate_bwd", grid=(n,),
        in_specs=[_rows(ts, w, 0), _rows(ts, w, 1), _next_halo(ts, SC_HALO, w, 0, s),
                  _rows(ts, w, 0), _rows(ts, w, 1), _rows(ts, w, 2), _rows(ts, w, 3), _rows(ts, w, 5),
                  _prev_halo(ts, SC_HALO, w, 1), _prev_halo(ts, SC_HALO, w, 2),
                  _next_halo(ts, SC_HALO, w, 0, s), _next_halo(ts, SC_HALO, w, 3, s),
                  _rows(ts, w), _vec(w, 0, SC_KERNEL), _vec(w)],
        out_specs=[_rows(ts, EVEN_PAD), _rows(ts, w), pl.BlockSpec((HEADS // 2, 1, HEAD_PAD, ts), lambda i: (0, i, 0, 0)),
                   _vec(w, 0, SC_KERNEL), _vec(w)],
        out_shape=[_sds((s, EVEN_PAD), MXU_DTYPE), _sds((s, w)), _sds((HEADS // 2, n, HEAD_PAD, ts), MXU_DTYPE),
                   _sds((SC_KERNEL, w)), _sds((1, w))],
        scratch_shapes=[pltpu.VMEM((ts + SC_HALO, w), F32), pltpu.VMEM((ts + SC_HALO, w), F32)],
        compiler_params=_cp(1))(dy, dy, dy, z, z, z, z, z, z, z, z, z, o, sc_w, sc_b)


def _ln_act(uc, sg, g, b):
    mu = jnp.mean(uc, axis=-1, keepdims=True)
    var = jnp.mean(jnp.square(uc - mu), axis=-1, keepdims=True)
    return _silu((uc - mu) * lax.rsqrt(var + EPS) * g + b) * _silu(sg)


def _shifted_copies(buf, shifted, rows):
    for b in range(1, SUBLANES):
        shifted[b - 1, 0:rows, :] = buf[pl.ds(b, rows), :]


def _rows_at(buf, shifted, start, n):
    a, b = divmod(start, SUBLANES)
    return buf[pl.ds(SUBLANES * a, n), :] if b == 0 else shifted[b - 1, pl.ds(SUBLANES * a, n), :]


def _odd_fwd(z, conv_w, conv_b, ln_g, ln_b, ts):
    s = z.shape[0]
    d = D_MODEL
    k = CONF_KERNEL

    def body(val_ref, glu_ref, sg_ref, hval_ref, hglu_ref, w_ref, b_ref, g_ref, beta_ref, y_ref, uc_ref, ubuf, ush):
        i = pl.program_id(0)
        ubuf[0:CONF_HALO, :] = jnp.where(i > 0, hval_ref[...] * _sigmoid(hglu_ref[...]), 0.0)
        ubuf[CONF_HALO:, :] = val_ref[...] * _sigmoid(glu_ref[...])
        _shifted_copies(ubuf, ush, ts + CONF_HALO - SUBLANES)
        for r0 in range(0, ts, CONV_ROWS):
            acc = jnp.broadcast_to(b_ref[...], (CONV_ROWS, d))
            for j in range(k):
                acc = acc + w_ref[j:j + 1, :] * _rows_at(ubuf, ush, r0 + CONF_HALO - (k - 1) + j, CONV_ROWS)
            uc_ref[r0:r0 + CONV_ROWS, :] = acc
        y_ref[...] = _ln_act(uc_ref[...], sg_ref[...], g_ref[...], beta_ref[...]).astype(y_ref.dtype)

    return pl.pallas_call(
        body, name="odd_fwd", grid=(s // ts,),
        in_specs=[_rows(ts, d, 0), _rows(ts, d, 1), _rows(ts, d, 2),
                  _prev_halo(ts, CONF_HALO, d, 0), _prev_halo(ts, CONF_HALO, d, 1),
                  _vec(d, 0, k), _vec(d), _vec(d), _vec(d)],
        out_specs=[_rows(ts, d), _rows(ts, d)], out_shape=[_sds((s, d), MXU_DTYPE), _sds((s, d))],
        scratch_shapes=[pltpu.VMEM((ts + CONF_HALO, d), F32),
                        pltpu.VMEM((SUBLANES - 1, ts + CONF_HALO - SUBLANES, d), F32)],
        compiler_params=_cp(1))(z, z, z, z, z, conv_w, conv_b, ln_g, ln_b)


def _odd_bwd(dy, z, uc, conv_w, ln_g, ln_b, ts):
    s = z.shape[0]
    d = D_MODEL
    k = CONF_KERNEL
    n = s // ts

    def body(dy_ref, dyn_ref, val_ref, glu_ref, sg_ref, sgn_ref, uc_ref, ucn_ref,
             w_ref, g_ref, beta_ref, dz_ref, dw_ref, db_ref, dg_ref, dbeta_ref, dbuf, dsh, dw_acc):
        i = pl.program_id(0)
        val, glu = val_ref[...], glu_ref[...]
        sig = _sigmoid(glu)
        u = val * sig
        _, vjp = jax.vjp(_ln_act, uc_ref[...], sg_ref[...], g_ref[...], beta_ref[...])
        duc, dsg, dg, dbeta = vjp(dy_ref[...])
        _, vjp_n = jax.vjp(_ln_act, ucn_ref[...], sgn_ref[...], g_ref[...], beta_ref[...])
        dbuf[0:ts, :] = duc
        dbuf[ts:, :] = jnp.where(i < n - 1, vjp_n(dyn_ref[...])[0], 0.0)
        dz_ref[:, 2 * d:] = dsg.astype(dz_ref.dtype)
        _shifted_copies(dbuf, dsh, ts + CONF_HALO - SUBLANES)

        @pl.when(i == 0)
        def _():
            dw_acc[...] = jnp.zeros_like(dw_acc)
            db_ref[...] = jnp.zeros_like(db_ref)
            dg_ref[...] = jnp.zeros_like(dg_ref)
            dbeta_ref[...] = jnp.zeros_like(dbeta_ref)

        db_ref[...] += jnp.sum(duc, axis=0, keepdims=True)
        dg_ref[...] += dg
        dbeta_ref[...] += dbeta
        for r0 in range(0, ts, CONV_ROWS):
            acc = jnp.zeros((CONV_ROWS, d), F32)
            for j in range(k):
                acc = acc + w_ref[j:j + 1, :] * _rows_at(dbuf, dsh, r0 + (k - 1) - j, CONV_ROWS)
            sig_r = sig[r0:r0 + CONV_ROWS, :]
            dz_ref[r0:r0 + CONV_ROWS, 0:d] = (acc * sig_r).astype(dz_ref.dtype)
            dz_ref[r0:r0 + CONV_ROWS, d:2 * d] = (acc * val[r0:r0 + CONV_ROWS, :] * sig_r * (1.0 - sig_r)).astype(dz_ref.dtype)
        for j in range(k):
            prod = _rows_at(dbuf, dsh, (k - 1) - j, ts) * u
            dw_acc[j] += jnp.sum(prod.reshape(ts // SUBLANES, SUBLANES, d), axis=0)

        @pl.when(i == n - 1)
        def _():
            dw_ref[...] = jnp.sum(dw_acc[...], axis=1)

    return pl.pallas_call(
        body, name="odd_bwd", grid=(n,),
        in_specs=[_rows(ts, d), _next_halo(ts, CONF_HALO, d, 0, s),
                  _rows(ts, d, 0), _rows(ts, d, 1), _rows(ts, d, 2), _next_halo(ts, CONF_HALO, d, 2, s),
                  _rows(ts, d), _next_halo(ts, CONF_HALO, d, 0, s),
                  _vec(d, 0, k), _vec(d), _vec(d)],
        out_specs=[_rows(ts, ODD_IN), _vec(d, 0, k), _vec(d), _vec(d), _vec(d)],
        out_shape=[_sds((s, ODD_IN), MXU_DTYPE), _sds((k, d)), _sds((1, d)), _sds((1, d)), _sds((1, d))],
        scratch_shapes=[pltpu.VMEM((ts + CONF_HALO, d), F32),
                        pltpu.VMEM((SUBLANES - 1, ts + CONF_HALO - SUBLANES, d), F32), pltpu.VMEM((k, SUBLANES, d), F32)],
        compiler_params=_cp(1))(dy, dy, z, z, z, z, uc, uc, conv_w, ln_g, ln_b)


def _local_step(x, target, cos, sin, mod, p, layer_weights, fwd_dep=None, grads_done=None):
    s = x.shape[0]
    tsf, tsb = min(512, s // 2), min(256, s // 2)
    tq = min(512, s // 2)
    row1 = lambda a, i: a[i:i + 1]
    saved = []
    for layer in range(DEPTH):
        i = layer // 2
        mod_l = row1(mod, layer)
        h = _pre_fwd(x, row1(p["pre_norm_g"], layer), mod_l, tsf, fwd_dep if layer == 0 else None)
        wl = layer_weights(layer, h)
        if layer % 2 == 0:
            z = _mm(h, wl["w_in"], "nn", F32, 256, EVEN_PAD, "even_in_fwd")
            if "late" in wl:
                wl.update(wl.pop("late")(z))
            q, q_t, k, v = _mla_prep_fwd(z, cos, sin, row1(p["even_q_norm_g"], i), row1(p["even_kv_norm_g"], i),
                                    wl["wq"], wl["wq_rot"], wl["wuk"], wl["wuv"], tsf)
            o, lse = _attn_fwd(q, k, v, tq)
            y = _even_gate_fwd(z, o, wl["sc_conv_w"], row1(p["even_sc_conv_b"], i), tsf)
            yo = _mm(y, wl["w_out"], "nn", F32, 512, 1024, "even_out_fwd")
            saved.append((x, h, z, y, yo, wl, (q, q_t, k, v, o, lse)))
        else:
            z = _mm(h, wl["w_in"], "nn", F32, 256, ODD_IN, "odd_in_fwd")
            y, uc = _odd_fwd(z, wl["conv_w"], wl["conv_b"], wl["ln_g"], wl["ln_b"], tsf)
            yo = _mm(y, wl["w_out"], "nn", F32, 512, 1024, "odd_out_fwd")
            saved.append((x, h, z, y, yo, wl, uc))
        x = _post_fwd(x, yo, row1(p["post_norm_g"], layer), mod_l, tsf)

    loss, dx = _loss_fwd_bwd(x, target, tsf)

    g = {n: [None] * (DEPTH if n in ("pre_norm_g", "post_norm_g") else N_PAIRS) for n in (
        "pre_norm_g", "post_norm_g", "even_sc_conv_w", "even_sc_conv_b", "even_q_norm_g", "even_kv_norm_g",
        "odd_conv_w", "odd_conv_b", "odd_ln_g", "odd_ln_b")}
    dmod = [None] * DEPTH
    dep = None
    for layer in reversed(range(DEPTH)):
        i = layer // 2
        mod_l = row1(mod, layer)
        x_in, h, z, y, yo, wl, extra = saved[layer]
        dyo, dgate, g["post_norm_g"][layer] = _post_bwd(dx, yo, row1(p["post_norm_g"], layer), mod_l, tsb, dep)
        bufs = {}
        if layer % 2 == 0:
            q, q_t, k, v, o, lse = extra
            dy = _mm(dyo, wl["w_out"], "nt", F32, 512, 1024, "even_out_bwd_x")
            bufs["even_w_out"] = _mm_tn_shards(y, dyo, "rows", "even_out_bwd_w")
            dz, do, do_t, g["even_sc_conv_w"][i], g["even_sc_conv_b"][i] = _even_gate_bwd(
                dy, z, o, wl["sc_conv_w"], row1(p["even_sc_conv_b"], i), tsb)
            dq, dk, dv = _attn_bwd(q, q_t, k, v, do, do_t, o, lse, tq)
            dz, bufs["even_mla"], g["even_q_norm_g"][i], g["even_kv_norm_g"][i] = _mla_prep_bwd(
                dz, dq, dk, dv, z, cos, sin, row1(p["even_q_norm_g"], i), row1(p["even_kv_norm_g"], i),
                wl["wq"], wl["wuk"], wl["wuv"], tsb)
            bufs["even_w_in"] = _ein_to_shards(_mm(h, dz, "tn", F32, D_MODEL, 512, "even_in_bwd_w"))
        else:
            uc = extra
            dy = _mm(dyo, wl["w_out"], "nt", F32, 512, 1024, "odd_out_bwd_x")
            bufs["odd_w_out"] = _mm_tn_shards(y, dyo, "rows", "odd_out_bwd_w")
            dz, g["odd_conv_w"][i], g["odd_conv_b"][i], g["odd_ln_g"][i], g["odd_ln_b"][i] = _odd_bwd(
                dy, z, uc, wl["conv_w"], wl["ln_g"], wl["ln_b"], tsb)
            bufs["odd_w_in"] = _mm_tn_shards(h, dz, "cols", "odd_in_bwd_w")
        dx, dshift, dscale, g["pre_norm_g"][layer] = _pre_bwd(
            dz, wl["w_in"], dx, x_in, row1(p["pre_norm_g"], layer), mod_l, tsb)
        dmod[layer] = jnp.concatenate([dshift, dscale, dgate], axis=-1)
        dep = grads_done(layer, bufs, dx) if grads_done is not None else None
    stack = lambda parts: jnp.stack([a[0] if a.shape[0] == 1 and a.ndim == 2 else a for a in parts])
    small = {n: stack(parts) for n, parts in g.items()}
    small["dmod"] = jnp.concatenate(dmod, axis=0)
    return loss, dx, small


def _uq_to_heads(w):
    w = w.reshape(N_CHIPS, Q_LORA, 2, QK_NOPE + QK_ROPE).transpose(0, 2, 1, 3).reshape(HEADS, Q_LORA, QK_NOPE + QK_ROPE)
    half = QK_ROPE // 2
    rotated = jnp.concatenate([jnp.zeros_like(w[..., :QK_NOPE]), -w[..., QK_NOPE + half:], w[..., QK_NOPE:QK_NOPE + half]],
                              axis=-1)
    pad = ((0, 0), (0, 0), (0, HEAD_PAD - QK_NOPE - QK_ROPE))
    return _side_by_side(jnp.pad(w, pad)), _side_by_side(jnp.pad(rotated, pad))


def _side_by_side(w):
    return w.transpose(1, 0, 2).reshape(w.shape[1], HEADS * HEAD_PAD)


def _ukv_to_heads(w):
    w = w.reshape(N_CHIPS, KV_LORA, 2, QK_NOPE + V_HEAD).transpose(0, 2, 1, 3).reshape(HEADS, KV_LORA, QK_NOPE + V_HEAD)
    wk = jnp.pad(w[..., :QK_NOPE], ((0, 0), (0, 0), (0, HEAD_PAD - QK_NOPE)))
    wv = w[..., QK_NOPE:]
    zero = jnp.zeros_like(wv)
    odd = (jnp.arange(HEADS) % 2 == 1)[:, None, None]
    wv = jnp.concatenate([jnp.where(odd, zero, wv), jnp.where(odd, wv, zero)], axis=-1)
    return _side_by_side(wk), _side_by_side(wv)


def _mla_local(q):
    blocks = q.reshape(2, MLA_ROWS, HEAD_PAD)
    uq = jnp.concatenate([blocks[r, :Q_LORA, :QK_NOPE + QK_ROPE] for r in range(2)], axis=-1)
    ukv = jnp.concatenate(
        [jnp.concatenate([blocks[r, Q_LORA:Q_LORA + KV_LORA, :QK_NOPE],
                          blocks[r, Q_LORA + KV_LORA:, V_HEAD * r:V_HEAD * (r + 1)]], axis=-1) for r in range(2)], axis=-1)
    return uq, ukv


def _place():
    return lax.axis_index("x"), lax.axis_index("y"), lax.axis_index("c")


def _flip(v, bit):
    return 1 - v if bit else v


def _sem(a, k):
    return a * (N_CHIPS - 1) + k - 1


def _remote(src, dst, send_sem, recv_sem, peer):
    return pltpu.make_async_remote_copy(src_ref=src, dst_ref=dst, send_sem=send_sem, recv_sem=recv_sem,
                                        device_id=peer, device_id_type=MESH)


_VMEM_SPEC = pl.BlockSpec(memory_space=pltpu.VMEM)
_HBM_SPEC = pl.BlockSpec(memory_space=pl.ANY)


def _ada_fwd(c8, ada_w, ada_b_sh):
    depth, d, cols = ada_w.shape

    def body(c_ref, w_ref, b_ref, call_ref, mod_ref, s1, r1, s2, r2):
        x, y, c = _place()
        chip = 2 * x + y
        me = 2 * chip + c
        call_ref[me] = c_ref[...]
        sends = []
        for k in range(1, N_DEV):
            peer = (_flip(x, k & 4), _flip(y, k & 2), _flip(c, k & 1))
            cp = _remote(c_ref, call_ref.at[me], s1.at[k - 1], r1.at[k - 1], peer)
            cp.start()
            sends.append(cp)
        for k in range(1, N_DEV):
            src = 4 * _flip(x, k & 4) + 2 * _flip(y, k & 2) + _flip(c, k & 1)
            _remote(c_ref, call_ref.at[src], s1.at[k - 1], r1.at[k - 1], (x, y, c)).wait_recv()
        act = _silu(jnp.concatenate([call_ref[e, 0:1, :] for e in range(N_DEV)], axis=0))
        for l in range(depth):
            mod_ref[chip, l] = _dot_nn(act, w_ref[l]) + b_ref[l:l + 1, :]
        for k in range(1, N_CHIPS):
            peer = (_flip(x, k & 2), _flip(y, k & 1), c)
            cp = _remote(mod_ref.at[chip], mod_ref.at[chip], s2.at[k - 1], r2.at[k - 1], peer)
            cp.start()
            sends.append(cp)
        for k in range(1, N_CHIPS):
            src = 2 * _flip(x, k & 2) + _flip(y, k & 1)
            _remote(mod_ref.at[src], mod_ref.at[src], s2.at[k - 1], r2.at[k - 1], (x, y, c)).wait_recv()
        for cp in sends:
            cp.wait_send()

    return pl.pallas_call(
        body, name="ada_fwd", in_specs=[_VMEM_SPEC] * 3, out_specs=[_VMEM_SPEC] * 2,
        out_shape=[_sds((N_DEV, 8, d)), _sds((N_CHIPS, depth, N_DEV, cols))],
        scratch_shapes=[pltpu.SemaphoreType.DMA((N_DEV - 1,)), pltpu.SemaphoreType.DMA((N_DEV - 1,)),
                        pltpu.SemaphoreType.DMA((N_CHIPS - 1,)), pltpu.SemaphoreType.DMA((N_CHIPS - 1,))],
        compiler_params=pltpu.CompilerParams(vmem_limit_bytes=VMEM_LIMIT_V7X))(c8, ada_w, ada_b_sh)


def _ada_bwd(c_t, dmod_sh):
    depth, n, cols = dmod_sh.shape
    d = c_t.shape[0]
    tr = 256

    def body(c_ref, dm_ref, o_ref):
        act = _silu(c_ref[...])
        acc = act[:, 0:1] * dm_ref[0, 0:1, :]
        for e in range(1, n):
            acc = acc + act[:, e:e + 1] * dm_ref[0, e:e + 1, :]
        o_ref[0] = acc

    return pl.pallas_call(
        body, name="ada_bwd", grid=(depth, d // tr),
        in_specs=[pl.BlockSpec((tr, n), lambda l, i: (i, 0)), pl.BlockSpec((1, n, cols), lambda l, i: (l, 0, 0))],
        out_specs=pl.BlockSpec((1, tr, cols), lambda l, i: (l, i, 0)), out_shape=_sds((depth, d, cols)),
        compiler_params=_cp(2))(c_t, dmod_sh)


def _gathered_shape(shape, how):
    if how == "slot":
        return (N_CHIPS,) + shape
    r, cc = shape
    return (r, N_CHIPS * cc) if how == "cols" else (N_CHIPS * r, cc)


def _gathered_part(ref, shape, how, chip):
    if how == "slot":
        return ref.at[chip]
    if how == "cols":
        return ref.at[:, pl.ds(pl.multiple_of(chip * shape[1], 128), shape[1])]
    return ref.at[pl.ds(pl.multiple_of(chip * shape[0], 8), shape[0]), :]


_SEM_SPEC = pl.BlockSpec(memory_space=pltpu.SEMAPHORE)
_TOKEN = jax.ShapeDtypeStruct((8, 128), F32)
_SPLIT_COPY = pltpu.CompilerParams(has_side_effects=pltpu.SideEffectType.DATAFLOW_SIDE_EFFECTING)


def _in_hbm(a):
    return pltpu.with_memory_space_constraint(a, pltpu.HBM)


def _gather_start(items, gathered, name, after=()):
    n = len(items)

    def body(*refs):
        ins, outs = refs[:n], refs[n:2 * n]
        send_sems, recv_sems = refs[2 * n + len(after)], refs[2 * n + len(after) + 1]
        x, y, c = _place()
        for a in range(n):
            for k in range(1, N_CHIPS):
                part = _gathered_part(outs[a], items[a][0].shape, items[a][1], 2 * x + y)
                _remote(ins[a], part, send_sems.at[_sem(a, k)], recv_sems.at[_sem(a, k)],
                        (_flip(x, k & 2), _flip(y, k & 1), c)).start()
        refs[-1][...] = jnp.zeros(_TOKEN.shape, _TOKEN.dtype)

    arrays = [_in_hbm(a) for a, _ in items] + [_in_hbm(a) for a in gathered]
    res = pl.pallas_call(
        body, name=name, in_specs=[_HBM_SPEC] * (2 * n + len(after)),
        out_specs=[_SEM_SPEC, _SEM_SPEC] + [_HBM_SPEC] * (2 * n) + [_VMEM_SPEC],
        out_shape=[pltpu.SemaphoreType.DMA((n * (N_CHIPS - 1),)), pltpu.SemaphoreType.DMA((n * (N_CHIPS - 1),))]
        + [pltpu.HBM(a.shape, a.dtype) for a in arrays] + [_TOKEN],
        input_output_aliases={a: 2 + a for a in range(2 * n)}, compiler_params=_SPLIT_COPY)(*arrays, *after)
    return res[0], res[1], res[2:2 + n], res[2 + n:2 + 2 * n], res[-1]


def _gather_wait(items, started, after, name):
    n = len(items)
    send_sems, recv_sems, shards, gathered, _ = started

    def body(*refs):
        ins, outs, send_sems, recv_sems = refs[:n], refs[n:2 * n], refs[2 * n], refs[2 * n + 1]
        x, y, c = _place()
        for a in range(n):
            for k in range(1, N_CHIPS):
                part = _gathered_part(outs[a], items[a][0].shape, items[a][1], 2 * _flip(x, k & 2) + _flip(y, k & 1))
                cp = _remote(ins[a], part, send_sems.at[_sem(a, k)], recv_sems.at[_sem(a, k)], (x, y, c))
                cp.wait_send()
                cp.wait_recv()

    res = pl.pallas_call(
        body, name=name, in_specs=[_HBM_SPEC] * (2 * n) + [_SEM_SPEC, _SEM_SPEC] + [_HBM_SPEC] * len(after),
        out_specs=[_HBM_SPEC] * (2 * n), out_shape=[pltpu.HBM(a.shape, a.dtype) for a in (*shards, *gathered)],
        input_output_aliases={a: a for a in range(2 * n)}, compiler_params=_SPLIT_COPY)(
            *shards, *gathered, send_sems, recv_sems, *after)
    return res[n:]


def _rs_start(bufs, name, after=()):
    n = len(bufs)

    def body(*refs):
        srcs, lands = refs[:n], refs[n:2 * n]
        send_sems, recv_sems = refs[2 * n + len(after)], refs[2 * n + len(after) + 1]
        x, y, c = _place()
        for a in range(n):
            for k in range(1, N_CHIPS):
                tx, ty = _flip(x, k & 2), _flip(y, k & 1)
                _remote(srcs[a].at[2 * tx + ty], lands[a].at[k - 1], send_sems.at[_sem(a, k)], recv_sems.at[_sem(a, k)],
                        (tx, ty, c)).start()
        refs[-1][...] = jnp.zeros(_TOKEN.shape, _TOKEN.dtype)

    arrays = [_in_hbm(b) for b in bufs] + [_in_hbm(lax.empty((N_CHIPS - 1,) + b.shape[1:], b.dtype)) for b in bufs]
    res = pl.pallas_call(
        body, name=name, in_specs=[_HBM_SPEC] * (2 * n + len(after)),
        out_specs=[_SEM_SPEC, _SEM_SPEC] + [_HBM_SPEC] * (2 * n) + [_VMEM_SPEC],
        out_shape=[pltpu.SemaphoreType.DMA((n * (N_CHIPS - 1),)), pltpu.SemaphoreType.DMA((n * (N_CHIPS - 1),))]
        + [pltpu.HBM(a.shape, a.dtype) for a in arrays] + [_TOKEN],
        input_output_aliases={a: 2 + a for a in range(2 * n)}, compiler_params=_SPLIT_COPY)(*arrays, *after)
    return res[0], res[1], res[2:2 + n], res[2 + n:2 + 2 * n], res[-1]


def _rs_wait(started, after, name):
    send_sems, recv_sems, bufs, lands, _ = started
    n = len(bufs)

    def body(*refs):
        srcs, lnds, send_sems, recv_sems = refs[:n], refs[n:2 * n], refs[2 * n], refs[2 * n + 1]
        x, y, c = _place()
        for a in range(n):
            for k in range(1, N_CHIPS):
                cp = _remote(srcs[a].at[0], lnds[a].at[k - 1], send_sems.at[_sem(a, k)], recv_sems.at[_sem(a, k)], (x, y, c))
                cp.wait_send()
                cp.wait_recv()

    res = pl.pallas_call(
        body, name=name, in_specs=[_HBM_SPEC] * (2 * n) + [_SEM_SPEC, _SEM_SPEC] + [_HBM_SPEC] * len(after),
        out_specs=[_HBM_SPEC] * (2 * n), out_shape=[pltpu.HBM(a.shape, a.dtype) for a in (*bufs, *lands)],
        input_output_aliases={a: a for a in range(2 * n)}, compiler_params=_SPLIT_COPY)(
            *bufs, *lands, send_sems, recv_sems, *after)
    return res[:n], res[n:]


def _place_own(shard, how, chip_idx):
    r, cc = shard.shape
    block, index = {"slot": ((1, r, cc), lambda i, c: (c[0], 0, 0)), "cols": ((r, cc), lambda i, c: (0, c[0])),
                    "rows": ((r, cc), lambda i, c: (c[0], 0))}[how]

    def body(c_ref, in_ref, o_ref):
        del c_ref
        o_ref[...] = in_ref[...].reshape(o_ref.shape)

    return pl.pallas_call(
        body, name="place_own", out_shape=_sds(_gathered_shape(shard.shape, how), shard.dtype),
        grid_spec=pltpu.PrefetchScalarGridSpec(
            num_scalar_prefetch=1, grid=(1,), in_specs=[pl.BlockSpec((r, cc), lambda i, c: (0, 0))],
            out_specs=pl.BlockSpec(block, index)),
        compiler_params=_cp(1))(chip_idx, shard)


def _gather_sum_all(small):
    r, w = small.shape

    def body(in_ref, all_ref, sum_ref, send_sems, recv_sems):
        x, y, c = _place()
        me = 4 * x + 2 * y + c
        all_ref[me] = in_ref[...]
        sends = []
        for k in range(1, N_DEV):
            peer = (_flip(x, k & 4), _flip(y, k & 2), _flip(c, k & 1))
            cp = _remote(in_ref, all_ref.at[me], send_sems.at[k - 1], recv_sems.at[k - 1], peer)
            cp.start()
            sends.append(cp)
        for k in range(1, N_DEV):
            src = 4 * _flip(x, k & 4) + 2 * _flip(y, k & 2) + _flip(c, k & 1)
            _remote(in_ref, all_ref.at[src], send_sems.at[k - 1], recv_sems.at[k - 1], (x, y, c)).wait_recv()
        acc = all_ref[0]
        for e in range(1, N_DEV):
            acc = acc + all_ref[e]
        sum_ref[...] = acc
        for cp in sends:
            cp.wait_send()

    return pl.pallas_call(
        body, name="gather_sum_all", in_specs=[_VMEM_SPEC], out_specs=[_VMEM_SPEC] * 2,
        out_shape=[_sds((N_DEV, r, w)), _sds((r, w))],
        scratch_shapes=[pltpu.SemaphoreType.DMA((N_DEV - 1,)), pltpu.SemaphoreType.DMA((N_DEV - 1,))],
        compiler_params=pltpu.CompilerParams(vmem_limit_bytes=VMEM_LIMIT_V7X))(small)


def _add_chips(buf, t, chip_idx):
    r, cc = buf.shape[1:]
    tr = min(256, r)

    def body(c_ref, p_ref, t_ref, o_ref):
        del c_ref
        o_ref[...] = p_ref[0] + t_ref[0].astype(F32) + t_ref[1].astype(F32) + t_ref[2].astype(F32)

    return pl.pallas_call(
        body, name="add_chips", out_shape=_sds((r, cc)),
        grid_spec=pltpu.PrefetchScalarGridSpec(
            num_scalar_prefetch=1, grid=(r // tr,),
            in_specs=[pl.BlockSpec((1, tr, cc), lambda i, c: (c[0], i, 0)),
                      pl.BlockSpec((N_CHIPS - 1, tr, cc), lambda i, c: (0, i, 0))],
            out_specs=pl.BlockSpec((tr, cc), lambda i, c: (i, 0))),
        compiler_params=_cp(1))(chip_idx, buf, t)


def _rs_sibling(qs):
    n = len(qs)

    def body(*refs):
        ins, outs = refs[:n], refs[n:2 * n]
        send_sems, recv_sems = refs[2 * n:]
        x, y, c = _place()
        copies = [_remote(ins[a], outs[a], send_sems.at[a], recv_sems.at[a], (x, y, 1 - c)) for a in range(n)]
        for cp in copies:
            cp.start()
        for cp in copies:
            cp.wait()

    return pl.pallas_call(
        body, name="rs_sibling", in_specs=[_HBM_SPEC] * n, out_specs=[_HBM_SPEC] * n,
        out_shape=[_sds(q.shape) for q in qs],
        scratch_shapes=[pltpu.SemaphoreType.DMA((n,)), pltpu.SemaphoreType.DMA((n,))])(*qs)


def _adamw_update(w, g, m, v):
    m = ADAM_B1 * m + (1.0 - ADAM_B1) * g
    v = ADAM_B2 * v + (1.0 - ADAM_B2) * jnp.square(g)
    m_hat = m / (1.0 - ADAM_B1 ** ADAM_STEP)
    v_hat = v / (1.0 - ADAM_B2 ** ADAM_STEP)
    return -ADAM_LR * (m_hat / (jnp.sqrt(v_hat) + ADAM_EPS) + ADAM_WD * w), m, v


def _adamw(w, g_parts, m, v, name):
    shape = w.shape
    cols = shape[-1]
    rows = _size(shape[:-1])
    tr = 512 if rows % 512 == 0 else rows
    spec = pl.BlockSpec((tr, cols), lambda i: (i, 0))
    n = len(g_parts)
    n_out = 4 if n > 1 else 3

    def body(*refs):
        w_ref, m_ref, v_ref = refs[:3]
        d_ref, nm_ref, nv_ref = refs[-3:]
        g = refs[3][...]
        for r in refs[4:3 + n]:
            g = g + r[...]
        if n > 1:
            refs[3 + n][...] = g
        d_ref[...], nm_ref[...], nv_ref[...] = _adamw_update(w_ref[...], g, m_ref[...], v_ref[...])

    outs = pl.pallas_call(
        body, name="adamw_" + name, grid=(rows // tr,), in_specs=[spec] * (3 + n), out_specs=[spec] * n_out,
        out_shape=[_sds((rows, cols))] * n_out, compiler_params=_cp(1))(
            *[a.reshape(rows, cols) for a in (w, m, v, *g_parts)])
    outs = tuple(o.reshape(shape) for o in outs)
    return outs if n > 1 else (g_parts[0],) + outs


def _adamw_layer(w, g_parts, m, v, layer, prev, name):
    _, r, cc = w.shape
    tr = 512 if r % 512 == 0 else r
    spec = pl.BlockSpec((1, tr, cc), lambda i: (layer, i, 0))
    n = len(g_parts)

    def body(*refs):
        w_ref, m_ref, v_ref = refs[:3]
        g_ref, d_ref, nm_ref, nv_ref = refs[-4:]
        g = refs[3][...]
        for q in refs[4:3 + n]:
            g = g + q[...]
        g = g[:, :cc]
        g_ref[0] = g
        d_ref[0], nm_ref[0], nv_ref[0] = _adamw_update(w_ref[0], g, m_ref[0], v_ref[0])

    g_specs = [pl.BlockSpec((tr, q.shape[1]), lambda i: (i, 0)) for q in g_parts]
    passed = () if prev is None else tuple(prev)
    return pl.pallas_call(
        body, name="adamw_" + name, grid=(r // tr,),
        in_specs=[spec] * 3 + g_specs + [_HBM_SPEC] * len(passed), out_specs=[spec] * 4,
        out_shape=[_sds(w.shape)] * 4, input_output_aliases={3 + n + k: k for k in range(len(passed))},
        compiler_params=_cp(1))(w, m, v, *g_parts, *passed)


def _size(shape):
    n = 1
    for s in shape:
        n *= s
    return n


_SMALL = (("dmod", (DEPTH, 3 * D_MODEL)), ("pre_norm_g", (DEPTH, D_MODEL)), ("post_norm_g", (DEPTH, D_MODEL)),
          ("even_sc_conv_w", (2, SC_KERNEL, SC_WIDTH)), ("even_sc_conv_b", (2, SC_WIDTH)),
          ("even_q_norm_g", (2, Q_LORA)), ("even_kv_norm_g", (2, KV_LORA)),
          ("odd_conv_w", (2, CONF_KERNEL, D_MODEL)), ("odd_conv_b", (2, D_MODEL)), ("odd_ln_g", (2, D_MODEL)),
          ("odd_ln_b", (2, D_MODEL)))
SMALL_ROWS = -(-sum(_size(s) for _, s in _SMALL) // (8 * 128)) * 8

_SMALL_W = (("even_sc_conv_w", (2, SC_KERNEL, SC_WIDTH // N_CHIPS)), ("odd_conv_w", (2, CONF_KERNEL, D_MODEL // N_CHIPS)),
            ("odd_conv_b", (2, D_MODEL // N_CHIPS)), ("odd_ln_g", (2, D_MODEL // N_CHIPS)),
            ("odd_ln_b", (2, D_MODEL // N_CHIPS)))
SMALL_W_ROWS = -(-sum(_size(s) for _, s in _SMALL_W) // (8 * 128)) * 8


def _pack_rows(arrays, layout, rows):
    flat = jnp.concatenate([arrays[n].reshape(-1) for n, _ in layout])
    return jnp.pad(flat, (0, rows * 128 - flat.shape[0])).reshape(rows, 128)


def _unpack_small(t):
    flat = t.reshape(-1)
    out, at = {}, 0
    for n, shape in _SMALL:
        out[n] = flat[at:at + _size(shape)].reshape(shape)
        at += _size(shape)
    return out


def _unpack_small_w(t):
    flat = t.reshape(N_CHIPS, -1)
    out, at = {}, 0
    for n, shape in _SMALL_W:
        a = flat[:, at:at + _size(shape)].reshape((N_CHIPS,) + shape)
        out[n] = jnp.moveaxis(a, 0, -2).reshape(shape[:-1] + (N_CHIPS * shape[-1],))
        at += _size(shape)
    return out


def _chip_cols(a, chip):
    n = a.shape[-1] // N_CHIPS
    return lax.dynamic_slice_in_dim(a, chip * n, n, axis=a.ndim - 1)


WEIGHT_NAMES = ("ada_w", "ada_b", "pre_norm_g", "post_norm_g", "even_w_in", "even_sc_conv_w", "even_sc_conv_b",
                "even_q_norm_g", "even_kv_norm_g", "even_w_uq", "even_w_ukv", "even_w_out", "odd_w_in", "odd_conv_w",
                "odd_conv_b", "odd_ln_g", "odd_ln_b", "odd_w_out")
GATHER_HOW = ((("even_w_in", "slot"), ("even_w_uq", "slot"), ("even_w_ukv", "slot"), ("even_w_out", "rows")),
              (("odd_w_in", "cols"), ("odd_w_out", "rows")))


def kernel(x, c, positions, ada_w, ada_b, pre_norm_g, post_norm_g, even_w_in, even_sc_conv_w, even_sc_conv_b, even_q_norm_g, even_kv_norm_g, even_w_uq, even_w_ukv, even_w_out, odd_w_in, odd_conv_w, odd_conv_b, odd_ln_g, odd_ln_b, odd_w_out, loss_target, m_ada_w, m_ada_b, m_pre_norm_g, m_post_norm_g, m_even_w_in, m_even_sc_conv_w, m_even_sc_conv_b, m_even_q_norm_g, m_even_kv_norm_g, m_even_w_uq, m_even_w_ukv, m_even_w_out, m_odd_w_in, m_odd_conv_w, m_odd_conv_b, m_odd_ln_g, m_odd_ln_b, m_odd_w_out, v_ada_w, v_ada_b, v_pre_norm_g, v_post_norm_g, v_even_w_in, v_even_sc_conv_w, v_even_sc_conv_b, v_even_q_norm_g, v_even_kv_norm_g, v_even_w_uq, v_even_w_ukv, v_even_w_out, v_odd_w_in, v_odd_conv_w, v_odd_conv_b, v_odd_ln_g, v_odd_ln_b, v_odd_w_out):
    w = dict(zip(WEIGHT_NAMES, (ada_w, ada_b, pre_norm_g, post_norm_g, even_w_in, even_sc_conv_w, even_sc_conv_b,
                                even_q_norm_g, even_kv_norm_g, even_w_uq, even_w_ukv, even_w_out, odd_w_in, odd_conv_w,
                                odd_conv_b, odd_ln_g, odd_ln_b, odd_w_out)))
    m = dict(zip(WEIGHT_NAMES, (m_ada_w, m_ada_b, m_pre_norm_g, m_post_norm_g, m_even_w_in, m_even_sc_conv_w,
                                m_even_sc_conv_b, m_even_q_norm_g, m_even_kv_norm_g, m_even_w_uq, m_even_w_ukv,
                                m_even_w_out, m_odd_w_in, m_odd_conv_w, m_odd_conv_b, m_odd_ln_g, m_odd_ln_b, m_odd_w_out)))
    v = dict(zip(WEIGHT_NAMES, (v_ada_w, v_ada_b, v_pre_norm_g, v_post_norm_g, v_even_w_in, v_even_sc_conv_w,
                                v_even_sc_conv_b, v_even_q_norm_g, v_even_kv_norm_g, v_even_w_uq, v_even_w_ukv,
                                v_even_w_out, v_odd_w_in, v_odd_conv_w, v_odd_conv_b, v_odd_ln_g, v_odd_ln_b, v_odd_w_out)))
    ix, iy, ic = _place()
    chip = 2 * ix + iy
    me = 2 * chip + ic
    s = x.shape[1]

    c_all, mod_all = _ada_fwd(jnp.broadcast_to(c, (8, D_MODEL)), ada_w, _chip_cols(ada_b, chip))
    mod = lax.dynamic_index_in_dim(mod_all, me, axis=2, keepdims=False)
    mod = mod.transpose(1, 0, 2).reshape(DEPTH, 3 * D_MODEL)

    items = [[(w[n][layer // 2].astype(MXU_DTYPE), how) for n, how in GATHER_HOW[layer % 2]] for layer in range(DEPTH)]
    groups = [items[0][:1], items[0][1:] + [(_pack_rows(w, _SMALL_W, SMALL_W_ROWS), "slot")],
              [item for layer_items in items[1:] for item in layer_items]]
    sent, dep = [], mod_all
    for number, group in enumerate(groups):
        sent.append(_gather_start(group, [_place_own(a, how, chip.reshape(1)) for a, how in group],
                                  "gather_start_%d" % number, [dep]))
        dep = sent[-1][-1]
    arrived = {}

    def group(number, after):
        if number not in arrived:
            arrived[number] = _gather_wait(groups[number], sent[number], after, "gather_wait_%d" % number)
        return arrived[number]

    def even_rest(i, uq, ukv, eout, small_w):
        wuk, wuv = _ukv_to_heads(ukv)
        wq, wq_rot = _uq_to_heads(uq)
        return {"wq": wq, "wq_rot": wq_rot, "wuk": wuk, "wuv": wuv, "w_out": eout, "sc_conv_w": small_w["even_sc_conv_w"][i]}

    def layer_weights(layer, h):
        i = layer // 2
        if layer == 0:
            def late(z):
                uq, ukv, eout, small = group(1, [z])
                return even_rest(i, uq, ukv, eout, _unpack_small_w(small))
            return {"w_in": _ein_from_shards(group(0, [h])[0]), "late": late}
        small_w = _unpack_small_w(group(1, [h])[-1])
        at = sum(len(layer_items) for layer_items in items[1:layer])
        arrays = group(2, [h])[at:at + len(items[layer])]
        if layer % 2 == 0:
            return {"w_in": _ein_from_shards(arrays[0]), **even_rest(i, *arrays[1:], small_w)}
        oin, oout = arrays
        return {"w_in": oin, "w_out": oout, "conv_w": small_w["odd_conv_w"][i], "conv_b": small_w["odd_conv_b"][i:i + 1],
                "ln_g": small_w["odd_ln_g"][i:i + 1], "ln_b": small_w["odd_ln_b"][i:i + 1]}

    in_flight, own, sib, last = {}, {}, {}, {}

    def land(layer, after):
        names, started, kept = in_flight.pop(layer)
        bufs, arrived = _rs_wait(started, after, "rs_wait_%d" % layer)
        sums = [_add_chips(b, t, chip.reshape(1)) for b, t in zip(bufs if kept is None else kept, arrived)]
        for n, mine, theirs in zip(names, sums, _rs_sibling(sums)):
            own[n, layer // 2], sib[n, layer // 2] = mine, theirs

    def grads_done(layer, bufs, dx_in):
        if layer + 1 in in_flight:
            land(layer + 1, [dx_in])
        if layer == 0:
            last.update(bufs)
            return None
        names = sorted(bufs)
        in_flight[layer] = (names, _rs_start([bufs[n] for n in names], "rs_start_%d" % layer), None)
        return in_flight[layer][1][-1]

    p = {"pre_norm_g": pre_norm_g, "post_norm_g": post_norm_g, "even_sc_conv_b": even_sc_conv_b,
         "even_q_norm_g": even_q_norm_g, "even_kv_norm_g": even_kv_norm_g}
    inv_freq = 1.0 / (ROPE_THETA ** (jnp.arange(0, QK_ROPE, 2, dtype=F32) / QK_ROPE))
    inv_freq = jnp.zeros((1, HEAD_PAD), F32).at[0, QK_NOPE:QK_NOPE + QK_ROPE].set(jnp.tile(inv_freq, 2))
    cos, sin = _rope_tables(positions.reshape(s, 1), inv_freq)

    loss, dx, g = _local_step(x[0], loss_target[0], cos, sin, mod, p, layer_weights, dep, grads_done)

    grads, deltas, new_m, new_v = {}, {}, {}, {}

    def update_layers(n, results, pairs):
        for i in pairs:
            results = _adamw_layer(w[n], [own[n, i], sib[n, i]], m[n], v[n], i, results, n)
        return results

    small_all, small_sum = _gather_sum_all(_pack_rows(g, _SMALL, SMALL_ROWS))
    names = sorted(last)
    kept = [last[n] for n in names]
    in_flight[0] = (names, _rs_start([b.astype(jnp.bfloat16) for b in kept], "rs_start_0", [small_sum]), kept)
    tot = _unpack_small(small_sum)
    dmod_all = small_all[:, :DEPTH * 3 * D_MODEL // 128].reshape(N_DEV, DEPTH, 3 * D_MODEL)
    grads["ada_w"] = _ada_bwd(c_all[:, 0, :].T, _chip_cols(dmod_all, chip).transpose(1, 0, 2))
    grads["ada_b"] = tot["dmod"]
    for n in ("pre_norm_g", "post_norm_g", "even_sc_conv_b", "even_q_norm_g", "even_kv_norm_g"):
        grads[n] = tot[n]
    for n in ("even_sc_conv_w", "odd_conv_w", "odd_conv_b", "odd_ln_g", "odd_ln_b"):
        grads[n] = _chip_cols(tot[n], chip)
    for n in list(grads):
        _, deltas[n], new_m[n], new_v[n] = _adamw(w[n], [grads[n]], m[n], v[n], n)

    for n in ("odd_w_in", "odd_w_out"):
        grads[n], deltas[n], new_m[n], new_v[n] = update_layers(n, None, (1, 0))
    partly = {n: update_layers(n, None, (1,)) for n in ("even_w_in", "even_w_out")}
    land(0, [deltas["ada_w"], deltas["odd_w_in"], partly["even_w_in"][1]])
    for n in ("even_w_in", "even_w_out"):
        grads[n], deltas[n], new_m[n], new_v[n] = update_layers(n, partly[n], (0,))
    uq_parts, ukv_parts = zip(*[[jnp.stack(part) for part in zip(*[_mla_local(q["even_mla", i]) for i in range(N_PAIRS)])]
                                for q in (own, sib)])
    for n, parts in (("even_w_uq", uq_parts), ("even_w_ukv", ukv_parts)):
        grads[n], deltas[n], new_m[n], new_v[n] = _adamw(w[n], list(parts), m[n], v[n], n)

    total_loss = lax.psum(loss[0, 0], ("x", "y", "c"))
    return (total_loss, dx[None], *[grads[n] for n in WEIGHT_NAMES], *[deltas[n] for n in WEIGHT_NAMES],
            *[new_m[n] for n in WEIGHT_NAMES], *[new_v[n] for n in WEIGHT_NAMES])
```

```python
import jax
import jax.numpy as jnp
from jax import lax
from jax.experimental import pallas as pl
from jax.experimental.pallas import tpu as pltpu

F32 = jnp.float32
MXU_DTYPE = jnp.bfloat16
MESH = pl.DeviceIdType.MESH
VMEM_LIMIT_V7X = 56 * 2 ** 20

EPS = 1e-6
D_MODEL = 1024
DEPTH = 4
CHUNK = 64
SC_WIDTH = 512
SC_KERNEL = 3
SC_HALO = 8
HEADS = 8
QK_NOPE = 64
QK_ROPE = 32
V_HEAD = 64
HEAD_PAD = 128
Q_LORA = 256
KV_LORA = 128
ROPE_THETA = 10000.0
CONF_KERNEL = 31
CONF_HALO = 32
CONV_ROWS = 32
SUBLANES = 8
EVEN_IN = 2976
EVEN_PAD = 3072
ODD_IN = 3072
N_CHIPS = 4
N_DEV = 8
NEG = -1e30

ADAM_LR = 0.001
ADAM_B1 = 0.9
ADAM_B2 = 0.999
ADAM_EPS = 1e-08
ADAM_WD = 0.01
ADAM_STEP = 10

N_PAIRS = DEPTH // 2
EVEN_SHARD = EVEN_IN // N_CHIPS
EVEN_SHARD_PAD = 768
MLA_ROWS = Q_LORA + 2 * KV_LORA


def _cp(n_grid=0, **kw):
    return pltpu.CompilerParams(dimension_semantics=("arbitrary",) * n_grid,
                                vmem_limit_bytes=VMEM_LIMIT_V7X, **kw)


def _sigmoid(x):
    return 1.0 / (1.0 + jnp.exp(-x))


def _silu(x):
    return x * _sigmoid(x)


def _dsilu(x):
    s = _sigmoid(x)
    return s * (1.0 + x * (1.0 - s))


def _rms(x, g):
    return x * lax.rsqrt(jnp.mean(x * x, axis=-1, keepdims=True) + EPS) * g


def _dot(a, b, dims):
    return lax.dot_general(a.astype(MXU_DTYPE), b.astype(MXU_DTYPE), (dims, ((), ())),
                           preferred_element_type=F32)


def _dot_nn(a, b):
    return _dot(a, b, ((1,), (0,)))


def _dot_nt(a, b):
    return _dot(a, b, ((1,), (1,)))


def _dot_tn(a, b):
    return _dot(a, b, ((0,), (0,)))


def _rows(ts, w, cb=0):
    return pl.BlockSpec((ts, w), lambda i: (i, cb))


def _vec(w, cb=0, r=1):
    return pl.BlockSpec((r, w), lambda i: (0, cb))


def _prev_halo(ts, hr, w, cb):
    return pl.BlockSpec((hr, w), lambda i: (jnp.maximum(i * (ts // hr) - 1, 0), cb))


def _next_halo(ts, hr, w, cb, s):
    return pl.BlockSpec((hr, w), lambda i: (jnp.minimum((i + 1) * (ts // hr), s // hr - 1), cb))


def _sds(shape, dtype=F32):
    return jax.ShapeDtypeStruct(shape, dtype)


def _mm(a, b, mode, out_dtype, tm, tn, name):
    tm = min(tm, a.shape[1] if mode == "tn" else a.shape[0])
    tn = min(tn, b.shape[0] if mode == "nt" else b.shape[1])
    if mode == "nn":
        (m, k), n = a.shape, b.shape[1]
        a_spec = pl.BlockSpec((tm, k), lambda i, j: (i, 0))
        b_spec = pl.BlockSpec((k, tn), lambda i, j: (0, j))
        dot = _dot_nn
    elif mode == "nt":
        (m, k), n = a.shape, b.shape[0]
        a_spec = pl.BlockSpec((tm, k), lambda i, j: (i, 0))
        b_spec = pl.BlockSpec((tn, k), lambda i, j: (j, 0))
        dot = _dot_nt
    else:
        (k, m), n = a.shape, b.shape[1]
        a_spec = pl.BlockSpec((k, tm), lambda i, j: (0, i))
        b_spec = pl.BlockSpec((k, tn), lambda i, j: (0, j))
        dot = _dot_tn
    assert m % tm == 0 and n % tn == 0, (name, m, n, tm, tn)

    def body(a_ref, b_ref, o_ref):
        o_ref[...] = dot(a_ref[...], b_ref[...]).astype(o_ref.dtype)

    return pl.pallas_call(
        body, name=name, grid=(m // tm, n // tn), in_specs=[a_spec, b_spec],
        out_specs=pl.BlockSpec((tm, tn), lambda i, j: (i, j)), out_shape=_sds((m, n), out_dtype),
        compiler_params=_cp(2))(a, b)


def _mm_tn_shards(a, b, by, name):
    k, m = a.shape
    n = b.shape[1]
    if by == "cols":
        tm, tn = m, n // N_CHIPS
        shape, grid = (N_CHIPS, m, tn), (1, N_CHIPS)
        out_spec = pl.BlockSpec((1, tm, tn), lambda i, j: (j, i, 0))
    else:
        tm, tn = m // N_CHIPS, n
        shape, grid = (N_CHIPS, tm, n), (N_CHIPS, 1)
        out_spec = pl.BlockSpec((1, tm, tn), lambda i, j: (i, 0, j))

    def body(a_ref, b_ref, o_ref):
        o_ref[0] = _dot_tn(a_ref[...], b_ref[...])

    return pl.pallas_call(
        body, name=name, grid=grid,
        in_specs=[pl.BlockSpec((k, tm), lambda i, j: (0, i)), pl.BlockSpec((k, tn), lambda i, j: (0, j))],
        out_specs=out_spec, out_shape=_sds(shape), compiler_params=_cp(2))(a, b)


def _even_col(q):
    return q if q < 2432 else (q + 64 if q < 2464 else q + 96)


def _shard_pieces(j):
    lo, hi = EVEN_SHARD * j, EVEN_SHARD * (j + 1)
    cuts = [lo] + [b for b in (2432, 2464) if lo < b < hi] + [hi]
    return [(a - lo, _even_col(a), b - a) for a, b in zip(cuts[:-1], cuts[1:])]


def _ein_from_shards(w):
    _, d, _ = w.shape
    tr = 256

    def body(w_ref, o_ref):
        parts, at = [], 0
        for j in range(N_CHIPS):
            for d0, s0, n in _shard_pieces(j):
                if s0 > at:
                    parts.append(jnp.zeros((tr, s0 - at), F32))
                parts.append(w_ref[j, :, d0:d0 + n].astype(F32))
                at = s0 + n
        o_ref[...] = jnp.concatenate(parts, axis=1).astype(o_ref.dtype)

    return pl.pallas_call(
        body, name="ein_from_shards", grid=(d // tr,),
        in_specs=[pl.BlockSpec((N_CHIPS, tr, EVEN_SHARD), lambda i: (0, i, 0))],
        out_specs=_rows(tr, EVEN_PAD), out_shape=_sds((d, EVEN_PAD), w.dtype), compiler_params=_cp(1))(w)


def _ein_to_shards(dw):
    d = dw.shape[0]
    tr = 256

    def body(dw_ref, o_ref):
        for j in range(N_CHIPS):
            parts = [dw_ref[:, s0:s0 + n] for _, s0, n in _shard_pieces(j)]
            o_ref[j] = jnp.concatenate(parts + [jnp.zeros((tr, EVEN_SHARD_PAD - EVEN_SHARD), F32)], axis=1)

    return pl.pallas_call(
        body, name="ein_to_shards", grid=(d // tr,), in_specs=[_rows(tr, EVEN_PAD)],
        out_specs=pl.BlockSpec((N_CHIPS, tr, EVEN_SHARD_PAD), lambda i: (0, i, 0)),
        out_shape=_sds((N_CHIPS, d, EVEN_SHARD_PAD)), compiler_params=_cp(1))(dw)


def _rope_tables(pos_col, invf):
    s = pos_col.shape[0]
    ts = min(512, s)

    def body(p_ref, f_ref, c_ref, s_ref):
        ang = p_ref[...].astype(F32) * f_ref[...]
        lane = lax.broadcasted_iota(jnp.int32, ang.shape, 1)
        rope = (lane >= QK_NOPE) & (lane < QK_NOPE + QK_ROPE)
        c_ref[...] = jnp.where(lane < QK_NOPE, 1.0, jnp.where(rope, jnp.cos(ang), 0.0))
        s_ref[...] = jnp.where(rope, jnp.sin(ang), 0.0)

    return pl.pallas_call(
        body, name="rope_tables", grid=(s // ts,), in_specs=[_rows(ts, 1), _vec(HEAD_PAD)],
        out_specs=[_rows(ts, HEAD_PAD)] * 2, out_shape=[_sds((s, HEAD_PAD))] * 2,
        compiler_params=_cp(1))(pos_col, invf)


def _after(dep):
    return () if dep is None else (dep,)


def _pre_fwd(x, g, mod_l, ts, dep=None):
    s, d = x.shape

    def body(x_ref, g_ref, sh_ref, sc_ref, *rest):
        h = _rms(x_ref[...], g_ref[...]) * (1.0 + sc_ref[...]) + sh_ref[...]
        rest[-1][...] = h.astype(rest[-1].dtype)

    return pl.pallas_call(
        body, name="pre_fwd", grid=(s // ts,),
        in_specs=[_rows(ts, d), _vec(d), _vec(d, 0), _vec(d, 1)] + [_HBM_SPEC] * len(_after(dep)),
        out_specs=_rows(ts, d), out_shape=_sds((s, d), MXU_DTYPE), compiler_params=_cp(1))(
            x, g, mod_l, mod_l, *_after(dep))


def _pre_bwd(dz, w_in, dx_out, x, g, mod_l, ts):
    s, d = x.shape
    n_in = dz.shape[1]

    def f(xv, gv, sh, sc):
        return _rms(xv, gv) * (1.0 + sc) + sh

    def body(dz_ref, w_ref, dxo_ref, x_ref, g_ref, sh_ref, sc_ref, dx_ref, dsh_ref, dsc_ref, dg_ref):
        @pl.when(pl.program_id(0) == 0)
        def _():
            dsh_ref[...] = jnp.zeros_like(dsh_ref)
            dsc_ref[...] = jnp.zeros_like(dsc_ref)
            dg_ref[...] = jnp.zeros_like(dg_ref)

        _, vjp = jax.vjp(f, x_ref[...], g_ref[...], sh_ref[...], sc_ref[...])
        dx, dg, dsh, dsc = vjp(_dot_nt(dz_ref[...], w_ref[...]))
        dx_ref[...] = dxo_ref[...] + dx
        dsh_ref[...] += dsh
        dsc_ref[...] += dsc
        dg_ref[...] += dg

    return pl.pallas_call(
        body, name="pre_bwd", grid=(s // ts,),
        in_specs=[_rows(ts, n_in), _vec(n_in, 0, d), _rows(ts, d), _rows(ts, d), _vec(d), _vec(d, 0), _vec(d, 1)],
        out_specs=[_rows(ts, d), _vec(d), _vec(d), _vec(d)],
        out_shape=[_sds((s, d)), _sds((1, d)), _sds((1, d)), _sds((1, d))],
        compiler_params=_cp(1))(dz, w_in, dx_out, x, g, mod_l, mod_l)


def _post_fwd(x, yo, g, mod_l, ts):
    s, d = x.shape

    def body(x_ref, yo_ref, g_ref, gate_ref, o_ref):
        o_ref[...] = x_ref[...] + gate_ref[...] * _rms(yo_ref[...], g_ref[...])

    return pl.pallas_call(
        body, name="post_fwd", grid=(s // ts,),
        in_specs=[_rows(ts, d), _rows(ts, d), _vec(d), _vec(d, 2)],
        out_specs=_rows(ts, d), out_shape=_sds((s, d)), compiler_params=_cp(1))(x, yo, g, mod_l)


def _post_bwd(dx_out, yo, g, mod_l, ts, dep=None):
    s, d = yo.shape

    def f(yov, gv, gate):
        return gate * _rms(yov, gv)

    def body(dx_ref, yo_ref, g_ref, gate_ref, *rest):
        dyo_ref, dgate_ref, dg_ref = rest[-3:]
        i = pl.program_id(0)
        _, vjp = jax.vjp(f, yo_ref[...], g_ref[...], gate_ref[...])
        dyo, dg, dgate = vjp(dx_ref[...])
        dyo_ref[...] = dyo.astype(dyo_ref.dtype)

        @pl.when(i == 0)
        def _():
            dgate_ref[...] = jnp.zeros_like(dgate_ref)
            dg_ref[...] = jnp.zeros_like(dg_ref)

        dgate_ref[...] += dgate
        dg_ref[...] += dg

    return pl.pallas_call(
        body, name="post_bwd", grid=(s // ts,),
        in_specs=[_rows(ts, d), _rows(ts, d), _vec(d), _vec(d, 2)] + [_HBM_SPEC] * len(_after(dep)),
        out_specs=[_rows(ts, d), _vec(d), _vec(d)],
        out_shape=[_sds((s, d), MXU_DTYPE), _sds((1, d)), _sds((1, d))],
        compiler_params=_cp(1))(dx_out, yo, g, mod_l, *_after(dep))


def _loss_fwd_bwd(x, target, ts):
    s, d = x.shape

    def body(x_ref, t_ref, loss_ref, dx_ref):
        i = pl.program_id(0)
        err = x_ref[...] - t_ref[...]
        dx_ref[...] = err * (1.0 / d)

        @pl.when(i == 0)
        def _():
            loss_ref[...] = jnp.zeros_like(loss_ref)

        loss_ref[...] += 0.5 * jnp.sum(jnp.sum(err * err, axis=-1, keepdims=True) * (1.0 / d), axis=0, keepdims=True)

    return pl.pallas_call(
        body, name="loss", grid=(s // ts,), in_specs=[_rows(ts, d), _rows(ts, d)],
        out_specs=[_vec(1), _rows(ts, d)], out_shape=[_sds((1, 1)), _sds((s, d))],
        compiler_params=_cp(1))(x, target)


def _rope(t, cos, sin):
    lane = lax.broadcasted_iota(jnp.int32, t.shape, 1)
    first = (lane >= QK_NOPE) & (lane < QK_NOPE + QK_ROPE // 2)
    second = (lane >= QK_NOPE + QK_ROPE // 2) & (lane < QK_NOPE + QK_ROPE)
    up = pltpu.roll(t, QK_ROPE // 2, 1)
    down = pltpu.roll(t, HEAD_PAD - QK_ROPE // 2, 1)
    return t * cos + jnp.where(first, -down, jnp.where(second, up, 0.0)) * sin


def _rope_transposed(g, cos, sin):
    lane = lax.broadcasted_iota(jnp.int32, g.shape, 1)
    first = (lane >= QK_NOPE) & (lane < QK_NOPE + QK_ROPE // 2)
    second = (lane >= QK_NOPE + QK_ROPE // 2) & (lane < QK_NOPE + QK_ROPE)
    u = g * sin
    up = pltpu.roll(u, QK_ROPE // 2, 1)
    down = pltpu.roll(u, HEAD_PAD - QK_ROPE // 2, 1)
    return g * cos + jnp.where(first, down, jnp.where(second, -up, 0.0))


def _mla_prep_fwd(z, cos, sin, qg, kvg, wq, wq_rot, wuk, wuv, ts):
    s = z.shape[0]
    wide = HEADS * HEAD_PAD

    def body(cq_ref, ckv_ref, kr_ref, cos_ref, sin_ref, qg_ref, kvg_ref, wq_ref, wqr_ref, wuk_ref, wuv_ref,
             q_ref, qt_ref, k_ref, v_ref):
        cos_v, sin_v = cos_ref[...], sin_ref[...]
        cqn = _rms(cq_ref[...], qg_ref[...])
        ckvn = _rms(ckv_ref[...], kvg_ref[...])
        kr = _rope(kr_ref[...], cos_v, sin_v)
        q_lin, q_rot = _dot_nn(cqn, wq_ref[...]), _dot_nn(cqn, wqr_ref[...])
        k_lin, v_all = _dot_nn(ckvn, wuk_ref[...]), _dot_nn(ckvn, wuv_ref[...])
        for h in range(HEADS):
            lanes = slice(h * HEAD_PAD, (h + 1) * HEAD_PAD)
            qh = q_lin[:, lanes] * cos_v + q_rot[:, lanes] * sin_v
            q_ref[h] = qh.astype(q_ref.dtype)
            qt_ref[h, 0] = qh.T.astype(qt_ref.dtype)
            k_ref[h] = (k_lin[:, lanes] + kr).astype(k_ref.dtype)
            v_ref[h] = v_all[:, lanes].astype(v_ref.dtype)

    out = pl.BlockSpec((HEADS, ts, HEAD_PAD), lambda i: (0, i, 0))
    return pl.pallas_call(
        body, name="mla_prep_fwd", grid=(s // ts,),
        in_specs=[_rows(ts, Q_LORA, 8), _rows(ts, KV_LORA, 18), _rows(ts, HEAD_PAD, 19), _rows(ts, HEAD_PAD), _rows(ts, HEAD_PAD),
                  _vec(Q_LORA), _vec(KV_LORA), _vec(wide, 0, Q_LORA), _vec(wide, 0, Q_LORA), _vec(wide, 0, KV_LORA),
                  _vec(wide, 0, KV_LORA)],
        out_specs=[out, pl.BlockSpec((HEADS, 1, HEAD_PAD, ts), lambda i: (0, i, 0, 0)), out, out],
        out_shape=[_sds((HEADS, s, HEAD_PAD), MXU_DTYPE), _sds((HEADS, s // ts, HEAD_PAD, ts), MXU_DTYPE)]
        + [_sds((HEADS, s, HEAD_PAD), MXU_DTYPE)] * 2,
        compiler_params=_cp(1))(z, z, z, cos, sin, qg, kvg, wq, wq_rot, wuk, wuv)


def _mla_prep_bwd(dz, dq, dk, dv, z, cos, sin, qg, kvg, wq, wuk, wuv, ts):
    s = z.shape[0]

    def fq(cq, g):
        return _rms(cq, g)

    def body(dz_in_ref, dq_ref, dk_ref, dv_ref, cq_ref, ckv_ref, cos_ref, sin_ref, qg_ref, kvg_ref, wq_ref, wuk_ref,
             wuv_ref, dz_ref, dw_ref, dqg_ref, dkvg_ref):
        del dz_in_ref
        cos_v, sin_v = cos_ref[...], sin_ref[...]

        @pl.when(pl.program_id(0) == 0)
        def _():
            dw_ref[...] = jnp.zeros_like(dw_ref)
            dqg_ref[...] = jnp.zeros_like(dqg_ref)
            dkvg_ref[...] = jnp.zeros_like(dkvg_ref)

        cqn, vjp_q = jax.vjp(fq, cq_ref[...], qg_ref[...])
        ckvn, vjp_kv = jax.vjp(fq, ckv_ref[...], kvg_ref[...])
        lane = lax.broadcasted_iota(jnp.int32, (ts, HEAD_PAD), 1)
        rope_lanes = (lane >= QK_NOPE) & (lane < QK_NOPE + QK_ROPE)
        dq_lin = jnp.concatenate([_rope_transposed(dq_ref[h], cos_v, sin_v).astype(MXU_DTYPE) for h in range(HEADS)], axis=1)
        dk_all = jnp.concatenate([dk_ref[h].astype(MXU_DTYPE) for h in range(HEADS)], axis=1)
        dv_all = jnp.concatenate([dv_ref[h].astype(MXU_DTYPE) for h in range(HEADS)], axis=1)
        dkr = jnp.where(rope_lanes, dk_ref[0], 0.0)
        for h in range(1, HEADS):
            dkr = dkr + jnp.where(rope_lanes, dk_ref[h], 0.0)
        dcq, dqg = vjp_q(_dot_nt(dq_lin, wq_ref[...]))
        dckv, dkvg = vjp_kv(_dot_nt(dk_all, wuk_ref[...]) + _dot_nt(dv_all, wuv_ref[...]))
        dz_ref[:, 0:Q_LORA] = dcq.astype(dz_ref.dtype)
        dz_ref[:, Q_LORA:Q_LORA + KV_LORA] = dckv.astype(dz_ref.dtype)
        dz_ref[:, Q_LORA + KV_LORA:] = _rope_transposed(dkr, cos_v, sin_v).astype(dz_ref.dtype)
        dqg_ref[...] += dqg
        dkvg_ref[...] += dkvg
        dwq, dwuk, dwuv = _dot_tn(cqn, dq_lin), _dot_tn(ckvn, dk_all), _dot_tn(ckvn, dv_all)
        for h in range(HEADS):
            lanes = slice(h * HEAD_PAD, (h + 1) * HEAD_PAD)
            row0 = (h % 2) * MLA_ROWS
            dw_ref[h // 2, row0:row0 + Q_LORA, :] += dwq[:, lanes]
            dw_ref[h // 2, row0 + Q_LORA:row0 + Q_LORA + KV_LORA, :] += dwuk[:, lanes]
            dw_ref[h // 2, row0 + Q_LORA + KV_LORA:row0 + MLA_ROWS, :] += dwuv[:, lanes]

    wide = HEADS * HEAD_PAD
    heads = pl.BlockSpec((HEADS, ts, HEAD_PAD), lambda i: (0, i, 0))
    whole = pl.BlockSpec((N_CHIPS, 2 * MLA_ROWS, HEAD_PAD), lambda i: (0, 0, 0))
    return pl.pallas_call(
        body, name="mla_prep_bwd", grid=(s // ts,),
        in_specs=[_HBM_SPEC, heads, heads, heads, _rows(ts, Q_LORA, 8), _rows(ts, KV_LORA, 18),
                  _rows(ts, HEAD_PAD), _rows(ts, HEAD_PAD), _vec(Q_LORA), _vec(KV_LORA), _vec(wide, 0, Q_LORA),
                  _vec(wide, 0, KV_LORA), _vec(wide, 0, KV_LORA)],
        out_specs=[_rows(ts, 512, 4), whole, _vec(Q_LORA), _vec(KV_LORA)],
        out_shape=[_sds(dz.shape, dz.dtype), _sds((N_CHIPS, 2 * MLA_ROWS, HEAD_PAD)), _sds((1, Q_LORA)), _sds((1, KV_LORA))],
        input_output_aliases={0: 0}, compiler_params=_cp(1))(dz, dq, dk, dv, z, z, cos, sin, qg, kvg, wq, wuk, wuv)


def _chunk_mask(q0, k0, tq, tk):
    rows = q0 + lax.broadcasted_iota(jnp.int32, (tq, tk), 0)
    cols = k0 + lax.broadcasted_iota(jnp.int32, (tq, tk), 1)
    shift = CHUNK.bit_length() - 1
    return lax.shift_right_logical(cols, shift) <= lax.shift_right_logical(rows, shift)


def _attn_fwd(q, k, v, tq):
    s = q.shape[1]
    nq = s // tq
    scale = 1.0 / float(QK_NOPE + QK_ROPE) ** 0.5

    assert nq % 2 == 0, (s, tq)

    def body(q_ref, k_ref, v_ref, o_ref, lse_ref):
        pair, hh = pl.program_id(1), pl.program_id(2)

        def step(qv, q0, kj, carry, masked):
            m, l, acc = carry
            k0 = pl.multiple_of(kj * tq, tq)
            sc = _dot_nt(qv, k_ref[0, pl.ds(k0, tq), :]) * scale
            if masked:
                sc = jnp.where(_chunk_mask(q0, k0, tq, tq), sc, NEG)
            m_new = jnp.maximum(m, jnp.max(sc, axis=-1, keepdims=True))
            alpha = jnp.exp(m - m_new)
            p = jnp.exp(sc - m_new)
            l = alpha * l + jnp.sum(p, axis=-1, keepdims=True)
            acc = alpha * acc + _dot_nn(p, v_ref[0, pl.ds(k0, tq), :])
            return m_new, l, acc

        for half in range(2):
            rows = slice(half * tq, (half + 1) * tq)
            qv = q_ref[0, rows, :]
            q0 = (2 * pair + half) * tq
            two = lambda i, c: step(qv, q0, 2 * i + 1, step(qv, q0, 2 * i, c, False), False)
            init = (jnp.full((tq, 1), NEG, F32), jnp.zeros((tq, 1), F32), jnp.zeros((tq, HEAD_PAD), F32))
            carry = lax.fori_loop(0, pair, two, init)
            if half == 1:
                carry = step(qv, q0, 2 * pair, carry, False)
            m, l, acc = step(qv, q0, 2 * pair + half, carry, True)
            o = acc / l
            lse_ref[0, rows, :] = m + jnp.log(l)

            @pl.when(hh == 0)
            def _():
                o_ref[rows, :] = o

            @pl.when(hh == 1)
            def _():
                o_ref[rows, :] += o

    head = lambda hp, pair, hh: 2 * hp + hh
    return pl.pallas_call(
        body, name="attn_fwd", grid=(HEADS // 2, nq // 2, 2),
        in_specs=[pl.BlockSpec((1, 2 * tq, HEAD_PAD), lambda hp, pair, hh: (head(hp, pair, hh), pair, 0)),
                  pl.BlockSpec((1, s, HEAD_PAD), lambda hp, pair, hh: (head(hp, pair, hh), 0, 0)),
                  pl.BlockSpec((1, s, HEAD_PAD), lambda hp, pair, hh: (head(hp, pair, hh), 0, 0))],
        out_specs=[pl.BlockSpec((2 * tq, HEAD_PAD), lambda hp, pair, hh: (pair, hp)),
                   pl.BlockSpec((1, 2 * tq, 1), lambda hp, pair, hh: (head(hp, pair, hh), pair, 0))],
        out_shape=[_sds((s, HEADS * V_HEAD)), _sds((HEADS, s, 1))],
        compiler_params=_cp(3))(q, k, v)


def _attn_bwd(q, q_t, k, v, do, do_t, o, lse, tq):
    s = q.shape[1]
    nq = s // tq
    per_q = tq // do_t.shape[3]
    scale = 1.0 / float(QK_NOPE + QK_ROPE) ** 0.5

    def body(q_ref, qt_ref, k_ref, v_ref, do_ref, dot_ref, o_ref, lse_ref, dq_ref, dk_ref, dv_ref, dk_t, dv_t):
        hh, kj = pl.program_id(1), pl.program_id(2)

        @pl.when(kj == 0)
        def _():
            dq_ref[...] = jnp.zeros_like(dq_ref)

        kv, vv = k_ref[0], v_ref[0]
        lane = lax.broadcasted_iota(jnp.int32, (tq, HEAD_PAD), 1)
        mine = lax.shift_right_logical(lane, 6) == hh
        dk_t[...] = jnp.zeros_like(dk_t)
        dv_t[...] = jnp.zeros_like(dv_t)

        def step(qi, masked):
            q0 = pl.multiple_of(qi * tq, tq)
            qv = q_ref[0, pl.ds(q0, tq), :]
            dov = do_ref[pl.ds(q0, tq), :]
            delta = jnp.sum(jnp.where(mine, dov * o_ref[pl.ds(q0, tq), :], 0.0), axis=-1, keepdims=True)
            sc = _dot_nt(qv, kv) * scale
            if masked:
                sc = jnp.where(_chunk_mask(q0, kj * tq, tq, tq), sc, NEG)
            p = jnp.exp(sc - lse_ref[0, pl.ds(q0, tq), :])
            ds = (p * (_dot_nt(dov, vv) - delta) * scale).astype(MXU_DTYPE)
            do_tv = jnp.concatenate([dot_ref[0, qi * per_q + r] for r in range(per_q)], axis=1)
            dv_t[...] += _dot_nn(do_tv, p)
            dk_t[...] += _dot_nn(qt_ref[0, qi], ds)
            dq_ref[0, pl.ds(q0, tq), :] += _dot_nn(ds, kv)

        step(kj, True)
        odd = (nq - 1 - kj) % 2

        @pl.when(odd == 1)
        def _():
            step(kj + 1, False)

        def two(i, c):
            step(kj + 1 + odd + 2 * i, False)
            step(kj + 2 + odd + 2 * i, False)
            return c

        lax.fori_loop(0, (nq - 1 - kj) // 2, two, 0)
        dk_ref[0] = dk_t[...].T
        dv_ref[0] = dv_t[...].T

    head = lambda hp, hh, kj: 2 * hp + hh
    full = pl.BlockSpec((1, s, HEAD_PAD), lambda hp, hh, kj: (head(hp, hh, kj), 0, 0))
    blk = pl.BlockSpec((1, tq, HEAD_PAD), lambda hp, hh, kj: (head(hp, hh, kj), kj, 0))
    pair = pl.BlockSpec((s, HEAD_PAD), lambda hp, hh, kj: (0, hp))
    return pl.pallas_call(
        body, name="attn_bwd", grid=(HEADS // 2, 2, nq),
        in_specs=[full, pl.BlockSpec((1,) + q_t.shape[1:], lambda hp, hh, kj: (head(hp, hh, kj), 0, 0, 0)), blk, blk,
                  pair, pl.BlockSpec((1,) + do_t.shape[1:], lambda hp, hh, kj: (hp, 0, 0, 0)), pair,
                  pl.BlockSpec((1, s, 1), lambda hp, hh, kj: (head(hp, hh, kj), 0, 0))],
        out_specs=[full, blk, blk], out_shape=[_sds((HEADS, s, HEAD_PAD))] * 3,
        scratch_shapes=[pltpu.VMEM((HEAD_PAD, tq), F32), pltpu.VMEM((HEAD_PAD, tq), F32)],
        compiler_params=_cp(3))(q, q_t, k, v, do, do_t, o, lse)


def _sc_conv(u, ubuf, w_ref, b_ref, ts):
    return (w_ref[2:3, :] * u + w_ref[1:2, :] * ubuf[pl.ds(SC_HALO - 1, ts), :]
            + w_ref[0:1, :] * ubuf[pl.ds(SC_HALO - 2, ts), :] + b_ref[...])


def _even_gate_fwd(z, o, sc_w, sc_b, ts):
    s = z.shape[0]
    w = SC_WIDTH

    def body(ab_ref, ac_ref, ax_ref, ag_ref, bg_ref, hc_ref, hx_ref, o_ref, w_ref, b_ref, y_ref, ubuf):
        i = pl.program_id(0)
        u = ac_ref[...] * ax_ref[...]
        ubuf[0:SC_HALO, :] = jnp.where(i > 0, hc_ref[...] * hx_ref[...], 0.0)
        ubuf[SC_HALO:, :] = u
        conv = _sc_conv(u, ubuf, w_ref, b_ref, ts)
        y_ref[:, 0:w] = (ab_ref[...] * conv * _silu(ag_ref[...])).astype(y_ref.dtype)
        y_ref[:, w:] = (o_ref[...] * _silu(bg_ref[...])).astype(y_ref.dtype)

    return pl.pallas_call(
        body, name="even_gate_fwd", grid=(s // ts,),
        in_specs=[_rows(ts, w, 0), _rows(ts, w, 1), _rows(ts, w, 2), _rows(ts, w, 3), _rows(ts, w, 5),
                  _prev_halo(ts, SC_HALO, w, 1), _prev_halo(ts, SC_HALO, w, 2), _rows(ts, w),
                  _vec(w, 0, SC_KERNEL), _vec(w)],
        out_specs=_rows(ts, 2 * w), out_shape=_sds((s, 2 * w), MXU_DTYPE),
        scratch_shapes=[pltpu.VMEM((ts + SC_HALO, w), F32)],
        compiler_params=_cp(1))(z, z, z, z, z, z, z, o, sc_w, sc_b)


def _even_gate_bwd(dy, z, o, sc_w, sc_b, ts):
    s = z.shape[0]
    w = SC_WIDTH
    n = s // ts

    def body(dya_ref, dyb_ref, dyan_ref, ab_ref, ac_ref, ax_ref, ag_ref, bg_ref, hc_ref, hx_ref, abn_ref, agn_ref,
             o_ref, w_ref, b_ref, dz_ref, do_ref, dot_ref, dw_ref, db_ref, ubuf, dbuf):
        i = pl.program_id(0)
        ab, ac, ax, ag, bg = ab_ref[...], ac_ref[...], ax_ref[...], ag_ref[...], bg_ref[...]
        dya, dyb = dya_ref[...], dyb_ref[...]
        u = ac * ax
        ubuf[0:SC_HALO, :] = jnp.where(i > 0, hc_ref[...] * hx_ref[...], 0.0)
        ubuf[SC_HALO:, :] = u
        conv = _sc_conv(u, ubuf, w_ref, b_ref, ts)
        sg = _silu(ag)
        dconv = dya * ab * sg
        dbuf[0:ts, :] = dconv
        dbuf[ts:, :] = jnp.where(i < n - 1, dyan_ref[...] * abn_ref[...] * _silu(agn_ref[...]), 0.0)
        du = w_ref[2:3, :] * dconv + w_ref[1:2, :] * dbuf[pl.ds(1, ts), :] + w_ref[0:1, :] * dbuf[pl.ds(2, ts), :]
        dz_ref[:, 0:w] = (dya * conv * sg).astype(dz_ref.dtype)
        dz_ref[:, w:2 * w] = (du * ax).astype(dz_ref.dtype)
        dz_ref[:, 2 * w:3 * w] = (du * ac).astype(dz_ref.dtype)
        dz_ref[:, 3 * w:4 * w] = (dya * ab * conv * _dsilu(ag)).astype(dz_ref.dtype)
        dz_ref[:, 4 * w:5 * w] = jnp.zeros((ts, w), dz_ref.dtype)
        dz_ref[:, 5 * w:] = (dyb * o_ref[...] * _dsilu(bg)).astype(dz_ref.dtype)
        do = dyb * _silu(bg)
        do_ref[...] = do
        for pair in range(HEADS // 2):
            dot_ref[pair, 0] = do[:, pair * HEAD_PAD:(pair + 1) * HEAD_PAD].T.astype(dot_ref.dtype)

        @pl.when(i == 0)
        def _():
            dw_ref[...] = jnp.zeros_like(dw_ref)
            db_ref[...] = jnp.zeros_like(db_ref)

        dw_ref[0:1, :] += jnp.sum(dconv * ubuf[pl.ds(SC_HALO - 2, ts), :], axis=0, keepdims=True)
        dw_ref[1:2, :] += jnp.sum(dconv * ubuf[pl.ds(SC_HALO - 1, ts), :], axis=0, keepdims=True)
        dw_ref[2:3, :] += jnp.sum(dconv * u, axis=0, keepdims=True)
        db_ref[...] += jnp.sum(dconv, axis=0, keepdims=True)

    return pl.pallas_call(
        body, name="even_gate_bwd", grid=(n,),
        in_specs=[_rows(ts, w, 0), _rows(ts, w, 1), _next_halo(ts, SC_HALO, w, 0, s),
                  _rows(ts, w, 0), _rows(ts, w, 1), _rows(ts, w, 2), _rows(ts, w, 3), _rows(ts, w, 5),
                  _prev_halo(ts, SC_HALO, w, 1), _prev_halo(ts, SC_HALO, w, 2),
                  _next_halo(ts, SC_HALO, w, 0, s), _next_halo(ts, SC_HALO, w, 3, s),
                  _rows(ts, w), _vec(w, 0, SC_KERNEL), _vec(w)],
        out_specs=[_rows(ts, EVEN_PAD), _rows(ts, w), pl.BlockSpec((HEADS // 2, 1, HEAD_PAD, ts), lambda i: (0, i, 0, 0)),
                   _vec(w, 0, SC_KERNEL), _vec(w)],
        out_shape=[_sds((s, EVEN_PAD), MXU_DTYPE), _sds((s, w)), _sds((HEADS // 2, n, HEAD_PAD, ts), MXU_DTYPE),
                   _sds((SC_KERNEL, w)), _sds((1, w))],
        scratch_shapes=[pltpu.VMEM((ts + SC_HALO, w), F32), pltpu.VMEM((ts + SC_HALO, w), F32)],
        compiler_params=_cp(1))(dy, dy, dy, z, z, z, z, z, z, z, z, z, o, sc_w, sc_b)


def _ln_act(uc, sg, g, b):
    mu = jnp.mean(uc, axis=-1, keepdims=True)
    var = jnp.mean(jnp.square(uc - mu), axis=-1, keepdims=True)
    return _silu((uc - mu) * lax.rsqrt(var + EPS) * g + b) * _silu(sg)


def _shifted_copies(buf, shifted, rows):
    for b in range(1, SUBLANES):
        shifted[b - 1, 0:rows, :] = buf[pl.ds(b, rows), :]


def _rows_at(buf, shifted, start, n):
    a, b = divmod(start, SUBLANES)
    return buf[pl.ds(SUBLANES * a, n), :] if b == 0 else shifted[b - 1, pl.ds(SUBLANES * a, n), :]


def _odd_fwd(z, conv_w, conv_b, ln_g, ln_b, ts):
    s = z.shape[0]
    d = D_MODEL
    k = CONF_KERNEL

    def body(val_ref, glu_ref, sg_ref, hval_ref, hglu_ref, w_ref, b_ref, g_ref, beta_ref, y_ref, uc_ref, ubuf, ush):
        i = pl.program_id(0)
        ubuf[0:CONF_HALO, :] = jnp.where(i > 0, hval_ref[...] * _sigmoid(hglu_ref[...]), 0.0)
        ubuf[CONF_HALO:, :] = val_ref[...] * _sigmoid(glu_ref[...])
        _shifted_copies(ubuf, ush, ts + CONF_HALO - SUBLANES)
        for r0 in range(0, ts, CONV_ROWS):
            acc = jnp.broadcast_to(b_ref[...], (CONV_ROWS, d))
            for j in range(k):
                acc = acc + w_ref[j:j + 1, :] * _rows_at(ubuf, ush, r0 + CONF_HALO - (k - 1) + j, CONV_ROWS)
            uc_ref[r0:r0 + CONV_ROWS, :] = acc
        y_ref[...] = _ln_act(uc_ref[...], sg_ref[...], g_ref[...], beta_ref[...]).astype(y_ref.dtype)

    return pl.pallas_call(
        body, name="odd_fwd", grid=(s // ts,),
        in_specs=[_rows(ts, d, 0), _rows(ts, d, 1), _rows(ts, d, 2),
                  _prev_halo(ts, CONF_HALO, d, 0), _prev_halo(ts, CONF_HALO, d, 1),
                  _vec(d, 0, k), _vec(d), _vec(d), _vec(d)],
        out_specs=[_rows(ts, d), _rows(ts, d)], out_shape=[_sds((s, d), MXU_DTYPE), _sds((s, d))],
        scratch_shapes=[pltpu.VMEM((ts + CONF_HALO, d), F32),
                        pltpu.VMEM((SUBLANES - 1, ts + CONF_HALO - SUBLANES, d), F32)],
        compiler_params=_cp(1))(z, z, z, z, z, conv_w, conv_b, ln_g, ln_b)


def _odd_bwd(dy, z, uc, conv_w, ln_g, ln_b, ts):
    s = z.shape[0]
    d = D_MODEL
    k = CONF_KERNEL
    n = s // ts

    def body(dy_ref, dyn_ref, val_ref, glu_ref, sg_ref, sgn_ref, uc_ref, ucn_ref,
             w_ref, g_ref, beta_ref, dz_ref, dw_ref, db_ref, dg_ref, dbeta_ref, dbuf, dsh, dw_acc):
        i = pl.program_id(0)
        val, glu = val_ref[...], glu_ref[...]
        sig = _sigmoid(glu)
        u = val * sig
        _, vjp = jax.vjp(_ln_act, uc_ref[...], sg_ref[...], g_ref[...], beta_ref[...])
        duc, dsg, dg, dbeta = vjp(dy_ref[...])
        _, vjp_n = jax.vjp(_ln_act, ucn_ref[...], sgn_ref[...], g_ref[...], beta_ref[...])
        dbuf[0:ts, :] = duc
        dbuf[ts:, :] = jnp.where(i < n - 1, vjp_n(dyn_ref[...])[0], 0.0)
        dz_ref[:, 2 * d:] = dsg.astype(dz_ref.dtype)
        _shifted_copies(dbuf, dsh, ts + CONF_HALO - SUBLANES)

        @pl.when(i == 0)
        def _():
            dw_acc[...] = jnp.zeros_like(dw_acc)
            db_ref[...] = jnp.zeros_like(db_ref)
            dg_ref[...] = jnp.zeros_like(dg_ref)
            dbeta_ref[...] = jnp.zeros_like(dbeta_ref)

        db_ref[...] += jnp.sum(duc, axis=0, keepdims=True)
        dg_ref[...] += dg
        dbeta_ref[...] += dbeta
        for r0 in range(0, ts, CONV_ROWS):
            acc = jnp.zeros((CONV_ROWS, d), F32)
            for j in range(k):
                acc = acc + w_ref[j:j + 1, :] * _rows_at(dbuf, dsh, r0 + (k - 1) - j, CONV_ROWS)
            sig_r = sig[r0:r0 + CONV_ROWS, :]
            dz_ref[r0:r0 + CONV_ROWS, 0:d] = (acc * sig_r).astype(dz_ref.dtype)
            dz_ref[r0:r0 + CONV_ROWS, d:2 * d] = (acc * val[r0:r0 + CONV_ROWS, :] * sig_r * (1.0 - sig_r)).astype(dz_ref.dtype)
        for j in range(k):
            prod = _rows_at(dbuf, dsh, (k - 1) - j, ts) * u
            dw_acc[j] += jnp.sum(prod.reshape(ts // SUBLANES, SUBLANES, d), axis=0)

        @pl.when(i == n - 1)
        def _():
            dw_ref[...] = jnp.sum(dw_acc[...], axis=1)

    return pl.pallas_call(
        body, name="odd_bwd", grid=(n,),
        in_specs=[_rows(ts, d), _next_halo(ts, CONF_HALO, d, 0, s),
                  _rows(ts, d, 0), _rows(ts, d, 1), _rows(ts, d, 2), _next_halo(ts, CONF_HALO, d, 2, s),
                  _rows(ts, d), _next_halo(ts, CONF_HALO, d, 0, s),
                  _vec(d, 0, k), _vec(d), _vec(d)],
        out_specs=[_rows(ts, ODD_IN), _vec(d, 0, k), _vec(d), _vec(d), _vec(d)],
        out_shape=[_sds((s, ODD_IN), MXU_DTYPE), _sds((k, d)), _sds((1, d)), _sds((1, d)), _sds((1, d))],
        scratch_shapes=[pltpu.VMEM((ts + CONF_HALO, d), F32),
                        pltpu.VMEM((SUBLANES - 1, ts + CONF_HALO - SUBLANES, d), F32), pltpu.VMEM((k, SUBLANES, d), F32)],
        compiler_params=_cp(1))(dy, dy, z, z, z, z, uc, uc, conv_w, ln_g, ln_b)


def _local_step(x, target, cos, sin, mod, p, layer_weights, fwd_dep=None, grads_done=None):
    s = x.shape[0]
    tsf, tsb = min(512, s // 2), min(256, s // 2)
    tq = min(512, s // 2)
    row1 = lambda a, i: a[i:i + 1]
    saved = []
    for layer in range(DEPTH):
        i = layer // 2
        mod_l = row1(mod, layer)
        h = _pre_fwd(x, row1(p["pre_norm_g"], layer), mod_l, tsf, fwd_dep if layer == 0 else None)
        wl = layer_weights(layer, h)
        if layer % 2 == 0:
            z = _mm(h, wl["w_in"], "nn", F32, 512, EVEN_PAD, "even_in_fwd")
            if "late" in wl:
                wl.update(wl.pop("late")(z))
            q, q_t, k, v = _mla_prep_fwd(z, cos, sin, row1(p["even_q_norm_g"], i), row1(p["even_kv_norm_g"], i),
                                    wl["wq"], wl["wq_rot"], wl["wuk"], wl["wuv"], tsf)
            o, lse = _attn_fwd(q, k, v, tq)
            y = _even_gate_fwd(z, o, wl["sc_conv_w"], row1(p["even_sc_conv_b"], i), tsf)
            yo = _mm(y, wl["w_out"], "nn", F32, 512, 1024, "even_out_fwd")
            saved.append((x, h, z, y, yo, wl, (q, q_t, k, v, o, lse)))
        else:
            z = _mm(h, wl["w_in"], "nn", F32, 512, ODD_IN, "odd_in_fwd")
            y, uc = _odd_fwd(z, wl["conv_w"], wl["conv_b"], wl["ln_g"], wl["ln_b"], tsf)
            yo = _mm(y, wl["w_out"], "nn", F32, 512, 1024, "odd_out_fwd")
            saved.append((x, h, z, y, yo, wl, uc))
        x = _post_fwd(x, yo, row1(p["post_norm_g"], layer), mod_l, tsf)

    loss, dx = _loss_fwd_bwd(x, target, tsf)

    g = {n: [None] * (DEPTH if n in ("pre_norm_g", "post_norm_g") else N_PAIRS) for n in (
        "pre_norm_g", "post_norm_g", "even_sc_conv_w", "even_sc_conv_b", "even_q_norm_g", "even_kv_norm_g",
        "odd_conv_w", "odd_conv_b", "odd_ln_g", "odd_ln_b")}
    dmod = [None] * DEPTH
    dep = None
    for layer in reversed(range(DEPTH)):
        i = layer // 2
        mod_l = row1(mod, layer)
        x_in, h, z, y, yo, wl, extra = saved[layer]
        dyo, dgate, g["post_norm_g"][layer] = _post_bwd(dx, yo, row1(p["post_norm_g"], layer), mod_l, tsb, dep)
        bufs = {}
        if layer % 2 == 0:
            q, q_t, k, v, o, lse = extra
            dy = _mm(dyo, wl["w_out"], "nt", F32, 512, 1024, "even_out_bwd_x")
            bufs["even_w_out"] = _mm_tn_shards(y, dyo, "rows", "even_out_bwd_w")
            dz, do, do_t, g["even_sc_conv_w"][i], g["even_sc_conv_b"][i] = _even_gate_bwd(
                dy, z, o, wl["sc_conv_w"], row1(p["even_sc_conv_b"], i), tsb)
            dq, dk, dv = _attn_bwd(q, q_t, k, v, do, do_t, o, lse, tq)
            dz, bufs["even_mla"], g["even_q_norm_g"][i], g["even_kv_norm_g"][i] = _mla_prep_bwd(
                dz, dq, dk, dv, z, cos, sin, row1(p["even_q_norm_g"], i), row1(p["even_kv_norm_g"], i),
                wl["wq"], wl["wuk"], wl["wuv"], tsb)
            bufs["even_w_in"] = _ein_to_shards(_mm(h, dz, "tn", F32, D_MODEL, 512, "even_in_bwd_w"))
        else:
            uc = extra
            dy = _mm(dyo, wl["w_out"], "nt", F32, 512, 1024, "odd_out_bwd_x")
            bufs["odd_w_out"] = _mm_tn_shards(y, dyo, "rows", "odd_out_bwd_w")
            dz, g["odd_conv_w"][i], g["odd_conv_b"][i], g["odd_ln_g"][i], g["odd_ln_b"][i] = _odd_bwd(
                dy, z, uc, wl["conv_w"], wl["ln_g"], wl["ln_b"], tsb)
            bufs["odd_w_in"] = _mm_tn_shards(h, dz, "cols", "odd_in_bwd_w")
        dx, dshift, dscale, g["pre_norm_g"][layer] = _pre_bwd(
            dz, wl["w_in"], dx, x_in, row1(p["pre_norm_g"], layer), mod_l, tsf)
        dmod[layer] = jnp.concatenate([dshift, dscale, dgate], axis=-1)
        dep = grads_done(layer, bufs, dx) if grads_done is not None else None
    stack = lambda parts: jnp.stack([a[0] if a.shape[0] == 1 and a.ndim == 2 else a for a in parts])
    small = {n: stack(parts) for n, parts in g.items()}
    small["dmod"] = jnp.concatenate(dmod, axis=0)
    return loss, dx, small


def _uq_to_heads(w):
    w = w.reshape(N_CHIPS, Q_LORA, 2, QK_NOPE + QK_ROPE).transpose(0, 2, 1, 3).reshape(HEADS, Q_LORA, QK_NOPE + QK_ROPE)
    half = QK_ROPE // 2
    rotated = jnp.concatenate([jnp.zeros_like(w[..., :QK_NOPE]), -w[..., QK_NOPE + half:], w[..., QK_NOPE:QK_NOPE + half]],
                              axis=-1)
    pad = ((0, 0), (0, 0), (0, HEAD_PAD - QK_NOPE - QK_ROPE))
    return _side_by_side(jnp.pad(w, pad)), _side_by_side(jnp.pad(rotated, pad))


def _side_by_side(w):
    return w.transpose(1, 0, 2).reshape(w.shape[1], HEADS * HEAD_PAD)


def _ukv_to_heads(w):
    w = w.reshape(N_CHIPS, KV_LORA, 2, QK_NOPE + V_HEAD).transpose(0, 2, 1, 3).reshape(HEADS, KV_LORA, QK_NOPE + V_HEAD)
    wk = jnp.pad(w[..., :QK_NOPE], ((0, 0), (0, 0), (0, HEAD_PAD - QK_NOPE)))
    wv = w[..., QK_NOPE:]
    zero = jnp.zeros_like(wv)
    odd = (jnp.arange(HEADS) % 2 == 1)[:, None, None]
    wv = jnp.concatenate([jnp.where(odd, zero, wv), jnp.where(odd, wv, zero)], axis=-1)
    return _side_by_side(wk), _side_by_side(wv)


def _mla_local(q):
    blocks = q.reshape(2, MLA_ROWS, HEAD_PAD)
    uq = jnp.concatenate([blocks[r, :Q_LORA, :QK_NOPE + QK_ROPE] for r in range(2)], axis=-1)
    ukv = jnp.concatenate(
        [jnp.concatenate([blocks[r, Q_LORA:Q_LORA + KV_LORA, :QK_NOPE],
                          blocks[r, Q_LORA + KV_LORA:, V_HEAD * r:V_HEAD * (r + 1)]], axis=-1) for r in range(2)], axis=-1)
    return uq, ukv


def _place():
    return lax.axis_index("x"), lax.axis_index("y"), lax.axis_index("c")


def _flip(v, bit):
    return 1 - v if bit else v


def _sem(a, k):
    return a * (N_CHIPS - 1) + k - 1


def _remote(src, dst, send_sem, recv_sem, peer):
    return pltpu.make_async_remote_copy(src_ref=src, dst_ref=dst, send_sem=send_sem, recv_sem=recv_sem,
                                        device_id=peer, device_id_type=MESH)


_VMEM_SPEC = pl.BlockSpec(memory_space=pltpu.VMEM)
_HBM_SPEC = pl.BlockSpec(memory_space=pl.ANY)


def _ada_fwd(c8, ada_w, ada_b_sh):
    depth, d, cols = ada_w.shape

    def body(c_ref, w_ref, b_ref, call_ref, mod_ref, s1, r1, s2, r2):
        x, y, c = _place()
        chip = 2 * x + y
        me = 2 * chip + c
        call_ref[me] = c_ref[...]
        sends = []
        for k in range(1, N_DEV):
            peer = (_flip(x, k & 4), _flip(y, k & 2), _flip(c, k & 1))
            cp = _remote(c_ref, call_ref.at[me], s1.at[k - 1], r1.at[k - 1], peer)
            cp.start()
            sends.append(cp)
        for k in range(1, N_DEV):
            src = 4 * _flip(x, k & 4) + 2 * _flip(y, k & 2) + _flip(c, k & 1)
            _remote(c_ref, call_ref.at[src], s1.at[k - 1], r1.at[k - 1], (x, y, c)).wait_recv()
        act = _silu(jnp.concatenate([call_ref[e, 0:1, :] for e in range(N_DEV)], axis=0))
        for l in range(depth):
            mod_ref[chip, l] = _dot_nn(act, w_ref[l]) + b_ref[l:l + 1, :]
        for k in range(1, N_CHIPS):
            peer = (_flip(x, k & 2), _flip(y, k & 1), c)
            cp = _remote(mod_ref.at[chip], mod_ref.at[chip], s2.at[k - 1], r2.at[k - 1], peer)
            cp.start()
            sends.append(cp)
        for k in range(1, N_CHIPS):
            src = 2 * _flip(x, k & 2) + _flip(y, k & 1)
            _remote(mod_ref.at[src], mod_ref.at[src], s2.at[k - 1], r2.at[k - 1], (x, y, c)).wait_recv()
        for cp in sends:
            cp.wait_send()

    return pl.pallas_call(
        body, name="ada_fwd", in_specs=[_VMEM_SPEC] * 3, out_specs=[_VMEM_SPEC] * 2,
        out_shape=[_sds((N_DEV, 8, d)), _sds((N_CHIPS, depth, N_DEV, cols))],
        scratch_shapes=[pltpu.SemaphoreType.DMA((N_DEV - 1,)), pltpu.SemaphoreType.DMA((N_DEV - 1,)),
                        pltpu.SemaphoreType.DMA((N_CHIPS - 1,)), pltpu.SemaphoreType.DMA((N_CHIPS - 1,))],
        compiler_params=pltpu.CompilerParams(vmem_limit_bytes=VMEM_LIMIT_V7X))(c8, ada_w, ada_b_sh)


def _ada_bwd(c_t, dmod_sh):
    depth, n, cols = dmod_sh.shape
    d = c_t.shape[0]
    tr = 256

    def body(c_ref, dm_ref, o_ref):
        act = _silu(c_ref[...])
        acc = act[:, 0:1] * dm_ref[0, 0:1, :]
        for e in range(1, n):
            acc = acc + act[:, e:e + 1] * dm_ref[0, e:e + 1, :]
        o_ref[0] = acc

    return pl.pallas_call(
        body, name="ada_bwd", grid=(depth, d // tr),
        in_specs=[pl.BlockSpec((tr, n), lambda l, i: (i, 0)), pl.BlockSpec((1, n, cols), lambda l, i: (l, 0, 0))],
        out_specs=pl.BlockSpec((1, tr, cols), lambda l, i: (l, i, 0)), out_shape=_sds((depth, d, cols)),
        compiler_params=_cp(2))(c_t, dmod_sh)


def _gathered_shape(shape, how):
    if how == "slot":
        return (N_CHIPS,) + shape
    r, cc = shape
    return (r, N_CHIPS * cc) if how == "cols" else (N_CHIPS * r, cc)


def _gathered_part(ref, shape, how, chip):
    if how == "slot":
        return ref.at[chip]
    if how == "cols":
        return ref.at[:, pl.ds(pl.multiple_of(chip * shape[1], 128), shape[1])]
    return ref.at[pl.ds(pl.multiple_of(chip * shape[0], 8), shape[0]), :]


_SEM_SPEC = pl.BlockSpec(memory_space=pltpu.SEMAPHORE)
_TOKEN = jax.ShapeDtypeStruct((8, 128), F32)
_SPLIT_COPY = pltpu.CompilerParams(has_side_effects=pltpu.SideEffectType.DATAFLOW_SIDE_EFFECTING)


def _in_hbm(a):
    return pltpu.with_memory_space_constraint(a, pltpu.HBM)


def _gather_start(items, gathered, name, after=()):
    n = len(items)

    def body(*refs):
        ins, outs = refs[:n], refs[n:2 * n]
        send_sems, recv_sems = refs[2 * n + len(after)], refs[2 * n + len(after) + 1]
        x, y, c = _place()
        for a in range(n):
            for k in range(1, N_CHIPS):
                part = _gathered_part(outs[a], items[a][0].shape, items[a][1], 2 * x + y)
                _remote(ins[a], part, send_sems.at[_sem(a, k)], recv_sems.at[_sem(a, k)],
                        (_flip(x, k & 2), _flip(y, k & 1), c)).start()
        refs[-1][...] = jnp.zeros(_TOKEN.shape, _TOKEN.dtype)

    arrays = [_in_hbm(a) for a, _ in items] + [_in_hbm(a) for a in gathered]
    res = pl.pallas_call(
        body, name=name, in_specs=[_HBM_SPEC] * (2 * n + len(after)),
        out_specs=[_SEM_SPEC, _SEM_SPEC] + [_HBM_SPEC] * (2 * n) + [_VMEM_SPEC],
        out_shape=[pltpu.SemaphoreType.DMA((n * (N_CHIPS - 1),)), pltpu.SemaphoreType.DMA((n * (N_CHIPS - 1),))]
        + [pltpu.HBM(a.shape, a.dtype) for a in arrays] + [_TOKEN],
        input_output_aliases={a: 2 + a for a in range(2 * n)}, compiler_params=_SPLIT_COPY)(*arrays, *after)
    return res[0], res[1], res[2:2 + n], res[2 + n:2 + 2 * n], res[-1]


def _gather_wait(items, started, after, name):
    n = len(items)
    send_sems, recv_sems, shards, gathered, _ = started

    def body(*refs):
        ins, outs, send_sems, recv_sems = refs[:n], refs[n:2 * n], refs[2 * n], refs[2 * n + 1]
        x, y, c = _place()
        for a in range(n):
            for k in range(1, N_CHIPS):
                part = _gathered_part(outs[a], items[a][0].shape, items[a][1], 2 * _flip(x, k & 2) + _flip(y, k & 1))
                cp = _remote(ins[a], part, send_sems.at[_sem(a, k)], recv_sems.at[_sem(a, k)], (x, y, c))
                cp.wait_send()
                cp.wait_recv()

    res = pl.pallas_call(
        body, name=name, in_specs=[_HBM_SPEC] * (2 * n) + [_SEM_SPEC, _SEM_SPEC] + [_HBM_SPEC] * len(after),
        out_specs=[_HBM_SPEC] * (2 * n), out_shape=[pltpu.HBM(a.shape, a.dtype) for a in (*shards, *gathered)],
        input_output_aliases={a: a for a in range(2 * n)}, compiler_params=_SPLIT_COPY)(
            *shards, *gathered, send_sems, recv_sems, *after)
    return res[n:]


def _rs_start(bufs, name, after=()):
    n = len(bufs)

    def body(*refs):
        srcs, lands = refs[:n], refs[n:2 * n]
        send_sems, recv_sems = refs[2 * n + len(after)], refs[2 * n + len(after) + 1]
        x, y, c = _place()
        for a in range(n):
            for k in range(1, N_CHIPS):
                tx, ty = _flip(x, k & 2), _flip(y, k & 1)
                _remote(srcs[a].at[2 * tx + ty], lands[a].at[k - 1], send_sems.at[_sem(a, k)], recv_sems.at[_sem(a, k)],
                        (tx, ty, c)).start()
        refs[-1][...] = jnp.zeros(_TOKEN.shape, _TOKEN.dtype)

    arrays = [_in_hbm(b) for b in bufs] + [_in_hbm(lax.empty((N_CHIPS - 1,) + b.shape[1:], b.dtype)) for b in bufs]
    res = pl.pallas_call(
        body, name=name, in_specs=[_HBM_SPEC] * (2 * n + len(after)),
        out_specs=[_SEM_SPEC, _SEM_SPEC] + [_HBM_SPEC] * (2 * n) + [_VMEM_SPEC],
        out_shape=[pltpu.SemaphoreType.DMA((n * (N_CHIPS - 1),)), pltpu.SemaphoreType.DMA((n * (N_CHIPS - 1),))]
        + [pltpu.HBM(a.shape, a.dtype) for a in arrays] + [_TOKEN],
        input_output_aliases={a: 2 + a for a in range(2 * n)}, compiler_params=_SPLIT_COPY)(*arrays, *after)
    return res[0], res[1], res[2:2 + n], res[2 + n:2 + 2 * n], res[-1]


def _rs_wait(started, after, name):
    send_sems, recv_sems, bufs, lands, _ = started
    n = len(bufs)

    def body(*refs):
        srcs, lnds, send_sems, recv_sems = refs[:n], refs[n:2 * n], refs[2 * n], refs[2 * n + 1]
        x, y, c = _place()
        for a in range(n):
            for k in range(1, N_CHIPS):
                cp = _remote(srcs[a].at[0], lnds[a].at[k - 1], send_sems.at[_sem(a, k)], recv_sems.at[_sem(a, k)], (x, y, c))
                cp.wait_send()
                cp.wait_recv()

    res = pl.pallas_call(
        body, name=name, in_specs=[_HBM_SPEC] * (2 * n) + [_SEM_SPEC, _SEM_SPEC] + [_HBM_SPEC] * len(after),
        out_specs=[_HBM_SPEC] * (2 * n), out_shape=[pltpu.HBM(a.shape, a.dtype) for a in (*bufs, *lands)],
        input_output_aliases={a: a for a in range(2 * n)}, compiler_params=_SPLIT_COPY)(
            *bufs, *lands, send_sems, recv_sems, *after)
    return res[:n], res[n:]


def _place_own(shard, how, chip_idx):
    r, cc = shard.shape
    block, index = {"slot": ((1, r, cc), lambda i, c: (c[0], 0, 0)), "cols": ((r, cc), lambda i, c: (0, c[0])),
                    "rows": ((r, cc), lambda i, c: (c[0], 0))}[how]

    def body(c_ref, in_ref, o_ref):
        del c_ref
        o_ref[...] = in_ref[...].reshape(o_ref.shape)

    return pl.pallas_call(
        body, name="place_own", out_shape=_sds(_gathered_shape(shard.shape, how), shard.dtype),
        grid_spec=pltpu.PrefetchScalarGridSpec(
            num_scalar_prefetch=1, grid=(1,), in_specs=[pl.BlockSpec((r, cc), lambda i, c: (0, 0))],
            out_specs=pl.BlockSpec(block, index)),
        compiler_params=_cp(1))(chip_idx, shard)


def _gather_sum_all(small):
    r, w = small.shape

    def body(in_ref, all_ref, sum_ref, send_sems, recv_sems):
        x, y, c = _place()
        me = 4 * x + 2 * y + c
        all_ref[me] = in_ref[...]
        sends = []
        for k in range(1, N_DEV):
            peer = (_flip(x, k & 4), _flip(y, k & 2), _flip(c, k & 1))
            cp = _remote(in_ref, all_ref.at[me], send_sems.at[k - 1], recv_sems.at[k - 1], peer)
            cp.start()
            sends.append(cp)
        for k in range(1, N_DEV):
            src = 4 * _flip(x, k & 4) + 2 * _flip(y, k & 2) + _flip(c, k & 1)
            _remote(in_ref, all_ref.at[src], send_sems.at[k - 1], recv_sems.at[k - 1], (x, y, c)).wait_recv()
        acc = all_ref[0]
        for e in range(1, N_DEV):
            acc = acc + all_ref[e]
        sum_ref[...] = acc
        for cp in sends:
            cp.wait_send()

    return pl.pallas_call(
        body, name="gather_sum_all", in_specs=[_VMEM_SPEC], out_specs=[_VMEM_SPEC] * 2,
        out_shape=[_sds((N_DEV, r, w)), _sds((r, w))],
        scratch_shapes=[pltpu.SemaphoreType.DMA((N_DEV - 1,)), pltpu.SemaphoreType.DMA((N_DEV - 1,))],
        compiler_params=pltpu.CompilerParams(vmem_limit_bytes=VMEM_LIMIT_V7X))(small)


def _add_chips(buf, t, chip_idx):
    r, cc = buf.shape[1:]
    tr = min(256, r)

    def body(c_ref, p_ref, t_ref, o_ref):
        del c_ref
        o_ref[...] = p_ref[0] + t_ref[0].astype(F32) + t_ref[1].astype(F32) + t_ref[2].astype(F32)

    return pl.pallas_call(
        body, name="add_chips", out_shape=_sds((r, cc)),
        grid_spec=pltpu.PrefetchScalarGridSpec(
            num_scalar_prefetch=1, grid=(r // tr,),
            in_specs=[pl.BlockSpec((1, tr, cc), lambda i, c: (c[0], i, 0)),
                      pl.BlockSpec((N_CHIPS - 1, tr, cc), lambda i, c: (0, i, 0))],
            out_specs=pl.BlockSpec((tr, cc), lambda i, c: (i, 0))),
        compiler_params=_cp(1))(chip_idx, buf, t)


def _rs_sibling(qs):
    n = len(qs)

    def body(*refs):
        ins, outs = refs[:n], refs[n:2 * n]
        send_sems, recv_sems = refs[2 * n:]
        x, y, c = _place()
        copies = [_remote(ins[a], outs[a], send_sems.at[a], recv_sems.at[a], (x, y, 1 - c)) for a in range(n)]
        for cp in copies:
            cp.start()
        for cp in copies:
            cp.wait()

    return pl.pallas_call(
        body, name="rs_sibling", in_specs=[_HBM_SPEC] * n, out_specs=[_HBM_SPEC] * n,
        out_shape=[_sds(q.shape) for q in qs],
        scratch_shapes=[pltpu.SemaphoreType.DMA((n,)), pltpu.SemaphoreType.DMA((n,))])(*qs)


def _adamw_update(w, g, m, v):
    m = ADAM_B1 * m + (1.0 - ADAM_B1) * g
    v = ADAM_B2 * v + (1.0 - ADAM_B2) * jnp.square(g)
    m_hat = m / (1.0 - ADAM_B1 ** ADAM_STEP)
    v_hat = v / (1.0 - ADAM_B2 ** ADAM_STEP)
    return -ADAM_LR * (m_hat / (jnp.sqrt(v_hat) + ADAM_EPS) + ADAM_WD * w), m, v


def _adamw(w, g_parts, m, v, name):
    shape = w.shape
    cols = shape[-1]
    rows = _size(shape[:-1])
    tr = 512 if rows % 512 == 0 else rows
    spec = pl.BlockSpec((tr, cols), lambda i: (i, 0))
    n = len(g_parts)
    n_out = 4 if n > 1 else 3

    def body(*refs):
        w_ref, m_ref, v_ref = refs[:3]
        d_ref, nm_ref, nv_ref = refs[-3:]
        g = refs[3][...]
        for r in refs[4:3 + n]:
            g = g + r[...]
        if n > 1:
            refs[3 + n][...] = g
        d_ref[...], nm_ref[...], nv_ref[...] = _adamw_update(w_ref[...], g, m_ref[...], v_ref[...])

    outs = pl.pallas_call(
        body, name="adamw_" + name, grid=(rows // tr,), in_specs=[spec] * (3 + n), out_specs=[spec] * n_out,
        out_shape=[_sds((rows, cols))] * n_out, compiler_params=_cp(1))(
            *[a.reshape(rows, cols) for a in (w, m, v, *g_parts)])
    outs = tuple(o.reshape(shape) for o in outs)
    return outs if n > 1 else (g_parts[0],) + outs


def _adamw_layer(w, g_parts, m, v, layer, prev, name):
    _, r, cc = w.shape
    tr = 512 if r % 512 == 0 else r
    spec = pl.BlockSpec((1, tr, cc), lambda i: (layer, i, 0))
    n = len(g_parts)

    def body(*refs):
        w_ref, m_ref, v_ref = refs[:3]
        g_ref, d_ref, nm_ref, nv_ref = refs[-4:]
        g = refs[3][...]
        for q in refs[4:3 + n]:
            g = g + q[...]
        g = g[:, :cc]
        g_ref[0] = g
        d_ref[0], nm_ref[0], nv_ref[0] = _adamw_update(w_ref[0], g, m_ref[0], v_ref[0])

    g_specs = [pl.BlockSpec((tr, q.shape[1]), lambda i: (i, 0)) for q in g_parts]
    passed = () if prev is None else tuple(prev)
    return pl.pallas_call(
        body, name="adamw_" + name, grid=(r // tr,),
        in_specs=[spec] * 3 + g_specs + [_HBM_SPEC] * len(passed), out_specs=[spec] * 4,
        out_shape=[_sds(w.shape)] * 4, input_output_aliases={3 + n + k: k for k in range(len(passed))},
        compiler_params=_cp(1))(w, m, v, *g_parts, *passed)


def _size(shape):
    n = 1
    for s in shape:
        n *= s
    return n


_SMALL = (("dmod", (DEPTH, 3 * D_MODEL)), ("pre_norm_g", (DEPTH, D_MODEL)), ("post_norm_g", (DEPTH, D_MODEL)),
          ("even_sc_conv_w", (2, SC_KERNEL, SC_WIDTH)), ("even_sc_conv_b", (2, SC_WIDTH)),
          ("even_q_norm_g", (2, Q_LORA)), ("even_kv_norm_g", (2, KV_LORA)),
          ("odd_conv_w", (2, CONF_KERNEL, D_MODEL)), ("odd_conv_b", (2, D_MODEL)), ("odd_ln_g", (2, D_MODEL)),
          ("odd_ln_b", (2, D_MODEL)))
SMALL_ROWS = -(-sum(_size(s) for _, s in _SMALL) // (8 * 128)) * 8

_SMALL_W = (("even_sc_conv_w", (2, SC_KERNEL, SC_WIDTH // N_CHIPS)), ("odd_conv_w", (2, CONF_KERNEL, D_MODEL // N_CHIPS)),
            ("odd_conv_b", (2, D_MODEL // N_CHIPS)), ("odd_ln_g", (2, D_MODEL // N_CHIPS)),
            ("odd_ln_b", (2, D_MODEL // N_CHIPS)))
SMALL_W_ROWS = -(-sum(_size(s) for _, s in _SMALL_W) // (8 * 128)) * 8


def _pack_rows(arrays, layout, rows):
    flat = jnp.concatenate([arrays[n].reshape(-1) for n, _ in layout])
    return jnp.pad(flat, (0, rows * 128 - flat.shape[0])).reshape(rows, 128)


def _unpack_small(t):
    flat = t.reshape(-1)
    out, at = {}, 0
    for n, shape in _SMALL:
        out[n] = flat[at:at + _size(shape)].reshape(shape)
        at += _size(shape)
    return out


def _unpack_small_w(t):
    flat = t.reshape(N_CHIPS, -1)
    out, at = {}, 0
    for n, shape in _SMALL_W:
        a = flat[:, at:at + _size(shape)].reshape((N_CHIPS,) + shape)
        out[n] = jnp.moveaxis(a, 0, -2).reshape(shape[:-1] + (N_CHIPS * shape[-1],))
        at += _size(shape)
    return out


def _chip_cols(a, chip):
    n = a.shape[-1] // N_CHIPS
    return lax.dynamic_slice_in_dim(a, chip * n, n, axis=a.ndim - 1)


WEIGHT_NAMES = ("ada_w", "ada_b", "pre_norm_g", "post_norm_g", "even_w_in", "even_sc_conv_w", "even_sc_conv_b",
                "even_q_norm_g", "even_kv_norm_g", "even_w_uq", "even_w_ukv", "even_w_out", "odd_w_in", "odd_conv_w",
                "odd_conv_b", "odd_ln_g", "odd_ln_b", "odd_w_out")
GATHER_HOW = ((("even_w_in", "slot"), ("even_w_uq", "slot"), ("even_w_ukv", "slot"), ("even_w_out", "rows")),
              (("odd_w_in", "cols"), ("odd_w_out", "rows")))


def kernel(x, c, positions, ada_w, ada_b, pre_norm_g, post_norm_g, even_w_in, even_sc_conv_w, even_sc_conv_b, even_q_norm_g, even_kv_norm_g, even_w_uq, even_w_ukv, even_w_out, odd_w_in, odd_conv_w, odd_conv_b, odd_ln_g, odd_ln_b, odd_w_out, loss_target, m_ada_w, m_ada_b, m_pre_norm_g, m_post_norm_g, m_even_w_in, m_even_sc_conv_w, m_even_sc_conv_b, m_even_q_norm_g, m_even_kv_norm_g, m_even_w_uq, m_even_w_ukv, m_even_w_out, m_odd_w_in, m_odd_conv_w, m_odd_conv_b, m_odd_ln_g, m_odd_ln_b, m_odd_w_out, v_ada_w, v_ada_b, v_pre_norm_g, v_post_norm_g, v_even_w_in, v_even_sc_conv_w, v_even_sc_conv_b, v_even_q_norm_g, v_even_kv_norm_g, v_even_w_uq, v_even_w_ukv, v_even_w_out, v_odd_w_in, v_odd_conv_w, v_odd_conv_b, v_odd_ln_g, v_odd_ln_b, v_odd_w_out):
    w = dict(zip(WEIGHT_NAMES, (ada_w, ada_b, pre_norm_g, post_norm_g, even_w_in, even_sc_conv_w, even_sc_conv_b,
                                even_q_norm_g, even_kv_norm_g, even_w_uq, even_w_ukv, even_w_out, odd_w_in, odd_conv_w,
                                odd_conv_b, odd_ln_g, odd_ln_b, odd_w_out)))
    m = dict(zip(WEIGHT_NAMES, (m_ada_w, m_ada_b, m_pre_norm_g, m_post_norm_g, m_even_w_in, m_even_sc_conv_w,
                                m_even_sc_conv_b, m_even_q_norm_g, m_even_kv_norm_g, m_even_w_uq, m_even_w_ukv,
                                m_even_w_out, m_odd_w_in, m_odd_conv_w, m_odd_conv_b, m_odd_ln_g, m_odd_ln_b, m_odd_w_out)))
    v = dict(zip(WEIGHT_NAMES, (v_ada_w, v_ada_b, v_pre_norm_g, v_post_norm_g, v_even_w_in, v_even_sc_conv_w,
                                v_even_sc_conv_b, v_even_q_norm_g, v_even_kv_norm_g, v_even_w_uq, v_even_w_ukv,
                                v_even_w_out, v_odd_w_in, v_odd_conv_w, v_odd_conv_b, v_odd_ln_g, v_odd_ln_b, v_odd_w_out)))
    ix, iy, ic = _place()
    chip = 2 * ix + iy
    me = 2 * chip + ic
    s = x.shape[1]

    c_all, mod_all = _ada_fwd(jnp.broadcast_to(c, (8, D_MODEL)), ada_w, _chip_cols(ada_b, chip))
    mod = lax.dynamic_index_in_dim(mod_all, me, axis=2, keepdims=False)
    mod = mod.transpose(1, 0, 2).reshape(DEPTH, 3 * D_MODEL)

    items = [[(w[n][layer // 2].astype(MXU_DTYPE), how) for n, how in GATHER_HOW[layer % 2]] for layer in range(DEPTH)]
    groups = [items[0][:1], items[0][1:] + [(_pack_rows(w, _SMALL_W, SMALL_W_ROWS), "slot")],
              [item for layer_items in items[1:] for item in layer_items]]
    sent, dep = [], mod_all
    for number, group in enumerate(groups):
        sent.append(_gather_start(group, [_place_own(a, how, chip.reshape(1)) for a, how in group],
                                  "gather_start_%d" % number, [dep]))
        dep = sent[-1][-1]
    arrived = {}

    def group(number, after):
        if number not in arrived:
            arrived[number] = _gather_wait(groups[number], sent[number], after, "gather_wait_%d" % number)
        return arrived[number]

    def even_rest(i, uq, ukv, eout, small_w):
        wuk, wuv = _ukv_to_heads(ukv)
        wq, wq_rot = _uq_to_heads(uq)
        return {"wq": wq, "wq_rot": wq_rot, "wuk": wuk, "wuv": wuv, "w_out": eout, "sc_conv_w": small_w["even_sc_conv_w"][i]}

    def layer_weights(layer, h):
        i = layer // 2
        if layer == 0:
            def late(z):
                uq, ukv, eout, small = group(1, [z])
                return even_rest(i, uq, ukv, eout, _unpack_small_w(small))
            return {"w_in": _ein_from_shards(group(0, [h])[0]), "late": late}
        small_w = _unpack_small_w(group(1, [h])[-1])
        at = sum(len(layer_items) for layer_items in items[1:layer])
        arrays = group(2, [h])[at:at + len(items[layer])]
        if layer % 2 == 0:
            return {"w_in": _ein_from_shards(arrays[0]), **even_rest(i, *arrays[1:], small_w)}
        oin, oout = arrays
        return {"w_in": oin, "w_out": oout, "conv_w": small_w["odd_conv_w"][i], "conv_b": small_w["odd_conv_b"][i:i + 1],
                "ln_g": small_w["odd_ln_g"][i:i + 1], "ln_b": small_w["odd_ln_b"][i:i + 1]}

    in_flight, own, sib, last = {}, {}, {}, {}

    def land(layer, after):
        names, started, kept = in_flight.pop(layer)
        bufs, arrived = _rs_wait(started, after, "rs_wait_%d" % layer)
        sums = [_add_chips(b, t, chip.reshape(1)) for b, t in zip(bufs if kept is None else kept, arrived)]
        for n, mine, theirs in zip(names, sums, _rs_sibling(sums)):
            own[n, layer // 2], sib[n, layer // 2] = mine, theirs

    def grads_done(layer, bufs, dx_in):
        if layer + 1 in in_flight:
            land(layer + 1, [dx_in])
        if layer == 0:
            last.update(bufs)
            return None
        names = sorted(bufs)
        in_flight[layer] = (names, _rs_start([bufs[n] for n in names], "rs_start_%d" % layer), None)
        return in_flight[layer][1][-1]

    p = {"pre_norm_g": pre_norm_g, "post_norm_g": post_norm_g, "even_sc_conv_b": even_sc_conv_b,
         "even_q_norm_g": even_q_norm_g, "even_kv_norm_g": even_kv_norm_g}
    inv_freq = 1.0 / (ROPE_THETA ** (jnp.arange(0, QK_ROPE, 2, dtype=F32) / QK_ROPE))
    inv_freq = jnp.zeros((1, HEAD_PAD), F32).at[0, QK_NOPE:QK_NOPE + QK_ROPE].set(jnp.tile(inv_freq, 2))
    cos, sin = _rope_tables(positions.reshape(s, 1), inv_freq)

    loss, dx, g = _local_step(x[0], loss_target[0], cos, sin, mod, p, layer_weights, dep, grads_done)

    grads, deltas, new_m, new_v = {}, {}, {}, {}

    def update_layers(n, results, pairs):
        for i in pairs:
            results = _adamw_layer(w[n], [own[n, i], sib[n, i]], m[n], v[n], i, results, n)
        return results

    small_all, small_sum = _gather_sum_all(_pack_rows(g, _SMALL, SMALL_ROWS))
    names = sorted(last)
    kept = [last[n] for n in names]
    in_flight[0] = (names, _rs_start([b.astype(jnp.bfloat16) for b in kept], "rs_start_0", [small_sum]), kept)
    tot = _unpack_small(small_sum)
    dmod_all = small_all[:, :DEPTH * 3 * D_MODEL // 128].reshape(N_DEV, DEPTH, 3 * D_MODEL)
    grads["ada_w"] = _ada_bwd(c_all[:, 0, :].T, _chip_cols(dmod_all, chip).transpose(1, 0, 2))
    grads["ada_b"] = tot["dmod"]
    for n in ("pre_norm_g", "post_norm_g", "even_sc_conv_b", "even_q_norm_g", "even_kv_norm_g"):
        grads[n] = tot[n]
    for n in ("even_sc_conv_w", "odd_conv_w", "odd_conv_b", "odd_ln_g", "odd_ln_b"):
        grads[n] = _chip_cols(tot[n], chip)
    for n in list(grads):
        _, deltas[n], new_m[n], new_v[n] = _adamw(w[n], [grads[n]], m[n], v[n], n)

    for n in ("odd_w_in", "odd_w_out"):
        grads[n], deltas[n], new_m[n], new_v[n] = update_layers(n, None, (1, 0))
    partly = {n: update_layers(n, None, (1,)) for n in ("even_w_in", "even_w_out")}
    land(0, [deltas["ada_w"], deltas["odd_w_in"], partly["even_w_in"][1]])
    for n in ("even_w_in", "even_w_out"):
        grads[n], deltas[n], new_m[n], new_v[n] = update_layers(n, partly[n], (0,))
    uq_parts, ukv_parts = zip(*[[jnp.stack(part) for part in zip(*[_mla_local(q["even_mla", i]) for i in range(N_PAIRS)])]
                                for q in (own, sib)])
    for n, parts in (("even_w_uq", uq_parts), ("even_w_ukv", ukv_parts)):
        grads[n], deltas[n], new_m[n], new_v[n] = _adamw(w[n], list(parts), m[n], v[n], n)

    total_loss = lax.psum(loss[0, 0], ("x", "y", "c"))
    return (total_loss, dx[None], *[grads[n] for n in WEIGHT_NAMES], *[deltas[n] for n in WEIGHT_NAMES],
            *[new_m[n] for n in WEIGHT_NAMES], *[new_v[n] for n in WEIGHT_NAMES])
```

```python
import jax
import jax.numpy as jnp
from jax import lax
from jax.experimental import pallas as pl
from jax.experimental.pallas import tpu as pltpu

F32 = jnp.float32
MXU_DTYPE = jnp.bfloat16
MESH = pl.DeviceIdType.MESH
VMEM_LIMIT_V7X = 56 * 2 ** 20

EPS = 1e-6
D_MODEL = 1024
DEPTH = 4
CHUNK = 64
SC_WIDTH = 512
SC_KERNEL = 3
SC_HALO = 8
HEADS = 8
QK_NOPE = 64
QK_ROPE = 32
V_HEAD = 64
HEAD_PAD = 128
Q_LORA = 256
KV_LORA = 128
ROPE_THETA = 10000.0
CONF_KERNEL = 31
CONF_HALO = 32
CONV_ROWS = 32
SUBLANES = 8
EVEN_IN = 2976
EVEN_PAD = 3072
ODD_IN = 3072
N_CHIPS = 4
N_DEV = 8
NEG = -1e30

ADAM_LR = 0.001
ADAM_B1 = 0.9
ADAM_B2 = 0.999
ADAM_EPS = 1e-08
ADAM_WD = 0.01
ADAM_STEP = 10

N_PAIRS = DEPTH // 2
EVEN_SHARD = EVEN_IN // N_CHIPS
EVEN_SHARD_PAD = 768
MLA_ROWS = Q_LORA + 2 * KV_LORA


def _cp(n_grid=0, **kw):
    return pltpu.CompilerParams(dimension_semantics=("arbitrary",) * n_grid,
                                vmem_limit_bytes=VMEM_LIMIT_V7X, **kw)


def _sigmoid(x):
    return 1.0 / (1.0 + jnp.exp(-x))


def _silu(x):
    return x * _sigmoid(x)


def _dsilu(x):
    s = _sigmoid(x)
    return s * (1.0 + x * (1.0 - s))


def _rms(x, g):
    return x * lax.rsqrt(jnp.mean(x * x, axis=-1, keepdims=True) + EPS) * g


def _dot(a, b, dims):
    return lax.dot_general(a.astype(MXU_DTYPE), b.astype(MXU_DTYPE), (dims, ((), ())),
                           preferred_element_type=F32)


def _dot_nn(a, b):
    return _dot(a, b, ((1,), (0,)))


def _dot_nt(a, b):
    return _dot(a, b, ((1,), (1,)))


def _dot_tn(a, b):
    return _dot(a, b, ((0,), (0,)))


def _rows(ts, w, cb=0):
    return pl.BlockSpec((ts, w), lambda i: (i, cb))


def _vec(w, cb=0, r=1):
    return pl.BlockSpec((r, w), lambda i: (0, cb))


def _prev_halo(ts, hr, w, cb):
    return pl.BlockSpec((hr, w), lambda i: (jnp.maximum(i * (ts // hr) - 1, 0), cb))


def _next_halo(ts, hr, w, cb, s):
    return pl.BlockSpec((hr, w), lambda i: (jnp.minimum((i + 1) * (ts // hr), s // hr - 1), cb))


def _sds(shape, dtype=F32):
    return jax.ShapeDtypeStruct(shape, dtype)


def _mm(a, b, mode, out_dtype, tm, tn, name):
    tm = min(tm, a.shape[1] if mode == "tn" else a.shape[0])
    tn = min(tn, b.shape[0] if mode == "nt" else b.shape[1])
    if mode == "nn":
        (m, k), n = a.shape, b.shape[1]
        a_spec = pl.BlockSpec((tm, k), lambda i, j: (i, 0))
        b_spec = pl.BlockSpec((k, tn), lambda i, j: (0, j))
        dot = _dot_nn
    elif mode == "nt":
        (m, k), n = a.shape, b.shape[0]
        a_spec = pl.BlockSpec((tm, k), lambda i, j: (i, 0))
        b_spec = pl.BlockSpec((tn, k), lambda i, j: (j, 0))
        dot = _dot_nt
    else:
        (k, m), n = a.shape, b.shape[1]
        a_spec = pl.BlockSpec((k, tm), lambda i, j: (0, i))
        b_spec = pl.BlockSpec((k, tn), lambda i, j: (0, j))
        dot = _dot_tn
    assert m % tm == 0 and n % tn == 0, (name, m, n, tm, tn)

    def body(a_ref, b_ref, o_ref):
        o_ref[...] = dot(a_ref[...], b_ref[...]).astype(o_ref.dtype)

    return pl.pallas_call(
        body, name=name, grid=(m // tm, n // tn), in_specs=[a_spec, b_spec],
        out_specs=pl.BlockSpec((tm, tn), lambda i, j: (i, j)), out_shape=_sds((m, n), out_dtype),
        compiler_params=_cp(2))(a, b)


def _mm_tn_shards(a, b, by, name):
    k, m = a.shape
    n = b.shape[1]
    if by == "cols":
        tm, tn = m, n // N_CHIPS
        shape, grid = (N_CHIPS, m, tn), (1, N_CHIPS)
        out_spec = pl.BlockSpec((1, tm, tn), lambda i, j: (j, i, 0))
    else:
        tm, tn = m // N_CHIPS, n
        shape, grid = (N_CHIPS, tm, n), (N_CHIPS, 1)
        out_spec = pl.BlockSpec((1, tm, tn), lambda i, j: (i, 0, j))

    def body(a_ref, b_ref, o_ref):
        o_ref[0] = _dot_tn(a_ref[...], b_ref[...])

    return pl.pallas_call(
        body, name=name, grid=grid,
        in_specs=[pl.BlockSpec((k, tm), lambda i, j: (0, i)), pl.BlockSpec((k, tn), lambda i, j: (0, j))],
        out_specs=out_spec, out_shape=_sds(shape), compiler_params=_cp(2))(a, b)


def _even_col(q):
    return q if q < 2432 else (q + 64 if q < 2464 else q + 96)


def _shard_pieces(j):
    lo, hi = EVEN_SHARD * j, EVEN_SHARD * (j + 1)
    cuts = [lo] + [b for b in (2432, 2464) if lo < b < hi] + [hi]
    return [(a - lo, _even_col(a), b - a) for a, b in zip(cuts[:-1], cuts[1:])]


def _ein_from_shards(w):
    _, d, _ = w.shape
    tr = 256

    def body(w_ref, o_ref):
        parts, at = [], 0
        for j in range(N_CHIPS):
            for d0, s0, n in _shard_pieces(j):
                if s0 > at:
                    parts.append(jnp.zeros((tr, s0 - at), F32))
                parts.append(w_ref[j, :, d0:d0 + n].astype(F32))
                at = s0 + n
        o_ref[...] = jnp.concatenate(parts, axis=1).astype(o_ref.dtype)

    return pl.pallas_call(
        body, name="ein_from_shards", grid=(d // tr,),
        in_specs=[pl.BlockSpec((N_CHIPS, tr, EVEN_SHARD), lambda i: (0, i, 0))],
        out_specs=_rows(tr, EVEN_PAD), out_shape=_sds((d, EVEN_PAD), w.dtype), compiler_params=_cp(1))(w)


def _ein_to_shards(dw):
    d = dw.shape[0]
    tr = 256

    def body(dw_ref, o_ref):
        for j in range(N_CHIPS):
            parts = [dw_ref[:, s0:s0 + n] for _, s0, n in _shard_pieces(j)]
            o_ref[j] = jnp.concatenate(parts + [jnp.zeros((tr, EVEN_SHARD_PAD - EVEN_SHARD), F32)], axis=1)

    return pl.pallas_call(
        body, name="ein_to_shards", grid=(d // tr,), in_specs=[_rows(tr, EVEN_PAD)],
        out_specs=pl.BlockSpec((N_CHIPS, tr, EVEN_SHARD_PAD), lambda i: (0, i, 0)),
        out_shape=_sds((N_CHIPS, d, EVEN_SHARD_PAD)), compiler_params=_cp(1))(dw)


def _rope_tables(pos_col, invf):
    s = pos_col.shape[0]
    ts = min(512, s)

    def body(p_ref, f_ref, c_ref, s_ref):
        ang = p_ref[...].astype(F32) * f_ref[...]
        lane = lax.broadcasted_iota(jnp.int32, ang.shape, 1)
        rope = (lane >= QK_NOPE) & (lane < QK_NOPE + QK_ROPE)
        c_ref[...] = jnp.where(lane < QK_NOPE, 1.0, jnp.where(rope, jnp.cos(ang), 0.0))
        s_ref[...] = jnp.where(rope, jnp.sin(ang), 0.0)

    return pl.pallas_call(
        body, name="rope_tables", grid=(s // ts,), in_specs=[_rows(ts, 1), _vec(HEAD_PAD)],
        out_specs=[_rows(ts, HEAD_PAD)] * 2, out_shape=[_sds((s, HEAD_PAD))] * 2,
        compiler_params=_cp(1))(pos_col, invf)


def _after(dep):
    return () if dep is None else (dep,)


def _pre_fwd(x, g, mod_l, ts, dep=None):
    s, d = x.shape

    def body(x_ref, g_ref, sh_ref, sc_ref, *rest):
        h = _rms(x_ref[...], g_ref[...]) * (1.0 + sc_ref[...]) + sh_ref[...]
        rest[-1][...] = h.astype(rest[-1].dtype)

    return pl.pallas_call(
        body, name="pre_fwd", grid=(s // ts,),
        in_specs=[_rows(ts, d), _vec(d), _vec(d, 0), _vec(d, 1)] + [_HBM_SPEC] * len(_after(dep)),
        out_specs=_rows(ts, d), out_shape=_sds((s, d), MXU_DTYPE), compiler_params=_cp(1))(
            x, g, mod_l, mod_l, *_after(dep))


def _pre_bwd(dz, w_in, dx_out, x, g, mod_l, ts):
    s, d = x.shape
    n_in = dz.shape[1]

    def f(xv, gv, sh, sc):
        return _rms(xv, gv) * (1.0 + sc) + sh

    def body(dz_ref, w_ref, dxo_ref, x_ref, g_ref, sh_ref, sc_ref, dx_ref, dsh_ref, dsc_ref, dg_ref):
        @pl.when(pl.program_id(0) == 0)
        def _():
            dsh_ref[...] = jnp.zeros_like(dsh_ref)
            dsc_ref[...] = jnp.zeros_like(dsc_ref)
            dg_ref[...] = jnp.zeros_like(dg_ref)

        _, vjp = jax.vjp(f, x_ref[...], g_ref[...], sh_ref[...], sc_ref[...])
        dx, dg, dsh, dsc = vjp(_dot_nt(dz_ref[...], w_ref[...]))
        dx_ref[...] = dxo_ref[...] + dx
        dsh_ref[...] += dsh
        dsc_ref[...] += dsc
        dg_ref[...] += dg

    return pl.pallas_call(
        body, name="pre_bwd", grid=(s // ts,),
        in_specs=[_rows(ts, n_in), _vec(n_in, 0, d), _rows(ts, d), _rows(ts, d), _vec(d), _vec(d, 0), _vec(d, 1)],
        out_specs=[_rows(ts, d), _vec(d), _vec(d), _vec(d)],
        out_shape=[_sds((s, d)), _sds((1, d)), _sds((1, d)), _sds((1, d))],
        compiler_params=_cp(1))(dz, w_in, dx_out, x, g, mod_l, mod_l)


def _post_pre_fwd(x, yo, g_post, mod_l, g_pre, mod_next, ts):
    s, d = x.shape

    def body(x_ref, yo_ref, gp_ref, gate_ref, g_ref, sh_ref, sc_ref, x_out_ref, h_ref):
        x_new = x_ref[...] + gate_ref[...] * _rms(yo_ref[...], gp_ref[...])
        x_out_ref[...] = x_new
        h_ref[...] = (_rms(x_new, g_ref[...]) * (1.0 + sc_ref[...]) + sh_ref[...]).astype(h_ref.dtype)

    return pl.pallas_call(
        body, name="post_pre_fwd", grid=(s // ts,),
        in_specs=[_rows(ts, d), _rows(ts, d), _vec(d), _vec(d, 2), _vec(d), _vec(d, 0), _vec(d, 1)],
        out_specs=[_rows(ts, d), _rows(ts, d)], out_shape=[_sds((s, d)), _sds((s, d), MXU_DTYPE)],
        compiler_params=_cp(1))(x, yo, g_post, mod_l, g_pre, mod_next, mod_next)


def _post_loss(x, yo, g_post, mod_l, target, ts):
    s, d = x.shape

    def body(x_ref, yo_ref, gp_ref, gate_ref, t_ref, loss_ref, dx_ref):
        err = x_ref[...] + gate_ref[...] * _rms(yo_ref[...], gp_ref[...]) - t_ref[...]
        dx_ref[...] = err * (1.0 / d)

        @pl.when(pl.program_id(0) == 0)
        def _():
            loss_ref[...] = jnp.zeros_like(loss_ref)

        loss_ref[...] += 0.5 * jnp.sum(jnp.sum(err * err, axis=-1, keepdims=True) * (1.0 / d), axis=0, keepdims=True)

    return pl.pallas_call(
        body, name="post_loss", grid=(s // ts,),
        in_specs=[_rows(ts, d), _rows(ts, d), _vec(d), _vec(d, 2), _rows(ts, d)],
        out_specs=[_vec(1), _rows(ts, d)], out_shape=[_sds((1, 1)), _sds((s, d))],
        compiler_params=_cp(1))(x, yo, g_post, mod_l, target)


def _post_bwd(dx_out, yo, g, mod_l, ts, dep=None):
    s, d = yo.shape

    def f(yov, gv, gate):
        return gate * _rms(yov, gv)

    def body(dx_ref, yo_ref, g_ref, gate_ref, *rest):
        dyo_ref, dgate_ref, dg_ref = rest[-3:]
        i = pl.program_id(0)
        _, vjp = jax.vjp(f, yo_ref[...], g_ref[...], gate_ref[...])
        dyo, dg, dgate = vjp(dx_ref[...])
        dyo_ref[...] = dyo.astype(dyo_ref.dtype)

        @pl.when(i == 0)
        def _():
            dgate_ref[...] = jnp.zeros_like(dgate_ref)
            dg_ref[...] = jnp.zeros_like(dg_ref)

        dgate_ref[...] += dgate
        dg_ref[...] += dg

    return pl.pallas_call(
        body, name="post_bwd", grid=(s // ts,),
        in_specs=[_rows(ts, d), _rows(ts, d), _vec(d), _vec(d, 2)] + [_HBM_SPEC] * len(_after(dep)),
        out_specs=[_rows(ts, d), _vec(d), _vec(d)],
        out_shape=[_sds((s, d), MXU_DTYPE), _sds((1, d)), _sds((1, d))],
        compiler_params=_cp(1))(dx_out, yo, g, mod_l, *_after(dep))


def _rope(t, cos, sin):
    lane = lax.broadcasted_iota(jnp.int32, t.shape, 1)
    first = (lane >= QK_NOPE) & (lane < QK_NOPE + QK_ROPE // 2)
    second = (lane >= QK_NOPE + QK_ROPE // 2) & (lane < QK_NOPE + QK_ROPE)
    up = pltpu.roll(t, QK_ROPE // 2, 1)
    down = pltpu.roll(t, HEAD_PAD - QK_ROPE // 2, 1)
    return t * cos + jnp.where(first, -down, jnp.where(second, up, 0.0)) * sin


def _rope_transposed(g, cos, sin):
    lane = lax.broadcasted_iota(jnp.int32, g.shape, 1)
    first = (lane >= QK_NOPE) & (lane < QK_NOPE + QK_ROPE // 2)
    second = (lane >= QK_NOPE + QK_ROPE // 2) & (lane < QK_NOPE + QK_ROPE)
    u = g * sin
    up = pltpu.roll(u, QK_ROPE // 2, 1)
    down = pltpu.roll(u, HEAD_PAD - QK_ROPE // 2, 1)
    return g * cos + jnp.where(first, down, jnp.where(second, -up, 0.0))


def _mla_prep_fwd(z, cos, sin, qg, kvg, wq, wq_rot, wuk, wuv, ts):
    s = z.shape[0]
    wide = HEADS * HEAD_PAD

    def body(cq_ref, ckv_ref, kr_ref, cos_ref, sin_ref, qg_ref, kvg_ref, wq_ref, wqr_ref, wuk_ref, wuv_ref,
             q_ref, qt_ref, k_ref, v_ref):
        cos_v, sin_v = cos_ref[...], sin_ref[...]
        cqn = _rms(cq_ref[...], qg_ref[...])
        ckvn = _rms(ckv_ref[...], kvg_ref[...])
        kr = _rope(kr_ref[...], cos_v, sin_v)
        q_lin, q_rot = _dot_nn(cqn, wq_ref[...]), _dot_nn(cqn, wqr_ref[...])
        k_lin, v_all = _dot_nn(ckvn, wuk_ref[...]), _dot_nn(ckvn, wuv_ref[...])
        for h in range(HEADS):
            lanes = slice(h * HEAD_PAD, (h + 1) * HEAD_PAD)
            qh = q_lin[:, lanes] * cos_v + q_rot[:, lanes] * sin_v
            q_ref[h] = qh.astype(q_ref.dtype)
            qt_ref[h, 0] = qh.T.astype(qt_ref.dtype)
            k_ref[h] = (k_lin[:, lanes] + kr).astype(k_ref.dtype)
            v_ref[h] = v_all[:, lanes].astype(v_ref.dtype)

    out = pl.BlockSpec((HEADS, ts, HEAD_PAD), lambda i: (0, i, 0))
    return pl.pallas_call(
        body, name="mla_prep_fwd", grid=(s // ts,),
        in_specs=[_rows(ts, Q_LORA, 8), _rows(ts, KV_LORA, 18), _rows(ts, HEAD_PAD, 19), _rows(ts, HEAD_PAD), _rows(ts, HEAD_PAD),
                  _vec(Q_LORA), _vec(KV_LORA), _vec(wide, 0, Q_LORA), _vec(wide, 0, Q_LORA), _vec(wide, 0, KV_LORA),
                  _vec(wide, 0, KV_LORA)],
        out_specs=[out, pl.BlockSpec((HEADS, 1, HEAD_PAD, ts), lambda i: (0, i, 0, 0)), out, out],
        out_shape=[_sds((HEADS, s, HEAD_PAD), MXU_DTYPE), _sds((HEADS, s // ts, HEAD_PAD, ts), MXU_DTYPE)]
        + [_sds((HEADS, s, HEAD_PAD), MXU_DTYPE)] * 2,
        compiler_params=_cp(1))(z, z, z, cos, sin, qg, kvg, wq, wq_rot, wuk, wuv)


def _mla_prep_bwd(dz, dq, dk, dv, z, cos, sin, qg, kvg, wq, wuk, wuv, ts):
    s = z.shape[0]

    def fq(cq, g):
        return _rms(cq, g)

    def body(dz_in_ref, dq_ref, dk_ref, dv_ref, cq_ref, ckv_ref, cos_ref, sin_ref, qg_ref, kvg_ref, wq_ref, wuk_ref,
             wuv_ref, dz_ref, dw_ref, dqg_ref, dkvg_ref):
        del dz_in_ref
        cos_v, sin_v = cos_ref[...], sin_ref[...]

        @pl.when(pl.program_id(0) == 0)
        def _():
            dw_ref[...] = jnp.zeros_like(dw_ref)
            dqg_ref[...] = jnp.zeros_like(dqg_ref)
            dkvg_ref[...] = jnp.zeros_like(dkvg_ref)

        cqn, vjp_q = jax.vjp(fq, cq_ref[...], qg_ref[...])
        ckvn, vjp_kv = jax.vjp(fq, ckv_ref[...], kvg_ref[...])
        lane = lax.broadcasted_iota(jnp.int32, (ts, HEAD_PAD), 1)
        rope_lanes = (lane >= QK_NOPE) & (lane < QK_NOPE + QK_ROPE)
        dq_lin = jnp.concatenate([_rope_transposed(dq_ref[h], cos_v, sin_v).astype(MXU_DTYPE) for h in range(HEADS)], axis=1)
        dk_all = jnp.concatenate([dk_ref[h].astype(MXU_DTYPE) for h in range(HEADS)], axis=1)
        dv_all = jnp.concatenate([dv_ref[h].astype(MXU_DTYPE) for h in range(HEADS)], axis=1)
        dkr = jnp.where(rope_lanes, dk_ref[0], 0.0)
        for h in range(1, HEADS):
            dkr = dkr + jnp.where(rope_lanes, dk_ref[h], 0.0)
        dcq, dqg = vjp_q(_dot_nt(dq_lin, wq_ref[...]))
        dckv, dkvg = vjp_kv(_dot_nt(dk_all, wuk_ref[...]) + _dot_nt(dv_all, wuv_ref[...]))
        dz_ref[:, 0:Q_LORA] = dcq.astype(dz_ref.dtype)
        dz_ref[:, Q_LORA:Q_LORA + KV_LORA] = dckv.astype(dz_ref.dtype)
        dz_ref[:, Q_LORA + KV_LORA:] = _rope_transposed(dkr, cos_v, sin_v).astype(dz_ref.dtype)
        dqg_ref[...] += dqg
        dkvg_ref[...] += dkvg
        dwq, dwuk, dwuv = _dot_tn(cqn, dq_lin), _dot_tn(ckvn, dk_all), _dot_tn(ckvn, dv_all)
        for h in range(HEADS):
            lanes = slice(h * HEAD_PAD, (h + 1) * HEAD_PAD)
            row0 = (h % 2) * MLA_ROWS
            dw_ref[h // 2, row0:row0 + Q_LORA, :] += dwq[:, lanes]
            dw_ref[h // 2, row0 + Q_LORA:row0 + Q_LORA + KV_LORA, :] += dwuk[:, lanes]
            dw_ref[h // 2, row0 + Q_LORA + KV_LORA:row0 + MLA_ROWS, :] += dwuv[:, lanes]

    wide = HEADS * HEAD_PAD
    heads = pl.BlockSpec((HEADS, ts, HEAD_PAD), lambda i: (0, i, 0))
    whole = pl.BlockSpec((N_CHIPS, 2 * MLA_ROWS, HEAD_PAD), lambda i: (0, 0, 0))
    return pl.pallas_call(
        body, name="mla_prep_bwd", grid=(s // ts,),
        in_specs=[_HBM_SPEC, heads, heads, heads, _rows(ts, Q_LORA, 8), _rows(ts, KV_LORA, 18),
                  _rows(ts, HEAD_PAD), _rows(ts, HEAD_PAD), _vec(Q_LORA), _vec(KV_LORA), _vec(wide, 0, Q_LORA),
                  _vec(wide, 0, KV_LORA), _vec(wide, 0, KV_LORA)],
        out_specs=[_rows(ts, 512, 4), whole, _vec(Q_LORA), _vec(KV_LORA)],
        out_shape=[_sds(dz.shape, dz.dtype), _sds((N_CHIPS, 2 * MLA_ROWS, HEAD_PAD)), _sds((1, Q_LORA)), _sds((1, KV_LORA))],
        input_output_aliases={0: 0}, compiler_params=_cp(1))(dz, dq, dk, dv, z, z, cos, sin, qg, kvg, wq, wuk, wuv)


def _chunk_mask(q0, k0, tq, tk):
    rows = q0 + lax.broadcasted_iota(jnp.int32, (tq, tk), 0)
    cols = k0 + lax.broadcasted_iota(jnp.int32, (tq, tk), 1)
    shift = CHUNK.bit_length() - 1
    return lax.shift_right_logical(cols, shift) <= lax.shift_right_logical(rows, shift)


def _attn_fwd(q, k, v, tq):
    s = q.shape[1]
    nq = s // tq
    scale = 1.0 / float(QK_NOPE + QK_ROPE) ** 0.5

    assert nq % 2 == 0, (s, tq)

    def body(q_ref, k_ref, v_ref, o_ref, lse_ref):
        pair, hh = pl.program_id(1), pl.program_id(2)

        def step(qv, q0, kj, carry, masked):
            m, l, acc = carry
            k0 = pl.multiple_of(kj * tq, tq)
            sc = _dot_nt(qv, k_ref[0, pl.ds(k0, tq), :]) * scale
            if masked:
                sc = jnp.where(_chunk_mask(q0, k0, tq, tq), sc, NEG)
            m_new = jnp.maximum(m, jnp.max(sc, axis=-1, keepdims=True))
            alpha = jnp.exp(m - m_new)
            p = jnp.exp(sc - m_new)
            l = alpha * l + jnp.sum(p, axis=-1, keepdims=True)
            acc = alpha * acc + _dot_nn(p, v_ref[0, pl.ds(k0, tq), :])
            return m_new, l, acc

        for half in range(2):
            rows = slice(half * tq, (half + 1) * tq)
            qv = q_ref[0, rows, :]
            q0 = (2 * pair + half) * tq
            two = lambda i, c: step(qv, q0, 2 * i + 1, step(qv, q0, 2 * i, c, False), False)
            init = (jnp.full((tq, 1), NEG, F32), jnp.zeros((tq, 1), F32), jnp.zeros((tq, HEAD_PAD), F32))
            carry = lax.fori_loop(0, pair, two, init)
            if half == 1:
                carry = step(qv, q0, 2 * pair, carry, False)
            m, l, acc = step(qv, q0, 2 * pair + half, carry, True)
            o = acc / l
            lse_ref[0, rows, :] = m + jnp.log(l)

            @pl.when(hh == 0)
            def _():
                o_ref[rows, :] = o

            @pl.when(hh == 1)
            def _():
                o_ref[rows, :] += o

    head = lambda hp, pair, hh: 2 * hp + hh
    return pl.pallas_call(
        body, name="attn_fwd", grid=(HEADS // 2, nq // 2, 2),
        in_specs=[pl.BlockSpec((1, 2 * tq, HEAD_PAD), lambda hp, pair, hh: (head(hp, pair, hh), pair, 0)),
                  pl.BlockSpec((1, s, HEAD_PAD), lambda hp, pair, hh: (head(hp, pair, hh), 0, 0)),
                  pl.BlockSpec((1, s, HEAD_PAD), lambda hp, pair, hh: (head(hp, pair, hh), 0, 0))],
        out_specs=[pl.BlockSpec((2 * tq, HEAD_PAD), lambda hp, pair, hh: (pair, hp)),
                   pl.BlockSpec((1, 2 * tq, 1), lambda hp, pair, hh: (head(hp, pair, hh), pair, 0))],
        out_shape=[_sds((s, HEADS * V_HEAD)), _sds((HEADS, s, 1))],
        compiler_params=_cp(3))(q, k, v)


def _attn_bwd(q, q_t, k, v, do, do_t, o, lse, tq):
    s = q.shape[1]
    nq = s // tq
    per_q = tq // do_t.shape[3]
    scale = 1.0 / float(QK_NOPE + QK_ROPE) ** 0.5

    def body(q_ref, qt_ref, k_ref, v_ref, do_ref, dot_ref, o_ref, lse_ref, dq_ref, dk_ref, dv_ref, dk_t, dv_t):
        hh, kj = pl.program_id(1), pl.program_id(2)

        @pl.when(kj == 0)
        def _():
            dq_ref[...] = jnp.zeros_like(dq_ref)

        kv, vv = k_ref[0], v_ref[0]
        lane = lax.broadcasted_iota(jnp.int32, (tq, HEAD_PAD), 1)
        mine = lax.shift_right_logical(lane, 6) == hh
        dk_t[...] = jnp.zeros_like(dk_t)
        dv_t[...] = jnp.zeros_like(dv_t)

        def step(qi, masked):
            q0 = pl.multiple_of(qi * tq, tq)
            qv = q_ref[0, pl.ds(q0, tq), :]
            dov = do_ref[pl.ds(q0, tq), :]
            delta = jnp.sum(jnp.where(mine, dov * o_ref[pl.ds(q0, tq), :], 0.0), axis=-1, keepdims=True)
            sc = _dot_nt(qv, kv) * scale
            if masked:
                sc = jnp.where(_chunk_mask(q0, kj * tq, tq, tq), sc, NEG)
            p = jnp.exp(sc - lse_ref[0, pl.ds(q0, tq), :])
            ds = (p * (_dot_nt(dov, vv) - delta) * scale).astype(MXU_DTYPE)
            do_tv = jnp.concatenate([dot_ref[0, qi * per_q + r] for r in range(per_q)], axis=1)
            dv_t[...] += _dot_nn(do_tv, p)
            dk_t[...] += _dot_nn(qt_ref[0, qi], ds)
            dq_ref[0, pl.ds(q0, tq), :] += _dot_nn(ds, kv)

        step(kj, True)
        odd = (nq - 1 - kj) % 2

        @pl.when(odd == 1)
        def _():
            step(kj + 1, False)

        def two(i, c):
            step(kj + 1 + odd + 2 * i, False)
            step(kj + 2 + odd + 2 * i, False)
            return c

        lax.fori_loop(0, (nq - 1 - kj) // 2, two, 0)
        dk_ref[0] = dk_t[...].T
        dv_ref[0] = dv_t[...].T

    head = lambda hp, hh, kj: 2 * hp + hh
    full = pl.BlockSpec((1, s, HEAD_PAD), lambda hp, hh, kj: (head(hp, hh, kj), 0, 0))
    blk = pl.BlockSpec((1, tq, HEAD_PAD), lambda hp, hh, kj: (head(hp, hh, kj), kj, 0))
    pair = pl.BlockSpec((s, HEAD_PAD), lambda hp, hh, kj: (0, hp))
    return pl.pallas_call(
        body, name="attn_bwd", grid=(HEADS // 2, 2, nq),
        in_specs=[full, pl.BlockSpec((1,) + q_t.shape[1:], lambda hp, hh, kj: (head(hp, hh, kj), 0, 0, 0)), blk, blk,
                  pair, pl.BlockSpec((1,) + do_t.shape[1:], lambda hp, hh, kj: (hp, 0, 0, 0)), pair,
                  pl.BlockSpec((1, s, 1), lambda hp, hh, kj: (head(hp, hh, kj), 0, 0))],
        out_specs=[full, blk, blk], out_shape=[_sds((HEADS, s, HEAD_PAD))] * 3,
        scratch_shapes=[pltpu.VMEM((HEAD_PAD, tq), F32), pltpu.VMEM((HEAD_PAD, tq), F32)],
        compiler_params=_cp(3))(q, q_t, k, v, do, do_t, o, lse)


def _sc_conv(u, ubuf, w_ref, b_ref, ts):
    return (w_ref[2:3, :] * u + w_ref[1:2, :] * ubuf[pl.ds(SC_HALO - 1, ts), :]
            + w_ref[0:1, :] * ubuf[pl.ds(SC_HALO - 2, ts), :] + b_ref[...])


def _even_gate_fwd(z, o, sc_w, sc_b, ts):
    s = z.shape[0]
    w = SC_WIDTH

    def body(ab_ref, ac_ref, ax_ref, ag_ref, bg_ref, hc_ref, hx_ref, o_ref, w_ref, b_ref, y_ref, ubuf):
        i = pl.program_id(0)
        u = ac_ref[...] * ax_ref[...]
        ubuf[0:SC_HALO, :] = jnp.where(i > 0, hc_ref[...] * hx_ref[...], 0.0)
        ubuf[SC_HALO:, :] = u
        conv = _sc_conv(u, ubuf, w_ref, b_ref, ts)
        y_ref[:, 0:w] = (ab_ref[...] * conv * _silu(ag_ref[...])).astype(y_ref.dtype)
        y_ref[:, w:] = (o_ref[...] * _silu(bg_ref[...])).astype(y_ref.dtype)

    return pl.pallas_call(
        body, name="even_gate_fwd", grid=(s // ts,),
        in_specs=[_rows(ts, w, 0), _rows(ts, w, 1), _rows(ts, w, 2), _rows(ts, w, 3), _rows(ts, w, 5),
                  _prev_halo(ts, SC_HALO, w, 1), _prev_halo(ts, SC_HALO, w, 2), _rows(ts, w),
                  _vec(w, 0, SC_KERNEL), _vec(w)],
        out_specs=_rows(ts, 2 * w), out_shape=_sds((s, 2 * w), MXU_DTYPE),
        scratch_shapes=[pltpu.VMEM((ts + SC_HALO, w), F32)],
        compiler_params=_cp(1))(z, z, z, z, z, z, z, o, sc_w, sc_b)


def _even_gate_bwd(dy, z, o, sc_w, sc_b, ts):
    s = z.shape[0]
    w = SC_WIDTH
    n = s // ts

    def body(dya_ref, dyb_ref, dyan_ref, ab_ref, ac_ref, ax_ref, ag_ref, bg_ref, hc_ref, hx_ref, abn_ref, agn_ref,
             o_ref, w_ref, b_ref, dz_ref, do_ref, dot_ref, dw_ref, db_ref, ubuf, dbuf):
        i = pl.program_id(0)
        ab, ac, ax, ag, bg = ab_ref[...], ac_ref[...], ax_ref[...], ag_ref[...], bg_ref[...]
        dya, dyb = dya_ref[...], dyb_ref[...]
        u = ac * ax
        ubuf[0:SC_HALO, :] = jnp.where(i > 0, hc_ref[...] * hx_ref[...], 0.0)
        ubuf[SC_HALO:, :] = u
        conv = _sc_conv(u, ubuf, w_ref, b_ref, ts)
        sg = _silu(ag)
        dconv = dya * ab * sg
        dbuf[0:ts, :] = dconv
        dbuf[ts:, :] = jnp.where(i < n - 1, dyan_ref[...] * abn_ref[...] * _silu(agn_ref[...]), 0.0)
        du = w_ref[2:3, :] * dconv + w_ref[1:2, :] * dbuf[pl.ds(1, ts), :] + w_ref[0:1, :] * dbuf[pl.ds(2, ts), :]
        dz_ref[:, 0:w] = (dya * conv * sg).astype(dz_ref.dtype)
        dz_ref[:, w:2 * w] = (du * ax).astype(dz_ref.dtype)
        dz_ref[:, 2 * w:3 * w] = (du * ac).astype(dz_ref.dtype)
        dz_ref[:, 3 * w:4 * w] = (dya * ab * conv * _dsilu(ag)).astype(dz_ref.dtype)
        dz_ref[:, 4 * w:5 * w] = jnp.zeros((ts, w), dz_ref.dtype)
        dz_ref[:, 5 * w:] = (dyb * o_ref[...] * _dsilu(bg)).astype(dz_ref.dtype)
        do = dyb * _silu(bg)
        do_ref[...] = do
        for pair in range(HEADS // 2):
            dot_ref[pair, 0] = do[:, pair * HEAD_PAD:(pair + 1) * HEAD_PAD].T.astype(dot_ref.dtype)

        @pl.when(i == 0)
        def _():
            dw_ref[...] = jnp.zeros_like(dw_ref)
            db_ref[...] = jnp.zeros_like(db_ref)

        dw_ref[0:1, :] += jnp.sum(dconv * ubuf[pl.ds(SC_HALO - 2, ts), :], axis=0, keepdims=True)
        dw_ref[1:2, :] += jnp.sum(dconv * ubuf[pl.ds(SC_HALO - 1, ts), :], axis=0, keepdims=True)
        dw_ref[2:3, :] += jnp.sum(dconv * u, axis=0, keepdims=True)
        db_ref[...] += jnp.sum(dconv, axis=0, keepdims=True)

    return pl.pallas_call(
        body, name="even_gate_bwd", grid=(n,),
        in_specs=[_rows(ts, w, 0), _rows(ts, w, 1), _next_halo(ts, SC_HALO, w, 0, s),
                  _rows(ts, w, 0), _rows(ts, w, 1), _rows(ts, w, 2), _rows(ts, w, 3), _rows(ts, w, 5),
                  _prev_halo(ts, SC_HALO, w, 1), _prev_halo(ts, SC_HALO, w, 2),
                  _next_halo(ts, SC_HALO, w, 0, s), _next_halo(ts, SC_HALO, w, 3, s),
                  _rows(ts, w), _vec(w, 0, SC_KERNEL), _vec(w)],
        out_specs=[_rows(ts, EVEN_PAD), _rows(ts, w), pl.BlockSpec((HEADS // 2, 1, HEAD_PAD, ts), lambda i: (0, i, 0, 0)),
                   _vec(w, 0, SC_KERNEL), _vec(w)],
        out_shape=[_sds((s, EVEN_PAD), MXU_DTYPE), _sds((s, w)), _sds((HEADS // 2, n, HEAD_PAD, ts), MXU_DTYPE),
                   _sds((SC_KERNEL, w)), _sds((1, w))],
        scratch_shapes=[pltpu.VMEM((ts + SC_HALO, w), F32), pltpu.VMEM((ts + SC_HALO, w), F32)],
        compiler_params=_cp(1))(dy, dy, dy, z, z, z, z, z, z, z, z, z, o, sc_w, sc_b)


def _ln_act(uc, sg, g, b):
    mu = jnp.mean(uc, axis=-1, keepdims=True)
    var = jnp.mean(jnp.square(uc - mu), axis=-1, keepdims=True)
    return _silu((uc - mu) * lax.rsqrt(var + EPS) * g + b) * _silu(sg)


def _shifted_copies(buf, shifted, rows):
    for b in range(1, SUBLANES):
        shifted[b - 1, 0:rows, :] = buf[pl.ds(b, rows), :]


def _rows_at(buf, shifted, start, n):
    a, b = divmod(start, SUBLANES)
    return buf[pl.ds(SUBLANES * a, n), :] if b == 0 else shifted[b - 1, pl.ds(SUBLANES * a, n), :]


def _odd_fwd(z, conv_w, conv_b, ln_g, ln_b, ts):
    s = z.shape[0]
    d = D_MODEL
    k = CONF_KERNEL

    def body(val_ref, glu_ref, sg_ref, hval_ref, hglu_ref, w_ref, b_ref, g_ref, beta_ref, y_ref, uc_ref, ubuf, ush):
        i = pl.program_id(0)
        ubuf[0:CONF_HALO, :] = jnp.where(i > 0, hval_ref[...] * _sigmoid(hglu_ref[...]), 0.0)
        ubuf[CONF_HALO:, :] = val_ref[...] * _sigmoid(glu_ref[...])
        _shifted_copies(ubuf, ush, ts + CONF_HALO - SUBLANES)
        for r0 in range(0, ts, CONV_ROWS):
            acc = jnp.broadcast_to(b_ref[...], (CONV_ROWS, d))
            for j in range(k):
                acc = acc + w_ref[j:j + 1, :] * _rows_at(ubuf, ush, r0 + CONF_HALO - (k - 1) + j, CONV_ROWS)
            uc_ref[r0:r0 + CONV_ROWS, :] = acc
        y_ref[...] = _ln_act(uc_ref[...], sg_ref[...], g_ref[...], beta_ref[...]).astype(y_ref.dtype)

    return pl.pallas_call(
        body, name="odd_fwd", grid=(s // ts,),
        in_specs=[_rows(ts, d, 0), _rows(ts, d, 1), _rows(ts, d, 2),
                  _prev_halo(ts, CONF_HALO, d, 0), _prev_halo(ts, CONF_HALO, d, 1),
                  _vec(d, 0, k), _vec(d), _vec(d), _vec(d)],
        out_specs=[_rows(ts, d), _rows(ts, d)], out_shape=[_sds((s, d), MXU_DTYPE), _sds((s, d))],
        scratch_shapes=[pltpu.VMEM((ts + CONF_HALO, d), F32),
                        pltpu.VMEM((SUBLANES - 1, ts + CONF_HALO - SUBLANES, d), F32)],
        compiler_params=_cp(1))(z, z, z, z, z, conv_w, conv_b, ln_g, ln_b)


def _odd_bwd(dy, z, uc, conv_w, ln_g, ln_b, ts):
    s = z.shape[0]
    d = D_MODEL
    k = CONF_KERNEL
    n = s // ts

    def body(dy_ref, dyn_ref, val_ref, glu_ref, sg_ref, sgn_ref, uc_ref, ucn_ref,
             w_ref, g_ref, beta_ref, dz_ref, dw_ref, db_ref, dg_ref, dbeta_ref, dbuf, dsh, dw_acc):
        i = pl.program_id(0)
        val, glu = val_ref[...], glu_ref[...]
        sig = _sigmoid(glu)
        u = val * sig
        _, vjp = jax.vjp(_ln_act, uc_ref[...], sg_ref[...], g_ref[...], beta_ref[...])
        duc, dsg, dg, dbeta = vjp(dy_ref[...])
        _, vjp_n = jax.vjp(_ln_act, ucn_ref[...], sgn_ref[...], g_ref[...], beta_ref[...])
        dbuf[0:ts, :] = duc
        dbuf[ts:, :] = jnp.where(i < n - 1, vjp_n(dyn_ref[...])[0], 0.0)
        dz_ref[:, 2 * d:] = dsg.astype(dz_ref.dtype)
        _shifted_copies(dbuf, dsh, ts + CONF_HALO - SUBLANES)

        @pl.when(i == 0)
        def _():
            dw_acc[...] = jnp.zeros_like(dw_acc)
            db_ref[...] = jnp.zeros_like(db_ref)
            dg_ref[...] = jnp.zeros_like(dg_ref)
            dbeta_ref[...] = jnp.zeros_like(dbeta_ref)

        db_ref[...] += jnp.sum(duc, axis=0, keepdims=True)
        dg_ref[...] += dg
        dbeta_ref[...] += dbeta
        for r0 in range(0, ts, CONV_ROWS):
            acc = jnp.zeros((CONV_ROWS, d), F32)
            for j in range(k):
                acc = acc + w_ref[j:j + 1, :] * _rows_at(dbuf, dsh, r0 + (k - 1) - j, CONV_ROWS)
            sig_r = sig[r0:r0 + CONV_ROWS, :]
            dz_ref[r0:r0 + CONV_ROWS, 0:d] = (acc * sig_r).astype(dz_ref.dtype)
            dz_ref[r0:r0 + CONV_ROWS, d:2 * d] = (acc * val[r0:r0 + CONV_ROWS, :] * sig_r * (1.0 - sig_r)).astype(dz_ref.dtype)
        for j in range(k):
            prod = _rows_at(dbuf, dsh, (k - 1) - j, ts) * u
            dw_acc[j] += jnp.sum(prod.reshape(ts // SUBLANES, SUBLANES, d), axis=0)

        @pl.when(i == n - 1)
        def _():
            dw_ref[...] = jnp.sum(dw_acc[...], axis=1)

    return pl.pallas_call(
        body, name="odd_bwd", grid=(n,),
        in_specs=[_rows(ts, d), _next_halo(ts, CONF_HALO, d, 0, s),
                  _rows(ts, d, 0), _rows(ts, d, 1), _rows(ts, d, 2), _next_halo(ts, CONF_HALO, d, 2, s),
                  _rows(ts, d), _next_halo(ts, CONF_HALO, d, 0, s),
                  _vec(d, 0, k), _vec(d), _vec(d)],
        out_specs=[_rows(ts, ODD_IN), _vec(d, 0, k), _vec(d), _vec(d), _vec(d)],
        out_shape=[_sds((s, ODD_IN), MXU_DTYPE), _sds((k, d)), _sds((1, d)), _sds((1, d)), _sds((1, d))],
        scratch_shapes=[pltpu.VMEM((ts + CONF_HALO, d), F32),
                        pltpu.VMEM((SUBLANES - 1, ts + CONF_HALO - SUBLANES, d), F32), pltpu.VMEM((k, SUBLANES, d), F32)],
        compiler_params=_cp(1))(dy, dy, z, z, z, z, uc, uc, conv_w, ln_g, ln_b)


def _local_step(x, target, cos, sin, mod, p, layer_weights, fwd_dep=None, grads_done=None):
    s = x.shape[0]
    tsf, tsb = min(512, s // 2), min(256, s // 2)
    tq = min(512, s // 2)
    row1 = lambda a, i: a[i:i + 1]
    saved = []
    h = _pre_fwd(x, row1(p["pre_norm_g"], 0), row1(mod, 0), tsf, fwd_dep)
    for layer in range(DEPTH):
        i = layer // 2
        mod_l = row1(mod, layer)
        wl = layer_weights(layer, h)
        if layer % 2 == 0:
            z = _mm(h, wl["w_in"], "nn", F32, 512, EVEN_PAD, "even_in_fwd")
            if "late" in wl:
                wl.update(wl.pop("late")(z))
            q, q_t, k, v = _mla_prep_fwd(z, cos, sin, row1(p["even_q_norm_g"], i), row1(p["even_kv_norm_g"], i),
                                    wl["wq"], wl["wq_rot"], wl["wuk"], wl["wuv"], tsf)
            o, lse = _attn_fwd(q, k, v, tq)
            y = _even_gate_fwd(z, o, wl["sc_conv_w"], row1(p["even_sc_conv_b"], i), tsf)
            yo = _mm(y, wl["w_out"], "nn", F32, 512, 1024, "even_out_fwd")
            saved.append((x, h, z, y, yo, wl, (q, q_t, k, v, o, lse)))
        else:
            z = _mm(h, wl["w_in"], "nn", F32, 512, ODD_IN, "odd_in_fwd")
            y, uc = _odd_fwd(z, wl["conv_w"], wl["conv_b"], wl["ln_g"], wl["ln_b"], tsf)
            yo = _mm(y, wl["w_out"], "nn", F32, 512, 1024, "odd_out_fwd")
            saved.append((x, h, z, y, yo, wl, uc))
        if layer + 1 < DEPTH:
            x, h = _post_pre_fwd(x, yo, row1(p["post_norm_g"], layer), mod_l, row1(p["pre_norm_g"], layer + 1),
                                 row1(mod, layer + 1), tsf)
        else:
            loss, dx = _post_loss(x, yo, row1(p["post_norm_g"], layer), mod_l, target, tsf)

    g = {n: [None] * (DEPTH if n in ("pre_norm_g", "post_norm_g") else N_PAIRS) for n in (
        "pre_norm_g", "post_norm_g", "even_sc_conv_w", "even_sc_conv_b", "even_q_norm_g", "even_kv_norm_g",
        "odd_conv_w", "odd_conv_b", "odd_ln_g", "odd_ln_b")}
    dmod = [None] * DEPTH
    dep = None
    for layer in reversed(range(DEPTH)):
        i = layer // 2
        mod_l = row1(mod, layer)
        x_in, h, z, y, yo, wl, extra = saved[layer]
        dyo, dgate, g["post_norm_g"][layer] = _post_bwd(dx, yo, row1(p["post_norm_g"], layer), mod_l, tsb, dep)
        bufs = {}
        if layer % 2 == 0:
            q, q_t, k, v, o, lse = extra
            dy = _mm(dyo, wl["w_out"], "nt", F32, 512, 1024, "even_out_bwd_x")
            bufs["even_w_out"] = _mm_tn_shards(y, dyo, "rows", "even_out_bwd_w")
            dz, do, do_t, g["even_sc_conv_w"][i], g["even_sc_conv_b"][i] = _even_gate_bwd(
                dy, z, o, wl["sc_conv_w"], row1(p["even_sc_conv_b"], i), tsb)
            dq, dk, dv = _attn_bwd(q, q_t, k, v, do, do_t, o, lse, tq)
            dz, bufs["even_mla"], g["even_q_norm_g"][i], g["even_kv_norm_g"][i] = _mla_prep_bwd(
                dz, dq, dk, dv, z, cos, sin, row1(p["even_q_norm_g"], i), row1(p["even_kv_norm_g"], i),
                wl["wq"], wl["wuk"], wl["wuv"], tsb)
            bufs["even_w_in"] = _ein_to_shards(_mm(h, dz, "tn", F32, D_MODEL, 512, "even_in_bwd_w"))
        else:
            uc = extra
            dy = _mm(dyo, wl["w_out"], "nt", F32, 512, 1024, "odd_out_bwd_x")
            bufs["odd_w_out"] = _mm_tn_shards(y, dyo, "rows", "odd_out_bwd_w")
            dz, g["odd_conv_w"][i], g["odd_conv_b"][i], g["odd_ln_g"][i], g["odd_ln_b"][i] = _odd_bwd(
                dy, z, uc, wl["conv_w"], wl["ln_g"], wl["ln_b"], tsb)
            bufs["odd_w_in"] = _mm_tn_shards(h, dz, "cols", "odd_in_bwd_w")
        dx, dshift, dscale, g["pre_norm_g"][layer] = _pre_bwd(
            dz, wl["w_in"], dx, x_in, row1(p["pre_norm_g"], layer), mod_l, tsf)
        dmod[layer] = jnp.concatenate([dshift, dscale, dgate], axis=-1)
        dep = grads_done(layer, bufs, dx) if grads_done is not None else None
    stack = lambda parts: jnp.stack([a[0] if a.shape[0] == 1 and a.ndim == 2 else a for a in parts])
    small = {n: stack(parts) for n, parts in g.items()}
    small["dmod"] = jnp.concatenate(dmod, axis=0)
    return loss, dx, small


def _uq_to_heads(w):
    w = w.reshape(N_CHIPS, Q_LORA, 2, QK_NOPE + QK_ROPE).transpose(0, 2, 1, 3).reshape(HEADS, Q_LORA, QK_NOPE + QK_ROPE)
    half = QK_ROPE // 2
    rotated = jnp.concatenate([jnp.zeros_like(w[..., :QK_NOPE]), -w[..., QK_NOPE + half:], w[..., QK_NOPE:QK_NOPE + half]],
                              axis=-1)
    pad = ((0, 0), (0, 0), (0, HEAD_PAD - QK_NOPE - QK_ROPE))
    return _side_by_side(jnp.pad(w, pad)), _side_by_side(jnp.pad(rotated, pad))


def _side_by_side(w):
    return w.transpose(1, 0, 2).reshape(w.shape[1], HEADS * HEAD_PAD)


def _ukv_to_heads(w):
    w = w.reshape(N_CHIPS, KV_LORA, 2, QK_NOPE + V_HEAD).transpose(0, 2, 1, 3).reshape(HEADS, KV_LORA, QK_NOPE + V_HEAD)
    wk = jnp.pad(w[..., :QK_NOPE], ((0, 0), (0, 0), (0, HEAD_PAD - QK_NOPE)))
    wv = w[..., QK_NOPE:]
    zero = jnp.zeros_like(wv)
    odd = (jnp.arange(HEADS) % 2 == 1)[:, None, None]
    wv = jnp.concatenate([jnp.where(odd, zero, wv), jnp.where(odd, wv, zero)], axis=-1)
    return _side_by_side(wk), _side_by_side(wv)


def _mla_local(q):
    blocks = q.reshape(2, MLA_ROWS, HEAD_PAD)
    uq = jnp.concatenate([blocks[r, :Q_LORA, :QK_NOPE + QK_ROPE] for r in range(2)], axis=-1)
    ukv = jnp.concatenate(
        [jnp.concatenate([blocks[r, Q_LORA:Q_LORA + KV_LORA, :QK_NOPE],
                          blocks[r, Q_LORA + KV_LORA:, V_HEAD * r:V_HEAD * (r + 1)]], axis=-1) for r in range(2)], axis=-1)
    return uq, ukv


def _place():
    return lax.axis_index("x"), lax.axis_index("y"), lax.axis_index("c")


def _flip(v, bit):
    return 1 - v if bit else v


def _sem(a, k):
    return a * (N_CHIPS - 1) + k - 1


def _remote(src, dst, send_sem, recv_sem, peer):
    return pltpu.make_async_remote_copy(src_ref=src, dst_ref=dst, send_sem=send_sem, recv_sem=recv_sem,
                                        device_id=peer, device_id_type=MESH)


_VMEM_SPEC = pl.BlockSpec(memory_space=pltpu.VMEM)
_HBM_SPEC = pl.BlockSpec(memory_space=pl.ANY)


def _ada_fwd(c8, ada_w, ada_b_sh):
    depth, d, cols = ada_w.shape

    def body(c_ref, w_ref, b_ref, call_ref, mod_ref, s1, r1, s2, r2):
        x, y, c = _place()
        chip = 2 * x + y
        me = 2 * chip + c
        call_ref[me] = c_ref[...]
        sends = []
        for k in range(1, N_DEV):
            peer = (_flip(x, k & 4), _flip(y, k & 2), _flip(c, k & 1))
            cp = _remote(c_ref, call_ref.at[me], s1.at[k - 1], r1.at[k - 1], peer)
            cp.start()
            sends.append(cp)
        for k in range(1, N_DEV):
            src = 4 * _flip(x, k & 4) + 2 * _flip(y, k & 2) + _flip(c, k & 1)
            _remote(c_ref, call_ref.at[src], s1.at[k - 1], r1.at[k - 1], (x, y, c)).wait_recv()
        act = _silu(jnp.concatenate([call_ref[e, 0:1, :] for e in range(N_DEV)], axis=0))
        for l in range(depth):
            mod_ref[chip, l] = _dot_nn(act, w_ref[l]) + b_ref[l:l + 1, :]
        for k in range(1, N_CHIPS):
            peer = (_flip(x, k & 2), _flip(y, k & 1), c)
            cp = _remote(mod_ref.at[chip], mod_ref.at[chip], s2.at[k - 1], r2.at[k - 1], peer)
            cp.start()
            sends.append(cp)
        for k in range(1, N_CHIPS):
            src = 2 * _flip(x, k & 2) + _flip(y, k & 1)
            _remote(mod_ref.at[src], mod_ref.at[src], s2.at[k - 1], r2.at[k - 1], (x, y, c)).wait_recv()
        for cp in sends:
            cp.wait_send()

    return pl.pallas_call(
        body, name="ada_fwd", in_specs=[_VMEM_SPEC] * 3, out_specs=[_VMEM_SPEC] * 2,
        out_shape=[_sds((N_DEV, 8, d)), _sds((N_CHIPS, depth, N_DEV, cols))],
        scratch_shapes=[pltpu.SemaphoreType.DMA((N_DEV - 1,)), pltpu.SemaphoreType.DMA((N_DEV - 1,)),
                        pltpu.SemaphoreType.DMA((N_CHIPS - 1,)), pltpu.SemaphoreType.DMA((N_CHIPS - 1,))],
        compiler_params=pltpu.CompilerParams(vmem_limit_bytes=VMEM_LIMIT_V7X))(c8, ada_w, ada_b_sh)


def _ada_bwd(c_t, dmod_sh):
    depth, n, cols = dmod_sh.shape
    d = c_t.shape[0]
    tr = 256

    def body(c_ref, dm_ref, o_ref):
        act = _silu(c_ref[...])
        acc = act[:, 0:1] * dm_ref[0, 0:1, :]
        for e in range(1, n):
            acc = acc + act[:, e:e + 1] * dm_ref[0, e:e + 1, :]
        o_ref[0] = acc

    return pl.pallas_call(
        body, name="ada_bwd", grid=(depth, d // tr),
        in_specs=[pl.BlockSpec((tr, n), lambda l, i: (i, 0)), pl.BlockSpec((1, n, cols), lambda l, i: (l, 0, 0))],
        out_specs=pl.BlockSpec((1, tr, cols), lambda l, i: (l, i, 0)), out_shape=_sds((depth, d, cols)),
        compiler_params=_cp(2))(c_t, dmod_sh)


def _gathered_shape(shape, how):
    if how == "slot":
        return (N_CHIPS,) + shape
    r, cc = shape
    return (r, N_CHIPS * cc) if how == "cols" else (N_CHIPS * r, cc)


def _gathered_part(ref, shape, how, chip):
    if how == "slot":
        return ref.at[chip]
    if how == "cols":
        return ref.at[:, pl.ds(pl.multiple_of(chip * shape[1], 128), shape[1])]
    return ref.at[pl.ds(pl.multiple_of(chip * shape[0], 8), shape[0]), :]


_SEM_SPEC = pl.BlockSpec(memory_space=pltpu.SEMAPHORE)
_TOKEN = jax.ShapeDtypeStruct((8, 128), F32)
_SPLIT_COPY = pltpu.CompilerParams(has_side_effects=pltpu.SideEffectType.DATAFLOW_SIDE_EFFECTING)


def _in_hbm(a):
    return pltpu.with_memory_space_constraint(a, pltpu.HBM)


def _gather_start(items, gathered, name, after=()):
    n = len(items)

    def body(*refs):
        ins, outs = refs[:n], refs[n:2 * n]
        send_sems, recv_sems = refs[2 * n + len(after)], refs[2 * n + len(after) + 1]
        x, y, c = _place()
        for a in range(n):
            for k in range(1, N_CHIPS):
                part = _gathered_part(outs[a], items[a][0].shape, items[a][1], 2 * x + y)
                _remote(ins[a], part, send_sems.at[_sem(a, k)], recv_sems.at[_sem(a, k)],
                        (_flip(x, k & 2), _flip(y, k & 1), c)).start()
        refs[-1][...] = jnp.zeros(_TOKEN.shape, _TOKEN.dtype)

    arrays = [_in_hbm(a) for a, _ in items] + [_in_hbm(a) for a in gathered]
    res = pl.pallas_call(
        body, name=name, in_specs=[_HBM_SPEC] * (2 * n + len(after)),
        out_specs=[_SEM_SPEC, _SEM_SPEC] + [_HBM_SPEC] * (2 * n) + [_VMEM_SPEC],
        out_shape=[pltpu.SemaphoreType.DMA((n * (N_CHIPS - 1),)), pltpu.SemaphoreType.DMA((n * (N_CHIPS - 1),))]
        + [pltpu.HBM(a.shape, a.dtype) for a in arrays] + [_TOKEN],
        input_output_aliases={a: 2 + a for a in range(2 * n)}, compiler_params=_SPLIT_COPY)(*arrays, *after)
    return res[0], res[1], res[2:2 + n], res[2 + n:2 + 2 * n], res[-1]


def _gather_wait(items, started, after, name):
    n = len(items)
    send_sems, recv_sems, shards, gathered, _ = started

    def body(*refs):
        ins, outs, send_sems, recv_sems = refs[:n], refs[n:2 * n], refs[2 * n], refs[2 * n + 1]
        x, y, c = _place()
        for a in range(n):
            for k in range(1, N_CHIPS):
                part = _gathered_part(outs[a], items[a][0].shape, items[a][1], 2 * _flip(x, k & 2) + _flip(y, k & 1))
                cp = _remote(ins[a], part, send_sems.at[_sem(a, k)], recv_sems.at[_sem(a, k)], (x, y, c))
                cp.wait_send()
                cp.wait_recv()

    res = pl.pallas_call(
        body, name=name, in_specs=[_HBM_SPEC] * (2 * n) + [_SEM_SPEC, _SEM_SPEC] + [_HBM_SPEC] * len(after),
        out_specs=[_HBM_SPEC] * (2 * n), out_shape=[pltpu.HBM(a.shape, a.dtype) for a in (*shards, *gathered)],
        input_output_aliases={a: a for a in range(2 * n)}, compiler_params=_SPLIT_COPY)(
            *shards, *gathered, send_sems, recv_sems, *after)
    return res[n:]


def _rs_start(bufs, name, after=()):
    n = len(bufs)

    def body(*refs):
        srcs, lands = refs[:n], refs[n:2 * n]
        send_sems, recv_sems = refs[2 * n + len(after)], refs[2 * n + len(after) + 1]
        x, y, c = _place()
        for a in range(n):
            for k in range(1, N_CHIPS):
                tx, ty = _flip(x, k & 2), _flip(y, k & 1)
                _remote(srcs[a].at[2 * tx + ty], lands[a].at[k - 1], send_sems.at[_sem(a, k)], recv_sems.at[_sem(a, k)],
                        (tx, ty, c)).start()
        refs[-1][...] = jnp.zeros(_TOKEN.shape, _TOKEN.dtype)

    arrays = [_in_hbm(b) for b in bufs] + [_in_hbm(lax.empty((N_CHIPS - 1,) + b.shape[1:], b.dtype)) for b in bufs]
    res = pl.pallas_call(
        body, name=name, in_specs=[_HBM_SPEC] * (2 * n + len(after)),
        out_specs=[_SEM_SPEC, _SEM_SPEC] + [_HBM_SPEC] * (2 * n) + [_VMEM_SPEC],
        out_shape=[pltpu.SemaphoreType.DMA((n * (N_CHIPS - 1),)), pltpu.SemaphoreType.DMA((n * (N_CHIPS - 1),))]
        + [pltpu.HBM(a.shape, a.dtype) for a in arrays] + [_TOKEN],
        input_output_aliases={a: 2 + a for a in range(2 * n)}, compiler_params=_SPLIT_COPY)(*arrays, *after)
    return res[0], res[1], res[2:2 + n], res[2 + n:2 + 2 * n], res[-1]


def _rs_wait(started, after, name):
    send_sems, recv_sems, bufs, lands, _ = started
    n = len(bufs)

    def body(*refs):
        srcs, lnds, send_sems, recv_sems = refs[:n], refs[n:2 * n], refs[2 * n], refs[2 * n + 1]
        x, y, c = _place()
        for a in range(n):
            for k in range(1, N_CHIPS):
                cp = _remote(srcs[a].at[0], lnds[a].at[k - 1], send_sems.at[_sem(a, k)], recv_sems.at[_sem(a, k)], (x, y, c))
                cp.wait_send()
                cp.wait_recv()

    res = pl.pallas_call(
        body, name=name, in_specs=[_HBM_SPEC] * (2 * n) + [_SEM_SPEC, _SEM_SPEC] + [_HBM_SPEC] * len(after),
        out_specs=[_HBM_SPEC] * (2 * n), out_shape=[pltpu.HBM(a.shape, a.dtype) for a in (*bufs, *lands)],
        input_output_aliases={a: a for a in range(2 * n)}, compiler_params=_SPLIT_COPY)(
            *bufs, *lands, send_sems, recv_sems, *after)
    return res[:n], res[n:]


def _place_own(shard, how, chip_idx):
    r, cc = shard.shape
    block, index = {"slot": ((1, r, cc), lambda i, c: (c[0], 0, 0)), "cols": ((r, cc), lambda i, c: (0, c[0])),
                    "rows": ((r, cc), lambda i, c: (c[0], 0))}[how]

    def body(c_ref, in_ref, o_ref):
        del c_ref
        o_ref[...] = in_ref[...].reshape(o_ref.shape)

    return pl.pallas_call(
        body, name="place_own", out_shape=_sds(_gathered_shape(shard.shape, how), shard.dtype),
        grid_spec=pltpu.PrefetchScalarGridSpec(
            num_scalar_prefetch=1, grid=(1,), in_specs=[pl.BlockSpec((r, cc), lambda i, c: (0, 0))],
            out_specs=pl.BlockSpec(block, index)),
        compiler_params=_cp(1))(chip_idx, shard)


def _gather_sum_all(small):
    r, w = small.shape

    def body(in_ref, all_ref, sum_ref, send_sems, recv_sems):
        x, y, c = _place()
        me = 4 * x + 2 * y + c
        all_ref[me] = in_ref[...]
        sends = []
        for k in range(1, N_DEV):
            peer = (_flip(x, k & 4), _flip(y, k & 2), _flip(c, k & 1))
            cp = _remote(in_ref, all_ref.at[me], send_sems.at[k - 1], recv_sems.at[k - 1], peer)
            cp.start()
            sends.append(cp)
        for k in range(1, N_DEV):
            src = 4 * _flip(x, k & 4) + 2 * _flip(y, k & 2) + _flip(c, k & 1)
            _remote(in_ref, all_ref.at[src], send_sems.at[k - 1], recv_sems.at[k - 1], (x, y, c)).wait_recv()
        acc = all_ref[0]
        for e in range(1, N_DEV):
            acc = acc + all_ref[e]
        sum_ref[...] = acc
        for cp in sends:
            cp.wait_send()

    return pl.pallas_call(
        body, name="gather_sum_all", in_specs=[_VMEM_SPEC], out_specs=[_VMEM_SPEC] * 2,
        out_shape=[_sds((N_DEV, r, w)), _sds((r, w))],
        scratch_shapes=[pltpu.SemaphoreType.DMA((N_DEV - 1,)), pltpu.SemaphoreType.DMA((N_DEV - 1,))],
        compiler_params=pltpu.CompilerParams(vmem_limit_bytes=VMEM_LIMIT_V7X))(small)


def _add_chips(buf, t, chip_idx):
    r, cc = buf.shape[1:]
    tr = min(256, r)

    def body(c_ref, p_ref, t_ref, o_ref):
        del c_ref
        o_ref[...] = p_ref[0] + t_ref[0].astype(F32) + t_ref[1].astype(F32) + t_ref[2].astype(F32)

    return pl.pallas_call(
        body, name="add_chips", out_shape=_sds((r, cc)),
        grid_spec=pltpu.PrefetchScalarGridSpec(
            num_scalar_prefetch=1, grid=(r // tr,),
            in_specs=[pl.BlockSpec((1, tr, cc), lambda i, c: (c[0], i, 0)),
                      pl.BlockSpec((N_CHIPS - 1, tr, cc), lambda i, c: (0, i, 0))],
            out_specs=pl.BlockSpec((tr, cc), lambda i, c: (i, 0))),
        compiler_params=_cp(1))(chip_idx, buf, t)


def _rs_sibling(qs):
    n = len(qs)

    def body(*refs):
        ins, outs = refs[:n], refs[n:2 * n]
        send_sems, recv_sems = refs[2 * n:]
        x, y, c = _place()
        copies = [_remote(ins[a], outs[a], send_sems.at[a], recv_sems.at[a], (x, y, 1 - c)) for a in range(n)]
        for cp in copies:
            cp.start()
        for cp in copies:
            cp.wait()

    return pl.pallas_call(
        body, name="rs_sibling", in_specs=[_HBM_SPEC] * n, out_specs=[_HBM_SPEC] * n,
        out_shape=[_sds(q.shape) for q in qs],
        scratch_shapes=[pltpu.SemaphoreType.DMA((n,)), pltpu.SemaphoreType.DMA((n,))])(*qs)


def _adamw_update(w, g, m, v):
    m = ADAM_B1 * m + (1.0 - ADAM_B1) * g
    v = ADAM_B2 * v + (1.0 - ADAM_B2) * jnp.square(g)
    m_hat = m / (1.0 - ADAM_B1 ** ADAM_STEP)
    v_hat = v / (1.0 - ADAM_B2 ** ADAM_STEP)
    return -ADAM_LR * (m_hat / (jnp.sqrt(v_hat) + ADAM_EPS) + ADAM_WD * w), m, v


def _adamw(w, g_parts, m, v, name):
    shape = w.shape
    cols = shape[-1]
    rows = _size(shape[:-1])
    tr = 512 if rows % 512 == 0 else rows
    spec = pl.BlockSpec((tr, cols), lambda i: (i, 0))
    n = len(g_parts)
    n_out = 4 if n > 1 else 3

    def body(*refs):
        w_ref, m_ref, v_ref = refs[:3]
        d_ref, nm_ref, nv_ref = refs[-3:]
        g = refs[3][...]
        for r in refs[4:3 + n]:
            g = g + r[...]
        if n > 1:
            refs[3 + n][...] = g
        d_ref[...], nm_ref[...], nv_ref[...] = _adamw_update(w_ref[...], g, m_ref[...], v_ref[...])

    outs = pl.pallas_call(
        body, name="adamw_" + name, grid=(rows // tr,), in_specs=[spec] * (3 + n), out_specs=[spec] * n_out,
        out_shape=[_sds((rows, cols))] * n_out, compiler_params=_cp(1))(
            *[a.reshape(rows, cols) for a in (w, m, v, *g_parts)])
    outs = tuple(o.reshape(shape) for o in outs)
    return outs if n > 1 else (g_parts[0],) + outs


def _adamw_layer(w, g_parts, m, v, layer, prev, name):
    _, r, cc = w.shape
    tr = 512 if r % 512 == 0 else r
    spec = pl.BlockSpec((1, tr, cc), lambda i: (layer, i, 0))
    n = len(g_parts)

    def body(*refs):
        w_ref, m_ref, v_ref = refs[:3]
        g_ref, d_ref, nm_ref, nv_ref = refs[-4:]
        g = refs[3][...]
        for q in refs[4:3 + n]:
            g = g + q[...]
        g = g[:, :cc]
        g_ref[0] = g
        d_ref[0], nm_ref[0], nv_ref[0] = _adamw_update(w_ref[0], g, m_ref[0], v_ref[0])

    g_specs = [pl.BlockSpec((tr, q.shape[1]), lambda i: (i, 0)) for q in g_parts]
    passed = () if prev is None else tuple(prev)
    return pl.pallas_call(
        body, name="adamw_" + name, grid=(r // tr,),
        in_specs=[spec] * 3 + g_specs + [_HBM_SPEC] * len(passed), out_specs=[spec] * 4,
        out_shape=[_sds(w.shape)] * 4, input_output_aliases={3 + n + k: k for k in range(len(passed))},
        compiler_params=_cp(1))(w, m, v, *g_parts, *passed)


def _size(shape):
    n = 1
    for s in shape:
        n *= s
    return n


_SMALL = (("dmod", (DEPTH, 3 * D_MODEL)), ("pre_norm_g", (DEPTH, D_MODEL)), ("post_norm_g", (DEPTH, D_MODEL)),
          ("even_sc_conv_w", (2, SC_KERNEL, SC_WIDTH)), ("even_sc_conv_b", (2, SC_WIDTH)),
          ("even_q_norm_g", (2, Q_LORA)), ("even_kv_norm_g", (2, KV_LORA)),
          ("odd_conv_w", (2, CONF_KERNEL, D_MODEL)), ("odd_conv_b", (2, D_MODEL)), ("odd_ln_g", (2, D_MODEL)),
          ("odd_ln_b", (2, D_MODEL)))
SMALL_ROWS = -(-sum(_size(s) for _, s in _SMALL) // (8 * 128)) * 8

_SMALL_W = (("even_sc_conv_w", (2, SC_KERNEL, SC_WIDTH // N_CHIPS)), ("odd_conv_w", (2, CONF_KERNEL, D_MODEL // N_CHIPS)),
            ("odd_conv_b", (2, D_MODEL // N_CHIPS)), ("odd_ln_g", (2, D_MODEL // N_CHIPS)),
            ("odd_ln_b", (2, D_MODEL // N_CHIPS)))
SMALL_W_ROWS = -(-sum(_size(s) for _, s in _SMALL_W) // (8 * 128)) * 8


def _pack_rows(arrays, layout, rows):
    flat = jnp.concatenate([arrays[n].reshape(-1) for n, _ in layout])
    return jnp.pad(flat, (0, rows * 128 - flat.shape[0])).reshape(rows, 128)


def _unpack_small(t):
    flat = t.reshape(-1)
    out, at = {}, 0
    for n, shape in _SMALL:
        out[n] = flat[at:at + _size(shape)].reshape(shape)
        at += _size(shape)
    return out


def _unpack_small_w(t):
    flat = t.reshape(N_CHIPS, -1)
    out, at = {}, 0
    for n, shape in _SMALL_W:
        a = flat[:, at:at + _size(shape)].reshape((N_CHIPS,) + shape)
        out[n] = jnp.moveaxis(a, 0, -2).reshape(shape[:-1] + (N_CHIPS * shape[-1],))
        at += _size(shape)
    return out


def _chip_cols(a, chip):
    n = a.shape[-1] // N_CHIPS
    return lax.dynamic_slice_in_dim(a, chip * n, n, axis=a.ndim - 1)


WEIGHT_NAMES = ("ada_w", "ada_b", "pre_norm_g", "post_norm_g", "even_w_in", "even_sc_conv_w", "even_sc_conv_b",
                "even_q_norm_g", "even_kv_norm_g", "even_w_uq", "even_w_ukv", "even_w_out", "odd_w_in", "odd_conv_w",
                "odd_conv_b", "odd_ln_g", "odd_ln_b", "odd_w_out")
GATHER_HOW = ((("even_w_in", "slot"), ("even_w_uq", "slot"), ("even_w_ukv", "slot"), ("even_w_out", "rows")),
              (("odd_w_in", "cols"), ("odd_w_out", "rows")))


def kernel(x, c, positions, ada_w, ada_b, pre_norm_g, post_norm_g, even_w_in, even_sc_conv_w, even_sc_conv_b, even_q_norm_g, even_kv_norm_g, even_w_uq, even_w_ukv, even_w_out, odd_w_in, odd_conv_w, odd_conv_b, odd_ln_g, odd_ln_b, odd_w_out, loss_target, m_ada_w, m_ada_b, m_pre_norm_g, m_post_norm_g, m_even_w_in, m_even_sc_conv_w, m_even_sc_conv_b, m_even_q_norm_g, m_even_kv_norm_g, m_even_w_uq, m_even_w_ukv, m_even_w_out, m_odd_w_in, m_odd_conv_w, m_odd_conv_b, m_odd_ln_g, m_odd_ln_b, m_odd_w_out, v_ada_w, v_ada_b, v_pre_norm_g, v_post_norm_g, v_even_w_in, v_even_sc_conv_w, v_even_sc_conv_b, v_even_q_norm_g, v_even_kv_norm_g, v_even_w_uq, v_even_w_ukv, v_even_w_out, v_odd_w_in, v_odd_conv_w, v_odd_conv_b, v_odd_ln_g, v_odd_ln_b, v_odd_w_out):
    w = dict(zip(WEIGHT_NAMES, (ada_w, ada_b, pre_norm_g, post_norm_g, even_w_in, even_sc_conv_w, even_sc_conv_b,
                                even_q_norm_g, even_kv_norm_g, even_w_uq, even_w_ukv, even_w_out, odd_w_in, odd_conv_w,
                                odd_conv_b, odd_ln_g, odd_ln_b, odd_w_out)))
    m = dict(zip(WEIGHT_NAMES, (m_ada_w, m_ada_b, m_pre_norm_g, m_post_norm_g, m_even_w_in, m_even_sc_conv_w,
                                m_even_sc_conv_b, m_even_q_norm_g, m_even_kv_norm_g, m_even_w_uq, m_even_w_ukv,
                                m_even_w_out, m_odd_w_in, m_odd_conv_w, m_odd_conv_b, m_odd_ln_g, m_odd_ln_b, m_odd_w_out)))
    v = dict(zip(WEIGHT_NAMES, (v_ada_w, v_ada_b, v_pre_norm_g, v_post_norm_g, v_even_w_in, v_even_sc_conv_w,
                                v_even_sc_conv_b, v_even_q_norm_g, v_even_kv_norm_g, v_even_w_uq, v_even_w_ukv,
                                v_even_w_out, v_odd_w_in, v_odd_conv_w, v_odd_conv_b, v_odd_ln_g, v_odd_ln_b, v_odd_w_out)))
    ix, iy, ic = _place()
    chip = 2 * ix + iy
    me = 2 * chip + ic
    s = x.shape[1]

    c_all, mod_all = _ada_fwd(jnp.broadcast_to(c, (8, D_MODEL)), ada_w, _chip_cols(ada_b, chip))
    mod = lax.dynamic_index_in_dim(mod_all, me, axis=2, keepdims=False)
    mod = mod.transpose(1, 0, 2).reshape(DEPTH, 3 * D_MODEL)

    items = [[(w[n][layer // 2].astype(MXU_DTYPE), how) for n, how in GATHER_HOW[layer % 2]] for layer in range(DEPTH)]
    groups = [items[0][:1], items[0][1:] + [(_pack_rows(w, _SMALL_W, SMALL_W_ROWS), "slot")],
              [item for layer_items in items[1:] for item in layer_items]]
    sent, dep = [], mod_all
    for number, group in enumerate(groups):
        sent.append(_gather_start(group, [_place_own(a, how, chip.reshape(1)) for a, how in group],
                                  "gather_start_%d" % number, [dep]))
        dep = sent[-1][-1]
    arrived = {}

    def group(number, after):
        if number not in arrived:
            arrived[number] = _gather_wait(groups[number], sent[number], after, "gather_wait_%d" % number)
        return arrived[number]

    def even_rest(i, uq, ukv, eout, small_w):
        wuk, wuv = _ukv_to_heads(ukv)
        wq, wq_rot = _uq_to_heads(uq)
        return {"wq": wq, "wq_rot": wq_rot, "wuk": wuk, "wuv": wuv, "w_out": eout, "sc_conv_w": small_w["even_sc_conv_w"][i]}

    def layer_weights(layer, h):
        i = layer // 2
        if layer == 0:
            def late(z):
                uq, ukv, eout, small = group(1, [z])
                return even_rest(i, uq, ukv, eout, _unpack_small_w(small))
            return {"w_in": _ein_from_shards(group(0, [h])[0]), "late": late}
        small_w = _unpack_small_w(group(1, [h])[-1])
        at = sum(len(layer_items) for layer_items in items[1:layer])
        arrays = group(2, [h])[at:at + len(items[layer])]
        if layer % 2 == 0:
            return {"w_in": _ein_from_shards(arrays[0]), **even_rest(i, *arrays[1:], small_w)}
        oin, oout = arrays
        return {"w_in": oin, "w_out": oout, "conv_w": small_w["odd_conv_w"][i], "conv_b": small_w["odd_conv_b"][i:i + 1],
                "ln_g": small_w["odd_ln_g"][i:i + 1], "ln_b": small_w["odd_ln_b"][i:i + 1]}

    in_flight, own, sib, last = {}, {}, {}, {}

    def land(layer, after):
        names, started, kept = in_flight.pop(layer)
        bufs, arrived = _rs_wait(started, after, "rs_wait_%d" % layer)
        sums = [_add_chips(b, t, chip.reshape(1)) for b, t in zip(bufs if kept is None else kept, arrived)]
        for n, mine, theirs in zip(names, sums, _rs_sibling(sums)):
            own[n, layer // 2], sib[n, layer // 2] = mine, theirs

    def grads_done(layer, bufs, dx_in):
        if layer + 1 in in_flight:
            land(layer + 1, [dx_in])
        if layer == 0:
            last.update(bufs)
            return None
        names = sorted(bufs)
        in_flight[layer] = (names, _rs_start([bufs[n] for n in names], "rs_start_%d" % layer), None)
        return in_flight[layer][1][-1]

    p = {"pre_norm_g": pre_norm_g, "post_norm_g": post_norm_g, "even_sc_conv_b": even_sc_conv_b,
         "even_q_norm_g": even_q_norm_g, "even_kv_norm_g": even_kv_norm_g}
    inv_freq = 1.0 / (ROPE_THETA ** (jnp.arange(0, QK_ROPE, 2, dtype=F32) / QK_ROPE))
    inv_freq = jnp.zeros((1, HEAD_PAD), F32).at[0, QK_NOPE:QK_NOPE + QK_ROPE].set(jnp.tile(inv_freq, 2))
    cos, sin = _rope_tables(positions.reshape(s, 1), inv_freq)

    loss, dx, g = _local_step(x[0], loss_target[0], cos, sin, mod, p, layer_weights, dep, grads_done)

    grads, deltas, new_m, new_v = {}, {}, {}, {}

    def update_layers(n, results, pairs):
        for i in pairs:
            results = _adamw_layer(w[n], [own[n, i], sib[n, i]], m[n], v[n], i, results, n)
        return results

    small_all, small_sum = _gather_sum_all(_pack_rows(g, _SMALL, SMALL_ROWS))
    names = sorted(last)
    kept = [last[n] for n in names]
    in_flight[0] = (names, _rs_start([b.astype(jnp.bfloat16) for b in kept], "rs_start_0", [small_sum]), kept)
    tot = _unpack_small(small_sum)
    dmod_all = small_all[:, :DEPTH * 3 * D_MODEL // 128].reshape(N_DEV, DEPTH, 3 * D_MODEL)
    grads["ada_w"] = _ada_bwd(c_all[:, 0, :].T, _chip_cols(dmod_all, chip).transpose(1, 0, 2))
    grads["ada_b"] = tot["dmod"]
    for n in ("pre_norm_g", "post_norm_g", "even_sc_conv_b", "even_q_norm_g", "even_kv_norm_g"):
        grads[n] = tot[n]
    for n in ("even_sc_conv_w", "odd_conv_w", "odd_conv_b", "odd_ln_g", "odd_ln_b"):
        grads[n] = _chip_cols(tot[n], chip)
    for n in list(grads):
        _, deltas[n], new_m[n], new_v[n] = _adamw(w[n], [grads[n]], m[n], v[n], n)

    for n in ("odd_w_in", "odd_w_out"):
        grads[n], deltas[n], new_m[n], new_v[n] = update_layers(n, None, (1, 0))
    partly = {n: update_layers(n, None, (1,)) for n in ("even_w_in", "even_w_out")}
    land(0, [deltas["ada_w"], deltas["odd_w_in"], partly["even_w_in"][1]])
    for n in ("even_w_in", "even_w_out"):
        grads[n], deltas[n], new_m[n], new_v[n] = update_layers(n, partly[n], (0,))
    uq_parts, ukv_parts = zip(*[[jnp.stack(part) for part in zip(*[_mla_local(q["even_mla", i]) for i in range(N_PAIRS)])]
                                for q in (own, sib)])
    for n, parts in (("even_w_uq", uq_parts), ("even_w_ukv", ukv_parts)):
        grads[n], deltas[n], new_m[n], new_v[n] = _adamw(w[n], list(parts), m[n], v[n], n)

    total_loss = lax.psum(loss[0, 0], ("x", "y", "c"))
    return (total_loss, dx[None], *[grads[n] for n in WEIGHT_NAMES], *[deltas[n] for n in WEIGHT_NAMES],
            *[new_m[n] for n in WEIGHT_NAMES], *[new_v[n] for n in WEIGHT_NAMES])
```

```python
import jax
import jax.numpy as jnp
from jax import lax
from jax.experimental import pallas as pl
from jax.experimental.pallas import tpu as pltpu

F32 = jnp.float32
MXU_DTYPE = jnp.bfloat16
MESH = pl.DeviceIdType.MESH
VMEM_LIMIT_V7X = 56 * 2 ** 20

EPS = 1e-6
D_MODEL = 1024
DEPTH = 4
CHUNK = 64
SC_WIDTH = 512
SC_KERNEL = 3
SC_HALO = 8
HEADS = 8
QK_NOPE = 64
QK_ROPE = 32
V_HEAD = 64
HEAD_PAD = 128
Q_LORA = 256
KV_LORA = 128
ROPE_THETA = 10000.0
CONF_KERNEL = 31
CONF_HALO = 32
CONV_ROWS = 32
SUBLANES = 8
EVEN_IN = 2976
EVEN_PAD = 3072
ODD_IN = 3072
N_CHIPS = 4
N_DEV = 8
NEG = -1e30

ADAM_LR = 0.001
ADAM_B1 = 0.9
ADAM_B2 = 0.999
ADAM_EPS = 1e-08
ADAM_WD = 0.01
ADAM_STEP = 10

N_PAIRS = DEPTH // 2
EVEN_SHARD = EVEN_IN // N_CHIPS
EVEN_SHARD_PAD = 768
MLA_ROWS = Q_LORA + 2 * KV_LORA


def _cp(n_grid=0, **kw):
    return pltpu.CompilerParams(dimension_semantics=("arbitrary",) * n_grid,
                                vmem_limit_bytes=VMEM_LIMIT_V7X, **kw)


def _sigmoid(x):
    return 1.0 / (1.0 + jnp.exp(-x))


def _silu(x):
    return x * _sigmoid(x)


def _dsilu(x):
    s = _sigmoid(x)
    return s * (1.0 + x * (1.0 - s))


def _rms(x, g):
    return x * lax.rsqrt(jnp.mean(x * x, axis=-1, keepdims=True) + EPS) * g


def _dot(a, b, dims):
    return lax.dot_general(a.astype(MXU_DTYPE), b.astype(MXU_DTYPE), (dims, ((), ())),
                           preferred_element_type=F32)


def _dot_nn(a, b):
    return _dot(a, b, ((1,), (0,)))


def _dot_nt(a, b):
    return _dot(a, b, ((1,), (1,)))


def _dot_tn(a, b):
    return _dot(a, b, ((0,), (0,)))


def _rows(ts, w, cb=0):
    return pl.BlockSpec((ts, w), lambda i: (i, cb))


def _vec(w, cb=0, r=1):
    return pl.BlockSpec((r, w), lambda i: (0, cb))


def _prev_halo(ts, hr, w, cb):
    return pl.BlockSpec((hr, w), lambda i: (jnp.maximum(i * (ts // hr) - 1, 0), cb))


def _next_halo(ts, hr, w, cb, s):
    return pl.BlockSpec((hr, w), lambda i: (jnp.minimum((i + 1) * (ts // hr), s // hr - 1), cb))


def _sds(shape, dtype=F32):
    return jax.ShapeDtypeStruct(shape, dtype)


def _mm(a, b, mode, out_dtype, tm, tn, name):
    tm = min(tm, a.shape[1] if mode == "tn" else a.shape[0])
    tn = min(tn, b.shape[0] if mode == "nt" else b.shape[1])
    if mode == "nn":
        (m, k), n = a.shape, b.shape[1]
        a_spec = pl.BlockSpec((tm, k), lambda i, j: (i, 0))
        b_spec = pl.BlockSpec((k, tn), lambda i, j: (0, j))
        dot = _dot_nn
    elif mode == "nt":
        (m, k), n = a.shape, b.shape[0]
        a_spec = pl.BlockSpec((tm, k), lambda i, j: (i, 0))
        b_spec = pl.BlockSpec((tn, k), lambda i, j: (j, 0))
        dot = _dot_nt
    else:
        (k, m), n = a.shape, b.shape[1]
        a_spec = pl.BlockSpec((k, tm), lambda i, j: (0, i))
        b_spec = pl.BlockSpec((k, tn), lambda i, j: (0, j))
        dot = _dot_tn
    assert m % tm == 0 and n % tn == 0, (name, m, n, tm, tn)

    def body(a_ref, b_ref, o_ref):
        o_ref[...] = dot(a_ref[...], b_ref[...]).astype(o_ref.dtype)

    return pl.pallas_call(
        body, name=name, grid=(m // tm, n // tn), in_specs=[a_spec, b_spec],
        out_specs=pl.BlockSpec((tm, tn), lambda i, j: (i, j)), out_shape=_sds((m, n), out_dtype),
        compiler_params=_cp(2))(a, b)


def _mm_tn_shards(a, b, by, name):
    k, m = a.shape
    n = b.shape[1]
    if by == "cols":
        tm, tn = m, n // N_CHIPS
        shape, grid = (N_CHIPS, m, tn), (1, N_CHIPS)
        out_spec = pl.BlockSpec((1, tm, tn), lambda i, j: (j, i, 0))
    else:
        tm, tn = m // N_CHIPS, n
        shape, grid = (N_CHIPS, tm, n), (N_CHIPS, 1)
        out_spec = pl.BlockSpec((1, tm, tn), lambda i, j: (i, 0, j))

    def body(a_ref, b_ref, o_ref):
        o_ref[0] = _dot_tn(a_ref[...], b_ref[...])

    return pl.pallas_call(
        body, name=name, grid=grid,
        in_specs=[pl.BlockSpec((k, tm), lambda i, j: (0, i)), pl.BlockSpec((k, tn), lambda i, j: (0, j))],
        out_specs=out_spec, out_shape=_sds(shape), compiler_params=_cp(2))(a, b)


def _even_col(q):
    return q if q < 2432 else (q + 64 if q < 2464 else q + 96)


def _shard_pieces(j):
    lo, hi = EVEN_SHARD * j, EVEN_SHARD * (j + 1)
    cuts = [lo] + [b for b in (2432, 2464) if lo < b < hi] + [hi]
    return [(a - lo, _even_col(a), b - a) for a, b in zip(cuts[:-1], cuts[1:])]


def _ein_from_shards(w):
    _, d, _ = w.shape
    tr = 256

    def body(w_ref, o_ref):
        parts, at = [], 0
        for j in range(N_CHIPS):
            for d0, s0, n in _shard_pieces(j):
                if s0 > at:
                    parts.append(jnp.zeros((tr, s0 - at), F32))
                parts.append(w_ref[j, :, d0:d0 + n].astype(F32))
                at = s0 + n
        o_ref[...] = jnp.concatenate(parts, axis=1).astype(o_ref.dtype)

    return pl.pallas_call(
        body, name="ein_from_shards", grid=(d // tr,),
        in_specs=[pl.BlockSpec((N_CHIPS, tr, EVEN_SHARD), lambda i: (0, i, 0))],
        out_specs=_rows(tr, EVEN_PAD), out_shape=_sds((d, EVEN_PAD), w.dtype), compiler_params=_cp(1))(w)


def _ein_to_shards(dw):
    d = dw.shape[0]
    tr = 256

    def body(dw_ref, o_ref):
        for j in range(N_CHIPS):
            parts = [dw_ref[:, s0:s0 + n] for _, s0, n in _shard_pieces(j)]
            o_ref[j] = jnp.concatenate(parts + [jnp.zeros((tr, EVEN_SHARD_PAD - EVEN_SHARD), F32)], axis=1)

    return pl.pallas_call(
        body, name="ein_to_shards", grid=(d // tr,), in_specs=[_rows(tr, EVEN_PAD)],
        out_specs=pl.BlockSpec((N_CHIPS, tr, EVEN_SHARD_PAD), lambda i: (0, i, 0)),
        out_shape=_sds((N_CHIPS, d, EVEN_SHARD_PAD)), compiler_params=_cp(1))(dw)


def _rope_tables(pos_col, invf):
    s = pos_col.shape[0]
    ts = min(512, s)

    def body(p_ref, f_ref, c_ref, s_ref):
        ang = p_ref[...].astype(F32) * f_ref[...]
        lane = lax.broadcasted_iota(jnp.int32, ang.shape, 1)
        rope = (lane >= QK_NOPE) & (lane < QK_NOPE + QK_ROPE)
        c_ref[...] = jnp.where(lane < QK_NOPE, 1.0, jnp.where(rope, jnp.cos(ang), 0.0))
        s_ref[...] = jnp.where(rope, jnp.sin(ang), 0.0)

    return pl.pallas_call(
        body, name="rope_tables", grid=(s // ts,), in_specs=[_rows(ts, 1), _vec(HEAD_PAD)],
        out_specs=[_rows(ts, HEAD_PAD)] * 2, out_shape=[_sds((s, HEAD_PAD))] * 2,
        compiler_params=_cp(1))(pos_col, invf)


def _after(dep):
    return () if dep is None else (dep,)


def _pre_fwd(x, g, mod_l, ts, dep=None):
    s, d = x.shape

    def body(x_ref, g_ref, sh_ref, sc_ref, *rest):
        h = _rms(x_ref[...], g_ref[...]) * (1.0 + sc_ref[...]) + sh_ref[...]
        rest[-1][...] = h.astype(rest[-1].dtype)

    return pl.pallas_call(
        body, name="pre_fwd", grid=(s // ts,),
        in_specs=[_rows(ts, d), _vec(d), _vec(d, 0), _vec(d, 1)] + [_HBM_SPEC] * len(_after(dep)),
        out_specs=_rows(ts, d), out_shape=_sds((s, d), MXU_DTYPE), compiler_params=_cp(1))(
            x, g, mod_l, mod_l, *_after(dep))


def _pre_bwd(dz, w_in, dx_out, x, g, mod_l, ts, dep=None):
    s, d = x.shape
    n_in = dz.shape[1]

    def f(xv, gv, sh, sc):
        return _rms(xv, gv) * (1.0 + sc) + sh

    def body(dz_ref, w_ref, dxo_ref, x_ref, g_ref, sh_ref, sc_ref, *rest):
        dx_ref, dsh_ref, dsc_ref, dg_ref = rest[-4:]
        @pl.when(pl.program_id(0) == 0)
        def _():
            dsh_ref[...] = jnp.zeros_like(dsh_ref)
            dsc_ref[...] = jnp.zeros_like(dsc_ref)
            dg_ref[...] = jnp.zeros_like(dg_ref)

        _, vjp = jax.vjp(f, x_ref[...], g_ref[...], sh_ref[...], sc_ref[...])
        dx, dg, dsh, dsc = vjp(_dot_nt(dz_ref[...], w_ref[...]))
        dx_ref[...] = dxo_ref[...] + dx
        dsh_ref[...] += dsh
        dsc_ref[...] += dsc
        dg_ref[...] += dg

    return pl.pallas_call(
        body, name="pre_bwd", grid=(s // ts,),
        in_specs=[_rows(ts, n_in), _vec(n_in, 0, d), _rows(ts, d), _rows(ts, d), _vec(d), _vec(d, 0), _vec(d, 1)]
        + [_HBM_SPEC] * len(_after(dep)),
        out_specs=[_rows(ts, d), _vec(d), _vec(d), _vec(d)],
        out_shape=[_sds((s, d)), _sds((1, d)), _sds((1, d)), _sds((1, d))],
        compiler_params=_cp(1))(dz, w_in, dx_out, x, g, mod_l, mod_l, *_after(dep))


def _post_pre_fwd(x, yo, g_post, mod_l, g_pre, mod_next, ts):
    s, d = x.shape

    def body(x_ref, yo_ref, gp_ref, gate_ref, g_ref, sh_ref, sc_ref, x_out_ref, h_ref):
        x_new = x_ref[...] + gate_ref[...] * _rms(yo_ref[...], gp_ref[...])
        x_out_ref[...] = x_new
        h_ref[...] = (_rms(x_new, g_ref[...]) * (1.0 + sc_ref[...]) + sh_ref[...]).astype(h_ref.dtype)

    return pl.pallas_call(
        body, name="post_pre_fwd", grid=(s // ts,),
        in_specs=[_rows(ts, d), _rows(ts, d), _vec(d), _vec(d, 2), _vec(d), _vec(d, 0), _vec(d, 1)],
        out_specs=[_rows(ts, d), _rows(ts, d)], out_shape=[_sds((s, d)), _sds((s, d), MXU_DTYPE)],
        compiler_params=_cp(1))(x, yo, g_post, mod_l, g_pre, mod_next, mod_next)


def _post_loss(x, yo, g_post, mod_l, target, ts):
    s, d = x.shape

    def body(x_ref, yo_ref, gp_ref, gate_ref, t_ref, loss_ref, dx_ref):
        err = x_ref[...] + gate_ref[...] * _rms(yo_ref[...], gp_ref[...]) - t_ref[...]
        dx_ref[...] = err * (1.0 / d)

        @pl.when(pl.program_id(0) == 0)
        def _():
            loss_ref[...] = jnp.zeros_like(loss_ref)

        loss_ref[...] += 0.5 * jnp.sum(jnp.sum(err * err, axis=-1, keepdims=True) * (1.0 / d), axis=0, keepdims=True)

    return pl.pallas_call(
        body, name="post_loss", grid=(s // ts,),
        in_specs=[_rows(ts, d), _rows(ts, d), _vec(d), _vec(d, 2), _rows(ts, d)],
        out_specs=[_vec(1), _rows(ts, d)], out_shape=[_sds((1, 1)), _sds((s, d))],
        compiler_params=_cp(1))(x, yo, g_post, mod_l, target)


def _post_bwd(dx_out, yo, g, mod_l, ts, dep=None):
    s, d = yo.shape

    def f(yov, gv, gate):
        return gate * _rms(yov, gv)

    def body(dx_ref, yo_ref, g_ref, gate_ref, *rest):
        dyo_ref, dgate_ref, dg_ref = rest[-3:]
        i = pl.program_id(0)
        _, vjp = jax.vjp(f, yo_ref[...], g_ref[...], gate_ref[...])
        dyo, dg, dgate = vjp(dx_ref[...])
        dyo_ref[...] = dyo.astype(dyo_ref.dtype)

        @pl.when(i == 0)
        def _():
            dgate_ref[...] = jnp.zeros_like(dgate_ref)
            dg_ref[...] = jnp.zeros_like(dg_ref)

        dgate_ref[...] += dgate
        dg_ref[...] += dg

    return pl.pallas_call(
        body, name="post_bwd", grid=(s // ts,),
        in_specs=[_rows(ts, d), _rows(ts, d), _vec(d), _vec(d, 2)] + [_HBM_SPEC] * len(_after(dep)),
        out_specs=[_rows(ts, d), _vec(d), _vec(d)],
        out_shape=[_sds((s, d), MXU_DTYPE), _sds((1, d)), _sds((1, d))],
        compiler_params=_cp(1))(dx_out, yo, g, mod_l, *_after(dep))


def _rope(t, cos, sin):
    lane = lax.broadcasted_iota(jnp.int32, t.shape, 1)
    first = (lane >= QK_NOPE) & (lane < QK_NOPE + QK_ROPE // 2)
    second = (lane >= QK_NOPE + QK_ROPE // 2) & (lane < QK_NOPE + QK_ROPE)
    up = pltpu.roll(t, QK_ROPE // 2, 1)
    down = pltpu.roll(t, HEAD_PAD - QK_ROPE // 2, 1)
    return t * cos + jnp.where(first, -down, jnp.where(second, up, 0.0)) * sin


def _rope_transposed(g, cos, sin):
    lane = lax.broadcasted_iota(jnp.int32, g.shape, 1)
    first = (lane >= QK_NOPE) & (lane < QK_NOPE + QK_ROPE // 2)
    second = (lane >= QK_NOPE + QK_ROPE // 2) & (lane < QK_NOPE + QK_ROPE)
    u = g * sin
    up = pltpu.roll(u, QK_ROPE // 2, 1)
    down = pltpu.roll(u, HEAD_PAD - QK_ROPE // 2, 1)
    return g * cos + jnp.where(first, down, jnp.where(second, -up, 0.0))


def _mla_prep_fwd(z, cos, sin, qg, kvg, wq, wq_rot, wuk, wuv, ts):
    s = z.shape[0]
    wide = HEADS * HEAD_PAD

    def body(cq_ref, ckv_ref, kr_ref, cos_ref, sin_ref, qg_ref, kvg_ref, wq_ref, wqr_ref, wuk_ref, wuv_ref,
             q_ref, qt_ref, k_ref, v_ref):
        cos_v, sin_v = cos_ref[...], sin_ref[...]
        cqn = _rms(cq_ref[...], qg_ref[...])
        ckvn = _rms(ckv_ref[...], kvg_ref[...])
        kr = _rope(kr_ref[...], cos_v, sin_v)
        q_lin, q_rot = _dot_nn(cqn, wq_ref[...]), _dot_nn(cqn, wqr_ref[...])
        k_lin, v_all = _dot_nn(ckvn, wuk_ref[...]), _dot_nn(ckvn, wuv_ref[...])
        for h in range(HEADS):
            lanes = slice(h * HEAD_PAD, (h + 1) * HEAD_PAD)
            qh = q_lin[:, lanes] * cos_v + q_rot[:, lanes] * sin_v
            q_ref[h] = qh.astype(q_ref.dtype)
            qt_ref[h, 0] = qh.T.astype(qt_ref.dtype)
            k_ref[h] = (k_lin[:, lanes] + kr).astype(k_ref.dtype)
            v_ref[h] = v_all[:, lanes].astype(v_ref.dtype)

    out = pl.BlockSpec((HEADS, ts, HEAD_PAD), lambda i: (0, i, 0))
    return pl.pallas_call(
        body, name="mla_prep_fwd", grid=(s // ts,),
        in_specs=[_rows(ts, Q_LORA, 8), _rows(ts, KV_LORA, 18), _rows(ts, HEAD_PAD, 19), _rows(ts, HEAD_PAD), _rows(ts, HEAD_PAD),
                  _vec(Q_LORA), _vec(KV_LORA), _vec(wide, 0, Q_LORA), _vec(wide, 0, Q_LORA), _vec(wide, 0, KV_LORA),
                  _vec(wide, 0, KV_LORA)],
        out_specs=[out, pl.BlockSpec((HEADS, 1, HEAD_PAD, ts), lambda i: (0, i, 0, 0)), out, out],
        out_shape=[_sds((HEADS, s, HEAD_PAD), MXU_DTYPE), _sds((HEADS, s // ts, HEAD_PAD, ts), MXU_DTYPE)]
        + [_sds((HEADS, s, HEAD_PAD), MXU_DTYPE)] * 2,
        compiler_params=_cp(1))(z, z, z, cos, sin, qg, kvg, wq, wq_rot, wuk, wuv)


def _mla_prep_bwd(dz, dq, dk, dv, z, cos, sin, qg, kvg, wq, wuk, wuv, ts):
    s = z.shape[0]

    def fq(cq, g):
        return _rms(cq, g)

    def body(dz_in_ref, dq_ref, dk_ref, dv_ref, cq_ref, ckv_ref, cos_ref, sin_ref, qg_ref, kvg_ref, wq_ref, wuk_ref,
             wuv_ref, dz_ref, dw_ref, dqg_ref, dkvg_ref):
        del dz_in_ref
        cos_v, sin_v = cos_ref[...], sin_ref[...]

        @pl.when(pl.program_id(0) == 0)
        def _():
            dw_ref[...] = jnp.zeros_like(dw_ref)
            dqg_ref[...] = jnp.zeros_like(dqg_ref)
            dkvg_ref[...] = jnp.zeros_like(dkvg_ref)

        cqn, vjp_q = jax.vjp(fq, cq_ref[...], qg_ref[...])
        ckvn, vjp_kv = jax.vjp(fq, ckv_ref[...], kvg_ref[...])
        lane = lax.broadcasted_iota(jnp.int32, (ts, HEAD_PAD), 1)
        rope_lanes = (lane >= QK_NOPE) & (lane < QK_NOPE + QK_ROPE)
        dq_lin = jnp.concatenate([_rope_transposed(dq_ref[h], cos_v, sin_v).astype(MXU_DTYPE) for h in range(HEADS)], axis=1)
        dk_all = jnp.concatenate([dk_ref[h].astype(MXU_DTYPE) for h in range(HEADS)], axis=1)
        dv_all = jnp.concatenate([dv_ref[h].astype(MXU_DTYPE) for h in range(HEADS)], axis=1)
        dkr = jnp.where(rope_lanes, dk_ref[0], 0.0)
        for h in range(1, HEADS):
            dkr = dkr + jnp.where(rope_lanes, dk_ref[h], 0.0)
        dcq, dqg = vjp_q(_dot_nt(dq_lin, wq_ref[...]))
        dckv, dkvg = vjp_kv(_dot_nt(dk_all, wuk_ref[...]) + _dot_nt(dv_all, wuv_ref[...]))
        dz_ref[:, 0:Q_LORA] = dcq.astype(dz_ref.dtype)
        dz_ref[:, Q_LORA:Q_LORA + KV_LORA] = dckv.astype(dz_ref.dtype)
        dz_ref[:, Q_LORA + KV_LORA:] = _rope_transposed(dkr, cos_v, sin_v).astype(dz_ref.dtype)
        dqg_ref[...] += dqg
        dkvg_ref[...] += dkvg
        dwq, dwuk, dwuv = _dot_tn(cqn, dq_lin), _dot_tn(ckvn, dk_all), _dot_tn(ckvn, dv_all)
        for h in range(HEADS):
            lanes = slice(h * HEAD_PAD, (h + 1) * HEAD_PAD)
            row0 = (h % 2) * MLA_ROWS
            dw_ref[h // 2, row0:row0 + Q_LORA, :] += dwq[:, lanes]
            dw_ref[h // 2, row0 + Q_LORA:row0 + Q_LORA + KV_LORA, :] += dwuk[:, lanes]
            dw_ref[h // 2, row0 + Q_LORA + KV_LORA:row0 + MLA_ROWS, :] += dwuv[:, lanes]

    wide = HEADS * HEAD_PAD
    heads = pl.BlockSpec((HEADS, ts, HEAD_PAD), lambda i: (0, i, 0))
    whole = pl.BlockSpec((N_CHIPS, 2 * MLA_ROWS, HEAD_PAD), lambda i: (0, 0, 0))
    return pl.pallas_call(
        body, name="mla_prep_bwd", grid=(s // ts,),
        in_specs=[_HBM_SPEC, heads, heads, heads, _rows(ts, Q_LORA, 8), _rows(ts, KV_LORA, 18),
                  _rows(ts, HEAD_PAD), _rows(ts, HEAD_PAD), _vec(Q_LORA), _vec(KV_LORA), _vec(wide, 0, Q_LORA),
                  _vec(wide, 0, KV_LORA), _vec(wide, 0, KV_LORA)],
        out_specs=[_rows(ts, 512, 4), whole, _vec(Q_LORA), _vec(KV_LORA)],
        out_shape=[_sds(dz.shape, dz.dtype), _sds((N_CHIPS, 2 * MLA_ROWS, HEAD_PAD)), _sds((1, Q_LORA)), _sds((1, KV_LORA))],
        input_output_aliases={0: 0}, compiler_params=_cp(1))(dz, dq, dk, dv, z, z, cos, sin, qg, kvg, wq, wuk, wuv)


def _chunk_mask(q0, k0, tq, tk):
    rows = q0 + lax.broadcasted_iota(jnp.int32, (tq, tk), 0)
    cols = k0 + lax.broadcasted_iota(jnp.int32, (tq, tk), 1)
    shift = CHUNK.bit_length() - 1
    return lax.shift_right_logical(cols, shift) <= lax.shift_right_logical(rows, shift)


def _attn_fwd(q, k, v, tq):
    s = q.shape[1]
    nq = s // tq
    scale = 1.0 / float(QK_NOPE + QK_ROPE) ** 0.5

    assert nq % 2 == 0, (s, tq)

    def body(q_ref, k_ref, v_ref, o_ref, lse_ref):
        pair, hh = pl.program_id(1), pl.program_id(2)

        def step(qv, q0, kj, carry, masked):
            m, l, acc = carry
            k0 = pl.multiple_of(kj * tq, tq)
            sc = _dot_nt(qv, k_ref[0, pl.ds(k0, tq), :]) * scale
            if masked:
                sc = jnp.where(_chunk_mask(q0, k0, tq, tq), sc, NEG)
            m_new = jnp.maximum(m, jnp.max(sc, axis=-1, keepdims=True))
            alpha = jnp.exp(m - m_new)
            p = jnp.exp(sc - m_new)
            l = alpha * l + jnp.sum(p, axis=-1, keepdims=True)
            acc = alpha * acc + _dot_nn(p, v_ref[0, pl.ds(k0, tq), :])
            return m_new, l, acc

        for half in range(2):
            rows = slice(half * tq, (half + 1) * tq)
            qv = q_ref[0, rows, :]
            q0 = (2 * pair + half) * tq
            two = lambda i, c: step(qv, q0, 2 * i + 1, step(qv, q0, 2 * i, c, False), False)
            init = (jnp.full((tq, 1), NEG, F32), jnp.zeros((tq, 1), F32), jnp.zeros((tq, HEAD_PAD), F32))
            carry = lax.fori_loop(0, pair, two, init)
            if half == 1:
                carry = step(qv, q0, 2 * pair, carry, False)
            m, l, acc = step(qv, q0, 2 * pair + half, carry, True)
            o = acc / l
            lse_ref[0, rows, :] = m + jnp.log(l)

            @pl.when(hh == 0)
            def _():
                o_ref[rows, :] = o

            @pl.when(hh == 1)
            def _():
                o_ref[rows, :] += o

    head = lambda hp, pair, hh: 2 * hp + hh
    return pl.pallas_call(
        body, name="attn_fwd", grid=(HEADS // 2, nq // 2, 2),
        in_specs=[pl.BlockSpec((1, 2 * tq, HEAD_PAD), lambda hp, pair, hh: (head(hp, pair, hh), pair, 0)),
                  pl.BlockSpec((1, s, HEAD_PAD), lambda hp, pair, hh: (head(hp, pair, hh), 0, 0)),
                  pl.BlockSpec((1, s, HEAD_PAD), lambda hp, pair, hh: (head(hp, pair, hh), 0, 0))],
        out_specs=[pl.BlockSpec((2 * tq, HEAD_PAD), lambda hp, pair, hh: (pair, hp)),
                   pl.BlockSpec((1, 2 * tq, 1), lambda hp, pair, hh: (head(hp, pair, hh), pair, 0))],
        out_shape=[_sds((s, HEADS * V_HEAD)), _sds((HEADS, s, 1))],
        compiler_params=_cp(3))(q, k, v)


def _attn_bwd(q, q_t, k, v, do, do_t, o, lse, tq):
    s = q.shape[1]
    nq = s // tq
    per_q = tq // do_t.shape[3]
    scale = 1.0 / float(QK_NOPE + QK_ROPE) ** 0.5

    def body(q_ref, qt_ref, k_ref, v_ref, do_ref, dot_ref, o_ref, lse_ref, dq_ref, dk_ref, dv_ref, dk_t, dv_t):
        hh, kj = pl.program_id(1), pl.program_id(2)

        @pl.when(kj == 0)
        def _():
            dq_ref[...] = jnp.zeros_like(dq_ref)

        kv, vv = k_ref[0], v_ref[0]
        lane = lax.broadcasted_iota(jnp.int32, (tq, HEAD_PAD), 1)
        mine = lax.shift_right_logical(lane, 6) == hh
        dk_t[...] = jnp.zeros_like(dk_t)
        dv_t[...] = jnp.zeros_like(dv_t)

        def step(qi, masked):
            q0 = pl.multiple_of(qi * tq, tq)
            qv = q_ref[0, pl.ds(q0, tq), :]
            dov = do_ref[pl.ds(q0, tq), :]
            delta = jnp.sum(jnp.where(mine, dov * o_ref[pl.ds(q0, tq), :], 0.0), axis=-1, keepdims=True)
            sc = _dot_nt(qv, kv) * scale
            if masked:
                sc = jnp.where(_chunk_mask(q0, kj * tq, tq, tq), sc, NEG)
            p = jnp.exp(sc - lse_ref[0, pl.ds(q0, tq), :])
            ds = (p * (_dot_nt(dov, vv) - delta) * scale).astype(MXU_DTYPE)
            do_tv = jnp.concatenate([dot_ref[0, qi * per_q + r] for r in range(per_q)], axis=1)
            dv_t[...] += _dot_nn(do_tv, p)
            dk_t[...] += _dot_nn(qt_ref[0, qi], ds)
            dq_ref[0, pl.ds(q0, tq), :] += _dot_nn(ds, kv)

        step(kj, True)
        odd = (nq - 1 - kj) % 2

        @pl.when(odd == 1)
        def _():
            step(kj + 1, False)

        def two(i, c):
            step(kj + 1 + odd + 2 * i, False)
            step(kj + 2 + odd + 2 * i, False)
            return c

        lax.fori_loop(0, (nq - 1 - kj) // 2, two, 0)
        dk_ref[0] = dk_t[...].T
        dv_ref[0] = dv_t[...].T

    head = lambda hp, hh, kj: 2 * hp + hh
    full = pl.BlockSpec((1, s, HEAD_PAD), lambda hp, hh, kj: (head(hp, hh, kj), 0, 0))
    blk = pl.BlockSpec((1, tq, HEAD_PAD), lambda hp, hh, kj: (head(hp, hh, kj), kj, 0))
    pair = pl.BlockSpec((s, HEAD_PAD), lambda hp, hh, kj: (0, hp))
    return pl.pallas_call(
        body, name="attn_bwd", grid=(HEADS // 2, 2, nq),
        in_specs=[full, pl.BlockSpec((1,) + q_t.shape[1:], lambda hp, hh, kj: (head(hp, hh, kj), 0, 0, 0)), blk, blk,
                  pair, pl.BlockSpec((1,) + do_t.shape[1:], lambda hp, hh, kj: (hp, 0, 0, 0)), pair,
                  pl.BlockSpec((1, s, 1), lambda hp, hh, kj: (head(hp, hh, kj), 0, 0))],
        out_specs=[full, blk, blk], out_shape=[_sds((HEADS, s, HEAD_PAD))] * 3,
        scratch_shapes=[pltpu.VMEM((HEAD_PAD, tq), F32), pltpu.VMEM((HEAD_PAD, tq), F32)],
        compiler_params=_cp(3))(q, q_t, k, v, do, do_t, o, lse)


def _sc_conv(u, ubuf, w_ref, b_ref, ts):
    return (w_ref[2:3, :] * u + w_ref[1:2, :] * ubuf[pl.ds(SC_HALO - 1, ts), :]
            + w_ref[0:1, :] * ubuf[pl.ds(SC_HALO - 2, ts), :] + b_ref[...])


def _even_gate_fwd(z, o, sc_w, sc_b, ts):
    s = z.shape[0]
    w = SC_WIDTH

    def body(ab_ref, ac_ref, ax_ref, ag_ref, bg_ref, hc_ref, hx_ref, o_ref, w_ref, b_ref, y_ref, ubuf):
        i = pl.program_id(0)
        u = ac_ref[...] * ax_ref[...]
        ubuf[0:SC_HALO, :] = jnp.where(i > 0, hc_ref[...] * hx_ref[...], 0.0)
        ubuf[SC_HALO:, :] = u
        conv = _sc_conv(u, ubuf, w_ref, b_ref, ts)
        y_ref[:, 0:w] = (ab_ref[...] * conv * _silu(ag_ref[...])).astype(y_ref.dtype)
        y_ref[:, w:] = (o_ref[...] * _silu(bg_ref[...])).astype(y_ref.dtype)

    return pl.pallas_call(
        body, name="even_gate_fwd", grid=(s // ts,),
        in_specs=[_rows(ts, w, 0), _rows(ts, w, 1), _rows(ts, w, 2), _rows(ts, w, 3), _rows(ts, w, 5),
                  _prev_halo(ts, SC_HALO, w, 1), _prev_halo(ts, SC_HALO, w, 2), _rows(ts, w),
                  _vec(w, 0, SC_KERNEL), _vec(w)],
        out_specs=_rows(ts, 2 * w), out_shape=_sds((s, 2 * w), MXU_DTYPE),
        scratch_shapes=[pltpu.VMEM((ts + SC_HALO, w), F32)],
        compiler_params=_cp(1))(z, z, z, z, z, z, z, o, sc_w, sc_b)


def _even_gate_bwd(dy, z, o, sc_w, sc_b, ts):
    s = z.shape[0]
    w = SC_WIDTH
    n = s // ts

    def body(dya_ref, dyb_ref, dyan_ref, ab_ref, ac_ref, ax_ref, ag_ref, bg_ref, hc_ref, hx_ref, abn_ref, agn_ref,
             o_ref, w_ref, b_ref, dz_ref, do_ref, dot_ref, dw_ref, db_ref, ubuf, dbuf):
        i = pl.program_id(0)
        ab, ac, ax, ag, bg = ab_ref[...], ac_ref[...], ax_ref[...], ag_ref[...], bg_ref[...]
        dya, dyb = dya_ref[...], dyb_ref[...]
        u = ac * ax
        ubuf[0:SC_HALO, :] = jnp.where(i > 0, hc_ref[...] * hx_ref[...], 0.0)
        ubuf[SC_HALO:, :] = u
        conv = _sc_conv(u, ubuf, w_ref, b_ref, ts)
        sg = _silu(ag)
        dconv = dya * ab * sg
        dbuf[0:ts, :] = dconv
        dbuf[ts:, :] = jnp.where(i < n - 1, dyan_ref[...] * abn_ref[...] * _silu(agn_ref[...]), 0.0)
        du = w_ref[2:3, :] * dconv + w_ref[1:2, :] * dbuf[pl.ds(1, ts), :] + w_ref[0:1, :] * dbuf[pl.ds(2, ts), :]
        dz_ref[:, 0:w] = (dya * conv * sg).astype(dz_ref.dtype)
        dz_ref[:, w:2 * w] = (du * ax).astype(dz_ref.dtype)
        dz_ref[:, 2 * w:3 * w] = (du * ac).astype(dz_ref.dtype)
        dz_ref[:, 3 * w:4 * w] = (dya * ab * conv * _dsilu(ag)).astype(dz_ref.dtype)
        dz_ref[:, 4 * w:5 * w] = jnp.zeros((ts, w), dz_ref.dtype)
        dz_ref[:, 5 * w:] = (dyb * o_ref[...] * _dsilu(bg)).astype(dz_ref.dtype)
        do = dyb * _silu(bg)
        do_ref[...] = do
        for pair in range(HEADS // 2):
            dot_ref[pair, 0] = do[:, pair * HEAD_PAD:(pair + 1) * HEAD_PAD].T.astype(dot_ref.dtype)

        @pl.when(i == 0)
        def _():
            dw_ref[...] = jnp.zeros_like(dw_ref)
            db_ref[...] = jnp.zeros_like(db_ref)

        dw_ref[0:1, :] += jnp.sum(dconv * ubuf[pl.ds(SC_HALO - 2, ts), :], axis=0, keepdims=True)
        dw_ref[1:2, :] += jnp.sum(dconv * ubuf[pl.ds(SC_HALO - 1, ts), :], axis=0, keepdims=True)
        dw_ref[2:3, :] += jnp.sum(dconv * u, axis=0, keepdims=True)
        db_ref[...] += jnp.sum(dconv, axis=0, keepdims=True)

    return pl.pallas_call(
        body, name="even_gate_bwd", grid=(n,),
        in_specs=[_rows(ts, w, 0), _rows(ts, w, 1), _next_halo(ts, SC_HALO, w, 0, s),
                  _rows(ts, w, 0), _rows(ts, w, 1), _rows(ts, w, 2), _rows(ts, w, 3), _rows(ts, w, 5),
                  _prev_halo(ts, SC_HALO, w, 1), _prev_halo(ts, SC_HALO, w, 2),
                  _next_halo(ts, SC_HALO, w, 0, s), _next_halo(ts, SC_HALO, w, 3, s),
                  _rows(ts, w), _vec(w, 0, SC_KERNEL), _vec(w)],
        out_specs=[_rows(ts, EVEN_PAD), _rows(ts, w), pl.BlockSpec((HEADS // 2, 1, HEAD_PAD, ts), lambda i: (0, i, 0, 0)),
                   _vec(w, 0, SC_KERNEL), _vec(w)],
        out_shape=[_sds((s, EVEN_PAD), MXU_DTYPE), _sds((s, w)), _sds((HEADS // 2, n, HEAD_PAD, ts), MXU_DTYPE),
                   _sds((SC_KERNEL, w)), _sds((1, w))],
        scratch_shapes=[pltpu.VMEM((ts + SC_HALO, w), F32), pltpu.VMEM((ts + SC_HALO, w), F32)],
        compiler_params=_cp(1))(dy, dy, dy, z, z, z, z, z, z, z, z, z, o, sc_w, sc_b)


def _ln_act(uc, sg, g, b):
    mu = jnp.mean(uc, axis=-1, keepdims=True)
    var = jnp.mean(jnp.square(uc - mu), axis=-1, keepdims=True)
    return _silu((uc - mu) * lax.rsqrt(var + EPS) * g + b) * _silu(sg)


def _shifted_copies(buf, shifted, rows):
    for b in range(1, SUBLANES):
        shifted[b - 1, 0:rows, :] = buf[pl.ds(b, rows), :]


def _rows_at(buf, shifted, start, n):
    a, b = divmod(start, SUBLANES)
    return buf[pl.ds(SUBLANES * a, n), :] if b == 0 else shifted[b - 1, pl.ds(SUBLANES * a, n), :]


def _odd_fwd(z, conv_w, conv_b, ln_g, ln_b, ts):
    s = z.shape[0]
    d = D_MODEL
    k = CONF_KERNEL

    def body(val_ref, glu_ref, sg_ref, hval_ref, hglu_ref, w_ref, b_ref, g_ref, beta_ref, y_ref, uc_ref, ubuf, ush):
        i = pl.program_id(0)
        ubuf[0:CONF_HALO, :] = jnp.where(i > 0, hval_ref[...] * _sigmoid(hglu_ref[...]), 0.0)
        ubuf[CONF_HALO:, :] = val_ref[...] * _sigmoid(glu_ref[...])
        _shifted_copies(ubuf, ush, ts + CONF_HALO - SUBLANES)
        for r0 in range(0, ts, CONV_ROWS):
            acc = jnp.broadcast_to(b_ref[...], (CONV_ROWS, d))
            for j in range(k):
                acc = acc + w_ref[j:j + 1, :] * _rows_at(ubuf, ush, r0 + CONF_HALO - (k - 1) + j, CONV_ROWS)
            uc_ref[r0:r0 + CONV_ROWS, :] = acc
        y_ref[...] = _ln_act(uc_ref[...], sg_ref[...], g_ref[...], beta_ref[...]).astype(y_ref.dtype)

    return pl.pallas_call(
        body, name="odd_fwd", grid=(s // ts,),
        in_specs=[_rows(ts, d, 0), _rows(ts, d, 1), _rows(ts, d, 2),
                  _prev_halo(ts, CONF_HALO, d, 0), _prev_halo(ts, CONF_HALO, d, 1),
                  _vec(d, 0, k), _vec(d), _vec(d), _vec(d)],
        out_specs=[_rows(ts, d), _rows(ts, d)], out_shape=[_sds((s, d), MXU_DTYPE), _sds((s, d))],
        scratch_shapes=[pltpu.VMEM((ts + CONF_HALO, d), F32),
                        pltpu.VMEM((SUBLANES - 1, ts + CONF_HALO - SUBLANES, d), F32)],
        compiler_params=_cp(1))(z, z, z, z, z, conv_w, conv_b, ln_g, ln_b)


def _odd_bwd(dy, z, uc, conv_w, ln_g, ln_b, ts):
    s = z.shape[0]
    d = D_MODEL
    k = CONF_KERNEL
    n = s // ts

    def body(dy_ref, dyn_ref, val_ref, glu_ref, sg_ref, sgn_ref, uc_ref, ucn_ref,
             w_ref, g_ref, beta_ref, dz_ref, dw_ref, db_ref, dg_ref, dbeta_ref, dbuf, dsh, dw_acc):
        i = pl.program_id(0)
        val, glu = val_ref[...], glu_ref[...]
        sig = _sigmoid(glu)
        u = val * sig
        _, vjp = jax.vjp(_ln_act, uc_ref[...], sg_ref[...], g_ref[...], beta_ref[...])
        duc, dsg, dg, dbeta = vjp(dy_ref[...])
        _, vjp_n = jax.vjp(_ln_act, ucn_ref[...], sgn_ref[...], g_ref[...], beta_ref[...])
        dbuf[0:ts, :] = duc
        dbuf[ts:, :] = jnp.where(i < n - 1, vjp_n(dyn_ref[...])[0], 0.0)
        dz_ref[:, 2 * d:] = dsg.astype(dz_ref.dtype)
        _shifted_copies(dbuf, dsh, ts + CONF_HALO - SUBLANES)

        @pl.when(i == 0)
        def _():
            dw_acc[...] = jnp.zeros_like(dw_acc)
            db_ref[...] = jnp.zeros_like(db_ref)
            dg_ref[...] = jnp.zeros_like(dg_ref)
            dbeta_ref[...] = jnp.zeros_like(dbeta_ref)

        db_ref[...] += jnp.sum(duc, axis=0, keepdims=True)
        dg_ref[...] += dg
        dbeta_ref[...] += dbeta
        for r0 in range(0, ts, CONV_ROWS):
            acc = jnp.zeros((CONV_ROWS, d), F32)
            for j in range(k):
                acc = acc + w_ref[j:j + 1, :] * _rows_at(dbuf, dsh, r0 + (k - 1) - j, CONV_ROWS)
            sig_r = sig[r0:r0 + CONV_ROWS, :]
            dz_ref[r0:r0 + CONV_ROWS, 0:d] = (acc * sig_r).astype(dz_ref.dtype)
            dz_ref[r0:r0 + CONV_ROWS, d:2 * d] = (acc * val[r0:r0 + CONV_ROWS, :] * sig_r * (1.0 - sig_r)).astype(dz_ref.dtype)
        for j in range(k):
            prod = _rows_at(dbuf, dsh, (k - 1) - j, ts) * u
            dw_acc[j] += jnp.sum(prod.reshape(ts // SUBLANES, SUBLANES, d), axis=0)

        @pl.when(i == n - 1)
        def _():
            dw_ref[...] = jnp.sum(dw_acc[...], axis=1)

    return pl.pallas_call(
        body, name="odd_bwd", grid=(n,),
        in_specs=[_rows(ts, d), _next_halo(ts, CONF_HALO, d, 0, s),
                  _rows(ts, d, 0), _rows(ts, d, 1), _rows(ts, d, 2), _next_halo(ts, CONF_HALO, d, 2, s),
                  _rows(ts, d), _next_halo(ts, CONF_HALO, d, 0, s),
                  _vec(d, 0, k), _vec(d), _vec(d)],
        out_specs=[_rows(ts, ODD_IN), _vec(d, 0, k), _vec(d), _vec(d), _vec(d)],
        out_shape=[_sds((s, ODD_IN), MXU_DTYPE), _sds((k, d)), _sds((1, d)), _sds((1, d)), _sds((1, d))],
        scratch_shapes=[pltpu.VMEM((ts + CONF_HALO, d), F32),
                        pltpu.VMEM((SUBLANES - 1, ts + CONF_HALO - SUBLANES, d), F32), pltpu.VMEM((k, SUBLANES, d), F32)],
        compiler_params=_cp(1))(dy, dy, z, z, z, z, uc, uc, conv_w, ln_g, ln_b)


def _local_step(x, target, cos, sin, mod, p, layer_weights, fwd_dep=None, grads_done=None, grads_early=None):
    s = x.shape[0]
    tsf, tsb = min(512, s // 2), min(256, s // 2)
    tq = min(512, s // 2)
    row1 = lambda a, i: a[i:i + 1]
    saved = []
    h = _pre_fwd(x, row1(p["pre_norm_g"], 0), row1(mod, 0), tsf, fwd_dep)
    for layer in range(DEPTH):
        i = layer // 2
        mod_l = row1(mod, layer)
        wl = layer_weights(layer, h)
        if layer % 2 == 0:
            z = _mm(h, wl["w_in"], "nn", F32, 512, EVEN_PAD, "even_in_fwd")
            if "late" in wl:
                wl.update(wl.pop("late")(z))
            q, q_t, k, v = _mla_prep_fwd(z, cos, sin, row1(p["even_q_norm_g"], i), row1(p["even_kv_norm_g"], i),
                                    wl["wq"], wl["wq_rot"], wl["wuk"], wl["wuv"], tsf)
            o, lse = _attn_fwd(q, k, v, tq)
            y = _even_gate_fwd(z, o, wl["sc_conv_w"], row1(p["even_sc_conv_b"], i), tsf)
            yo = _mm(y, wl["w_out"], "nn", F32, 512, 1024, "even_out_fwd")
            saved.append((x, h, z, y, yo, wl, (q, q_t, k, v, o, lse)))
        else:
            z = _mm(h, wl["w_in"], "nn", F32, 512, ODD_IN, "odd_in_fwd")
            y, uc = _odd_fwd(z, wl["conv_w"], wl["conv_b"], wl["ln_g"], wl["ln_b"], tsf)
            yo = _mm(y, wl["w_out"], "nn", F32, 512, 1024, "odd_out_fwd")
            saved.append((x, h, z, y, yo, wl, uc))
        if layer + 1 < DEPTH:
            x, h = _post_pre_fwd(x, yo, row1(p["post_norm_g"], layer), mod_l, row1(p["pre_norm_g"], layer + 1),
                                 row1(mod, layer + 1), tsf)
        else:
            loss, dx = _post_loss(x, yo, row1(p["post_norm_g"], layer), mod_l, target, tsf)

    g = {n: [None] * (DEPTH if n in ("pre_norm_g", "post_norm_g") else N_PAIRS) for n in (
        "pre_norm_g", "post_norm_g", "even_sc_conv_w", "even_sc_conv_b", "even_q_norm_g", "even_kv_norm_g",
        "odd_conv_w", "odd_conv_b", "odd_ln_g", "odd_ln_b")}
    dmod = [None] * DEPTH
    dep = None
    for layer in reversed(range(DEPTH)):
        i = layer // 2
        mod_l = row1(mod, layer)
        x_in, h, z, y, yo, wl, extra = saved[layer]
        dyo, dgate, g["post_norm_g"][layer] = _post_bwd(dx, yo, row1(p["post_norm_g"], layer), mod_l, tsb, dep)
        bufs = {}
        if layer % 2 == 0:
            q, q_t, k, v, o, lse = extra
            dy = _mm(dyo, wl["w_out"], "nt", F32, 512, 1024, "even_out_bwd_x")
            bufs["even_w_out"] = _mm_tn_shards(y, dyo, "rows", "even_out_bwd_w")
            dz, do, do_t, g["even_sc_conv_w"][i], g["even_sc_conv_b"][i] = _even_gate_bwd(
                dy, z, o, wl["sc_conv_w"], row1(p["even_sc_conv_b"], i), tsb)
            dq, dk, dv = _attn_bwd(q, q_t, k, v, do, do_t, o, lse, tq)
            dz, bufs["even_mla"], g["even_q_norm_g"][i], g["even_kv_norm_g"][i] = _mla_prep_bwd(
                dz, dq, dk, dv, z, cos, sin, row1(p["even_q_norm_g"], i), row1(p["even_kv_norm_g"], i),
                wl["wq"], wl["wuk"], wl["wuv"], tsb)
            early = grads_early(layer, bufs) if grads_early is not None else None
            bufs["even_w_in"] = _ein_to_shards(_mm(h, dz, "tn", F32, D_MODEL, 512, "even_in_bwd_w"))
        else:
            uc = extra
            dy = _mm(dyo, wl["w_out"], "nt", F32, 512, 1024, "odd_out_bwd_x")
            bufs["odd_w_out"] = _mm_tn_shards(y, dyo, "rows", "odd_out_bwd_w")
            dz, g["odd_conv_w"][i], g["odd_conv_b"][i], g["odd_ln_g"][i], g["odd_ln_b"][i] = _odd_bwd(
                dy, z, uc, wl["conv_w"], wl["ln_g"], wl["ln_b"], tsb)
            bufs["odd_w_in"] = _mm_tn_shards(h, dz, "cols", "odd_in_bwd_w")
            early = None
        dx, dshift, dscale, g["pre_norm_g"][layer] = _pre_bwd(
            dz, wl["w_in"], dx, x_in, row1(p["pre_norm_g"], layer), mod_l, tsf, early)
        dmod[layer] = jnp.concatenate([dshift, dscale, dgate], axis=-1)
        dep = grads_done(layer, bufs, dx) if grads_done is not None else None
    stack = lambda parts: jnp.stack([a[0] if a.shape[0] == 1 and a.ndim == 2 else a for a in parts])
    small = {n: stack(parts) for n, parts in g.items()}
    small["dmod"] = jnp.concatenate(dmod, axis=0)
    return loss, dx, small


def _uq_to_heads(w):
    w = w.reshape(N_CHIPS, Q_LORA, 2, QK_NOPE + QK_ROPE).transpose(0, 2, 1, 3).reshape(HEADS, Q_LORA, QK_NOPE + QK_ROPE)
    half = QK_ROPE // 2
    rotated = jnp.concatenate([jnp.zeros_like(w[..., :QK_NOPE]), -w[..., QK_NOPE + half:], w[..., QK_NOPE:QK_NOPE + half]],
                              axis=-1)
    pad = ((0, 0), (0, 0), (0, HEAD_PAD - QK_NOPE - QK_ROPE))
    return _side_by_side(jnp.pad(w, pad)), _side_by_side(jnp.pad(rotated, pad))


def _side_by_side(w):
    return w.transpose(1, 0, 2).reshape(w.shape[1], HEADS * HEAD_PAD)


def _ukv_to_heads(w):
    w = w.reshape(N_CHIPS, KV_LORA, 2, QK_NOPE + V_HEAD).transpose(0, 2, 1, 3).reshape(HEADS, KV_LORA, QK_NOPE + V_HEAD)
    wk = jnp.pad(w[..., :QK_NOPE], ((0, 0), (0, 0), (0, HEAD_PAD - QK_NOPE)))
    wv = w[..., QK_NOPE:]
    zero = jnp.zeros_like(wv)
    odd = (jnp.arange(HEADS) % 2 == 1)[:, None, None]
    wv = jnp.concatenate([jnp.where(odd, zero, wv), jnp.where(odd, wv, zero)], axis=-1)
    return _side_by_side(wk), _side_by_side(wv)


def _mla_local(q):
    blocks = q.reshape(2, MLA_ROWS, HEAD_PAD)
    uq = jnp.concatenate([blocks[r, :Q_LORA, :QK_NOPE + QK_ROPE] for r in range(2)], axis=-1)
    ukv = jnp.concatenate(
        [jnp.concatenate([blocks[r, Q_LORA:Q_LORA + KV_LORA, :QK_NOPE],
                          blocks[r, Q_LORA + KV_LORA:, V_HEAD * r:V_HEAD * (r + 1)]], axis=-1) for r in range(2)], axis=-1)
    return uq, ukv


def _place():
    return lax.axis_index("x"), lax.axis_index("y"), lax.axis_index("c")


def _flip(v, bit):
    return 1 - v if bit else v


def _sem(a, k):
    return a * (N_CHIPS - 1) + k - 1


def _remote(src, dst, send_sem, recv_sem, peer):
    return pltpu.make_async_remote_copy(src_ref=src, dst_ref=dst, send_sem=send_sem, recv_sem=recv_sem,
                                        device_id=peer, device_id_type=MESH)


_VMEM_SPEC = pl.BlockSpec(memory_space=pltpu.VMEM)
_HBM_SPEC = pl.BlockSpec(memory_space=pl.ANY)


def _ada_fwd(c8, ada_w, ada_b_sh):
    depth, d, cols = ada_w.shape

    def body(c_ref, w_ref, b_ref, call_ref, mod_ref, s1, r1, s2, r2):
        x, y, c = _place()
        chip = 2 * x + y
        me = 2 * chip + c
        call_ref[me] = c_ref[...]
        sends = []
        for k in range(1, N_DEV):
            peer = (_flip(x, k & 4), _flip(y, k & 2), _flip(c, k & 1))
            cp = _remote(c_ref, call_ref.at[me], s1.at[k - 1], r1.at[k - 1], peer)
            cp.start()
            sends.append(cp)
        for k in range(1, N_DEV):
            src = 4 * _flip(x, k & 4) + 2 * _flip(y, k & 2) + _flip(c, k & 1)
            _remote(c_ref, call_ref.at[src], s1.at[k - 1], r1.at[k - 1], (x, y, c)).wait_recv()
        act = _silu(jnp.concatenate([call_ref[e, 0:1, :] for e in range(N_DEV)], axis=0))
        for l in range(depth):
            mod_ref[chip, l] = _dot_nn(act, w_ref[l]) + b_ref[l:l + 1, :]
        for k in range(1, N_CHIPS):
            peer = (_flip(x, k & 2), _flip(y, k & 1), c)
            cp = _remote(mod_ref.at[chip], mod_ref.at[chip], s2.at[k - 1], r2.at[k - 1], peer)
            cp.start()
            sends.append(cp)
        for k in range(1, N_CHIPS):
            src = 2 * _flip(x, k & 2) + _flip(y, k & 1)
            _remote(mod_ref.at[src], mod_ref.at[src], s2.at[k - 1], r2.at[k - 1], (x, y, c)).wait_recv()
        for cp in sends:
            cp.wait_send()

    return pl.pallas_call(
        body, name="ada_fwd", in_specs=[_VMEM_SPEC] * 3, out_specs=[_VMEM_SPEC] * 2,
        out_shape=[_sds((N_DEV, 8, d)), _sds((N_CHIPS, depth, N_DEV, cols))],
        scratch_shapes=[pltpu.SemaphoreType.DMA((N_DEV - 1,)), pltpu.SemaphoreType.DMA((N_DEV - 1,)),
                        pltpu.SemaphoreType.DMA((N_CHIPS - 1,)), pltpu.SemaphoreType.DMA((N_CHIPS - 1,))],
        compiler_params=pltpu.CompilerParams(vmem_limit_bytes=VMEM_LIMIT_V7X))(c8, ada_w, ada_b_sh)


def _ada_bwd(c_t, dmod_sh):
    depth, n, cols = dmod_sh.shape
    d = c_t.shape[0]
    tr = 256

    def body(c_ref, dm_ref, o_ref):
        act = _silu(c_ref[...])
        acc = act[:, 0:1] * dm_ref[0, 0:1, :]
        for e in range(1, n):
            acc = acc + act[:, e:e + 1] * dm_ref[0, e:e + 1, :]
        o_ref[0] = acc

    return pl.pallas_call(
        body, name="ada_bwd", grid=(depth, d // tr),
        in_specs=[pl.BlockSpec((tr, n), lambda l, i: (i, 0)), pl.BlockSpec((1, n, cols), lambda l, i: (l, 0, 0))],
        out_specs=pl.BlockSpec((1, tr, cols), lambda l, i: (l, i, 0)), out_shape=_sds((depth, d, cols)),
        compiler_params=_cp(2))(c_t, dmod_sh)


def _gathered_shape(shape, how):
    if how == "slot":
        return (N_CHIPS,) + shape
    r, cc = shape
    return (r, N_CHIPS * cc) if how == "cols" else (N_CHIPS * r, cc)


def _gathered_part(ref, shape, how, chip):
    if how == "slot":
        return ref.at[chip]
    if how == "cols":
        return ref.at[:, pl.ds(pl.multiple_of(chip * shape[1], 128), shape[1])]
    return ref.at[pl.ds(pl.multiple_of(chip * shape[0], 8), shape[0]), :]


_SEM_SPEC = pl.BlockSpec(memory_space=pltpu.SEMAPHORE)
_TOKEN = jax.ShapeDtypeStruct((8, 128), F32)
_SPLIT_COPY = pltpu.CompilerParams(has_side_effects=pltpu.SideEffectType.DATAFLOW_SIDE_EFFECTING)


def _in_hbm(a):
    return pltpu.with_memory_space_constraint(a, pltpu.HBM)


def _gather_start(items, gathered, name, after=()):
    n = len(items)

    def body(*refs):
        ins, outs = refs[:n], refs[n:2 * n]
        send_sems, recv_sems = refs[2 * n + len(after)], refs[2 * n + len(after) + 1]
        x, y, c = _place()
        for a in range(n):
            for k in range(1, N_CHIPS):
                part = _gathered_part(outs[a], items[a][0].shape, items[a][1], 2 * x + y)
                _remote(ins[a], part, send_sems.at[_sem(a, k)], recv_sems.at[_sem(a, k)],
                        (_flip(x, k & 2), _flip(y, k & 1), c)).start()
        refs[-1][...] = jnp.zeros(_TOKEN.shape, _TOKEN.dtype)

    arrays = [_in_hbm(a) for a, _ in items] + [_in_hbm(a) for a in gathered]
    res = pl.pallas_call(
        body, name=name, in_specs=[_HBM_SPEC] * (2 * n + len(after)),
        out_specs=[_SEM_SPEC, _SEM_SPEC] + [_HBM_SPEC] * (2 * n) + [_VMEM_SPEC],
        out_shape=[pltpu.SemaphoreType.DMA((n * (N_CHIPS - 1),)), pltpu.SemaphoreType.DMA((n * (N_CHIPS - 1),))]
        + [pltpu.HBM(a.shape, a.dtype) for a in arrays] + [_TOKEN],
        input_output_aliases={a: 2 + a for a in range(2 * n)}, compiler_params=_SPLIT_COPY)(*arrays, *after)
    return res[0], res[1], res[2:2 + n], res[2 + n:2 + 2 * n], res[-1]


def _gather_wait(items, started, after, name):
    n = len(items)
    send_sems, recv_sems, shards, gathered, _ = started

    def body(*refs):
        ins, outs, send_sems, recv_sems = refs[:n], refs[n:2 * n], refs[2 * n], refs[2 * n + 1]
        x, y, c = _place()
        for a in range(n):
            for k in range(1, N_CHIPS):
                part = _gathered_part(outs[a], items[a][0].shape, items[a][1], 2 * _flip(x, k & 2) + _flip(y, k & 1))
                cp = _remote(ins[a], part, send_sems.at[_sem(a, k)], recv_sems.at[_sem(a, k)], (x, y, c))
                cp.wait_send()
                cp.wait_recv()

    res = pl.pallas_call(
        body, name=name, in_specs=[_HBM_SPEC] * (2 * n) + [_SEM_SPEC, _SEM_SPEC] + [_HBM_SPEC] * len(after),
        out_specs=[_HBM_SPEC] * (2 * n), out_shape=[pltpu.HBM(a.shape, a.dtype) for a in (*shards, *gathered)],
        input_output_aliases={a: a for a in range(2 * n)}, compiler_params=_SPLIT_COPY)(
            *shards, *gathered, send_sems, recv_sems, *after)
    return res[n:]


def _rs_start(bufs, name, after=()):
    n = len(bufs)

    def body(*refs):
        srcs, lands = refs[:n], refs[n:2 * n]
        send_sems, recv_sems = refs[2 * n + len(after)], refs[2 * n + len(after) + 1]
        x, y, c = _place()
        for a in range(n):
            for k in range(1, N_CHIPS):
                tx, ty = _flip(x, k & 2), _flip(y, k & 1)
                _remote(srcs[a].at[2 * tx + ty], lands[a].at[k - 1], send_sems.at[_sem(a, k)], recv_sems.at[_sem(a, k)],
                        (tx, ty, c)).start()
        refs[-1][...] = jnp.zeros(_TOKEN.shape, _TOKEN.dtype)

    arrays = [_in_hbm(b) for b in bufs] + [_in_hbm(lax.empty((N_CHIPS - 1,) + b.shape[1:], b.dtype)) for b in bufs]
    res = pl.pallas_call(
        body, name=name, in_specs=[_HBM_SPEC] * (2 * n + len(after)),
        out_specs=[_SEM_SPEC, _SEM_SPEC] + [_HBM_SPEC] * (2 * n) + [_VMEM_SPEC],
        out_shape=[pltpu.SemaphoreType.DMA((n * (N_CHIPS - 1),)), pltpu.SemaphoreType.DMA((n * (N_CHIPS - 1),))]
        + [pltpu.HBM(a.shape, a.dtype) for a in arrays] + [_TOKEN],
        input_output_aliases={a: 2 + a for a in range(2 * n)}, compiler_params=_SPLIT_COPY)(*arrays, *after)
    return res[0], res[1], res[2:2 + n], res[2 + n:2 + 2 * n], res[-1]


def _rs_wait(started, after, name):
    send_sems, recv_sems, bufs, lands, _ = started
    n = len(bufs)

    def body(*refs):
        srcs, lnds, send_sems, recv_sems = refs[:n], refs[n:2 * n], refs[2 * n], refs[2 * n + 1]
        x, y, c = _place()
        for a in range(n):
            for k in range(1, N_CHIPS):
                cp = _remote(srcs[a].at[0], lnds[a].at[k - 1], send_sems.at[_sem(a, k)], recv_sems.at[_sem(a, k)], (x, y, c))
                cp.wait_send()
                cp.wait_recv()

    res = pl.pallas_call(
        body, name=name, in_specs=[_HBM_SPEC] * (2 * n) + [_SEM_SPEC, _SEM_SPEC] + [_HBM_SPEC] * len(after),
        out_specs=[_HBM_SPEC] * (2 * n), out_shape=[pltpu.HBM(a.shape, a.dtype) for a in (*bufs, *lands)],
        input_output_aliases={a: a for a in range(2 * n)}, compiler_params=_SPLIT_COPY)(
            *bufs, *lands, send_sems, recv_sems, *after)
    return res[:n], res[n:]


def _place_own(shard, how, chip_idx):
    r, cc = shard.shape
    block, index = {"slot": ((1, r, cc), lambda i, c: (c[0], 0, 0)), "cols": ((r, cc), lambda i, c: (0, c[0])),
                    "rows": ((r, cc), lambda i, c: (c[0], 0))}[how]

    def body(c_ref, in_ref, o_ref):
        del c_ref
        o_ref[...] = in_ref[...].reshape(o_ref.shape)

    return pl.pallas_call(
        body, name="place_own", out_shape=_sds(_gathered_shape(shard.shape, how), shard.dtype),
        grid_spec=pltpu.PrefetchScalarGridSpec(
            num_scalar_prefetch=1, grid=(1,), in_specs=[pl.BlockSpec((r, cc), lambda i, c: (0, 0))],
            out_specs=pl.BlockSpec(block, index)),
        compiler_params=_cp(1))(chip_idx, shard)


def _gather_sum_all(small):
    r, w = small.shape

    def body(in_ref, all_ref, sum_ref, send_sems, recv_sems):
        x, y, c = _place()
        me = 4 * x + 2 * y + c
        all_ref[me] = in_ref[...]
        sends = []
        for k in range(1, N_DEV):
            peer = (_flip(x, k & 4), _flip(y, k & 2), _flip(c, k & 1))
            cp = _remote(in_ref, all_ref.at[me], send_sems.at[k - 1], recv_sems.at[k - 1], peer)
            cp.start()
            sends.append(cp)
        for k in range(1, N_DEV):
            src = 4 * _flip(x, k & 4) + 2 * _flip(y, k & 2) + _flip(c, k & 1)
            _remote(in_ref, all_ref.at[src], send_sems.at[k - 1], recv_sems.at[k - 1], (x, y, c)).wait_recv()
        acc = all_ref[0]
        for e in range(1, N_DEV):
            acc = acc + all_ref[e]
        sum_ref[...] = acc
        for cp in sends:
            cp.wait_send()

    return pl.pallas_call(
        body, name="gather_sum_all", in_specs=[_VMEM_SPEC], out_specs=[_VMEM_SPEC] * 2,
        out_shape=[_sds((N_DEV, r, w)), _sds((r, w))],
        scratch_shapes=[pltpu.SemaphoreType.DMA((N_DEV - 1,)), pltpu.SemaphoreType.DMA((N_DEV - 1,))],
        compiler_params=pltpu.CompilerParams(vmem_limit_bytes=VMEM_LIMIT_V7X))(small)


def _add_chips(buf, t, chip_idx):
    r, cc = buf.shape[1:]
    tr = min(256, r)

    def body(c_ref, p_ref, t_ref, o_ref):
        del c_ref
        o_ref[...] = p_ref[0] + t_ref[0].astype(F32) + t_ref[1].astype(F32) + t_ref[2].astype(F32)

    return pl.pallas_call(
        body, name="add_chips", out_shape=_sds((r, cc)),
        grid_spec=pltpu.PrefetchScalarGridSpec(
            num_scalar_prefetch=1, grid=(r // tr,),
            in_specs=[pl.BlockSpec((1, tr, cc), lambda i, c: (c[0], i, 0)),
                      pl.BlockSpec((N_CHIPS - 1, tr, cc), lambda i, c: (0, i, 0))],
            out_specs=pl.BlockSpec((tr, cc), lambda i, c: (i, 0))),
        compiler_params=_cp(1))(chip_idx, buf, t)


def _rs_sibling(qs):
    n = len(qs)

    def body(*refs):
        ins, outs = refs[:n], refs[n:2 * n]
        send_sems, recv_sems = refs[2 * n:]
        x, y, c = _place()
        copies = [_remote(ins[a], outs[a], send_sems.at[a], recv_sems.at[a], (x, y, 1 - c)) for a in range(n)]
        for cp in copies:
            cp.start()
        for cp in copies:
            cp.wait()

    return pl.pallas_call(
        body, name="rs_sibling", in_specs=[_HBM_SPEC] * n, out_specs=[_HBM_SPEC] * n,
        out_shape=[_sds(q.shape) for q in qs],
        scratch_shapes=[pltpu.SemaphoreType.DMA((n,)), pltpu.SemaphoreType.DMA((n,))])(*qs)


def _adamw_update(w, g, m, v):
    m = ADAM_B1 * m + (1.0 - ADAM_B1) * g
    v = ADAM_B2 * v + (1.0 - ADAM_B2) * jnp.square(g)
    m_hat = m / (1.0 - ADAM_B1 ** ADAM_STEP)
    v_hat = v / (1.0 - ADAM_B2 ** ADAM_STEP)
    return -ADAM_LR * (m_hat / (jnp.sqrt(v_hat) + ADAM_EPS) + ADAM_WD * w), m, v


def _adamw(w, g_parts, m, v, name):
    shape = w.shape
    cols = shape[-1]
    rows = _size(shape[:-1])
    tr = 512 if rows % 512 == 0 else rows
    spec = pl.BlockSpec((tr, cols), lambda i: (i, 0))
    n = len(g_parts)
    n_out = 4 if n > 1 else 3

    def body(*refs):
        w_ref, m_ref, v_ref = refs[:3]
        d_ref, nm_ref, nv_ref = refs[-3:]
        g = refs[3][...]
        for r in refs[4:3 + n]:
            g = g + r[...]
        if n > 1:
            refs[3 + n][...] = g
        d_ref[...], nm_ref[...], nv_ref[...] = _adamw_update(w_ref[...], g, m_ref[...], v_ref[...])

    outs = pl.pallas_call(
        body, name="adamw_" + name, grid=(rows // tr,), in_specs=[spec] * (3 + n), out_specs=[spec] * n_out,
        out_shape=[_sds((rows, cols))] * n_out, compiler_params=_cp(1))(
            *[a.reshape(rows, cols) for a in (w, m, v, *g_parts)])
    outs = tuple(o.reshape(shape) for o in outs)
    return outs if n > 1 else (g_parts[0],) + outs


def _adamw_layer(w, g_parts, m, v, layer, prev, name):
    _, r, cc = w.shape
    tr = 512 if r % 512 == 0 else r
    spec = pl.BlockSpec((1, tr, cc), lambda i: (layer, i, 0))
    n = len(g_parts)

    def body(*refs):
        w_ref, m_ref, v_ref = refs[:3]
        g_ref, d_ref, nm_ref, nv_ref = refs[-4:]
        g = refs[3][...]
        for q in refs[4:3 + n]:
            g = g + q[...]
        g = g[:, :cc]
        g_ref[0] = g
        d_ref[0], nm_ref[0], nv_ref[0] = _adamw_update(w_ref[0], g, m_ref[0], v_ref[0])

    g_specs = [pl.BlockSpec((tr, q.shape[1]), lambda i: (i, 0)) for q in g_parts]
    passed = () if prev is None else tuple(prev)
    return pl.pallas_call(
        body, name="adamw_" + name, grid=(r // tr,),
        in_specs=[spec] * 3 + g_specs + [_HBM_SPEC] * len(passed), out_specs=[spec] * 4,
        out_shape=[_sds(w.shape)] * 4, input_output_aliases={3 + n + k: k for k in range(len(passed))},
        compiler_params=_cp(1))(w, m, v, *g_parts, *passed)


def _size(shape):
    n = 1
    for s in shape:
        n *= s
    return n


_SMALL = (("dmod", (DEPTH, 3 * D_MODEL)), ("pre_norm_g", (DEPTH, D_MODEL)), ("post_norm_g", (DEPTH, D_MODEL)),
          ("even_sc_conv_w", (2, SC_KERNEL, SC_WIDTH)), ("even_sc_conv_b", (2, SC_WIDTH)),
          ("even_q_norm_g", (2, Q_LORA)), ("even_kv_norm_g", (2, KV_LORA)),
          ("odd_conv_w", (2, CONF_KERNEL, D_MODEL)), ("odd_conv_b", (2, D_MODEL)), ("odd_ln_g", (2, D_MODEL)),
          ("odd_ln_b", (2, D_MODEL)))
SMALL_ROWS = -(-sum(_size(s) for _, s in _SMALL) // (8 * 128)) * 8

_SMALL_W = (("even_sc_conv_w", (2, SC_KERNEL, SC_WIDTH // N_CHIPS)), ("odd_conv_w", (2, CONF_KERNEL, D_MODEL // N_CHIPS)),
            ("odd_conv_b", (2, D_MODEL // N_CHIPS)), ("odd_ln_g", (2, D_MODEL // N_CHIPS)),
            ("odd_ln_b", (2, D_MODEL // N_CHIPS)))
SMALL_W_ROWS = -(-sum(_size(s) for _, s in _SMALL_W) // (8 * 128)) * 8


def _pack_rows(arrays, layout, rows):
    flat = jnp.concatenate([arrays[n].reshape(-1) for n, _ in layout])
    return jnp.pad(flat, (0, rows * 128 - flat.shape[0])).reshape(rows, 128)


def _unpack_small(t):
    flat = t.reshape(-1)
    out, at = {}, 0
    for n, shape in _SMALL:
        out[n] = flat[at:at + _size(shape)].reshape(shape)
        at += _size(shape)
    return out


def _unpack_small_w(t):
    flat = t.reshape(N_CHIPS, -1)
    out, at = {}, 0
    for n, shape in _SMALL_W:
        a = flat[:, at:at + _size(shape)].reshape((N_CHIPS,) + shape)
        out[n] = jnp.moveaxis(a, 0, -2).reshape(shape[:-1] + (N_CHIPS * shape[-1],))
        at += _size(shape)
    return out


def _chip_cols(a, chip):
    n = a.shape[-1] // N_CHIPS
    return lax.dynamic_slice_in_dim(a, chip * n, n, axis=a.ndim - 1)


WEIGHT_NAMES = ("ada_w", "ada_b", "pre_norm_g", "post_norm_g", "even_w_in", "even_sc_conv_w", "even_sc_conv_b",
                "even_q_norm_g", "even_kv_norm_g", "even_w_uq", "even_w_ukv", "even_w_out", "odd_w_in", "odd_conv_w",
                "odd_conv_b", "odd_ln_g", "odd_ln_b", "odd_w_out")
GATHER_HOW = ((("even_w_in", "slot"), ("even_w_uq", "slot"), ("even_w_ukv", "slot"), ("even_w_out", "rows")),
              (("odd_w_in", "cols"), ("odd_w_out", "rows")))


def kernel(x, c, positions, ada_w, ada_b, pre_norm_g, post_norm_g, even_w_in, even_sc_conv_w, even_sc_conv_b, even_q_norm_g, even_kv_norm_g, even_w_uq, even_w_ukv, even_w_out, odd_w_in, odd_conv_w, odd_conv_b, odd_ln_g, odd_ln_b, odd_w_out, loss_target, m_ada_w, m_ada_b, m_pre_norm_g, m_post_norm_g, m_even_w_in, m_even_sc_conv_w, m_even_sc_conv_b, m_even_q_norm_g, m_even_kv_norm_g, m_even_w_uq, m_even_w_ukv, m_even_w_out, m_odd_w_in, m_odd_conv_w, m_odd_conv_b, m_odd_ln_g, m_odd_ln_b, m_odd_w_out, v_ada_w, v_ada_b, v_pre_norm_g, v_post_norm_g, v_even_w_in, v_even_sc_conv_w, v_even_sc_conv_b, v_even_q_norm_g, v_even_kv_norm_g, v_even_w_uq, v_even_w_ukv, v_even_w_out, v_odd_w_in, v_odd_conv_w, v_odd_conv_b, v_odd_ln_g, v_odd_ln_b, v_odd_w_out):
    w = dict(zip(WEIGHT_NAMES, (ada_w, ada_b, pre_norm_g, post_norm_g, even_w_in, even_sc_conv_w, even_sc_conv_b,
                                even_q_norm_g, even_kv_norm_g, even_w_uq, even_w_ukv, even_w_out, odd_w_in, odd_conv_w,
                                odd_conv_b, odd_ln_g, odd_ln_b, odd_w_out)))
    m = dict(zip(WEIGHT_NAMES, (m_ada_w, m_ada_b, m_pre_norm_g, m_post_norm_g, m_even_w_in, m_even_sc_conv_w,
                                m_even_sc_conv_b, m_even_q_norm_g, m_even_kv_norm_g, m_even_w_uq, m_even_w_ukv,
                                m_even_w_out, m_odd_w_in, m_odd_conv_w, m_odd_conv_b, m_odd_ln_g, m_odd_ln_b, m_odd_w_out)))
    v = dict(zip(WEIGHT_NAMES, (v_ada_w, v_ada_b, v_pre_norm_g, v_post_norm_g, v_even_w_in, v_even_sc_conv_w,
                                v_even_sc_conv_b, v_even_q_norm_g, v_even_kv_norm_g, v_even_w_uq, v_even_w_ukv,
                                v_even_w_out, v_odd_w_in, v_odd_conv_w, v_odd_conv_b, v_odd_ln_g, v_odd_ln_b, v_odd_w_out)))
    ix, iy, ic = _place()
    chip = 2 * ix + iy
    me = 2 * chip + ic
    s = x.shape[1]

    c_all, mod_all = _ada_fwd(jnp.broadcast_to(c, (8, D_MODEL)), ada_w, _chip_cols(ada_b, chip))
    mod = lax.dynamic_index_in_dim(mod_all, me, axis=2, keepdims=False)
    mod = mod.transpose(1, 0, 2).reshape(DEPTH, 3 * D_MODEL)

    items = [[(w[n][layer // 2].astype(MXU_DTYPE), how) for n, how in GATHER_HOW[layer % 2]] for layer in range(DEPTH)]
    groups = [items[0][:1], items[0][1:] + [(_pack_rows(w, _SMALL_W, SMALL_W_ROWS), "slot")],
              [item for layer_items in items[1:] for item in layer_items]]
    sent, dep = [], mod_all
    for number, group in enumerate(groups):
        sent.append(_gather_start(group, [_place_own(a, how, chip.reshape(1)) for a, how in group],
                                  "gather_start_%d" % number, [dep]))
        dep = sent[-1][-1]
    arrived = {}

    def group(number, after):
        if number not in arrived:
            arrived[number] = _gather_wait(groups[number], sent[number], after, "gather_wait_%d" % number)
        return arrived[number]

    def even_rest(i, uq, ukv, eout, small_w):
        wuk, wuv = _ukv_to_heads(ukv)
        wq, wq_rot = _uq_to_heads(uq)
        return {"wq": wq, "wq_rot": wq_rot, "wuk": wuk, "wuv": wuv, "w_out": eout, "sc_conv_w": small_w["even_sc_conv_w"][i]}

    def layer_weights(layer, h):
        i = layer // 2
        if layer == 0:
            def late(z):
                uq, ukv, eout, small = group(1, [z])
                return even_rest(i, uq, ukv, eout, _unpack_small_w(small))
            return {"w_in": _ein_from_shards(group(0, [h])[0]), "late": late}
        small_w = _unpack_small_w(group(1, [h])[-1])
        at = sum(len(layer_items) for layer_items in items[1:layer])
        arrays = group(2, [h])[at:at + len(items[layer])]
        if layer % 2 == 0:
            return {"w_in": _ein_from_shards(arrays[0]), **even_rest(i, *arrays[1:], small_w)}
        oin, oout = arrays
        return {"w_in": oin, "w_out": oout, "conv_w": small_w["odd_conv_w"][i], "conv_b": small_w["odd_conv_b"][i:i + 1],
                "ln_g": small_w["odd_ln_g"][i:i + 1], "ln_b": small_w["odd_ln_b"][i:i + 1]}

    in_flight, own, sib, last = {}, {}, {}, {}

    def send(layer, bufs, name, after=()):
        names = sorted(bufs)
        kept = [bufs[n] for n in names] if layer == 0 else None
        started = _rs_start([bufs[n].astype(jnp.bfloat16) if layer == 0 else bufs[n] for n in names], name, after)
        in_flight.setdefault(layer, []).append((names, started, kept, name.replace("start", "wait")))
        return started[-1]

    def land(layer, after):
        names, sums = [], []
        for part, started, kept, name in in_flight.pop(layer):
            bufs, arrived = _rs_wait(started, after, name)
            names += part
            sums += [_add_chips(b, t, chip.reshape(1)) for b, t in zip(bufs if kept is None else kept, arrived)]
        for n, mine, theirs in zip(names, sums, _rs_sibling(sums)):
            own[n, layer // 2], sib[n, layer // 2] = mine, theirs

    def grads_early(layer, bufs):
        if layer != 0:
            return None
        return send(0, {n: bufs.pop(n) for n in sorted(bufs)}, "rs_start_0_early")

    def grads_done(layer, bufs, dx_in):
        if layer + 1 in in_flight:
            land(layer + 1, [dx_in])
        if layer == 0:
            last.update(bufs)
            return None
        return send(layer, bufs, "rs_start_%d" % layer)

    p = {"pre_norm_g": pre_norm_g, "post_norm_g": post_norm_g, "even_sc_conv_b": even_sc_conv_b,
         "even_q_norm_g": even_q_norm_g, "even_kv_norm_g": even_kv_norm_g}
    inv_freq = 1.0 / (ROPE_THETA ** (jnp.arange(0, QK_ROPE, 2, dtype=F32) / QK_ROPE))
    inv_freq = jnp.zeros((1, HEAD_PAD), F32).at[0, QK_NOPE:QK_NOPE + QK_ROPE].set(jnp.tile(inv_freq, 2))
    cos, sin = _rope_tables(positions.reshape(s, 1), inv_freq)

    loss, dx, g = _local_step(x[0], loss_target[0], cos, sin, mod, p, layer_weights, dep, grads_done, grads_early)

    grads, deltas, new_m, new_v = {}, {}, {}, {}

    def update_layers(n, results, pairs):
        for i in pairs:
            results = _adamw_layer(w[n], [own[n, i], sib[n, i]], m[n], v[n], i, results, n)
        return results

    small_all, small_sum = _gather_sum_all(_pack_rows(g, _SMALL, SMALL_ROWS))
    send(0, last, "rs_start_0", [small_sum])
    tot = _unpack_small(small_sum)
    dmod_all = small_all[:, :DEPTH * 3 * D_MODEL // 128].reshape(N_DEV, DEPTH, 3 * D_MODEL)
    grads["ada_w"] = _ada_bwd(c_all[:, 0, :].T, _chip_cols(dmod_all, chip).transpose(1, 0, 2))
    grads["ada_b"] = tot["dmod"]
    for n in ("pre_norm_g", "post_norm_g", "even_sc_conv_b", "even_q_norm_g", "even_kv_norm_g"):
        grads[n] = tot[n]
    for n in ("even_sc_conv_w", "odd_conv_w", "odd_conv_b", "odd_ln_g", "odd_ln_b"):
        grads[n] = _chip_cols(tot[n], chip)
    for n in list(grads):
        _, deltas[n], new_m[n], new_v[n] = _adamw(w[n], [grads[n]], m[n], v[n], n)

    for n in ("odd_w_in", "odd_w_out"):
        grads[n], deltas[n], new_m[n], new_v[n] = update_layers(n, None, (1, 0))
    partly = {n: update_layers(n, None, (1,)) for n in ("even_w_in", "even_w_out")}
    land(0, [deltas["ada_w"], deltas["odd_w_in"], partly["even_w_in"][1]])
    for n in ("even_w_in", "even_w_out"):
        grads[n], deltas[n], new_m[n], new_v[n] = update_layers(n, partly[n], (0,))
    uq_parts, ukv_parts = zip(*[[jnp.stack(part) for part in zip(*[_mla_local(q["even_mla", i]) for i in range(N_PAIRS)])]
                                for q in (own, sib)])
    for n, parts in (("even_w_uq", uq_parts), ("even_w_ukv", ukv_parts)):
        grads[n], deltas[n], new_m[n], new_v[n] = _adamw(w[n], list(parts), m[n], v[n], n)

    total_loss = lax.psum(loss[0, 0], ("x", "y", "c"))
    return (total_loss, dx[None], *[grads[n] for n in WEIGHT_NAMES], *[deltas[n] for n in WEIGHT_NAMES],
            *[new_m[n] for n in WEIGHT_NAMES], *[new_v[n] for n in WEIGHT_NAMES])
```

```python
import jax
import jax.numpy as jnp
from jax import lax
from jax.experimental import pallas as pl
from jax.experimental.pallas import tpu as pltpu

F32 = jnp.float32
MXU_DTYPE = jnp.bfloat16
MESH = pl.DeviceIdType.MESH
VMEM_LIMIT_V7X = 56 * 2 ** 20

EPS = 1e-6
D_MODEL = 1024
DEPTH = 4
CHUNK = 64
SC_WIDTH = 512
SC_KERNEL = 3
SC_HALO = 8
HEADS = 8
QK_NOPE = 64
QK_ROPE = 32
V_HEAD = 64
HEAD_PAD = 128
Q_LORA = 256
KV_LORA = 128
ROPE_THETA = 10000.0
CONF_KERNEL = 31
CONF_HALO = 32
CONV_ROWS = 64
SUBLANES = 8
EVEN_IN = 2976
EVEN_PAD = 3072
ODD_IN = 3072
N_CHIPS = 4
N_DEV = 8
NEG = -1e30

ADAM_LR = 0.001
ADAM_B1 = 0.9
ADAM_B2 = 0.999
ADAM_EPS = 1e-08
ADAM_WD = 0.01
ADAM_STEP = 10

N_PAIRS = DEPTH // 2
EVEN_SHARD = EVEN_IN // N_CHIPS
EVEN_SHARD_PAD = 768
MLA_ROWS = Q_LORA + 2 * KV_LORA


def _cp(n_grid=0, **kw):
    return pltpu.CompilerParams(dimension_semantics=("arbitrary",) * n_grid,
                                vmem_limit_bytes=VMEM_LIMIT_V7X, **kw)


def _sigmoid(x):
    return 1.0 / (1.0 + jnp.exp(-x))


def _silu(x):
    return x * _sigmoid(x)


def _dsilu(x):
    s = _sigmoid(x)
    return s * (1.0 + x * (1.0 - s))


def _rms(x, g):
    return x * lax.rsqrt(jnp.mean(x * x, axis=-1, keepdims=True) + EPS) * g


def _dot(a, b, dims):
    return lax.dot_general(a.astype(MXU_DTYPE), b.astype(MXU_DTYPE), (dims, ((), ())),
                           preferred_element_type=F32)


def _dot_nn(a, b):
    return _dot(a, b, ((1,), (0,)))


def _dot_nt(a, b):
    return _dot(a, b, ((1,), (1,)))


def _dot_tn(a, b):
    return _dot(a, b, ((0,), (0,)))


def _rows(ts, w, cb=0):
    return pl.BlockSpec((ts, w), lambda i: (i, cb))


def _vec(w, cb=0, r=1):
    return pl.BlockSpec((r, w), lambda i: (0, cb))


def _prev_halo(ts, hr, w, cb):
    return pl.BlockSpec((hr, w), lambda i: (jnp.maximum(i * (ts // hr) - 1, 0), cb))


def _next_halo(ts, hr, w, cb, s):
    return pl.BlockSpec((hr, w), lambda i: (jnp.minimum((i + 1) * (ts // hr), s // hr - 1), cb))


def _sds(shape, dtype=F32):
    return jax.ShapeDtypeStruct(shape, dtype)


def _mm(a, b, mode, out_dtype, tm, tn, name):
    tm = min(tm, a.shape[1] if mode == "tn" else a.shape[0])
    tn = min(tn, b.shape[0] if mode == "nt" else b.shape[1])
    if mode == "nn":
        (m, k), n = a.shape, b.shape[1]
        a_spec = pl.BlockSpec((tm, k), lambda i, j: (i, 0))
        b_spec = pl.BlockSpec((k, tn), lambda i, j: (0, j))
        dot = _dot_nn
    elif mode == "nt":
        (m, k), n = a.shape, b.shape[0]
        a_spec = pl.BlockSpec((tm, k), lambda i, j: (i, 0))
        b_spec = pl.BlockSpec((tn, k), lambda i, j: (j, 0))
        dot = _dot_nt
    else:
        (k, m), n = a.shape, b.shape[1]
        a_spec = pl.BlockSpec((k, tm), lambda i, j: (0, i))
        b_spec = pl.BlockSpec((k, tn), lambda i, j: (0, j))
        dot = _dot_tn
    assert m % tm == 0 and n % tn == 0, (name, m, n, tm, tn)

    def body(a_ref, b_ref, o_ref):
        o_ref[...] = dot(a_ref[...], b_ref[...]).astype(o_ref.dtype)

    return pl.pallas_call(
        body, name=name, grid=(m // tm, n // tn), in_specs=[a_spec, b_spec],
        out_specs=pl.BlockSpec((tm, tn), lambda i, j: (i, j)), out_shape=_sds((m, n), out_dtype),
        compiler_params=_cp(2))(a, b)


def _mm_tn_shards(a, b, by, name):
    k, m = a.shape
    n = b.shape[1]
    if by == "cols":
        tm, tn = m, n // N_CHIPS
        shape, grid = (N_CHIPS, m, tn), (1, N_CHIPS)
        out_spec = pl.BlockSpec((1, tm, tn), lambda i, j: (j, i, 0))
    else:
        tm, tn = m // N_CHIPS, n
        shape, grid = (N_CHIPS, tm, n), (N_CHIPS, 1)
        out_spec = pl.BlockSpec((1, tm, tn), lambda i, j: (i, 0, j))

    def body(a_ref, b_ref, o_ref):
        o_ref[0] = _dot_tn(a_ref[...], b_ref[...])

    return pl.pallas_call(
        body, name=name, grid=grid,
        in_specs=[pl.BlockSpec((k, tm), lambda i, j: (0, i)), pl.BlockSpec((k, tn), lambda i, j: (0, j))],
        out_specs=out_spec, out_shape=_sds(shape), compiler_params=_cp(2))(a, b)


def _even_col(q):
    return q if q < 2432 else (q + 64 if q < 2464 else q + 96)


def _shard_pieces(j):
    lo, hi = EVEN_SHARD * j, EVEN_SHARD * (j + 1)
    cuts = [lo] + [b for b in (2432, 2464) if lo < b < hi] + [hi]
    return [(a - lo, _even_col(a), b - a) for a, b in zip(cuts[:-1], cuts[1:])]


def _ein_from_shards(w):
    _, d, _ = w.shape
    tr = 256

    def body(w_ref, o_ref):
        parts, at = [], 0
        for j in range(N_CHIPS):
            for d0, s0, n in _shard_pieces(j):
                if s0 > at:
                    parts.append(jnp.zeros((tr, s0 - at), F32))
                parts.append(w_ref[j, :, d0:d0 + n].astype(F32))
                at = s0 + n
        o_ref[...] = jnp.concatenate(parts, axis=1).astype(o_ref.dtype)

    return pl.pallas_call(
        body, name="ein_from_shards", grid=(d // tr,),
        in_specs=[pl.BlockSpec((N_CHIPS, tr, EVEN_SHARD), lambda i: (0, i, 0))],
        out_specs=_rows(tr, EVEN_PAD), out_shape=_sds((d, EVEN_PAD), w.dtype), compiler_params=_cp(1))(w)


def _ein_to_shards(dw):
    d = dw.shape[0]
    tr = 256

    def body(dw_ref, o_ref):
        for j in range(N_CHIPS):
            parts = [dw_ref[:, s0:s0 + n] for _, s0, n in _shard_pieces(j)]
            o_ref[j] = jnp.concatenate(parts + [jnp.zeros((tr, EVEN_SHARD_PAD - EVEN_SHARD), F32)], axis=1)

    return pl.pallas_call(
        body, name="ein_to_shards", grid=(d // tr,), in_specs=[_rows(tr, EVEN_PAD)],
        out_specs=pl.BlockSpec((N_CHIPS, tr, EVEN_SHARD_PAD), lambda i: (0, i, 0)),
        out_shape=_sds((N_CHIPS, d, EVEN_SHARD_PAD)), compiler_params=_cp(1))(dw)


def _rope_tables(pos_col, invf):
    s = pos_col.shape[0]
    ts = min(512, s)

    def body(p_ref, f_ref, c_ref, s_ref):
        ang = p_ref[...].astype(F32) * f_ref[...]
        lane = lax.broadcasted_iota(jnp.int32, ang.shape, 1)
        rope = (lane >= QK_NOPE) & (lane < QK_NOPE + QK_ROPE)
        c_ref[...] = jnp.where(lane < QK_NOPE, 1.0, jnp.where(rope, jnp.cos(ang), 0.0))
        s_ref[...] = jnp.where(rope, jnp.sin(ang), 0.0)

    return pl.pallas_call(
        body, name="rope_tables", grid=(s // ts,), in_specs=[_rows(ts, 1), _vec(HEAD_PAD)],
        out_specs=[_rows(ts, HEAD_PAD)] * 2, out_shape=[_sds((s, HEAD_PAD))] * 2,
        compiler_params=_cp(1))(pos_col, invf)


def _after(dep):
    return () if dep is None else (dep,)


def _pre_fwd(x, g, mod_l, ts, dep=None):
    s, d = x.shape

    def body(x_ref, g_ref, sh_ref, sc_ref, *rest):
        h = _rms(x_ref[...], g_ref[...]) * (1.0 + sc_ref[...]) + sh_ref[...]
        rest[-1][...] = h.astype(rest[-1].dtype)

    return pl.pallas_call(
        body, name="pre_fwd", grid=(s // ts,),
        in_specs=[_rows(ts, d), _vec(d), _vec(d, 0), _vec(d, 1)] + [_HBM_SPEC] * len(_after(dep)),
        out_specs=_rows(ts, d), out_shape=_sds((s, d), MXU_DTYPE), compiler_params=_cp(1))(
            x, g, mod_l, mod_l, *_after(dep))


def _pre_bwd(dz, w_in, dx_out, x, g, mod_l, ts):
    s, d = x.shape
    n_in = dz.shape[1]

    def f(xv, gv, sh, sc):
        return _rms(xv, gv) * (1.0 + sc) + sh

    def body(dz_ref, w_ref, dxo_ref, x_ref, g_ref, sh_ref, sc_ref, dx_ref, dsh_ref, dsc_ref, dg_ref):
        @pl.when(pl.program_id(0) == 0)
        def _():
            dsh_ref[...] = jnp.zeros_like(dsh_ref)
            dsc_ref[...] = jnp.zeros_like(dsc_ref)
            dg_ref[...] = jnp.zeros_like(dg_ref)

        _, vjp = jax.vjp(f, x_ref[...], g_ref[...], sh_ref[...], sc_ref[...])
        dx, dg, dsh, dsc = vjp(_dot_nt(dz_ref[...], w_ref[...]))
        dx_ref[...] = dxo_ref[...] + dx
        dsh_ref[...] += dsh
        dsc_ref[...] += dsc
        dg_ref[...] += dg

    return pl.pallas_call(
        body, name="pre_bwd", grid=(s // ts,),
        in_specs=[_rows(ts, n_in), _vec(n_in, 0, d), _rows(ts, d), _rows(ts, d), _vec(d), _vec(d, 0), _vec(d, 1)],
        out_specs=[_rows(ts, d), _vec(d), _vec(d), _vec(d)],
        out_shape=[_sds((s, d)), _sds((1, d)), _sds((1, d)), _sds((1, d))],
        compiler_params=_cp(1))(dz, w_in, dx_out, x, g, mod_l, mod_l)


def _post_pre_fwd(x, yo, g_post, mod_l, g_pre, mod_next, ts):
    s, d = x.shape

    def body(x_ref, yo_ref, gp_ref, gate_ref, g_ref, sh_ref, sc_ref, x_out_ref, h_ref):
        x_new = x_ref[...] + gate_ref[...] * _rms(yo_ref[...], gp_ref[...])
        x_out_ref[...] = x_new
        h_ref[...] = (_rms(x_new, g_ref[...]) * (1.0 + sc_ref[...]) + sh_ref[...]).astype(h_ref.dtype)

    return pl.pallas_call(
        body, name="post_pre_fwd", grid=(s // ts,),
        in_specs=[_rows(ts, d), _rows(ts, d), _vec(d), _vec(d, 2), _vec(d), _vec(d, 0), _vec(d, 1)],
        out_specs=[_rows(ts, d), _rows(ts, d)], out_shape=[_sds((s, d)), _sds((s, d), MXU_DTYPE)],
        compiler_params=_cp(1))(x, yo, g_post, mod_l, g_pre, mod_next, mod_next)


def _post_loss(x, yo, g_post, mod_l, target, ts):
    s, d = x.shape

    def body(x_ref, yo_ref, gp_ref, gate_ref, t_ref, loss_ref, dx_ref):
        err = x_ref[...] + gate_ref[...] * _rms(yo_ref[...], gp_ref[...]) - t_ref[...]
        dx_ref[...] = err * (1.0 / d)

        @pl.when(pl.program_id(0) == 0)
        def _():
            loss_ref[...] = jnp.zeros_like(loss_ref)

        loss_ref[...] += 0.5 * jnp.sum(jnp.sum(err * err, axis=-1, keepdims=True) * (1.0 / d), axis=0, keepdims=True)

    return pl.pallas_call(
        body, name="post_loss", grid=(s // ts,),
        in_specs=[_rows(ts, d), _rows(ts, d), _vec(d), _vec(d, 2), _rows(ts, d)],
        out_specs=[_vec(1), _rows(ts, d)], out_shape=[_sds((1, 1)), _sds((s, d))],
        compiler_params=_cp(1))(x, yo, g_post, mod_l, target)


def _post_bwd(dx_out, yo, g, mod_l, ts, dep=None):
    s, d = yo.shape

    def f(yov, gv, gate):
        return gate * _rms(yov, gv)

    def body(dx_ref, yo_ref, g_ref, gate_ref, *rest):
        dyo_ref, dgate_ref, dg_ref = rest[-3:]
        i = pl.program_id(0)
        _, vjp = jax.vjp(f, yo_ref[...], g_ref[...], gate_ref[...])
        dyo, dg, dgate = vjp(dx_ref[...])
        dyo_ref[...] = dyo.astype(dyo_ref.dtype)

        @pl.when(i == 0)
        def _():
            dgate_ref[...] = jnp.zeros_like(dgate_ref)
            dg_ref[...] = jnp.zeros_like(dg_ref)

        dgate_ref[...] += dgate
        dg_ref[...] += dg

    return pl.pallas_call(
        body, name="post_bwd", grid=(s // ts,),
        in_specs=[_rows(ts, d), _rows(ts, d), _vec(d), _vec(d, 2)] + [_HBM_SPEC] * len(_after(dep)),
        out_specs=[_rows(ts, d), _vec(d), _vec(d)],
        out_shape=[_sds((s, d), MXU_DTYPE), _sds((1, d)), _sds((1, d))],
        compiler_params=_cp(1))(dx_out, yo, g, mod_l, *_after(dep))


def _rope(t, cos, sin):
    lane = lax.broadcasted_iota(jnp.int32, t.shape, 1)
    first = (lane >= QK_NOPE) & (lane < QK_NOPE + QK_ROPE // 2)
    second = (lane >= QK_NOPE + QK_ROPE // 2) & (lane < QK_NOPE + QK_ROPE)
    up = pltpu.roll(t, QK_ROPE // 2, 1)
    down = pltpu.roll(t, HEAD_PAD - QK_ROPE // 2, 1)
    return t * cos + jnp.where(first, -down, jnp.where(second, up, 0.0)) * sin


def _rope_transposed(g, cos, sin):
    lane = lax.broadcasted_iota(jnp.int32, g.shape, 1)
    first = (lane >= QK_NOPE) & (lane < QK_NOPE + QK_ROPE // 2)
    second = (lane >= QK_NOPE + QK_ROPE // 2) & (lane < QK_NOPE + QK_ROPE)
    u = g * sin
    up = pltpu.roll(u, QK_ROPE // 2, 1)
    down = pltpu.roll(u, HEAD_PAD - QK_ROPE // 2, 1)
    return g * cos + jnp.where(first, down, jnp.where(second, -up, 0.0))


def _mla_prep_fwd(z, cos, sin, qg, kvg, wq, wq_rot, wuk, wuv, ts):
    s = z.shape[0]
    wide = HEADS * HEAD_PAD

    def body(cq_ref, ckv_ref, kr_ref, cos_ref, sin_ref, qg_ref, kvg_ref, wq_ref, wqr_ref, wuk_ref, wuv_ref,
             q_ref, qt_ref, k_ref, v_ref):
        cos_v, sin_v = cos_ref[...], sin_ref[...]
        cqn = _rms(cq_ref[...], qg_ref[...])
        ckvn = _rms(ckv_ref[...], kvg_ref[...])
        kr = _rope(kr_ref[...], cos_v, sin_v)
        q_lin, q_rot = _dot_nn(cqn, wq_ref[...]), _dot_nn(cqn, wqr_ref[...])
        k_lin, v_all = _dot_nn(ckvn, wuk_ref[...]), _dot_nn(ckvn, wuv_ref[...])
        for h in range(HEADS):
            lanes = slice(h * HEAD_PAD, (h + 1) * HEAD_PAD)
            qh = q_lin[:, lanes] * cos_v + q_rot[:, lanes] * sin_v
            q_ref[h] = qh.astype(q_ref.dtype)
            qt_ref[h, 0] = qh.T.astype(qt_ref.dtype)
            k_ref[h] = (k_lin[:, lanes] + kr).astype(k_ref.dtype)
            v_ref[h] = v_all[:, lanes].astype(v_ref.dtype)

    out = pl.BlockSpec((HEADS, ts, HEAD_PAD), lambda i: (0, i, 0))
    return pl.pallas_call(
        body, name="mla_prep_fwd", grid=(s // ts,),
        in_specs=[_rows(ts, Q_LORA, 8), _rows(ts, KV_LORA, 18), _rows(ts, HEAD_PAD, 19), _rows(ts, HEAD_PAD), _rows(ts, HEAD_PAD),
                  _vec(Q_LORA), _vec(KV_LORA), _vec(wide, 0, Q_LORA), _vec(wide, 0, Q_LORA), _vec(wide, 0, KV_LORA),
                  _vec(wide, 0, KV_LORA)],
        out_specs=[out, pl.BlockSpec((HEADS, 1, HEAD_PAD, ts), lambda i: (0, i, 0, 0)), out, out],
        out_shape=[_sds((HEADS, s, HEAD_PAD), MXU_DTYPE), _sds((HEADS, s // ts, HEAD_PAD, ts), MXU_DTYPE)]
        + [_sds((HEADS, s, HEAD_PAD), MXU_DTYPE)] * 2,
        compiler_params=_cp(1))(z, z, z, cos, sin, qg, kvg, wq, wq_rot, wuk, wuv)


def _mla_prep_bwd(dz, dq, dk, dv, z, cos, sin, qg, kvg, wq, wuk, wuv, ts):
    s = z.shape[0]

    def fq(cq, g):
        return _rms(cq, g)

    def body(dz_in_ref, dq_ref, dk_ref, dv_ref, cq_ref, ckv_ref, cos_ref, sin_ref, qg_ref, kvg_ref, wq_ref, wuk_ref,
             wuv_ref, dz_ref, dw_ref, dqg_ref, dkvg_ref):
        del dz_in_ref
        cos_v, sin_v = cos_ref[...], sin_ref[...]

        @pl.when(pl.program_id(0) == 0)
        def _():
            dw_ref[...] = jnp.zeros_like(dw_ref)
            dqg_ref[...] = jnp.zeros_like(dqg_ref)
            dkvg_ref[...] = jnp.zeros_like(dkvg_ref)

        cqn, vjp_q = jax.vjp(fq, cq_ref[...], qg_ref[...])
        ckvn, vjp_kv = jax.vjp(fq, ckv_ref[...], kvg_ref[...])
        lane = lax.broadcasted_iota(jnp.int32, (ts, HEAD_PAD), 1)
        rope_lanes = (lane >= QK_NOPE) & (lane < QK_NOPE + QK_ROPE)
        dq_lin = jnp.concatenate([_rope_transposed(dq_ref[h], cos_v, sin_v).astype(MXU_DTYPE) for h in range(HEADS)], axis=1)
        dk_all = jnp.concatenate([dk_ref[h].astype(MXU_DTYPE) for h in range(HEADS)], axis=1)
        dv_all = jnp.concatenate([dv_ref[h].astype(MXU_DTYPE) for h in range(HEADS)], axis=1)
        dkr = jnp.where(rope_lanes, dk_ref[0], 0.0)
        for h in range(1, HEADS):
            dkr = dkr + jnp.where(rope_lanes, dk_ref[h], 0.0)
        dcq, dqg = vjp_q(_dot_nt(dq_lin, wq_ref[...]))
        dckv, dkvg = vjp_kv(_dot_nt(dk_all, wuk_ref[...]) + _dot_nt(dv_all, wuv_ref[...]))
        dz_ref[:, 0:Q_LORA] = dcq.astype(dz_ref.dtype)
        dz_ref[:, Q_LORA:Q_LORA + KV_LORA] = dckv.astype(dz_ref.dtype)
        dz_ref[:, Q_LORA + KV_LORA:] = _rope_transposed(dkr, cos_v, sin_v).astype(dz_ref.dtype)
        dqg_ref[...] += dqg
        dkvg_ref[...] += dkvg
        dwq, dwuk, dwuv = _dot_tn(cqn, dq_lin), _dot_tn(ckvn, dk_all), _dot_tn(ckvn, dv_all)
        for h in range(HEADS):
            lanes = slice(h * HEAD_PAD, (h + 1) * HEAD_PAD)
            row0 = (h % 2) * MLA_ROWS
            dw_ref[h // 2, row0:row0 + Q_LORA, :] += dwq[:, lanes]
            dw_ref[h // 2, row0 + Q_LORA:row0 + Q_LORA + KV_LORA, :] += dwuk[:, lanes]
            dw_ref[h // 2, row0 + Q_LORA + KV_LORA:row0 + MLA_ROWS, :] += dwuv[:, lanes]

    wide = HEADS * HEAD_PAD
    heads = pl.BlockSpec((HEADS, ts, HEAD_PAD), lambda i: (0, i, 0))
    whole = pl.BlockSpec((N_CHIPS, 2 * MLA_ROWS, HEAD_PAD), lambda i: (0, 0, 0))
    return pl.pallas_call(
        body, name="mla_prep_bwd", grid=(s // ts,),
        in_specs=[_HBM_SPEC, heads, heads, heads, _rows(ts, Q_LORA, 8), _rows(ts, KV_LORA, 18),
                  _rows(ts, HEAD_PAD), _rows(ts, HEAD_PAD), _vec(Q_LORA), _vec(KV_LORA), _vec(wide, 0, Q_LORA),
                  _vec(wide, 0, KV_LORA), _vec(wide, 0, KV_LORA)],
        out_specs=[_rows(ts, 512, 4), whole, _vec(Q_LORA), _vec(KV_LORA)],
        out_shape=[_sds(dz.shape, dz.dtype), _sds((N_CHIPS, 2 * MLA_ROWS, HEAD_PAD)), _sds((1, Q_LORA)), _sds((1, KV_LORA))],
        input_output_aliases={0: 0}, compiler_params=_cp(1))(dz, dq, dk, dv, z, z, cos, sin, qg, kvg, wq, wuk, wuv)


def _chunk_mask(q0, k0, tq, tk):
    rows = q0 + lax.broadcasted_iota(jnp.int32, (tq, tk), 0)
    cols = k0 + lax.broadcasted_iota(jnp.int32, (tq, tk), 1)
    shift = CHUNK.bit_length() - 1
    return lax.shift_right_logical(cols, shift) <= lax.shift_right_logical(rows, shift)


def _attn_fwd(q, k, v, tq):
    s = q.shape[1]
    nq = s // tq
    scale = 1.0 / float(QK_NOPE + QK_ROPE) ** 0.5

    assert nq % 2 == 0, (s, tq)

    def body(q_ref, k_ref, v_ref, o_ref, lse_ref):
        pair, hh = pl.program_id(1), pl.program_id(2)

        def step(qv, q0, kj, carry, masked):
            m, l, acc = carry
            k0 = pl.multiple_of(kj * tq, tq)
            sc = _dot_nt(qv, k_ref[0, pl.ds(k0, tq), :]) * scale
            if masked:
                sc = jnp.where(_chunk_mask(q0, k0, tq, tq), sc, NEG)
            m_new = jnp.maximum(m, jnp.max(sc, axis=-1, keepdims=True))
            alpha = jnp.exp(m - m_new)
            p = jnp.exp(sc - m_new)
            l = alpha * l + jnp.sum(p, axis=-1, keepdims=True)
            acc = alpha * acc + _dot_nn(p, v_ref[0, pl.ds(k0, tq), :])
            return m_new, l, acc

        for half in range(2):
            rows = slice(half * tq, (half + 1) * tq)
            qv = q_ref[0, rows, :]
            q0 = (2 * pair + half) * tq
            two = lambda i, c: step(qv, q0, 2 * i + 1, step(qv, q0, 2 * i, c, False), False)
            init = (jnp.full((tq, 1), NEG, F32), jnp.zeros((tq, 1), F32), jnp.zeros((tq, HEAD_PAD), F32))
            carry = lax.fori_loop(0, pair, two, init)
            if half == 1:
                carry = step(qv, q0, 2 * pair, carry, False)
            m, l, acc = step(qv, q0, 2 * pair + half, carry, True)
            o = acc / l
            lse_ref[0, rows, :] = m + jnp.log(l)

            @pl.when(hh == 0)
            def _():
                o_ref[rows, :] = o

            @pl.when(hh == 1)
            def _():
                o_ref[rows, :] += o

    head = lambda hp, pair, hh: 2 * hp + hh
    return pl.pallas_call(
        body, name="attn_fwd", grid=(HEADS // 2, nq // 2, 2),
        in_specs=[pl.BlockSpec((1, 2 * tq, HEAD_PAD), lambda hp, pair, hh: (head(hp, pair, hh), pair, 0)),
                  pl.BlockSpec((1, s, HEAD_PAD), lambda hp, pair, hh: (head(hp, pair, hh), 0, 0)),
                  pl.BlockSpec((1, s, HEAD_PAD), lambda hp, pair, hh: (head(hp, pair, hh), 0, 0))],
        out_specs=[pl.BlockSpec((2 * tq, HEAD_PAD), lambda hp, pair, hh: (pair, hp)),
                   pl.BlockSpec((1, 2 * tq, 1), lambda hp, pair, hh: (head(hp, pair, hh), pair, 0))],
        out_shape=[_sds((s, HEADS * V_HEAD)), _sds((HEADS, s, 1))],
        compiler_params=_cp(3))(q, k, v)


def _attn_bwd(q, q_t, k, v, do, do_t, o, lse, tq):
    s = q.shape[1]
    nq = s // tq
    per_q = tq // do_t.shape[3]
    scale = 1.0 / float(QK_NOPE + QK_ROPE) ** 0.5

    def body(q_ref, qt_ref, k_ref, v_ref, do_ref, dot_ref, o_ref, lse_ref, dq_ref, dk_ref, dv_ref, dk_t, dv_t):
        hh, kj = pl.program_id(1), pl.program_id(2)

        @pl.when(kj == 0)
        def _():
            dq_ref[...] = jnp.zeros_like(dq_ref)

        kv, vv = k_ref[0], v_ref[0]
        lane = lax.broadcasted_iota(jnp.int32, (tq, HEAD_PAD), 1)
        mine = lax.shift_right_logical(lane, 6) == hh
        dk_t[...] = jnp.zeros_like(dk_t)
        dv_t[...] = jnp.zeros_like(dv_t)

        def step(qi, masked):
            q0 = pl.multiple_of(qi * tq, tq)
            qv = q_ref[0, pl.ds(q0, tq), :]
            dov = do_ref[pl.ds(q0, tq), :]
            delta = jnp.sum(jnp.where(mine, dov * o_ref[pl.ds(q0, tq), :], 0.0), axis=-1, keepdims=True)
            sc = _dot_nt(qv, kv) * scale
            if masked:
                sc = jnp.where(_chunk_mask(q0, kj * tq, tq, tq), sc, NEG)
            p = jnp.exp(sc - lse_ref[0, pl.ds(q0, tq), :])
            ds = (p * (_dot_nt(dov, vv) - delta) * scale).astype(MXU_DTYPE)
            do_tv = jnp.concatenate([dot_ref[0, qi * per_q + r] for r in range(per_q)], axis=1)
            dv_t[...] += _dot_nn(do_tv, p)
            dk_t[...] += _dot_nn(qt_ref[0, qi], ds)
            dq_ref[0, pl.ds(q0, tq), :] += _dot_nn(ds, kv)

        step(kj, True)
        odd = (nq - 1 - kj) % 2

        @pl.when(odd == 1)
        def _():
            step(kj + 1, False)

        def two(i, c):
            step(kj + 1 + odd + 2 * i, False)
            step(kj + 2 + odd + 2 * i, False)
            return c

        lax.fori_loop(0, (nq - 1 - kj) // 2, two, 0)
        dk_ref[0] = dk_t[...].T
        dv_ref[0] = dv_t[...].T

    head = lambda hp, hh, kj: 2 * hp + hh
    full = pl.BlockSpec((1, s, HEAD_PAD), lambda hp, hh, kj: (head(hp, hh, kj), 0, 0))
    blk = pl.BlockSpec((1, tq, HEAD_PAD), lambda hp, hh, kj: (head(hp, hh, kj), kj, 0))
    pair = pl.BlockSpec((s, HEAD_PAD), lambda hp, hh, kj: (0, hp))
    return pl.pallas_call(
        body, name="attn_bwd", grid=(HEADS // 2, 2, nq),
        in_specs=[full, pl.BlockSpec((1,) + q_t.shape[1:], lambda hp, hh, kj: (head(hp, hh, kj), 0, 0, 0)), blk, blk,
                  pair, pl.BlockSpec((1,) + do_t.shape[1:], lambda hp, hh, kj: (hp, 0, 0, 0)), pair,
                  pl.BlockSpec((1, s, 1), lambda hp, hh, kj: (head(hp, hh, kj), 0, 0))],
        out_specs=[full, blk, blk], out_shape=[_sds((HEADS, s, HEAD_PAD))] * 3,
        scratch_shapes=[pltpu.VMEM((HEAD_PAD, tq), F32), pltpu.VMEM((HEAD_PAD, tq), F32)],
        compiler_params=_cp(3))(q, q_t, k, v, do, do_t, o, lse)


def _sc_conv(u, ubuf, w_ref, b_ref, ts):
    return (w_ref[2:3, :] * u + w_ref[1:2, :] * ubuf[pl.ds(SC_HALO - 1, ts), :]
            + w_ref[0:1, :] * ubuf[pl.ds(SC_HALO - 2, ts), :] + b_ref[...])


def _even_gate_fwd(z, o, sc_w, sc_b, ts):
    s = z.shape[0]
    w = SC_WIDTH

    def body(ab_ref, ac_ref, ax_ref, ag_ref, bg_ref, hc_ref, hx_ref, o_ref, w_ref, b_ref, y_ref, ubuf):
        i = pl.program_id(0)
        u = ac_ref[...] * ax_ref[...]
        ubuf[0:SC_HALO, :] = jnp.where(i > 0, hc_ref[...] * hx_ref[...], 0.0)
        ubuf[SC_HALO:, :] = u
        conv = _sc_conv(u, ubuf, w_ref, b_ref, ts)
        y_ref[:, 0:w] = (ab_ref[...] * conv * _silu(ag_ref[...])).astype(y_ref.dtype)
        y_ref[:, w:] = (o_ref[...] * _silu(bg_ref[...])).astype(y_ref.dtype)

    return pl.pallas_call(
        body, name="even_gate_fwd", grid=(s // ts,),
        in_specs=[_rows(ts, w, 0), _rows(ts, w, 1), _rows(ts, w, 2), _rows(ts, w, 3), _rows(ts, w, 5),
                  _prev_halo(ts, SC_HALO, w, 1), _prev_halo(ts, SC_HALO, w, 2), _rows(ts, w),
                  _vec(w, 0, SC_KERNEL), _vec(w)],
        out_specs=_rows(ts, 2 * w), out_shape=_sds((s, 2 * w), MXU_DTYPE),
        scratch_shapes=[pltpu.VMEM((ts + SC_HALO, w), F32)],
        compiler_params=_cp(1))(z, z, z, z, z, z, z, o, sc_w, sc_b)


def _even_gate_bwd(dy, z, o, sc_w, sc_b, ts):
    s = z.shape[0]
    w = SC_WIDTH
    n = s // ts

    def body(dya_ref, dyb_ref, dyan_ref, ab_ref, ac_ref, ax_ref, ag_ref, bg_ref, hc_ref, hx_ref, abn_ref, agn_ref,
             o_ref, w_ref, b_ref, dz_ref, do_ref, dot_ref, dw_ref, db_ref, ubuf, dbuf):
        i = pl.program_id(0)
        ab, ac, ax, ag, bg = ab_ref[...], ac_ref[...], ax_ref[...], ag_ref[...], bg_ref[...]
        dya, dyb = dya_ref[...], dyb_ref[...]
        u = ac * ax
        ubuf[0:SC_HALO, :] = jnp.where(i > 0, hc_ref[...] * hx_ref[...], 0.0)
        ubuf[SC_HALO:, :] = u
        conv = _sc_conv(u, ubuf, w_ref, b_ref, ts)
        sg = _silu(ag)
        dconv = dya * ab * sg
        dbuf[0:ts, :] = dconv
        dbuf[ts:, :] = jnp.where(i < n - 1, dyan_ref[...] * abn_ref[...] * _silu(agn_ref[...]), 0.0)
        du = w_ref[2:3, :] * dconv + w_ref[1:2, :] * dbuf[pl.ds(1, ts), :] + w_ref[0:1, :] * dbuf[pl.ds(2, ts), :]
        dz_ref[:, 0:w] = (dya * conv * sg).astype(dz_ref.dtype)
        dz_ref[:, w:2 * w] = (du * ax).astype(dz_ref.dtype)
        dz_ref[:, 2 * w:3 * w] = (du * ac).astype(dz_ref.dtype)
        dz_ref[:, 3 * w:4 * w] = (dya * ab * conv * _dsilu(ag)).astype(dz_ref.dtype)
        dz_ref[:, 4 * w:5 * w] = jnp.zeros((ts, w), dz_ref.dtype)
        dz_ref[:, 5 * w:] = (dyb * o_ref[...] * _dsilu(bg)).astype(dz_ref.dtype)
        do = dyb * _silu(bg)
        do_ref[...] = do
        for pair in range(HEADS // 2):
            dot_ref[pair, 0] = do[:, pair * HEAD_PAD:(pair + 1) * HEAD_PAD].T.astype(dot_ref.dtype)

        @pl.when(i == 0)
        def _():
            dw_ref[...] = jnp.zeros_like(dw_ref)
            db_ref[...] = jnp.zeros_like(db_ref)

        dw_ref[0:1, :] += jnp.sum(dconv * ubuf[pl.ds(SC_HALO - 2, ts), :], axis=0, keepdims=True)
        dw_ref[1:2, :] += jnp.sum(dconv * ubuf[pl.ds(SC_HALO - 1, ts), :], axis=0, keepdims=True)
        dw_ref[2:3, :] += jnp.sum(dconv * u, axis=0, keepdims=True)
        db_ref[...] += jnp.sum(dconv, axis=0, keepdims=True)

    return pl.pallas_call(
        body, name="even_gate_bwd", grid=(n,),
        in_specs=[_rows(ts, w, 0), _rows(ts, w, 1), _next_halo(ts, SC_HALO, w, 0, s),
                  _rows(ts, w, 0), _rows(ts, w, 1), _rows(ts, w, 2), _rows(ts, w, 3), _rows(ts, w, 5),
                  _prev_halo(ts, SC_HALO, w, 1), _prev_halo(ts, SC_HALO, w, 2),
                  _next_halo(ts, SC_HALO, w, 0, s), _next_halo(ts, SC_HALO, w, 3, s),
                  _rows(ts, w), _vec(w, 0, SC_KERNEL), _vec(w)],
        out_specs=[_rows(ts, EVEN_PAD), _rows(ts, w), pl.BlockSpec((HEADS // 2, 1, HEAD_PAD, ts), lambda i: (0, i, 0, 0)),
                   _vec(w, 0, SC_KERNEL), _vec(w)],
        out_shape=[_sds((s, EVEN_PAD), MXU_DTYPE), _sds((s, w)), _sds((HEADS // 2, n, HEAD_PAD, ts), MXU_DTYPE),
                   _sds((SC_KERNEL, w)), _sds((1, w))],
        scratch_shapes=[pltpu.VMEM((ts + SC_HALO, w), F32), pltpu.VMEM((ts + SC_HALO, w), F32)],
        compiler_params=_cp(1))(dy, dy, dy, z, z, z, z, z, z, z, z, z, o, sc_w, sc_b)


def _ln_act(uc, sg, g, b):
    mu = jnp.mean(uc, axis=-1, keepdims=True)
    var = jnp.mean(jnp.square(uc - mu), axis=-1, keepdims=True)
    return _silu((uc - mu) * lax.rsqrt(var + EPS) * g + b) * _silu(sg)


def _shifted_copies(buf, shifted, rows):
    for b in range(1, SUBLANES):
        shifted[b - 1, 0:rows, :] = buf[pl.ds(b, rows), :]


def _rows_at(buf, shifted, start, n):
    a, b = divmod(start, SUBLANES)
    return buf[pl.ds(SUBLANES * a, n), :] if b == 0 else shifted[b - 1, pl.ds(SUBLANES * a, n), :]


def _odd_fwd(z, conv_w, conv_b, ln_g, ln_b, ts):
    s = z.shape[0]
    d = D_MODEL
    k = CONF_KERNEL

    def body(val_ref, glu_ref, sg_ref, hval_ref, hglu_ref, w_ref, b_ref, g_ref, beta_ref, y_ref, uc_ref, ubuf, ush):
        i = pl.program_id(0)
        ubuf[0:CONF_HALO, :] = jnp.where(i > 0, hval_ref[...] * _sigmoid(hglu_ref[...]), 0.0)
        ubuf[CONF_HALO:, :] = val_ref[...] * _sigmoid(glu_ref[...])
        _shifted_copies(ubuf, ush, ts + CONF_HALO - SUBLANES)
        for r0 in range(0, ts, CONV_ROWS):
            acc = jnp.broadcast_to(b_ref[...], (CONV_ROWS, d))
            for j in range(k):
                acc = acc + w_ref[j:j + 1, :] * _rows_at(ubuf, ush, r0 + CONF_HALO - (k - 1) + j, CONV_ROWS)
            uc_ref[r0:r0 + CONV_ROWS, :] = acc
        y_ref[...] = _ln_act(uc_ref[...], sg_ref[...], g_ref[...], beta_ref[...]).astype(y_ref.dtype)

    return pl.pallas_call(
        body, name="odd_fwd", grid=(s // ts,),
        in_specs=[_rows(ts, d, 0), _rows(ts, d, 1), _rows(ts, d, 2),
                  _prev_halo(ts, CONF_HALO, d, 0), _prev_halo(ts, CONF_HALO, d, 1),
                  _vec(d, 0, k), _vec(d), _vec(d), _vec(d)],
        out_specs=[_rows(ts, d), _rows(ts, d)], out_shape=[_sds((s, d), MXU_DTYPE), _sds((s, d))],
        scratch_shapes=[pltpu.VMEM((ts + CONF_HALO, d), F32),
                        pltpu.VMEM((SUBLANES - 1, ts + CONF_HALO - SUBLANES, d), F32)],
        compiler_params=_cp(1))(z, z, z, z, z, conv_w, conv_b, ln_g, ln_b)


def _odd_bwd(dy, z, uc, conv_w, ln_g, ln_b, ts):
    s = z.shape[0]
    d = D_MODEL
    k = CONF_KERNEL
    n = s // ts

    def body(dy_ref, dyn_ref, val_ref, glu_ref, sg_ref, sgn_ref, uc_ref, ucn_ref,
             w_ref, g_ref, beta_ref, dz_ref, dw_ref, db_ref, dg_ref, dbeta_ref, dbuf, dsh, dw_acc):
        i = pl.program_id(0)
        val, glu = val_ref[...], glu_ref[...]
        sig = _sigmoid(glu)
        u = val * sig
        _, vjp = jax.vjp(_ln_act, uc_ref[...], sg_ref[...], g_ref[...], beta_ref[...])
        duc, dsg, dg, dbeta = vjp(dy_ref[...])
        _, vjp_n = jax.vjp(_ln_act, ucn_ref[...], sgn_ref[...], g_ref[...], beta_ref[...])
        dbuf[0:ts, :] = duc
        dbuf[ts:, :] = jnp.where(i < n - 1, vjp_n(dyn_ref[...])[0], 0.0)
        dz_ref[:, 2 * d:] = dsg.astype(dz_ref.dtype)
        _shifted_copies(dbuf, dsh, ts + CONF_HALO - SUBLANES)

        @pl.when(i == 0)
        def _():
            dw_acc[...] = jnp.zeros_like(dw_acc)
            db_ref[...] = jnp.zeros_like(db_ref)
            dg_ref[...] = jnp.zeros_like(dg_ref)
            dbeta_ref[...] = jnp.zeros_like(dbeta_ref)

        db_ref[...] += jnp.sum(duc, axis=0, keepdims=True)
        dg_ref[...] += dg
        dbeta_ref[...] += dbeta
        for r0 in range(0, ts, CONV_ROWS):
            acc = jnp.zeros((CONV_ROWS, d), F32)
            for j in range(k):
                acc = acc + w_ref[j:j + 1, :] * _rows_at(dbuf, dsh, r0 + (k - 1) - j, CONV_ROWS)
            sig_r = sig[r0:r0 + CONV_ROWS, :]
            dz_ref[r0:r0 + CONV_ROWS, 0:d] = (acc * sig_r).astype(dz_ref.dtype)
            dz_ref[r0:r0 + CONV_ROWS, d:2 * d] = (acc * val[r0:r0 + CONV_ROWS, :] * sig_r * (1.0 - sig_r)).astype(dz_ref.dtype)
        for j in range(k):
            prod = _rows_at(dbuf, dsh, (k - 1) - j, ts) * u
            dw_acc[j] += jnp.sum(prod.reshape(ts // SUBLANES, SUBLANES, d), axis=0)

        @pl.when(i == n - 1)
        def _():
            dw_ref[...] = jnp.sum(dw_acc[...], axis=1)

    return pl.pallas_call(
        body, name="odd_bwd", grid=(n,),
        in_specs=[_rows(ts, d), _next_halo(ts, CONF_HALO, d, 0, s),
                  _rows(ts, d, 0), _rows(ts, d, 1), _rows(ts, d, 2), _next_halo(ts, CONF_HALO, d, 2, s),
                  _rows(ts, d), _next_halo(ts, CONF_HALO, d, 0, s),
                  _vec(d, 0, k), _vec(d), _vec(d)],
        out_specs=[_rows(ts, ODD_IN), _vec(d, 0, k), _vec(d), _vec(d), _vec(d)],
        out_shape=[_sds((s, ODD_IN), MXU_DTYPE), _sds((k, d)), _sds((1, d)), _sds((1, d)), _sds((1, d))],
        scratch_shapes=[pltpu.VMEM((ts + CONF_HALO, d), F32),
                        pltpu.VMEM((SUBLANES - 1, ts + CONF_HALO - SUBLANES, d), F32), pltpu.VMEM((k, SUBLANES, d), F32)],
        compiler_params=_cp(1))(dy, dy, z, z, z, z, uc, uc, conv_w, ln_g, ln_b)


def _local_step(x, target, cos, sin, mod, p, layer_weights, fwd_dep=None, grads_done=None):
    s = x.shape[0]
    tsf, tsb = min(512, s // 2), min(256, s // 2)
    tq = min(512, s // 2)
    row1 = lambda a, i: a[i:i + 1]
    saved = []
    h = _pre_fwd(x, row1(p["pre_norm_g"], 0), row1(mod, 0), tsf, fwd_dep)
    for layer in range(DEPTH):
        i = layer // 2
        mod_l = row1(mod, layer)
        wl = layer_weights(layer, h)
        if layer % 2 == 0:
            z = _mm(h, wl["w_in"], "nn", F32, 512, EVEN_PAD, "even_in_fwd")
            if "late" in wl:
                wl.update(wl.pop("late")(z))
            q, q_t, k, v = _mla_prep_fwd(z, cos, sin, row1(p["even_q_norm_g"], i), row1(p["even_kv_norm_g"], i),
                                    wl["wq"], wl["wq_rot"], wl["wuk"], wl["wuv"], tsf)
            o, lse = _attn_fwd(q, k, v, tq)
            y = _even_gate_fwd(z, o, wl["sc_conv_w"], row1(p["even_sc_conv_b"], i), tsf)
            yo = _mm(y, wl["w_out"], "nn", F32, 256, 1024, "even_out_fwd")
            saved.append((x, h, z, y, yo, wl, (q, q_t, k, v, o, lse)))
        else:
            z = _mm(h, wl["w_in"], "nn", F32, 512, ODD_IN, "odd_in_fwd")
            y, uc = _odd_fwd(z, wl["conv_w"], wl["conv_b"], wl["ln_g"], wl["ln_b"], tsf)
            yo = _mm(y, wl["w_out"], "nn", F32, 256, 1024, "odd_out_fwd")
            saved.append((x, h, z, y, yo, wl, uc))
        if layer + 1 < DEPTH:
            x, h = _post_pre_fwd(x, yo, row1(p["post_norm_g"], layer), mod_l, row1(p["pre_norm_g"], layer + 1),
                                 row1(mod, layer + 1), tsf)
        else:
            loss, dx = _post_loss(x, yo, row1(p["post_norm_g"], layer), mod_l, target, tsf)

    g = {n: [None] * (DEPTH if n in ("pre_norm_g", "post_norm_g") else N_PAIRS) for n in (
        "pre_norm_g", "post_norm_g", "even_sc_conv_w", "even_sc_conv_b", "even_q_norm_g", "even_kv_norm_g",
        "odd_conv_w", "odd_conv_b", "odd_ln_g", "odd_ln_b")}
    dmod = [None] * DEPTH
    dep = None
    for layer in reversed(range(DEPTH)):
        i = layer // 2
        mod_l = row1(mod, layer)
        x_in, h, z, y, yo, wl, extra = saved[layer]
        dyo, dgate, g["post_norm_g"][layer] = _post_bwd(dx, yo, row1(p["post_norm_g"], layer), mod_l, tsb, dep)
        bufs = {}
        if layer % 2 == 0:
            q, q_t, k, v, o, lse = extra
            dy = _mm(dyo, wl["w_out"], "nt", F32, 256, 1024, "even_out_bwd_x")
            bufs["even_w_out"] = _mm_tn_shards(y, dyo, "rows", "even_out_bwd_w")
            dz, do, do_t, g["even_sc_conv_w"][i], g["even_sc_conv_b"][i] = _even_gate_bwd(
                dy, z, o, wl["sc_conv_w"], row1(p["even_sc_conv_b"], i), tsb)
            dq, dk, dv = _attn_bwd(q, q_t, k, v, do, do_t, o, lse, tq)
            dz, bufs["even_mla"], g["even_q_norm_g"][i], g["even_kv_norm_g"][i] = _mla_prep_bwd(
                dz, dq, dk, dv, z, cos, sin, row1(p["even_q_norm_g"], i), row1(p["even_kv_norm_g"], i),
                wl["wq"], wl["wuk"], wl["wuv"], tsb)
            bufs["even_w_in"] = _ein_to_shards(_mm(h, dz, "tn", F32, D_MODEL, 512, "even_in_bwd_w"))
        else:
            uc = extra
            dy = _mm(dyo, wl["w_out"], "nt", F32, 256, 1024, "odd_out_bwd_x")
            bufs["odd_w_out"] = _mm_tn_shards(y, dyo, "rows", "odd_out_bwd_w")
            dz, g["odd_conv_w"][i], g["odd_conv_b"][i], g["odd_ln_g"][i], g["odd_ln_b"][i] = _odd_bwd(
                dy, z, uc, wl["conv_w"], wl["ln_g"], wl["ln_b"], tsb)
            bufs["odd_w_in"] = _mm_tn_shards(h, dz, "cols", "odd_in_bwd_w")
        dx, dshift, dscale, g["pre_norm_g"][layer] = _pre_bwd(
            dz, wl["w_in"], dx, x_in, row1(p["pre_norm_g"], layer), mod_l, tsf)
        dmod[layer] = jnp.concatenate([dshift, dscale, dgate], axis=-1)
        dep = grads_done(layer, bufs, dx) if grads_done is not None else None
    stack = lambda parts: jnp.stack([a[0] if a.shape[0] == 1 and a.ndim == 2 else a for a in parts])
    small = {n: stack(parts) for n, parts in g.items()}
    small["dmod"] = jnp.concatenate(dmod, axis=0)
    return loss, dx, small


def _uq_to_heads(w):
    w = w.reshape(N_CHIPS, Q_LORA, 2, QK_NOPE + QK_ROPE).transpose(0, 2, 1, 3).reshape(HEADS, Q_LORA, QK_NOPE + QK_ROPE)
    half = QK_ROPE // 2
    rotated = jnp.concatenate([jnp.zeros_like(w[..., :QK_NOPE]), -w[..., QK_NOPE + half:], w[..., QK_NOPE:QK_NOPE + half]],
                              axis=-1)
    pad = ((0, 0), (0, 0), (0, HEAD_PAD - QK_NOPE - QK_ROPE))
    return _side_by_side(jnp.pad(w, pad)), _side_by_side(jnp.pad(rotated, pad))


def _side_by_side(w):
    return w.transpose(1, 0, 2).reshape(w.shape[1], HEADS * HEAD_PAD)


def _ukv_to_heads(w):
    w = w.reshape(N_CHIPS, KV_LORA, 2, QK_NOPE + V_HEAD).transpose(0, 2, 1, 3).reshape(HEADS, KV_LORA, QK_NOPE + V_HEAD)
    wk = jnp.pad(w[..., :QK_NOPE], ((0, 0), (0, 0), (0, HEAD_PAD - QK_NOPE)))
    wv = w[..., QK_NOPE:]
    zero = jnp.zeros_like(wv)
    odd = (jnp.arange(HEADS) % 2 == 1)[:, None, None]
    wv = jnp.concatenate([jnp.where(odd, zero, wv), jnp.where(odd, wv, zero)], axis=-1)
    return _side_by_side(wk), _side_by_side(wv)


def _mla_local(q):
    blocks = q.reshape(2, MLA_ROWS, HEAD_PAD)
    uq = jnp.concatenate([blocks[r, :Q_LORA, :QK_NOPE + QK_ROPE] for r in range(2)], axis=-1)
    ukv = jnp.concatenate(
        [jnp.concatenate([blocks[r, Q_LORA:Q_LORA + KV_LORA, :QK_NOPE],
                          blocks[r, Q_LORA + KV_LORA:, V_HEAD * r:V_HEAD * (r + 1)]], axis=-1) for r in range(2)], axis=-1)
    return uq, ukv


def _place():
    return lax.axis_index("x"), lax.axis_index("y"), lax.axis_index("c")


def _flip(v, bit):
    return 1 - v if bit else v


def _sem(a, k):
    return a * (N_CHIPS - 1) + k - 1


def _remote(src, dst, send_sem, recv_sem, peer):
    return pltpu.make_async_remote_copy(src_ref=src, dst_ref=dst, send_sem=send_sem, recv_sem=recv_sem,
                                        device_id=peer, device_id_type=MESH)


_VMEM_SPEC = pl.BlockSpec(memory_space=pltpu.VMEM)
_HBM_SPEC = pl.BlockSpec(memory_space=pl.ANY)


def _ada_fwd(c8, ada_w, ada_b_sh):
    depth, d, cols = ada_w.shape

    def body(c_ref, w_ref, b_ref, call_ref, mod_ref, s1, r1, s2, r2):
        x, y, c = _place()
        chip = 2 * x + y
        me = 2 * chip + c
        call_ref[me] = c_ref[...]
        sends = []
        for k in range(1, N_DEV):
            peer = (_flip(x, k & 4), _flip(y, k & 2), _flip(c, k & 1))
            cp = _remote(c_ref, call_ref.at[me], s1.at[k - 1], r1.at[k - 1], peer)
            cp.start()
            sends.append(cp)
        for k in range(1, N_DEV):
            src = 4 * _flip(x, k & 4) + 2 * _flip(y, k & 2) + _flip(c, k & 1)
            _remote(c_ref, call_ref.at[src], s1.at[k - 1], r1.at[k - 1], (x, y, c)).wait_recv()
        act = _silu(jnp.concatenate([call_ref[e, 0:1, :] for e in range(N_DEV)], axis=0))
        for l in range(depth):
            mod_ref[chip, l] = _dot_nn(act, w_ref[l]) + b_ref[l:l + 1, :]
        for k in range(1, N_CHIPS):
            peer = (_flip(x, k & 2), _flip(y, k & 1), c)
            cp = _remote(mod_ref.at[chip], mod_ref.at[chip], s2.at[k - 1], r2.at[k - 1], peer)
            cp.start()
            sends.append(cp)
        for k in range(1, N_CHIPS):
            src = 2 * _flip(x, k & 2) + _flip(y, k & 1)
            _remote(mod_ref.at[src], mod_ref.at[src], s2.at[k - 1], r2.at[k - 1], (x, y, c)).wait_recv()
        for cp in sends:
            cp.wait_send()

    return pl.pallas_call(
        body, name="ada_fwd", in_specs=[_VMEM_SPEC] * 3, out_specs=[_VMEM_SPEC] * 2,
        out_shape=[_sds((N_DEV, 8, d)), _sds((N_CHIPS, depth, N_DEV, cols))],
        scratch_shapes=[pltpu.SemaphoreType.DMA((N_DEV - 1,)), pltpu.SemaphoreType.DMA((N_DEV - 1,)),
                        pltpu.SemaphoreType.DMA((N_CHIPS - 1,)), pltpu.SemaphoreType.DMA((N_CHIPS - 1,))],
        compiler_params=pltpu.CompilerParams(vmem_limit_bytes=VMEM_LIMIT_V7X))(c8, ada_w, ada_b_sh)


def _ada_bwd(c_t, dmod_sh):
    depth, n, cols = dmod_sh.shape
    d = c_t.shape[0]
    tr = 256

    def body(c_ref, dm_ref, o_ref):
        act = _silu(c_ref[...])
        acc = act[:, 0:1] * dm_ref[0, 0:1, :]
        for e in range(1, n):
            acc = acc + act[:, e:e + 1] * dm_ref[0, e:e + 1, :]
        o_ref[0] = acc

    return pl.pallas_call(
        body, name="ada_bwd", grid=(depth, d // tr),
        in_specs=[pl.BlockSpec((tr, n), lambda l, i: (i, 0)), pl.BlockSpec((1, n, cols), lambda l, i: (l, 0, 0))],
        out_specs=pl.BlockSpec((1, tr, cols), lambda l, i: (l, i, 0)), out_shape=_sds((depth, d, cols)),
        compiler_params=_cp(2))(c_t, dmod_sh)


def _gathered_shape(shape, how):
    if how == "slot":
        return (N_CHIPS,) + shape
    r, cc = shape
    return (r, N_CHIPS * cc) if how == "cols" else (N_CHIPS * r, cc)


def _gathered_part(ref, shape, how, chip):
    if how == "slot":
        return ref.at[chip]
    if how == "cols":
        return ref.at[:, pl.ds(pl.multiple_of(chip * shape[1], 128), shape[1])]
    return ref.at[pl.ds(pl.multiple_of(chip * shape[0], 8), shape[0]), :]


_SEM_SPEC = pl.BlockSpec(memory_space=pltpu.SEMAPHORE)
_TOKEN = jax.ShapeDtypeStruct((8, 128), F32)
_SPLIT_COPY = pltpu.CompilerParams(has_side_effects=pltpu.SideEffectType.DATAFLOW_SIDE_EFFECTING)


def _in_hbm(a):
    return pltpu.with_memory_space_constraint(a, pltpu.HBM)


def _gather_start(items, gathered, name, after=()):
    n = len(items)

    def body(*refs):
        ins, outs = refs[:n], refs[n:2 * n]
        send_sems, recv_sems = refs[2 * n + len(after)], refs[2 * n + len(after) + 1]
        x, y, c = _place()
        for a in range(n):
            for k in range(1, N_CHIPS):
                part = _gathered_part(outs[a], items[a][0].shape, items[a][1], 2 * x + y)
                _remote(ins[a], part, send_sems.at[_sem(a, k)], recv_sems.at[_sem(a, k)],
                        (_flip(x, k & 2), _flip(y, k & 1), c)).start()
        refs[-1][...] = jnp.zeros(_TOKEN.shape, _TOKEN.dtype)

    arrays = [_in_hbm(a) for a, _ in items] + [_in_hbm(a) for a in gathered]
    res = pl.pallas_call(
        body, name=name, in_specs=[_HBM_SPEC] * (2 * n + len(after)),
        out_specs=[_SEM_SPEC, _SEM_SPEC] + [_HBM_SPEC] * (2 * n) + [_VMEM_SPEC],
        out_shape=[pltpu.SemaphoreType.DMA((n * (N_CHIPS - 1),)), pltpu.SemaphoreType.DMA((n * (N_CHIPS - 1),))]
        + [pltpu.HBM(a.shape, a.dtype) for a in arrays] + [_TOKEN],
        input_output_aliases={a: 2 + a for a in range(2 * n)}, compiler_params=_SPLIT_COPY)(*arrays, *after)
    return res[0], res[1], res[2:2 + n], res[2 + n:2 + 2 * n], res[-1]


def _gather_wait(items, started, after, name):
    n = len(items)
    send_sems, recv_sems, shards, gathered, _ = started

    def body(*refs):
        ins, outs, send_sems, recv_sems = refs[:n], refs[n:2 * n], refs[2 * n], refs[2 * n + 1]
        x, y, c = _place()
        for a in range(n):
            for k in range(1, N_CHIPS):
                part = _gathered_part(outs[a], items[a][0].shape, items[a][1], 2 * _flip(x, k & 2) + _flip(y, k & 1))
                cp = _remote(ins[a], part, send_sems.at[_sem(a, k)], recv_sems.at[_sem(a, k)], (x, y, c))
                cp.wait_send()
                cp.wait_recv()

    res = pl.pallas_call(
        body, name=name, in_specs=[_HBM_SPEC] * (2 * n) + [_SEM_SPEC, _SEM_SPEC] + [_HBM_SPEC] * len(after),
        out_specs=[_HBM_SPEC] * (2 * n), out_shape=[pltpu.HBM(a.shape, a.dtype) for a in (*shards, *gathered)],
        input_output_aliases={a: a for a in range(2 * n)}, compiler_params=_SPLIT_COPY)(
            *shards, *gathered, send_sems, recv_sems, *after)
    return res[n:]


def _rs_start(bufs, name, after=()):
    n = len(bufs)

    def body(*refs):
        srcs, lands = refs[:n], refs[n:2 * n]
        send_sems, recv_sems = refs[2 * n + len(after)], refs[2 * n + len(after) + 1]
        x, y, c = _place()
        for a in range(n):
            for k in range(1, N_CHIPS):
                tx, ty = _flip(x, k & 2), _flip(y, k & 1)
                _remote(srcs[a].at[2 * tx + ty], lands[a].at[k - 1], send_sems.at[_sem(a, k)], recv_sems.at[_sem(a, k)],
                        (tx, ty, c)).start()
        refs[-1][...] = jnp.zeros(_TOKEN.shape, _TOKEN.dtype)

    arrays = [_in_hbm(b) for b in bufs] + [_in_hbm(lax.empty((N_CHIPS - 1,) + b.shape[1:], b.dtype)) for b in bufs]
    res = pl.pallas_call(
        body, name=name, in_specs=[_HBM_SPEC] * (2 * n + len(after)),
        out_specs=[_SEM_SPEC, _SEM_SPEC] + [_HBM_SPEC] * (2 * n) + [_VMEM_SPEC],
        out_shape=[pltpu.SemaphoreType.DMA((n * (N_CHIPS - 1),)), pltpu.SemaphoreType.DMA((n * (N_CHIPS - 1),))]
        + [pltpu.HBM(a.shape, a.dtype) for a in arrays] + [_TOKEN],
        input_output_aliases={a: 2 + a for a in range(2 * n)}, compiler_params=_SPLIT_COPY)(*arrays, *after)
    return res[0], res[1], res[2:2 + n], res[2 + n:2 + 2 * n], res[-1]


def _rs_wait(started, after, name):
    send_sems, recv_sems, bufs, lands, _ = started
    n = len(bufs)

    def body(*refs):
        srcs, lnds, send_sems, recv_sems = refs[:n], refs[n:2 * n], refs[2 * n], refs[2 * n + 1]
        x, y, c = _place()
        for a in range(n):
            for k in range(1, N_CHIPS):
                cp = _remote(srcs[a].at[0], lnds[a].at[k - 1], send_sems.at[_sem(a, k)], recv_sems.at[_sem(a, k)], (x, y, c))
                cp.wait_send()
                cp.wait_recv()

    res = pl.pallas_call(
        body, name=name, in_specs=[_HBM_SPEC] * (2 * n) + [_SEM_SPEC, _SEM_SPEC] + [_HBM_SPEC] * len(after),
        out_specs=[_HBM_SPEC] * (2 * n), out_shape=[pltpu.HBM(a.shape, a.dtype) for a in (*bufs, *lands)],
        input_output_aliases={a: a for a in range(2 * n)}, compiler_params=_SPLIT_COPY)(
            *bufs, *lands, send_sems, recv_sems, *after)
    return res[:n], res[n:]


def _place_own(shard, how, chip_idx):
    r, cc = shard.shape
    block, index = {"slot": ((1, r, cc), lambda i, c: (c[0], 0, 0)), "cols": ((r, cc), lambda i, c: (0, c[0])),
                    "rows": ((r, cc), lambda i, c: (c[0], 0))}[how]

    def body(c_ref, in_ref, o_ref):
        del c_ref
        o_ref[...] = in_ref[...].reshape(o_ref.shape)

    return pl.pallas_call(
        body, name="place_own", out_shape=_sds(_gathered_shape(shard.shape, how), shard.dtype),
        grid_spec=pltpu.PrefetchScalarGridSpec(
            num_scalar_prefetch=1, grid=(1,), in_specs=[pl.BlockSpec((r, cc), lambda i, c: (0, 0))],
            out_specs=pl.BlockSpec(block, index)),
        compiler_params=_cp(1))(chip_idx, shard)


def _gather_sum_all(small):
    r, w = small.shape

    def body(in_ref, all_ref, sum_ref, send_sems, recv_sems):
        x, y, c = _place()
        me = 4 * x + 2 * y + c
        all_ref[me] = in_ref[...]
        sends = []
        for k in range(1, N_DEV):
            peer = (_flip(x, k & 4), _flip(y, k & 2), _flip(c, k & 1))
            cp = _remote(in_ref, all_ref.at[me], send_sems.at[k - 1], recv_sems.at[k - 1], peer)
            cp.start()
            sends.append(cp)
        for k in range(1, N_DEV):
            src = 4 * _flip(x, k & 4) + 2 * _flip(y, k & 2) + _flip(c, k & 1)
            _remote(in_ref, all_ref.at[src], send_sems.at[k - 1], recv_sems.at[k - 1], (x, y, c)).wait_recv()
        acc = all_ref[0]
        for e in range(1, N_DEV):
            acc = acc + all_ref[e]
        sum_ref[...] = acc
        for cp in sends:
            cp.wait_send()

    return pl.pallas_call(
        body, name="gather_sum_all", in_specs=[_VMEM_SPEC], out_specs=[_VMEM_SPEC] * 2,
        out_shape=[_sds((N_DEV, r, w)), _sds((r, w))],
        scratch_shapes=[pltpu.SemaphoreType.DMA((N_DEV - 1,)), pltpu.SemaphoreType.DMA((N_DEV - 1,))],
        compiler_params=pltpu.CompilerParams(vmem_limit_bytes=VMEM_LIMIT_V7X))(small)


def _add_chips(buf, t, chip_idx):
    r, cc = buf.shape[1:]
    tr = min(256, r)

    def body(c_ref, p_ref, t_ref, o_ref):
        del c_ref
        o_ref[...] = p_ref[0] + t_ref[0].astype(F32) + t_ref[1].astype(F32) + t_ref[2].astype(F32)

    return pl.pallas_call(
        body, name="add_chips", out_shape=_sds((r, cc)),
        grid_spec=pltpu.PrefetchScalarGridSpec(
            num_scalar_prefetch=1, grid=(r // tr,),
            in_specs=[pl.BlockSpec((1, tr, cc), lambda i, c: (c[0], i, 0)),
                      pl.BlockSpec((N_CHIPS - 1, tr, cc), lambda i, c: (0, i, 0))],
            out_specs=pl.BlockSpec((tr, cc), lambda i, c: (i, 0))),
        compiler_params=_cp(1))(chip_idx, buf, t)


def _rs_sibling(qs):
    n = len(qs)

    def body(*refs):
        ins, outs = refs[:n], refs[n:2 * n]
        send_sems, recv_sems = refs[2 * n:]
        x, y, c = _place()
        copies = [_remote(ins[a], outs[a], send_sems.at[a], recv_sems.at[a], (x, y, 1 - c)) for a in range(n)]
        for cp in copies:
            cp.start()
        for cp in copies:
            cp.wait()

    return pl.pallas_call(
        body, name="rs_sibling", in_specs=[_HBM_SPEC] * n, out_specs=[_HBM_SPEC] * n,
        out_shape=[_sds(q.shape) for q in qs],
        scratch_shapes=[pltpu.SemaphoreType.DMA((n,)), pltpu.SemaphoreType.DMA((n,))])(*qs)


def _adamw_update(w, g, m, v):
    m = ADAM_B1 * m + (1.0 - ADAM_B1) * g
    v = ADAM_B2 * v + (1.0 - ADAM_B2) * jnp.square(g)
    m_hat = m / (1.0 - ADAM_B1 ** ADAM_STEP)
    v_hat = v / (1.0 - ADAM_B2 ** ADAM_STEP)
    return -ADAM_LR * (m_hat / (jnp.sqrt(v_hat) + ADAM_EPS) + ADAM_WD * w), m, v


def _adamw(w, g_parts, m, v, name):
    shape = w.shape
    cols = shape[-1]
    rows = _size(shape[:-1])
    tr = 512 if rows % 512 == 0 else rows
    spec = pl.BlockSpec((tr, cols), lambda i: (i, 0))
    n = len(g_parts)
    n_out = 4 if n > 1 else 3

    def body(*refs):
        w_ref, m_ref, v_ref = refs[:3]
        d_ref, nm_ref, nv_ref = refs[-3:]
        g = refs[3][...]
        for r in refs[4:3 + n]:
            g = g + r[...]
        if n > 1:
            refs[3 + n][...] = g
        d_ref[...], nm_ref[...], nv_ref[...] = _adamw_update(w_ref[...], g, m_ref[...], v_ref[...])

    outs = pl.pallas_call(
        body, name="adamw_" + name, grid=(rows // tr,), in_specs=[spec] * (3 + n), out_specs=[spec] * n_out,
        out_shape=[_sds((rows, cols))] * n_out, compiler_params=_cp(1))(
            *[a.reshape(rows, cols) for a in (w, m, v, *g_parts)])
    outs = tuple(o.reshape(shape) for o in outs)
    return outs if n > 1 else (g_parts[0],) + outs


def _adamw_layer(w, g_parts, m, v, layer, prev, name):
    _, r, cc = w.shape
    tr = 512 if r % 512 == 0 else r
    spec = pl.BlockSpec((1, tr, cc), lambda i: (layer, i, 0))
    n = len(g_parts)

    def body(*refs):
        w_ref, m_ref, v_ref = refs[:3]
        g_ref, d_ref, nm_ref, nv_ref = refs[-4:]
        g = refs[3][...]
        for q in refs[4:3 + n]:
            g = g + q[...]
        g = g[:, :cc]
        g_ref[0] = g
        d_ref[0], nm_ref[0], nv_ref[0] = _adamw_update(w_ref[0], g, m_ref[0], v_ref[0])

    g_specs = [pl.BlockSpec((tr, q.shape[1]), lambda i: (i, 0)) for q in g_parts]
    passed = () if prev is None else tuple(prev)
    return pl.pallas_call(
        body, name="adamw_" + name, grid=(r // tr,),
        in_specs=[spec] * 3 + g_specs + [_HBM_SPEC] * len(passed), out_specs=[spec] * 4,
        out_shape=[_sds(w.shape)] * 4, input_output_aliases={3 + n + k: k for k in range(len(passed))},
        compiler_params=_cp(1))(w, m, v, *g_parts, *passed)


def _size(shape):
    n = 1
    for s in shape:
        n *= s
    return n


_SMALL = (("dmod", (DEPTH, 3 * D_MODEL)), ("pre_norm_g", (DEPTH, D_MODEL)), ("post_norm_g", (DEPTH, D_MODEL)),
          ("even_sc_conv_w", (2, SC_KERNEL, SC_WIDTH)), ("even_sc_conv_b", (2, SC_WIDTH)),
          ("even_q_norm_g", (2, Q_LORA)), ("even_kv_norm_g", (2, KV_LORA)),
          ("odd_conv_w", (2, CONF_KERNEL, D_MODEL)), ("odd_conv_b", (2, D_MODEL)), ("odd_ln_g", (2, D_MODEL)),
          ("odd_ln_b", (2, D_MODEL)))
SMALL_ROWS = -(-sum(_size(s) for _, s in _SMALL) // (8 * 128)) * 8

_SMALL_W = (("even_sc_conv_w", (2, SC_KERNEL, SC_WIDTH // N_CHIPS)), ("odd_conv_w", (2, CONF_KERNEL, D_MODEL // N_CHIPS)),
            ("odd_conv_b", (2, D_MODEL // N_CHIPS)), ("odd_ln_g", (2, D_MODEL // N_CHIPS)),
            ("odd_ln_b", (2, D_MODEL // N_CHIPS)))
SMALL_W_ROWS = -(-sum(_size(s) for _, s in _SMALL_W) // (8 * 128)) * 8


def _pack_rows(arrays, layout, rows):
    flat = jnp.concatenate([arrays[n].reshape(-1) for n, _ in layout])
    return jnp.pad(flat, (0, rows * 128 - flat.shape[0])).reshape(rows, 128)


def _unpack_small(t):
    flat = t.reshape(-1)
    out, at = {}, 0
    for n, shape in _SMALL:
        out[n] = flat[at:at + _size(shape)].reshape(shape)
        at += _size(shape)
    return out


def _unpack_small_w(t):
    flat = t.reshape(N_CHIPS, -1)
    out, at = {}, 0
    for n, shape in _SMALL_W:
        a = flat[:, at:at + _size(shape)].reshape((N_CHIPS,) + shape)
        out[n] = jnp.moveaxis(a, 0, -2).reshape(shape[:-1] + (N_CHIPS * shape[-1],))
        at += _size(shape)
    return out


def _chip_cols(a, chip):
    n = a.shape[-1] // N_CHIPS
    return lax.dynamic_slice_in_dim(a, chip * n, n, axis=a.ndim - 1)


WEIGHT_NAMES = ("ada_w", "ada_b", "pre_norm_g", "post_norm_g", "even_w_in", "even_sc_conv_w", "even_sc_conv_b",
                "even_q_norm_g", "even_kv_norm_g", "even_w_uq", "even_w_ukv", "even_w_out", "odd_w_in", "odd_conv_w",
                "odd_conv_b", "odd_ln_g", "odd_ln_b", "odd_w_out")
GATHER_HOW = ((("even_w_in", "slot"), ("even_w_uq", "slot"), ("even_w_ukv", "slot"), ("even_w_out", "rows")),
              (("odd_w_in", "cols"), ("odd_w_out", "rows")))


def kernel(x, c, positions, ada_w, ada_b, pre_norm_g, post_norm_g, even_w_in, even_sc_conv_w, even_sc_conv_b, even_q_norm_g, even_kv_norm_g, even_w_uq, even_w_ukv, even_w_out, odd_w_in, odd_conv_w, odd_conv_b, odd_ln_g, odd_ln_b, odd_w_out, loss_target, m_ada_w, m_ada_b, m_pre_norm_g, m_post_norm_g, m_even_w_in, m_even_sc_conv_w, m_even_sc_conv_b, m_even_q_norm_g, m_even_kv_norm_g, m_even_w_uq, m_even_w_ukv, m_even_w_out, m_odd_w_in, m_odd_conv_w, m_odd_conv_b, m_odd_ln_g, m_odd_ln_b, m_odd_w_out, v_ada_w, v_ada_b, v_pre_norm_g, v_post_norm_g, v_even_w_in, v_even_sc_conv_w, v_even_sc_conv_b, v_even_q_norm_g, v_even_kv_norm_g, v_even_w_uq, v_even_w_ukv, v_even_w_out, v_odd_w_in, v_odd_conv_w, v_odd_conv_b, v_odd_ln_g, v_odd_ln_b, v_odd_w_out):
    w = dict(zip(WEIGHT_NAMES, (ada_w, ada_b, pre_norm_g, post_norm_g, even_w_in, even_sc_conv_w, even_sc_conv_b,
                                even_q_norm_g, even_kv_norm_g, even_w_uq, even_w_ukv, even_w_out, odd_w_in, odd_conv_w,
                                odd_conv_b, odd_ln_g, odd_ln_b, odd_w_out)))
    m = dict(zip(WEIGHT_NAMES, (m_ada_w, m_ada_b, m_pre_norm_g, m_post_norm_g, m_even_w_in, m_even_sc_conv_w,
                                m_even_sc_conv_b, m_even_q_norm_g, m_even_kv_norm_g, m_even_w_uq, m_even_w_ukv,
                                m_even_w_out, m_odd_w_in, m_odd_conv_w, m_odd_conv_b, m_odd_ln_g, m_odd_ln_b, m_odd_w_out)))
    v = dict(zip(WEIGHT_NAMES, (v_ada_w, v_ada_b, v_pre_norm_g, v_post_norm_g, v_even_w_in, v_even_sc_conv_w,
                                v_even_sc_conv_b, v_even_q_norm_g, v_even_kv_norm_g, v_even_w_uq, v_even_w_ukv,
                                v_even_w_out, v_odd_w_in, v_odd_conv_w, v_odd_conv_b, v_odd_ln_g, v_odd_ln_b, v_odd_w_out)))
    ix, iy, ic = _place()
    chip = 2 * ix + iy
    me = 2 * chip + ic
    s = x.shape[1]

    c_all, mod_all = _ada_fwd(jnp.broadcast_to(c, (8, D_MODEL)), ada_w, _chip_cols(ada_b, chip))
    mod = lax.dynamic_index_in_dim(mod_all, me, axis=2, keepdims=False)
    mod = mod.transpose(1, 0, 2).reshape(DEPTH, 3 * D_MODEL)

    items = [[(w[n][layer // 2].astype(MXU_DTYPE), how) for n, how in GATHER_HOW[layer % 2]] for layer in range(DEPTH)]
    groups = [items[0][:1], items[0][1:] + [(_pack_rows(w, _SMALL_W, SMALL_W_ROWS), "slot")],
              [item for layer_items in items[1:] for item in layer_items]]
    sent, dep = [], mod_all
    for number, group in enumerate(groups):
        sent.append(_gather_start(group, [_place_own(a, how, chip.reshape(1)) for a, how in group],
                                  "gather_start_%d" % number, [dep]))
        dep = sent[-1][-1]
    arrived = {}

    def group(number, after):
        if number not in arrived:
            arrived[number] = _gather_wait(groups[number], sent[number], after, "gather_wait_%d" % number)
        return arrived[number]

    def even_rest(i, uq, ukv, eout, small_w):
        wuk, wuv = _ukv_to_heads(ukv)
        wq, wq_rot = _uq_to_heads(uq)
        return {"wq": wq, "wq_rot": wq_rot, "wuk": wuk, "wuv": wuv, "w_out": eout, "sc_conv_w": small_w["even_sc_conv_w"][i]}

    def layer_weights(layer, h):
        i = layer // 2
        if layer == 0:
            def late(z):
                uq, ukv, eout, small = group(1, [z])
                return even_rest(i, uq, ukv, eout, _unpack_small_w(small))
            return {"w_in": _ein_from_shards(group(0, [h])[0]), "late": late}
        small_w = _unpack_small_w(group(1, [h])[-1])
        at = sum(len(layer_items) for layer_items in items[1:layer])
        arrays = group(2, [h])[at:at + len(items[layer])]
        if layer % 2 == 0:
            return {"w_in": _ein_from_shards(arrays[0]), **even_rest(i, *arrays[1:], small_w)}
        oin, oout = arrays
        return {"w_in": oin, "w_out": oout, "conv_w": small_w["odd_conv_w"][i], "conv_b": small_w["odd_conv_b"][i:i + 1],
                "ln_g": small_w["odd_ln_g"][i:i + 1], "ln_b": small_w["odd_ln_b"][i:i + 1]}

    in_flight, own, sib, last = {}, {}, {}, {}

    def land(layer, after):
        names, started, kept = in_flight.pop(layer)
        bufs, arrived = _rs_wait(started, after, "rs_wait_%d" % layer)
        sums = [_add_chips(b, t, chip.reshape(1)) for b, t in zip(bufs if kept is None else kept, arrived)]
        for n, mine, theirs in zip(names, sums, _rs_sibling(sums)):
            own[n, layer // 2], sib[n, layer // 2] = mine, theirs

    def grads_done(layer, bufs, dx_in):
        if layer + 1 in in_flight:
            land(layer + 1, [dx_in])
        if layer == 0:
            last.update(bufs)
            return None
        names = sorted(bufs)
        in_flight[layer] = (names, _rs_start([bufs[n] for n in names], "rs_start_%d" % layer), None)
        return in_flight[layer][1][-1]

    p = {"pre_norm_g": pre_norm_g, "post_norm_g": post_norm_g, "even_sc_conv_b": even_sc_conv_b,
         "even_q_norm_g": even_q_norm_g, "even_kv_norm_g": even_kv_norm_g}
    inv_freq = 1.0 / (ROPE_THETA ** (jnp.arange(0, QK_ROPE, 2, dtype=F32) / QK_ROPE))
    inv_freq = jnp.zeros((1, HEAD_PAD), F32).at[0, QK_NOPE:QK_NOPE + QK_ROPE].set(jnp.tile(inv_freq, 2))
    cos, sin = _rope_tables(positions.reshape(s, 1), inv_freq)

    loss, dx, g = _local_step(x[0], loss_target[0], cos, sin, mod, p, layer_weights, dep, grads_done)

    grads, deltas, new_m, new_v = {}, {}, {}, {}

    def update_layers(n, results, pairs):
        for i in pairs:
            results = _adamw_layer(w[n], [own[n, i], sib[n, i]], m[n], v[n], i, results, n)
        return results

    small_all, small_sum = _gather_sum_all(_pack_rows(g, _SMALL, SMALL_ROWS))
    names = sorted(last)
    kept = [last[n] for n in names]
    in_flight[0] = (names, _rs_start([b.astype(jnp.bfloat16) for b in kept], "rs_start_0", [small_sum]), kept)
    tot = _unpack_small(small_sum)
    dmod_all = small_all[:, :DEPTH * 3 * D_MODEL // 128].reshape(N_DEV, DEPTH, 3 * D_MODEL)
    grads["ada_w"] = _ada_bwd(c_all[:, 0, :].T, _chip_cols(dmod_all, chip).transpose(1, 0, 2))
    grads["ada_b"] = tot["dmod"]
    for n in ("pre_norm_g", "post_norm_g", "even_sc_conv_b", "even_q_norm_g", "even_kv_norm_g"):
        grads[n] = tot[n]
    for n in ("even_sc_conv_w", "odd_conv_w", "odd_conv_b", "odd_ln_g", "odd_ln_b"):
        grads[n] = _chip_cols(tot[n], chip)
    for n in list(grads):
        _, deltas[n], new_m[n], new_v[n] = _adamw(w[n], [grads[n]], m[n], v[n], n)

    for n in ("odd_w_in", "odd_w_out"):
        grads[n], deltas[n], new_m[n], new_v[n] = update_layers(n, None, (1, 0))
    partly = {n: update_layers(n, None, (1,)) for n in ("even_w_in", "even_w_out")}
    land(0, [deltas["ada_w"], deltas["odd_w_in"], partly["even_w_in"][1]])
    for n in ("even_w_in", "even_w_out"):
        grads[n], deltas[n], new_m[n], new_v[n] = update_layers(n, partly[n], (0,))
    uq_parts, ukv_parts = zip(*[[jnp.stack(part) for part in zip(*[_mla_local(q["even_mla", i]) for i in range(N_PAIRS)])]
                                for q in (own, sib)])
    for n, parts in (("even_w_uq", uq_parts), ("even_w_ukv", ukv_parts)):
        grads[n], deltas[n], new_m[n], new_v[n] = _adamw(w[n], list(parts), m[n], v[n], n)

    total_loss = lax.psum(loss[0, 0], ("x", "y", "c"))
    return (total_loss, dx[None], *[grads[n] for n in WEIGHT_NAMES], *[deltas[n] for n in WEIGHT_NAMES],
            *[new_m[n] for n in WEIGHT_NAMES], *[new_v[n] for n in WEIGHT_NAMES])
```

```python
import jax
import jax.numpy as jnp
from jax import lax
from jax.experimental import pallas as pl
from jax.experimental.pallas import tpu as pltpu

F32 = jnp.float32
MXU_DTYPE = jnp.bfloat16
MESH = pl.DeviceIdType.MESH
VMEM_LIMIT_V7X = 56 * 2 ** 20

EPS = 1e-6
D_MODEL = 1024
DEPTH = 4
CHUNK = 64
SC_WIDTH = 512
SC_KERNEL = 3
SC_HALO = 8
HEADS = 8
QK_NOPE = 64
QK_ROPE = 32
V_HEAD = 64
HEAD_PAD = 128
Q_LORA = 256
KV_LORA = 128
ROPE_THETA = 10000.0
CONF_KERNEL = 31
CONF_HALO = 32
CONV_ROWS = 64
SUBLANES = 8
EVEN_IN = 2976
EVEN_PAD = 3072
ODD_IN = 3072
N_CHIPS = 4
N_DEV = 8
NEG = -1e30

ADAM_LR = 0.001
ADAM_B1 = 0.9
ADAM_B2 = 0.999
ADAM_EPS = 1e-08
ADAM_WD = 0.01
ADAM_STEP = 10

N_PAIRS = DEPTH // 2
EVEN_SHARD = EVEN_IN // N_CHIPS
EVEN_SHARD_PAD = 768
MLA_ROWS = Q_LORA + 2 * KV_LORA


def _cp(n_grid=0, **kw):
    return pltpu.CompilerParams(dimension_semantics=("arbitrary",) * n_grid,
                                vmem_limit_bytes=VMEM_LIMIT_V7X, **kw)


def _sigmoid(x):
    return 1.0 / (1.0 + jnp.exp(-x))


def _silu(x):
    return x * _sigmoid(x)


def _dsilu(x):
    s = _sigmoid(x)
    return s * (1.0 + x * (1.0 - s))


def _rms(x, g):
    return x * lax.rsqrt(jnp.mean(x * x, axis=-1, keepdims=True) + EPS) * g


def _dot(a, b, dims):
    return lax.dot_general(a.astype(MXU_DTYPE), b.astype(MXU_DTYPE), (dims, ((), ())),
                           preferred_element_type=F32)


def _dot_nn(a, b):
    return _dot(a, b, ((1,), (0,)))


def _dot_nt(a, b):
    return _dot(a, b, ((1,), (1,)))


def _dot_tn(a, b):
    return _dot(a, b, ((0,), (0,)))


def _rows(ts, w, cb=0):
    return pl.BlockSpec((ts, w), lambda i: (i, cb))


def _vec(w, cb=0, r=1):
    return pl.BlockSpec((r, w), lambda i: (0, cb))


def _prev_halo(ts, hr, w, cb):
    return pl.BlockSpec((hr, w), lambda i: (jnp.maximum(i * (ts // hr) - 1, 0), cb))


def _next_halo(ts, hr, w, cb, s):
    return pl.BlockSpec((hr, w), lambda i: (jnp.minimum((i + 1) * (ts // hr), s // hr - 1), cb))


def _sds(shape, dtype=F32):
    return jax.ShapeDtypeStruct(shape, dtype)


def _mm(a, b, mode, out_dtype, tm, tn, name):
    tm = min(tm, a.shape[1] if mode == "tn" else a.shape[0])
    tn = min(tn, b.shape[0] if mode == "nt" else b.shape[1])
    if mode == "nn":
        (m, k), n = a.shape, b.shape[1]
        a_spec = pl.BlockSpec((tm, k), lambda i, j: (i, 0))
        b_spec = pl.BlockSpec((k, tn), lambda i, j: (0, j))
        dot = _dot_nn
    elif mode == "nt":
        (m, k), n = a.shape, b.shape[0]
        a_spec = pl.BlockSpec((tm, k), lambda i, j: (i, 0))
        b_spec = pl.BlockSpec((tn, k), lambda i, j: (j, 0))
        dot = _dot_nt
    else:
        (k, m), n = a.shape, b.shape[1]
        a_spec = pl.BlockSpec((k, tm), lambda i, j: (0, i))
        b_spec = pl.BlockSpec((k, tn), lambda i, j: (0, j))
        dot = _dot_tn
    assert m % tm == 0 and n % tn == 0, (name, m, n, tm, tn)

    def body(a_ref, b_ref, o_ref):
        o_ref[...] = dot(a_ref[...], b_ref[...]).astype(o_ref.dtype)

    return pl.pallas_call(
        body, name=name, grid=(m // tm, n // tn), in_specs=[a_spec, b_spec],
        out_specs=pl.BlockSpec((tm, tn), lambda i, j: (i, j)), out_shape=_sds((m, n), out_dtype),
        compiler_params=_cp(2))(a, b)


def _mm_tn_shards(a, b, by, name):
    k, m = a.shape
    n = b.shape[1]
    if by == "cols":
        tm, tn = m, n // N_CHIPS
        shape, grid = (N_CHIPS, m, tn), (1, N_CHIPS)
        out_spec = pl.BlockSpec((1, tm, tn), lambda i, j: (j, i, 0))
    else:
        tm, tn = m // N_CHIPS, n
        shape, grid = (N_CHIPS, tm, n), (N_CHIPS, 1)
        out_spec = pl.BlockSpec((1, tm, tn), lambda i, j: (i, 0, j))

    def body(a_ref, b_ref, o_ref):
        o_ref[0] = _dot_tn(a_ref[...], b_ref[...])

    return pl.pallas_call(
        body, name=name, grid=grid,
        in_specs=[pl.BlockSpec((k, tm), lambda i, j: (0, i)), pl.BlockSpec((k, tn), lambda i, j: (0, j))],
        out_specs=out_spec, out_shape=_sds(shape), compiler_params=_cp(2))(a, b)


def _even_col(q):
    return q if q < 2432 else (q + 64 if q < 2464 else q + 96)


def _shard_pieces(j):
    lo, hi = EVEN_SHARD * j, EVEN_SHARD * (j + 1)
    cuts = [lo] + [b for b in (2432, 2464) if lo < b < hi] + [hi]
    return [(a - lo, _even_col(a), b - a) for a, b in zip(cuts[:-1], cuts[1:])]


def _ein_from_shards(w):
    _, d, _ = w.shape
    tr = 256

    def body(w_ref, o_ref):
        parts, at = [], 0
        for j in range(N_CHIPS):
            for d0, s0, n in _shard_pieces(j):
                if s0 > at:
                    parts.append(jnp.zeros((tr, s0 - at), F32))
                parts.append(w_ref[j, :, d0:d0 + n].astype(F32))
                at = s0 + n
        o_ref[...] = jnp.concatenate(parts, axis=1).astype(o_ref.dtype)

    return pl.pallas_call(
        body, name="ein_from_shards", grid=(d // tr,),
        in_specs=[pl.BlockSpec((N_CHIPS, tr, EVEN_SHARD), lambda i: (0, i, 0))],
        out_specs=_rows(tr, EVEN_PAD), out_shape=_sds((d, EVEN_PAD), w.dtype), compiler_params=_cp(1))(w)


def _ein_to_shards(dw):
    d = dw.shape[0]
    tr = 256

    def body(dw_ref, o_ref):
        for j in range(N_CHIPS):
            parts = [dw_ref[:, s0:s0 + n] for _, s0, n in _shard_pieces(j)]
            o_ref[j] = jnp.concatenate(parts + [jnp.zeros((tr, EVEN_SHARD_PAD - EVEN_SHARD), F32)], axis=1)

    return pl.pallas_call(
        body, name="ein_to_shards", grid=(d // tr,), in_specs=[_rows(tr, EVEN_PAD)],
        out_specs=pl.BlockSpec((N_CHIPS, tr, EVEN_SHARD_PAD), lambda i: (0, i, 0)),
        out_shape=_sds((N_CHIPS, d, EVEN_SHARD_PAD)), compiler_params=_cp(1))(dw)


def _rope_tables(pos_col, invf):
    s = pos_col.shape[0]
    ts = min(512, s)

    def body(p_ref, f_ref, c_ref, s_ref):
        ang = p_ref[...].astype(F32) * f_ref[...]
        lane = lax.broadcasted_iota(jnp.int32, ang.shape, 1)
        rope = (lane >= QK_NOPE) & (lane < QK_NOPE + QK_ROPE)
        c_ref[...] = jnp.where(lane < QK_NOPE, 1.0, jnp.where(rope, jnp.cos(ang), 0.0))
        s_ref[...] = jnp.where(rope, jnp.sin(ang), 0.0)

    return pl.pallas_call(
        body, name="rope_tables", grid=(s // ts,), in_specs=[_rows(ts, 1), _vec(HEAD_PAD)],
        out_specs=[_rows(ts, HEAD_PAD)] * 2, out_shape=[_sds((s, HEAD_PAD))] * 2,
        compiler_params=_cp(1))(pos_col, invf)


def _after(dep):
    return () if dep is None else (dep,)


def _pre_fwd(x, g, mod_l, ts, dep=None):
    s, d = x.shape

    def body(x_ref, g_ref, sh_ref, sc_ref, *rest):
        h = _rms(x_ref[...], g_ref[...]) * (1.0 + sc_ref[...]) + sh_ref[...]
        rest[-1][...] = h.astype(rest[-1].dtype)

    return pl.pallas_call(
        body, name="pre_fwd", grid=(s // ts,),
        in_specs=[_rows(ts, d), _vec(d), _vec(d, 0), _vec(d, 1)] + [_HBM_SPEC] * len(_after(dep)),
        out_specs=_rows(ts, d), out_shape=_sds((s, d), MXU_DTYPE), compiler_params=_cp(1))(
            x, g, mod_l, mod_l, *_after(dep))


def _pre_bwd(dz, w_in, dx_out, x, g, mod_l, ts):
    s, d = x.shape
    n_in = dz.shape[1]

    def f(xv, gv, sh, sc):
        return _rms(xv, gv) * (1.0 + sc) + sh

    def body(dz_ref, w_ref, dxo_ref, x_ref, g_ref, sh_ref, sc_ref, dx_ref, dsh_ref, dsc_ref, dg_ref):
        @pl.when(pl.program_id(0) == 0)
        def _():
            dsh_ref[...] = jnp.zeros_like(dsh_ref)
            dsc_ref[...] = jnp.zeros_like(dsc_ref)
            dg_ref[...] = jnp.zeros_like(dg_ref)

        _, vjp = jax.vjp(f, x_ref[...], g_ref[...], sh_ref[...], sc_ref[...])
        dx, dg, dsh, dsc = vjp(_dot_nt(dz_ref[...], w_ref[...]))
        dx_ref[...] = dxo_ref[...] + dx
        dsh_ref[...] += dsh
        dsc_ref[...] += dsc
        dg_ref[...] += dg

    return pl.pallas_call(
        body, name="pre_bwd", grid=(s // ts,),
        in_specs=[_rows(ts, n_in), _vec(n_in, 0, d), _rows(ts, d), _rows(ts, d), _vec(d), _vec(d, 0), _vec(d, 1)],
        out_specs=[_rows(ts, d), _vec(d), _vec(d), _vec(d)],
        out_shape=[_sds((s, d)), _sds((1, d)), _sds((1, d)), _sds((1, d))],
        compiler_params=_cp(1))(dz, w_in, dx_out, x, g, mod_l, mod_l)


def _post_pre_fwd(x, yo, g_post, mod_l, g_pre, mod_next, ts):
    s, d = x.shape

    def body(x_ref, yo_ref, gp_ref, gate_ref, g_ref, sh_ref, sc_ref, x_out_ref, h_ref):
        x_new = x_ref[...] + gate_ref[...] * _rms(yo_ref[...], gp_ref[...])
        x_out_ref[...] = x_new
        h_ref[...] = (_rms(x_new, g_ref[...]) * (1.0 + sc_ref[...]) + sh_ref[...]).astype(h_ref.dtype)

    return pl.pallas_call(
        body, name="post_pre_fwd", grid=(s // ts,),
        in_specs=[_rows(ts, d), _rows(ts, d), _vec(d), _vec(d, 2), _vec(d), _vec(d, 0), _vec(d, 1)],
        out_specs=[_rows(ts, d), _rows(ts, d)], out_shape=[_sds((s, d)), _sds((s, d), MXU_DTYPE)],
        compiler_params=_cp(1))(x, yo, g_post, mod_l, g_pre, mod_next, mod_next)


def _post_loss(x, yo, g_post, mod_l, target, ts):
    s, d = x.shape

    def body(x_ref, yo_ref, gp_ref, gate_ref, t_ref, loss_ref, dx_ref):
        err = x_ref[...] + gate_ref[...] * _rms(yo_ref[...], gp_ref[...]) - t_ref[...]
        dx_ref[...] = err * (1.0 / d)

        @pl.when(pl.program_id(0) == 0)
        def _():
            loss_ref[...] = jnp.zeros_like(loss_ref)

        loss_ref[...] += 0.5 * jnp.sum(jnp.sum(err * err, axis=-1, keepdims=True) * (1.0 / d), axis=0, keepdims=True)

    return pl.pallas_call(
        body, name="post_loss", grid=(s // ts,),
        in_specs=[_rows(ts, d), _rows(ts, d), _vec(d), _vec(d, 2), _rows(ts, d)],
        out_specs=[_vec(1), _rows(ts, d)], out_shape=[_sds((1, 1)), _sds((s, d))],
        compiler_params=_cp(1))(x, yo, g_post, mod_l, target)


def _post_bwd(dx_out, yo, g, mod_l, ts, dep=None):
    s, d = yo.shape

    def f(yov, gv, gate):
        return gate * _rms(yov, gv)

    def body(dx_ref, yo_ref, g_ref, gate_ref, *rest):
        dyo_ref, dgate_ref, dg_ref = rest[-3:]
        i = pl.program_id(0)
        _, vjp = jax.vjp(f, yo_ref[...], g_ref[...], gate_ref[...])
        dyo, dg, dgate = vjp(dx_ref[...])
        dyo_ref[...] = dyo.astype(dyo_ref.dtype)

        @pl.when(i == 0)
        def _():
            dgate_ref[...] = jnp.zeros_like(dgate_ref)
            dg_ref[...] = jnp.zeros_like(dg_ref)

        dgate_ref[...] += dgate
        dg_ref[...] += dg

    return pl.pallas_call(
        body, name="post_bwd", grid=(s // ts,),
        in_specs=[_rows(ts, d), _rows(ts, d), _vec(d), _vec(d, 2)] + [_HBM_SPEC] * len(_after(dep)),
        out_specs=[_rows(ts, d), _vec(d), _vec(d)],
        out_shape=[_sds((s, d), MXU_DTYPE), _sds((1, d)), _sds((1, d))],
        compiler_params=_cp(1))(dx_out, yo, g, mod_l, *_after(dep))


def _rope(t, cos, sin):
    lane = lax.broadcasted_iota(jnp.int32, t.shape, 1)
    first = (lane >= QK_NOPE) & (lane < QK_NOPE + QK_ROPE // 2)
    second = (lane >= QK_NOPE + QK_ROPE // 2) & (lane < QK_NOPE + QK_ROPE)
    up = pltpu.roll(t, QK_ROPE // 2, 1)
    down = pltpu.roll(t, HEAD_PAD - QK_ROPE // 2, 1)
    return t * cos + jnp.where(first, -down, jnp.where(second, up, 0.0)) * sin


def _rope_transposed(g, cos, sin):
    lane = lax.broadcasted_iota(jnp.int32, g.shape, 1)
    first = (lane >= QK_NOPE) & (lane < QK_NOPE + QK_ROPE // 2)
    second = (lane >= QK_NOPE + QK_ROPE // 2) & (lane < QK_NOPE + QK_ROPE)
    u = g * sin
    up = pltpu.roll(u, QK_ROPE // 2, 1)
    down = pltpu.roll(u, HEAD_PAD - QK_ROPE // 2, 1)
    return g * cos + jnp.where(first, down, jnp.where(second, -up, 0.0))


def _mla_prep_fwd(z, cos, sin, qg, kvg, wq, wq_rot, wuk, wuv, ts):
    s = z.shape[0]
    wide = HEADS * HEAD_PAD

    def body(cq_ref, ckv_ref, kr_ref, cos_ref, sin_ref, qg_ref, kvg_ref, wq_ref, wqr_ref, wuk_ref, wuv_ref,
             q_ref, qt_ref, k_ref, v_ref):
        cos_v, sin_v = cos_ref[...], sin_ref[...]
        cqn = _rms(cq_ref[...], qg_ref[...])
        ckvn = _rms(ckv_ref[...], kvg_ref[...])
        kr = _rope(kr_ref[...], cos_v, sin_v)
        q_lin, q_rot = _dot_nn(cqn, wq_ref[...]), _dot_nn(cqn, wqr_ref[...])
        k_lin, v_all = _dot_nn(ckvn, wuk_ref[...]), _dot_nn(ckvn, wuv_ref[...])
        for h in range(HEADS):
            lanes = slice(h * HEAD_PAD, (h + 1) * HEAD_PAD)
            qh = q_lin[:, lanes] * cos_v + q_rot[:, lanes] * sin_v
            q_ref[h] = qh.astype(q_ref.dtype)
            qt_ref[h, 0] = qh.T.astype(qt_ref.dtype)
            k_ref[h] = (k_lin[:, lanes] + kr).astype(k_ref.dtype)
            v_ref[h] = v_all[:, lanes].astype(v_ref.dtype)

    out = pl.BlockSpec((HEADS, ts, HEAD_PAD), lambda i: (0, i, 0))
    return pl.pallas_call(
        body, name="mla_prep_fwd", grid=(s // ts,),
        in_specs=[_rows(ts, Q_LORA, 8), _rows(ts, KV_LORA, 18), _rows(ts, HEAD_PAD, 19), _rows(ts, HEAD_PAD), _rows(ts, HEAD_PAD),
                  _vec(Q_LORA), _vec(KV_LORA), _vec(wide, 0, Q_LORA), _vec(wide, 0, Q_LORA), _vec(wide, 0, KV_LORA),
                  _vec(wide, 0, KV_LORA)],
        out_specs=[out, pl.BlockSpec((HEADS, 1, HEAD_PAD, ts), lambda i: (0, i, 0, 0)), out, out],
        out_shape=[_sds((HEADS, s, HEAD_PAD), MXU_DTYPE), _sds((HEADS, s // ts, HEAD_PAD, ts), MXU_DTYPE)]
        + [_sds((HEADS, s, HEAD_PAD), MXU_DTYPE)] * 2,
        compiler_params=_cp(1))(z, z, z, cos, sin, qg, kvg, wq, wq_rot, wuk, wuv)


def _mla_prep_bwd(dz, dq, dk, dv, z, cos, sin, qg, kvg, wq, wuk, wuv, ts):
    s = z.shape[0]

    def fq(cq, g):
        return _rms(cq, g)

    def body(dz_in_ref, dq_ref, dk_ref, dv_ref, cq_ref, ckv_ref, cos_ref, sin_ref, qg_ref, kvg_ref, wq_ref, wuk_ref,
             wuv_ref, dz_ref, dw_ref, dqg_ref, dkvg_ref):
        del dz_in_ref
        cos_v, sin_v = cos_ref[...], sin_ref[...]

        @pl.when(pl.program_id(0) == 0)
        def _():
            dw_ref[...] = jnp.zeros_like(dw_ref)
            dqg_ref[...] = jnp.zeros_like(dqg_ref)
            dkvg_ref[...] = jnp.zeros_like(dkvg_ref)

        cqn, vjp_q = jax.vjp(fq, cq_ref[...], qg_ref[...])
        ckvn, vjp_kv = jax.vjp(fq, ckv_ref[...], kvg_ref[...])
        lane = lax.broadcasted_iota(jnp.int32, (ts, HEAD_PAD), 1)
        rope_lanes = (lane >= QK_NOPE) & (lane < QK_NOPE + QK_ROPE)
        dq_lin = jnp.concatenate([_rope_transposed(dq_ref[h], cos_v, sin_v).astype(MXU_DTYPE) for h in range(HEADS)], axis=1)
        dk_all = jnp.concatenate([dk_ref[h].astype(MXU_DTYPE) for h in range(HEADS)], axis=1)
        dv_all = jnp.concatenate([dv_ref[h].astype(MXU_DTYPE) for h in range(HEADS)], axis=1)
        dkr = jnp.where(rope_lanes, dk_ref[0], 0.0)
        for h in range(1, HEADS):
            dkr = dkr + jnp.where(rope_lanes, dk_ref[h], 0.0)
        dcq, dqg = vjp_q(_dot_nt(dq_lin, wq_ref[...]))
        dckv, dkvg = vjp_kv(_dot_nt(dk_all, wuk_ref[...]) + _dot_nt(dv_all, wuv_ref[...]))
        dz_ref[:, 0:Q_LORA] = dcq.astype(dz_ref.dtype)
        dz_ref[:, Q_LORA:Q_LORA + KV_LORA] = dckv.astype(dz_ref.dtype)
        dz_ref[:, Q_LORA + KV_LORA:] = _rope_transposed(dkr, cos_v, sin_v).astype(dz_ref.dtype)
        dqg_ref[...] += dqg
        dkvg_ref[...] += dkvg
        dwq, dwuk, dwuv = _dot_tn(cqn, dq_lin), _dot_tn(ckvn, dk_all), _dot_tn(ckvn, dv_all)
        for h in range(HEADS):
            lanes = slice(h * HEAD_PAD, (h + 1) * HEAD_PAD)
            row0 = (h % 2) * MLA_ROWS
            dw_ref[h // 2, row0:row0 + Q_LORA, :] += dwq[:, lanes]
            dw_ref[h // 2, row0 + Q_LORA:row0 + Q_LORA + KV_LORA, :] += dwuk[:, lanes]
            dw_ref[h // 2, row0 + Q_LORA + KV_LORA:row0 + MLA_ROWS, :] += dwuv[:, lanes]

    wide = HEADS * HEAD_PAD
    heads = pl.BlockSpec((HEADS, ts, HEAD_PAD), lambda i: (0, i, 0))
    whole = pl.BlockSpec((N_CHIPS, 2 * MLA_ROWS, HEAD_PAD), lambda i: (0, 0, 0))
    return pl.pallas_call(
        body, name="mla_prep_bwd", grid=(s // ts,),
        in_specs=[_HBM_SPEC, heads, heads, heads, _rows(ts, Q_LORA, 8), _rows(ts, KV_LORA, 18),
                  _rows(ts, HEAD_PAD), _rows(ts, HEAD_PAD), _vec(Q_LORA), _vec(KV_LORA), _vec(wide, 0, Q_LORA),
                  _vec(wide, 0, KV_LORA), _vec(wide, 0, KV_LORA)],
        out_specs=[_rows(ts, 512, 4), whole, _vec(Q_LORA), _vec(KV_LORA)],
        out_shape=[_sds(dz.shape, dz.dtype), _sds((N_CHIPS, 2 * MLA_ROWS, HEAD_PAD)), _sds((1, Q_LORA)), _sds((1, KV_LORA))],
        input_output_aliases={0: 0}, compiler_params=_cp(1))(dz, dq, dk, dv, z, z, cos, sin, qg, kvg, wq, wuk, wuv)


def _chunk_mask(q0, k0, tq, tk):
    rows = q0 + lax.broadcasted_iota(jnp.int32, (tq, tk), 0)
    cols = k0 + lax.broadcasted_iota(jnp.int32, (tq, tk), 1)
    shift = CHUNK.bit_length() - 1
    return lax.shift_right_logical(cols, shift) <= lax.shift_right_logical(rows, shift)


def _attn_fwd(q, k, v, tq):
    s = q.shape[1]
    nq = s // tq
    scale = 1.0 / float(QK_NOPE + QK_ROPE) ** 0.5

    assert nq % 2 == 0, (s, tq)

    def body(q_ref, k_ref, v_ref, o_ref, lse_ref):
        pair, hh = pl.program_id(1), pl.program_id(2)

        def step(qv, q0, kj, carry, masked):
            m, l, acc = carry
            k0 = pl.multiple_of(kj * tq, tq)
            sc = _dot_nt(qv, k_ref[0, pl.ds(k0, tq), :]) * scale
            if masked:
                sc = jnp.where(_chunk_mask(q0, k0, tq, tq), sc, NEG)
            m_new = jnp.maximum(m, jnp.max(sc, axis=-1, keepdims=True))
            alpha = jnp.exp(m - m_new)
            p = jnp.exp(sc - m_new)
            l = alpha * l + jnp.sum(p, axis=-1, keepdims=True)
            acc = alpha * acc + _dot_nn(p, v_ref[0, pl.ds(k0, tq), :])
            return m_new, l, acc

        for half in range(2):
            rows = slice(half * tq, (half + 1) * tq)
            qv = q_ref[0, rows, :]
            q0 = (2 * pair + half) * tq
            two = lambda i, c: step(qv, q0, 2 * i + 1, step(qv, q0, 2 * i, c, False), False)
            init = (jnp.full((tq, 1), NEG, F32), jnp.zeros((tq, 1), F32), jnp.zeros((tq, HEAD_PAD), F32))
            carry = lax.fori_loop(0, pair, two, init)
            if half == 1:
                carry = step(qv, q0, 2 * pair, carry, False)
            m, l, acc = step(qv, q0, 2 * pair + half, carry, True)
            o = acc / l
            lse_ref[0, rows, :] = m + jnp.log(l)

            @pl.when(hh == 0)
            def _():
                o_ref[rows, :] = o

            @pl.when(hh == 1)
            def _():
                o_ref[rows, :] += o

    head = lambda hp, pair, hh: 2 * hp + hh
    return pl.pallas_call(
        body, name="attn_fwd", grid=(HEADS // 2, nq // 2, 2),
        in_specs=[pl.BlockSpec((1, 2 * tq, HEAD_PAD), lambda hp, pair, hh: (head(hp, pair, hh), pair, 0)),
                  pl.BlockSpec((1, s, HEAD_PAD), lambda hp, pair, hh: (head(hp, pair, hh), 0, 0)),
                  pl.BlockSpec((1, s, HEAD_PAD), lambda hp, pair, hh: (head(hp, pair, hh), 0, 0))],
        out_specs=[pl.BlockSpec((2 * tq, HEAD_PAD), lambda hp, pair, hh: (pair, hp)),
                   pl.BlockSpec((1, 2 * tq, 1), lambda hp, pair, hh: (head(hp, pair, hh), pair, 0))],
        out_shape=[_sds((s, HEADS * V_HEAD)), _sds((HEADS, s, 1))],
        compiler_params=_cp(3))(q, k, v)


def _attn_bwd(q, q_t, k, v, do, do_t, o, lse, tq):
    s = q.shape[1]
    nq = s // tq
    per_q = tq // do_t.shape[3]
    scale = 1.0 / float(QK_NOPE + QK_ROPE) ** 0.5

    def body(q_ref, qt_ref, k_ref, v_ref, do_ref, dot_ref, o_ref, lse_ref, dq_ref, dk_ref, dv_ref, dk_t, dv_t):
        hh, kj = pl.program_id(1), pl.program_id(2)

        @pl.when(kj == 0)
        def _():
            dq_ref[...] = jnp.zeros_like(dq_ref)

        kv, vv = k_ref[0], v_ref[0]
        lane = lax.broadcasted_iota(jnp.int32, (tq, HEAD_PAD), 1)
        mine = lax.shift_right_logical(lane, 6) == hh
        dk_t[...] = jnp.zeros_like(dk_t)
        dv_t[...] = jnp.zeros_like(dv_t)

        def step(qi, masked):
            q0 = pl.multiple_of(qi * tq, tq)
            qv = q_ref[0, pl.ds(q0, tq), :]
            dov = do_ref[pl.ds(q0, tq), :]
            delta = jnp.sum(jnp.where(mine, dov * o_ref[pl.ds(q0, tq), :], 0.0), axis=-1, keepdims=True)
            sc = _dot_nt(qv, kv) * scale
            if masked:
                sc = jnp.where(_chunk_mask(q0, kj * tq, tq, tq), sc, NEG)
            p = jnp.exp(sc - lse_ref[0, pl.ds(q0, tq), :])
            ds = (p * (_dot_nt(dov, vv) - delta) * scale).astype(MXU_DTYPE)
            do_tv = jnp.concatenate([dot_ref[0, qi * per_q + r] for r in range(per_q)], axis=1)
            dv_t[...] += _dot_nn(do_tv, p)
            dk_t[...] += _dot_nn(qt_ref[0, qi], ds)
            dq_ref[0, pl.ds(q0, tq), :] += _dot_nn(ds, kv)

        step(kj, True)
        odd = (nq - 1 - kj) % 2

        @pl.when(odd == 1)
        def _():
            step(kj + 1, False)

        def two(i, c):
            step(kj + 1 + odd + 2 * i, False)
            step(kj + 2 + odd + 2 * i, False)
            return c

        lax.fori_loop(0, (nq - 1 - kj) // 2, two, 0)
        dk_ref[0] = dk_t[...].T
        dv_ref[0] = dv_t[...].T

    head = lambda hp, hh, kj: 2 * hp + hh
    full = pl.BlockSpec((1, s, HEAD_PAD), lambda hp, hh, kj: (head(hp, hh, kj), 0, 0))
    blk = pl.BlockSpec((1, tq, HEAD_PAD), lambda hp, hh, kj: (head(hp, hh, kj), kj, 0))
    pair = pl.BlockSpec((s, HEAD_PAD), lambda hp, hh, kj: (0, hp))
    return pl.pallas_call(
        body, name="attn_bwd", grid=(HEADS // 2, 2, nq),
        in_specs=[full, pl.BlockSpec((1,) + q_t.shape[1:], lambda hp, hh, kj: (head(hp, hh, kj), 0, 0, 0)), blk, blk,
                  pair, pl.BlockSpec((1,) + do_t.shape[1:], lambda hp, hh, kj: (hp, 0, 0, 0)), pair,
                  pl.BlockSpec((1, s, 1), lambda hp, hh, kj: (head(hp, hh, kj), 0, 0))],
        out_specs=[full, blk, blk], out_shape=[_sds((HEADS, s, HEAD_PAD))] * 3,
        scratch_shapes=[pltpu.VMEM((HEAD_PAD, tq), F32), pltpu.VMEM((HEAD_PAD, tq), F32)],
        compiler_params=_cp(3))(q, q_t, k, v, do, do_t, o, lse)


def _sc_conv(u, ubuf, w_ref, b_ref, ts):
    return (w_ref[2:3, :] * u + w_ref[1:2, :] * ubuf[pl.ds(SC_HALO - 1, ts), :]
            + w_ref[0:1, :] * ubuf[pl.ds(SC_HALO - 2, ts), :] + b_ref[...])


def _even_gate_fwd(z, o, sc_w, sc_b, ts):
    s = z.shape[0]
    w = SC_WIDTH

    def body(ab_ref, ac_ref, ax_ref, ag_ref, bg_ref, hc_ref, hx_ref, o_ref, w_ref, b_ref, y_ref, ubuf):
        i = pl.program_id(0)
        u = ac_ref[...] * ax_ref[...]
        ubuf[0:SC_HALO, :] = jnp.where(i > 0, hc_ref[...] * hx_ref[...], 0.0)
        ubuf[SC_HALO:, :] = u
        conv = _sc_conv(u, ubuf, w_ref, b_ref, ts)
        y_ref[:, 0:w] = (ab_ref[...] * conv * _silu(ag_ref[...])).astype(y_ref.dtype)
        y_ref[:, w:] = (o_ref[...] * _silu(bg_ref[...])).astype(y_ref.dtype)

    return pl.pallas_call(
        body, name="even_gate_fwd", grid=(s // ts,),
        in_specs=[_rows(ts, w, 0), _rows(ts, w, 1), _rows(ts, w, 2), _rows(ts, w, 3), _rows(ts, w, 5),
                  _prev_halo(ts, SC_HALO, w, 1), _prev_halo(ts, SC_HALO, w, 2), _rows(ts, w),
                  _vec(w, 0, SC_KERNEL), _vec(w)],
        out_specs=_rows(ts, 2 * w), out_shape=_sds((s, 2 * w), MXU_DTYPE),
        scratch_shapes=[pltpu.VMEM((ts + SC_HALO, w), F32)],
        compiler_params=_cp(1))(z, z, z, z, z, z, z, o, sc_w, sc_b)


def _even_gate_bwd(dy, z, o, sc_w, sc_b, ts):
    s = z.shape[0]
    w = SC_WIDTH
    n = s // ts

    def body(dya_ref, dyb_ref, dyan_ref, ab_ref, ac_ref, ax_ref, ag_ref, bg_ref, hc_ref, hx_ref, abn_ref, agn_ref,
             o_ref, w_ref, b_ref, dz_ref, do_ref, dot_ref, dw_ref, db_ref, ubuf, dbuf):
        i = pl.program_id(0)
        ab, ac, ax, ag, bg = ab_ref[...], ac_ref[...], ax_ref[...], ag_ref[...], bg_ref[...]
        dya, dyb = dya_ref[...], dyb_ref[...]
        u = ac * ax
        ubuf[0:SC_HALO, :] = jnp.where(i > 0, hc_ref[...] * hx_ref[...], 0.0)
        ubuf[SC_HALO:, :] = u
        conv = _sc_conv(u, ubuf, w_ref, b_ref, ts)
        sg = _silu(ag)
        dconv = dya * ab * sg
        dbuf[0:ts, :] = dconv
        dbuf[ts:, :] = jnp.where(i < n - 1, dyan_ref[...] * abn_ref[...] * _silu(agn_ref[...]), 0.0)
        du = w_ref[2:3, :] * dconv + w_ref[1:2, :] * dbuf[pl.ds(1, ts), :] + w_ref[0:1, :] * dbuf[pl.ds(2, ts), :]
        dz_ref[:, 0:w] = (dya * conv * sg).astype(dz_ref.dtype)
        dz_ref[:, w:2 * w] = (du * ax).astype(dz_ref.dtype)
        dz_ref[:, 2 * w:3 * w] = (du * ac).astype(dz_ref.dtype)
        dz_ref[:, 3 * w:4 * w] = (dya * ab * conv * _dsilu(ag)).astype(dz_ref.dtype)
        dz_ref[:, 4 * w:5 * w] = jnp.zeros((ts, w), dz_ref.dtype)
        dz_ref[:, 5 * w:] = (dyb * o_ref[...] * _dsilu(bg)).astype(dz_ref.dtype)
        do = dyb * _silu(bg)
        do_ref[...] = do
        for pair in range(HEADS // 2):
            dot_ref[pair, 0] = do[:, pair * HEAD_PAD:(pair + 1) * HEAD_PAD].T.astype(dot_ref.dtype)

        @pl.when(i == 0)
        def _():
            dw_ref[...] = jnp.zeros_like(dw_ref)
            db_ref[...] = jnp.zeros_like(db_ref)

        dw_ref[0:1, :] += jnp.sum(dconv * ubuf[pl.ds(SC_HALO - 2, ts), :], axis=0, keepdims=True)
        dw_ref[1:2, :] += jnp.sum(dconv * ubuf[pl.ds(SC_HALO - 1, ts), :], axis=0, keepdims=True)
        dw_ref[2:3, :] += jnp.sum(dconv * u, axis=0, keepdims=True)
        db_ref[...] += jnp.sum(dconv, axis=0, keepdims=True)

    return pl.pallas_call(
        body, name="even_gate_bwd", grid=(n,),
        in_specs=[_rows(ts, w, 0), _rows(ts, w, 1), _next_halo(ts, SC_HALO, w, 0, s),
                  _rows(ts, w, 0), _rows(ts, w, 1), _rows(ts, w, 2), _rows(ts, w, 3), _rows(ts, w, 5),
                  _prev_halo(ts, SC_HALO, w, 1), _prev_halo(ts, SC_HALO, w, 2),
                  _next_halo(ts, SC_HALO, w, 0, s), _next_halo(ts, SC_HALO, w, 3, s),
                  _rows(ts, w), _vec(w, 0, SC_KERNEL), _vec(w)],
        out_specs=[_rows(ts, EVEN_PAD), _rows(ts, w), pl.BlockSpec((HEADS // 2, 1, HEAD_PAD, ts), lambda i: (0, i, 0, 0)),
                   _vec(w, 0, SC_KERNEL), _vec(w)],
        out_shape=[_sds((s, EVEN_PAD), MXU_DTYPE), _sds((s, w)), _sds((HEADS // 2, n, HEAD_PAD, ts), MXU_DTYPE),
                   _sds((SC_KERNEL, w)), _sds((1, w))],
        scratch_shapes=[pltpu.VMEM((ts + SC_HALO, w), F32), pltpu.VMEM((ts + SC_HALO, w), F32)],
        compiler_params=_cp(1))(dy, dy, dy, z, z, z, z, z, z, z, z, z, o, sc_w, sc_b)


def _ln_act(uc, sg, g, b):
    mu = jnp.mean(uc, axis=-1, keepdims=True)
    var = jnp.mean(jnp.square(uc - mu), axis=-1, keepdims=True)
    return _silu((uc - mu) * lax.rsqrt(var + EPS) * g + b) * _silu(sg)


def _shifted_copies(buf, shifted, rows):
    for b in range(1, SUBLANES):
        shifted[b - 1, 0:rows, :] = buf[pl.ds(b, rows), :]


def _rows_at(buf, shifted, start, n):
    a, b = divmod(start, SUBLANES)
    return buf[pl.ds(SUBLANES * a, n), :] if b == 0 else shifted[b - 1, pl.ds(SUBLANES * a, n), :]


def _odd_fwd(z, conv_w, conv_b, ln_g, ln_b, ts):
    s = z.shape[0]
    d = D_MODEL
    k = CONF_KERNEL

    def body(val_ref, glu_ref, sg_ref, hval_ref, hglu_ref, w_ref, b_ref, g_ref, beta_ref, y_ref, uc_ref, ubuf, ush):
        i = pl.program_id(0)
        ubuf[0:CONF_HALO, :] = jnp.where(i > 0, hval_ref[...] * _sigmoid(hglu_ref[...]), 0.0)
        ubuf[CONF_HALO:, :] = val_ref[...] * _sigmoid(glu_ref[...])
        _shifted_copies(ubuf, ush, ts + CONF_HALO - SUBLANES)
        for r0 in range(0, ts, CONV_ROWS):
            acc = jnp.broadcast_to(b_ref[...], (CONV_ROWS, d))
            for j in range(k):
                acc = acc + w_ref[j:j + 1, :] * _rows_at(ubuf, ush, r0 + CONF_HALO - (k - 1) + j, CONV_ROWS)
            uc_ref[r0:r0 + CONV_ROWS, :] = acc
        y_ref[...] = _ln_act(uc_ref[...], sg_ref[...], g_ref[...], beta_ref[...]).astype(y_ref.dtype)

    return pl.pallas_call(
        body, name="odd_fwd", grid=(s // ts,),
        in_specs=[_rows(ts, d, 0), _rows(ts, d, 1), _rows(ts, d, 2),
                  _prev_halo(ts, CONF_HALO, d, 0), _prev_halo(ts, CONF_HALO, d, 1),
                  _vec(d, 0, k), _vec(d), _vec(d), _vec(d)],
        out_specs=[_rows(ts, d), _rows(ts, d)], out_shape=[_sds((s, d), MXU_DTYPE), _sds((s, d))],
        scratch_shapes=[pltpu.VMEM((ts + CONF_HALO, d), F32),
                        pltpu.VMEM((SUBLANES - 1, ts + CONF_HALO - SUBLANES, d), F32)],
        compiler_params=_cp(1))(z, z, z, z, z, conv_w, conv_b, ln_g, ln_b)


def _odd_bwd(dy, z, uc, conv_w, ln_g, ln_b, ts):
    s = z.shape[0]
    d = D_MODEL
    k = CONF_KERNEL
    n = s // ts

    def body(dy_ref, dyn_ref, val_ref, glu_ref, sg_ref, sgn_ref, uc_ref, ucn_ref,
             w_ref, g_ref, beta_ref, dz_ref, dw_ref, db_ref, dg_ref, dbeta_ref, dbuf, dsh, dw_acc):
        i = pl.program_id(0)
        val, glu = val_ref[...], glu_ref[...]
        sig = _sigmoid(glu)
        u = val * sig
        _, vjp = jax.vjp(_ln_act, uc_ref[...], sg_ref[...], g_ref[...], beta_ref[...])
        duc, dsg, dg, dbeta = vjp(dy_ref[...])
        _, vjp_n = jax.vjp(_ln_act, ucn_ref[...], sgn_ref[...], g_ref[...], beta_ref[...])
        dbuf[0:ts, :] = duc
        dbuf[ts:, :] = jnp.where(i < n - 1, vjp_n(dyn_ref[...])[0], 0.0)
        dz_ref[:, 2 * d:] = dsg.astype(dz_ref.dtype)
        _shifted_copies(dbuf, dsh, ts + CONF_HALO - SUBLANES)

        @pl.when(i == 0)
        def _():
            dw_acc[...] = jnp.zeros_like(dw_acc)
            db_ref[...] = jnp.zeros_like(db_ref)
            dg_ref[...] = jnp.zeros_like(dg_ref)
            dbeta_ref[...] = jnp.zeros_like(dbeta_ref)

        db_ref[...] += jnp.sum(duc, axis=0, keepdims=True)
        dg_ref[...] += dg
        dbeta_ref[...] += dbeta
        for r0 in range(0, ts, CONV_ROWS):
            acc = jnp.zeros((CONV_ROWS, d), F32)
            for j in range(k):
                acc = acc + w_ref[j:j + 1, :] * _rows_at(dbuf, dsh, r0 + (k - 1) - j, CONV_ROWS)
            sig_r = sig[r0:r0 + CONV_ROWS, :]
            dz_ref[r0:r0 + CONV_ROWS, 0:d] = (acc * sig_r).astype(dz_ref.dtype)
            dz_ref[r0:r0 + CONV_ROWS, d:2 * d] = (acc * val[r0:r0 + CONV_ROWS, :] * sig_r * (1.0 - sig_r)).astype(dz_ref.dtype)
        for j in range(k):
            prod = _rows_at(dbuf, dsh, (k - 1) - j, ts) * u
            dw_acc[j] += jnp.sum(prod.reshape(ts // SUBLANES, SUBLANES, d), axis=0)

        @pl.when(i == n - 1)
        def _():
            dw_ref[...] = jnp.sum(dw_acc[...], axis=1)

    return pl.pallas_call(
        body, name="odd_bwd", grid=(n,),
        in_specs=[_rows(ts, d), _next_halo(ts, CONF_HALO, d, 0, s),
                  _rows(ts, d, 0), _rows(ts, d, 1), _rows(ts, d, 2), _next_halo(ts, CONF_HALO, d, 2, s),
                  _rows(ts, d), _next_halo(ts, CONF_HALO, d, 0, s),
                  _vec(d, 0, k), _vec(d), _vec(d)],
        out_specs=[_rows(ts, ODD_IN), _vec(d, 0, k), _vec(d), _vec(d), _vec(d)],
        out_shape=[_sds((s, ODD_IN), MXU_DTYPE), _sds((k, d)), _sds((1, d)), _sds((1, d)), _sds((1, d))],
        scratch_shapes=[pltpu.VMEM((ts + CONF_HALO, d), F32),
                        pltpu.VMEM((SUBLANES - 1, ts + CONF_HALO - SUBLANES, d), F32), pltpu.VMEM((k, SUBLANES, d), F32)],
        compiler_params=_cp(1))(dy, dy, z, z, z, z, uc, uc, conv_w, ln_g, ln_b)


def _local_step(x, target, cos, sin, mod, p, layer_weights, fwd_dep=None, grads_done=None):
    s = x.shape[0]
    tsf, tsb = min(512, s // 2), min(256, s // 2)
    tq = min(512, s // 2)
    row1 = lambda a, i: a[i:i + 1]
    saved = []
    h = _pre_fwd(x, row1(p["pre_norm_g"], 0), row1(mod, 0), tsf, fwd_dep)
    for layer in range(DEPTH):
        i = layer // 2
        mod_l = row1(mod, layer)
        wl = layer_weights(layer, h)
        if layer % 2 == 0:
            z = _mm(h, wl["w_in"], "nn", F32, 512, EVEN_PAD, "even_in_fwd")
            if "late" in wl:
                wl.update(wl.pop("late")(z))
            q, q_t, k, v = _mla_prep_fwd(z, cos, sin, row1(p["even_q_norm_g"], i), row1(p["even_kv_norm_g"], i),
                                    wl["wq"], wl["wq_rot"], wl["wuk"], wl["wuv"], tsf)
            o, lse = _attn_fwd(q, k, v, tq)
            y = _even_gate_fwd(z, o, wl["sc_conv_w"], row1(p["even_sc_conv_b"], i), tsf)
            yo = _mm(y, wl["w_out"], "nn", F32, 1024, 1024, "even_out_fwd")
            saved.append((x, h, z, y, yo, wl, (q, q_t, k, v, o, lse)))
        else:
            z = _mm(h, wl["w_in"], "nn", F32, 512, ODD_IN, "odd_in_fwd")
            y, uc = _odd_fwd(z, wl["conv_w"], wl["conv_b"], wl["ln_g"], wl["ln_b"], tsf)
            yo = _mm(y, wl["w_out"], "nn", F32, 1024, 1024, "odd_out_fwd")
            saved.append((x, h, z, y, yo, wl, uc))
        if layer + 1 < DEPTH:
            x, h = _post_pre_fwd(x, yo, row1(p["post_norm_g"], layer), mod_l, row1(p["pre_norm_g"], layer + 1),
                                 row1(mod, layer + 1), tsf)
        else:
            loss, dx = _post_loss(x, yo, row1(p["post_norm_g"], layer), mod_l, target, tsf)

    g = {n: [None] * (DEPTH if n in ("pre_norm_g", "post_norm_g") else N_PAIRS) for n in (
        "pre_norm_g", "post_norm_g", "even_sc_conv_w", "even_sc_conv_b", "even_q_norm_g", "even_kv_norm_g",
        "odd_conv_w", "odd_conv_b", "odd_ln_g", "odd_ln_b")}
    dmod = [None] * DEPTH
    dep = None
    for layer in reversed(range(DEPTH)):
        i = layer // 2
        mod_l = row1(mod, layer)
        x_in, h, z, y, yo, wl, extra = saved[layer]
        dyo, dgate, g["post_norm_g"][layer] = _post_bwd(dx, yo, row1(p["post_norm_g"], layer), mod_l, tsb, dep)
        bufs = {}
        if layer % 2 == 0:
            q, q_t, k, v, o, lse = extra
            dy = _mm(dyo, wl["w_out"], "nt", F32, 1024, 1024, "even_out_bwd_x")
            bufs["even_w_out"] = _mm_tn_shards(y, dyo, "rows", "even_out_bwd_w")
            dz, do, do_t, g["even_sc_conv_w"][i], g["even_sc_conv_b"][i] = _even_gate_bwd(
                dy, z, o, wl["sc_conv_w"], row1(p["even_sc_conv_b"], i), tsb)
            dq, dk, dv = _attn_bwd(q, q_t, k, v, do, do_t, o, lse, tq)
            dz, bufs["even_mla"], g["even_q_norm_g"][i], g["even_kv_norm_g"][i] = _mla_prep_bwd(
                dz, dq, dk, dv, z, cos, sin, row1(p["even_q_norm_g"], i), row1(p["even_kv_norm_g"], i),
                wl["wq"], wl["wuk"], wl["wuv"], tsb)
            bufs["even_w_in"] = _ein_to_shards(_mm(h, dz, "tn", F32, D_MODEL, 512, "even_in_bwd_w"))
        else:
            uc = extra
            dy = _mm(dyo, wl["w_out"], "nt", F32, 1024, 1024, "odd_out_bwd_x")
            bufs["odd_w_out"] = _mm_tn_shards(y, dyo, "rows", "odd_out_bwd_w")
            dz, g["odd_conv_w"][i], g["odd_conv_b"][i], g["odd_ln_g"][i], g["odd_ln_b"][i] = _odd_bwd(
                dy, z, uc, wl["conv_w"], wl["ln_g"], wl["ln_b"], tsb)
            bufs["odd_w_in"] = _mm_tn_shards(h, dz, "cols", "odd_in_bwd_w")
        dx, dshift, dscale, g["pre_norm_g"][layer] = _pre_bwd(
            dz, wl["w_in"], dx, x_in, row1(p["pre_norm_g"], layer), mod_l, tsf)
        dmod[layer] = jnp.concatenate([dshift, dscale, dgate], axis=-1)
        dep = grads_done(layer, bufs, dx) if grads_done is not None else None
    stack = lambda parts: jnp.stack([a[0] if a.shape[0] == 1 and a.ndim == 2 else a for a in parts])
    small = {n: stack(parts) for n, parts in g.items()}
    small["dmod"] = jnp.concatenate(dmod, axis=0)
    return loss, dx, small


def _uq_to_heads(w):
    w = w.reshape(N_CHIPS, Q_LORA, 2, QK_NOPE + QK_ROPE).transpose(0, 2, 1, 3).reshape(HEADS, Q_LORA, QK_NOPE + QK_ROPE)
    half = QK_ROPE // 2
    rotated = jnp.concatenate([jnp.zeros_like(w[..., :QK_NOPE]), -w[..., QK_NOPE + half:], w[..., QK_NOPE:QK_NOPE + half]],
                              axis=-1)
    pad = ((0, 0), (0, 0), (0, HEAD_PAD - QK_NOPE - QK_ROPE))
    return _side_by_side(jnp.pad(w, pad)), _side_by_side(jnp.pad(rotated, pad))


def _side_by_side(w):
    return w.transpose(1, 0, 2).reshape(w.shape[1], HEADS * HEAD_PAD)


def _ukv_to_heads(w):
    w = w.reshape(N_CHIPS, KV_LORA, 2, QK_NOPE + V_HEAD).transpose(0, 2, 1, 3).reshape(HEADS, KV_LORA, QK_NOPE + V_HEAD)
    wk = jnp.pad(w[..., :QK_NOPE], ((0, 0), (0, 0), (0, HEAD_PAD - QK_NOPE)))
    wv = w[..., QK_NOPE:]
    zero = jnp.zeros_like(wv)
    odd = (jnp.arange(HEADS) % 2 == 1)[:, None, None]
    wv = jnp.concatenate([jnp.where(odd, zero, wv), jnp.where(odd, wv, zero)], axis=-1)
    return _side_by_side(wk), _side_by_side(wv)


def _mla_local(q):
    blocks = q.reshape(2, MLA_ROWS, HEAD_PAD)
    uq = jnp.concatenate([blocks[r, :Q_LORA, :QK_NOPE + QK_ROPE] for r in range(2)], axis=-1)
    ukv = jnp.concatenate(
        [jnp.concatenate([blocks[r, Q_LORA:Q_LORA + KV_LORA, :QK_NOPE],
                          blocks[r, Q_LORA + KV_LORA:, V_HEAD * r:V_HEAD * (r + 1)]], axis=-1) for r in range(2)], axis=-1)
    return uq, ukv


def _place():
    return lax.axis_index("x"), lax.axis_index("y"), lax.axis_index("c")


def _flip(v, bit):
    return 1 - v if bit else v


def _sem(a, k):
    return a * (N_CHIPS - 1) + k - 1


def _remote(src, dst, send_sem, recv_sem, peer):
    return pltpu.make_async_remote_copy(src_ref=src, dst_ref=dst, send_sem=send_sem, recv_sem=recv_sem,
                                        device_id=peer, device_id_type=MESH)


_VMEM_SPEC = pl.BlockSpec(memory_space=pltpu.VMEM)
_HBM_SPEC = pl.BlockSpec(memory_space=pl.ANY)


def _ada_fwd(c8, ada_w, ada_b_sh):
    depth, d, cols = ada_w.shape

    def body(c_ref, w_ref, b_ref, call_ref, mod_ref, s1, r1, s2, r2):
        x, y, c = _place()
        chip = 2 * x + y
        me = 2 * chip + c
        call_ref[me] = c_ref[...]
        sends = []
        for k in range(1, N_DEV):
            peer = (_flip(x, k & 4), _flip(y, k & 2), _flip(c, k & 1))
            cp = _remote(c_ref, call_ref.at[me], s1.at[k - 1], r1.at[k - 1], peer)
            cp.start()
            sends.append(cp)
        for k in range(1, N_DEV):
            src = 4 * _flip(x, k & 4) + 2 * _flip(y, k & 2) + _flip(c, k & 1)
            _remote(c_ref, call_ref.at[src], s1.at[k - 1], r1.at[k - 1], (x, y, c)).wait_recv()
        act = _silu(jnp.concatenate([call_ref[e, 0:1, :] for e in range(N_DEV)], axis=0))
        for l in range(depth):
            mod_ref[chip, l] = _dot_nn(act, w_ref[l]) + b_ref[l:l + 1, :]
        for k in range(1, N_CHIPS):
            peer = (_flip(x, k & 2), _flip(y, k & 1), c)
            cp = _remote(mod_ref.at[chip], mod_ref.at[chip], s2.at[k - 1], r2.at[k - 1], peer)
            cp.start()
            sends.append(cp)
        for k in range(1, N_CHIPS):
            src = 2 * _flip(x, k & 2) + _flip(y, k & 1)
            _remote(mod_ref.at[src], mod_ref.at[src], s2.at[k - 1], r2.at[k - 1], (x, y, c)).wait_recv()
        for cp in sends:
            cp.wait_send()

    return pl.pallas_call(
        body, name="ada_fwd", in_specs=[_VMEM_SPEC] * 3, out_specs=[_VMEM_SPEC] * 2,
        out_shape=[_sds((N_DEV, 8, d)), _sds((N_CHIPS, depth, N_DEV, cols))],
        scratch_shapes=[pltpu.SemaphoreType.DMA((N_DEV - 1,)), pltpu.SemaphoreType.DMA((N_DEV - 1,)),
                        pltpu.SemaphoreType.DMA((N_CHIPS - 1,)), pltpu.SemaphoreType.DMA((N_CHIPS - 1,))],
        compiler_params=pltpu.CompilerParams(vmem_limit_bytes=VMEM_LIMIT_V7X))(c8, ada_w, ada_b_sh)


def _ada_bwd(c_t, dmod_sh):
    depth, n, cols = dmod_sh.shape
    d = c_t.shape[0]
    tr = 256

    def body(c_ref, dm_ref, o_ref):
        act = _silu(c_ref[...])
        acc = act[:, 0:1] * dm_ref[0, 0:1, :]
        for e in range(1, n):
            acc = acc + act[:, e:e + 1] * dm_ref[0, e:e + 1, :]
        o_ref[0] = acc

    return pl.pallas_call(
        body, name="ada_bwd", grid=(depth, d // tr),
        in_specs=[pl.BlockSpec((tr, n), lambda l, i: (i, 0)), pl.BlockSpec((1, n, cols), lambda l, i: (l, 0, 0))],
        out_specs=pl.BlockSpec((1, tr, cols), lambda l, i: (l, i, 0)), out_shape=_sds((depth, d, cols)),
        compiler_params=_cp(2))(c_t, dmod_sh)


def _gathered_shape(shape, how):
    if how == "slot":
        return (N_CHIPS,) + shape
    r, cc = shape
    return (r, N_CHIPS * cc) if how == "cols" else (N_CHIPS * r, cc)


def _gathered_part(ref, shape, how, chip):
    if how == "slot":
        return ref.at[chip]
    if how == "cols":
        return ref.at[:, pl.ds(pl.multiple_of(chip * shape[1], 128), shape[1])]
    return ref.at[pl.ds(pl.multiple_of(chip * shape[0], 8), shape[0]), :]


_SEM_SPEC = pl.BlockSpec(memory_space=pltpu.SEMAPHORE)
_TOKEN = jax.ShapeDtypeStruct((8, 128), F32)
_SPLIT_COPY = pltpu.CompilerParams(has_side_effects=pltpu.SideEffectType.DATAFLOW_SIDE_EFFECTING)


def _in_hbm(a):
    return pltpu.with_memory_space_constraint(a, pltpu.HBM)


def _gather_start(items, gathered, name, after=()):
    n = len(items)

    def body(*refs):
        ins, outs = refs[:n], refs[n:2 * n]
        send_sems, recv_sems = refs[2 * n + len(after)], refs[2 * n + len(after) + 1]
        x, y, c = _place()
        for a in range(n):
            for k in range(1, N_CHIPS):
                part = _gathered_part(outs[a], items[a][0].shape, items[a][1], 2 * x + y)
                _remote(ins[a], part, send_sems.at[_sem(a, k)], recv_sems.at[_sem(a, k)],
                        (_flip(x, k & 2), _flip(y, k & 1), c)).start()
        refs[-1][...] = jnp.zeros(_TOKEN.shape, _TOKEN.dtype)

    arrays = [_in_hbm(a) for a, _ in items] + [_in_hbm(a) for a in gathered]
    res = pl.pallas_call(
        body, name=name, in_specs=[_HBM_SPEC] * (2 * n + len(after)),
        out_specs=[_SEM_SPEC, _SEM_SPEC] + [_HBM_SPEC] * (2 * n) + [_VMEM_SPEC],
        out_shape=[pltpu.SemaphoreType.DMA((n * (N_CHIPS - 1),)), pltpu.SemaphoreType.DMA((n * (N_CHIPS - 1),))]
        + [pltpu.HBM(a.shape, a.dtype) for a in arrays] + [_TOKEN],
        input_output_aliases={a: 2 + a for a in range(2 * n)}, compiler_params=_SPLIT_COPY)(*arrays, *after)
    return res[0], res[1], res[2:2 + n], res[2 + n:2 + 2 * n], res[-1]


def _gather_wait(items, started, after, name):
    n = len(items)
    send_sems, recv_sems, shards, gathered, _ = started

    def body(*refs):
        ins, outs, send_sems, recv_sems = refs[:n], refs[n:2 * n], refs[2 * n], refs[2 * n + 1]
        x, y, c = _place()
        for a in range(n):
            for k in range(1, N_CHIPS):
                part = _gathered_part(outs[a], items[a][0].shape, items[a][1], 2 * _flip(x, k & 2) + _flip(y, k & 1))
                cp = _remote(ins[a], part, send_sems.at[_sem(a, k)], recv_sems.at[_sem(a, k)], (x, y, c))
                cp.wait_send()
                cp.wait_recv()

    res = pl.pallas_call(
        body, name=name, in_specs=[_HBM_SPEC] * (2 * n) + [_SEM_SPEC, _SEM_SPEC] + [_HBM_SPEC] * len(after),
        out_specs=[_HBM_SPEC] * (2 * n), out_shape=[pltpu.HBM(a.shape, a.dtype) for a in (*shards, *gathered)],
        input_output_aliases={a: a for a in range(2 * n)}, compiler_params=_SPLIT_COPY)(
            *shards, *gathered, send_sems, recv_sems, *after)
    return res[n:]


def _rs_start(bufs, name, after=()):
    n = len(bufs)

    def body(*refs):
        srcs, lands = refs[:n], refs[n:2 * n]
        send_sems, recv_sems = refs[2 * n + len(after)], refs[2 * n + len(after) + 1]
        x, y, c = _place()
        for a in range(n):
            for k in range(1, N_CHIPS):
                tx, ty = _flip(x, k & 2), _flip(y, k & 1)
                _remote(srcs[a].at[2 * tx + ty], lands[a].at[k - 1], send_sems.at[_sem(a, k)], recv_sems.at[_sem(a, k)],
                        (tx, ty, c)).start()
        refs[-1][...] = jnp.zeros(_TOKEN.shape, _TOKEN.dtype)

    arrays = [_in_hbm(b) for b in bufs] + [_in_hbm(lax.empty((N_CHIPS - 1,) + b.shape[1:], b.dtype)) for b in bufs]
    res = pl.pallas_call(
        body, name=name, in_specs=[_HBM_SPEC] * (2 * n + len(after)),
        out_specs=[_SEM_SPEC, _SEM_SPEC] + [_HBM_SPEC] * (2 * n) + [_VMEM_SPEC],
        out_shape=[pltpu.SemaphoreType.DMA((n * (N_CHIPS - 1),)), pltpu.SemaphoreType.DMA((n * (N_CHIPS - 1),))]
        + [pltpu.HBM(a.shape, a.dtype) for a in arrays] + [_TOKEN],
        input_output_aliases={a: 2 + a for a in range(2 * n)}, compiler_params=_SPLIT_COPY)(*arrays, *after)
    return res[0], res[1], res[2:2 + n], res[2 + n:2 + 2 * n], res[-1]


def _rs_wait(started, after, name):
    send_sems, recv_sems, bufs, lands, _ = started
    n = len(bufs)

    def body(*refs):
        srcs, lnds, send_sems, recv_sems = refs[:n], refs[n:2 * n], refs[2 * n], refs[2 * n + 1]
        x, y, c = _place()
        for a in range(n):
            for k in range(1, N_CHIPS):
                cp = _remote(srcs[a].at[0], lnds[a].at[k - 1], send_sems.at[_sem(a, k)], recv_sems.at[_sem(a, k)], (x, y, c))
                cp.wait_send()
                cp.wait_recv()

    res = pl.pallas_call(
        body, name=name, in_specs=[_HBM_SPEC] * (2 * n) + [_SEM_SPEC, _SEM_SPEC] + [_HBM_SPEC] * len(after),
        out_specs=[_HBM_SPEC] * (2 * n), out_shape=[pltpu.HBM(a.shape, a.dtype) for a in (*bufs, *lands)],
        input_output_aliases={a: a for a in range(2 * n)}, compiler_params=_SPLIT_COPY)(
            *bufs, *lands, send_sems, recv_sems, *after)
    return res[:n], res[n:]


def _place_own(shard, how, chip_idx):
    r, cc = shard.shape
    block, index = {"slot": ((1, r, cc), lambda i, c: (c[0], 0, 0)), "cols": ((r, cc), lambda i, c: (0, c[0])),
                    "rows": ((r, cc), lambda i, c: (c[0], 0))}[how]

    def body(c_ref, in_ref, o_ref):
        del c_ref
        o_ref[...] = in_ref[...].reshape(o_ref.shape)

    return pl.pallas_call(
        body, name="place_own", out_shape=_sds(_gathered_shape(shard.shape, how), shard.dtype),
        grid_spec=pltpu.PrefetchScalarGridSpec(
            num_scalar_prefetch=1, grid=(1,), in_specs=[pl.BlockSpec((r, cc), lambda i, c: (0, 0))],
            out_specs=pl.BlockSpec(block, index)),
        compiler_params=_cp(1))(chip_idx, shard)


def _gather_sum_all(small):
    r, w = small.shape

    def body(in_ref, all_ref, sum_ref, send_sems, recv_sems):
        x, y, c = _place()
        me = 4 * x + 2 * y + c
        all_ref[me] = in_ref[...]
        sends = []
        for k in range(1, N_DEV):
            peer = (_flip(x, k & 4), _flip(y, k & 2), _flip(c, k & 1))
            cp = _remote(in_ref, all_ref.at[me], send_sems.at[k - 1], recv_sems.at[k - 1], peer)
            cp.start()
            sends.append(cp)
        for k in range(1, N_DEV):
            src = 4 * _flip(x, k & 4) + 2 * _flip(y, k & 2) + _flip(c, k & 1)
            _remote(in_ref, all_ref.at[src], send_sems.at[k - 1], recv_sems.at[k - 1], (x, y, c)).wait_recv()
        acc = all_ref[0]
        for e in range(1, N_DEV):
            acc = acc + all_ref[e]
        sum_ref[...] = acc
        for cp in sends:
            cp.wait_send()

    return pl.pallas_call(
        body, name="gather_sum_all", in_specs=[_VMEM_SPEC], out_specs=[_VMEM_SPEC] * 2,
        out_shape=[_sds((N_DEV, r, w)), _sds((r, w))],
        scratch_shapes=[pltpu.SemaphoreType.DMA((N_DEV - 1,)), pltpu.SemaphoreType.DMA((N_DEV - 1,))],
        compiler_params=pltpu.CompilerParams(vmem_limit_bytes=VMEM_LIMIT_V7X))(small)


def _add_chips(buf, t, chip_idx):
    r, cc = buf.shape[1:]
    tr = min(256, r)

    def body(c_ref, p_ref, t_ref, o_ref):
        del c_ref
        o_ref[...] = p_ref[0] + t_ref[0].astype(F32) + t_ref[1].astype(F32) + t_ref[2].astype(F32)

    return pl.pallas_call(
        body, name="add_chips", out_shape=_sds((r, cc)),
        grid_spec=pltpu.PrefetchScalarGridSpec(
            num_scalar_prefetch=1, grid=(r // tr,),
            in_specs=[pl.BlockSpec((1, tr, cc), lambda i, c: (c[0], i, 0)),
                      pl.BlockSpec((N_CHIPS - 1, tr, cc), lambda i, c: (0, i, 0))],
            out_specs=pl.BlockSpec((tr, cc), lambda i, c: (i, 0))),
        compiler_params=_cp(1))(chip_idx, buf, t)


def _rs_sibling(qs):
    n = len(qs)

    def body(*refs):
        ins, outs = refs[:n], refs[n:2 * n]
        send_sems, recv_sems = refs[2 * n:]
        x, y, c = _place()
        copies = [_remote(ins[a], outs[a], send_sems.at[a], recv_sems.at[a], (x, y, 1 - c)) for a in range(n)]
        for cp in copies:
            cp.start()
        for cp in copies:
            cp.wait()

    return pl.pallas_call(
        body, name="rs_sibling", in_specs=[_HBM_SPEC] * n, out_specs=[_HBM_SPEC] * n,
        out_shape=[_sds(q.shape) for q in qs],
        scratch_shapes=[pltpu.SemaphoreType.DMA((n,)), pltpu.SemaphoreType.DMA((n,))])(*qs)


def _adamw_update(w, g, m, v):
    m = ADAM_B1 * m + (1.0 - ADAM_B1) * g
    v = ADAM_B2 * v + (1.0 - ADAM_B2) * jnp.square(g)
    m_hat = m / (1.0 - ADAM_B1 ** ADAM_STEP)
    v_hat = v / (1.0 - ADAM_B2 ** ADAM_STEP)
    return -ADAM_LR * (m_hat / (jnp.sqrt(v_hat) + ADAM_EPS) + ADAM_WD * w), m, v


def _adamw(w, g_parts, m, v, name):
    shape = w.shape
    cols = shape[-1]
    rows = _size(shape[:-1])
    tr = 512 if rows % 512 == 0 else rows
    spec = pl.BlockSpec((tr, cols), lambda i: (i, 0))
    n = len(g_parts)
    n_out = 4 if n > 1 else 3

    def body(*refs):
        w_ref, m_ref, v_ref = refs[:3]
        d_ref, nm_ref, nv_ref = refs[-3:]
        g = refs[3][...]
        for r in refs[4:3 + n]:
            g = g + r[...]
        if n > 1:
            refs[3 + n][...] = g
        d_ref[...], nm_ref[...], nv_ref[...] = _adamw_update(w_ref[...], g, m_ref[...], v_ref[...])

    outs = pl.pallas_call(
        body, name="adamw_" + name, grid=(rows // tr,), in_specs=[spec] * (3 + n), out_specs=[spec] * n_out,
        out_shape=[_sds((rows, cols))] * n_out, compiler_params=_cp(1))(
            *[a.reshape(rows, cols) for a in (w, m, v, *g_parts)])
    outs = tuple(o.reshape(shape) for o in outs)
    return outs if n > 1 else (g_parts[0],) + outs


def _adamw_layer(w, g_parts, m, v, layer, prev, name):
    _, r, cc = w.shape
    tr = 512 if r % 512 == 0 else r
    spec = pl.BlockSpec((1, tr, cc), lambda i: (layer, i, 0))
    n = len(g_parts)

    def body(*refs):
        w_ref, m_ref, v_ref = refs[:3]
        g_ref, d_ref, nm_ref, nv_ref = refs[-4:]
        g = refs[3][...]
        for q in refs[4:3 + n]:
            g = g + q[...]
        g = g[:, :cc]
        g_ref[0] = g
        d_ref[0], nm_ref[0], nv_ref[0] = _adamw_update(w_ref[0], g, m_ref[0], v_ref[0])

    g_specs = [pl.BlockSpec((tr, q.shape[1]), lambda i: (i, 0)) for q in g_parts]
    passed = () if prev is None else tuple(prev)
    return pl.pallas_call(
        body, name="adamw_" + name, grid=(r // tr,),
        in_specs=[spec] * 3 + g_specs + [_HBM_SPEC] * len(passed), out_specs=[spec] * 4,
        out_shape=[_sds(w.shape)] * 4, input_output_aliases={3 + n + k: k for k in range(len(passed))},
        compiler_params=_cp(1))(w, m, v, *g_parts, *passed)


def _size(shape):
    n = 1
    for s in shape:
        n *= s
    return n


_SMALL = (("dmod", (DEPTH, 3 * D_MODEL)), ("pre_norm_g", (DEPTH, D_MODEL)), ("post_norm_g", (DEPTH, D_MODEL)),
          ("even_sc_conv_w", (2, SC_KERNEL, SC_WIDTH)), ("even_sc_conv_b", (2, SC_WIDTH)),
          ("even_q_norm_g", (2, Q_LORA)), ("even_kv_norm_g", (2, KV_LORA)),
          ("odd_conv_w", (2, CONF_KERNEL, D_MODEL)), ("odd_conv_b", (2, D_MODEL)), ("odd_ln_g", (2, D_MODEL)),
          ("odd_ln_b", (2, D_MODEL)))
SMALL_ROWS = -(-sum(_size(s) for _, s in _SMALL) // (8 * 128)) * 8

_SMALL_W = (("even_sc_conv_w", (2, SC_KERNEL, SC_WIDTH // N_CHIPS)), ("odd_conv_w", (2, CONF_KERNEL, D_MODEL // N_CHIPS)),
            ("odd_conv_b", (2, D_MODEL // N_CHIPS)), ("odd_ln_g", (2, D_MODEL // N_CHIPS)),
            ("odd_ln_b", (2, D_MODEL // N_CHIPS)))
SMALL_W_ROWS = -(-sum(_size(s) for _, s in _SMALL_W) // (8 * 128)) * 8


def _pack_rows(arrays, layout, rows):
    flat = jnp.concatenate([arrays[n].reshape(-1) for n, _ in layout])
    return jnp.pad(flat, (0, rows * 128 - flat.shape[0])).reshape(rows, 128)


def _unpack_small(t):
    flat = t.reshape(-1)
    out, at = {}, 0
    for n, shape in _SMALL:
        out[n] = flat[at:at + _size(shape)].reshape(shape)
        at += _size(shape)
    return out


def _unpack_small_w(t):
    flat = t.reshape(N_CHIPS, -1)
    out, at = {}, 0
    for n, shape in _SMALL_W:
        a = flat[:, at:at + _size(shape)].reshape((N_CHIPS,) + shape)
        out[n] = jnp.moveaxis(a, 0, -2).reshape(shape[:-1] + (N_CHIPS * shape[-1],))
        at += _size(shape)
    return out


def _chip_cols(a, chip):
    n = a.shape[-1] // N_CHIPS
    return lax.dynamic_slice_in_dim(a, chip * n, n, axis=a.ndim - 1)


WEIGHT_NAMES = ("ada_w", "ada_b", "pre_norm_g", "post_norm_g", "even_w_in", "even_sc_conv_w", "even_sc_conv_b",
                "even_q_norm_g", "even_kv_norm_g", "even_w_uq", "even_w_ukv", "even_w_out", "odd_w_in", "odd_conv_w",
                "odd_conv_b", "odd_ln_g", "odd_ln_b", "odd_w_out")
GATHER_HOW = ((("even_w_in", "slot"), ("even_w_uq", "slot"), ("even_w_ukv", "slot"), ("even_w_out", "rows")),
              (("odd_w_in", "cols"), ("odd_w_out", "rows")))


def kernel(x, c, positions, ada_w, ada_b, pre_norm_g, post_norm_g, even_w_in, even_sc_conv_w, even_sc_conv_b, even_q_norm_g, even_kv_norm_g, even_w_uq, even_w_ukv, even_w_out, odd_w_in, odd_conv_w, odd_conv_b, odd_ln_g, odd_ln_b, odd_w_out, loss_target, m_ada_w, m_ada_b, m_pre_norm_g, m_post_norm_g, m_even_w_in, m_even_sc_conv_w, m_even_sc_conv_b, m_even_q_norm_g, m_even_kv_norm_g, m_even_w_uq, m_even_w_ukv, m_even_w_out, m_odd_w_in, m_odd_conv_w, m_odd_conv_b, m_odd_ln_g, m_odd_ln_b, m_odd_w_out, v_ada_w, v_ada_b, v_pre_norm_g, v_post_norm_g, v_even_w_in, v_even_sc_conv_w, v_even_sc_conv_b, v_even_q_norm_g, v_even_kv_norm_g, v_even_w_uq, v_even_w_ukv, v_even_w_out, v_odd_w_in, v_odd_conv_w, v_odd_conv_b, v_odd_ln_g, v_odd_ln_b, v_odd_w_out):
    w = dict(zip(WEIGHT_NAMES, (ada_w, ada_b, pre_norm_g, post_norm_g, even_w_in, even_sc_conv_w, even_sc_conv_b,
                                even_q_norm_g, even_kv_norm_g, even_w_uq, even_w_ukv, even_w_out, odd_w_in, odd_conv_w,
                                odd_conv_b, odd_ln_g, odd_ln_b, odd_w_out)))
    m = dict(zip(WEIGHT_NAMES, (m_ada_w, m_ada_b, m_pre_norm_g, m_post_norm_g, m_even_w_in, m_even_sc_conv_w,
                                m_even_sc_conv_b, m_even_q_norm_g, m_even_kv_norm_g, m_even_w_uq, m_even_w_ukv,
                                m_even_w_out, m_odd_w_in, m_odd_conv_w, m_odd_conv_b, m_odd_ln_g, m_odd_ln_b, m_odd_w_out)))
    v = dict(zip(WEIGHT_NAMES, (v_ada_w, v_ada_b, v_pre_norm_g, v_post_norm_g, v_even_w_in, v_even_sc_conv_w,
                                v_even_sc_conv_b, v_even_q_norm_g, v_even_kv_norm_g, v_even_w_uq, v_even_w_ukv,
                                v_even_w_out, v_odd_w_in, v_odd_conv_w, v_odd_conv_b, v_odd_ln_g, v_odd_ln_b, v_odd_w_out)))
    ix, iy, ic = _place()
    chip = 2 * ix + iy
    me = 2 * chip + ic
    s = x.shape[1]

    c_all, mod_all = _ada_fwd(jnp.broadcast_to(c, (8, D_MODEL)), ada_w, _chip_cols(ada_b, chip))
    mod = lax.dynamic_index_in_dim(mod_all, me, axis=2, keepdims=False)
    mod = mod.transpose(1, 0, 2).reshape(DEPTH, 3 * D_MODEL)

    items = [[(w[n][layer // 2].astype(MXU_DTYPE), how) for n, how in GATHER_HOW[layer % 2]] for layer in range(DEPTH)]
    groups = [items[0][:1], items[0][1:] + [(_pack_rows(w, _SMALL_W, SMALL_W_ROWS), "slot")],
              [item for layer_items in items[1:] for item in layer_items]]
    sent, dep = [], mod_all
    for number, group in enumerate(groups):
        sent.append(_gather_start(group, [_place_own(a, how, chip.reshape(1)) for a, how in group],
                                  "gather_start_%d" % number, [dep]))
        dep = sent[-1][-1]
    arrived = {}

    def group(number, after):
        if number not in arrived:
            arrived[number] = _gather_wait(groups[number], sent[number], after, "gather_wait_%d" % number)
        return arrived[number]

    def even_rest(i, uq, ukv, eout, small_w):
        wuk, wuv = _ukv_to_heads(ukv)
        wq, wq_rot = _uq_to_heads(uq)
        return {"wq": wq, "wq_rot": wq_rot, "wuk": wuk, "wuv": wuv, "w_out": eout, "sc_conv_w": small_w["even_sc_conv_w"][i]}

    def layer_weights(layer, h):
        i = layer // 2
        if layer == 0:
            def late(z):
                uq, ukv, eout, small = group(1, [z])
                return even_rest(i, uq, ukv, eout, _unpack_small_w(small))
            return {"w_in": _ein_from_shards(group(0, [h])[0]), "late": late}
        small_w = _unpack_small_w(group(1, [h])[-1])
        at = sum(len(layer_items) for layer_items in items[1:layer])
        arrays = group(2, [h])[at:at + len(items[layer])]
        if layer % 2 == 0:
            return {"w_in": _ein_from_shards(arrays[0]), **even_rest(i, *arrays[1:], small_w)}
        oin, oout = arrays
        return {"w_in": oin, "w_out": oout, "conv_w": small_w["odd_conv_w"][i], "conv_b": small_w["odd_conv_b"][i:i + 1],
                "ln_g": small_w["odd_ln_g"][i:i + 1], "ln_b": small_w["odd_ln_b"][i:i + 1]}

    in_flight, own, sib, last = {}, {}, {}, {}

    def land(layer, after):
        names, started, kept = in_flight.pop(layer)
        bufs, arrived = _rs_wait(started, after, "rs_wait_%d" % layer)
        sums = [_add_chips(b, t, chip.reshape(1)) for b, t in zip(bufs if kept is None else kept, arrived)]
        for n, mine, theirs in zip(names, sums, _rs_sibling(sums)):
            own[n, layer // 2], sib[n, layer // 2] = mine, theirs

    def grads_done(layer, bufs, dx_in):
        if layer + 1 in in_flight:
            land(layer + 1, [dx_in])
        if layer == 0:
            last.update(bufs)
            return None
        names = sorted(bufs)
        in_flight[layer] = (names, _rs_start([bufs[n] for n in names], "rs_start_%d" % layer), None)
        return in_flight[layer][1][-1]

    p = {"pre_norm_g": pre_norm_g, "post_norm_g": post_norm_g, "even_sc_conv_b": even_sc_conv_b,
         "even_q_norm_g": even_q_norm_g, "even_kv_norm_g": even_kv_norm_g}
    inv_freq = 1.0 / (ROPE_THETA ** (jnp.arange(0, QK_ROPE, 2, dtype=F32) / QK_ROPE))
    inv_freq = jnp.zeros((1, HEAD_PAD), F32).at[0, QK_NOPE:QK_NOPE + QK_ROPE].set(jnp.tile(inv_freq, 2))
    cos, sin = _rope_tables(positions.reshape(s, 1), inv_freq)

    loss, dx, g = _local_step(x[0], loss_target[0], cos, sin, mod, p, layer_weights, dep, grads_done)

    grads, deltas, new_m, new_v = {}, {}, {}, {}

    def update_layers(n, results, pairs):
        for i in pairs:
            results = _adamw_layer(w[n], [own[n, i], sib[n, i]], m[n], v[n], i, results, n)
        return results

    small_all, small_sum = _gather_sum_all(_pack_rows(g, _SMALL, SMALL_ROWS))
    names = sorted(last)
    kept = [last[n] for n in names]
    in_flight[0] = (names, _rs_start([b.astype(jnp.bfloat16) for b in kept], "rs_start_0", [small_sum]), kept)
    tot = _unpack_small(small_sum)
    dmod_all = small_all[:, :DEPTH * 3 * D_MODEL // 128].reshape(N_DEV, DEPTH, 3 * D_MODEL)
    grads["ada_w"] = _ada_bwd(c_all[:, 0, :].T, _chip_cols(dmod_all, chip).transpose(1, 0, 2))
    grads["ada_b"] = tot["dmod"]
    for n in ("pre_norm_g", "post_norm_g", "even_sc_conv_b", "even_q_norm_g", "even_kv_norm_g"):
        grads[n] = tot[n]
    for n in ("even_sc_conv_w", "odd_conv_w", "odd_conv_b", "odd_ln_g", "odd_ln_b"):
        grads[n] = _chip_cols(tot[n], chip)
    for n in list(grads):
        _, deltas[n], new_m[n], new_v[n] = _adamw(w[n], [grads[n]], m[n], v[n], n)

    for n in ("odd_w_in", "odd_w_out"):
        grads[n], deltas[n], new_m[n], new_v[n] = update_layers(n, None, (1, 0))
    partly = {n: update_layers(n, None, (1,)) for n in ("even_w_in", "even_w_out")}
    land(0, [deltas["ada_w"], deltas["odd_w_in"], partly["even_w_in"][1]])
    for n in ("even_w_in", "even_w_out"):
        grads[n], deltas[n], new_m[n], new_v[n] = update_layers(n, partly[n], (0,))
    uq_parts, ukv_parts = zip(*[[jnp.stack(part) for part in zip(*[_mla_local(q["even_mla", i]) for i in range(N_PAIRS)])]
                                for q in (own, sib)])
    for n, parts in (("even_w_uq", uq_parts), ("even_w_ukv", ukv_parts)):
        grads[n], deltas[n], new_m[n], new_v[n] = _adamw(w[n], list(parts), m[n], v[n], n)

    total_loss = lax.psum(loss[0, 0], ("x", "y", "c"))
    return (total_loss, dx[None], *[grads[n] for n in WEIGHT_NAMES], *[deltas[n] for n in WEIGHT_NAMES],
            *[new_m[n] for n in WEIGHT_NAMES], *[new_v[n] for n in WEIGHT_NAMES])
```

```python
import jax
import jax.numpy as jnp
from jax import lax
from jax.experimental import pallas as pl
from jax.experimental.pallas import tpu as pltpu

F32 = jnp.float32
MXU_DTYPE = jnp.bfloat16
MESH = pl.DeviceIdType.MESH
VMEM_LIMIT_V7X = 56 * 2 ** 20

EPS = 1e-6
D_MODEL = 1024
DEPTH = 4
CHUNK = 64
SC_WIDTH = 512
SC_KERNEL = 3
SC_HALO = 8
HEADS = 8
QK_NOPE = 64
QK_ROPE = 32
V_HEAD = 64
HEAD_PAD = 128
Q_LORA = 256
KV_LORA = 128
ROPE_THETA = 10000.0
CONF_KERNEL = 31
CONF_HALO = 32
CONV_ROWS = 64
SUBLANES = 8
EVEN_IN = 2976
EVEN_PAD = 3072
ODD_IN = 3072
N_CHIPS = 4
N_DEV = 8
NEG = -1e30

ADAM_LR = 0.001
ADAM_B1 = 0.9
ADAM_B2 = 0.999
ADAM_EPS = 1e-08
ADAM_WD = 0.01
ADAM_STEP = 10

N_PAIRS = DEPTH // 2
EVEN_SHARD = EVEN_IN // N_CHIPS
EVEN_SHARD_PAD = 768
MLA_ROWS = Q_LORA + 2 * KV_LORA


def _cp(n_grid=0, **kw):
    return pltpu.CompilerParams(dimension_semantics=("arbitrary",) * n_grid,
                                vmem_limit_bytes=VMEM_LIMIT_V7X, **kw)


def _sigmoid(x):
    return 1.0 / (1.0 + jnp.exp(-x))


def _silu(x):
    return x * _sigmoid(x)


def _dsilu(x):
    s = _sigmoid(x)
    return s * (1.0 + x * (1.0 - s))


def _rms(x, g):
    return x * lax.rsqrt(jnp.mean(x * x, axis=-1, keepdims=True) + EPS) * g


def _dot(a, b, dims):
    return lax.dot_general(a.astype(MXU_DTYPE), b.astype(MXU_DTYPE), (dims, ((), ())),
                           preferred_element_type=F32)


def _dot_nn(a, b):
    return _dot(a, b, ((1,), (0,)))


def _dot_nt(a, b):
    return _dot(a, b, ((1,), (1,)))


def _dot_tn(a, b):
    return _dot(a, b, ((0,), (0,)))


def _rows(ts, w, cb=0):
    return pl.BlockSpec((ts, w), lambda i: (i, cb))


def _vec(w, cb=0, r=1):
    return pl.BlockSpec((r, w), lambda i: (0, cb))


def _prev_halo(ts, hr, w, cb):
    return pl.BlockSpec((hr, w), lambda i: (jnp.maximum(i * (ts // hr) - 1, 0), cb))


def _next_halo(ts, hr, w, cb, s):
    return pl.BlockSpec((hr, w), lambda i: (jnp.minimum((i + 1) * (ts // hr), s // hr - 1), cb))


def _sds(shape, dtype=F32):
    return jax.ShapeDtypeStruct(shape, dtype)


def _mm(a, b, mode, out_dtype, tm, tn, name):
    tm = min(tm, a.shape[1] if mode == "tn" else a.shape[0])
    tn = min(tn, b.shape[0] if mode == "nt" else b.shape[1])
    if mode == "nn":
        (m, k), n = a.shape, b.shape[1]
        a_spec = pl.BlockSpec((tm, k), lambda i, j: (i, 0))
        b_spec = pl.BlockSpec((k, tn), lambda i, j: (0, j))
        dot = _dot_nn
    elif mode == "nt":
        (m, k), n = a.shape, b.shape[0]
        a_spec = pl.BlockSpec((tm, k), lambda i, j: (i, 0))
        b_spec = pl.BlockSpec((tn, k), lambda i, j: (j, 0))
        dot = _dot_nt
    else:
        (k, m), n = a.shape, b.shape[1]
        a_spec = pl.BlockSpec((k, tm), lambda i, j: (0, i))
        b_spec = pl.BlockSpec((k, tn), lambda i, j: (0, j))
        dot = _dot_tn
    assert m % tm == 0 and n % tn == 0, (name, m, n, tm, tn)

    def body(a_ref, b_ref, o_ref):
        o_ref[...] = dot(a_ref[...], b_ref[...]).astype(o_ref.dtype)

    return pl.pallas_call(
        body, name=name, grid=(m // tm, n // tn), in_specs=[a_spec, b_spec],
        out_specs=pl.BlockSpec((tm, tn), lambda i, j: (i, j)), out_shape=_sds((m, n), out_dtype),
        compiler_params=_cp(2))(a, b)


def _mm_tn_shards(a, b, by, name):
    k, m = a.shape
    n = b.shape[1]
    if by == "cols":
        tm, tn = m, n // N_CHIPS
        shape, grid = (N_CHIPS, m, tn), (1, N_CHIPS)
        out_spec = pl.BlockSpec((1, tm, tn), lambda i, j: (j, i, 0))
    else:
        tm, tn = m // N_CHIPS, n
        shape, grid = (N_CHIPS, tm, n), (N_CHIPS, 1)
        out_spec = pl.BlockSpec((1, tm, tn), lambda i, j: (i, 0, j))

    def body(a_ref, b_ref, o_ref):
        o_ref[0] = _dot_tn(a_ref[...], b_ref[...])

    return pl.pallas_call(
        body, name=name, grid=grid,
        in_specs=[pl.BlockSpec((k, tm), lambda i, j: (0, i)), pl.BlockSpec((k, tn), lambda i, j: (0, j))],
        out_specs=out_spec, out_shape=_sds(shape), compiler_params=_cp(2))(a, b)


def _even_col(q):
    return q if q < 2432 else (q + 64 if q < 2464 else q + 96)


def _shard_pieces(j):
    lo, hi = EVEN_SHARD * j, EVEN_SHARD * (j + 1)
    cuts = [lo] + [b for b in (2432, 2464) if lo < b < hi] + [hi]
    return [(a - lo, _even_col(a), b - a) for a, b in zip(cuts[:-1], cuts[1:])]


def _ein_from_shards(w):
    _, d, _ = w.shape
    tr = 256

    def body(w_ref, o_ref):
        parts, at = [], 0
        for j in range(N_CHIPS):
            for d0, s0, n in _shard_pieces(j):
                if s0 > at:
                    parts.append(jnp.zeros((tr, s0 - at), F32))
                parts.append(w_ref[j, :, d0:d0 + n].astype(F32))
                at = s0 + n
        o_ref[...] = jnp.concatenate(parts, axis=1).astype(o_ref.dtype)

    return pl.pallas_call(
        body, name="ein_from_shards", grid=(d // tr,),
        in_specs=[pl.BlockSpec((N_CHIPS, tr, EVEN_SHARD), lambda i: (0, i, 0))],
        out_specs=_rows(tr, EVEN_PAD), out_shape=_sds((d, EVEN_PAD), w.dtype), compiler_params=_cp(1))(w)


def _ein_to_shards(dw):
    d = dw.shape[0]
    tr = 256

    def body(dw_ref, o_ref):
        for j in range(N_CHIPS):
            parts = [dw_ref[:, s0:s0 + n] for _, s0, n in _shard_pieces(j)]
            o_ref[j] = jnp.concatenate(parts + [jnp.zeros((tr, EVEN_SHARD_PAD - EVEN_SHARD), F32)], axis=1)

    return pl.pallas_call(
        body, name="ein_to_shards", grid=(d // tr,), in_specs=[_rows(tr, EVEN_PAD)],
        out_specs=pl.BlockSpec((N_CHIPS, tr, EVEN_SHARD_PAD), lambda i: (0, i, 0)),
        out_shape=_sds((N_CHIPS, d, EVEN_SHARD_PAD)), compiler_params=_cp(1))(dw)


def _rope_tables(pos_col, invf):
    s = pos_col.shape[0]
    ts = min(512, s)

    def body(p_ref, f_ref, c_ref, s_ref):
        ang = p_ref[...].astype(F32) * f_ref[...]
        lane = lax.broadcasted_iota(jnp.int32, ang.shape, 1)
        rope = (lane >= QK_NOPE) & (lane < QK_NOPE + QK_ROPE)
        c_ref[...] = jnp.where(lane < QK_NOPE, 1.0, jnp.where(rope, jnp.cos(ang), 0.0))
        s_ref[...] = jnp.where(rope, jnp.sin(ang), 0.0)

    return pl.pallas_call(
        body, name="rope_tables", grid=(s // ts,), in_specs=[_rows(ts, 1), _vec(HEAD_PAD)],
        out_specs=[_rows(ts, HEAD_PAD)] * 2, out_shape=[_sds((s, HEAD_PAD))] * 2,
        compiler_params=_cp(1))(pos_col, invf)


def _after(dep):
    return () if dep is None else (dep,)


def _pre_fwd(x, g, mod_l, ts, dep=None):
    s, d = x.shape

    def body(x_ref, g_ref, sh_ref, sc_ref, *rest):
        h = _rms(x_ref[...], g_ref[...]) * (1.0 + sc_ref[...]) + sh_ref[...]
        rest[-1][...] = h.astype(rest[-1].dtype)

    return pl.pallas_call(
        body, name="pre_fwd", grid=(s // ts,),
        in_specs=[_rows(ts, d), _vec(d), _vec(d, 0), _vec(d, 1)] + [_HBM_SPEC] * len(_after(dep)),
        out_specs=_rows(ts, d), out_shape=_sds((s, d), MXU_DTYPE), compiler_params=_cp(1))(
            x, g, mod_l, mod_l, *_after(dep))


def _pre_bwd(dz, w_in, dx_out, x, g, mod_l, ts):
    s, d = x.shape
    n_in = dz.shape[1]

    def f(xv, gv, sh, sc):
        return _rms(xv, gv) * (1.0 + sc) + sh

    def body(dz_ref, w_ref, dxo_ref, x_ref, g_ref, sh_ref, sc_ref, dx_ref, dsh_ref, dsc_ref, dg_ref):
        @pl.when(pl.program_id(0) == 0)
        def _():
            dsh_ref[...] = jnp.zeros_like(dsh_ref)
            dsc_ref[...] = jnp.zeros_like(dsc_ref)
            dg_ref[...] = jnp.zeros_like(dg_ref)

        _, vjp = jax.vjp(f, x_ref[...], g_ref[...], sh_ref[...], sc_ref[...])
        dx, dg, dsh, dsc = vjp(_dot_nt(dz_ref[...], w_ref[...]))
        dx_ref[...] = dxo_ref[...] + dx
        dsh_ref[...] += dsh
        dsc_ref[...] += dsc
        dg_ref[...] += dg

    return pl.pallas_call(
        body, name="pre_bwd", grid=(s // ts,),
        in_specs=[_rows(ts, n_in), _vec(n_in, 0, d), _rows(ts, d), _rows(ts, d), _vec(d), _vec(d, 0), _vec(d, 1)],
        out_specs=[_rows(ts, d), _vec(d), _vec(d), _vec(d)],
        out_shape=[_sds((s, d)), _sds((1, d)), _sds((1, d)), _sds((1, d))],
        compiler_params=_cp(1))(dz, w_in, dx_out, x, g, mod_l, mod_l)


def _post_pre_fwd(x, yo, g_post, mod_l, g_pre, mod_next, ts):
    s, d = x.shape

    def body(x_ref, yo_ref, gp_ref, gate_ref, g_ref, sh_ref, sc_ref, x_out_ref, h_ref):
        x_new = x_ref[...] + gate_ref[...] * _rms(yo_ref[...], gp_ref[...])
        x_out_ref[...] = x_new
        h_ref[...] = (_rms(x_new, g_ref[...]) * (1.0 + sc_ref[...]) + sh_ref[...]).astype(h_ref.dtype)

    return pl.pallas_call(
        body, name="post_pre_fwd", grid=(s // ts,),
        in_specs=[_rows(ts, d), _rows(ts, d), _vec(d), _vec(d, 2), _vec(d), _vec(d, 0), _vec(d, 1)],
        out_specs=[_rows(ts, d), _rows(ts, d)], out_shape=[_sds((s, d)), _sds((s, d), MXU_DTYPE)],
        compiler_params=_cp(1))(x, yo, g_post, mod_l, g_pre, mod_next, mod_next)


def _post_loss(x, yo, g_post, mod_l, target, ts):
    s, d = x.shape

    def body(x_ref, yo_ref, gp_ref, gate_ref, t_ref, loss_ref, dx_ref):
        err = x_ref[...] + gate_ref[...] * _rms(yo_ref[...], gp_ref[...]) - t_ref[...]
        dx_ref[...] = err * (1.0 / d)

        @pl.when(pl.program_id(0) == 0)
        def _():
            loss_ref[...] = jnp.zeros_like(loss_ref)

        loss_ref[...] += 0.5 * jnp.sum(jnp.sum(err * err, axis=-1, keepdims=True) * (1.0 / d), axis=0, keepdims=True)

    return pl.pallas_call(
        body, name="post_loss", grid=(s // ts,),
        in_specs=[_rows(ts, d), _rows(ts, d), _vec(d), _vec(d, 2), _rows(ts, d)],
        out_specs=[_vec(1), _rows(ts, d)], out_shape=[_sds((1, 1)), _sds((s, d))],
        compiler_params=_cp(1))(x, yo, g_post, mod_l, target)


def _post_bwd(dx_out, yo, g, mod_l, ts, dep=None):
    s, d = yo.shape

    def f(yov, gv, gate):
        return gate * _rms(yov, gv)

    def body(dx_ref, yo_ref, g_ref, gate_ref, *rest):
        dyo_ref, dgate_ref, dg_ref = rest[-3:]
        i = pl.program_id(0)
        _, vjp = jax.vjp(f, yo_ref[...], g_ref[...], gate_ref[...])
        dyo, dg, dgate = vjp(dx_ref[...])
        dyo_ref[...] = dyo.astype(dyo_ref.dtype)

        @pl.when(i == 0)
        def _():
            dgate_ref[...] = jnp.zeros_like(dgate_ref)
            dg_ref[...] = jnp.zeros_like(dg_ref)

        dgate_ref[...] += dgate
        dg_ref[...] += dg

    return pl.pallas_call(
        body, name="post_bwd", grid=(s // ts,),
        in_specs=[_rows(ts, d), _rows(ts, d), _vec(d), _vec(d, 2)] + [_HBM_SPEC] * len(_after(dep)),
        out_specs=[_rows(ts, d), _vec(d), _vec(d)],
        out_shape=[_sds((s, d), MXU_DTYPE), _sds((1, d)), _sds((1, d))],
        compiler_params=_cp(1))(dx_out, yo, g, mod_l, *_after(dep))


def _rope(t, cos, sin):
    lane = lax.broadcasted_iota(jnp.int32, t.shape, 1)
    first = (lane >= QK_NOPE) & (lane < QK_NOPE + QK_ROPE // 2)
    second = (lane >= QK_NOPE + QK_ROPE // 2) & (lane < QK_NOPE + QK_ROPE)
    up = pltpu.roll(t, QK_ROPE // 2, 1)
    down = pltpu.roll(t, HEAD_PAD - QK_ROPE // 2, 1)
    return t * cos + jnp.where(first, -down, jnp.where(second, up, 0.0)) * sin


def _rope_transposed(g, cos, sin):
    lane = lax.broadcasted_iota(jnp.int32, g.shape, 1)
    first = (lane >= QK_NOPE) & (lane < QK_NOPE + QK_ROPE // 2)
    second = (lane >= QK_NOPE + QK_ROPE // 2) & (lane < QK_NOPE + QK_ROPE)
    u = g * sin
    up = pltpu.roll(u, QK_ROPE // 2, 1)
    down = pltpu.roll(u, HEAD_PAD - QK_ROPE // 2, 1)
    return g * cos + jnp.where(first, down, jnp.where(second, -up, 0.0))


def _mla_prep_fwd(z, cos, sin, qg, kvg, wq, wq_rot, wuk, wuv, ts):
    s = z.shape[0]
    wide = HEADS * HEAD_PAD

    def body(cq_ref, ckv_ref, kr_ref, cos_ref, sin_ref, qg_ref, kvg_ref, wq_ref, wqr_ref, wuk_ref, wuv_ref,
             q_ref, qt_ref, k_ref, v_ref):
        cos_v, sin_v = cos_ref[...], sin_ref[...]
        cqn = _rms(cq_ref[...], qg_ref[...])
        ckvn = _rms(ckv_ref[...], kvg_ref[...])
        kr = _rope(kr_ref[...], cos_v, sin_v)
        q_lin, q_rot = _dot_nn(cqn, wq_ref[...]), _dot_nn(cqn, wqr_ref[...])
        k_lin, v_all = _dot_nn(ckvn, wuk_ref[...]), _dot_nn(ckvn, wuv_ref[...])
        for h in range(HEADS):
            lanes = slice(h * HEAD_PAD, (h + 1) * HEAD_PAD)
            qh = q_lin[:, lanes] * cos_v + q_rot[:, lanes] * sin_v
            q_ref[h] = qh.astype(q_ref.dtype)
            qt_ref[h, 0] = qh.T.astype(qt_ref.dtype)
            k_ref[h] = (k_lin[:, lanes] + kr).astype(k_ref.dtype)
            v_ref[h] = v_all[:, lanes].astype(v_ref.dtype)

    out = pl.BlockSpec((HEADS, ts, HEAD_PAD), lambda i: (0, i, 0))
    return pl.pallas_call(
        body, name="mla_prep_fwd", grid=(s // ts,),
        in_specs=[_rows(ts, Q_LORA, 8), _rows(ts, KV_LORA, 18), _rows(ts, HEAD_PAD, 19), _rows(ts, HEAD_PAD), _rows(ts, HEAD_PAD),
                  _vec(Q_LORA), _vec(KV_LORA), _vec(wide, 0, Q_LORA), _vec(wide, 0, Q_LORA), _vec(wide, 0, KV_LORA),
                  _vec(wide, 0, KV_LORA)],
        out_specs=[out, pl.BlockSpec((HEADS, 1, HEAD_PAD, ts), lambda i: (0, i, 0, 0)), out, out],
        out_shape=[_sds((HEADS, s, HEAD_PAD), MXU_DTYPE), _sds((HEADS, s // ts, HEAD_PAD, ts), MXU_DTYPE)]
        + [_sds((HEADS, s, HEAD_PAD), MXU_DTYPE)] * 2,
        compiler_params=_cp(1))(z, z, z, cos, sin, qg, kvg, wq, wq_rot, wuk, wuv)


def _mla_prep_bwd(dz, dq, dk, dv, z, cos, sin, qg, kvg, wq, wuk, wuv, ts):
    s = z.shape[0]

    def fq(cq, g):
        return _rms(cq, g)

    def body(dz_in_ref, dq_ref, dk_ref, dv_ref, cq_ref, ckv_ref, cos_ref, sin_ref, qg_ref, kvg_ref, wq_ref, wuk_ref,
             wuv_ref, dz_ref, dw_ref, dqg_ref, dkvg_ref):
        del dz_in_ref
        cos_v, sin_v = cos_ref[...], sin_ref[...]

        @pl.when(pl.program_id(0) == 0)
        def _():
            dw_ref[...] = jnp.zeros_like(dw_ref)
            dqg_ref[...] = jnp.zeros_like(dqg_ref)
            dkvg_ref[...] = jnp.zeros_like(dkvg_ref)

        cqn, vjp_q = jax.vjp(fq, cq_ref[...], qg_ref[...])
        ckvn, vjp_kv = jax.vjp(fq, ckv_ref[...], kvg_ref[...])
        lane = lax.broadcasted_iota(jnp.int32, (ts, HEAD_PAD), 1)
        rope_lanes = (lane >= QK_NOPE) & (lane < QK_NOPE + QK_ROPE)
        dq_lin = jnp.concatenate([_rope_transposed(dq_ref[h], cos_v, sin_v).astype(MXU_DTYPE) for h in range(HEADS)], axis=1)
        dk_all = jnp.concatenate([dk_ref[h].astype(MXU_DTYPE) for h in range(HEADS)], axis=1)
        dv_all = jnp.concatenate([dv_ref[h].astype(MXU_DTYPE) for h in range(HEADS)], axis=1)
        dkr = jnp.where(rope_lanes, dk_ref[0], 0.0)
        for h in range(1, HEADS):
            dkr = dkr + jnp.where(rope_lanes, dk_ref[h], 0.0)
        dcq, dqg = vjp_q(_dot_nt(dq_lin, wq_ref[...]))
        dckv, dkvg = vjp_kv(_dot_nt(dk_all, wuk_ref[...]) + _dot_nt(dv_all, wuv_ref[...]))
        dz_ref[:, 0:Q_LORA] = dcq.astype(dz_ref.dtype)
        dz_ref[:, Q_LORA:Q_LORA + KV_LORA] = dckv.astype(dz_ref.dtype)
        dz_ref[:, Q_LORA + KV_LORA:] = _rope_transposed(dkr, cos_v, sin_v).astype(dz_ref.dtype)
        dqg_ref[...] += dqg
        dkvg_ref[...] += dkvg
        dwq, dwuk, dwuv = _dot_tn(cqn, dq_lin), _dot_tn(ckvn, dk_all), _dot_tn(ckvn, dv_all)
        for h in range(HEADS):
            lanes = slice(h * HEAD_PAD, (h + 1) * HEAD_PAD)
            row0 = (h % 2) * MLA_ROWS
            dw_ref[h // 2, row0:row0 + Q_LORA, :] += dwq[:, lanes]
            dw_ref[h // 2, row0 + Q_LORA:row0 + Q_LORA + KV_LORA, :] += dwuk[:, lanes]
            dw_ref[h // 2, row0 + Q_LORA + KV_LORA:row0 + MLA_ROWS, :] += dwuv[:, lanes]

    wide = HEADS * HEAD_PAD
    heads = pl.BlockSpec((HEADS, ts, HEAD_PAD), lambda i: (0, i, 0))
    whole = pl.BlockSpec((N_CHIPS, 2 * MLA_ROWS, HEAD_PAD), lambda i: (0, 0, 0))
    return pl.pallas_call(
        body, name="mla_prep_bwd", grid=(s // ts,),
        in_specs=[_HBM_SPEC, heads, heads, heads, _rows(ts, Q_LORA, 8), _rows(ts, KV_LORA, 18),
                  _rows(ts, HEAD_PAD), _rows(ts, HEAD_PAD), _vec(Q_LORA), _vec(KV_LORA), _vec(wide, 0, Q_LORA),
                  _vec(wide, 0, KV_LORA), _vec(wide, 0, KV_LORA)],
        out_specs=[_rows(ts, 512, 4), whole, _vec(Q_LORA), _vec(KV_LORA)],
        out_shape=[_sds(dz.shape, dz.dtype), _sds((N_CHIPS, 2 * MLA_ROWS, HEAD_PAD)), _sds((1, Q_LORA)), _sds((1, KV_LORA))],
        input_output_aliases={0: 0}, compiler_params=_cp(1))(dz, dq, dk, dv, z, z, cos, sin, qg, kvg, wq, wuk, wuv)


def _chunk_mask(q0, k0, tq, tk):
    rows = q0 + lax.broadcasted_iota(jnp.int32, (tq, tk), 0)
    cols = k0 + lax.broadcasted_iota(jnp.int32, (tq, tk), 1)
    shift = CHUNK.bit_length() - 1
    return lax.shift_right_logical(cols, shift) <= lax.shift_right_logical(rows, shift)


def _attn_fwd(q, k, v, tq):
    s = q.shape[1]
    nq = s // tq
    scale = 1.0 / float(QK_NOPE + QK_ROPE) ** 0.5

    assert nq % 2 == 0, (s, tq)

    def body(q_ref, k_ref, v_ref, o_ref, lse_ref):
        pair, hh = pl.program_id(1), pl.program_id(2)

        def step(qv, q0, kj, carry, masked):
            m, l, acc = carry
            k0 = pl.multiple_of(kj * tq, tq)
            sc = _dot_nt(qv, k_ref[0, pl.ds(k0, tq), :]) * scale
            if masked:
                sc = jnp.where(_chunk_mask(q0, k0, tq, tq), sc, NEG)
            m_new = jnp.maximum(m, jnp.max(sc, axis=-1, keepdims=True))
            alpha = jnp.exp(m - m_new)
            p = jnp.exp(sc - m_new)
            l = alpha * l + jnp.sum(p, axis=-1, keepdims=True)
            acc = alpha * acc + _dot_nn(p, v_ref[0, pl.ds(k0, tq), :])
            return m_new, l, acc

        for half in range(2):
            rows = slice(half * tq, (half + 1) * tq)
            qv = q_ref[0, rows, :]
            q0 = (2 * pair + half) * tq
            two = lambda i, c: step(qv, q0, 2 * i + 1, step(qv, q0, 2 * i, c, False), False)
            init = (jnp.full((tq, 1), NEG, F32), jnp.zeros((tq, 1), F32), jnp.zeros((tq, HEAD_PAD), F32))
            carry = lax.fori_loop(0, pair, two, init)
            if half == 1:
                carry = step(qv, q0, 2 * pair, carry, False)
            m, l, acc = step(qv, q0, 2 * pair + half, carry, True)
            o = acc / l
            lse_ref[0, rows, :] = m + jnp.log(l)

            @pl.when(hh == 0)
            def _():
                o_ref[rows, :] = o

            @pl.when(hh == 1)
            def _():
                o_ref[rows, :] += o

    head = lambda hp, pair, hh: 2 * hp + hh
    return pl.pallas_call(
        body, name="attn_fwd", grid=(HEADS // 2, nq // 2, 2),
        in_specs=[pl.BlockSpec((1, 2 * tq, HEAD_PAD), lambda hp, pair, hh: (head(hp, pair, hh), pair, 0)),
                  pl.BlockSpec((1, s, HEAD_PAD), lambda hp, pair, hh: (head(hp, pair, hh), 0, 0)),
                  pl.BlockSpec((1, s, HEAD_PAD), lambda hp, pair, hh: (head(hp, pair, hh), 0, 0))],
        out_specs=[pl.BlockSpec((2 * tq, HEAD_PAD), lambda hp, pair, hh: (pair, hp)),
                   pl.BlockSpec((1, 2 * tq, 1), lambda hp, pair, hh: (head(hp, pair, hh), pair, 0))],
        out_shape=[_sds((s, HEADS * V_HEAD)), _sds((HEADS, s, 1))],
        compiler_params=_cp(3))(q, k, v)


def _attn_bwd(q, q_t, k, v, do, do_t, o, lse, tq):
    s = q.shape[1]
    nq = s // tq
    per_q = tq // do_t.shape[3]
    scale = 1.0 / float(QK_NOPE + QK_ROPE) ** 0.5

    def body(q_ref, qt_ref, k_ref, v_ref, do_ref, dot_ref, o_ref, lse_ref, dq_ref, dk_ref, dv_ref, dk_t, dv_t):
        hh, kj = pl.program_id(1), pl.program_id(2)

        @pl.when(kj == 0)
        def _():
            dq_ref[...] = jnp.zeros_like(dq_ref)

        kv, vv = k_ref[0], v_ref[0]
        lane = lax.broadcasted_iota(jnp.int32, (tq, HEAD_PAD), 1)
        mine = lax.shift_right_logical(lane, 6) == hh
        dk_t[...] = jnp.zeros_like(dk_t)
        dv_t[...] = jnp.zeros_like(dv_t)

        def step(qi, masked):
            q0 = pl.multiple_of(qi * tq, tq)
            qv = q_ref[0, pl.ds(q0, tq), :]
            dov = do_ref[pl.ds(q0, tq), :]
            delta = jnp.sum(jnp.where(mine, dov * o_ref[pl.ds(q0, tq), :], 0.0), axis=-1, keepdims=True)
            sc = _dot_nt(qv, kv) * scale
            if masked:
                sc = jnp.where(_chunk_mask(q0, kj * tq, tq, tq), sc, NEG)
            p = jnp.exp(sc - lse_ref[0, pl.ds(q0, tq), :])
            ds = (p * (_dot_nt(dov, vv) - delta) * scale).astype(MXU_DTYPE)
            do_tv = jnp.concatenate([dot_ref[0, qi * per_q + r] for r in range(per_q)], axis=1)
            dv_t[...] += _dot_nn(do_tv, p)
            dk_t[...] += _dot_nn(qt_ref[0, qi], ds)
            dq_ref[0, pl.ds(q0, tq), :] += _dot_nn(ds, kv)

        step(kj, True)
        odd = (nq - 1 - kj) % 2

        @pl.when(odd == 1)
        def _():
            step(kj + 1, False)

        def two(i, c):
            step(kj + 1 + odd + 2 * i, False)
            step(kj + 2 + odd + 2 * i, False)
            return c

        lax.fori_loop(0, (nq - 1 - kj) // 2, two, 0)
        dk_ref[0] = dk_t[...].T
        dv_ref[0] = dv_t[...].T

    head = lambda hp, hh, kj: 2 * hp + hh
    full = pl.BlockSpec((1, s, HEAD_PAD), lambda hp, hh, kj: (head(hp, hh, kj), 0, 0))
    blk = pl.BlockSpec((1, tq, HEAD_PAD), lambda hp, hh, kj: (head(hp, hh, kj), kj, 0))
    pair = pl.BlockSpec((s, HEAD_PAD), lambda hp, hh, kj: (0, hp))
    return pl.pallas_call(
        body, name="attn_bwd", grid=(HEADS // 2, 2, nq),
        in_specs=[full, pl.BlockSpec((1,) + q_t.shape[1:], lambda hp, hh, kj: (head(hp, hh, kj), 0, 0, 0)), blk, blk,
                  pair, pl.BlockSpec((1,) + do_t.shape[1:], lambda hp, hh, kj: (hp, 0, 0, 0)), pair,
                  pl.BlockSpec((1, s, 1), lambda hp, hh, kj: (head(hp, hh, kj), 0, 0))],
        out_specs=[full, blk, blk], out_shape=[_sds((HEADS, s, HEAD_PAD))] * 3,
        scratch_shapes=[pltpu.VMEM((HEAD_PAD, tq), F32), pltpu.VMEM((HEAD_PAD, tq), F32)],
        compiler_params=_cp(3))(q, q_t, k, v, do, do_t, o, lse)


def _sc_conv(u, ubuf, w_ref, b_ref, ts):
    return (w_ref[2:3, :] * u + w_ref[1:2, :] * ubuf[pl.ds(SC_HALO - 1, ts), :]
            + w_ref[0:1, :] * ubuf[pl.ds(SC_HALO - 2, ts), :] + b_ref[...])


def _even_gate_fwd(z, o, sc_w, sc_b, ts):
    s = z.shape[0]
    w = SC_WIDTH

    def body(ab_ref, ac_ref, ax_ref, ag_ref, bg_ref, hc_ref, hx_ref, o_ref, w_ref, b_ref, y_ref, ubuf):
        i = pl.program_id(0)
        u = ac_ref[...] * ax_ref[...]
        ubuf[0:SC_HALO, :] = jnp.where(i > 0, hc_ref[...] * hx_ref[...], 0.0)
        ubuf[SC_HALO:, :] = u
        conv = _sc_conv(u, ubuf, w_ref, b_ref, ts)
        y_ref[:, 0:w] = (ab_ref[...] * conv * _silu(ag_ref[...])).astype(y_ref.dtype)
        y_ref[:, w:] = (o_ref[...] * _silu(bg_ref[...])).astype(y_ref.dtype)

    return pl.pallas_call(
        body, name="even_gate_fwd", grid=(s // ts,),
        in_specs=[_rows(ts, w, 0), _rows(ts, w, 1), _rows(ts, w, 2), _rows(ts, w, 3), _rows(ts, w, 5),
                  _prev_halo(ts, SC_HALO, w, 1), _prev_halo(ts, SC_HALO, w, 2), _rows(ts, w),
                  _vec(w, 0, SC_KERNEL), _vec(w)],
        out_specs=_rows(ts, 2 * w), out_shape=_sds((s, 2 * w), MXU_DTYPE),
        scratch_shapes=[pltpu.VMEM((ts + SC_HALO, w), F32)],
        compiler_params=_cp(1))(z, z, z, z, z, z, z, o, sc_w, sc_b)


def _even_gate_bwd(dy, z, o, sc_w, sc_b, ts):
    s = z.shape[0]
    w = SC_WIDTH
    n = s // ts

    def body(dya_ref, dyb_ref, dyan_ref, ab_ref, ac_ref, ax_ref, ag_ref, bg_ref, hc_ref, hx_ref, abn_ref, agn_ref,
             o_ref, w_ref, b_ref, dz_ref, do_ref, dot_ref, dw_ref, db_ref, ubuf, dbuf):
        i = pl.program_id(0)
        ab, ac, ax, ag, bg = ab_ref[...], ac_ref[...], ax_ref[...], ag_ref[...], bg_ref[...]
        dya, dyb = dya_ref[...], dyb_ref[...]
        u = ac * ax
        ubuf[0:SC_HALO, :] = jnp.where(i > 0, hc_ref[...] * hx_ref[...], 0.0)
        ubuf[SC_HALO:, :] = u
        conv = _sc_conv(u, ubuf, w_ref, b_ref, ts)
        sg = _silu(ag)
        dconv = dya * ab * sg
        dbuf[0:ts, :] = dconv
        dbuf[ts:, :] = jnp.where(i < n - 1, dyan_ref[...] * abn_ref[...] * _silu(agn_ref[...]), 0.0)
        du = w_ref[2:3, :] * dconv + w_ref[1:2, :] * dbuf[pl.ds(1, ts), :] + w_ref[0:1, :] * dbuf[pl.ds(2, ts), :]
        dz_ref[:, 0:w] = (dya * conv * sg).astype(dz_ref.dtype)
        dz_ref[:, w:2 * w] = (du * ax).astype(dz_ref.dtype)
        dz_ref[:, 2 * w:3 * w] = (du * ac).astype(dz_ref.dtype)
        dz_ref[:, 3 * w:4 * w] = (dya * ab * conv * _dsilu(ag)).astype(dz_ref.dtype)
        dz_ref[:, 4 * w:5 * w] = jnp.zeros((ts, w), dz_ref.dtype)
        dz_ref[:, 5 * w:] = (dyb * o_ref[...] * _dsilu(bg)).astype(dz_ref.dtype)
        do = dyb * _silu(bg)
        do_ref[...] = do
        for pair in range(HEADS // 2):
            dot_ref[pair, 0] = do[:, pair * HEAD_PAD:(pair + 1) * HEAD_PAD].T.astype(dot_ref.dtype)

        @pl.when(i == 0)
        def _():
            dw_ref[...] = jnp.zeros_like(dw_ref)
            db_ref[...] = jnp.zeros_like(db_ref)

        dw_ref[0:1, :] += jnp.sum(dconv * ubuf[pl.ds(SC_HALO - 2, ts), :], axis=0, keepdims=True)
        dw_ref[1:2, :] += jnp.sum(dconv * ubuf[pl.ds(SC_HALO - 1, ts), :], axis=0, keepdims=True)
        dw_ref[2:3, :] += jnp.sum(dconv * u, axis=0, keepdims=True)
        db_ref[...] += jnp.sum(dconv, axis=0, keepdims=True)

    return pl.pallas_call(
        body, name="even_gate_bwd", grid=(n,),
        in_specs=[_rows(ts, w, 0), _rows(ts, w, 1), _next_halo(ts, SC_HALO, w, 0, s),
                  _rows(ts, w, 0), _rows(ts, w, 1), _rows(ts, w, 2), _rows(ts, w, 3), _rows(ts, w, 5),
                  _prev_halo(ts, SC_HALO, w, 1), _prev_halo(ts, SC_HALO, w, 2),
                  _next_halo(ts, SC_HALO, w, 0, s), _next_halo(ts, SC_HALO, w, 3, s),
                  _rows(ts, w), _vec(w, 0, SC_KERNEL), _vec(w)],
        out_specs=[_rows(ts, EVEN_PAD), _rows(ts, w), pl.BlockSpec((HEADS // 2, 1, HEAD_PAD, ts), lambda i: (0, i, 0, 0)),
                   _vec(w, 0, SC_KERNEL), _vec(w)],
        out_shape=[_sds((s, EVEN_PAD), MXU_DTYPE), _sds((s, w)), _sds((HEADS // 2, n, HEAD_PAD, ts), MXU_DTYPE),
                   _sds((SC_KERNEL, w)), _sds((1, w))],
        scratch_shapes=[pltpu.VMEM((ts + SC_HALO, w), F32), pltpu.VMEM((ts + SC_HALO, w), F32)],
        compiler_params=_cp(1))(dy, dy, dy, z, z, z, z, z, z, z, z, z, o, sc_w, sc_b)


def _ln_act(uc, sg, g, b):
    mu = jnp.mean(uc, axis=-1, keepdims=True)
    var = jnp.mean(jnp.square(uc - mu), axis=-1, keepdims=True)
    return _silu((uc - mu) * lax.rsqrt(var + EPS) * g + b) * _silu(sg)


def _shifted_copies(buf, shifted, rows):
    for b in range(1, SUBLANES):
        shifted[b - 1, 0:rows, :] = buf[pl.ds(b, rows), :]


def _rows_at(buf, shifted, start, n):
    a, b = divmod(start, SUBLANES)
    return buf[pl.ds(SUBLANES * a, n), :] if b == 0 else shifted[b - 1, pl.ds(SUBLANES * a, n), :]


def _odd_fwd(z, conv_w, conv_b, ln_g, ln_b, ts):
    s = z.shape[0]
    d = D_MODEL
    k = CONF_KERNEL

    def body(val_ref, glu_ref, sg_ref, hval_ref, hglu_ref, w_ref, b_ref, g_ref, beta_ref, y_ref, uc_ref, ubuf, ush):
        i = pl.program_id(0)
        ubuf[0:CONF_HALO, :] = jnp.where(i > 0, hval_ref[...] * _sigmoid(hglu_ref[...]), 0.0)
        ubuf[CONF_HALO:, :] = val_ref[...] * _sigmoid(glu_ref[...])
        _shifted_copies(ubuf, ush, ts + CONF_HALO - SUBLANES)
        for r0 in range(0, ts, CONV_ROWS):
            acc = jnp.broadcast_to(b_ref[...], (CONV_ROWS, d))
            for j in range(k):
                acc = acc + w_ref[j:j + 1, :] * _rows_at(ubuf, ush, r0 + CONF_HALO - (k - 1) + j, CONV_ROWS)
            uc_ref[r0:r0 + CONV_ROWS, :] = acc
        y_ref[...] = _ln_act(uc_ref[...], sg_ref[...], g_ref[...], beta_ref[...]).astype(y_ref.dtype)

    return pl.pallas_call(
        body, name="odd_fwd", grid=(s // ts,),
        in_specs=[_rows(ts, d, 0), _rows(ts, d, 1), _rows(ts, d, 2),
                  _prev_halo(ts, CONF_HALO, d, 0), _prev_halo(ts, CONF_HALO, d, 1),
                  _vec(d, 0, k), _vec(d), _vec(d), _vec(d)],
        out_specs=[_rows(ts, d), _rows(ts, d)], out_shape=[_sds((s, d), MXU_DTYPE), _sds((s, d))],
        scratch_shapes=[pltpu.VMEM((ts + CONF_HALO, d), F32),
                        pltpu.VMEM((SUBLANES - 1, ts + CONF_HALO - SUBLANES, d), F32)],
        compiler_params=_cp(1))(z, z, z, z, z, conv_w, conv_b, ln_g, ln_b)


def _odd_bwd(dy, z, uc, conv_w, ln_g, ln_b, ts):
    s = z.shape[0]
    d = D_MODEL
    k = CONF_KERNEL
    n = s // ts

    def body(dy_ref, dyn_ref, val_ref, glu_ref, sg_ref, sgn_ref, uc_ref, ucn_ref,
             w_ref, g_ref, beta_ref, dz_ref, dw_ref, db_ref, dg_ref, dbeta_ref, dbuf, dsh, dw_acc):
        i = pl.program_id(0)
        val, glu = val_ref[...], glu_ref[...]
        sig = _sigmoid(glu)
        u = val * sig
        _, vjp = jax.vjp(_ln_act, uc_ref[...], sg_ref[...], g_ref[...], beta_ref[...])
        duc, dsg, dg, dbeta = vjp(dy_ref[...])
        _, vjp_n = jax.vjp(_ln_act, ucn_ref[...], sgn_ref[...], g_ref[...], beta_ref[...])
        dbuf[0:ts, :] = duc
        dbuf[ts:, :] = jnp.where(i < n - 1, vjp_n(dyn_ref[...])[0], 0.0)
        dz_ref[:, 2 * d:] = dsg.astype(dz_ref.dtype)
        _shifted_copies(dbuf, dsh, ts + CONF_HALO - SUBLANES)

        @pl.when(i == 0)
        def _():
            dw_acc[...] = jnp.zeros_like(dw_acc)
            db_ref[...] = jnp.zeros_like(db_ref)
            dg_ref[...] = jnp.zeros_like(dg_ref)
            dbeta_ref[...] = jnp.zeros_like(dbeta_ref)

        db_ref[...] += jnp.sum(duc, axis=0, keepdims=True)
        dg_ref[...] += dg
        dbeta_ref[...] += dbeta
        for r0 in range(0, ts, CONV_ROWS):
            acc = jnp.zeros((CONV_ROWS, d), F32)
            for j in range(k):
                acc = acc + w_ref[j:j + 1, :] * _rows_at(dbuf, dsh, r0 + (k - 1) - j, CONV_ROWS)
            sig_r = sig[r0:r0 + CONV_ROWS, :]
            dz_ref[r0:r0 + CONV_ROWS, 0:d] = (acc * sig_r).astype(dz_ref.dtype)
            dz_ref[r0:r0 + CONV_ROWS, d:2 * d] = (acc * val[r0:r0 + CONV_ROWS, :] * sig_r * (1.0 - sig_r)).astype(dz_ref.dtype)
        for j in range(k):
            prod = _rows_at(dbuf, dsh, (k - 1) - j, ts) * u
            dw_acc[j] += jnp.sum(prod.reshape(ts // SUBLANES, SUBLANES, d), axis=0)

        @pl.when(i == n - 1)
        def _():
            dw_ref[...] = jnp.sum(dw_acc[...], axis=1)

    return pl.pallas_call(
        body, name="odd_bwd", grid=(n,),
        in_specs=[_rows(ts, d), _next_halo(ts, CONF_HALO, d, 0, s),
                  _rows(ts, d, 0), _rows(ts, d, 1), _rows(ts, d, 2), _next_halo(ts, CONF_HALO, d, 2, s),
                  _rows(ts, d), _next_halo(ts, CONF_HALO, d, 0, s),
                  _vec(d, 0, k), _vec(d), _vec(d)],
        out_specs=[_rows(ts, ODD_IN), _vec(d, 0, k), _vec(d), _vec(d), _vec(d)],
        out_shape=[_sds((s, ODD_IN), MXU_DTYPE), _sds((k, d)), _sds((1, d)), _sds((1, d)), _sds((1, d))],
        scratch_shapes=[pltpu.VMEM((ts + CONF_HALO, d), F32),
                        pltpu.VMEM((SUBLANES - 1, ts + CONF_HALO - SUBLANES, d), F32), pltpu.VMEM((k, SUBLANES, d), F32)],
        compiler_params=_cp(1))(dy, dy, z, z, z, z, uc, uc, conv_w, ln_g, ln_b)


def _local_step(x, target, cos, sin, mod, p, layer_weights, fwd_dep=None, grads_done=None):
    s = x.shape[0]
    tsf, tsb = min(512, s // 2), min(256, s // 2)
    tq = min(512, s // 2)
    row1 = lambda a, i: a[i:i + 1]
    saved = []
    h = _pre_fwd(x, row1(p["pre_norm_g"], 0), row1(mod, 0), tsf, fwd_dep)
    for layer in range(DEPTH):
        i = layer // 2
        mod_l = row1(mod, layer)
        wl = layer_weights(layer, h)
        if layer % 2 == 0:
            z = _mm(h, wl["w_in"], "nn", F32, 512, EVEN_PAD, "even_in_fwd")
            if "late" in wl:
                wl.update(wl.pop("late")(z))
            q, q_t, k, v = _mla_prep_fwd(z, cos, sin, row1(p["even_q_norm_g"], i), row1(p["even_kv_norm_g"], i),
                                    wl["wq"], wl["wq_rot"], wl["wuk"], wl["wuv"], tsf)
            o, lse = _attn_fwd(q, k, v, tq)
            y = _even_gate_fwd(z, o, wl["sc_conv_w"], row1(p["even_sc_conv_b"], i), tsf)
            yo = _mm(y, wl["w_out"], "nn", F32, 1024, 1024, "even_out_fwd")
            saved.append((x, h, z, y, yo, wl, (q, q_t, k, v, o, lse)))
        else:
            z = _mm(h, wl["w_in"], "nn", F32, 512, ODD_IN, "odd_in_fwd")
            y, uc = _odd_fwd(z, wl["conv_w"], wl["conv_b"], wl["ln_g"], wl["ln_b"], tsf)
            yo = _mm(y, wl["w_out"], "nn", F32, 1024, 1024, "odd_out_fwd")
            saved.append((x, h, z, y, yo, wl, uc))
        if layer + 1 < DEPTH:
            x, h = _post_pre_fwd(x, yo, row1(p["post_norm_g"], layer), mod_l, row1(p["pre_norm_g"], layer + 1),
                                 row1(mod, layer + 1), tsf)
        else:
            loss, dx = _post_loss(x, yo, row1(p["post_norm_g"], layer), mod_l, target, tsf)

    g = {n: [None] * (DEPTH if n in ("pre_norm_g", "post_norm_g") else N_PAIRS) for n in (
        "pre_norm_g", "post_norm_g", "even_sc_conv_w", "even_sc_conv_b", "even_q_norm_g", "even_kv_norm_g",
        "odd_conv_w", "odd_conv_b", "odd_ln_g", "odd_ln_b")}
    dmod = [None] * DEPTH
    dep = None
    for layer in reversed(range(DEPTH)):
        i = layer // 2
        mod_l = row1(mod, layer)
        x_in, h, z, y, yo, wl, extra = saved[layer]
        dyo, dgate, g["post_norm_g"][layer] = _post_bwd(dx, yo, row1(p["post_norm_g"], layer), mod_l, tsf, dep)
        bufs = {}
        if layer % 2 == 0:
            q, q_t, k, v, o, lse = extra
            dy = _mm(dyo, wl["w_out"], "nt", F32, 1024, 1024, "even_out_bwd_x")
            bufs["even_w_out"] = _mm_tn_shards(y, dyo, "rows", "even_out_bwd_w")
            dz, do, do_t, g["even_sc_conv_w"][i], g["even_sc_conv_b"][i] = _even_gate_bwd(
                dy, z, o, wl["sc_conv_w"], row1(p["even_sc_conv_b"], i), tsf)
            dq, dk, dv = _attn_bwd(q, q_t, k, v, do, do_t, o, lse, tq)
            dz, bufs["even_mla"], g["even_q_norm_g"][i], g["even_kv_norm_g"][i] = _mla_prep_bwd(
                dz, dq, dk, dv, z, cos, sin, row1(p["even_q_norm_g"], i), row1(p["even_kv_norm_g"], i),
                wl["wq"], wl["wuk"], wl["wuv"], tsb)
            bufs["even_w_in"] = _ein_to_shards(_mm(h, dz, "tn", F32, D_MODEL, 512, "even_in_bwd_w"))
        else:
            uc = extra
            dy = _mm(dyo, wl["w_out"], "nt", F32, 1024, 1024, "odd_out_bwd_x")
            bufs["odd_w_out"] = _mm_tn_shards(y, dyo, "rows", "odd_out_bwd_w")
            dz, g["odd_conv_w"][i], g["odd_conv_b"][i], g["odd_ln_g"][i], g["odd_ln_b"][i] = _odd_bwd(
                dy, z, uc, wl["conv_w"], wl["ln_g"], wl["ln_b"], tsb)
            bufs["odd_w_in"] = _mm_tn_shards(h, dz, "cols", "odd_in_bwd_w")
        dx, dshift, dscale, g["pre_norm_g"][layer] = _pre_bwd(
            dz, wl["w_in"], dx, x_in, row1(p["pre_norm_g"], layer), mod_l, tsf)
        dmod[layer] = jnp.concatenate([dshift, dscale, dgate], axis=-1)
        dep = grads_done(layer, bufs, dx) if grads_done is not None else None
    stack = lambda parts: jnp.stack([a[0] if a.shape[0] == 1 and a.ndim == 2 else a for a in parts])
    small = {n: stack(parts) for n, parts in g.items()}
    small["dmod"] = jnp.concatenate(dmod, axis=0)
    return loss, dx, small


def _uq_to_heads(w):
    w = w.reshape(N_CHIPS, Q_LORA, 2, QK_NOPE + QK_ROPE).transpose(0, 2, 1, 3).reshape(HEADS, Q_LORA, QK_NOPE + QK_ROPE)
    half = QK_ROPE // 2
    rotated = jnp.concatenate([jnp.zeros_like(w[..., :QK_NOPE]), -w[..., QK_NOPE + half:], w[..., QK_NOPE:QK_NOPE + half]],
                              axis=-1)
    pad = ((0, 0), (0, 0), (0, HEAD_PAD - QK_NOPE - QK_ROPE))
    return _side_by_side(jnp.pad(w, pad)), _side_by_side(jnp.pad(rotated, pad))


def _side_by_side(w):
    return w.transpose(1, 0, 2).reshape(w.shape[1], HEADS * HEAD_PAD)


def _ukv_to_heads(w):
    w = w.reshape(N_CHIPS, KV_LORA, 2, QK_NOPE + V_HEAD).transpose(0, 2, 1, 3).reshape(HEADS, KV_LORA, QK_NOPE + V_HEAD)
    wk = jnp.pad(w[..., :QK_NOPE], ((0, 0), (0, 0), (0, HEAD_PAD - QK_NOPE)))
    wv = w[..., QK_NOPE:]
    zero = jnp.zeros_like(wv)
    odd = (jnp.arange(HEADS) % 2 == 1)[:, None, None]
    wv = jnp.concatenate([jnp.where(odd, zero, wv), jnp.where(odd, wv, zero)], axis=-1)
    return _side_by_side(wk), _side_by_side(wv)


def _mla_local(q):
    blocks = q.reshape(2, MLA_ROWS, HEAD_PAD)
    uq = jnp.concatenate([blocks[r, :Q_LORA, :QK_NOPE + QK_ROPE] for r in range(2)], axis=-1)
    ukv = jnp.concatenate(
        [jnp.concatenate([blocks[r, Q_LORA:Q_LORA + KV_LORA, :QK_NOPE],
                          blocks[r, Q_LORA + KV_LORA:, V_HEAD * r:V_HEAD * (r + 1)]], axis=-1) for r in range(2)], axis=-1)
    return uq, ukv


def _place():
    return lax.axis_index("x"), lax.axis_index("y"), lax.axis_index("c")


def _flip(v, bit):
    return 1 - v if bit else v


def _sem(a, k):
    return a * (N_CHIPS - 1) + k - 1


def _remote(src, dst, send_sem, recv_sem, peer):
    return pltpu.make_async_remote_copy(src_ref=src, dst_ref=dst, send_sem=send_sem, recv_sem=recv_sem,
                                        device_id=peer, device_id_type=MESH)


_VMEM_SPEC = pl.BlockSpec(memory_space=pltpu.VMEM)
_HBM_SPEC = pl.BlockSpec(memory_space=pl.ANY)


def _ada_fwd(c8, ada_w, ada_b_sh):
    depth, d, cols = ada_w.shape

    def body(c_ref, w_ref, b_ref, call_ref, mod_ref, s1, r1, s2, r2):
        x, y, c = _place()
        chip = 2 * x + y
        me = 2 * chip + c
        call_ref[me] = c_ref[...]
        sends = []
        for k in range(1, N_DEV):
            peer = (_flip(x, k & 4), _flip(y, k & 2), _flip(c, k & 1))
            cp = _remote(c_ref, call_ref.at[me], s1.at[k - 1], r1.at[k - 1], peer)
            cp.start()
            sends.append(cp)
        for k in range(1, N_DEV):
            src = 4 * _flip(x, k & 4) + 2 * _flip(y, k & 2) + _flip(c, k & 1)
            _remote(c_ref, call_ref.at[src], s1.at[k - 1], r1.at[k - 1], (x, y, c)).wait_recv()
        act = _silu(jnp.concatenate([call_ref[e, 0:1, :] for e in range(N_DEV)], axis=0))
        for l in range(depth):
            mod_ref[chip, l] = _dot_nn(act, w_ref[l]) + b_ref[l:l + 1, :]
        for k in range(1, N_CHIPS):
            peer = (_flip(x, k & 2), _flip(y, k & 1), c)
            cp = _remote(mod_ref.at[chip], mod_ref.at[chip], s2.at[k - 1], r2.at[k - 1], peer)
            cp.start()
            sends.append(cp)
        for k in range(1, N_CHIPS):
            src = 2 * _flip(x, k & 2) + _flip(y, k & 1)
            _remote(mod_ref.at[src], mod_ref.at[src], s2.at[k - 1], r2.at[k - 1], (x, y, c)).wait_recv()
        for cp in sends:
            cp.wait_send()

    return pl.pallas_call(
        body, name="ada_fwd", in_specs=[_VMEM_SPEC] * 3, out_specs=[_VMEM_SPEC] * 2,
        out_shape=[_sds((N_DEV, 8, d)), _sds((N_CHIPS, depth, N_DEV, cols))],
        scratch_shapes=[pltpu.SemaphoreType.DMA((N_DEV - 1,)), pltpu.SemaphoreType.DMA((N_DEV - 1,)),
                        pltpu.SemaphoreType.DMA((N_CHIPS - 1,)), pltpu.SemaphoreType.DMA((N_CHIPS - 1,))],
        compiler_params=pltpu.CompilerParams(vmem_limit_bytes=VMEM_LIMIT_V7X))(c8, ada_w, ada_b_sh)


def _ada_bwd(c_t, dmod_sh):
    depth, n, cols = dmod_sh.shape
    d = c_t.shape[0]
    tr = 256

    def body(c_ref, dm_ref, o_ref):
        act = _silu(c_ref[...])
        acc = act[:, 0:1] * dm_ref[0, 0:1, :]
        for e in range(1, n):
            acc = acc + act[:, e:e + 1] * dm_ref[0, e:e + 1, :]
        o_ref[0] = acc

    return pl.pallas_call(
        body, name="ada_bwd", grid=(depth, d // tr),
        in_specs=[pl.BlockSpec((tr, n), lambda l, i: (i, 0)), pl.BlockSpec((1, n, cols), lambda l, i: (l, 0, 0))],
        out_specs=pl.BlockSpec((1, tr, cols), lambda l, i: (l, i, 0)), out_shape=_sds((depth, d, cols)),
        compiler_params=_cp(2))(c_t, dmod_sh)


def _gathered_shape(shape, how):
    if how == "slot":
        return (N_CHIPS,) + shape
    r, cc = shape
    return (r, N_CHIPS * cc) if how == "cols" else (N_CHIPS * r, cc)


def _gathered_part(ref, shape, how, chip):
    if how == "slot":
        return ref.at[chip]
    if how == "cols":
        return ref.at[:, pl.ds(pl.multiple_of(chip * shape[1], 128), shape[1])]
    return ref.at[pl.ds(pl.multiple_of(chip * shape[0], 8), shape[0]), :]


_SEM_SPEC = pl.BlockSpec(memory_space=pltpu.SEMAPHORE)
_TOKEN = jax.ShapeDtypeStruct((8, 128), F32)
_SPLIT_COPY = pltpu.CompilerParams(has_side_effects=pltpu.SideEffectType.DATAFLOW_SIDE_EFFECTING)


def _in_hbm(a):
    return pltpu.with_memory_space_constraint(a, pltpu.HBM)


def _gather_start(items, gathered, name, after=()):
    n = len(items)

    def body(*refs):
        ins, outs = refs[:n], refs[n:2 * n]
        send_sems, recv_sems = refs[2 * n + len(after)], refs[2 * n + len(after) + 1]
        x, y, c = _place()
        for a in range(n):
            for k in range(1, N_CHIPS):
                part = _gathered_part(outs[a], items[a][0].shape, items[a][1], 2 * x + y)
                _remote(ins[a], part, send_sems.at[_sem(a, k)], recv_sems.at[_sem(a, k)],
                        (_flip(x, k & 2), _flip(y, k & 1), c)).start()
        refs[-1][...] = jnp.zeros(_TOKEN.shape, _TOKEN.dtype)

    arrays = [_in_hbm(a) for a, _ in items] + [_in_hbm(a) for a in gathered]
    res = pl.pallas_call(
        body, name=name, in_specs=[_HBM_SPEC] * (2 * n + len(after)),
        out_specs=[_SEM_SPEC, _SEM_SPEC] + [_HBM_SPEC] * (2 * n) + [_VMEM_SPEC],
        out_shape=[pltpu.SemaphoreType.DMA((n * (N_CHIPS - 1),)), pltpu.SemaphoreType.DMA((n * (N_CHIPS - 1),))]
        + [pltpu.HBM(a.shape, a.dtype) for a in arrays] + [_TOKEN],
        input_output_aliases={a: 2 + a for a in range(2 * n)}, compiler_params=_SPLIT_COPY)(*arrays, *after)
    return res[0], res[1], res[2:2 + n], res[2 + n:2 + 2 * n], res[-1]


def _gather_wait(items, started, after, name):
    n = len(items)
    send_sems, recv_sems, shards, gathered, _ = started

    def body(*refs):
        ins, outs, send_sems, recv_sems = refs[:n], refs[n:2 * n], refs[2 * n], refs[2 * n + 1]
        x, y, c = _place()
        for a in range(n):
            for k in range(1, N_CHIPS):
                part = _gathered_part(outs[a], items[a][0].shape, items[a][1], 2 * _flip(x, k & 2) + _flip(y, k & 1))
                cp = _remote(ins[a], part, send_sems.at[_sem(a, k)], recv_sems.at[_sem(a, k)], (x, y, c))
                cp.wait_send()
                cp.wait_recv()

    res = pl.pallas_call(
        body, name=name, in_specs=[_HBM_SPEC] * (2 * n) + [_SEM_SPEC, _SEM_SPEC] + [_HBM_SPEC] * len(after),
        out_specs=[_HBM_SPEC] * (2 * n), out_shape=[pltpu.HBM(a.shape, a.dtype) for a in (*shards, *gathered)],
        input_output_aliases={a: a for a in range(2 * n)}, compiler_params=_SPLIT_COPY)(
            *shards, *gathered, send_sems, recv_sems, *after)
    return res[n:]


def _rs_start(bufs, name, after=()):
    n = len(bufs)

    def body(*refs):
        srcs, lands = refs[:n], refs[n:2 * n]
        send_sems, recv_sems = refs[2 * n + len(after)], refs[2 * n + len(after) + 1]
        x, y, c = _place()
        for a in range(n):
            for k in range(1, N_CHIPS):
                tx, ty = _flip(x, k & 2), _flip(y, k & 1)
                _remote(srcs[a].at[2 * tx + ty], lands[a].at[k - 1], send_sems.at[_sem(a, k)], recv_sems.at[_sem(a, k)],
                        (tx, ty, c)).start()
        refs[-1][...] = jnp.zeros(_TOKEN.shape, _TOKEN.dtype)

    arrays = [_in_hbm(b) for b in bufs] + [_in_hbm(lax.empty((N_CHIPS - 1,) + b.shape[1:], b.dtype)) for b in bufs]
    res = pl.pallas_call(
        body, name=name, in_specs=[_HBM_SPEC] * (2 * n + len(after)),
        out_specs=[_SEM_SPEC, _SEM_SPEC] + [_HBM_SPEC] * (2 * n) + [_VMEM_SPEC],
        out_shape=[pltpu.SemaphoreType.DMA((n * (N_CHIPS - 1),)), pltpu.SemaphoreType.DMA((n * (N_CHIPS - 1),))]
        + [pltpu.HBM(a.shape, a.dtype) for a in arrays] + [_TOKEN],
        input_output_aliases={a: 2 + a for a in range(2 * n)}, compiler_params=_SPLIT_COPY)(*arrays, *after)
    return res[0], res[1], res[2:2 + n], res[2 + n:2 + 2 * n], res[-1]


def _rs_wait(started, after, name):
    send_sems, recv_sems, bufs, lands, _ = started
    n = len(bufs)

    def body(*refs):
        srcs, lnds, send_sems, recv_sems = refs[:n], refs[n:2 * n], refs[2 * n], refs[2 * n + 1]
        x, y, c = _place()
        for a in range(n):
            for k in range(1, N_CHIPS):
                cp = _remote(srcs[a].at[0], lnds[a].at[k - 1], send_sems.at[_sem(a, k)], recv_sems.at[_sem(a, k)], (x, y, c))
                cp.wait_send()
                cp.wait_recv()

    res = pl.pallas_call(
        body, name=name, in_specs=[_HBM_SPEC] * (2 * n) + [_SEM_SPEC, _SEM_SPEC] + [_HBM_SPEC] * len(after),
        out_specs=[_HBM_SPEC] * (2 * n), out_shape=[pltpu.HBM(a.shape, a.dtype) for a in (*bufs, *lands)],
        input_output_aliases={a: a for a in range(2 * n)}, compiler_params=_SPLIT_COPY)(
            *bufs, *lands, send_sems, recv_sems, *after)
    return res[:n], res[n:]


def _place_own(shard, how, chip_idx):
    r, cc = shard.shape
    block, index = {"slot": ((1, r, cc), lambda i, c: (c[0], 0, 0)), "cols": ((r, cc), lambda i, c: (0, c[0])),
                    "rows": ((r, cc), lambda i, c: (c[0], 0))}[how]

    def body(c_ref, in_ref, o_ref):
        del c_ref
        o_ref[...] = in_ref[...].reshape(o_ref.shape)

    return pl.pallas_call(
        body, name="place_own", out_shape=_sds(_gathered_shape(shard.shape, how), shard.dtype),
        grid_spec=pltpu.PrefetchScalarGridSpec(
            num_scalar_prefetch=1, grid=(1,), in_specs=[pl.BlockSpec((r, cc), lambda i, c: (0, 0))],
            out_specs=pl.BlockSpec(block, index)),
        compiler_params=_cp(1))(chip_idx, shard)


def _gather_sum_all(small):
    r, w = small.shape

    def body(in_ref, all_ref, sum_ref, send_sems, recv_sems):
        x, y, c = _place()
        me = 4 * x + 2 * y + c
        all_ref[me] = in_ref[...]
        sends = []
        for k in range(1, N_DEV):
            peer = (_flip(x, k & 4), _flip(y, k & 2), _flip(c, k & 1))
            cp = _remote(in_ref, all_ref.at[me], send_sems.at[k - 1], recv_sems.at[k - 1], peer)
            cp.start()
            sends.append(cp)
        for k in range(1, N_DEV):
            src = 4 * _flip(x, k & 4) + 2 * _flip(y, k & 2) + _flip(c, k & 1)
            _remote(in_ref, all_ref.at[src], send_sems.at[k - 1], recv_sems.at[k - 1], (x, y, c)).wait_recv()
        acc = all_ref[0]
        for e in range(1, N_DEV):
            acc = acc + all_ref[e]
        sum_ref[...] = acc
        for cp in sends:
            cp.wait_send()

    return pl.pallas_call(
        body, name="gather_sum_all", in_specs=[_VMEM_SPEC], out_specs=[_VMEM_SPEC] * 2,
        out_shape=[_sds((N_DEV, r, w)), _sds((r, w))],
        scratch_shapes=[pltpu.SemaphoreType.DMA((N_DEV - 1,)), pltpu.SemaphoreType.DMA((N_DEV - 1,))],
        compiler_params=pltpu.CompilerParams(vmem_limit_bytes=VMEM_LIMIT_V7X))(small)


def _add_chips(buf, t, chip_idx):
    r, cc = buf.shape[1:]
    tr = min(256, r)

    def body(c_ref, p_ref, t_ref, o_ref):
        del c_ref
        o_ref[...] = p_ref[0] + t_ref[0].astype(F32) + t_ref[1].astype(F32) + t_ref[2].astype(F32)

    return pl.pallas_call(
        body, name="add_chips", out_shape=_sds((r, cc)),
        grid_spec=pltpu.PrefetchScalarGridSpec(
            num_scalar_prefetch=1, grid=(r // tr,),
            in_specs=[pl.BlockSpec((1, tr, cc), lambda i, c: (c[0], i, 0)),
                      pl.BlockSpec((N_CHIPS - 1, tr, cc), lambda i, c: (0, i, 0))],
            out_specs=pl.BlockSpec((tr, cc), lambda i, c: (i, 0))),
        compiler_params=_cp(1))(chip_idx, buf, t)


def _rs_sibling(qs):
    n = len(qs)

    def body(*refs):
        ins, outs = refs[:n], refs[n:2 * n]
        send_sems, recv_sems = refs[2 * n:]
        x, y, c = _place()
        copies = [_remote(ins[a], outs[a], send_sems.at[a], recv_sems.at[a], (x, y, 1 - c)) for a in range(n)]
        for cp in copies:
            cp.start()
        for cp in copies:
            cp.wait()

    return pl.pallas_call(
        body, name="rs_sibling", in_specs=[_HBM_SPEC] * n, out_specs=[_HBM_SPEC] * n,
        out_shape=[_sds(q.shape) for q in qs],
        scratch_shapes=[pltpu.SemaphoreType.DMA((n,)), pltpu.SemaphoreType.DMA((n,))])(*qs)


def _adamw_update(w, g, m, v):
    m = ADAM_B1 * m + (1.0 - ADAM_B1) * g
    v = ADAM_B2 * v + (1.0 - ADAM_B2) * jnp.square(g)
    m_hat = m / (1.0 - ADAM_B1 ** ADAM_STEP)
    v_hat = v / (1.0 - ADAM_B2 ** ADAM_STEP)
    return -ADAM_LR * (m_hat / (jnp.sqrt(v_hat) + ADAM_EPS) + ADAM_WD * w), m, v


def _adamw(w, g_parts, m, v, name):
    shape = w.shape
    cols = shape[-1]
    rows = _size(shape[:-1])
    tr = 512 if rows % 512 == 0 else rows
    spec = pl.BlockSpec((tr, cols), lambda i: (i, 0))
    n = len(g_parts)
    n_out = 4 if n > 1 else 3

    def body(*refs):
        w_ref, m_ref, v_ref = refs[:3]
        d_ref, nm_ref, nv_ref = refs[-3:]
        g = refs[3][...]
        for r in refs[4:3 + n]:
            g = g + r[...]
        if n > 1:
            refs[3 + n][...] = g
        d_ref[...], nm_ref[...], nv_ref[...] = _adamw_update(w_ref[...], g, m_ref[...], v_ref[...])

    outs = pl.pallas_call(
        body, name="adamw_" + name, grid=(rows // tr,), in_specs=[spec] * (3 + n), out_specs=[spec] * n_out,
        out_shape=[_sds((rows, cols))] * n_out, compiler_params=_cp(1))(
            *[a.reshape(rows, cols) for a in (w, m, v, *g_parts)])
    outs = tuple(o.reshape(shape) for o in outs)
    return outs if n > 1 else (g_parts[0],) + outs


def _adamw_layer(w, g_parts, m, v, layer, prev, name):
    _, r, cc = w.shape
    tr = 512 if r % 512 == 0 else r
    spec = pl.BlockSpec((1, tr, cc), lambda i: (layer, i, 0))
    n = len(g_parts)

    def body(*refs):
        w_ref, m_ref, v_ref = refs[:3]
        g_ref, d_ref, nm_ref, nv_ref = refs[-4:]
        g = refs[3][...]
        for q in refs[4:3 + n]:
            g = g + q[...]
        g = g[:, :cc]
        g_ref[0] = g
        d_ref[0], nm_ref[0], nv_ref[0] = _adamw_update(w_ref[0], g, m_ref[0], v_ref[0])

    g_specs = [pl.BlockSpec((tr, q.shape[1]), lambda i: (i, 0)) for q in g_parts]
    passed = () if prev is None else tuple(prev)
    return pl.pallas_call(
        body, name="adamw_" + name, grid=(r // tr,),
        in_specs=[spec] * 3 + g_specs + [_HBM_SPEC] * len(passed), out_specs=[spec] * 4,
        out_shape=[_sds(w.shape)] * 4, input_output_aliases={3 + n + k: k for k in range(len(passed))},
        compiler_params=_cp(1))(w, m, v, *g_parts, *passed)


def _size(shape):
    n = 1
    for s in shape:
        n *= s
    return n


_SMALL = (("dmod", (DEPTH, 3 * D_MODEL)), ("pre_norm_g", (DEPTH, D_MODEL)), ("post_norm_g", (DEPTH, D_MODEL)),
          ("even_sc_conv_w", (2, SC_KERNEL, SC_WIDTH)), ("even_sc_conv_b", (2, SC_WIDTH)),
          ("even_q_norm_g", (2, Q_LORA)), ("even_kv_norm_g", (2, KV_LORA)),
          ("odd_conv_w", (2, CONF_KERNEL, D_MODEL)), ("odd_conv_b", (2, D_MODEL)), ("odd_ln_g", (2, D_MODEL)),
          ("odd_ln_b", (2, D_MODEL)))
SMALL_ROWS = -(-sum(_size(s) for _, s in _SMALL) // (8 * 128)) * 8

_SMALL_W = (("even_sc_conv_w", (2, SC_KERNEL, SC_WIDTH // N_CHIPS)), ("odd_conv_w", (2, CONF_KERNEL, D_MODEL // N_CHIPS)),
            ("odd_conv_b", (2, D_MODEL // N_CHIPS)), ("odd_ln_g", (2, D_MODEL // N_CHIPS)),
            ("odd_ln_b", (2, D_MODEL // N_CHIPS)))
SMALL_W_ROWS = -(-sum(_size(s) for _, s in _SMALL_W) // (8 * 128)) * 8


def _pack_rows(arrays, layout, rows):
    flat = jnp.concatenate([arrays[n].reshape(-1) for n, _ in layout])
    return jnp.pad(flat, (0, rows * 128 - flat.shape[0])).reshape(rows, 128)


def _unpack_small(t):
    flat = t.reshape(-1)
    out, at = {}, 0
    for n, shape in _SMALL:
        out[n] = flat[at:at + _size(shape)].reshape(shape)
        at += _size(shape)
    return out


def _unpack_small_w(t):
    flat = t.reshape(N_CHIPS, -1)
    out, at = {}, 0
    for n, shape in _SMALL_W:
        a = flat[:, at:at + _size(shape)].reshape((N_CHIPS,) + shape)
        out[n] = jnp.moveaxis(a, 0, -2).reshape(shape[:-1] + (N_CHIPS * shape[-1],))
        at += _size(shape)
    return out


def _chip_cols(a, chip):
    n = a.shape[-1] // N_CHIPS
    return lax.dynamic_slice_in_dim(a, chip * n, n, axis=a.ndim - 1)


WEIGHT_NAMES = ("ada_w", "ada_b", "pre_norm_g", "post_norm_g", "even_w_in", "even_sc_conv_w", "even_sc_conv_b",
                "even_q_norm_g", "even_kv_norm_g", "even_w_uq", "even_w_ukv", "even_w_out", "odd_w_in", "odd_conv_w",
                "odd_conv_b", "odd_ln_g", "odd_ln_b", "odd_w_out")
GATHER_HOW = ((("even_w_in", "slot"), ("even_w_uq", "slot"), ("even_w_ukv", "slot"), ("even_w_out", "rows")),
              (("odd_w_in", "cols"), ("odd_w_out", "rows")))


def kernel(x, c, positions, ada_w, ada_b, pre_norm_g, post_norm_g, even_w_in, even_sc_conv_w, even_sc_conv_b, even_q_norm_g, even_kv_norm_g, even_w_uq, even_w_ukv, even_w_out, odd_w_in, odd_conv_w, odd_conv_b, odd_ln_g, odd_ln_b, odd_w_out, loss_target, m_ada_w, m_ada_b, m_pre_norm_g, m_post_norm_g, m_even_w_in, m_even_sc_conv_w, m_even_sc_conv_b, m_even_q_norm_g, m_even_kv_norm_g, m_even_w_uq, m_even_w_ukv, m_even_w_out, m_odd_w_in, m_odd_conv_w, m_odd_conv_b, m_odd_ln_g, m_odd_ln_b, m_odd_w_out, v_ada_w, v_ada_b, v_pre_norm_g, v_post_norm_g, v_even_w_in, v_even_sc_conv_w, v_even_sc_conv_b, v_even_q_norm_g, v_even_kv_norm_g, v_even_w_uq, v_even_w_ukv, v_even_w_out, v_odd_w_in, v_odd_conv_w, v_odd_conv_b, v_odd_ln_g, v_odd_ln_b, v_odd_w_out):
    w = dict(zip(WEIGHT_NAMES, (ada_w, ada_b, pre_norm_g, post_norm_g, even_w_in, even_sc_conv_w, even_sc_conv_b,
                                even_q_norm_g, even_kv_norm_g, even_w_uq, even_w_ukv, even_w_out, odd_w_in, odd_conv_w,
                                odd_conv_b, odd_ln_g, odd_ln_b, odd_w_out)))
    m = dict(zip(WEIGHT_NAMES, (m_ada_w, m_ada_b, m_pre_norm_g, m_post_norm_g, m_even_w_in, m_even_sc_conv_w,
                                m_even_sc_conv_b, m_even_q_norm_g, m_even_kv_norm_g, m_even_w_uq, m_even_w_ukv,
                                m_even_w_out, m_odd_w_in, m_odd_conv_w, m_odd_conv_b, m_odd_ln_g, m_odd_ln_b, m_odd_w_out)))
    v = dict(zip(WEIGHT_NAMES, (v_ada_w, v_ada_b, v_pre_norm_g, v_post_norm_g, v_even_w_in, v_even_sc_conv_w,
                                v_even_sc_conv_b, v_even_q_norm_g, v_even_kv_norm_g, v_even_w_uq, v_even_w_ukv,
                                v_even_w_out, v_odd_w_in, v_odd_conv_w, v_odd_conv_b, v_odd_ln_g, v_odd_ln_b, v_odd_w_out)))
    ix, iy, ic = _place()
    chip = 2 * ix + iy
    me = 2 * chip + ic
    s = x.shape[1]

    c_all, mod_all = _ada_fwd(jnp.broadcast_to(c, (8, D_MODEL)), ada_w, _chip_cols(ada_b, chip))
    mod = lax.dynamic_index_in_dim(mod_all, me, axis=2, keepdims=False)
    mod = mod.transpose(1, 0, 2).reshape(DEPTH, 3 * D_MODEL)

    items = [[(w[n][layer // 2].astype(MXU_DTYPE), how) for n, how in GATHER_HOW[layer % 2]] for layer in range(DEPTH)]
    groups = [items[0][:1], items[0][1:] + [(_pack_rows(w, _SMALL_W, SMALL_W_ROWS), "slot")],
              [item for layer_items in items[1:] for item in layer_items]]
    sent, dep = [], mod_all
    for number, group in enumerate(groups):
        sent.append(_gather_start(group, [_place_own(a, how, chip.reshape(1)) for a, how in group],
                                  "gather_start_%d" % number, [dep]))
        dep = sent[-1][-1]
    arrived = {}

    def group(number, after):
        if number not in arrived:
            arrived[number] = _gather_wait(groups[number], sent[number], after, "gather_wait_%d" % number)
        return arrived[number]

    def even_rest(i, uq, ukv, eout, small_w):
        wuk, wuv = _ukv_to_heads(ukv)
        wq, wq_rot = _uq_to_heads(uq)
        return {"wq": wq, "wq_rot": wq_rot, "wuk": wuk, "wuv": wuv, "w_out": eout, "sc_conv_w": small_w["even_sc_conv_w"][i]}

    def layer_weights(layer, h):
        i = layer // 2
        if layer == 0:
            def late(z):
                uq, ukv, eout, small = group(1, [z])
                return even_rest(i, uq, ukv, eout, _unpack_small_w(small))
            return {"w_in": _ein_from_shards(group(0, [h])[0]), "late": late}
        small_w = _unpack_small_w(group(1, [h])[-1])
        at = sum(len(layer_items) for layer_items in items[1:layer])
        arrays = group(2, [h])[at:at + len(items[layer])]
        if layer % 2 == 0:
            return {"w_in": _ein_from_shards(arrays[0]), **even_rest(i, *arrays[1:], small_w)}
        oin, oout = arrays
        return {"w_in": oin, "w_out": oout, "conv_w": small_w["odd_conv_w"][i], "conv_b": small_w["odd_conv_b"][i:i + 1],
                "ln_g": small_w["odd_ln_g"][i:i + 1], "ln_b": small_w["odd_ln_b"][i:i + 1]}

    in_flight, own, sib, last = {}, {}, {}, {}

    def land(layer, after):
        names, started, kept = in_flight.pop(layer)
        bufs, arrived = _rs_wait(started, after, "rs_wait_%d" % layer)
        sums = [_add_chips(b, t, chip.reshape(1)) for b, t in zip(bufs if kept is None else kept, arrived)]
        for n, mine, theirs in zip(names, sums, _rs_sibling(sums)):
            own[n, layer // 2], sib[n, layer // 2] = mine, theirs

    def grads_done(layer, bufs, dx_in):
        if layer + 1 in in_flight:
            land(layer + 1, [dx_in])
        if layer == 0:
            last.update(bufs)
            return None
        names = sorted(bufs)
        in_flight[layer] = (names, _rs_start([bufs[n] for n in names], "rs_start_%d" % layer), None)
        return in_flight[layer][1][-1]

    p = {"pre_norm_g": pre_norm_g, "post_norm_g": post_norm_g, "even_sc_conv_b": even_sc_conv_b,
         "even_q_norm_g": even_q_norm_g, "even_kv_norm_g": even_kv_norm_g}
    inv_freq = 1.0 / (ROPE_THETA ** (jnp.arange(0, QK_ROPE, 2, dtype=F32) / QK_ROPE))
    inv_freq = jnp.zeros((1, HEAD_PAD), F32).at[0, QK_NOPE:QK_NOPE + QK_ROPE].set(jnp.tile(inv_freq, 2))
    cos, sin = _rope_tables(positions.reshape(s, 1), inv_freq)

    loss, dx, g = _local_step(x[0], loss_target[0], cos, sin, mod, p, layer_weights, dep, grads_done)

    grads, deltas, new_m, new_v = {}, {}, {}, {}

    def update_layers(n, results, pairs):
        for i in pairs:
            results = _adamw_layer(w[n], [own[n, i], sib[n, i]], m[n], v[n], i, results, n)
        return results

    small_all, small_sum = _gather_sum_all(_pack_rows(g, _SMALL, SMALL_ROWS))
    names = sorted(last)
    kept = [last[n] for n in names]
    in_flight[0] = (names, _rs_start([b.astype(jnp.bfloat16) for b in kept], "rs_start_0", [small_sum]), kept)
    tot = _unpack_small(small_sum)
    dmod_all = small_all[:, :DEPTH * 3 * D_MODEL // 128].reshape(N_DEV, DEPTH, 3 * D_MODEL)
    grads["ada_w"] = _ada_bwd(c_all[:, 0, :].T, _chip_cols(dmod_all, chip).transpose(1, 0, 2))
    grads["ada_b"] = tot["dmod"]
    for n in ("pre_norm_g", "post_norm_g", "even_sc_conv_b", "even_q_norm_g", "even_kv_norm_g"):
        grads[n] = tot[n]
    for n in ("even_sc_conv_w", "odd_conv_w", "odd_conv_b", "odd_ln_g", "odd_ln_b"):
        grads[n] = _chip_cols(tot[n], chip)
    for n in list(grads):
        _, deltas[n], new_m[n], new_v[n] = _adamw(w[n], [grads[n]], m[n], v[n], n)

    for n in ("odd_w_in", "odd_w_out"):
        grads[n], deltas[n], new_m[n], new_v[n] = update_layers(n, None, (1, 0))
    partly = {n: update_layers(n, None, (1,)) for n in ("even_w_in", "even_w_out")}
    land(0, [deltas["ada_w"], deltas["odd_w_in"], partly["even_w_in"][1]])
    for n in ("even_w_in", "even_w_out"):
        grads[n], deltas[n], new_m[n], new_v[n] = update_layers(n, partly[n], (0,))
    uq_parts, ukv_parts = zip(*[[jnp.stack(part) for part in zip(*[_mla_local(q["even_mla", i]) for i in range(N_PAIRS)])]
                                for q in (own, sib)])
    for n, parts in (("even_w_uq", uq_parts), ("even_w_ukv", ukv_parts)):
        grads[n], deltas[n], new_m[n], new_v[n] = _adamw(w[n], list(parts), m[n], v[n], n)

    total_loss = lax.psum(loss[0, 0], ("x", "y", "c"))
    return (total_loss, dx[None], *[grads[n] for n in WEIGHT_NAMES], *[deltas[n] for n in WEIGHT_NAMES],
            *[new_m[n] for n in WEIGHT_NAMES], *[new_v[n] for n in WEIGHT_NAMES])
```

```python
import jax
import jax.numpy as jnp
from jax import lax
from jax.experimental import pallas as pl
from jax.experimental.pallas import tpu as pltpu

F32 = jnp.float32
MXU_DTYPE = jnp.bfloat16
MESH = pl.DeviceIdType.MESH
VMEM_LIMIT_V7X = 56 * 2 ** 20

EPS = 1e-6
D_MODEL = 1024
DEPTH = 4
CHUNK = 64
SC_WIDTH = 512
SC_KERNEL = 3
SC_HALO = 8
HEADS = 8
QK_NOPE = 64
QK_ROPE = 32
V_HEAD = 64
HEAD_PAD = 128
Q_LORA = 256
KV_LORA = 128
ROPE_THETA = 10000.0
CONF_KERNEL = 31
CONF_HALO = 32
CONV_ROWS = 64
SUBLANES = 8
EVEN_IN = 2976
EVEN_PAD = 3072
ODD_IN = 3072
N_CHIPS = 4
N_DEV = 8
NEG = -1e30

ADAM_LR = 0.001
ADAM_B1 = 0.9
ADAM_B2 = 0.999
ADAM_EPS = 1e-08
ADAM_WD = 0.01
ADAM_STEP = 10

N_PAIRS = DEPTH // 2
EVEN_SHARD = EVEN_IN // N_CHIPS
EVEN_SHARD_PAD = 768
MLA_ROWS = Q_LORA + 2 * KV_LORA


def _cp(n_grid=0, **kw):
    return pltpu.CompilerParams(dimension_semantics=("arbitrary",) * n_grid,
                                vmem_limit_bytes=VMEM_LIMIT_V7X, **kw)


def _sigmoid(x):
    return 1.0 / (1.0 + jnp.exp(-x))


def _silu(x):
    return x * _sigmoid(x)


def _dsilu(x):
    s = _sigmoid(x)
    return s * (1.0 + x * (1.0 - s))


def _rms(x, g):
    return x * lax.rsqrt(jnp.mean(x * x, axis=-1, keepdims=True) + EPS) * g


def _dot(a, b, dims):
    return lax.dot_general(a.astype(MXU_DTYPE), b.astype(MXU_DTYPE), (dims, ((), ())),
                           preferred_element_type=F32)


def _dot_nn(a, b):
    return _dot(a, b, ((1,), (0,)))


def _dot_nt(a, b):
    return _dot(a, b, ((1,), (1,)))


def _dot_tn(a, b):
    return _dot(a, b, ((0,), (0,)))


def _rows(ts, w, cb=0):
    return pl.BlockSpec((ts, w), lambda i: (i, cb))


def _vec(w, cb=0, r=1):
    return pl.BlockSpec((r, w), lambda i: (0, cb))


def _prev_halo(ts, hr, w, cb):
    return pl.BlockSpec((hr, w), lambda i: (jnp.maximum(i * (ts // hr) - 1, 0), cb))


def _next_halo(ts, hr, w, cb, s):
    return pl.BlockSpec((hr, w), lambda i: (jnp.minimum((i + 1) * (ts // hr), s // hr - 1), cb))


def _sds(shape, dtype=F32):
    return jax.ShapeDtypeStruct(shape, dtype)


def _mm(a, b, mode, out_dtype, tm, tn, name):
    tm = min(tm, a.shape[1] if mode == "tn" else a.shape[0])
    tn = min(tn, b.shape[0] if mode == "nt" else b.shape[1])
    if mode == "nn":
        (m, k), n = a.shape, b.shape[1]
        a_spec = pl.BlockSpec((tm, k), lambda i, j: (i, 0))
        b_spec = pl.BlockSpec((k, tn), lambda i, j: (0, j))
        dot = _dot_nn
    elif mode == "nt":
        (m, k), n = a.shape, b.shape[0]
        a_spec = pl.BlockSpec((tm, k), lambda i, j: (i, 0))
        b_spec = pl.BlockSpec((tn, k), lambda i, j: (j, 0))
        dot = _dot_nt
    else:
        (k, m), n = a.shape, b.shape[1]
        a_spec = pl.BlockSpec((k, tm), lambda i, j: (0, i))
        b_spec = pl.BlockSpec((k, tn), lambda i, j: (0, j))
        dot = _dot_tn
    assert m % tm == 0 and n % tn == 0, (name, m, n, tm, tn)

    def body(a_ref, b_ref, o_ref):
        o_ref[...] = dot(a_ref[...], b_ref[...]).astype(o_ref.dtype)

    return pl.pallas_call(
        body, name=name, grid=(m // tm, n // tn), in_specs=[a_spec, b_spec],
        out_specs=pl.BlockSpec((tm, tn), lambda i, j: (i, j)), out_shape=_sds((m, n), out_dtype),
        compiler_params=_cp(2))(a, b)


def _mm_tn_shards(a, b, by, name):
    k, m = a.shape
    n = b.shape[1]
    if by == "cols":
        tm, tn = m, n // N_CHIPS
        shape, grid = (N_CHIPS, m, tn), (1, N_CHIPS)
        out_spec = pl.BlockSpec((1, tm, tn), lambda i, j: (j, i, 0))
    else:
        tm, tn = m // N_CHIPS, n
        shape, grid = (N_CHIPS, tm, n), (N_CHIPS, 1)
        out_spec = pl.BlockSpec((1, tm, tn), lambda i, j: (i, 0, j))

    def body(a_ref, b_ref, o_ref):
        o_ref[0] = _dot_tn(a_ref[...], b_ref[...])

    return pl.pallas_call(
        body, name=name, grid=grid,
        in_specs=[pl.BlockSpec((k, tm), lambda i, j: (0, i)), pl.BlockSpec((k, tn), lambda i, j: (0, j))],
        out_specs=out_spec, out_shape=_sds(shape), compiler_params=_cp(2))(a, b)


def _even_col(q):
    return q if q < 2432 else (q + 64 if q < 2464 else q + 96)


def _shard_pieces(j):
    lo, hi = EVEN_SHARD * j, EVEN_SHARD * (j + 1)
    cuts = [lo] + [b for b in (2432, 2464) if lo < b < hi] + [hi]
    return [(a - lo, _even_col(a), b - a) for a, b in zip(cuts[:-1], cuts[1:])]


def _ein_from_shards(w):
    _, d, _ = w.shape
    tr = 256

    def body(w_ref, o_ref):
        parts, at = [], 0
        for j in range(N_CHIPS):
            for d0, s0, n in _shard_pieces(j):
                if s0 > at:
                    parts.append(jnp.zeros((tr, s0 - at), F32))
                parts.append(w_ref[j, :, d0:d0 + n].astype(F32))
                at = s0 + n
        o_ref[...] = jnp.concatenate(parts, axis=1).astype(o_ref.dtype)

    return pl.pallas_call(
        body, name="ein_from_shards", grid=(d // tr,),
        in_specs=[pl.BlockSpec((N_CHIPS, tr, EVEN_SHARD), lambda i: (0, i, 0))],
        out_specs=_rows(tr, EVEN_PAD), out_shape=_sds((d, EVEN_PAD), w.dtype), compiler_params=_cp(1))(w)


def _ein_to_shards(dw):
    d = dw.shape[0]
    tr = 256

    def body(dw_ref, o_ref):
        for j in range(N_CHIPS):
            parts = [dw_ref[:, s0:s0 + n] for _, s0, n in _shard_pieces(j)]
            o_ref[j] = jnp.concatenate(parts + [jnp.zeros((tr, EVEN_SHARD_PAD - EVEN_SHARD), F32)], axis=1)

    return pl.pallas_call(
        body, name="ein_to_shards", grid=(d // tr,), in_specs=[_rows(tr, EVEN_PAD)],
        out_specs=pl.BlockSpec((N_CHIPS, tr, EVEN_SHARD_PAD), lambda i: (0, i, 0)),
        out_shape=_sds((N_CHIPS, d, EVEN_SHARD_PAD)), compiler_params=_cp(1))(dw)


def _rope_tables(pos_col, invf):
    s = pos_col.shape[0]
    ts = min(512, s)

    def body(p_ref, f_ref, c_ref, s_ref):
        ang = p_ref[...].astype(F32) * f_ref[...]
        lane = lax.broadcasted_iota(jnp.int32, ang.shape, 1)
        rope = (lane >= QK_NOPE) & (lane < QK_NOPE + QK_ROPE)
        c_ref[...] = jnp.where(lane < QK_NOPE, 1.0, jnp.where(rope, jnp.cos(ang), 0.0))
        s_ref[...] = jnp.where(rope, jnp.sin(ang), 0.0)

    return pl.pallas_call(
        body, name="rope_tables", grid=(s // ts,), in_specs=[_rows(ts, 1), _vec(HEAD_PAD)],
        out_specs=[_rows(ts, HEAD_PAD)] * 2, out_shape=[_sds((s, HEAD_PAD))] * 2,
        compiler_params=_cp(1))(pos_col, invf)


def _after(dep):
    return () if dep is None else (dep,)


def _pre_fwd(x, g, mod_l, ts, dep=None):
    s, d = x.shape

    def body(x_ref, g_ref, sh_ref, sc_ref, *rest):
        h = _rms(x_ref[...], g_ref[...]) * (1.0 + sc_ref[...]) + sh_ref[...]
        rest[-1][...] = h.astype(rest[-1].dtype)

    return pl.pallas_call(
        body, name="pre_fwd", grid=(s // ts,),
        in_specs=[_rows(ts, d), _vec(d), _vec(d, 0), _vec(d, 1)] + [_HBM_SPEC] * len(_after(dep)),
        out_specs=_rows(ts, d), out_shape=_sds((s, d), MXU_DTYPE), compiler_params=_cp(1))(
            x, g, mod_l, mod_l, *_after(dep))


def _pre_bwd(dz, w_in, dx_out, x, g, mod_l, ts):
    s, d = x.shape
    n_in = dz.shape[1]

    def f(xv, gv, sh, sc):
        return _rms(xv, gv) * (1.0 + sc) + sh

    def body(dz_ref, w_ref, dxo_ref, x_ref, g_ref, sh_ref, sc_ref, dx_ref, dsh_ref, dsc_ref, dg_ref):
        @pl.when(pl.program_id(0) == 0)
        def _():
            dsh_ref[...] = jnp.zeros_like(dsh_ref)
            dsc_ref[...] = jnp.zeros_like(dsc_ref)
            dg_ref[...] = jnp.zeros_like(dg_ref)

        _, vjp = jax.vjp(f, x_ref[...], g_ref[...], sh_ref[...], sc_ref[...])
        dx, dg, dsh, dsc = vjp(_dot_nt(dz_ref[...], w_ref[...]))
        dx_ref[...] = dxo_ref[...] + dx
        dsh_ref[...] += dsh
        dsc_ref[...] += dsc
        dg_ref[...] += dg

    return pl.pallas_call(
        body, name="pre_bwd", grid=(s // ts,),
        in_specs=[_rows(ts, n_in), _vec(n_in, 0, d), _rows(ts, d), _rows(ts, d), _vec(d), _vec(d, 0), _vec(d, 1)],
        out_specs=[_rows(ts, d), _vec(d), _vec(d), _vec(d)],
        out_shape=[_sds((s, d)), _sds((1, d)), _sds((1, d)), _sds((1, d))],
        compiler_params=_cp(1))(dz, w_in, dx_out, x, g, mod_l, mod_l)


def _post_pre_fwd(x, yo, g_post, mod_l, g_pre, mod_next, ts):
    s, d = x.shape

    def body(x_ref, yo_ref, gp_ref, gate_ref, g_ref, sh_ref, sc_ref, x_out_ref, h_ref):
        x_new = x_ref[...] + gate_ref[...] * _rms(yo_ref[...], gp_ref[...])
        x_out_ref[...] = x_new
        h_ref[...] = (_rms(x_new, g_ref[...]) * (1.0 + sc_ref[...]) + sh_ref[...]).astype(h_ref.dtype)

    return pl.pallas_call(
        body, name="post_pre_fwd", grid=(s // ts,),
        in_specs=[_rows(ts, d), _rows(ts, d), _vec(d), _vec(d, 2), _vec(d), _vec(d, 0), _vec(d, 1)],
        out_specs=[_rows(ts, d), _rows(ts, d)], out_shape=[_sds((s, d)), _sds((s, d), MXU_DTYPE)],
        compiler_params=_cp(1))(x, yo, g_post, mod_l, g_pre, mod_next, mod_next)


def _post_loss(x, yo, g_post, mod_l, target, ts):
    s, d = x.shape

    def body(x_ref, yo_ref, gp_ref, gate_ref, t_ref, loss_ref, dx_ref):
        err = x_ref[...] + gate_ref[...] * _rms(yo_ref[...], gp_ref[...]) - t_ref[...]
        dx_ref[...] = err * (1.0 / d)

        @pl.when(pl.program_id(0) == 0)
        def _():
            loss_ref[...] = jnp.zeros_like(loss_ref)

        loss_ref[...] += 0.5 * jnp.sum(jnp.sum(err * err, axis=-1, keepdims=True) * (1.0 / d), axis=0, keepdims=True)

    return pl.pallas_call(
        body, name="post_loss", grid=(s // ts,),
        in_specs=[_rows(ts, d), _rows(ts, d), _vec(d), _vec(d, 2), _rows(ts, d)],
        out_specs=[_vec(1), _rows(ts, d)], out_shape=[_sds((1, 1)), _sds((s, d))],
        compiler_params=_cp(1))(x, yo, g_post, mod_l, target)


def _post_bwd(dx_out, yo, g, mod_l, ts, dep=None):
    s, d = yo.shape

    def f(yov, gv, gate):
        return gate * _rms(yov, gv)

    def body(dx_ref, yo_ref, g_ref, gate_ref, *rest):
        dyo_ref, dgate_ref, dg_ref = rest[-3:]
        i = pl.program_id(0)
        _, vjp = jax.vjp(f, yo_ref[...], g_ref[...], gate_ref[...])
        dyo, dg, dgate = vjp(dx_ref[...])
        dyo_ref[...] = dyo.astype(dyo_ref.dtype)

        @pl.when(i == 0)
        def _():
            dgate_ref[...] = jnp.zeros_like(dgate_ref)
            dg_ref[...] = jnp.zeros_like(dg_ref)

        dgate_ref[...] += dgate
        dg_ref[...] += dg

    return pl.pallas_call(
        body, name="post_bwd", grid=(s // ts,),
        in_specs=[_rows(ts, d), _rows(ts, d), _vec(d), _vec(d, 2)] + [_HBM_SPEC] * len(_after(dep)),
        out_specs=[_rows(ts, d), _vec(d), _vec(d)],
        out_shape=[_sds((s, d), MXU_DTYPE), _sds((1, d)), _sds((1, d))],
        compiler_params=_cp(1))(dx_out, yo, g, mod_l, *_after(dep))


def _rope(t, cos, sin):
    lane = lax.broadcasted_iota(jnp.int32, t.shape, 1)
    first = (lane >= QK_NOPE) & (lane < QK_NOPE + QK_ROPE // 2)
    second = (lane >= QK_NOPE + QK_ROPE // 2) & (lane < QK_NOPE + QK_ROPE)
    up = pltpu.roll(t, QK_ROPE // 2, 1)
    down = pltpu.roll(t, HEAD_PAD - QK_ROPE // 2, 1)
    return t * cos + jnp.where(first, -down, jnp.where(second, up, 0.0)) * sin


def _rope_transposed(g, cos, sin):
    lane = lax.broadcasted_iota(jnp.int32, g.shape, 1)
    first = (lane >= QK_NOPE) & (lane < QK_NOPE + QK_ROPE // 2)
    second = (lane >= QK_NOPE + QK_ROPE // 2) & (lane < QK_NOPE + QK_ROPE)
    u = g * sin
    up = pltpu.roll(u, QK_ROPE // 2, 1)
    down = pltpu.roll(u, HEAD_PAD - QK_ROPE // 2, 1)
    return g * cos + jnp.where(first, down, jnp.where(second, -up, 0.0))


def _mla_prep_fwd(z, cos, sin, qg, kvg, wq, wq_rot, wuk, wuv, ts):
    s = z.shape[0]
    wide = HEADS * HEAD_PAD

    def body(cq_ref, ckv_ref, kr_ref, cos_ref, sin_ref, qg_ref, kvg_ref, wq_ref, wqr_ref, wuk_ref, wuv_ref,
             q_ref, qt_ref, k_ref, v_ref):
        cos_v, sin_v = cos_ref[...], sin_ref[...]
        cqn = _rms(cq_ref[...], qg_ref[...])
        ckvn = _rms(ckv_ref[...], kvg_ref[...])
        kr = _rope(kr_ref[...], cos_v, sin_v)
        q_lin, q_rot = _dot_nn(cqn, wq_ref[...]), _dot_nn(cqn, wqr_ref[...])
        k_lin, v_all = _dot_nn(ckvn, wuk_ref[...]), _dot_nn(ckvn, wuv_ref[...])
        for h in range(HEADS):
            lanes = slice(h * HEAD_PAD, (h + 1) * HEAD_PAD)
            qh = q_lin[:, lanes] * cos_v + q_rot[:, lanes] * sin_v
            q_ref[h] = qh.astype(q_ref.dtype)
            qt_ref[h, 0] = qh.T.astype(qt_ref.dtype)
            k_ref[h] = (k_lin[:, lanes] + kr).astype(k_ref.dtype)
            v_ref[h] = v_all[:, lanes].astype(v_ref.dtype)

    out = pl.BlockSpec((HEADS, ts, HEAD_PAD), lambda i: (0, i, 0))
    return pl.pallas_call(
        body, name="mla_prep_fwd", grid=(s // ts,),
        in_specs=[_rows(ts, Q_LORA, 8), _rows(ts, KV_LORA, 18), _rows(ts, HEAD_PAD, 19), _rows(ts, HEAD_PAD), _rows(ts, HEAD_PAD),
                  _vec(Q_LORA), _vec(KV_LORA), _vec(wide, 0, Q_LORA), _vec(wide, 0, Q_LORA), _vec(wide, 0, KV_LORA),
                  _vec(wide, 0, KV_LORA)],
        out_specs=[out, pl.BlockSpec((HEADS, 1, HEAD_PAD, ts), lambda i: (0, i, 0, 0)), out, out],
        out_shape=[_sds((HEADS, s, HEAD_PAD), MXU_DTYPE), _sds((HEADS, s // ts, HEAD_PAD, ts), MXU_DTYPE)]
        + [_sds((HEADS, s, HEAD_PAD), MXU_DTYPE)] * 2,
        compiler_params=_cp(1))(z, z, z, cos, sin, qg, kvg, wq, wq_rot, wuk, wuv)


def _mla_prep_bwd(dz, dq, dk, dv, z, cos, sin, qg, kvg, wq, wuk, wuv, ts):
    s = z.shape[0]

    def fq(cq, g):
        return _rms(cq, g)

    def body(dz_in_ref, dq_ref, dk_ref, dv_ref, cq_ref, ckv_ref, cos_ref, sin_ref, qg_ref, kvg_ref, wq_ref, wuk_ref,
             wuv_ref, dz_ref, dw_ref, dqg_ref, dkvg_ref):
        del dz_in_ref
        cos_v, sin_v = cos_ref[...], sin_ref[...]

        @pl.when(pl.program_id(0) == 0)
        def _():
            dw_ref[...] = jnp.zeros_like(dw_ref)
            dqg_ref[...] = jnp.zeros_like(dqg_ref)
            dkvg_ref[...] = jnp.zeros_like(dkvg_ref)

        cqn, vjp_q = jax.vjp(fq, cq_ref[...], qg_ref[...])
        ckvn, vjp_kv = jax.vjp(fq, ckv_ref[...], kvg_ref[...])
        lane = lax.broadcasted_iota(jnp.int32, (ts, HEAD_PAD), 1)
        rope_lanes = (lane >= QK_NOPE) & (lane < QK_NOPE + QK_ROPE)
        dq_lin = jnp.concatenate([_rope_transposed(dq_ref[h], cos_v, sin_v).astype(MXU_DTYPE) for h in range(HEADS)], axis=1)
        dk_all = jnp.concatenate([dk_ref[h].astype(MXU_DTYPE) for h in range(HEADS)], axis=1)
        dv_all = jnp.concatenate([dv_ref[h].astype(MXU_DTYPE) for h in range(HEADS)], axis=1)
        dkr = jnp.where(rope_lanes, dk_ref[0], 0.0)
        for h in range(1, HEADS):
            dkr = dkr + jnp.where(rope_lanes, dk_ref[h], 0.0)
        dcq, dqg = vjp_q(_dot_nt(dq_lin, wq_ref[...]))
        dckv, dkvg = vjp_kv(_dot_nt(dk_all, wuk_ref[...]) + _dot_nt(dv_all, wuv_ref[...]))
        dz_ref[:, 0:Q_LORA] = dcq.astype(dz_ref.dtype)
        dz_ref[:, Q_LORA:Q_LORA + KV_LORA] = dckv.astype(dz_ref.dtype)
        dz_ref[:, Q_LORA + KV_LORA:] = _rope_transposed(dkr, cos_v, sin_v).astype(dz_ref.dtype)
        dqg_ref[...] += dqg
        dkvg_ref[...] += dkvg
        dwq, dwuk, dwuv = _dot_tn(cqn, dq_lin), _dot_tn(ckvn, dk_all), _dot_tn(ckvn, dv_all)
        for h in range(HEADS):
            lanes = slice(h * HEAD_PAD, (h + 1) * HEAD_PAD)
            row0 = (h % 2) * MLA_ROWS
            dw_ref[h // 2, row0:row0 + Q_LORA, :] += dwq[:, lanes]
            dw_ref[h // 2, row0 + Q_LORA:row0 + Q_LORA + KV_LORA, :] += dwuk[:, lanes]
            dw_ref[h // 2, row0 + Q_LORA + KV_LORA:row0 + MLA_ROWS, :] += dwuv[:, lanes]

    wide = HEADS * HEAD_PAD
    heads = pl.BlockSpec((HEADS, ts, HEAD_PAD), lambda i: (0, i, 0))
    whole = pl.BlockSpec((N_CHIPS, 2 * MLA_ROWS, HEAD_PAD), lambda i: (0, 0, 0))
    return pl.pallas_call(
        body, name="mla_prep_bwd", grid=(s // ts,),
        in_specs=[_HBM_SPEC, heads, heads, heads, _rows(ts, Q_LORA, 8), _rows(ts, KV_LORA, 18),
                  _rows(ts, HEAD_PAD), _rows(ts, HEAD_PAD), _vec(Q_LORA), _vec(KV_LORA), _vec(wide, 0, Q_LORA),
                  _vec(wide, 0, KV_LORA), _vec(wide, 0, KV_LORA)],
        out_specs=[_rows(ts, 512, 4), whole, _vec(Q_LORA), _vec(KV_LORA)],
        out_shape=[_sds(dz.shape, dz.dtype), _sds((N_CHIPS, 2 * MLA_ROWS, HEAD_PAD)), _sds((1, Q_LORA)), _sds((1, KV_LORA))],
        input_output_aliases={0: 0}, compiler_params=_cp(1))(dz, dq, dk, dv, z, z, cos, sin, qg, kvg, wq, wuk, wuv)


def _chunk_mask(q0, k0, tq, tk):
    rows = q0 + lax.broadcasted_iota(jnp.int32, (tq, tk), 0)
    cols = k0 + lax.broadcasted_iota(jnp.int32, (tq, tk), 1)
    shift = CHUNK.bit_length() - 1
    return lax.shift_right_logical(cols, shift) <= lax.shift_right_logical(rows, shift)


def _attn_fwd(q, k, v, tq):
    s = q.shape[1]
    nq = s // tq
    scale = 1.0 / float(QK_NOPE + QK_ROPE) ** 0.5

    assert nq % 2 == 0, (s, tq)

    def body(q_ref, k_ref, v_ref, o_ref, lse_ref):
        pair, hh = pl.program_id(1), pl.program_id(2)

        def step(qv, q0, kj, carry, masked):
            m, l, acc = carry
            k0 = pl.multiple_of(kj * tq, tq)
            sc = _dot_nt(qv, k_ref[0, pl.ds(k0, tq), :]) * scale
            if masked:
                sc = jnp.where(_chunk_mask(q0, k0, tq, tq), sc, NEG)
            m_new = jnp.maximum(m, jnp.max(sc, axis=-1, keepdims=True))
            alpha = jnp.exp(m - m_new)
            p = jnp.exp(sc - m_new)
            l = alpha * l + jnp.sum(p, axis=-1, keepdims=True)
            acc = alpha * acc + _dot_nn(p, v_ref[0, pl.ds(k0, tq), :])
            return m_new, l, acc

        for half in range(2):
            rows = slice(half * tq, (half + 1) * tq)
            qv = q_ref[0, rows, :]
            q0 = (2 * pair + half) * tq
            two = lambda i, c: step(qv, q0, 2 * i + 1, step(qv, q0, 2 * i, c, False), False)
            init = (jnp.full((tq, 1), NEG, F32), jnp.zeros((tq, 1), F32), jnp.zeros((tq, HEAD_PAD), F32))
            carry = lax.fori_loop(0, pair, two, init)
            if half == 1:
                carry = step(qv, q0, 2 * pair, carry, False)
            m, l, acc = step(qv, q0, 2 * pair + half, carry, True)
            o = acc / l
            lse_ref[0, rows, :] = m + jnp.log(l)

            @pl.when(hh == 0)
            def _():
                o_ref[rows, :] = o

            @pl.when(hh == 1)
            def _():
                o_ref[rows, :] += o

    head = lambda hp, pair, hh: 2 * hp + hh
    return pl.pallas_call(
        body, name="attn_fwd", grid=(HEADS // 2, nq // 2, 2),
        in_specs=[pl.BlockSpec((1, 2 * tq, HEAD_PAD), lambda hp, pair, hh: (head(hp, pair, hh), pair, 0)),
                  pl.BlockSpec((1, s, HEAD_PAD), lambda hp, pair, hh: (head(hp, pair, hh), 0, 0)),
                  pl.BlockSpec((1, s, HEAD_PAD), lambda hp, pair, hh: (head(hp, pair, hh), 0, 0))],
        out_specs=[pl.BlockSpec((2 * tq, HEAD_PAD), lambda hp, pair, hh: (pair, hp)),
                   pl.BlockSpec((1, 2 * tq, 1), lambda hp, pair, hh: (head(hp, pair, hh), pair, 0))],
        out_shape=[_sds((s, HEADS * V_HEAD)), _sds((HEADS, s, 1))],
        compiler_params=_cp(3))(q, k, v)


def _attn_bwd(q, q_t, k, v, do, do_t, o, lse, tq):
    s = q.shape[1]
    nq = s // tq
    per_q = tq // do_t.shape[3]
    scale = 1.0 / float(QK_NOPE + QK_ROPE) ** 0.5

    def body(q_ref, qt_ref, k_ref, v_ref, do_ref, dot_ref, o_ref, lse_ref, dq_ref, dk_ref, dv_ref, dk_t, dv_t):
        hh, kj = pl.program_id(1), pl.program_id(2)

        @pl.when(kj == 0)
        def _():
            dq_ref[...] = jnp.zeros_like(dq_ref)

        kv, vv = k_ref[0], v_ref[0]
        lane = lax.broadcasted_iota(jnp.int32, (tq, HEAD_PAD), 1)
        mine = lax.shift_right_logical(lane, 6) == hh
        dk_t[...] = jnp.zeros_like(dk_t)
        dv_t[...] = jnp.zeros_like(dv_t)

        def step(qi, masked):
            q0 = pl.multiple_of(qi * tq, tq)
            qv = q_ref[0, pl.ds(q0, tq), :]
            dov = do_ref[pl.ds(q0, tq), :]
            delta = jnp.sum(jnp.where(mine, dov * o_ref[pl.ds(q0, tq), :], 0.0), axis=-1, keepdims=True)
            sc = _dot_nt(qv, kv) * scale
            if masked:
                sc = jnp.where(_chunk_mask(q0, kj * tq, tq, tq), sc, NEG)
            p = jnp.exp(sc - lse_ref[0, pl.ds(q0, tq), :])
            ds = (p * (_dot_nt(dov, vv) - delta) * scale).astype(MXU_DTYPE)
            do_tv = jnp.concatenate([dot_ref[0, qi * per_q + r] for r in range(per_q)], axis=1)
            dv_t[...] += _dot_nn(do_tv, p)
            dk_t[...] += _dot_nn(qt_ref[0, qi], ds)
            dq_ref[0, pl.ds(q0, tq), :] += _dot_nn(ds, kv)

        step(kj, True)
        odd = (nq - 1 - kj) % 2

        @pl.when(odd == 1)
        def _():
            step(kj + 1, False)

        def two(i, c):
            step(kj + 1 + odd + 2 * i, False)
            step(kj + 2 + odd + 2 * i, False)
            return c

        lax.fori_loop(0, (nq - 1 - kj) // 2, two, 0)
        dk_ref[0] = dk_t[...].T
        dv_ref[0] = dv_t[...].T

    head = lambda hp, hh, kj: 2 * hp + hh
    full = pl.BlockSpec((1, s, HEAD_PAD), lambda hp, hh, kj: (head(hp, hh, kj), 0, 0))
    blk = pl.BlockSpec((1, tq, HEAD_PAD), lambda hp, hh, kj: (head(hp, hh, kj), kj, 0))
    pair = pl.BlockSpec((s, HEAD_PAD), lambda hp, hh, kj: (0, hp))
    return pl.pallas_call(
        body, name="attn_bwd", grid=(HEADS // 2, 2, nq),
        in_specs=[full, pl.BlockSpec((1,) + q_t.shape[1:], lambda hp, hh, kj: (head(hp, hh, kj), 0, 0, 0)), blk, blk,
                  pair, pl.BlockSpec((1,) + do_t.shape[1:], lambda hp, hh, kj: (hp, 0, 0, 0)), pair,
                  pl.BlockSpec((1, s, 1), lambda hp, hh, kj: (head(hp, hh, kj), 0, 0))],
        out_specs=[full, blk, blk], out_shape=[_sds((HEADS, s, HEAD_PAD))] * 3,
        scratch_shapes=[pltpu.VMEM((HEAD_PAD, tq), F32), pltpu.VMEM((HEAD_PAD, tq), F32)],
        compiler_params=_cp(3))(q, q_t, k, v, do, do_t, o, lse)


def _sc_conv(u, ubuf, w_ref, b_ref, ts):
    return (w_ref[2:3, :] * u + w_ref[1:2, :] * ubuf[pl.ds(SC_HALO - 1, ts), :]
            + w_ref[0:1, :] * ubuf[pl.ds(SC_HALO - 2, ts), :] + b_ref[...])


def _even_gate_fwd(z, o, sc_w, sc_b, ts):
    s = z.shape[0]
    w = SC_WIDTH

    def body(ab_ref, ac_ref, ax_ref, ag_ref, bg_ref, hc_ref, hx_ref, o_ref, w_ref, b_ref, y_ref, ubuf):
        i = pl.program_id(0)
        u = ac_ref[...] * ax_ref[...]
        ubuf[0:SC_HALO, :] = jnp.where(i > 0, hc_ref[...] * hx_ref[...], 0.0)
        ubuf[SC_HALO:, :] = u
        conv = _sc_conv(u, ubuf, w_ref, b_ref, ts)
        y_ref[:, 0:w] = (ab_ref[...] * conv * _silu(ag_ref[...])).astype(y_ref.dtype)
        y_ref[:, w:] = (o_ref[...] * _silu(bg_ref[...])).astype(y_ref.dtype)

    return pl.pallas_call(
        body, name="even_gate_fwd", grid=(s // ts,),
        in_specs=[_rows(ts, w, 0), _rows(ts, w, 1), _rows(ts, w, 2), _rows(ts, w, 3), _rows(ts, w, 5),
                  _prev_halo(ts, SC_HALO, w, 1), _prev_halo(ts, SC_HALO, w, 2), _rows(ts, w),
                  _vec(w, 0, SC_KERNEL), _vec(w)],
        out_specs=_rows(ts, 2 * w), out_shape=_sds((s, 2 * w), MXU_DTYPE),
        scratch_shapes=[pltpu.VMEM((ts + SC_HALO, w), F32)],
        compiler_params=_cp(1))(z, z, z, z, z, z, z, o, sc_w, sc_b)


def _even_gate_bwd(dy, z, o, sc_w, sc_b, ts):
    s = z.shape[0]
    w = SC_WIDTH
    n = s // ts

    def body(dya_ref, dyb_ref, dyan_ref, ab_ref, ac_ref, ax_ref, ag_ref, bg_ref, hc_ref, hx_ref, abn_ref, agn_ref,
             o_ref, w_ref, b_ref, dz_ref, do_ref, dot_ref, dw_ref, db_ref, ubuf, dbuf):
        i = pl.program_id(0)
        ab, ac, ax, ag, bg = ab_ref[...], ac_ref[...], ax_ref[...], ag_ref[...], bg_ref[...]
        dya, dyb = dya_ref[...], dyb_ref[...]
        u = ac * ax
        ubuf[0:SC_HALO, :] = jnp.where(i > 0, hc_ref[...] * hx_ref[...], 0.0)
        ubuf[SC_HALO:, :] = u
        conv = _sc_conv(u, ubuf, w_ref, b_ref, ts)
        sg = _silu(ag)
        dconv = dya * ab * sg
        dbuf[0:ts, :] = dconv
        dbuf[ts:, :] = jnp.where(i < n - 1, dyan_ref[...] * abn_ref[...] * _silu(agn_ref[...]), 0.0)
        du = w_ref[2:3, :] * dconv + w_ref[1:2, :] * dbuf[pl.ds(1, ts), :] + w_ref[0:1, :] * dbuf[pl.ds(2, ts), :]
        dz_ref[:, 0:w] = (dya * conv * sg).astype(dz_ref.dtype)
        dz_ref[:, w:2 * w] = (du * ax).astype(dz_ref.dtype)
        dz_ref[:, 2 * w:3 * w] = (du * ac).astype(dz_ref.dtype)
        dz_ref[:, 3 * w:4 * w] = (dya * ab * conv * _dsilu(ag)).astype(dz_ref.dtype)
        dz_ref[:, 4 * w:5 * w] = jnp.zeros((ts, w), dz_ref.dtype)
        dz_ref[:, 5 * w:] = (dyb * o_ref[...] * _dsilu(bg)).astype(dz_ref.dtype)
        do = dyb * _silu(bg)
        do_ref[...] = do
        for pair in range(HEADS // 2):
            dot_ref[pair, 0] = do[:, pair * HEAD_PAD:(pair + 1) * HEAD_PAD].T.astype(dot_ref.dtype)

        @pl.when(i == 0)
        def _():
            dw_ref[...] = jnp.zeros_like(dw_ref)
            db_ref[...] = jnp.zeros_like(db_ref)

        dw_ref[0:1, :] += jnp.sum(dconv * ubuf[pl.ds(SC_HALO - 2, ts), :], axis=0, keepdims=True)
        dw_ref[1:2, :] += jnp.sum(dconv * ubuf[pl.ds(SC_HALO - 1, ts), :], axis=0, keepdims=True)
        dw_ref[2:3, :] += jnp.sum(dconv * u, axis=0, keepdims=True)
        db_ref[...] += jnp.sum(dconv, axis=0, keepdims=True)

    return pl.pallas_call(
        body, name="even_gate_bwd", grid=(n,),
        in_specs=[_rows(ts, w, 0), _rows(ts, w, 1), _next_halo(ts, SC_HALO, w, 0, s),
                  _rows(ts, w, 0), _rows(ts, w, 1), _rows(ts, w, 2), _rows(ts, w, 3), _rows(ts, w, 5),
                  _prev_halo(ts, SC_HALO, w, 1), _prev_halo(ts, SC_HALO, w, 2),
                  _next_halo(ts, SC_HALO, w, 0, s), _next_halo(ts, SC_HALO, w, 3, s),
                  _rows(ts, w), _vec(w, 0, SC_KERNEL), _vec(w)],
        out_specs=[_rows(ts, EVEN_PAD), _rows(ts, w), pl.BlockSpec((HEADS // 2, 1, HEAD_PAD, ts), lambda i: (0, i, 0, 0)),
                   _vec(w, 0, SC_KERNEL), _vec(w)],
        out_shape=[_sds((s, EVEN_PAD), MXU_DTYPE), _sds((s, w)), _sds((HEADS // 2, n, HEAD_PAD, ts), MXU_DTYPE),
                   _sds((SC_KERNEL, w)), _sds((1, w))],
        scratch_shapes=[pltpu.VMEM((ts + SC_HALO, w), F32), pltpu.VMEM((ts + SC_HALO, w), F32)],
        compiler_params=_cp(1))(dy, dy, dy, z, z, z, z, z, z, z, z, z, o, sc_w, sc_b)


def _ln_act(uc, sg, g, b):
    mu = jnp.mean(uc, axis=-1, keepdims=True)
    var = jnp.mean(jnp.square(uc - mu), axis=-1, keepdims=True)
    return _silu((uc - mu) * lax.rsqrt(var + EPS) * g + b) * _silu(sg)


def _shifted_copies(buf, shifted, rows):
    for b in range(1, SUBLANES):
        shifted[b - 1, 0:rows, :] = buf[pl.ds(b, rows), :]


def _rows_at(buf, shifted, start, n):
    a, b = divmod(start, SUBLANES)
    return buf[pl.ds(SUBLANES * a, n), :] if b == 0 else shifted[b - 1, pl.ds(SUBLANES * a, n), :]


def _odd_fwd(z, conv_w, conv_b, ln_g, ln_b, ts):
    s = z.shape[0]
    d = D_MODEL
    k = CONF_KERNEL

    def body(val_ref, glu_ref, sg_ref, hval_ref, hglu_ref, w_ref, b_ref, g_ref, beta_ref, y_ref, uc_ref, ubuf, ush):
        i = pl.program_id(0)
        ubuf[0:CONF_HALO, :] = jnp.where(i > 0, hval_ref[...] * _sigmoid(hglu_ref[...]), 0.0)
        ubuf[CONF_HALO:, :] = val_ref[...] * _sigmoid(glu_ref[...])
        _shifted_copies(ubuf, ush, ts + CONF_HALO - SUBLANES)
        for r0 in range(0, ts, CONV_ROWS):
            acc = jnp.broadcast_to(b_ref[...], (CONV_ROWS, d))
            for j in range(k):
                acc = acc + w_ref[j:j + 1, :] * _rows_at(ubuf, ush, r0 + CONF_HALO - (k - 1) + j, CONV_ROWS)
            uc_ref[r0:r0 + CONV_ROWS, :] = acc
        y_ref[...] = _ln_act(uc_ref[...], sg_ref[...], g_ref[...], beta_ref[...]).astype(y_ref.dtype)

    return pl.pallas_call(
        body, name="odd_fwd", grid=(s // ts,),
        in_specs=[_rows(ts, d, 0), _rows(ts, d, 1), _rows(ts, d, 2),
                  _prev_halo(ts, CONF_HALO, d, 0), _prev_halo(ts, CONF_HALO, d, 1),
                  _vec(d, 0, k), _vec(d), _vec(d), _vec(d)],
        out_specs=[_rows(ts, d), _rows(ts, d)], out_shape=[_sds((s, d), MXU_DTYPE), _sds((s, d))],
        scratch_shapes=[pltpu.VMEM((ts + CONF_HALO, d), F32),
                        pltpu.VMEM((SUBLANES - 1, ts + CONF_HALO - SUBLANES, d), F32)],
        compiler_params=_cp(1))(z, z, z, z, z, conv_w, conv_b, ln_g, ln_b)


def _odd_bwd(dy, z, uc, conv_w, ln_g, ln_b, ts):
    s = z.shape[0]
    d = D_MODEL
    k = CONF_KERNEL
    n = s // ts

    def body(dy_ref, dyn_ref, val_ref, glu_ref, sg_ref, sgn_ref, uc_ref, ucn_ref,
             w_ref, g_ref, beta_ref, dz_ref, dw_ref, db_ref, dg_ref, dbeta_ref, dbuf, dsh, dw_acc):
        i = pl.program_id(0)
        val, glu = val_ref[...], glu_ref[...]
        sig = _sigmoid(glu)
        u = val * sig
        _, vjp = jax.vjp(_ln_act, uc_ref[...], sg_ref[...], g_ref[...], beta_ref[...])
        duc, dsg, dg, dbeta = vjp(dy_ref[...])
        _, vjp_n = jax.vjp(_ln_act, ucn_ref[...], sgn_ref[...], g_ref[...], beta_ref[...])
        dbuf[0:ts, :] = duc
        dbuf[ts:, :] = jnp.where(i < n - 1, vjp_n(dyn_ref[...])[0], 0.0)
        dz_ref[:, 2 * d:] = dsg.astype(dz_ref.dtype)
        _shifted_copies(dbuf, dsh, ts + CONF_HALO - SUBLANES)

        @pl.when(i == 0)
        def _():
            dw_acc[...] = jnp.zeros_like(dw_acc)
            db_ref[...] = jnp.zeros_like(db_ref)
            dg_ref[...] = jnp.zeros_like(dg_ref)
            dbeta_ref[...] = jnp.zeros_like(dbeta_ref)

        db_ref[...] += jnp.sum(duc, axis=0, keepdims=True)
        dg_ref[...] += dg
        dbeta_ref[...] += dbeta
        for r0 in range(0, ts, CONV_ROWS):
            acc = jnp.zeros((CONV_ROWS, d), F32)
            for j in range(k):
                acc = acc + w_ref[j:j + 1, :] * _rows_at(dbuf, dsh, r0 + (k - 1) - j, CONV_ROWS)
            sig_r = sig[r0:r0 + CONV_ROWS, :]
            dz_ref[r0:r0 + CONV_ROWS, 0:d] = (acc * sig_r).astype(dz_ref.dtype)
            dz_ref[r0:r0 + CONV_ROWS, d:2 * d] = (acc * val[r0:r0 + CONV_ROWS, :] * sig_r * (1.0 - sig_r)).astype(dz_ref.dtype)
        for j in range(k):
            prod = _rows_at(dbuf, dsh, (k - 1) - j, ts) * u
            dw_acc[j] += jnp.sum(prod.reshape(ts // SUBLANES, SUBLANES, d), axis=0)

        @pl.when(i == n - 1)
        def _():
            dw_ref[...] = jnp.sum(dw_acc[...], axis=1)

    return pl.pallas_call(
        body, name="odd_bwd", grid=(n,),
        in_specs=[_rows(ts, d), _next_halo(ts, CONF_HALO, d, 0, s),
                  _rows(ts, d, 0), _rows(ts, d, 1), _rows(ts, d, 2), _next_halo(ts, CONF_HALO, d, 2, s),
                  _rows(ts, d), _next_halo(ts, CONF_HALO, d, 0, s),
                  _vec(d, 0, k), _vec(d), _vec(d)],
        out_specs=[_rows(ts, ODD_IN), _vec(d, 0, k), _vec(d), _vec(d), _vec(d)],
        out_shape=[_sds((s, ODD_IN), MXU_DTYPE), _sds((k, d)), _sds((1, d)), _sds((1, d)), _sds((1, d))],
        scratch_shapes=[pltpu.VMEM((ts + CONF_HALO, d), F32),
                        pltpu.VMEM((SUBLANES - 1, ts + CONF_HALO - SUBLANES, d), F32), pltpu.VMEM((k, SUBLANES, d), F32)],
        compiler_params=_cp(1))(dy, dy, z, z, z, z, uc, uc, conv_w, ln_g, ln_b)


def _local_step(x, target, cos, sin, mod, p, layer_weights, fwd_dep=None, grads_done=None):
    s = x.shape[0]
    tsf, tsb = min(512, s // 2), min(256, s // 2)
    tq = min(512, s // 2)
    row1 = lambda a, i: a[i:i + 1]
    saved = []
    h = _pre_fwd(x, row1(p["pre_norm_g"], 0), row1(mod, 0), tsf, fwd_dep)
    for layer in range(DEPTH):
        i = layer // 2
        mod_l = row1(mod, layer)
        wl = layer_weights(layer, h)
        if layer % 2 == 0:
            z = _mm(h, wl["w_in"], "nn", F32, 512, EVEN_PAD, "even_in_fwd")
            if "late" in wl:
                wl.update(wl.pop("late")(z))
            q, q_t, k, v = _mla_prep_fwd(z, cos, sin, row1(p["even_q_norm_g"], i), row1(p["even_kv_norm_g"], i),
                                    wl["wq"], wl["wq_rot"], wl["wuk"], wl["wuv"], tsf)
            o, lse = _attn_fwd(q, k, v, tq)
            y = _even_gate_fwd(z, o, wl["sc_conv_w"], row1(p["even_sc_conv_b"], i), tsf)
            yo = _mm(y, wl["w_out"], "nn", F32, 1024, 1024, "even_out_fwd")
            saved.append((x, h, z, y, yo, wl, (q, q_t, k, v, o, lse)))
        else:
            z = _mm(h, wl["w_in"], "nn", F32, 512, ODD_IN, "odd_in_fwd")
            y, uc = _odd_fwd(z, wl["conv_w"], wl["conv_b"], wl["ln_g"], wl["ln_b"], tsf)
            yo = _mm(y, wl["w_out"], "nn", F32, 1024, 1024, "odd_out_fwd")
            saved.append((x, h, z, y, yo, wl, uc))
        if layer + 1 < DEPTH:
            x, h = _post_pre_fwd(x, yo, row1(p["post_norm_g"], layer), mod_l, row1(p["pre_norm_g"], layer + 1),
                                 row1(mod, layer + 1), tsf)
        else:
            loss, dx = _post_loss(x, yo, row1(p["post_norm_g"], layer), mod_l, target, tsf)

    g = {n: [None] * (DEPTH if n in ("pre_norm_g", "post_norm_g") else N_PAIRS) for n in (
        "pre_norm_g", "post_norm_g", "even_sc_conv_w", "even_sc_conv_b", "even_q_norm_g", "even_kv_norm_g",
        "odd_conv_w", "odd_conv_b", "odd_ln_g", "odd_ln_b")}
    dmod = [None] * DEPTH
    dep = None
    for layer in reversed(range(DEPTH)):
        i = layer // 2
        mod_l = row1(mod, layer)
        x_in, h, z, y, yo, wl, extra = saved[layer]
        dyo, dgate, g["post_norm_g"][layer] = _post_bwd(dx, yo, row1(p["post_norm_g"], layer), mod_l, tsf, dep)
        bufs = {}
        if layer % 2 == 0:
            q, q_t, k, v, o, lse = extra
            dy = _mm(dyo, wl["w_out"], "nt", F32, 1024, 1024, "even_out_bwd_x")
            bufs["even_w_out"] = _mm_tn_shards(y, dyo, "rows", "even_out_bwd_w")
            dz, do, do_t, g["even_sc_conv_w"][i], g["even_sc_conv_b"][i] = _even_gate_bwd(
                dy, z, o, wl["sc_conv_w"], row1(p["even_sc_conv_b"], i), tsf)
            dq, dk, dv = _attn_bwd(q, q_t, k, v, do, do_t, o, lse, tq)
            dz, bufs["even_mla"], g["even_q_norm_g"][i], g["even_kv_norm_g"][i] = _mla_prep_bwd(
                dz, dq, dk, dv, z, cos, sin, row1(p["even_q_norm_g"], i), row1(p["even_kv_norm_g"], i),
                wl["wq"], wl["wuk"], wl["wuv"], tsf)
            bufs["even_w_in"] = _ein_to_shards(_mm(h, dz, "tn", F32, D_MODEL, 768, "even_in_bwd_w"))
        else:
            uc = extra
            dy = _mm(dyo, wl["w_out"], "nt", F32, 1024, 1024, "odd_out_bwd_x")
            bufs["odd_w_out"] = _mm_tn_shards(y, dyo, "rows", "odd_out_bwd_w")
            dz, g["odd_conv_w"][i], g["odd_conv_b"][i], g["odd_ln_g"][i], g["odd_ln_b"][i] = _odd_bwd(
                dy, z, uc, wl["conv_w"], wl["ln_g"], wl["ln_b"], tsb)
            bufs["odd_w_in"] = _mm_tn_shards(h, dz, "cols", "odd_in_bwd_w")
        dx, dshift, dscale, g["pre_norm_g"][layer] = _pre_bwd(
            dz, wl["w_in"], dx, x_in, row1(p["pre_norm_g"], layer), mod_l, tsf)
        dmod[layer] = jnp.concatenate([dshift, dscale, dgate], axis=-1)
        dep = grads_done(layer, bufs, dx) if grads_done is not None else None
    stack = lambda parts: jnp.stack([a[0] if a.shape[0] == 1 and a.ndim == 2 else a for a in parts])
    small = {n: stack(parts) for n, parts in g.items()}
    small["dmod"] = jnp.concatenate(dmod, axis=0)
    return loss, dx, small


def _uq_to_heads(w):
    w = w.reshape(N_CHIPS, Q_LORA, 2, QK_NOPE + QK_ROPE).transpose(0, 2, 1, 3).reshape(HEADS, Q_LORA, QK_NOPE + QK_ROPE)
    half = QK_ROPE // 2
    rotated = jnp.concatenate([jnp.zeros_like(w[..., :QK_NOPE]), -w[..., QK_NOPE + half:], w[..., QK_NOPE:QK_NOPE + half]],
                              axis=-1)
    pad = ((0, 0), (0, 0), (0, HEAD_PAD - QK_NOPE - QK_ROPE))
    return _side_by_side(jnp.pad(w, pad)), _side_by_side(jnp.pad(rotated, pad))


def _side_by_side(w):
    return w.transpose(1, 0, 2).reshape(w.shape[1], HEADS * HEAD_PAD)


def _ukv_to_heads(w):
    w = w.reshape(N_CHIPS, KV_LORA, 2, QK_NOPE + V_HEAD).transpose(0, 2, 1, 3).reshape(HEADS, KV_LORA, QK_NOPE + V_HEAD)
    wk = jnp.pad(w[..., :QK_NOPE], ((0, 0), (0, 0), (0, HEAD_PAD - QK_NOPE)))
    wv = w[..., QK_NOPE:]
    zero = jnp.zeros_like(wv)
    odd = (jnp.arange(HEADS) % 2 == 1)[:, None, None]
    wv = jnp.concatenate([jnp.where(odd, zero, wv), jnp.where(odd, wv, zero)], axis=-1)
    return _side_by_side(wk), _side_by_side(wv)


def _mla_local(q):
    blocks = q.reshape(2, MLA_ROWS, HEAD_PAD)
    uq = jnp.concatenate([blocks[r, :Q_LORA, :QK_NOPE + QK_ROPE] for r in range(2)], axis=-1)
    ukv = jnp.concatenate(
        [jnp.concatenate([blocks[r, Q_LORA:Q_LORA + KV_LORA, :QK_NOPE],
                          blocks[r, Q_LORA + KV_LORA:, V_HEAD * r:V_HEAD * (r + 1)]], axis=-1) for r in range(2)], axis=-1)
    return uq, ukv


def _place():
    return lax.axis_index("x"), lax.axis_index("y"), lax.axis_index("c")


def _flip(v, bit):
    return 1 - v if bit else v


def _sem(a, k):
    return a * (N_CHIPS - 1) + k - 1


def _remote(src, dst, send_sem, recv_sem, peer):
    return pltpu.make_async_remote_copy(src_ref=src, dst_ref=dst, send_sem=send_sem, recv_sem=recv_sem,
                                        device_id=peer, device_id_type=MESH)


_VMEM_SPEC = pl.BlockSpec(memory_space=pltpu.VMEM)
_HBM_SPEC = pl.BlockSpec(memory_space=pl.ANY)


def _ada_fwd(c8, ada_w, ada_b_sh):
    depth, d, cols = ada_w.shape

    def body(c_ref, w_ref, b_ref, call_ref, mod_ref, s1, r1, s2, r2):
        x, y, c = _place()
        chip = 2 * x + y
        me = 2 * chip + c
        call_ref[me] = c_ref[...]
        sends = []
        for k in range(1, N_DEV):
            peer = (_flip(x, k & 4), _flip(y, k & 2), _flip(c, k & 1))
            cp = _remote(c_ref, call_ref.at[me], s1.at[k - 1], r1.at[k - 1], peer)
            cp.start()
            sends.append(cp)
        for k in range(1, N_DEV):
            src = 4 * _flip(x, k & 4) + 2 * _flip(y, k & 2) + _flip(c, k & 1)
            _remote(c_ref, call_ref.at[src], s1.at[k - 1], r1.at[k - 1], (x, y, c)).wait_recv()
        act = _silu(jnp.concatenate([call_ref[e, 0:1, :] for e in range(N_DEV)], axis=0))
        for l in range(depth):
            mod_ref[chip, l] = _dot_nn(act, w_ref[l]) + b_ref[l:l + 1, :]
        for k in range(1, N_CHIPS):
            peer = (_flip(x, k & 2), _flip(y, k & 1), c)
            cp = _remote(mod_ref.at[chip], mod_ref.at[chip], s2.at[k - 1], r2.at[k - 1], peer)
            cp.start()
            sends.append(cp)
        for k in range(1, N_CHIPS):
            src = 2 * _flip(x, k & 2) + _flip(y, k & 1)
            _remote(mod_ref.at[src], mod_ref.at[src], s2.at[k - 1], r2.at[k - 1], (x, y, c)).wait_recv()
        for cp in sends:
            cp.wait_send()

    return pl.pallas_call(
        body, name="ada_fwd", in_specs=[_VMEM_SPEC] * 3, out_specs=[_VMEM_SPEC] * 2,
        out_shape=[_sds((N_DEV, 8, d)), _sds((N_CHIPS, depth, N_DEV, cols))],
        scratch_shapes=[pltpu.SemaphoreType.DMA((N_DEV - 1,)), pltpu.SemaphoreType.DMA((N_DEV - 1,)),
                        pltpu.SemaphoreType.DMA((N_CHIPS - 1,)), pltpu.SemaphoreType.DMA((N_CHIPS - 1,))],
        compiler_params=pltpu.CompilerParams(vmem_limit_bytes=VMEM_LIMIT_V7X))(c8, ada_w, ada_b_sh)


def _ada_bwd(c_t, dmod_sh):
    depth, n, cols = dmod_sh.shape
    d = c_t.shape[0]
    tr = 256

    def body(c_ref, dm_ref, o_ref):
        act = _silu(c_ref[...])
        acc = act[:, 0:1] * dm_ref[0, 0:1, :]
        for e in range(1, n):
            acc = acc + act[:, e:e + 1] * dm_ref[0, e:e + 1, :]
        o_ref[0] = acc

    return pl.pallas_call(
        body, name="ada_bwd", grid=(depth, d // tr),
        in_specs=[pl.BlockSpec((tr, n), lambda l, i: (i, 0)), pl.BlockSpec((1, n, cols), lambda l, i: (l, 0, 0))],
        out_specs=pl.BlockSpec((1, tr, cols), lambda l, i: (l, i, 0)), out_shape=_sds((depth, d, cols)),
        compiler_params=_cp(2))(c_t, dmod_sh)


def _gathered_shape(shape, how):
    if how == "slot":
        return (N_CHIPS,) + shape
    r, cc = shape
    return (r, N_CHIPS * cc) if how == "cols" else (N_CHIPS * r, cc)


def _gathered_part(ref, shape, how, chip):
    if how == "slot":
        return ref.at[chip]
    if how == "cols":
        return ref.at[:, pl.ds(pl.multiple_of(chip * shape[1], 128), shape[1])]
    return ref.at[pl.ds(pl.multiple_of(chip * shape[0], 8), shape[0]), :]


_SEM_SPEC = pl.BlockSpec(memory_space=pltpu.SEMAPHORE)
_TOKEN = jax.ShapeDtypeStruct((8, 128), F32)
_SPLIT_COPY = pltpu.CompilerParams(has_side_effects=pltpu.SideEffectType.DATAFLOW_SIDE_EFFECTING)


def _in_hbm(a):
    return pltpu.with_memory_space_constraint(a, pltpu.HBM)


def _gather_start(items, gathered, name, after=()):
    n = len(items)

    def body(*refs):
        ins, outs = refs[:n], refs[n:2 * n]
        send_sems, recv_sems = refs[2 * n + len(after)], refs[2 * n + len(after) + 1]
        x, y, c = _place()
        for a in range(n):
            for k in range(1, N_CHIPS):
                part = _gathered_part(outs[a], items[a][0].shape, items[a][1], 2 * x + y)
                _remote(ins[a], part, send_sems.at[_sem(a, k)], recv_sems.at[_sem(a, k)],
                        (_flip(x, k & 2), _flip(y, k & 1), c)).start()
        refs[-1][...] = jnp.zeros(_TOKEN.shape, _TOKEN.dtype)

    arrays = [_in_hbm(a) for a, _ in items] + [_in_hbm(a) for a in gathered]
    res = pl.pallas_call(
        body, name=name, in_specs=[_HBM_SPEC] * (2 * n + len(after)),
        out_specs=[_SEM_SPEC, _SEM_SPEC] + [_HBM_SPEC] * (2 * n) + [_VMEM_SPEC],
        out_shape=[pltpu.SemaphoreType.DMA((n * (N_CHIPS - 1),)), pltpu.SemaphoreType.DMA((n * (N_CHIPS - 1),))]
        + [pltpu.HBM(a.shape, a.dtype) for a in arrays] + [_TOKEN],
        input_output_aliases={a: 2 + a for a in range(2 * n)}, compiler_params=_SPLIT_COPY)(*arrays, *after)
    return res[0], res[1], res[2:2 + n], res[2 + n:2 + 2 * n], res[-1]


def _gather_wait(items, started, after, name):
    n = len(items)
    send_sems, recv_sems, shards, gathered, _ = started

    def body(*refs):
        ins, outs, send_sems, recv_sems = refs[:n], refs[n:2 * n], refs[2 * n], refs[2 * n + 1]
        x, y, c = _place()
        for a in range(n):
            for k in range(1, N_CHIPS):
                part = _gathered_part(outs[a], items[a][0].shape, items[a][1], 2 * _flip(x, k & 2) + _flip(y, k & 1))
                cp = _remote(ins[a], part, send_sems.at[_sem(a, k)], recv_sems.at[_sem(a, k)], (x, y, c))
                cp.wait_send()
                cp.wait_recv()

    res = pl.pallas_call(
        body, name=name, in_specs=[_HBM_SPEC] * (2 * n) + [_SEM_SPEC, _SEM_SPEC] + [_HBM_SPEC] * len(after),
        out_specs=[_HBM_SPEC] * (2 * n), out_shape=[pltpu.HBM(a.shape, a.dtype) for a in (*shards, *gathered)],
        input_output_aliases={a: a for a in range(2 * n)}, compiler_params=_SPLIT_COPY)(
            *shards, *gathered, send_sems, recv_sems, *after)
    return res[n:]


def _rs_start(bufs, name, after=()):
    n = len(bufs)

    def body(*refs):
        srcs, lands = refs[:n], refs[n:2 * n]
        send_sems, recv_sems = refs[2 * n + len(after)], refs[2 * n + len(after) + 1]
        x, y, c = _place()
        for a in range(n):
            for k in range(1, N_CHIPS):
                tx, ty = _flip(x, k & 2), _flip(y, k & 1)
                _remote(srcs[a].at[2 * tx + ty], lands[a].at[k - 1], send_sems.at[_sem(a, k)], recv_sems.at[_sem(a, k)],
                        (tx, ty, c)).start()
        refs[-1][...] = jnp.zeros(_TOKEN.shape, _TOKEN.dtype)

    arrays = [_in_hbm(b) for b in bufs] + [_in_hbm(lax.empty((N_CHIPS - 1,) + b.shape[1:], b.dtype)) for b in bufs]
    res = pl.pallas_call(
        body, name=name, in_specs=[_HBM_SPEC] * (2 * n + len(after)),
        out_specs=[_SEM_SPEC, _SEM_SPEC] + [_HBM_SPEC] * (2 * n) + [_VMEM_SPEC],
        out_shape=[pltpu.SemaphoreType.DMA((n * (N_CHIPS - 1),)), pltpu.SemaphoreType.DMA((n * (N_CHIPS - 1),))]
        + [pltpu.HBM(a.shape, a.dtype) for a in arrays] + [_TOKEN],
        input_output_aliases={a: 2 + a for a in range(2 * n)}, compiler_params=_SPLIT_COPY)(*arrays, *after)
    return res[0], res[1], res[2:2 + n], res[2 + n:2 + 2 * n], res[-1]


def _rs_wait(started, after, name):
    send_sems, recv_sems, bufs, lands, _ = started
    n = len(bufs)

    def body(*refs):
        srcs, lnds, send_sems, recv_sems = refs[:n], refs[n:2 * n], refs[2 * n], refs[2 * n + 1]
        x, y, c = _place()
        for a in range(n):
            for k in range(1, N_CHIPS):
                cp = _remote(srcs[a].at[0], lnds[a].at[k - 1], send_sems.at[_sem(a, k)], recv_sems.at[_sem(a, k)], (x, y, c))
                cp.wait_send()
                cp.wait_recv()

    res = pl.pallas_call(
        body, name=name, in_specs=[_HBM_SPEC] * (2 * n) + [_SEM_SPEC, _SEM_SPEC] + [_HBM_SPEC] * len(after),
        out_specs=[_HBM_SPEC] * (2 * n), out_shape=[pltpu.HBM(a.shape, a.dtype) for a in (*bufs, *lands)],
        input_output_aliases={a: a for a in range(2 * n)}, compiler_params=_SPLIT_COPY)(
            *bufs, *lands, send_sems, recv_sems, *after)
    return res[:n], res[n:]


def _place_own(shard, how, chip_idx):
    r, cc = shard.shape
    block, index = {"slot": ((1, r, cc), lambda i, c: (c[0], 0, 0)), "cols": ((r, cc), lambda i, c: (0, c[0])),
                    "rows": ((r, cc), lambda i, c: (c[0], 0))}[how]

    def body(c_ref, in_ref, o_ref):
        del c_ref
        o_ref[...] = in_ref[...].reshape(o_ref.shape)

    return pl.pallas_call(
        body, name="place_own", out_shape=_sds(_gathered_shape(shard.shape, how), shard.dtype),
        grid_spec=pltpu.PrefetchScalarGridSpec(
            num_scalar_prefetch=1, grid=(1,), in_specs=[pl.BlockSpec((r, cc), lambda i, c: (0, 0))],
            out_specs=pl.BlockSpec(block, index)),
        compiler_params=_cp(1))(chip_idx, shard)


def _gather_sum_all(small):
    r, w = small.shape

    def body(in_ref, all_ref, sum_ref, send_sems, recv_sems):
        x, y, c = _place()
        me = 4 * x + 2 * y + c
        all_ref[me] = in_ref[...]
        sends = []
        for k in range(1, N_DEV):
            peer = (_flip(x, k & 4), _flip(y, k & 2), _flip(c, k & 1))
            cp = _remote(in_ref, all_ref.at[me], send_sems.at[k - 1], recv_sems.at[k - 1], peer)
            cp.start()
            sends.append(cp)
        for k in range(1, N_DEV):
            src = 4 * _flip(x, k & 4) + 2 * _flip(y, k & 2) + _flip(c, k & 1)
            _remote(in_ref, all_ref.at[src], send_sems.at[k - 1], recv_sems.at[k - 1], (x, y, c)).wait_recv()
        acc = all_ref[0]
        for e in range(1, N_DEV):
            acc = acc + all_ref[e]
        sum_ref[...] = acc
        for cp in sends:
            cp.wait_send()

    return pl.pallas_call(
        body, name="gather_sum_all", in_specs=[_VMEM_SPEC], out_specs=[_VMEM_SPEC] * 2,
        out_shape=[_sds((N_DEV, r, w)), _sds((r, w))],
        scratch_shapes=[pltpu.SemaphoreType.DMA((N_DEV - 1,)), pltpu.SemaphoreType.DMA((N_DEV - 1,))],
        compiler_params=pltpu.CompilerParams(vmem_limit_bytes=VMEM_LIMIT_V7X))(small)


def _add_chips(buf, t, chip_idx):
    r, cc = buf.shape[1:]
    tr = min(256, r)

    def body(c_ref, p_ref, t_ref, o_ref):
        del c_ref
        o_ref[...] = p_ref[0] + t_ref[0].astype(F32) + t_ref[1].astype(F32) + t_ref[2].astype(F32)

    return pl.pallas_call(
        body, name="add_chips", out_shape=_sds((r, cc)),
        grid_spec=pltpu.PrefetchScalarGridSpec(
            num_scalar_prefetch=1, grid=(r // tr,),
            in_specs=[pl.BlockSpec((1, tr, cc), lambda i, c: (c[0], i, 0)),
                      pl.BlockSpec((N_CHIPS - 1, tr, cc), lambda i, c: (0, i, 0))],
            out_specs=pl.BlockSpec((tr, cc), lambda i, c: (i, 0))),
        compiler_params=_cp(1))(chip_idx, buf, t)


def _rs_sibling(qs):
    n = len(qs)

    def body(*refs):
        ins, outs = refs[:n], refs[n:2 * n]
        send_sems, recv_sems = refs[2 * n:]
        x, y, c = _place()
        copies = [_remote(ins[a], outs[a], send_sems.at[a], recv_sems.at[a], (x, y, 1 - c)) for a in range(n)]
        for cp in copies:
            cp.start()
        for cp in copies:
            cp.wait()

    return pl.pallas_call(
        body, name="rs_sibling", in_specs=[_HBM_SPEC] * n, out_specs=[_HBM_SPEC] * n,
        out_shape=[_sds(q.shape) for q in qs],
        scratch_shapes=[pltpu.SemaphoreType.DMA((n,)), pltpu.SemaphoreType.DMA((n,))])(*qs)


def _adamw_update(w, g, m, v):
    m = ADAM_B1 * m + (1.0 - ADAM_B1) * g
    v = ADAM_B2 * v + (1.0 - ADAM_B2) * jnp.square(g)
    m_hat = m / (1.0 - ADAM_B1 ** ADAM_STEP)
    v_hat = v / (1.0 - ADAM_B2 ** ADAM_STEP)
    return -ADAM_LR * (m_hat / (jnp.sqrt(v_hat) + ADAM_EPS) + ADAM_WD * w), m, v


def _adamw(w, g_parts, m, v, name):
    shape = w.shape
    cols = shape[-1]
    rows = _size(shape[:-1])
    tr = 512 if rows % 512 == 0 else rows
    spec = pl.BlockSpec((tr, cols), lambda i: (i, 0))
    n = len(g_parts)
    n_out = 4 if n > 1 else 3

    def body(*refs):
        w_ref, m_ref, v_ref = refs[:3]
        d_ref, nm_ref, nv_ref = refs[-3:]
        g = refs[3][...]
        for r in refs[4:3 + n]:
            g = g + r[...]
        if n > 1:
            refs[3 + n][...] = g
        d_ref[...], nm_ref[...], nv_ref[...] = _adamw_update(w_ref[...], g, m_ref[...], v_ref[...])

    outs = pl.pallas_call(
        body, name="adamw_" + name, grid=(rows // tr,), in_specs=[spec] * (3 + n), out_specs=[spec] * n_out,
        out_shape=[_sds((rows, cols))] * n_out, compiler_params=_cp(1))(
            *[a.reshape(rows, cols) for a in (w, m, v, *g_parts)])
    outs = tuple(o.reshape(shape) for o in outs)
    return outs if n > 1 else (g_parts[0],) + outs


def _adamw_layer(w, g_parts, m, v, layer, prev, name):
    _, r, cc = w.shape
    tr = 512 if r % 512 == 0 else r
    spec = pl.BlockSpec((1, tr, cc), lambda i: (layer, i, 0))
    n = len(g_parts)

    def body(*refs):
        w_ref, m_ref, v_ref = refs[:3]
        g_ref, d_ref, nm_ref, nv_ref = refs[-4:]
        g = refs[3][...]
        for q in refs[4:3 + n]:
            g = g + q[...]
        g = g[:, :cc]
        g_ref[0] = g
        d_ref[0], nm_ref[0], nv_ref[0] = _adamw_update(w_ref[0], g, m_ref[0], v_ref[0])

    g_specs = [pl.BlockSpec((tr, q.shape[1]), lambda i: (i, 0)) for q in g_parts]
    passed = () if prev is None else tuple(prev)
    return pl.pallas_call(
        body, name="adamw_" + name, grid=(r // tr,),
        in_specs=[spec] * 3 + g_specs + [_HBM_SPEC] * len(passed), out_specs=[spec] * 4,
        out_shape=[_sds(w.shape)] * 4, input_output_aliases={3 + n + k: k for k in range(len(passed))},
        compiler_params=_cp(1))(w, m, v, *g_parts, *passed)


def _size(shape):
    n = 1
    for s in shape:
        n *= s
    return n


_SMALL = (("dmod", (DEPTH, 3 * D_MODEL)), ("pre_norm_g", (DEPTH, D_MODEL)), ("post_norm_g", (DEPTH, D_MODEL)),
          ("even_sc_conv_w", (2, SC_KERNEL, SC_WIDTH)), ("even_sc_conv_b", (2, SC_WIDTH)),
          ("even_q_norm_g", (2, Q_LORA)), ("even_kv_norm_g", (2, KV_LORA)),
          ("odd_conv_w", (2, CONF_KERNEL, D_MODEL)), ("odd_conv_b", (2, D_MODEL)), ("odd_ln_g", (2, D_MODEL)),
          ("odd_ln_b", (2, D_MODEL)))
SMALL_ROWS = -(-sum(_size(s) for _, s in _SMALL) // (8 * 128)) * 8

_SMALL_W = (("even_sc_conv_w", (2, SC_KERNEL, SC_WIDTH // N_CHIPS)), ("odd_conv_w", (2, CONF_KERNEL, D_MODEL // N_CHIPS)),
            ("odd_conv_b", (2, D_MODEL // N_CHIPS)), ("odd_ln_g", (2, D_MODEL // N_CHIPS)),
            ("odd_ln_b", (2, D_MODEL // N_CHIPS)))
SMALL_W_ROWS = -(-sum(_size(s) for _, s in _SMALL_W) // (8 * 128)) * 8


def _pack_rows(arrays, layout, rows):
    flat = jnp.concatenate([arrays[n].reshape(-1) for n, _ in layout])
    return jnp.pad(flat, (0, rows * 128 - flat.shape[0])).reshape(rows, 128)


def _unpack_small(t):
    flat = t.reshape(-1)
    out, at = {}, 0
    for n, shape in _SMALL:
        out[n] = flat[at:at + _size(shape)].reshape(shape)
        at += _size(shape)
    return out


def _unpack_small_w(t):
    flat = t.reshape(N_CHIPS, -1)
    out, at = {}, 0
    for n, shape in _SMALL_W:
        a = flat[:, at:at + _size(shape)].reshape((N_CHIPS,) + shape)
        out[n] = jnp.moveaxis(a, 0, -2).reshape(shape[:-1] + (N_CHIPS * shape[-1],))
        at += _size(shape)
    return out


def _chip_cols(a, chip):
    n = a.shape[-1] // N_CHIPS
    return lax.dynamic_slice_in_dim(a, chip * n, n, axis=a.ndim - 1)


WEIGHT_NAMES = ("ada_w", "ada_b", "pre_norm_g", "post_norm_g", "even_w_in", "even_sc_conv_w", "even_sc_conv_b",
                "even_q_norm_g", "even_kv_norm_g", "even_w_uq", "even_w_ukv", "even_w_out", "odd_w_in", "odd_conv_w",
                "odd_conv_b", "odd_ln_g", "odd_ln_b", "odd_w_out")
GATHER_HOW = ((("even_w_in", "slot"), ("even_w_uq", "slot"), ("even_w_ukv", "slot"), ("even_w_out", "rows")),
              (("odd_w_in", "cols"), ("odd_w_out", "rows")))


def kernel(x, c, positions, ada_w, ada_b, pre_norm_g, post_norm_g, even_w_in, even_sc_conv_w, even_sc_conv_b, even_q_norm_g, even_kv_norm_g, even_w_uq, even_w_ukv, even_w_out, odd_w_in, odd_conv_w, odd_conv_b, odd_ln_g, odd_ln_b, odd_w_out, loss_target, m_ada_w, m_ada_b, m_pre_norm_g, m_post_norm_g, m_even_w_in, m_even_sc_conv_w, m_even_sc_conv_b, m_even_q_norm_g, m_even_kv_norm_g, m_even_w_uq, m_even_w_ukv, m_even_w_out, m_odd_w_in, m_odd_conv_w, m_odd_conv_b, m_odd_ln_g, m_odd_ln_b, m_odd_w_out, v_ada_w, v_ada_b, v_pre_norm_g, v_post_norm_g, v_even_w_in, v_even_sc_conv_w, v_even_sc_conv_b, v_even_q_norm_g, v_even_kv_norm_g, v_even_w_uq, v_even_w_ukv, v_even_w_out, v_odd_w_in, v_odd_conv_w, v_odd_conv_b, v_odd_ln_g, v_odd_ln_b, v_odd_w_out):
    w = dict(zip(WEIGHT_NAMES, (ada_w, ada_b, pre_norm_g, post_norm_g, even_w_in, even_sc_conv_w, even_sc_conv_b,
                                even_q_norm_g, even_kv_norm_g, even_w_uq, even_w_ukv, even_w_out, odd_w_in, odd_conv_w,
                                odd_conv_b, odd_ln_g, odd_ln_b, odd_w_out)))
    m = dict(zip(WEIGHT_NAMES, (m_ada_w, m_ada_b, m_pre_norm_g, m_post_norm_g, m_even_w_in, m_even_sc_conv_w,
                                m_even_sc_conv_b, m_even_q_norm_g, m_even_kv_norm_g, m_even_w_uq, m_even_w_ukv,
                                m_even_w_out, m_odd_w_in, m_odd_conv_w, m_odd_conv_b, m_odd_ln_g, m_odd_ln_b, m_odd_w_out)))
    v = dict(zip(WEIGHT_NAMES, (v_ada_w, v_ada_b, v_pre_norm_g, v_post_norm_g, v_even_w_in, v_even_sc_conv_w,
                                v_even_sc_conv_b, v_even_q_norm_g, v_even_kv_norm_g, v_even_w_uq, v_even_w_ukv,
                                v_even_w_out, v_odd_w_in, v_odd_conv_w, v_odd_conv_b, v_odd_ln_g, v_odd_ln_b, v_odd_w_out)))
    ix, iy, ic = _place()
    chip = 2 * ix + iy
    me = 2 * chip + ic
    s = x.shape[1]

    c_all, mod_all = _ada_fwd(jnp.broadcast_to(c, (8, D_MODEL)), ada_w, _chip_cols(ada_b, chip))
    mod = lax.dynamic_index_in_dim(mod_all, me, axis=2, keepdims=False)
    mod = mod.transpose(1, 0, 2).reshape(DEPTH, 3 * D_MODEL)

    items = [[(w[n][layer // 2].astype(MXU_DTYPE), how) for n, how in GATHER_HOW[layer % 2]] for layer in range(DEPTH)]
    groups = [items[0][:1], items[0][1:] + [(_pack_rows(w, _SMALL_W, SMALL_W_ROWS), "slot")],
              [item for layer_items in items[1:] for item in layer_items]]
    sent, dep = [], mod_all
    for number, group in enumerate(groups):
        sent.append(_gather_start(group, [_place_own(a, how, chip.reshape(1)) for a, how in group],
                                  "gather_start_%d" % number, [dep]))
        dep = sent[-1][-1]
    arrived = {}

    def group(number, after):
        if number not in arrived:
            arrived[number] = _gather_wait(groups[number], sent[number], after, "gather_wait_%d" % number)
        return arrived[number]

    def even_rest(i, uq, ukv, eout, small_w):
        wuk, wuv = _ukv_to_heads(ukv)
        wq, wq_rot = _uq_to_heads(uq)
        return {"wq": wq, "wq_rot": wq_rot, "wuk": wuk, "wuv": wuv, "w_out": eout, "sc_conv_w": small_w["even_sc_conv_w"][i]}

    def layer_weights(layer, h):
        i = layer // 2
        if layer == 0:
            def late(z):
                uq, ukv, eout, small = group(1, [z])
                return even_rest(i, uq, ukv, eout, _unpack_small_w(small))
            return {"w_in": _ein_from_shards(group(0, [h])[0]), "late": late}
        small_w = _unpack_small_w(group(1, [h])[-1])
        at = sum(len(layer_items) for layer_items in items[1:layer])
        arrays = group(2, [h])[at:at + len(items[layer])]
        if layer % 2 == 0:
            return {"w_in": _ein_from_shards(arrays[0]), **even_rest(i, *arrays[1:], small_w)}
        oin, oout = arrays
        return {"w_in": oin, "w_out": oout, "conv_w": small_w["odd_conv_w"][i], "conv_b": small_w["odd_conv_b"][i:i + 1],
                "ln_g": small_w["odd_ln_g"][i:i + 1], "ln_b": small_w["odd_ln_b"][i:i + 1]}

    in_flight, own, sib, last = {}, {}, {}, {}

    def land(layer, after):
        names, started, kept = in_flight.pop(layer)
        bufs, arrived = _rs_wait(started, after, "rs_wait_%d" % layer)
        sums = [_add_chips(b, t, chip.reshape(1)) for b, t in zip(bufs if kept is None else kept, arrived)]
        for n, mine, theirs in zip(names, sums, _rs_sibling(sums)):
            own[n, layer // 2], sib[n, layer // 2] = mine, theirs

    def grads_done(layer, bufs, dx_in):
        if layer + 1 in in_flight:
            land(layer + 1, [dx_in])
        if layer == 0:
            last.update(bufs)
            return None
        names = sorted(bufs)
        in_flight[layer] = (names, _rs_start([bufs[n] for n in names], "rs_start_%d" % layer), None)
        return in_flight[layer][1][-1]

    p = {"pre_norm_g": pre_norm_g, "post_norm_g": post_norm_g, "even_sc_conv_b": even_sc_conv_b,
         "even_q_norm_g": even_q_norm_g, "even_kv_norm_g": even_kv_norm_g}
    inv_freq = 1.0 / (ROPE_THETA ** (jnp.arange(0, QK_ROPE, 2, dtype=F32) / QK_ROPE))
    inv_freq = jnp.zeros((1, HEAD_PAD), F32).at[0, QK_NOPE:QK_NOPE + QK_ROPE].set(jnp.tile(inv_freq, 2))
    cos, sin = _rope_tables(positions.reshape(s, 1), inv_freq)

    loss, dx, g = _local_step(x[0], loss_target[0], cos, sin, mod, p, layer_weights, dep, grads_done)

    grads, deltas, new_m, new_v = {}, {}, {}, {}

    def update_layers(n, results, pairs):
        for i in pairs:
            results = _adamw_layer(w[n], [own[n, i], sib[n, i]], m[n], v[n], i, results, n)
        return results

    small_all, small_sum = _gather_sum_all(_pack_rows(g, _SMALL, SMALL_ROWS))
    names = sorted(last)
    kept = [last[n] for n in names]
    in_flight[0] = (names, _rs_start([b.astype(jnp.bfloat16) for b in kept], "rs_start_0", [small_sum]), kept)
    tot = _unpack_small(small_sum)
    dmod_all = small_all[:, :DEPTH * 3 * D_MODEL // 128].reshape(N_DEV, DEPTH, 3 * D_MODEL)
    grads["ada_w"] = _ada_bwd(c_all[:, 0, :].T, _chip_cols(dmod_all, chip).transpose(1, 0, 2))
    grads["ada_b"] = tot["dmod"]
    for n in ("pre_norm_g", "post_norm_g", "even_sc_conv_b", "even_q_norm_g", "even_kv_norm_g"):
        grads[n] = tot[n]
    for n in ("even_sc_conv_w", "odd_conv_w", "odd_conv_b", "odd_ln_g", "odd_ln_b"):
        grads[n] = _chip_cols(tot[n], chip)
    for n in list(grads):
        _, deltas[n], new_m[n], new_v[n] = _adamw(w[n], [grads[n]], m[n], v[n], n)

    for n in ("odd_w_in", "odd_w_out"):
        grads[n], deltas[n], new_m[n], new_v[n] = update_layers(n, None, (1, 0))
    partly = {n: update_layers(n, None, (1,)) for n in ("even_w_in", "even_w_out")}
    land(0, [deltas["ada_w"], deltas["odd_w_in"], partly["even_w_in"][1]])
    for n in ("even_w_in", "even_w_out"):
        grads[n], deltas[n], new_m[n], new_v[n] = update_layers(n, partly[n], (0,))
    uq_parts, ukv_parts = zip(*[[jnp.stack(part) for part in zip(*[_mla_local(q["even_mla", i]) for i in range(N_PAIRS)])]
                                for q in (own, sib)])
    for n, parts in (("even_w_uq", uq_parts), ("even_w_ukv", ukv_parts)):
        grads[n], deltas[n], new_m[n], new_v[n] = _adamw(w[n], list(parts), m[n], v[n], n)

    total_loss = lax.psum(loss[0, 0], ("x", "y", "c"))
    return (total_loss, dx[None], *[grads[n] for n in WEIGHT_NAMES], *[deltas[n] for n in WEIGHT_NAMES],
            *[new_m[n] for n in WEIGHT_NAMES], *[new_v[n] for n in WEIGHT_NAMES])
```

```python
import jax
import jax.numpy as jnp
from jax import lax
from jax.experimental import pallas as pl
from jax.experimental.pallas import tpu as pltpu

F32 = jnp.float32
MXU_DTYPE = jnp.bfloat16
MESH = pl.DeviceIdType.MESH
VMEM_LIMIT_V7X = 56 * 2 ** 20

EPS = 1e-6
D_MODEL = 1024
DEPTH = 4
CHUNK = 64
SC_WIDTH = 512
SC_KERNEL = 3
SC_HALO = 8
HEADS = 8
QK_NOPE = 64
QK_ROPE = 32
V_HEAD = 64
HEAD_PAD = 128
Q_LORA = 256
KV_LORA = 128
ROPE_THETA = 10000.0
CONF_KERNEL = 31
CONF_HALO = 32
CONV_ROWS = 64
SUBLANES = 8
EVEN_IN = 2976
EVEN_PAD = 3072
ODD_IN = 3072
N_CHIPS = 4
N_DEV = 8
NEG = -1e30

ADAM_LR = 0.001
ADAM_B1 = 0.9
ADAM_B2 = 0.999
ADAM_EPS = 1e-08
ADAM_WD = 0.01
ADAM_STEP = 10

N_PAIRS = DEPTH // 2
EVEN_SHARD = EVEN_IN // N_CHIPS
EVEN_SHARD_PAD = 768
MLA_ROWS = Q_LORA + 2 * KV_LORA


def _cp(n_grid=0, **kw):
    return pltpu.CompilerParams(dimension_semantics=("arbitrary",) * n_grid,
                                vmem_limit_bytes=VMEM_LIMIT_V7X, **kw)


def _sigmoid(x):
    return 1.0 / (1.0 + jnp.exp(-x))


def _silu(x):
    return x * _sigmoid(x)


def _dsilu(x):
    s = _sigmoid(x)
    return s * (1.0 + x * (1.0 - s))


def _rms(x, g):
    return x * lax.rsqrt(jnp.mean(x * x, axis=-1, keepdims=True) + EPS) * g


def _dot(a, b, dims):
    return lax.dot_general(a.astype(MXU_DTYPE), b.astype(MXU_DTYPE), (dims, ((), ())),
                           preferred_element_type=F32)


def _dot_nn(a, b):
    return _dot(a, b, ((1,), (0,)))


def _dot_nt(a, b):
    return _dot(a, b, ((1,), (1,)))


def _dot_tn(a, b):
    return _dot(a, b, ((0,), (0,)))


def _rows(ts, w, cb=0):
    return pl.BlockSpec((ts, w), lambda i: (i, cb))


def _vec(w, cb=0, r=1):
    return pl.BlockSpec((r, w), lambda i: (0, cb))


def _prev_halo(ts, hr, w, cb):
    return pl.BlockSpec((hr, w), lambda i: (jnp.maximum(i * (ts // hr) - 1, 0), cb))


def _next_halo(ts, hr, w, cb, s):
    return pl.BlockSpec((hr, w), lambda i: (jnp.minimum((i + 1) * (ts // hr), s // hr - 1), cb))


def _sds(shape, dtype=F32):
    return jax.ShapeDtypeStruct(shape, dtype)


def _mm(a, b, mode, out_dtype, tm, tn, name):
    tm = min(tm, a.shape[1] if mode == "tn" else a.shape[0])
    tn = min(tn, b.shape[0] if mode == "nt" else b.shape[1])
    if mode == "nn":
        (m, k), n = a.shape, b.shape[1]
        a_spec = pl.BlockSpec((tm, k), lambda i, j: (i, 0))
        b_spec = pl.BlockSpec((k, tn), lambda i, j: (0, j))
        dot = _dot_nn
    elif mode == "nt":
        (m, k), n = a.shape, b.shape[0]
        a_spec = pl.BlockSpec((tm, k), lambda i, j: (i, 0))
        b_spec = pl.BlockSpec((tn, k), lambda i, j: (j, 0))
        dot = _dot_nt
    else:
        (k, m), n = a.shape, b.shape[1]
        a_spec = pl.BlockSpec((k, tm), lambda i, j: (0, i))
        b_spec = pl.BlockSpec((k, tn), lambda i, j: (0, j))
        dot = _dot_tn
    assert m % tm == 0 and n % tn == 0, (name, m, n, tm, tn)

    def body(a_ref, b_ref, o_ref):
        o_ref[...] = dot(a_ref[...], b_ref[...]).astype(o_ref.dtype)

    return pl.pallas_call(
        body, name=name, grid=(m // tm, n // tn), in_specs=[a_spec, b_spec],
        out_specs=pl.BlockSpec((tm, tn), lambda i, j: (i, j)), out_shape=_sds((m, n), out_dtype),
        compiler_params=_cp(2))(a, b)


def _mm_tn_shards(a, b, by, name):
    k, m = a.shape
    n = b.shape[1]
    if by == "cols":
        tm, tn = m, n // N_CHIPS
        shape, grid = (N_CHIPS, m, tn), (1, N_CHIPS)
        out_spec = pl.BlockSpec((1, tm, tn), lambda i, j: (j, i, 0))
    else:
        tm, tn = m // N_CHIPS, n
        shape, grid = (N_CHIPS, tm, n), (N_CHIPS, 1)
        out_spec = pl.BlockSpec((1, tm, tn), lambda i, j: (i, 0, j))

    def body(a_ref, b_ref, o_ref):
        o_ref[0] = _dot_tn(a_ref[...], b_ref[...])

    return pl.pallas_call(
        body, name=name, grid=grid,
        in_specs=[pl.BlockSpec((k, tm), lambda i, j: (0, i)), pl.BlockSpec((k, tn), lambda i, j: (0, j))],
        out_specs=out_spec, out_shape=_sds(shape), compiler_params=_cp(2))(a, b)


def _even_col(q):
    return q if q < 2432 else (q + 64 if q < 2464 else q + 96)


def _shard_pieces(j):
    lo, hi = EVEN_SHARD * j, EVEN_SHARD * (j + 1)
    cuts = [lo] + [b for b in (2432, 2464) if lo < b < hi] + [hi]
    return [(a - lo, _even_col(a), b - a) for a, b in zip(cuts[:-1], cuts[1:])]


def _ein_from_shards(w):
    _, d, _ = w.shape
    tr = 256

    def body(w_ref, o_ref):
        parts, at = [], 0
        for j in range(N_CHIPS):
            for d0, s0, n in _shard_pieces(j):
                if s0 > at:
                    parts.append(jnp.zeros((tr, s0 - at), F32))
                parts.append(w_ref[j, :, d0:d0 + n].astype(F32))
                at = s0 + n
        o_ref[...] = jnp.concatenate(parts, axis=1).astype(o_ref.dtype)

    return pl.pallas_call(
        body, name="ein_from_shards", grid=(d // tr,),
        in_specs=[pl.BlockSpec((N_CHIPS, tr, EVEN_SHARD), lambda i: (0, i, 0))],
        out_specs=_rows(tr, EVEN_PAD), out_shape=_sds((d, EVEN_PAD), w.dtype), compiler_params=_cp(1))(w)


def _ein_to_shards(dw):
    d = dw.shape[0]
    tr = 256

    def body(dw_ref, o_ref):
        for j in range(N_CHIPS):
            parts = [dw_ref[:, s0:s0 + n] for _, s0, n in _shard_pieces(j)]
            o_ref[j] = jnp.concatenate(parts + [jnp.zeros((tr, EVEN_SHARD_PAD - EVEN_SHARD), F32)], axis=1)

    return pl.pallas_call(
        body, name="ein_to_shards", grid=(d // tr,), in_specs=[_rows(tr, EVEN_PAD)],
        out_specs=pl.BlockSpec((N_CHIPS, tr, EVEN_SHARD_PAD), lambda i: (0, i, 0)),
        out_shape=_sds((N_CHIPS, d, EVEN_SHARD_PAD)), compiler_params=_cp(1))(dw)


def _rope_tables(pos_col, invf):
    s = pos_col.shape[0]
    ts = min(512, s)

    def body(p_ref, f_ref, c_ref, s_ref):
        ang = p_ref[...].astype(F32) * f_ref[...]
        lane = lax.broadcasted_iota(jnp.int32, ang.shape, 1)
        rope = (lane >= QK_NOPE) & (lane < QK_NOPE + QK_ROPE)
        c_ref[...] = jnp.where(lane < QK_NOPE, 1.0, jnp.where(rope, jnp.cos(ang), 0.0))
        s_ref[...] = jnp.where(rope, jnp.sin(ang), 0.0)

    return pl.pallas_call(
        body, name="rope_tables", grid=(s // ts,), in_specs=[_rows(ts, 1), _vec(HEAD_PAD)],
        out_specs=[_rows(ts, HEAD_PAD)] * 2, out_shape=[_sds((s, HEAD_PAD))] * 2,
        compiler_params=_cp(1))(pos_col, invf)


def _after(dep):
    return () if dep is None else (dep,)


def _pre_fwd(x, g, mod_l, ts, dep=None):
    s, d = x.shape

    def body(x_ref, g_ref, sh_ref, sc_ref, *rest):
        h = _rms(x_ref[...], g_ref[...]) * (1.0 + sc_ref[...]) + sh_ref[...]
        rest[-1][...] = h.astype(rest[-1].dtype)

    return pl.pallas_call(
        body, name="pre_fwd", grid=(s // ts,),
        in_specs=[_rows(ts, d), _vec(d), _vec(d, 0), _vec(d, 1)] + [_HBM_SPEC] * len(_after(dep)),
        out_specs=_rows(ts, d), out_shape=_sds((s, d), MXU_DTYPE), compiler_params=_cp(1))(
            x, g, mod_l, mod_l, *_after(dep))


def _pre_bwd(dz, w_in, dx_out, x, g, mod_l, ts):
    s, d = x.shape
    n_in = dz.shape[1]

    def f(xv, gv, sh, sc):
        return _rms(xv, gv) * (1.0 + sc) + sh

    def body(dz_ref, w_ref, dxo_ref, x_ref, g_ref, sh_ref, sc_ref, dx_ref, dsh_ref, dsc_ref, dg_ref):
        @pl.when(pl.program_id(0) == 0)
        def _():
            dsh_ref[...] = jnp.zeros_like(dsh_ref)
            dsc_ref[...] = jnp.zeros_like(dsc_ref)
            dg_ref[...] = jnp.zeros_like(dg_ref)

        _, vjp = jax.vjp(f, x_ref[...], g_ref[...], sh_ref[...], sc_ref[...])
        dx, dg, dsh, dsc = vjp(_dot_nt(dz_ref[...], w_ref[...]))
        dx_ref[...] = dxo_ref[...] + dx
        dsh_ref[...] += dsh
        dsc_ref[...] += dsc
        dg_ref[...] += dg

    return pl.pallas_call(
        body, name="pre_bwd", grid=(s // ts,),
        in_specs=[_rows(ts, n_in), _vec(n_in, 0, d), _rows(ts, d), _rows(ts, d), _vec(d), _vec(d, 0), _vec(d, 1)],
        out_specs=[_rows(ts, d), _vec(d), _vec(d), _vec(d)],
        out_shape=[_sds((s, d)), _sds((1, d)), _sds((1, d)), _sds((1, d))],
        compiler_params=_cp(1))(dz, w_in, dx_out, x, g, mod_l, mod_l)


def _post_pre_fwd(x, yo, g_post, mod_l, g_pre, mod_next, ts):
    s, d = x.shape

    def body(x_ref, yo_ref, gp_ref, gate_ref, g_ref, sh_ref, sc_ref, x_out_ref, h_ref):
        x_new = x_ref[...] + gate_ref[...] * _rms(yo_ref[...], gp_ref[...])
        x_out_ref[...] = x_new
        h_ref[...] = (_rms(x_new, g_ref[...]) * (1.0 + sc_ref[...]) + sh_ref[...]).astype(h_ref.dtype)

    return pl.pallas_call(
        body, name="post_pre_fwd", grid=(s // ts,),
        in_specs=[_rows(ts, d), _rows(ts, d), _vec(d), _vec(d, 2), _vec(d), _vec(d, 0), _vec(d, 1)],
        out_specs=[_rows(ts, d), _rows(ts, d)], out_shape=[_sds((s, d)), _sds((s, d), MXU_DTYPE)],
        compiler_params=_cp(1))(x, yo, g_post, mod_l, g_pre, mod_next, mod_next)


def _post_loss(x, yo, g_post, mod_l, target, ts):
    s, d = x.shape

    def body(x_ref, yo_ref, gp_ref, gate_ref, t_ref, loss_ref, dx_ref):
        err = x_ref[...] + gate_ref[...] * _rms(yo_ref[...], gp_ref[...]) - t_ref[...]
        dx_ref[...] = err * (1.0 / d)

        @pl.when(pl.program_id(0) == 0)
        def _():
            loss_ref[...] = jnp.zeros_like(loss_ref)

        loss_ref[...] += 0.5 * jnp.sum(jnp.sum(err * err, axis=-1, keepdims=True) * (1.0 / d), axis=0, keepdims=True)

    return pl.pallas_call(
        body, name="post_loss", grid=(s // ts,),
        in_specs=[_rows(ts, d), _rows(ts, d), _vec(d), _vec(d, 2), _rows(ts, d)],
        out_specs=[_vec(1), _rows(ts, d)], out_shape=[_sds((1, 1)), _sds((s, d))],
        compiler_params=_cp(1))(x, yo, g_post, mod_l, target)


def _post_bwd(dx_out, yo, g, mod_l, ts, dep=None):
    s, d = yo.shape

    def f(yov, gv, gate):
        return gate * _rms(yov, gv)

    def body(dx_ref, yo_ref, g_ref, gate_ref, *rest):
        dyo_ref, dgate_ref, dg_ref = rest[-3:]
        i = pl.program_id(0)
        _, vjp = jax.vjp(f, yo_ref[...], g_ref[...], gate_ref[...])
        dyo, dg, dgate = vjp(dx_ref[...])
        dyo_ref[...] = dyo.astype(dyo_ref.dtype)

        @pl.when(i == 0)
        def _():
            dgate_ref[...] = jnp.zeros_like(dgate_ref)
            dg_ref[...] = jnp.zeros_like(dg_ref)

        dgate_ref[...] += dgate
        dg_ref[...] += dg

    return pl.pallas_call(
        body, name="post_bwd", grid=(s // ts,),
        in_specs=[_rows(ts, d), _rows(ts, d), _vec(d), _vec(d, 2)] + [_HBM_SPEC] * len(_after(dep)),
        out_specs=[_rows(ts, d), _vec(d), _vec(d)],
        out_shape=[_sds((s, d), MXU_DTYPE), _sds((1, d)), _sds((1, d))],
        compiler_params=_cp(1))(dx_out, yo, g, mod_l, *_after(dep))


def _rope(t, cos, sin):
    lane = lax.broadcasted_iota(jnp.int32, t.shape, 1)
    first = (lane >= QK_NOPE) & (lane < QK_NOPE + QK_ROPE // 2)
    second = (lane >= QK_NOPE + QK_ROPE // 2) & (lane < QK_NOPE + QK_ROPE)
    up = pltpu.roll(t, QK_ROPE // 2, 1)
    down = pltpu.roll(t, HEAD_PAD - QK_ROPE // 2, 1)
    return t * cos + jnp.where(first, -down, jnp.where(second, up, 0.0)) * sin


def _rope_transposed(g, cos, sin):
    lane = lax.broadcasted_iota(jnp.int32, g.shape, 1)
    first = (lane >= QK_NOPE) & (lane < QK_NOPE + QK_ROPE // 2)
    second = (lane >= QK_NOPE + QK_ROPE // 2) & (lane < QK_NOPE + QK_ROPE)
    u = g * sin
    up = pltpu.roll(u, QK_ROPE // 2, 1)
    down = pltpu.roll(u, HEAD_PAD - QK_ROPE // 2, 1)
    return g * cos + jnp.where(first, down, jnp.where(second, -up, 0.0))


def _mla_prep_fwd(z, cos, sin, qg, kvg, wq, wq_rot, wuk, wuv, ts):
    s = z.shape[0]
    wide = HEADS * HEAD_PAD

    def body(cq_ref, ckv_ref, kr_ref, cos_ref, sin_ref, qg_ref, kvg_ref, wq_ref, wqr_ref, wuk_ref, wuv_ref,
             q_ref, qt_ref, k_ref, v_ref):
        cos_v, sin_v = cos_ref[...], sin_ref[...]
        cqn = _rms(cq_ref[...], qg_ref[...])
        ckvn = _rms(ckv_ref[...], kvg_ref[...])
        kr = _rope(kr_ref[...], cos_v, sin_v)
        q_lin, q_rot = _dot_nn(cqn, wq_ref[...]), _dot_nn(cqn, wqr_ref[...])
        k_lin, v_all = _dot_nn(ckvn, wuk_ref[...]), _dot_nn(ckvn, wuv_ref[...])
        for h in range(HEADS):
            lanes = slice(h * HEAD_PAD, (h + 1) * HEAD_PAD)
            qh = q_lin[:, lanes] * cos_v + q_rot[:, lanes] * sin_v
            q_ref[h] = qh.astype(q_ref.dtype)
            qt_ref[h, 0] = qh.T.astype(qt_ref.dtype)
            k_ref[h] = (k_lin[:, lanes] + kr).astype(k_ref.dtype)
            v_ref[h] = v_all[:, lanes].astype(v_ref.dtype)

    out = pl.BlockSpec((HEADS, ts, HEAD_PAD), lambda i: (0, i, 0))
    return pl.pallas_call(
        body, name="mla_prep_fwd", grid=(s // ts,),
        in_specs=[_rows(ts, Q_LORA, 8), _rows(ts, KV_LORA, 18), _rows(ts, HEAD_PAD, 19), _rows(ts, HEAD_PAD), _rows(ts, HEAD_PAD),
                  _vec(Q_LORA), _vec(KV_LORA), _vec(wide, 0, Q_LORA), _vec(wide, 0, Q_LORA), _vec(wide, 0, KV_LORA),
                  _vec(wide, 0, KV_LORA)],
        out_specs=[out, pl.BlockSpec((HEADS, 1, HEAD_PAD, ts), lambda i: (0, i, 0, 0)), out, out],
        out_shape=[_sds((HEADS, s, HEAD_PAD), MXU_DTYPE), _sds((HEADS, s // ts, HEAD_PAD, ts), MXU_DTYPE)]
        + [_sds((HEADS, s, HEAD_PAD), MXU_DTYPE)] * 2,
        compiler_params=_cp(1))(z, z, z, cos, sin, qg, kvg, wq, wq_rot, wuk, wuv)


def _mla_prep_bwd(dz, dq, dk, dv, z, cos, sin, qg, kvg, wq, wuk, wuv, ts):
    s = z.shape[0]

    def fq(cq, g):
        return _rms(cq, g)

    def body(dz_in_ref, dq_ref, dk_ref, dv_ref, cq_ref, ckv_ref, cos_ref, sin_ref, qg_ref, kvg_ref, wq_ref, wuk_ref,
             wuv_ref, dz_ref, dw_ref, dqg_ref, dkvg_ref):
        del dz_in_ref
        cos_v, sin_v = cos_ref[...], sin_ref[...]

        @pl.when(pl.program_id(0) == 0)
        def _():
            dw_ref[...] = jnp.zeros_like(dw_ref)
            dqg_ref[...] = jnp.zeros_like(dqg_ref)
            dkvg_ref[...] = jnp.zeros_like(dkvg_ref)

        cqn, vjp_q = jax.vjp(fq, cq_ref[...], qg_ref[...])
        ckvn, vjp_kv = jax.vjp(fq, ckv_ref[...], kvg_ref[...])
        lane = lax.broadcasted_iota(jnp.int32, (ts, HEAD_PAD), 1)
        rope_lanes = (lane >= QK_NOPE) & (lane < QK_NOPE + QK_ROPE)
        dq_lin = jnp.concatenate([_rope_transposed(dq_ref[h], cos_v, sin_v).astype(MXU_DTYPE) for h in range(HEADS)], axis=1)
        dk_all = jnp.concatenate([dk_ref[h].astype(MXU_DTYPE) for h in range(HEADS)], axis=1)
        dv_all = jnp.concatenate([dv_ref[h].astype(MXU_DTYPE) for h in range(HEADS)], axis=1)
        dkr = jnp.where(rope_lanes, dk_ref[0], 0.0)
        for h in range(1, HEADS):
            dkr = dkr + jnp.where(rope_lanes, dk_ref[h], 0.0)
        dcq, dqg = vjp_q(_dot_nt(dq_lin, wq_ref[...]))
        dckv, dkvg = vjp_kv(_dot_nt(dk_all, wuk_ref[...]) + _dot_nt(dv_all, wuv_ref[...]))
        dz_ref[:, 0:Q_LORA] = dcq.astype(dz_ref.dtype)
        dz_ref[:, Q_LORA:Q_LORA + KV_LORA] = dckv.astype(dz_ref.dtype)
        dz_ref[:, Q_LORA + KV_LORA:] = _rope_transposed(dkr, cos_v, sin_v).astype(dz_ref.dtype)
        dqg_ref[...] += dqg
        dkvg_ref[...] += dkvg
        dwq, dwuk, dwuv = _dot_tn(cqn, dq_lin), _dot_tn(ckvn, dk_all), _dot_tn(ckvn, dv_all)
        for h in range(HEADS):
            lanes = slice(h * HEAD_PAD, (h + 1) * HEAD_PAD)
            row0 = (h % 2) * MLA_ROWS
            dw_ref[h // 2, row0:row0 + Q_LORA, :] += dwq[:, lanes]
            dw_ref[h // 2, row0 + Q_LORA:row0 + Q_LORA + KV_LORA, :] += dwuk[:, lanes]
            dw_ref[h // 2, row0 + Q_LORA + KV_LORA:row0 + MLA_ROWS, :] += dwuv[:, lanes]

    wide = HEADS * HEAD_PAD
    heads = pl.BlockSpec((HEADS, ts, HEAD_PAD), lambda i: (0, i, 0))
    whole = pl.BlockSpec((N_CHIPS, 2 * MLA_ROWS, HEAD_PAD), lambda i: (0, 0, 0))
    return pl.pallas_call(
        body, name="mla_prep_bwd", grid=(s // ts,),
        in_specs=[_HBM_SPEC, heads, heads, heads, _rows(ts, Q_LORA, 8), _rows(ts, KV_LORA, 18),
                  _rows(ts, HEAD_PAD), _rows(ts, HEAD_PAD), _vec(Q_LORA), _vec(KV_LORA), _vec(wide, 0, Q_LORA),
                  _vec(wide, 0, KV_LORA), _vec(wide, 0, KV_LORA)],
        out_specs=[_rows(ts, 512, 4), whole, _vec(Q_LORA), _vec(KV_LORA)],
        out_shape=[_sds(dz.shape, dz.dtype), _sds((N_CHIPS, 2 * MLA_ROWS, HEAD_PAD)), _sds((1, Q_LORA)), _sds((1, KV_LORA))],
        input_output_aliases={0: 0}, compiler_params=_cp(1))(dz, dq, dk, dv, z, z, cos, sin, qg, kvg, wq, wuk, wuv)


def _chunk_mask(q0, k0, tq, tk):
    rows = q0 + lax.broadcasted_iota(jnp.int32, (tq, tk), 0)
    cols = k0 + lax.broadcasted_iota(jnp.int32, (tq, tk), 1)
    shift = CHUNK.bit_length() - 1
    return lax.shift_right_logical(cols, shift) <= lax.shift_right_logical(rows, shift)


def _attn_fwd(q, k, v, tq):
    s = q.shape[1]
    nq = s // tq
    scale = 1.0 / float(QK_NOPE + QK_ROPE) ** 0.5

    assert nq % 2 == 0, (s, tq)

    def body(q_ref, k_ref, v_ref, o_ref, lse_ref):
        pair, hh = pl.program_id(1), pl.program_id(2)

        def step(qv, q0, kj, carry, masked):
            m, l, acc = carry
            k0 = pl.multiple_of(kj * tq, tq)
            sc = _dot_nt(qv, k_ref[0, pl.ds(k0, tq), :]) * scale
            if masked:
                sc = jnp.where(_chunk_mask(q0, k0, tq, tq), sc, NEG)
            m_new = jnp.maximum(m, jnp.max(sc, axis=-1, keepdims=True))
            alpha = jnp.exp(m - m_new)
            p = jnp.exp(sc - m_new)
            l = alpha * l + jnp.sum(p, axis=-1, keepdims=True)
            acc = alpha * acc + _dot_nn(p, v_ref[0, pl.ds(k0, tq), :])
            return m_new, l, acc

        for half in range(2):
            rows = slice(half * tq, (half + 1) * tq)
            qv = q_ref[0, rows, :]
            q0 = (2 * pair + half) * tq
            two = lambda i, c: step(qv, q0, 2 * i + 1, step(qv, q0, 2 * i, c, False), False)
            init = (jnp.full((tq, 1), NEG, F32), jnp.zeros((tq, 1), F32), jnp.zeros((tq, HEAD_PAD), F32))
            carry = lax.fori_loop(0, pair, two, init)
            if half == 1:
                carry = step(qv, q0, 2 * pair, carry, False)
            m, l, acc = step(qv, q0, 2 * pair + half, carry, True)
            o = acc / l
            lse_ref[0, rows, :] = m + jnp.log(l)

            @pl.when(hh == 0)
            def _():
                o_ref[rows, :] = o

            @pl.when(hh == 1)
            def _():
                o_ref[rows, :] += o

    head = lambda hp, pair, hh: 2 * hp + hh
    return pl.pallas_call(
        body, name="attn_fwd", grid=(HEADS // 2, nq // 2, 2),
        in_specs=[pl.BlockSpec((1, 2 * tq, HEAD_PAD), lambda hp, pair, hh: (head(hp, pair, hh), pair, 0)),
                  pl.BlockSpec((1, s, HEAD_PAD), lambda hp, pair, hh: (head(hp, pair, hh), 0, 0)),
                  pl.BlockSpec((1, s, HEAD_PAD), lambda hp, pair, hh: (head(hp, pair, hh), 0, 0))],
        out_specs=[pl.BlockSpec((2 * tq, HEAD_PAD), lambda hp, pair, hh: (pair, hp)),
                   pl.BlockSpec((1, 2 * tq, 1), lambda hp, pair, hh: (head(hp, pair, hh), pair, 0))],
        out_shape=[_sds((s, HEADS * V_HEAD)), _sds((HEADS, s, 1))],
        compiler_params=_cp(3))(q, k, v)


def _attn_bwd(q, q_t, k, v, do, do_t, o, lse, tq):
    s = q.shape[1]
    nq = s // tq
    per_q = tq // do_t.shape[3]
    scale = 1.0 / float(QK_NOPE + QK_ROPE) ** 0.5

    def body(q_ref, qt_ref, k_ref, v_ref, do_ref, dot_ref, o_ref, lse_ref, dq_ref, dk_ref, dv_ref, dk_t, dv_t):
        hh, kj = pl.program_id(1), pl.program_id(2)

        @pl.when(kj == 0)
        def _():
            dq_ref[...] = jnp.zeros_like(dq_ref)

        kv, vv = k_ref[0], v_ref[0]
        lane = lax.broadcasted_iota(jnp.int32, (tq, HEAD_PAD), 1)
        mine = lax.shift_right_logical(lane, 6) == hh
        dk_t[...] = jnp.zeros_like(dk_t)
        dv_t[...] = jnp.zeros_like(dv_t)

        def step(qi, masked):
            q0 = pl.multiple_of(qi * tq, tq)
            qv = q_ref[0, pl.ds(q0, tq), :]
            dov = do_ref[pl.ds(q0, tq), :]
            delta = jnp.sum(jnp.where(mine, dov * o_ref[pl.ds(q0, tq), :], 0.0), axis=-1, keepdims=True)
            sc = _dot_nt(qv, kv) * scale
            if masked:
                sc = jnp.where(_chunk_mask(q0, kj * tq, tq, tq), sc, NEG)
            p = jnp.exp(sc - lse_ref[0, pl.ds(q0, tq), :])
            ds = (p * (_dot_nt(dov, vv) - delta) * scale).astype(MXU_DTYPE)
            do_tv = jnp.concatenate([dot_ref[0, qi * per_q + r] for r in range(per_q)], axis=1)
            dv_t[...] += _dot_nn(do_tv, p)
            dk_t[...] += _dot_nn(qt_ref[0, qi], ds)
            dq_ref[0, pl.ds(q0, tq), :] += _dot_nn(ds, kv)

        step(kj, True)
        odd = (nq - 1 - kj) % 2

        @pl.when(odd == 1)
        def _():
            step(kj + 1, False)

        def two(i, c):
            step(kj + 1 + odd + 2 * i, False)
            step(kj + 2 + odd + 2 * i, False)
            return c

        lax.fori_loop(0, (nq - 1 - kj) // 2, two, 0)
        dk_ref[0] = dk_t[...].T
        dv_ref[0] = dv_t[...].T

    head = lambda hp, hh, kj: 2 * hp + hh
    full = pl.BlockSpec((1, s, HEAD_PAD), lambda hp, hh, kj: (head(hp, hh, kj), 0, 0))
    blk = pl.BlockSpec((1, tq, HEAD_PAD), lambda hp, hh, kj: (head(hp, hh, kj), kj, 0))
    pair = pl.BlockSpec((s, HEAD_PAD), lambda hp, hh, kj: (0, hp))
    return pl.pallas_call(
        body, name="attn_bwd", grid=(HEADS // 2, 2, nq),
        in_specs=[full, pl.BlockSpec((1,) + q_t.shape[1:], lambda hp, hh, kj: (head(hp, hh, kj), 0, 0, 0)), blk, blk,
                  pair, pl.BlockSpec((1,) + do_t.shape[1:], lambda hp, hh, kj: (hp, 0, 0, 0)), pair,
                  pl.BlockSpec((1, s, 1), lambda hp, hh, kj: (head(hp, hh, kj), 0, 0))],
        out_specs=[full, blk, blk], out_shape=[_sds((HEADS, s, HEAD_PAD))] * 3,
        scratch_shapes=[pltpu.VMEM((HEAD_PAD, tq), F32), pltpu.VMEM((HEAD_PAD, tq), F32)],
        compiler_params=_cp(3))(q, q_t, k, v, do, do_t, o, lse)


def _sc_conv(u, ubuf, w_ref, b_ref, ts):
    return (w_ref[2:3, :] * u + w_ref[1:2, :] * ubuf[pl.ds(SC_HALO - 1, ts), :]
            + w_ref[0:1, :] * ubuf[pl.ds(SC_HALO - 2, ts), :] + b_ref[...])


def _even_gate_fwd(z, o, sc_w, sc_b, ts):
    s = z.shape[0]
    w = SC_WIDTH

    def body(ab_ref, ac_ref, ax_ref, ag_ref, bg_ref, hc_ref, hx_ref, o_ref, w_ref, b_ref, y_ref, ubuf):
        i = pl.program_id(0)
        u = ac_ref[...] * ax_ref[...]
        ubuf[0:SC_HALO, :] = jnp.where(i > 0, hc_ref[...] * hx_ref[...], 0.0)
        ubuf[SC_HALO:, :] = u
        conv = _sc_conv(u, ubuf, w_ref, b_ref, ts)
        y_ref[:, 0:w] = (ab_ref[...] * conv * _silu(ag_ref[...])).astype(y_ref.dtype)
        y_ref[:, w:] = (o_ref[...] * _silu(bg_ref[...])).astype(y_ref.dtype)

    return pl.pallas_call(
        body, name="even_gate_fwd", grid=(s // ts,),
        in_specs=[_rows(ts, w, 0), _rows(ts, w, 1), _rows(ts, w, 2), _rows(ts, w, 3), _rows(ts, w, 5),
                  _prev_halo(ts, SC_HALO, w, 1), _prev_halo(ts, SC_HALO, w, 2), _rows(ts, w),
                  _vec(w, 0, SC_KERNEL), _vec(w)],
        out_specs=_rows(ts, 2 * w), out_shape=_sds((s, 2 * w), MXU_DTYPE),
        scratch_shapes=[pltpu.VMEM((ts + SC_HALO, w), F32)],
        compiler_params=_cp(1))(z, z, z, z, z, z, z, o, sc_w, sc_b)


def _even_gate_bwd(dy, z, o, sc_w, sc_b, ts):
    s = z.shape[0]
    w = SC_WIDTH
    n = s // ts

    def body(dya_ref, dyb_ref, dyan_ref, ab_ref, ac_ref, ax_ref, ag_ref, bg_ref, hc_ref, hx_ref, abn_ref, agn_ref,
             o_ref, w_ref, b_ref, dz_ref, do_ref, dot_ref, dw_ref, db_ref, ubuf, dbuf):
        i = pl.program_id(0)
        ab, ac, ax, ag, bg = ab_ref[...], ac_ref[...], ax_ref[...], ag_ref[...], bg_ref[...]
        dya, dyb = dya_ref[...], dyb_ref[...]
        u = ac * ax
        ubuf[0:SC_HALO, :] = jnp.where(i > 0, hc_ref[...] * hx_ref[...], 0.0)
        ubuf[SC_HALO:, :] = u
        conv = _sc_conv(u, ubuf, w_ref, b_ref, ts)
        sg = _silu(ag)
        dconv = dya * ab * sg
        dbuf[0:ts, :] = dconv
        dbuf[ts:, :] = jnp.where(i < n - 1, dyan_ref[...] * abn_ref[...] * _silu(agn_ref[...]), 0.0)
        du = w_ref[2:3, :] * dconv + w_ref[1:2, :] * dbuf[pl.ds(1, ts), :] + w_ref[0:1, :] * dbuf[pl.ds(2, ts), :]
        dz_ref[:, 0:w] = (dya * conv * sg).astype(dz_ref.dtype)
        dz_ref[:, w:2 * w] = (du * ax).astype(dz_ref.dtype)
        dz_ref[:, 2 * w:3 * w] = (du * ac).astype(dz_ref.dtype)
        dz_ref[:, 3 * w:4 * w] = (dya * ab * conv * _dsilu(ag)).astype(dz_ref.dtype)
        dz_ref[:, 4 * w:5 * w] = jnp.zeros((ts, w), dz_ref.dtype)
        dz_ref[:, 5 * w:] = (dyb * o_ref[...] * _dsilu(bg)).astype(dz_ref.dtype)
        do = dyb * _silu(bg)
        do_ref[...] = do
        for pair in range(HEADS // 2):
            dot_ref[pair, 0] = do[:, pair * HEAD_PAD:(pair + 1) * HEAD_PAD].T.astype(dot_ref.dtype)

        @pl.when(i == 0)
        def _():
            dw_ref[...] = jnp.zeros_like(dw_ref)
            db_ref[...] = jnp.zeros_like(db_ref)

        dw_ref[0:1, :] += jnp.sum(dconv * ubuf[pl.ds(SC_HALO - 2, ts), :], axis=0, keepdims=True)
        dw_ref[1:2, :] += jnp.sum(dconv * ubuf[pl.ds(SC_HALO - 1, ts), :], axis=0, keepdims=True)
        dw_ref[2:3, :] += jnp.sum(dconv * u, axis=0, keepdims=True)
        db_ref[...] += jnp.sum(dconv, axis=0, keepdims=True)

    return pl.pallas_call(
        body, name="even_gate_bwd", grid=(n,),
        in_specs=[_rows(ts, w, 0), _rows(ts, w, 1), _next_halo(ts, SC_HALO, w, 0, s),
                  _rows(ts, w, 0), _rows(ts, w, 1), _rows(ts, w, 2), _rows(ts, w, 3), _rows(ts, w, 5),
                  _prev_halo(ts, SC_HALO, w, 1), _prev_halo(ts, SC_HALO, w, 2),
                  _next_halo(ts, SC_HALO, w, 0, s), _next_halo(ts, SC_HALO, w, 3, s),
                  _rows(ts, w), _vec(w, 0, SC_KERNEL), _vec(w)],
        out_specs=[_rows(ts, EVEN_PAD), _rows(ts, w), pl.BlockSpec((HEADS // 2, 1, HEAD_PAD, ts), lambda i: (0, i, 0, 0)),
                   _vec(w, 0, SC_KERNEL), _vec(w)],
        out_shape=[_sds((s, EVEN_PAD), MXU_DTYPE), _sds((s, w)), _sds((HEADS // 2, n, HEAD_PAD, ts), MXU_DTYPE),
                   _sds((SC_KERNEL, w)), _sds((1, w))],
        scratch_shapes=[pltpu.VMEM((ts + SC_HALO, w), F32), pltpu.VMEM((ts + SC_HALO, w), F32)],
        compiler_params=_cp(1))(dy, dy, dy, z, z, z, z, z, z, z, z, z, o, sc_w, sc_b)


def _ln_act(uc, sg, g, b):
    mu = jnp.mean(uc, axis=-1, keepdims=True)
    var = jnp.mean(jnp.square(uc - mu), axis=-1, keepdims=True)
    return _silu((uc - mu) * lax.rsqrt(var + EPS) * g + b) * _silu(sg)


def _shifted_copies(buf, shifted, rows):
    for b in range(1, SUBLANES):
        shifted[b - 1, 0:rows, :] = buf[pl.ds(b, rows), :]


def _rows_at(buf, shifted, start, n):
    a, b = divmod(start, SUBLANES)
    return buf[pl.ds(SUBLANES * a, n), :] if b == 0 else shifted[b - 1, pl.ds(SUBLANES * a, n), :]


def _odd_fwd(z, conv_w, conv_b, ln_g, ln_b, ts):
    s = z.shape[0]
    d = D_MODEL
    k = CONF_KERNEL

    def body(val_ref, glu_ref, sg_ref, hval_ref, hglu_ref, w_ref, b_ref, g_ref, beta_ref, y_ref, uc_ref, ubuf, ush):
        i = pl.program_id(0)
        ubuf[0:CONF_HALO, :] = jnp.where(i > 0, hval_ref[...] * _sigmoid(hglu_ref[...]), 0.0)
        ubuf[CONF_HALO:, :] = val_ref[...] * _sigmoid(glu_ref[...])
        _shifted_copies(ubuf, ush, ts + CONF_HALO - SUBLANES)
        for r0 in range(0, ts, CONV_ROWS):
            acc = jnp.broadcast_to(b_ref[...], (CONV_ROWS, d))
            for j in range(k):
                acc = acc + w_ref[j:j + 1, :] * _rows_at(ubuf, ush, r0 + CONF_HALO - (k - 1) + j, CONV_ROWS)
            uc_ref[r0:r0 + CONV_ROWS, :] = acc
        y_ref[...] = _ln_act(uc_ref[...], sg_ref[...], g_ref[...], beta_ref[...]).astype(y_ref.dtype)

    return pl.pallas_call(
        body, name="odd_fwd", grid=(s // ts,),
        in_specs=[_rows(ts, d, 0), _rows(ts, d, 1), _rows(ts, d, 2),
                  _prev_halo(ts, CONF_HALO, d, 0), _prev_halo(ts, CONF_HALO, d, 1),
                  _vec(d, 0, k), _vec(d), _vec(d), _vec(d)],
        out_specs=[_rows(ts, d), _rows(ts, d)], out_shape=[_sds((s, d), MXU_DTYPE), _sds((s, d))],
        scratch_shapes=[pltpu.VMEM((ts + CONF_HALO, d), F32),
                        pltpu.VMEM((SUBLANES - 1, ts + CONF_HALO - SUBLANES, d), F32)],
        compiler_params=_cp(1))(z, z, z, z, z, conv_w, conv_b, ln_g, ln_b)


def _odd_bwd(dy, z, uc, conv_w, ln_g, ln_b, ts):
    s = z.shape[0]
    d = D_MODEL
    k = CONF_KERNEL
    n = s // ts

    def body(dy_ref, dyn_ref, val_ref, glu_ref, sg_ref, sgn_ref, uc_ref, ucn_ref,
             w_ref, g_ref, beta_ref, dz_ref, dw_ref, db_ref, dg_ref, dbeta_ref, dbuf, dsh, dw_acc):
        i = pl.program_id(0)
        val, glu = val_ref[...], glu_ref[...]
        sig = _sigmoid(glu)
        u = val * sig
        _, vjp = jax.vjp(_ln_act, uc_ref[...], sg_ref[...], g_ref[...], beta_ref[...])
        duc, dsg, dg, dbeta = vjp(dy_ref[...])
        _, vjp_n = jax.vjp(_ln_act, ucn_ref[...], sgn_ref[...], g_ref[...], beta_ref[...])
        dbuf[0:ts, :] = duc
        dbuf[ts:, :] = jnp.where(i < n - 1, vjp_n(dyn_ref[...])[0], 0.0)
        dz_ref[:, 2 * d:] = dsg.astype(dz_ref.dtype)
        _shifted_copies(dbuf, dsh, ts + CONF_HALO - SUBLANES)

        @pl.when(i == 0)
        def _():
            dw_acc[...] = jnp.zeros_like(dw_acc)
            db_ref[...] = jnp.zeros_like(db_ref)
            dg_ref[...] = jnp.zeros_like(dg_ref)
            dbeta_ref[...] = jnp.zeros_like(dbeta_ref)

        db_ref[...] += jnp.sum(duc, axis=0, keepdims=True)
        dg_ref[...] += dg
        dbeta_ref[...] += dbeta
        for r0 in range(0, ts, CONV_ROWS):
            acc = jnp.zeros((CONV_ROWS, d), F32)
            for j in range(k):
                acc = acc + w_ref[j:j + 1, :] * _rows_at(dbuf, dsh, r0 + (k - 1) - j, CONV_ROWS)
            sig_r = sig[r0:r0 + CONV_ROWS, :]
            dz_ref[r0:r0 + CONV_ROWS, 0:d] = (acc * sig_r).astype(dz_ref.dtype)
            dz_ref[r0:r0 + CONV_ROWS, d:2 * d] = (acc * val[r0:r0 + CONV_ROWS, :] * sig_r * (1.0 - sig_r)).astype(dz_ref.dtype)
        for j in range(k):
            prod = _rows_at(dbuf, dsh, (k - 1) - j, ts) * u
            dw_acc[j] += jnp.sum(prod.reshape(ts // SUBLANES, SUBLANES, d), axis=0)

        @pl.when(i == n - 1)
        def _():
            dw_ref[...] = jnp.sum(dw_acc[...], axis=1)

    return pl.pallas_call(
        body, name="odd_bwd", grid=(n,),
        in_specs=[_rows(ts, d), _next_halo(ts, CONF_HALO, d, 0, s),
                  _rows(ts, d, 0), _rows(ts, d, 1), _rows(ts, d, 2), _next_halo(ts, CONF_HALO, d, 2, s),
                  _rows(ts, d), _next_halo(ts, CONF_HALO, d, 0, s),
                  _vec(d, 0, k), _vec(d), _vec(d)],
        out_specs=[_rows(ts, ODD_IN), _vec(d, 0, k), _vec(d), _vec(d), _vec(d)],
        out_shape=[_sds((s, ODD_IN), MXU_DTYPE), _sds((k, d)), _sds((1, d)), _sds((1, d)), _sds((1, d))],
        scratch_shapes=[pltpu.VMEM((ts + CONF_HALO, d), F32),
                        pltpu.VMEM((SUBLANES - 1, ts + CONF_HALO - SUBLANES, d), F32), pltpu.VMEM((k, SUBLANES, d), F32)],
        compiler_params=_cp(1))(dy, dy, z, z, z, z, uc, uc, conv_w, ln_g, ln_b)


def _local_step(x, target, cos, sin, mod, p, layer_weights, fwd_dep=None, grads_done=None):
    s = x.shape[0]
    tsf, tsb = min(512, s // 2), min(256, s // 2)
    tq = min(512, s // 2)
    row1 = lambda a, i: a[i:i + 1]
    saved = []
    h = _pre_fwd(x, row1(p["pre_norm_g"], 0), row1(mod, 0), tsf, fwd_dep)
    for layer in range(DEPTH):
        i = layer // 2
        mod_l = row1(mod, layer)
        wl = layer_weights(layer, h)
        if layer % 2 == 0:
            z = _mm(h, wl["w_in"], "nn", F32, 512, EVEN_PAD, "even_in_fwd")
            if "late" in wl:
                wl.update(wl.pop("late")(z))
            q, q_t, k, v = _mla_prep_fwd(z, cos, sin, row1(p["even_q_norm_g"], i), row1(p["even_kv_norm_g"], i),
                                    wl["wq"], wl["wq_rot"], wl["wuk"], wl["wuv"], tsf)
            o, lse = _attn_fwd(q, k, v, tq)
            y = _even_gate_fwd(z, o, wl["sc_conv_w"], row1(p["even_sc_conv_b"], i), tsf)
            yo = _mm(y, wl["w_out"], "nn", F32, 1024, 1024, "even_out_fwd")
            saved.append((x, h, z, y, yo, wl, (q, q_t, k, v, o, lse)))
        else:
            z = _mm(h, wl["w_in"], "nn", F32, 512, ODD_IN, "odd_in_fwd")
            y, uc = _odd_fwd(z, wl["conv_w"], wl["conv_b"], wl["ln_g"], wl["ln_b"], tsf)
            yo = _mm(y, wl["w_out"], "nn", F32, 1024, 1024, "odd_out_fwd")
            saved.append((x, h, z, y, yo, wl, uc))
        if layer + 1 < DEPTH:
            x, h = _post_pre_fwd(x, yo, row1(p["post_norm_g"], layer), mod_l, row1(p["pre_norm_g"], layer + 1),
                                 row1(mod, layer + 1), tsf)
        else:
            loss, dx = _post_loss(x, yo, row1(p["post_norm_g"], layer), mod_l, target, tsf)

    g = {n: [None] * (DEPTH if n in ("pre_norm_g", "post_norm_g") else N_PAIRS) for n in (
        "pre_norm_g", "post_norm_g", "even_sc_conv_w", "even_sc_conv_b", "even_q_norm_g", "even_kv_norm_g",
        "odd_conv_w", "odd_conv_b", "odd_ln_g", "odd_ln_b")}
    dmod = [None] * DEPTH
    dep = None
    for layer in reversed(range(DEPTH)):
        i = layer // 2
        mod_l = row1(mod, layer)
        x_in, h, z, y, yo, wl, extra = saved[layer]
        dyo, dgate, g["post_norm_g"][layer] = _post_bwd(dx, yo, row1(p["post_norm_g"], layer), mod_l, tsf, dep)
        bufs = {}
        if layer % 2 == 0:
            q, q_t, k, v, o, lse = extra
            dy = _mm(dyo, wl["w_out"], "nt", F32, 1024, 1024, "even_out_bwd_x")
            bufs["even_w_out"] = _mm_tn_shards(y, dyo, "rows", "even_out_bwd_w")
            dz, do, do_t, g["even_sc_conv_w"][i], g["even_sc_conv_b"][i] = _even_gate_bwd(
                dy, z, o, wl["sc_conv_w"], row1(p["even_sc_conv_b"], i), tsf)
            dq, dk, dv = _attn_bwd(q, q_t, k, v, do, do_t, o, lse, tq)
            dz, bufs["even_mla"], g["even_q_norm_g"][i], g["even_kv_norm_g"][i] = _mla_prep_bwd(
                dz, dq, dk, dv, z, cos, sin, row1(p["even_q_norm_g"], i), row1(p["even_kv_norm_g"], i),
                wl["wq"], wl["wuk"], wl["wuv"], tsf)
            bufs["even_w_in"] = _ein_to_shards(_mm(h, dz, "tn", F32, D_MODEL, 768, "even_in_bwd_w"))
        else:
            uc = extra
            dy = _mm(dyo, wl["w_out"], "nt", F32, 1024, 1024, "odd_out_bwd_x")
            bufs["odd_w_out"] = _mm_tn_shards(y, dyo, "rows", "odd_out_bwd_w")
            dz, g["odd_conv_w"][i], g["odd_conv_b"][i], g["odd_ln_g"][i], g["odd_ln_b"][i] = _odd_bwd(
                dy, z, uc, wl["conv_w"], wl["ln_g"], wl["ln_b"], tsb)
            bufs["odd_w_in"] = _mm_tn_shards(h, dz, "cols", "odd_in_bwd_w")
        dx, dshift, dscale, g["pre_norm_g"][layer] = _pre_bwd(
            dz, wl["w_in"], dx, x_in, row1(p["pre_norm_g"], layer), mod_l, tsf)
        dmod[layer] = jnp.concatenate([dshift, dscale, dgate], axis=-1)
        dep = grads_done(layer, bufs, dx) if grads_done is not None else None
    stack = lambda parts: jnp.stack([a[0] if a.shape[0] == 1 and a.ndim == 2 else a for a in parts])
    small = {n: stack(parts) for n, parts in g.items()}
    small["dmod"] = jnp.concatenate(dmod, axis=0)
    return loss, dx, small


def _uq_to_heads(w):
    w = w.reshape(N_CHIPS, Q_LORA, 2, QK_NOPE + QK_ROPE).transpose(0, 2, 1, 3).reshape(HEADS, Q_LORA, QK_NOPE + QK_ROPE)
    half = QK_ROPE // 2
    rotated = jnp.concatenate([jnp.zeros_like(w[..., :QK_NOPE]), -w[..., QK_NOPE + half:], w[..., QK_NOPE:QK_NOPE + half]],
                              axis=-1)
    pad = ((0, 0), (0, 0), (0, HEAD_PAD - QK_NOPE - QK_ROPE))
    return _side_by_side(jnp.pad(w, pad)), _side_by_side(jnp.pad(rotated, pad))


def _side_by_side(w):
    return w.transpose(1, 0, 2).reshape(w.shape[1], HEADS * HEAD_PAD)


def _ukv_to_heads(w):
    w = w.reshape(N_CHIPS, KV_LORA, 2, QK_NOPE + V_HEAD).transpose(0, 2, 1, 3).reshape(HEADS, KV_LORA, QK_NOPE + V_HEAD)
    wk = jnp.pad(w[..., :QK_NOPE], ((0, 0), (0, 0), (0, HEAD_PAD - QK_NOPE)))
    wv = w[..., QK_NOPE:]
    zero = jnp.zeros_like(wv)
    odd = (jnp.arange(HEADS) % 2 == 1)[:, None, None]
    wv = jnp.concatenate([jnp.where(odd, zero, wv), jnp.where(odd, wv, zero)], axis=-1)
    return _side_by_side(wk), _side_by_side(wv)


def _mla_local(q):
    blocks = q.reshape(2, MLA_ROWS, HEAD_PAD)
    uq = jnp.concatenate([blocks[r, :Q_LORA, :QK_NOPE + QK_ROPE] for r in range(2)], axis=-1)
    ukv = jnp.concatenate(
        [jnp.concatenate([blocks[r, Q_LORA:Q_LORA + KV_LORA, :QK_NOPE],
                          blocks[r, Q_LORA + KV_LORA:, V_HEAD * r:V_HEAD * (r + 1)]], axis=-1) for r in range(2)], axis=-1)
    return uq, ukv


def _place():
    return lax.axis_index("x"), lax.axis_index("y"), lax.axis_index("c")


def _flip(v, bit):
    return 1 - v if bit else v


def _sem(a, k):
    return a * (N_CHIPS - 1) + k - 1


def _remote(src, dst, send_sem, recv_sem, peer):
    return pltpu.make_async_remote_copy(src_ref=src, dst_ref=dst, send_sem=send_sem, recv_sem=recv_sem,
                                        device_id=peer, device_id_type=MESH)


_VMEM_SPEC = pl.BlockSpec(memory_space=pltpu.VMEM)
_HBM_SPEC = pl.BlockSpec(memory_space=pl.ANY)


def _ada_fwd(c8, ada_w, ada_b_sh):
    depth, d, cols = ada_w.shape

    def body(c_ref, w_ref, b_ref, call_ref, mod_ref, s1, r1, s2, r2):
        x, y, c = _place()
        chip = 2 * x + y
        me = 2 * chip + c
        call_ref[me] = c_ref[...]
        sends = []
        for k in range(1, N_DEV):
            peer = (_flip(x, k & 4), _flip(y, k & 2), _flip(c, k & 1))
            cp = _remote(c_ref, call_ref.at[me], s1.at[k - 1], r1.at[k - 1], peer)
            cp.start()
            sends.append(cp)
        for k in range(1, N_DEV):
            src = 4 * _flip(x, k & 4) + 2 * _flip(y, k & 2) + _flip(c, k & 1)
            _remote(c_ref, call_ref.at[src], s1.at[k - 1], r1.at[k - 1], (x, y, c)).wait_recv()
        act = _silu(jnp.concatenate([call_ref[e, 0:1, :] for e in range(N_DEV)], axis=0))
        for l in range(depth):
            mod_ref[chip, l] = _dot_nn(act, w_ref[l]) + b_ref[l:l + 1, :]
        for k in range(1, N_CHIPS):
            peer = (_flip(x, k & 2), _flip(y, k & 1), c)
            cp = _remote(mod_ref.at[chip], mod_ref.at[chip], s2.at[k - 1], r2.at[k - 1], peer)
            cp.start()
            sends.append(cp)
        for k in range(1, N_CHIPS):
            src = 2 * _flip(x, k & 2) + _flip(y, k & 1)
            _remote(mod_ref.at[src], mod_ref.at[src], s2.at[k - 1], r2.at[k - 1], (x, y, c)).wait_recv()
        for cp in sends:
            cp.wait_send()

    return pl.pallas_call(
        body, name="ada_fwd", in_specs=[_VMEM_SPEC] * 3, out_specs=[_VMEM_SPEC] * 2,
        out_shape=[_sds((N_DEV, 8, d)), _sds((N_CHIPS, depth, N_DEV, cols))],
        scratch_shapes=[pltpu.SemaphoreType.DMA((N_DEV - 1,)), pltpu.SemaphoreType.DMA((N_DEV - 1,)),
                        pltpu.SemaphoreType.DMA((N_CHIPS - 1,)), pltpu.SemaphoreType.DMA((N_CHIPS - 1,))],
        compiler_params=pltpu.CompilerParams(vmem_limit_bytes=VMEM_LIMIT_V7X))(c8, ada_w, ada_b_sh)


def _ada_bwd(c_t, dmod_sh):
    depth, n, cols = dmod_sh.shape
    d = c_t.shape[0]
    tr = 256

    def body(c_ref, dm_ref, o_ref):
        act = _silu(c_ref[...])
        acc = act[:, 0:1] * dm_ref[0, 0:1, :]
        for e in range(1, n):
            acc = acc + act[:, e:e + 1] * dm_ref[0, e:e + 1, :]
        o_ref[0] = acc

    return pl.pallas_call(
        body, name="ada_bwd", grid=(depth, d // tr),
        in_specs=[pl.BlockSpec((tr, n), lambda l, i: (i, 0)), pl.BlockSpec((1, n, cols), lambda l, i: (l, 0, 0))],
        out_specs=pl.BlockSpec((1, tr, cols), lambda l, i: (l, i, 0)), out_shape=_sds((depth, d, cols)),
        compiler_params=_cp(2))(c_t, dmod_sh)


def _gathered_shape(shape, how):
    if how == "slot":
        return (N_CHIPS,) + shape
    r, cc = shape
    return (r, N_CHIPS * cc) if how == "cols" else (N_CHIPS * r, cc)


def _gathered_part(ref, shape, how, chip):
    if how == "slot":
        return ref.at[chip]
    if how == "cols":
        return ref.at[:, pl.ds(pl.multiple_of(chip * shape[1], 128), shape[1])]
    return ref.at[pl.ds(pl.multiple_of(chip * shape[0], 8), shape[0]), :]


_SEM_SPEC = pl.BlockSpec(memory_space=pltpu.SEMAPHORE)
_TOKEN = jax.ShapeDtypeStruct((8, 128), F32)
_SPLIT_COPY = pltpu.CompilerParams(has_side_effects=pltpu.SideEffectType.DATAFLOW_SIDE_EFFECTING)


def _in_hbm(a):
    return pltpu.with_memory_space_constraint(a, pltpu.HBM)


def _gather_start(items, gathered, name, after=()):
    n = len(items)

    def body(*refs):
        ins, outs = refs[:n], refs[n:2 * n]
        send_sems, recv_sems = refs[2 * n + len(after)], refs[2 * n + len(after) + 1]
        x, y, c = _place()
        for a in range(n):
            for k in range(1, N_CHIPS):
                part = _gathered_part(outs[a], items[a][0].shape, items[a][1], 2 * x + y)
                _remote(ins[a], part, send_sems.at[_sem(a, k)], recv_sems.at[_sem(a, k)],
                        (_flip(x, k & 2), _flip(y, k & 1), c)).start()
        refs[-1][...] = jnp.zeros(_TOKEN.shape, _TOKEN.dtype)

    arrays = [_in_hbm(a) for a, _ in items] + [_in_hbm(a) for a in gathered]
    res = pl.pallas_call(
        body, name=name, in_specs=[_HBM_SPEC] * (2 * n + len(after)),
        out_specs=[_SEM_SPEC, _SEM_SPEC] + [_HBM_SPEC] * (2 * n) + [_VMEM_SPEC],
        out_shape=[pltpu.SemaphoreType.DMA((n * (N_CHIPS - 1),)), pltpu.SemaphoreType.DMA((n * (N_CHIPS - 1),))]
        + [pltpu.HBM(a.shape, a.dtype) for a in arrays] + [_TOKEN],
        input_output_aliases={a: 2 + a for a in range(2 * n)}, compiler_params=_SPLIT_COPY)(*arrays, *after)
    return res[0], res[1], res[2:2 + n], res[2 + n:2 + 2 * n], res[-1]


def _gather_wait(items, started, after, name):
    n = len(items)
    send_sems, recv_sems, shards, gathered, _ = started

    def body(*refs):
        ins, outs, send_sems, recv_sems = refs[:n], refs[n:2 * n], refs[2 * n], refs[2 * n + 1]
        x, y, c = _place()
        for a in range(n):
            for k in range(1, N_CHIPS):
                part = _gathered_part(outs[a], items[a][0].shape, items[a][1], 2 * _flip(x, k & 2) + _flip(y, k & 1))
                cp = _remote(ins[a], part, send_sems.at[_sem(a, k)], recv_sems.at[_sem(a, k)], (x, y, c))
                cp.wait_send()
                cp.wait_recv()

    res = pl.pallas_call(
        body, name=name, in_specs=[_HBM_SPEC] * (2 * n) + [_SEM_SPEC, _SEM_SPEC] + [_HBM_SPEC] * len(after),
        out_specs=[_HBM_SPEC] * (2 * n), out_shape=[pltpu.HBM(a.shape, a.dtype) for a in (*shards, *gathered)],
        input_output_aliases={a: a for a in range(2 * n)}, compiler_params=_SPLIT_COPY)(
            *shards, *gathered, send_sems, recv_sems, *after)
    return res[n:]


def _rs_start(bufs, name, after=()):
    n = len(bufs)

    def body(*refs):
        srcs, lands = refs[:n], refs[n:2 * n]
        send_sems, recv_sems = refs[2 * n + len(after)], refs[2 * n + len(after) + 1]
        x, y, c = _place()
        for a in range(n):
            for k in range(1, N_CHIPS):
                tx, ty = _flip(x, k & 2), _flip(y, k & 1)
                _remote(srcs[a].at[2 * tx + ty], lands[a].at[k - 1], send_sems.at[_sem(a, k)], recv_sems.at[_sem(a, k)],
                        (tx, ty, c)).start()
        refs[-1][...] = jnp.zeros(_TOKEN.shape, _TOKEN.dtype)

    arrays = [_in_hbm(b) for b in bufs] + [_in_hbm(lax.empty((N_CHIPS - 1,) + b.shape[1:], b.dtype)) for b in bufs]
    res = pl.pallas_call(
        body, name=name, in_specs=[_HBM_SPEC] * (2 * n + len(after)),
        out_specs=[_SEM_SPEC, _SEM_SPEC] + [_HBM_SPEC] * (2 * n) + [_VMEM_SPEC],
        out_shape=[pltpu.SemaphoreType.DMA((n * (N_CHIPS - 1),)), pltpu.SemaphoreType.DMA((n * (N_CHIPS - 1),))]
        + [pltpu.HBM(a.shape, a.dtype) for a in arrays] + [_TOKEN],
        input_output_aliases={a: 2 + a for a in range(2 * n)}, compiler_params=_SPLIT_COPY)(*arrays, *after)
    return res[0], res[1], res[2:2 + n], res[2 + n:2 + 2 * n], res[-1]


def _rs_wait(started, after, name):
    send_sems, recv_sems, bufs, lands, _ = started
    n = len(bufs)

    def body(*refs):
        srcs, lnds, send_sems, recv_sems = refs[:n], refs[n:2 * n], refs[2 * n], refs[2 * n + 1]
        x, y, c = _place()
        for a in range(n):
            for k in range(1, N_CHIPS):
                cp = _remote(srcs[a].at[0], lnds[a].at[k - 1], send_sems.at[_sem(a, k)], recv_sems.at[_sem(a, k)], (x, y, c))
                cp.wait_send()
                cp.wait_recv()

    res = pl.pallas_call(
        body, name=name, in_specs=[_HBM_SPEC] * (2 * n) + [_SEM_SPEC, _SEM_SPEC] + [_HBM_SPEC] * len(after),
        out_specs=[_HBM_SPEC] * (2 * n), out_shape=[pltpu.HBM(a.shape, a.dtype) for a in (*bufs, *lands)],
        input_output_aliases={a: a for a in range(2 * n)}, compiler_params=_SPLIT_COPY)(
            *bufs, *lands, send_sems, recv_sems, *after)
    return res[:n], res[n:]


def _place_own(shard, how, chip_idx):
    r, cc = shard.shape
    block, index = {"slot": ((1, r, cc), lambda i, c: (c[0], 0, 0)), "cols": ((r, cc), lambda i, c: (0, c[0])),
                    "rows": ((r, cc), lambda i, c: (c[0], 0))}[how]

    def body(c_ref, in_ref, o_ref):
        del c_ref
        o_ref[...] = in_ref[...].reshape(o_ref.shape)

    return pl.pallas_call(
        body, name="place_own", out_shape=_sds(_gathered_shape(shard.shape, how), shard.dtype),
        grid_spec=pltpu.PrefetchScalarGridSpec(
            num_scalar_prefetch=1, grid=(1,), in_specs=[pl.BlockSpec((r, cc), lambda i, c: (0, 0))],
            out_specs=pl.BlockSpec(block, index)),
        compiler_params=_cp(1))(chip_idx, shard)


def _gather_sum_all(small):
    r, w = small.shape

    def body(in_ref, all_ref, sum_ref, send_sems, recv_sems):
        x, y, c = _place()
        me = 4 * x + 2 * y + c
        all_ref[me] = in_ref[...]
        sends = []
        for k in range(1, N_DEV):
            peer = (_flip(x, k & 4), _flip(y, k & 2), _flip(c, k & 1))
            cp = _remote(in_ref, all_ref.at[me], send_sems.at[k - 1], recv_sems.at[k - 1], peer)
            cp.start()
            sends.append(cp)
        for k in range(1, N_DEV):
            src = 4 * _flip(x, k & 4) + 2 * _flip(y, k & 2) + _flip(c, k & 1)
            _remote(in_ref, all_ref.at[src], send_sems.at[k - 1], recv_sems.at[k - 1], (x, y, c)).wait_recv()
        acc = all_ref[0]
        for e in range(1, N_DEV):
            acc = acc + all_ref[e]
        sum_ref[...] = acc
        for cp in sends:
            cp.wait_send()

    return pl.pallas_call(
        body, name="gather_sum_all", in_specs=[_VMEM_SPEC], out_specs=[_VMEM_SPEC] * 2,
        out_shape=[_sds((N_DEV, r, w)), _sds((r, w))],
        scratch_shapes=[pltpu.SemaphoreType.DMA((N_DEV - 1,)), pltpu.SemaphoreType.DMA((N_DEV - 1,))],
        compiler_params=pltpu.CompilerParams(vmem_limit_bytes=VMEM_LIMIT_V7X))(small)


def _add_chips(buf, t, chip_idx):
    r, cc = buf.shape[1:]
    tr = min(512, r)

    def body(c_ref, p_ref, t_ref, o_ref):
        del c_ref
        o_ref[...] = p_ref[0] + t_ref[0].astype(F32) + t_ref[1].astype(F32) + t_ref[2].astype(F32)

    return pl.pallas_call(
        body, name="add_chips", out_shape=_sds((r, cc)),
        grid_spec=pltpu.PrefetchScalarGridSpec(
            num_scalar_prefetch=1, grid=(r // tr,),
            in_specs=[pl.BlockSpec((1, tr, cc), lambda i, c: (c[0], i, 0)),
                      pl.BlockSpec((N_CHIPS - 1, tr, cc), lambda i, c: (0, i, 0))],
            out_specs=pl.BlockSpec((tr, cc), lambda i, c: (i, 0))),
        compiler_params=_cp(1))(chip_idx, buf, t)


def _rs_sibling(qs):
    n = len(qs)

    def body(*refs):
        ins, outs = refs[:n], refs[n:2 * n]
        send_sems, recv_sems = refs[2 * n:]
        x, y, c = _place()
        copies = [_remote(ins[a], outs[a], send_sems.at[a], recv_sems.at[a], (x, y, 1 - c)) for a in range(n)]
        for cp in copies:
            cp.start()
        for cp in copies:
            cp.wait()

    return pl.pallas_call(
        body, name="rs_sibling", in_specs=[_HBM_SPEC] * n, out_specs=[_HBM_SPEC] * n,
        out_shape=[_sds(q.shape) for q in qs],
        scratch_shapes=[pltpu.SemaphoreType.DMA((n,)), pltpu.SemaphoreType.DMA((n,))])(*qs)


def _adamw_update(w, g, m, v):
    m = ADAM_B1 * m + (1.0 - ADAM_B1) * g
    v = ADAM_B2 * v + (1.0 - ADAM_B2) * jnp.square(g)
    m_hat = m / (1.0 - ADAM_B1 ** ADAM_STEP)
    v_hat = v / (1.0 - ADAM_B2 ** ADAM_STEP)
    return -ADAM_LR * (m_hat / (jnp.sqrt(v_hat) + ADAM_EPS) + ADAM_WD * w), m, v


def _adamw(w, g_parts, m, v, name):
    shape = w.shape
    cols = shape[-1]
    rows = _size(shape[:-1])
    tr = 512 if rows % 512 == 0 else rows
    spec = pl.BlockSpec((tr, cols), lambda i: (i, 0))
    n = len(g_parts)
    n_out = 4 if n > 1 else 3

    def body(*refs):
        w_ref, m_ref, v_ref = refs[:3]
        d_ref, nm_ref, nv_ref = refs[-3:]
        g = refs[3][...]
        for r in refs[4:3 + n]:
            g = g + r[...]
        if n > 1:
            refs[3 + n][...] = g
        d_ref[...], nm_ref[...], nv_ref[...] = _adamw_update(w_ref[...], g, m_ref[...], v_ref[...])

    outs = pl.pallas_call(
        body, name="adamw_" + name, grid=(rows // tr,), in_specs=[spec] * (3 + n), out_specs=[spec] * n_out,
        out_shape=[_sds((rows, cols))] * n_out, compiler_params=_cp(1))(
            *[a.reshape(rows, cols) for a in (w, m, v, *g_parts)])
    outs = tuple(o.reshape(shape) for o in outs)
    return outs if n > 1 else (g_parts[0],) + outs


def _adamw_layer(w, g_parts, m, v, layer, prev, name):
    _, r, cc = w.shape
    tr = 512 if r % 512 == 0 else r
    spec = pl.BlockSpec((1, tr, cc), lambda i: (layer, i, 0))
    n = len(g_parts)

    def body(*refs):
        w_ref, m_ref, v_ref = refs[:3]
        g_ref, d_ref, nm_ref, nv_ref = refs[-4:]
        g = refs[3][...]
        for q in refs[4:3 + n]:
            g = g + q[...]
        g = g[:, :cc]
        g_ref[0] = g
        d_ref[0], nm_ref[0], nv_ref[0] = _adamw_update(w_ref[0], g, m_ref[0], v_ref[0])

    g_specs = [pl.BlockSpec((tr, q.shape[1]), lambda i: (i, 0)) for q in g_parts]
    passed = () if prev is None else tuple(prev)
    return pl.pallas_call(
        body, name="adamw_" + name, grid=(r // tr,),
        in_specs=[spec] * 3 + g_specs + [_HBM_SPEC] * len(passed), out_specs=[spec] * 4,
        out_shape=[_sds(w.shape)] * 4, input_output_aliases={3 + n + k: k for k in range(len(passed))},
        compiler_params=_cp(1))(w, m, v, *g_parts, *passed)


def _size(shape):
    n = 1
    for s in shape:
        n *= s
    return n


_SMALL = (("dmod", (DEPTH, 3 * D_MODEL)), ("pre_norm_g", (DEPTH, D_MODEL)), ("post_norm_g", (DEPTH, D_MODEL)),
          ("even_sc_conv_w", (2, SC_KERNEL, SC_WIDTH)), ("even_sc_conv_b", (2, SC_WIDTH)),
          ("even_q_norm_g", (2, Q_LORA)), ("even_kv_norm_g", (2, KV_LORA)),
          ("odd_conv_w", (2, CONF_KERNEL, D_MODEL)), ("odd_conv_b", (2, D_MODEL)), ("odd_ln_g", (2, D_MODEL)),
          ("odd_ln_b", (2, D_MODEL)))
SMALL_ROWS = -(-sum(_size(s) for _, s in _SMALL) // (8 * 128)) * 8

_SMALL_W = (("even_sc_conv_w", (2, SC_KERNEL, SC_WIDTH // N_CHIPS)), ("odd_conv_w", (2, CONF_KERNEL, D_MODEL // N_CHIPS)),
            ("odd_conv_b", (2, D_MODEL // N_CHIPS)), ("odd_ln_g", (2, D_MODEL // N_CHIPS)),
            ("odd_ln_b", (2, D_MODEL // N_CHIPS)))
SMALL_W_ROWS = -(-sum(_size(s) for _, s in _SMALL_W) // (8 * 128)) * 8


def _pack_rows(arrays, layout, rows):
    flat = jnp.concatenate([arrays[n].reshape(-1) for n, _ in layout])
    return jnp.pad(flat, (0, rows * 128 - flat.shape[0])).reshape(rows, 128)


def _unpack_small(t):
    flat = t.reshape(-1)
    out, at = {}, 0
    for n, shape in _SMALL:
        out[n] = flat[at:at + _size(shape)].reshape(shape)
        at += _size(shape)
    return out


def _unpack_small_w(t):
    flat = t.reshape(N_CHIPS, -1)
    out, at = {}, 0
    for n, shape in _SMALL_W:
        a = flat[:, at:at + _size(shape)].reshape((N_CHIPS,) + shape)
        out[n] = jnp.moveaxis(a, 0, -2).reshape(shape[:-1] + (N_CHIPS * shape[-1],))
        at += _size(shape)
    return out


def _chip_cols(a, chip):
    n = a.shape[-1] // N_CHIPS
    return lax.dynamic_slice_in_dim(a, chip * n, n, axis=a.ndim - 1)


WEIGHT_NAMES = ("ada_w", "ada_b", "pre_norm_g", "post_norm_g", "even_w_in", "even_sc_conv_w", "even_sc_conv_b",
                "even_q_norm_g", "even_kv_norm_g", "even_w_uq", "even_w_ukv", "even_w_out", "odd_w_in", "odd_conv_w",
                "odd_conv_b", "odd_ln_g", "odd_ln_b", "odd_w_out")
GATHER_HOW = ((("even_w_in", "slot"), ("even_w_uq", "slot"), ("even_w_ukv", "slot"), ("even_w_out", "rows")),
              (("odd_w_in", "cols"), ("odd_w_out", "rows")))


def kernel(x, c, positions, ada_w, ada_b, pre_norm_g, post_norm_g, even_w_in, even_sc_conv_w, even_sc_conv_b, even_q_norm_g, even_kv_norm_g, even_w_uq, even_w_ukv, even_w_out, odd_w_in, odd_conv_w, odd_conv_b, odd_ln_g, odd_ln_b, odd_w_out, loss_target, m_ada_w, m_ada_b, m_pre_norm_g, m_post_norm_g, m_even_w_in, m_even_sc_conv_w, m_even_sc_conv_b, m_even_q_norm_g, m_even_kv_norm_g, m_even_w_uq, m_even_w_ukv, m_even_w_out, m_odd_w_in, m_odd_conv_w, m_odd_conv_b, m_odd_ln_g, m_odd_ln_b, m_odd_w_out, v_ada_w, v_ada_b, v_pre_norm_g, v_post_norm_g, v_even_w_in, v_even_sc_conv_w, v_even_sc_conv_b, v_even_q_norm_g, v_even_kv_norm_g, v_even_w_uq, v_even_w_ukv, v_even_w_out, v_odd_w_in, v_odd_conv_w, v_odd_conv_b, v_odd_ln_g, v_odd_ln_b, v_odd_w_out):
    w = dict(zip(WEIGHT_NAMES, (ada_w, ada_b, pre_norm_g, post_norm_g, even_w_in, even_sc_conv_w, even_sc_conv_b,
                                even_q_norm_g, even_kv_norm_g, even_w_uq, even_w_ukv, even_w_out, odd_w_in, odd_conv_w,
                                odd_conv_b, odd_ln_g, odd_ln_b, odd_w_out)))
    m = dict(zip(WEIGHT_NAMES, (m_ada_w, m_ada_b, m_pre_norm_g, m_post_norm_g, m_even_w_in, m_even_sc_conv_w,
                                m_even_sc_conv_b, m_even_q_norm_g, m_even_kv_norm_g, m_even_w_uq, m_even_w_ukv,
                                m_even_w_out, m_odd_w_in, m_odd_conv_w, m_odd_conv_b, m_odd_ln_g, m_odd_ln_b, m_odd_w_out)))
    v = dict(zip(WEIGHT_NAMES, (v_ada_w, v_ada_b, v_pre_norm_g, v_post_norm_g, v_even_w_in, v_even_sc_conv_w,
                                v_even_sc_conv_b, v_even_q_norm_g, v_even_kv_norm_g, v_even_w_uq, v_even_w_ukv,
                                v_even_w_out, v_odd_w_in, v_odd_conv_w, v_odd_conv_b, v_odd_ln_g, v_odd_ln_b, v_odd_w_out)))
    ix, iy, ic = _place()
    chip = 2 * ix + iy
    me = 2 * chip + ic
    s = x.shape[1]

    c_all, mod_all = _ada_fwd(jnp.broadcast_to(c, (8, D_MODEL)), ada_w, _chip_cols(ada_b, chip))
    mod = lax.dynamic_index_in_dim(mod_all, me, axis=2, keepdims=False)
    mod = mod.transpose(1, 0, 2).reshape(DEPTH, 3 * D_MODEL)

    items = [[(w[n][layer // 2].astype(MXU_DTYPE), how) for n, how in GATHER_HOW[layer % 2]] for layer in range(DEPTH)]
    groups = [items[0][:1], items[0][1:] + [(_pack_rows(w, _SMALL_W, SMALL_W_ROWS), "slot")],
              [item for layer_items in items[1:] for item in layer_items]]
    sent, dep = [], mod_all
    for number, group in enumerate(groups):
        sent.append(_gather_start(group, [_place_own(a, how, chip.reshape(1)) for a, how in group],
                                  "gather_start_%d" % number, [dep]))
        dep = sent[-1][-1]
    arrived = {}

    def group(number, after):
        if number not in arrived:
            arrived[number] = _gather_wait(groups[number], sent[number], after, "gather_wait_%d" % number)
        return arrived[number]

    def even_rest(i, uq, ukv, eout, small_w):
        wuk, wuv = _ukv_to_heads(ukv)
        wq, wq_rot = _uq_to_heads(uq)
        return {"wq": wq, "wq_rot": wq_rot, "wuk": wuk, "wuv": wuv, "w_out": eout, "sc_conv_w": small_w["even_sc_conv_w"][i]}

    def layer_weights(layer, h):
        i = layer // 2
        if layer == 0:
            def late(z):
                uq, ukv, eout, small = group(1, [z])
                return even_rest(i, uq, ukv, eout, _unpack_small_w(small))
            return {"w_in": _ein_from_shards(group(0, [h])[0]), "late": late}
        small_w = _unpack_small_w(group(1, [h])[-1])
        at = sum(len(layer_items) for layer_items in items[1:layer])
        arrays = group(2, [h])[at:at + len(items[layer])]
        if layer % 2 == 0:
            return {"w_in": _ein_from_shards(arrays[0]), **even_rest(i, *arrays[1:], small_w)}
        oin, oout = arrays
        return {"w_in": oin, "w_out": oout, "conv_w": small_w["odd_conv_w"][i], "conv_b": small_w["odd_conv_b"][i:i + 1],
                "ln_g": small_w["odd_ln_g"][i:i + 1], "ln_b": small_w["odd_ln_b"][i:i + 1]}

    in_flight, own, sib, last = {}, {}, {}, {}

    def land(layer, after):
        names, started, kept = in_flight.pop(layer)
        bufs, arrived = _rs_wait(started, after, "rs_wait_%d" % layer)
        sums = [_add_chips(b, t, chip.reshape(1)) for b, t in zip(bufs if kept is None else kept, arrived)]
        for n, mine, theirs in zip(names, sums, _rs_sibling(sums)):
            own[n, layer // 2], sib[n, layer // 2] = mine, theirs

    def grads_done(layer, bufs, dx_in):
        if layer + 1 in in_flight:
            land(layer + 1, [dx_in])
        if layer == 0:
            last.update(bufs)
            return None
        names = sorted(bufs)
        in_flight[layer] = (names, _rs_start([bufs[n] for n in names], "rs_start_%d" % layer), None)
        return in_flight[layer][1][-1]

    p = {"pre_norm_g": pre_norm_g, "post_norm_g": post_norm_g, "even_sc_conv_b": even_sc_conv_b,
         "even_q_norm_g": even_q_norm_g, "even_kv_norm_g": even_kv_norm_g}
    inv_freq = 1.0 / (ROPE_THETA ** (jnp.arange(0, QK_ROPE, 2, dtype=F32) / QK_ROPE))
    inv_freq = jnp.zeros((1, HEAD_PAD), F32).at[0, QK_NOPE:QK_NOPE + QK_ROPE].set(jnp.tile(inv_freq, 2))
    cos, sin = _rope_tables(positions.reshape(s, 1), inv_freq)

    loss, dx, g = _local_step(x[0], loss_target[0], cos, sin, mod, p, layer_weights, dep, grads_done)

    grads, deltas, new_m, new_v = {}, {}, {}, {}

    def update_layers(n, results, pairs):
        for i in pairs:
            results = _adamw_layer(w[n], [own[n, i], sib[n, i]], m[n], v[n], i, results, n)
        return results

    small_all, small_sum = _gather_sum_all(_pack_rows(g, _SMALL, SMALL_ROWS))
    names = sorted(last)
    kept = [last[n] for n in names]
    in_flight[0] = (names, _rs_start([b.astype(jnp.bfloat16) for b in kept], "rs_start_0", [small_sum]), kept)
    tot = _unpack_small(small_sum)
    dmod_all = small_all[:, :DEPTH * 3 * D_MODEL // 128].reshape(N_DEV, DEPTH, 3 * D_MODEL)
    grads["ada_w"] = _ada_bwd(c_all[:, 0, :].T, _chip_cols(dmod_all, chip).transpose(1, 0, 2))
    grads["ada_b"] = tot["dmod"]
    for n in ("pre_norm_g", "post_norm_g", "even_sc_conv_b", "even_q_norm_g", "even_kv_norm_g"):
        grads[n] = tot[n]
    for n in ("even_sc_conv_w", "odd_conv_w", "odd_conv_b", "odd_ln_g", "odd_ln_b"):
        grads[n] = _chip_cols(tot[n], chip)
    for n in list(grads):
        _, deltas[n], new_m[n], new_v[n] = _adamw(w[n], [grads[n]], m[n], v[n], n)

    for n in ("odd_w_in", "odd_w_out"):
        grads[n], deltas[n], new_m[n], new_v[n] = update_layers(n, None, (1, 0))
    partly = {n: update_layers(n, None, (1,)) for n in ("even_w_in", "even_w_out")}
    land(0, [deltas["ada_w"], deltas["odd_w_in"], partly["even_w_in"][1]])
    for n in ("even_w_in", "even_w_out"):
        grads[n], deltas[n], new_m[n], new_v[n] = update_layers(n, partly[n], (0,))
    uq_parts, ukv_parts = zip(*[[jnp.stack(part) for part in zip(*[_mla_local(q["even_mla", i]) for i in range(N_PAIRS)])]
                                for q in (own, sib)])
    for n, parts in (("even_w_uq", uq_parts), ("even_w_ukv", ukv_parts)):
        grads[n], deltas[n], new_m[n], new_v[n] = _adamw(w[n], list(parts), m[n], v[n], n)

    total_loss = lax.psum(loss[0, 0], ("x", "y", "c"))
    return (total_loss, dx[None], *[grads[n] for n in WEIGHT_NAMES], *[deltas[n] for n in WEIGHT_NAMES],
            *[new_m[n] for n in WEIGHT_NAMES], *[new_v[n] for n in WEIGHT_NAMES])
```

```python
import jax
import jax.numpy as jnp
from jax import lax
from jax.experimental import pallas as pl
from jax.experimental.pallas import tpu as pltpu

F32 = jnp.float32
MXU_DTYPE = jnp.bfloat16
MESH = pl.DeviceIdType.MESH
VMEM_LIMIT_V7X = 56 * 2 ** 20

EPS = 1e-6
D_MODEL = 1024
DEPTH = 4
CHUNK = 64
SC_WIDTH = 512
SC_KERNEL = 3
SC_HALO = 8
HEADS = 8
QK_NOPE = 64
QK_ROPE = 32
V_HEAD = 64
HEAD_PAD = 128
Q_LORA = 256
KV_LORA = 128
ROPE_THETA = 10000.0
CONF_KERNEL = 31
CONF_HALO = 32
CONV_ROWS = 64
SUBLANES = 8
EVEN_IN = 2976
EVEN_PAD = 3072
ODD_IN = 3072
N_CHIPS = 4
N_DEV = 8
NEG = -1e30

ADAM_LR = 0.001
ADAM_B1 = 0.9
ADAM_B2 = 0.999
ADAM_EPS = 1e-08
ADAM_WD = 0.01
ADAM_STEP = 10

N_PAIRS = DEPTH // 2
EVEN_SHARD = EVEN_IN // N_CHIPS
EVEN_SHARD_PAD = 768
MLA_ROWS = Q_LORA + 2 * KV_LORA


def _cp(n_grid=0, **kw):
    return pltpu.CompilerParams(dimension_semantics=("arbitrary",) * n_grid,
                                vmem_limit_bytes=VMEM_LIMIT_V7X, **kw)


def _sigmoid(x):
    return 1.0 / (1.0 + jnp.exp(-x))


def _silu(x):
    return x * _sigmoid(x)


def _dsilu(x):
    s = _sigmoid(x)
    return s * (1.0 + x * (1.0 - s))


def _rms(x, g):
    return x * lax.rsqrt(jnp.mean(x * x, axis=-1, keepdims=True) + EPS) * g


def _dot(a, b, dims):
    return lax.dot_general(a.astype(MXU_DTYPE), b.astype(MXU_DTYPE), (dims, ((), ())),
                           preferred_element_type=F32)


def _dot_nn(a, b):
    return _dot(a, b, ((1,), (0,)))


def _dot_nt(a, b):
    return _dot(a, b, ((1,), (1,)))


def _dot_tn(a, b):
    return _dot(a, b, ((0,), (0,)))


def _rows(ts, w, cb=0):
    return pl.BlockSpec((ts, w), lambda i: (i, cb))


def _vec(w, cb=0, r=1):
    return pl.BlockSpec((r, w), lambda i: (0, cb))


def _prev_halo(ts, hr, w, cb):
    return pl.BlockSpec((hr, w), lambda i: (jnp.maximum(i * (ts // hr) - 1, 0), cb))


def _next_halo(ts, hr, w, cb, s):
    return pl.BlockSpec((hr, w), lambda i: (jnp.minimum((i + 1) * (ts // hr), s // hr - 1), cb))


def _sds(shape, dtype=F32):
    return jax.ShapeDtypeStruct(shape, dtype)


def _mm(a, b, mode, out_dtype, tm, tn, name):
    tm = min(tm, a.shape[1] if mode == "tn" else a.shape[0])
    tn = min(tn, b.shape[0] if mode == "nt" else b.shape[1])
    if mode == "nn":
        (m, k), n = a.shape, b.shape[1]
        a_spec = pl.BlockSpec((tm, k), lambda i, j: (i, 0))
        b_spec = pl.BlockSpec((k, tn), lambda i, j: (0, j))
        dot = _dot_nn
    elif mode == "nt":
        (m, k), n = a.shape, b.shape[0]
        a_spec = pl.BlockSpec((tm, k), lambda i, j: (i, 0))
        b_spec = pl.BlockSpec((tn, k), lambda i, j: (j, 0))
        dot = _dot_nt
    else:
        (k, m), n = a.shape, b.shape[1]
        a_spec = pl.BlockSpec((k, tm), lambda i, j: (0, i))
        b_spec = pl.BlockSpec((k, tn), lambda i, j: (0, j))
        dot = _dot_tn
    assert m % tm == 0 and n % tn == 0, (name, m, n, tm, tn)

    def body(a_ref, b_ref, o_ref):
        o_ref[...] = dot(a_ref[...], b_ref[...]).astype(o_ref.dtype)

    return pl.pallas_call(
        body, name=name, grid=(m // tm, n // tn), in_specs=[a_spec, b_spec],
        out_specs=pl.BlockSpec((tm, tn), lambda i, j: (i, j)), out_shape=_sds((m, n), out_dtype),
        compiler_params=_cp(2))(a, b)


def _mm_tn_shards(a, b, by, name):
    k, m = a.shape
    n = b.shape[1]
    if by == "cols":
        tm, tn = m, n // N_CHIPS
        shape, grid = (N_CHIPS, m, tn), (1, N_CHIPS)
        out_spec = pl.BlockSpec((1, tm, tn), lambda i, j: (j, i, 0))
    else:
        tm, tn = m // N_CHIPS, n
        shape, grid = (N_CHIPS, tm, n), (N_CHIPS, 1)
        out_spec = pl.BlockSpec((1, tm, tn), lambda i, j: (i, 0, j))

    def body(a_ref, b_ref, o_ref):
        o_ref[0] = _dot_tn(a_ref[...], b_ref[...])

    return pl.pallas_call(
        body, name=name, grid=grid,
        in_specs=[pl.BlockSpec((k, tm), lambda i, j: (0, i)), pl.BlockSpec((k, tn), lambda i, j: (0, j))],
        out_specs=out_spec, out_shape=_sds(shape), compiler_params=_cp(2))(a, b)


def _even_col(q):
    return q if q < 2432 else (q + 64 if q < 2464 else q + 96)


def _shard_pieces(j):
    lo, hi = EVEN_SHARD * j, EVEN_SHARD * (j + 1)
    cuts = [lo] + [b for b in (2432, 2464) if lo < b < hi] + [hi]
    return [(a - lo, _even_col(a), b - a) for a, b in zip(cuts[:-1], cuts[1:])]


def _ein_from_shards(w):
    _, d, _ = w.shape
    tr = 256

    def body(w_ref, o_ref):
        parts, at = [], 0
        for j in range(N_CHIPS):
            for d0, s0, n in _shard_pieces(j):
                if s0 > at:
                    parts.append(jnp.zeros((tr, s0 - at), F32))
                parts.append(w_ref[j, :, d0:d0 + n].astype(F32))
                at = s0 + n
        o_ref[...] = jnp.concatenate(parts, axis=1).astype(o_ref.dtype)

    return pl.pallas_call(
        body, name="ein_from_shards", grid=(d // tr,),
        in_specs=[pl.BlockSpec((N_CHIPS, tr, EVEN_SHARD), lambda i: (0, i, 0))],
        out_specs=_rows(tr, EVEN_PAD), out_shape=_sds((d, EVEN_PAD), w.dtype), compiler_params=_cp(1))(w)


def _ein_to_shards(dw):
    d = dw.shape[0]
    tr = 256

    def body(dw_ref, o_ref):
        for j in range(N_CHIPS):
            parts = [dw_ref[:, s0:s0 + n] for _, s0, n in _shard_pieces(j)]
            o_ref[j] = jnp.concatenate(parts + [jnp.zeros((tr, EVEN_SHARD_PAD - EVEN_SHARD), F32)], axis=1)

    return pl.pallas_call(
        body, name="ein_to_shards", grid=(d // tr,), in_specs=[_rows(tr, EVEN_PAD)],
        out_specs=pl.BlockSpec((N_CHIPS, tr, EVEN_SHARD_PAD), lambda i: (0, i, 0)),
        out_shape=_sds((N_CHIPS, d, EVEN_SHARD_PAD)), compiler_params=_cp(1))(dw)


def _rope_tables(pos_col, invf):
    s = pos_col.shape[0]
    ts = min(512, s)

    def body(p_ref, f_ref, c_ref, s_ref):
        ang = p_ref[...].astype(F32) * f_ref[...]
        lane = lax.broadcasted_iota(jnp.int32, ang.shape, 1)
        rope = (lane >= QK_NOPE) & (lane < QK_NOPE + QK_ROPE)
        c_ref[...] = jnp.where(lane < QK_NOPE, 1.0, jnp.where(rope, jnp.cos(ang), 0.0))
        s_ref[...] = jnp.where(rope, jnp.sin(ang), 0.0)

    return pl.pallas_call(
        body, name="rope_tables", grid=(s // ts,), in_specs=[_rows(ts, 1), _vec(HEAD_PAD)],
        out_specs=[_rows(ts, HEAD_PAD)] * 2, out_shape=[_sds((s, HEAD_PAD))] * 2,
        compiler_params=_cp(1))(pos_col, invf)


def _after(dep):
    return () if dep is None else (dep,)


def _pre_fwd(x, g, mod_l, ts, dep=None):
    s, d = x.shape

    def body(x_ref, g_ref, sh_ref, sc_ref, *rest):
        h = _rms(x_ref[...], g_ref[...]) * (1.0 + sc_ref[...]) + sh_ref[...]
        rest[-1][...] = h.astype(rest[-1].dtype)

    return pl.pallas_call(
        body, name="pre_fwd", grid=(s // ts,),
        in_specs=[_rows(ts, d), _vec(d), _vec(d, 0), _vec(d, 1)] + [_HBM_SPEC] * len(_after(dep)),
        out_specs=_rows(ts, d), out_shape=_sds((s, d), MXU_DTYPE), compiler_params=_cp(1))(
            x, g, mod_l, mod_l, *_after(dep))


def _pre_bwd(dz, w_in, dx_out, x, g, mod_l, ts):
    s, d = x.shape
    n_in = dz.shape[1]

    def f(xv, gv, sh, sc):
        return _rms(xv, gv) * (1.0 + sc) + sh

    def body(dz_ref, w_ref, dxo_ref, x_ref, g_ref, sh_ref, sc_ref, dx_ref, dsh_ref, dsc_ref, dg_ref):
        @pl.when(pl.program_id(0) == 0)
        def _():
            dsh_ref[...] = jnp.zeros_like(dsh_ref)
            dsc_ref[...] = jnp.zeros_like(dsc_ref)
            dg_ref[...] = jnp.zeros_like(dg_ref)

        _, vjp = jax.vjp(f, x_ref[...], g_ref[...], sh_ref[...], sc_ref[...])
        dx, dg, dsh, dsc = vjp(_dot_nt(dz_ref[...], w_ref[...]))
        dx_ref[...] = dxo_ref[...] + dx
        dsh_ref[...] += dsh
        dsc_ref[...] += dsc
        dg_ref[...] += dg

    return pl.pallas_call(
        body, name="pre_bwd", grid=(s // ts,),
        in_specs=[_rows(ts, n_in), _vec(n_in, 0, d), _rows(ts, d), _rows(ts, d), _vec(d), _vec(d, 0), _vec(d, 1)],
        out_specs=[_rows(ts, d), _vec(d), _vec(d), _vec(d)],
        out_shape=[_sds((s, d)), _sds((1, d)), _sds((1, d)), _sds((1, d))],
        compiler_params=_cp(1))(dz, w_in, dx_out, x, g, mod_l, mod_l)


def _post_pre_fwd(x, yo, g_post, mod_l, g_pre, mod_next, ts):
    s, d = x.shape

    def body(x_ref, yo_ref, gp_ref, gate_ref, g_ref, sh_ref, sc_ref, x_out_ref, h_ref):
        x_new = x_ref[...] + gate_ref[...] * _rms(yo_ref[...], gp_ref[...])
        x_out_ref[...] = x_new
        h_ref[...] = (_rms(x_new, g_ref[...]) * (1.0 + sc_ref[...]) + sh_ref[...]).astype(h_ref.dtype)

    return pl.pallas_call(
        body, name="post_pre_fwd", grid=(s // ts,),
        in_specs=[_rows(ts, d), _rows(ts, d), _vec(d), _vec(d, 2), _vec(d), _vec(d, 0), _vec(d, 1)],
        out_specs=[_rows(ts, d), _rows(ts, d)], out_shape=[_sds((s, d)), _sds((s, d), MXU_DTYPE)],
        compiler_params=_cp(1))(x, yo, g_post, mod_l, g_pre, mod_next, mod_next)


def _post_loss(x, yo, g_post, mod_l, target, ts):
    s, d = x.shape

    def body(x_ref, yo_ref, gp_ref, gate_ref, t_ref, loss_ref, dx_ref):
        err = x_ref[...] + gate_ref[...] * _rms(yo_ref[...], gp_ref[...]) - t_ref[...]
        dx_ref[...] = err * (1.0 / d)

        @pl.when(pl.program_id(0) == 0)
        def _():
            loss_ref[...] = jnp.zeros_like(loss_ref)

        loss_ref[...] += 0.5 * jnp.sum(jnp.sum(err * err, axis=-1, keepdims=True) * (1.0 / d), axis=0, keepdims=True)

    return pl.pallas_call(
        body, name="post_loss", grid=(s // ts,),
        in_specs=[_rows(ts, d), _rows(ts, d), _vec(d), _vec(d, 2), _rows(ts, d)],
        out_specs=[_vec(1), _rows(ts, d)], out_shape=[_sds((1, 1)), _sds((s, d))],
        compiler_params=_cp(1))(x, yo, g_post, mod_l, target)


def _post_bwd(dx_out, yo, g, mod_l, ts, dep=None):
    s, d = yo.shape

    def f(yov, gv, gate):
        return gate * _rms(yov, gv)

    def body(dx_ref, yo_ref, g_ref, gate_ref, *rest):
        dyo_ref, dgate_ref, dg_ref = rest[-3:]
        i = pl.program_id(0)
        _, vjp = jax.vjp(f, yo_ref[...], g_ref[...], gate_ref[...])
        dyo, dg, dgate = vjp(dx_ref[...])
        dyo_ref[...] = dyo.astype(dyo_ref.dtype)

        @pl.when(i == 0)
        def _():
            dgate_ref[...] = jnp.zeros_like(dgate_ref)
            dg_ref[...] = jnp.zeros_like(dg_ref)

        dgate_ref[...] += dgate
        dg_ref[...] += dg

    return pl.pallas_call(
        body, name="post_bwd", grid=(s // ts,),
        in_specs=[_rows(ts, d), _rows(ts, d), _vec(d), _vec(d, 2)] + [_HBM_SPEC] * len(_after(dep)),
        out_specs=[_rows(ts, d), _vec(d), _vec(d)],
        out_shape=[_sds((s, d), MXU_DTYPE), _sds((1, d)), _sds((1, d))],
        compiler_params=_cp(1))(dx_out, yo, g, mod_l, *_after(dep))


def _rope(t, cos, sin):
    lane = lax.broadcasted_iota(jnp.int32, t.shape, 1)
    first = (lane >= QK_NOPE) & (lane < QK_NOPE + QK_ROPE // 2)
    second = (lane >= QK_NOPE + QK_ROPE // 2) & (lane < QK_NOPE + QK_ROPE)
    up = pltpu.roll(t, QK_ROPE // 2, 1)
    down = pltpu.roll(t, HEAD_PAD - QK_ROPE // 2, 1)
    return t * cos + jnp.where(first, -down, jnp.where(second, up, 0.0)) * sin


def _rope_transposed(g, cos, sin):
    lane = lax.broadcasted_iota(jnp.int32, g.shape, 1)
    first = (lane >= QK_NOPE) & (lane < QK_NOPE + QK_ROPE // 2)
    second = (lane >= QK_NOPE + QK_ROPE // 2) & (lane < QK_NOPE + QK_ROPE)
    u = g * sin
    up = pltpu.roll(u, QK_ROPE // 2, 1)
    down = pltpu.roll(u, HEAD_PAD - QK_ROPE // 2, 1)
    return g * cos + jnp.where(first, down, jnp.where(second, -up, 0.0))


def _mla_prep_fwd(z, cos, sin, qg, kvg, wq, wq_rot, wuk, wuv, ts):
    s = z.shape[0]
    wide = HEADS * HEAD_PAD

    def body(cq_ref, ckv_ref, kr_ref, cos_ref, sin_ref, qg_ref, kvg_ref, wq_ref, wqr_ref, wuk_ref, wuv_ref,
             q_ref, qt_ref, k_ref, v_ref):
        cos_v, sin_v = cos_ref[...], sin_ref[...]
        cqn = _rms(cq_ref[...], qg_ref[...])
        ckvn = _rms(ckv_ref[...], kvg_ref[...])
        kr = _rope(kr_ref[...], cos_v, sin_v)
        q_lin, q_rot = _dot_nn(cqn, wq_ref[...]), _dot_nn(cqn, wqr_ref[...])
        k_lin, v_all = _dot_nn(ckvn, wuk_ref[...]), _dot_nn(ckvn, wuv_ref[...])
        for h in range(HEADS):
            lanes = slice(h * HEAD_PAD, (h + 1) * HEAD_PAD)
            qh = q_lin[:, lanes] * cos_v + q_rot[:, lanes] * sin_v
            q_ref[h] = qh.astype(q_ref.dtype)
            qt_ref[h, 0] = qh.T.astype(qt_ref.dtype)
            k_ref[h] = (k_lin[:, lanes] + kr).astype(k_ref.dtype)
            v_ref[h] = v_all[:, lanes].astype(v_ref.dtype)

    out = pl.BlockSpec((HEADS, ts, HEAD_PAD), lambda i: (0, i, 0))
    return pl.pallas_call(
        body, name="mla_prep_fwd", grid=(s // ts,),
        in_specs=[_rows(ts, Q_LORA, 8), _rows(ts, KV_LORA, 18), _rows(ts, HEAD_PAD, 19), _rows(ts, HEAD_PAD), _rows(ts, HEAD_PAD),
                  _vec(Q_LORA), _vec(KV_LORA), _vec(wide, 0, Q_LORA), _vec(wide, 0, Q_LORA), _vec(wide, 0, KV_LORA),
                  _vec(wide, 0, KV_LORA)],
        out_specs=[out, pl.BlockSpec((HEADS, 1, HEAD_PAD, ts), lambda i: (0, i, 0, 0)), out, out],
        out_shape=[_sds((HEADS, s, HEAD_PAD), MXU_DTYPE), _sds((HEADS, s // ts, HEAD_PAD, ts), MXU_DTYPE)]
        + [_sds((HEADS, s, HEAD_PAD), MXU_DTYPE)] * 2,
        compiler_params=_cp(1))(z, z, z, cos, sin, qg, kvg, wq, wq_rot, wuk, wuv)


def _mla_prep_bwd(dz, dq, dk, dv, z, cos, sin, qg, kvg, wq, wuk, wuv, ts):
    s = z.shape[0]

    def fq(cq, g):
        return _rms(cq, g)

    def body(dz_in_ref, dq_ref, dk_ref, dv_ref, cq_ref, ckv_ref, cos_ref, sin_ref, qg_ref, kvg_ref, wq_ref, wuk_ref,
             wuv_ref, dz_ref, dw_ref, dqg_ref, dkvg_ref):
        del dz_in_ref
        cos_v, sin_v = cos_ref[...], sin_ref[...]

        @pl.when(pl.program_id(0) == 0)
        def _():
            dw_ref[...] = jnp.zeros_like(dw_ref)
            dqg_ref[...] = jnp.zeros_like(dqg_ref)
            dkvg_ref[...] = jnp.zeros_like(dkvg_ref)

        cqn, vjp_q = jax.vjp(fq, cq_ref[...], qg_ref[...])
        ckvn, vjp_kv = jax.vjp(fq, ckv_ref[...], kvg_ref[...])
        lane = lax.broadcasted_iota(jnp.int32, (ts, HEAD_PAD), 1)
        rope_lanes = (lane >= QK_NOPE) & (lane < QK_NOPE + QK_ROPE)
        dq_lin = jnp.concatenate([_rope_transposed(dq_ref[h], cos_v, sin_v).astype(MXU_DTYPE) for h in range(HEADS)], axis=1)
        dk_all = jnp.concatenate([dk_ref[h].astype(MXU_DTYPE) for h in range(HEADS)], axis=1)
        dv_all = jnp.concatenate([dv_ref[h].astype(MXU_DTYPE) for h in range(HEADS)], axis=1)
        dkr = jnp.where(rope_lanes, dk_ref[0], 0.0)
        for h in range(1, HEADS):
            dkr = dkr + jnp.where(rope_lanes, dk_ref[h], 0.0)
        dcq, dqg = vjp_q(_dot_nt(dq_lin, wq_ref[...]))
        dckv, dkvg = vjp_kv(_dot_nt(dk_all, wuk_ref[...]) + _dot_nt(dv_all, wuv_ref[...]))
        dz_ref[:, 0:Q_LORA] = dcq.astype(dz_ref.dtype)
        dz_ref[:, Q_LORA:Q_LORA + KV_LORA] = dckv.astype(dz_ref.dtype)
        dz_ref[:, Q_LORA + KV_LORA:] = _rope_transposed(dkr, cos_v, sin_v).astype(dz_ref.dtype)
        dqg_ref[...] += dqg
        dkvg_ref[...] += dkvg
        dwq, dwuk, dwuv = _dot_tn(cqn, dq_lin), _dot_tn(ckvn, dk_all), _dot_tn(ckvn, dv_all)
        for h in range(HEADS):
            lanes = slice(h * HEAD_PAD, (h + 1) * HEAD_PAD)
            row0 = (h % 2) * MLA_ROWS
            dw_ref[h // 2, row0:row0 + Q_LORA, :] += dwq[:, lanes]
            dw_ref[h // 2, row0 + Q_LORA:row0 + Q_LORA + KV_LORA, :] += dwuk[:, lanes]
            dw_ref[h // 2, row0 + Q_LORA + KV_LORA:row0 + MLA_ROWS, :] += dwuv[:, lanes]

    wide = HEADS * HEAD_PAD
    heads = pl.BlockSpec((HEADS, ts, HEAD_PAD), lambda i: (0, i, 0))
    whole = pl.BlockSpec((N_CHIPS, 2 * MLA_ROWS, HEAD_PAD), lambda i: (0, 0, 0))
    return pl.pallas_call(
        body, name="mla_prep_bwd", grid=(s // ts,),
        in_specs=[_HBM_SPEC, heads, heads, heads, _rows(ts, Q_LORA, 8), _rows(ts, KV_LORA, 18),
                  _rows(ts, HEAD_PAD), _rows(ts, HEAD_PAD), _vec(Q_LORA), _vec(KV_LORA), _vec(wide, 0, Q_LORA),
                  _vec(wide, 0, KV_LORA), _vec(wide, 0, KV_LORA)],
        out_specs=[_rows(ts, 512, 4), whole, _vec(Q_LORA), _vec(KV_LORA)],
        out_shape=[_sds(dz.shape, dz.dtype), _sds((N_CHIPS, 2 * MLA_ROWS, HEAD_PAD)), _sds((1, Q_LORA)), _sds((1, KV_LORA))],
        input_output_aliases={0: 0}, compiler_params=_cp(1))(dz, dq, dk, dv, z, z, cos, sin, qg, kvg, wq, wuk, wuv)


def _chunk_mask(q0, k0, tq, tk):
    rows = q0 + lax.broadcasted_iota(jnp.int32, (tq, tk), 0)
    cols = k0 + lax.broadcasted_iota(jnp.int32, (tq, tk), 1)
    shift = CHUNK.bit_length() - 1
    return lax.shift_right_logical(cols, shift) <= lax.shift_right_logical(rows, shift)


def _attn_fwd(q, k, v, tq):
    s = q.shape[1]
    nq = s // tq
    scale = 1.0 / float(QK_NOPE + QK_ROPE) ** 0.5

    assert nq % 2 == 0, (s, tq)

    def body(q_ref, k_ref, v_ref, o_ref, lse_ref):
        pair, hh = pl.program_id(1), pl.program_id(2)

        def step(qv, q0, kj, carry, masked):
            m, l, acc = carry
            k0 = pl.multiple_of(kj * tq, tq)
            sc = _dot_nt(qv, k_ref[0, pl.ds(k0, tq), :]) * scale
            if masked:
                sc = jnp.where(_chunk_mask(q0, k0, tq, tq), sc, NEG)
            m_new = jnp.maximum(m, jnp.max(sc, axis=-1, keepdims=True))
            alpha = jnp.exp(m - m_new)
            p = jnp.exp(sc - m_new)
            l = alpha * l + jnp.sum(p, axis=-1, keepdims=True)
            acc = alpha * acc + _dot_nn(p, v_ref[0, pl.ds(k0, tq), :])
            return m_new, l, acc

        for half in range(2):
            rows = slice(half * tq, (half + 1) * tq)
            qv = q_ref[0, rows, :]
            q0 = (2 * pair + half) * tq
            two = lambda i, c: step(qv, q0, 2 * i + 1, step(qv, q0, 2 * i, c, False), False)
            init = (jnp.full((tq, 1), NEG, F32), jnp.zeros((tq, 1), F32), jnp.zeros((tq, HEAD_PAD), F32))
            carry = lax.fori_loop(0, pair, two, init)
            if half == 1:
                carry = step(qv, q0, 2 * pair, carry, False)
            m, l, acc = step(qv, q0, 2 * pair + half, carry, True)
            o = acc / l
            lse_ref[0, rows, :] = m + jnp.log(l)

            @pl.when(hh == 0)
            def _():
                o_ref[rows, :] = o

            @pl.when(hh == 1)
            def _():
                o_ref[rows, :] += o

    head = lambda hp, pair, hh: 2 * hp + hh
    return pl.pallas_call(
        body, name="attn_fwd", grid=(HEADS // 2, nq // 2, 2),
        in_specs=[pl.BlockSpec((1, 2 * tq, HEAD_PAD), lambda hp, pair, hh: (head(hp, pair, hh), pair, 0)),
                  pl.BlockSpec((1, s, HEAD_PAD), lambda hp, pair, hh: (head(hp, pair, hh), 0, 0)),
                  pl.BlockSpec((1, s, HEAD_PAD), lambda hp, pair, hh: (head(hp, pair, hh), 0, 0))],
        out_specs=[pl.BlockSpec((2 * tq, HEAD_PAD), lambda hp, pair, hh: (pair, hp)),
                   pl.BlockSpec((1, 2 * tq, 1), lambda hp, pair, hh: (head(hp, pair, hh), pair, 0))],
        out_shape=[_sds((s, HEADS * V_HEAD)), _sds((HEADS, s, 1))],
        compiler_params=_cp(3))(q, k, v)


def _attn_bwd(q, q_t, k, v, do, do_t, o, lse, tq):
    s = q.shape[1]
    nq = s // tq
    per_q = tq // do_t.shape[3]
    scale = 1.0 / float(QK_NOPE + QK_ROPE) ** 0.5

    def body(q_ref, qt_ref, k_ref, v_ref, do_ref, dot_ref, o_ref, lse_ref, dq_ref, dk_ref, dv_ref, dk_t, dv_t):
        hh, kj = pl.program_id(1), pl.program_id(2)

        @pl.when(kj == 0)
        def _():
            dq_ref[...] = jnp.zeros_like(dq_ref)

        kv, vv = k_ref[0], v_ref[0]
        lane = lax.broadcasted_iota(jnp.int32, (tq, HEAD_PAD), 1)
        mine = lax.shift_right_logical(lane, 6) == hh
        dk_t[...] = jnp.zeros_like(dk_t)
        dv_t[...] = jnp.zeros_like(dv_t)

        def step(qi, masked):
            q0 = pl.multiple_of(qi * tq, tq)
            qv = q_ref[0, pl.ds(q0, tq), :]
            dov = do_ref[pl.ds(q0, tq), :]
            delta = jnp.sum(jnp.where(mine, dov * o_ref[pl.ds(q0, tq), :], 0.0), axis=-1, keepdims=True)
            sc = _dot_nt(qv, kv) * scale
            if masked:
                sc = jnp.where(_chunk_mask(q0, kj * tq, tq, tq), sc, NEG)
            p = jnp.exp(sc - lse_ref[0, pl.ds(q0, tq), :])
            ds = (p * (_dot_nt(dov, vv) - delta) * scale).astype(MXU_DTYPE)
            do_tv = jnp.concatenate([dot_ref[0, qi * per_q + r] for r in range(per_q)], axis=1)
            dv_t[...] += _dot_nn(do_tv, p)
            dk_t[...] += _dot_nn(qt_ref[0, qi], ds)
            dq_ref[0, pl.ds(q0, tq), :] += _dot_nn(ds, kv)

        step(kj, True)
        odd = (nq - 1 - kj) % 2

        @pl.when(odd == 1)
        def _():
            step(kj + 1, False)

        def two(i, c):
            step(kj + 1 + odd + 2 * i, False)
            step(kj + 2 + odd + 2 * i, False)
            return c

        lax.fori_loop(0, (nq - 1 - kj) // 2, two, 0)
        dk_ref[0] = dk_t[...].T
        dv_ref[0] = dv_t[...].T

    head = lambda hp, hh, kj: 2 * hp + hh
    full = pl.BlockSpec((1, s, HEAD_PAD), lambda hp, hh, kj: (head(hp, hh, kj), 0, 0))
    blk = pl.BlockSpec((1, tq, HEAD_PAD), lambda hp, hh, kj: (head(hp, hh, kj), kj, 0))
    pair = pl.BlockSpec((s, HEAD_PAD), lambda hp, hh, kj: (0, hp))
    return pl.pallas_call(
        body, name="attn_bwd", grid=(HEADS // 2, 2, nq),
        in_specs=[full, pl.BlockSpec((1,) + q_t.shape[1:], lambda hp, hh, kj: (head(hp, hh, kj), 0, 0, 0)), blk, blk,
                  pair, pl.BlockSpec((1,) + do_t.shape[1:], lambda hp, hh, kj: (hp, 0, 0, 0)), pair,
                  pl.BlockSpec((1, s, 1), lambda hp, hh, kj: (head(hp, hh, kj), 0, 0))],
        out_specs=[full, blk, blk], out_shape=[_sds((HEADS, s, HEAD_PAD))] * 3,
        scratch_shapes=[pltpu.VMEM((HEAD_PAD, tq), F32), pltpu.VMEM((HEAD_PAD, tq), F32)],
        compiler_params=_cp(3))(q, q_t, k, v, do, do_t, o, lse)


def _sc_conv(u, ubuf, w_ref, b_ref, ts):
    return (w_ref[2:3, :] * u + w_ref[1:2, :] * ubuf[pl.ds(SC_HALO - 1, ts), :]
            + w_ref[0:1, :] * ubuf[pl.ds(SC_HALO - 2, ts), :] + b_ref[...])


def _even_gate_fwd(z, o, sc_w, sc_b, ts):
    s = z.shape[0]
    w = SC_WIDTH

    def body(ab_ref, ac_ref, ax_ref, ag_ref, bg_ref, hc_ref, hx_ref, o_ref, w_ref, b_ref, y_ref, ubuf):
        i = pl.program_id(0)
        u = ac_ref[...] * ax_ref[...]
        ubuf[0:SC_HALO, :] = jnp.where(i > 0, hc_ref[...] * hx_ref[...], 0.0)
        ubuf[SC_HALO:, :] = u
        conv = _sc_conv(u, ubuf, w_ref, b_ref, ts)
        y_ref[:, 0:w] = (ab_ref[...] * conv * _silu(ag_ref[...])).astype(y_ref.dtype)
        y_ref[:, w:] = (o_ref[...] * _silu(bg_ref[...])).astype(y_ref.dtype)

    return pl.pallas_call(
        body, name="even_gate_fwd", grid=(s // ts,),
        in_specs=[_rows(ts, w, 0), _rows(ts, w, 1), _rows(ts, w, 2), _rows(ts, w, 3), _rows(ts, w, 5),
                  _prev_halo(ts, SC_HALO, w, 1), _prev_halo(ts, SC_HALO, w, 2), _rows(ts, w),
                  _vec(w, 0, SC_KERNEL), _vec(w)],
        out_specs=_rows(ts, 2 * w), out_shape=_sds((s, 2 * w), MXU_DTYPE),
        scratch_shapes=[pltpu.VMEM((ts + SC_HALO, w), F32)],
        compiler_params=_cp(1))(z, z, z, z, z, z, z, o, sc_w, sc_b)


def _even_gate_bwd(dy, z, o, sc_w, sc_b, ts):
    s = z.shape[0]
    w = SC_WIDTH
    n = s // ts

    def body(dya_ref, dyb_ref, dyan_ref, ab_ref, ac_ref, ax_ref, ag_ref, bg_ref, hc_ref, hx_ref, abn_ref, agn_ref,
             o_ref, w_ref, b_ref, dz_ref, do_ref, dot_ref, dw_ref, db_ref, ubuf, dbuf):
        i = pl.program_id(0)
        ab, ac, ax, ag, bg = ab_ref[...], ac_ref[...], ax_ref[...], ag_ref[...], bg_ref[...]
        dya, dyb = dya_ref[...], dyb_ref[...]
        u = ac * ax
        ubuf[0:SC_HALO, :] = jnp.where(i > 0, hc_ref[...] * hx_ref[...], 0.0)
        ubuf[SC_HALO:, :] = u
        conv = _sc_conv(u, ubuf, w_ref, b_ref, ts)
        sg = _silu(ag)
        dconv = dya * ab * sg
        dbuf[0:ts, :] = dconv
        dbuf[ts:, :] = jnp.where(i < n - 1, dyan_ref[...] * abn_ref[...] * _silu(agn_ref[...]), 0.0)
        du = w_ref[2:3, :] * dconv + w_ref[1:2, :] * dbuf[pl.ds(1, ts), :] + w_ref[0:1, :] * dbuf[pl.ds(2, ts), :]
        dz_ref[:, 0:w] = (dya * conv * sg).astype(dz_ref.dtype)
        dz_ref[:, w:2 * w] = (du * ax).astype(dz_ref.dtype)
        dz_ref[:, 2 * w:3 * w] = (du * ac).astype(dz_ref.dtype)
        dz_ref[:, 3 * w:4 * w] = (dya * ab * conv * _dsilu(ag)).astype(dz_ref.dtype)
        dz_ref[:, 4 * w:5 * w] = jnp.zeros((ts, w), dz_ref.dtype)
        dz_ref[:, 5 * w:] = (dyb * o_ref[...] * _dsilu(bg)).astype(dz_ref.dtype)
        do = dyb * _silu(bg)
        do_ref[...] = do
        for pair in range(HEADS // 2):
            dot_ref[pair, 0] = do[:, pair * HEAD_PAD:(pair + 1) * HEAD_PAD].T.astype(dot_ref.dtype)

        @pl.when(i == 0)
        def _():
            dw_ref[...] = jnp.zeros_like(dw_ref)
            db_ref[...] = jnp.zeros_like(db_ref)

        dw_ref[0:1, :] += jnp.sum(dconv * ubuf[pl.ds(SC_HALO - 2, ts), :], axis=0, keepdims=True)
        dw_ref[1:2, :] += jnp.sum(dconv * ubuf[pl.ds(SC_HALO - 1, ts), :], axis=0, keepdims=True)
        dw_ref[2:3, :] += jnp.sum(dconv * u, axis=0, keepdims=True)
        db_ref[...] += jnp.sum(dconv, axis=0, keepdims=True)

    return pl.pallas_call(
        body, name="even_gate_bwd", grid=(n,),
        in_specs=[_rows(ts, w, 0), _rows(ts, w, 1), _next_halo(ts, SC_HALO, w, 0, s),
                  _rows(ts, w, 0), _rows(ts, w, 1), _rows(ts, w, 2), _rows(ts, w, 3), _rows(ts, w, 5),
                  _prev_halo(ts, SC_HALO, w, 1), _prev_halo(ts, SC_HALO, w, 2),
                  _next_halo(ts, SC_HALO, w, 0, s), _next_halo(ts, SC_HALO, w, 3, s),
                  _rows(ts, w), _vec(w, 0, SC_KERNEL), _vec(w)],
        out_specs=[_rows(ts, EVEN_PAD), _rows(ts, w), pl.BlockSpec((HEADS // 2, 1, HEAD_PAD, ts), lambda i: (0, i, 0, 0)),
                   _vec(w, 0, SC_KERNEL), _vec(w)],
        out_shape=[_sds((s, EVEN_PAD), MXU_DTYPE), _sds((s, w)), _sds((HEADS // 2, n, HEAD_PAD, ts), MXU_DTYPE),
                   _sds((SC_KERNEL, w)), _sds((1, w))],
        scratch_shapes=[pltpu.VMEM((ts + SC_HALO, w), F32), pltpu.VMEM((ts + SC_HALO, w), F32)],
        compiler_params=_cp(1))(dy, dy, dy, z, z, z, z, z, z, z, z, z, o, sc_w, sc_b)


def _ln_act(uc, sg, g, b):
    mu = jnp.mean(uc, axis=-1, keepdims=True)
    var = jnp.mean(jnp.square(uc - mu), axis=-1, keepdims=True)
    return _silu((uc - mu) * lax.rsqrt(var + EPS) * g + b) * _silu(sg)


def _shifted_copies(buf, shifted, rows):
    for b in range(1, SUBLANES):
        shifted[b - 1, 0:rows, :] = buf[pl.ds(b, rows), :]


def _rows_at(buf, shifted, start, n):
    a, b = divmod(start, SUBLANES)
    return buf[pl.ds(SUBLANES * a, n), :] if b == 0 else shifted[b - 1, pl.ds(SUBLANES * a, n), :]


def _odd_fwd(z, conv_w, conv_b, ln_g, ln_b, ts):
    s = z.shape[0]
    d = D_MODEL
    k = CONF_KERNEL

    def body(val_ref, glu_ref, sg_ref, hval_ref, hglu_ref, w_ref, b_ref, g_ref, beta_ref, y_ref, uc_ref, ubuf, ush):
        i = pl.program_id(0)
        ubuf[0:CONF_HALO, :] = jnp.where(i > 0, hval_ref[...] * _sigmoid(hglu_ref[...]), 0.0)
        ubuf[CONF_HALO:, :] = val_ref[...] * _sigmoid(glu_ref[...])
        _shifted_copies(ubuf, ush, ts + CONF_HALO - SUBLANES)
        for r0 in range(0, ts, CONV_ROWS):
            acc = jnp.broadcast_to(b_ref[...], (CONV_ROWS, d))
            for j in range(k):
                acc = acc + w_ref[j:j + 1, :] * _rows_at(ubuf, ush, r0 + CONF_HALO - (k - 1) + j, CONV_ROWS)
            uc_ref[r0:r0 + CONV_ROWS, :] = acc
        y_ref[...] = _ln_act(uc_ref[...], sg_ref[...], g_ref[...], beta_ref[...]).astype(y_ref.dtype)

    return pl.pallas_call(
        body, name="odd_fwd", grid=(s // ts,),
        in_specs=[_rows(ts, d, 0), _rows(ts, d, 1), _rows(ts, d, 2),
                  _prev_halo(ts, CONF_HALO, d, 0), _prev_halo(ts, CONF_HALO, d, 1),
                  _vec(d, 0, k), _vec(d), _vec(d), _vec(d)],
        out_specs=[_rows(ts, d), _rows(ts, d)], out_shape=[_sds((s, d), MXU_DTYPE), _sds((s, d))],
        scratch_shapes=[pltpu.VMEM((ts + CONF_HALO, d), F32),
                        pltpu.VMEM((SUBLANES - 1, ts + CONF_HALO - SUBLANES, d), F32)],
        compiler_params=_cp(1))(z, z, z, z, z, conv_w, conv_b, ln_g, ln_b)


def _odd_bwd(dy, z, uc, conv_w, ln_g, ln_b, ts):
    s = z.shape[0]
    d = D_MODEL
    k = CONF_KERNEL
    n = s // ts

    def body(dy_ref, dyn_ref, val_ref, glu_ref, sg_ref, sgn_ref, uc_ref, ucn_ref,
             w_ref, g_ref, beta_ref, dz_ref, dw_ref, db_ref, dg_ref, dbeta_ref, dbuf, dsh, dw_acc):
        i = pl.program_id(0)
        val, glu = val_ref[...], glu_ref[...]
        sig = _sigmoid(glu)
        u = val * sig
        _, vjp = jax.vjp(_ln_act, uc_ref[...], sg_ref[...], g_ref[...], beta_ref[...])
        duc, dsg, dg, dbeta = vjp(dy_ref[...])
        _, vjp_n = jax.vjp(_ln_act, ucn_ref[...], sgn_ref[...], g_ref[...], beta_ref[...])
        dbuf[0:ts, :] = duc
        dbuf[ts:, :] = jnp.where(i < n - 1, vjp_n(dyn_ref[...])[0], 0.0)
        dz_ref[:, 2 * d:] = dsg.astype(dz_ref.dtype)
        _shifted_copies(dbuf, dsh, ts + CONF_HALO - SUBLANES)

        @pl.when(i == 0)
        def _():
            dw_acc[...] = jnp.zeros_like(dw_acc)
            db_ref[...] = jnp.zeros_like(db_ref)
            dg_ref[...] = jnp.zeros_like(dg_ref)
            dbeta_ref[...] = jnp.zeros_like(dbeta_ref)

        db_ref[...] += jnp.sum(duc, axis=0, keepdims=True)
        dg_ref[...] += dg
        dbeta_ref[...] += dbeta
        for r0 in range(0, ts, CONV_ROWS):
            acc = jnp.zeros((CONV_ROWS, d), F32)
            for j in range(k):
                acc = acc + w_ref[j:j + 1, :] * _rows_at(dbuf, dsh, r0 + (k - 1) - j, CONV_ROWS)
            sig_r = sig[r0:r0 + CONV_ROWS, :]
            dz_ref[r0:r0 + CONV_ROWS, 0:d] = (acc * sig_r).astype(dz_ref.dtype)
            dz_ref[r0:r0 + CONV_ROWS, d:2 * d] = (acc * val[r0:r0 + CONV_ROWS, :] * sig_r * (1.0 - sig_r)).astype(dz_ref.dtype)
        for j in range(k):
            prod = _rows_at(dbuf, dsh, (k - 1) - j, ts) * u
            dw_acc[j] += jnp.sum(prod.reshape(ts // SUBLANES, SUBLANES, d), axis=0)

        @pl.when(i == n - 1)
        def _():
            dw_ref[...] = jnp.sum(dw_acc[...], axis=1)

    return pl.pallas_call(
        body, name="odd_bwd", grid=(n,),
        in_specs=[_rows(ts, d), _next_halo(ts, CONF_HALO, d, 0, s),
                  _rows(ts, d, 0), _rows(ts, d, 1), _rows(ts, d, 2), _next_halo(ts, CONF_HALO, d, 2, s),
                  _rows(ts, d), _next_halo(ts, CONF_HALO, d, 0, s),
                  _vec(d, 0, k), _vec(d), _vec(d)],
        out_specs=[_rows(ts, ODD_IN), _vec(d, 0, k), _vec(d), _vec(d), _vec(d)],
        out_shape=[_sds((s, ODD_IN), MXU_DTYPE), _sds((k, d)), _sds((1, d)), _sds((1, d)), _sds((1, d))],
        scratch_shapes=[pltpu.VMEM((ts + CONF_HALO, d), F32),
                        pltpu.VMEM((SUBLANES - 1, ts + CONF_HALO - SUBLANES, d), F32), pltpu.VMEM((k, SUBLANES, d), F32)],
        compiler_params=_cp(1))(dy, dy, z, z, z, z, uc, uc, conv_w, ln_g, ln_b)


def _local_step(x, target, cos, sin, mod, p, layer_weights, fwd_dep=None, grads_done=None):
    s = x.shape[0]
    tsf, tsb = min(512, s // 2), min(256, s // 2)
    tq = min(512, s // 2)
    row1 = lambda a, i: a[i:i + 1]
    saved = []
    h = _pre_fwd(x, row1(p["pre_norm_g"], 0), row1(mod, 0), tsf, fwd_dep)
    for layer in range(DEPTH):
        i = layer // 2
        mod_l = row1(mod, layer)
        wl = layer_weights(layer, h)
        if layer % 2 == 0:
            z = _mm(h, wl["w_in"], "nn", F32, 512, EVEN_PAD, "even_in_fwd")
            if "late" in wl:
                wl.update(wl.pop("late")(z))
            q, q_t, k, v = _mla_prep_fwd(z, cos, sin, row1(p["even_q_norm_g"], i), row1(p["even_kv_norm_g"], i),
                                    wl["wq"], wl["wq_rot"], wl["wuk"], wl["wuv"], tsf)
            o, lse = _attn_fwd(q, k, v, tq)
            y = _even_gate_fwd(z, o, wl["sc_conv_w"], row1(p["even_sc_conv_b"], i), tsf)
            yo = _mm(y, wl["w_out"], "nn", F32, 1024, 1024, "even_out_fwd")
            saved.append((x, h, z, y, yo, wl, (q, q_t, k, v, o, lse)))
        else:
            z = _mm(h, wl["w_in"], "nn", F32, 512, ODD_IN, "odd_in_fwd")
            y, uc = _odd_fwd(z, wl["conv_w"], wl["conv_b"], wl["ln_g"], wl["ln_b"], tsf)
            yo = _mm(y, wl["w_out"], "nn", F32, 1024, 1024, "odd_out_fwd")
            saved.append((x, h, z, y, yo, wl, uc))
        if layer + 1 < DEPTH:
            x, h = _post_pre_fwd(x, yo, row1(p["post_norm_g"], layer), mod_l, row1(p["pre_norm_g"], layer + 1),
                                 row1(mod, layer + 1), tsf)
        else:
            loss, dx = _post_loss(x, yo, row1(p["post_norm_g"], layer), mod_l, target, tsf)

    g = {n: [None] * (DEPTH if n in ("pre_norm_g", "post_norm_g") else N_PAIRS) for n in (
        "pre_norm_g", "post_norm_g", "even_sc_conv_w", "even_sc_conv_b", "even_q_norm_g", "even_kv_norm_g",
        "odd_conv_w", "odd_conv_b", "odd_ln_g", "odd_ln_b")}
    dmod = [None] * DEPTH
    dep = None
    for layer in reversed(range(DEPTH)):
        i = layer // 2
        mod_l = row1(mod, layer)
        x_in, h, z, y, yo, wl, extra = saved[layer]
        dyo, dgate, g["post_norm_g"][layer] = _post_bwd(dx, yo, row1(p["post_norm_g"], layer), mod_l, tsf, dep)
        bufs = {}
        if layer % 2 == 0:
            q, q_t, k, v, o, lse = extra
            dy = _mm(dyo, wl["w_out"], "nt", F32, 1024, 1024, "even_out_bwd_x")
            bufs["even_w_out"] = _mm_tn_shards(y, dyo, "rows", "even_out_bwd_w")
            dz, do, do_t, g["even_sc_conv_w"][i], g["even_sc_conv_b"][i] = _even_gate_bwd(
                dy, z, o, wl["sc_conv_w"], row1(p["even_sc_conv_b"], i), tsf)
            dq, dk, dv = _attn_bwd(q, q_t, k, v, do, do_t, o, lse, tq)
            dz, bufs["even_mla"], g["even_q_norm_g"][i], g["even_kv_norm_g"][i] = _mla_prep_bwd(
                dz, dq, dk, dv, z, cos, sin, row1(p["even_q_norm_g"], i), row1(p["even_kv_norm_g"], i),
                wl["wq"], wl["wuk"], wl["wuv"], tsf)
            bufs["even_w_in"] = _ein_to_shards(_mm(h, dz, "tn", F32, D_MODEL, 768, "even_in_bwd_w"))
        else:
            uc = extra
            dy = _mm(dyo, wl["w_out"], "nt", F32, 1024, 1024, "odd_out_bwd_x")
            bufs["odd_w_out"] = _mm_tn_shards(y, dyo, "rows", "odd_out_bwd_w")
            dz, g["odd_conv_w"][i], g["odd_conv_b"][i], g["odd_ln_g"][i], g["odd_ln_b"][i] = _odd_bwd(
                dy, z, uc, wl["conv_w"], wl["ln_g"], wl["ln_b"], tsb)
            bufs["odd_w_in"] = _mm_tn_shards(h, dz, "cols", "odd_in_bwd_w")
        dx, dshift, dscale, g["pre_norm_g"][layer] = _pre_bwd(
            dz, wl["w_in"], dx, x_in, row1(p["pre_norm_g"], layer), mod_l, tsf)
        dmod[layer] = jnp.concatenate([dshift, dscale, dgate], axis=-1)
        dep = grads_done(layer, bufs, dx) if grads_done is not None else None
    stack = lambda parts: jnp.stack([a[0] if a.shape[0] == 1 and a.ndim == 2 else a for a in parts])
    small = {n: stack(parts) for n, parts in g.items()}
    small["dmod"] = jnp.concatenate(dmod, axis=0)
    return loss, dx, small


def _uq_to_heads(w):
    w = w.reshape(N_CHIPS, Q_LORA, 2, QK_NOPE + QK_ROPE).transpose(0, 2, 1, 3).reshape(HEADS, Q_LORA, QK_NOPE + QK_ROPE)
    half = QK_ROPE // 2
    rotated = jnp.concatenate([jnp.zeros_like(w[..., :QK_NOPE]), -w[..., QK_NOPE + half:], w[..., QK_NOPE:QK_NOPE + half]],
                              axis=-1)
    pad = ((0, 0), (0, 0), (0, HEAD_PAD - QK_NOPE - QK_ROPE))
    return _side_by_side(jnp.pad(w, pad)), _side_by_side(jnp.pad(rotated, pad))


def _side_by_side(w):
    return w.transpose(1, 0, 2).reshape(w.shape[1], HEADS * HEAD_PAD)


def _ukv_to_heads(w):
    w = w.reshape(N_CHIPS, KV_LORA, 2, QK_NOPE + V_HEAD).transpose(0, 2, 1, 3).reshape(HEADS, KV_LORA, QK_NOPE + V_HEAD)
    wk = jnp.pad(w[..., :QK_NOPE], ((0, 0), (0, 0), (0, HEAD_PAD - QK_NOPE)))
    wv = w[..., QK_NOPE:]
    zero = jnp.zeros_like(wv)
    odd = (jnp.arange(HEADS) % 2 == 1)[:, None, None]
    wv = jnp.concatenate([jnp.where(odd, zero, wv), jnp.where(odd, wv, zero)], axis=-1)
    return _side_by_side(wk), _side_by_side(wv)


def _mla_local(q):
    blocks = q.reshape(2, MLA_ROWS, HEAD_PAD)
    uq = jnp.concatenate([blocks[r, :Q_LORA, :QK_NOPE + QK_ROPE] for r in range(2)], axis=-1)
    ukv = jnp.concatenate(
        [jnp.concatenate([blocks[r, Q_LORA:Q_LORA + KV_LORA, :QK_NOPE],
                          blocks[r, Q_LORA + KV_LORA:, V_HEAD * r:V_HEAD * (r + 1)]], axis=-1) for r in range(2)], axis=-1)
    return uq, ukv


def _place():
    return lax.axis_index("x"), lax.axis_index("y"), lax.axis_index("c")


def _flip(v, bit):
    return 1 - v if bit else v


def _sem(a, k):
    return a * (N_CHIPS - 1) + k - 1


def _remote(src, dst, send_sem, recv_sem, peer):
    return pltpu.make_async_remote_copy(src_ref=src, dst_ref=dst, send_sem=send_sem, recv_sem=recv_sem,
                                        device_id=peer, device_id_type=MESH)


_VMEM_SPEC = pl.BlockSpec(memory_space=pltpu.VMEM)
_HBM_SPEC = pl.BlockSpec(memory_space=pl.ANY)


def _ada_fwd(c8, ada_w, ada_b_sh):
    depth, d, cols = ada_w.shape

    def body(c_ref, w_ref, b_ref, call_ref, mod_ref, s1, r1, s2, r2):
        x, y, c = _place()
        chip = 2 * x + y
        me = 2 * chip + c
        call_ref[me] = c_ref[...]
        sends = []
        for k in range(1, N_DEV):
            peer = (_flip(x, k & 4), _flip(y, k & 2), _flip(c, k & 1))
            cp = _remote(c_ref, call_ref.at[me], s1.at[k - 1], r1.at[k - 1], peer)
            cp.start()
            sends.append(cp)
        for k in range(1, N_DEV):
            src = 4 * _flip(x, k & 4) + 2 * _flip(y, k & 2) + _flip(c, k & 1)
            _remote(c_ref, call_ref.at[src], s1.at[k - 1], r1.at[k - 1], (x, y, c)).wait_recv()
        act = _silu(jnp.concatenate([call_ref[e, 0:1, :] for e in range(N_DEV)], axis=0))
        for l in range(depth):
            mod_ref[chip, l] = _dot_nn(act, w_ref[l]) + b_ref[l:l + 1, :]
        for k in range(1, N_CHIPS):
            peer = (_flip(x, k & 2), _flip(y, k & 1), c)
            cp = _remote(mod_ref.at[chip], mod_ref.at[chip], s2.at[k - 1], r2.at[k - 1], peer)
            cp.start()
            sends.append(cp)
        for k in range(1, N_CHIPS):
            src = 2 * _flip(x, k & 2) + _flip(y, k & 1)
            _remote(mod_ref.at[src], mod_ref.at[src], s2.at[k - 1], r2.at[k - 1], (x, y, c)).wait_recv()
        for cp in sends:
            cp.wait_send()

    return pl.pallas_call(
        body, name="ada_fwd", in_specs=[_VMEM_SPEC] * 3, out_specs=[_VMEM_SPEC] * 2,
        out_shape=[_sds((N_DEV, 8, d)), _sds((N_CHIPS, depth, N_DEV, cols))],
        scratch_shapes=[pltpu.SemaphoreType.DMA((N_DEV - 1,)), pltpu.SemaphoreType.DMA((N_DEV - 1,)),
                        pltpu.SemaphoreType.DMA((N_CHIPS - 1,)), pltpu.SemaphoreType.DMA((N_CHIPS - 1,))],
        compiler_params=pltpu.CompilerParams(vmem_limit_bytes=VMEM_LIMIT_V7X))(c8, ada_w, ada_b_sh)


def _ada_bwd(c_t, dmod_sh):
    depth, n, cols = dmod_sh.shape
    d = c_t.shape[0]
    tr = 256

    def body(c_ref, dm_ref, o_ref):
        act = _silu(c_ref[...])
        acc = act[:, 0:1] * dm_ref[0, 0:1, :]
        for e in range(1, n):
            acc = acc + act[:, e:e + 1] * dm_ref[0, e:e + 1, :]
        o_ref[0] = acc

    return pl.pallas_call(
        body, name="ada_bwd", grid=(depth, d // tr),
        in_specs=[pl.BlockSpec((tr, n), lambda l, i: (i, 0)), pl.BlockSpec((1, n, cols), lambda l, i: (l, 0, 0))],
        out_specs=pl.BlockSpec((1, tr, cols), lambda l, i: (l, i, 0)), out_shape=_sds((depth, d, cols)),
        compiler_params=_cp(2))(c_t, dmod_sh)


def _gathered_shape(shape, how):
    if how == "slot":
        return (N_CHIPS,) + shape
    r, cc = shape
    return (r, N_CHIPS * cc) if how == "cols" else (N_CHIPS * r, cc)


def _gathered_part(ref, shape, how, chip):
    if how == "slot":
        return ref.at[chip]
    if how == "cols":
        return ref.at[:, pl.ds(pl.multiple_of(chip * shape[1], 128), shape[1])]
    return ref.at[pl.ds(pl.multiple_of(chip * shape[0], 8), shape[0]), :]


_SEM_SPEC = pl.BlockSpec(memory_space=pltpu.SEMAPHORE)
_TOKEN = jax.ShapeDtypeStruct((8, 128), F32)
_SPLIT_COPY = pltpu.CompilerParams(has_side_effects=pltpu.SideEffectType.DATAFLOW_SIDE_EFFECTING)


def _in_hbm(a):
    return pltpu.with_memory_space_constraint(a, pltpu.HBM)


def _gather_start(items, gathered, name, after=()):
    n = len(items)

    def body(*refs):
        ins, outs = refs[:n], refs[n:2 * n]
        send_sems, recv_sems = refs[2 * n + len(after)], refs[2 * n + len(after) + 1]
        x, y, c = _place()
        for a in range(n):
            for k in range(1, N_CHIPS):
                part = _gathered_part(outs[a], items[a][0].shape, items[a][1], 2 * x + y)
                _remote(ins[a], part, send_sems.at[_sem(a, k)], recv_sems.at[_sem(a, k)],
                        (_flip(x, k & 2), _flip(y, k & 1), c)).start()
        refs[-1][...] = jnp.zeros(_TOKEN.shape, _TOKEN.dtype)

    arrays = [_in_hbm(a) for a, _ in items] + [_in_hbm(a) for a in gathered]
    res = pl.pallas_call(
        body, name=name, in_specs=[_HBM_SPEC] * (2 * n + len(after)),
        out_specs=[_SEM_SPEC, _SEM_SPEC] + [_HBM_SPEC] * (2 * n) + [_VMEM_SPEC],
        out_shape=[pltpu.SemaphoreType.DMA((n * (N_CHIPS - 1),)), pltpu.SemaphoreType.DMA((n * (N_CHIPS - 1),))]
        + [pltpu.HBM(a.shape, a.dtype) for a in arrays] + [_TOKEN],
        input_output_aliases={a: 2 + a for a in range(2 * n)}, compiler_params=_SPLIT_COPY)(*arrays, *after)
    return res[0], res[1], res[2:2 + n], res[2 + n:2 + 2 * n], res[-1]


def _gather_wait(items, started, after, name):
    n = len(items)
    send_sems, recv_sems, shards, gathered, _ = started

    def body(*refs):
        ins, outs, send_sems, recv_sems = refs[:n], refs[n:2 * n], refs[2 * n], refs[2 * n + 1]
        x, y, c = _place()
        for a in range(n):
            for k in range(1, N_CHIPS):
                part = _gathered_part(outs[a], items[a][0].shape, items[a][1], 2 * _flip(x, k & 2) + _flip(y, k & 1))
                cp = _remote(ins[a], part, send_sems.at[_sem(a, k)], recv_sems.at[_sem(a, k)], (x, y, c))
                cp.wait_send()
                cp.wait_recv()

    res = pl.pallas_call(
        body, name=name, in_specs=[_HBM_SPEC] * (2 * n) + [_SEM_SPEC, _SEM_SPEC] + [_HBM_SPEC] * len(after),
        out_specs=[_HBM_SPEC] * (2 * n), out_shape=[pltpu.HBM(a.shape, a.dtype) for a in (*shards, *gathered)],
        input_output_aliases={a: a for a in range(2 * n)}, compiler_params=_SPLIT_COPY)(
            *shards, *gathered, send_sems, recv_sems, *after)
    return res[n:]


def _rs_start(bufs, name, after=()):
    n = len(bufs)

    def body(*refs):
        srcs, lands = refs[:n], refs[n:2 * n]
        send_sems, recv_sems = refs[2 * n + len(after)], refs[2 * n + len(after) + 1]
        x, y, c = _place()
        for a in range(n):
            for k in range(1, N_CHIPS):
                tx, ty = _flip(x, k & 2), _flip(y, k & 1)
                _remote(srcs[a].at[2 * tx + ty], lands[a].at[k - 1], send_sems.at[_sem(a, k)], recv_sems.at[_sem(a, k)],
                        (tx, ty, c)).start()
        refs[-1][...] = jnp.zeros(_TOKEN.shape, _TOKEN.dtype)

    arrays = [_in_hbm(b) for b in bufs] + [_in_hbm(lax.empty((N_CHIPS - 1,) + b.shape[1:], b.dtype)) for b in bufs]
    res = pl.pallas_call(
        body, name=name, in_specs=[_HBM_SPEC] * (2 * n + len(after)),
        out_specs=[_SEM_SPEC, _SEM_SPEC] + [_HBM_SPEC] * (2 * n) + [_VMEM_SPEC],
        out_shape=[pltpu.SemaphoreType.DMA((n * (N_CHIPS - 1),)), pltpu.SemaphoreType.DMA((n * (N_CHIPS - 1),))]
        + [pltpu.HBM(a.shape, a.dtype) for a in arrays] + [_TOKEN],
        input_output_aliases={a: 2 + a for a in range(2 * n)}, compiler_params=_SPLIT_COPY)(*arrays, *after)
    return res[0], res[1], res[2:2 + n], res[2 + n:2 + 2 * n], res[-1]


def _rs_wait(started, after, name):
    send_sems, recv_sems, bufs, lands, _ = started
    n = len(bufs)

    def body(*refs):
        srcs, lnds, send_sems, recv_sems = refs[:n], refs[n:2 * n], refs[2 * n], refs[2 * n + 1]
        x, y, c = _place()
        for a in range(n):
            for k in range(1, N_CHIPS):
                cp = _remote(srcs[a].at[0], lnds[a].at[k - 1], send_sems.at[_sem(a, k)], recv_sems.at[_sem(a, k)], (x, y, c))
                cp.wait_send()
                cp.wait_recv()

    res = pl.pallas_call(
        body, name=name, in_specs=[_HBM_SPEC] * (2 * n) + [_SEM_SPEC, _SEM_SPEC] + [_HBM_SPEC] * len(after),
        out_specs=[_HBM_SPEC] * (2 * n), out_shape=[pltpu.HBM(a.shape, a.dtype) for a in (*bufs, *lands)],
        input_output_aliases={a: a for a in range(2 * n)}, compiler_params=_SPLIT_COPY)(
            *bufs, *lands, send_sems, recv_sems, *after)
    return res[:n], res[n:]


def _place_own(shard, how, chip_idx):
    r, cc = shard.shape
    block, index = {"slot": ((1, r, cc), lambda i, c: (c[0], 0, 0)), "cols": ((r, cc), lambda i, c: (0, c[0])),
                    "rows": ((r, cc), lambda i, c: (c[0], 0))}[how]

    def body(c_ref, in_ref, o_ref):
        del c_ref
        o_ref[...] = in_ref[...].reshape(o_ref.shape)

    return pl.pallas_call(
        body, name="place_own", out_shape=_sds(_gathered_shape(shard.shape, how), shard.dtype),
        grid_spec=pltpu.PrefetchScalarGridSpec(
            num_scalar_prefetch=1, grid=(1,), in_specs=[pl.BlockSpec((r, cc), lambda i, c: (0, 0))],
            out_specs=pl.BlockSpec(block, index)),
        compiler_params=_cp(1))(chip_idx, shard)


def _gather_sum_all(small):
    r, w = small.shape

    def body(in_ref, all_ref, sum_ref, send_sems, recv_sems):
        x, y, c = _place()
        me = 4 * x + 2 * y + c
        all_ref[me] = in_ref[...]
        sends = []
        for k in range(1, N_DEV):
            peer = (_flip(x, k & 4), _flip(y, k & 2), _flip(c, k & 1))
            cp = _remote(in_ref, all_ref.at[me], send_sems.at[k - 1], recv_sems.at[k - 1], peer)
            cp.start()
            sends.append(cp)
        for k in range(1, N_DEV):
            src = 4 * _flip(x, k & 4) + 2 * _flip(y, k & 2) + _flip(c, k & 1)
            _remote(in_ref, all_ref.at[src], send_sems.at[k - 1], recv_sems.at[k - 1], (x, y, c)).wait_recv()
        acc = all_ref[0]
        for e in range(1, N_DEV):
            acc = acc + all_ref[e]
        sum_ref[...] = acc
        for cp in sends:
            cp.wait_send()

    return pl.pallas_call(
        body, name="gather_sum_all", in_specs=[_VMEM_SPEC], out_specs=[_VMEM_SPEC] * 2,
        out_shape=[_sds((N_DEV, r, w)), _sds((r, w))],
        scratch_shapes=[pltpu.SemaphoreType.DMA((N_DEV - 1,)), pltpu.SemaphoreType.DMA((N_DEV - 1,))],
        compiler_params=pltpu.CompilerParams(vmem_limit_bytes=VMEM_LIMIT_V7X))(small)


def _add_chips(buf, t, chip_idx):
    r, cc = buf.shape[1:]
    tr = min(1024, r)

    def body(c_ref, p_ref, t_ref, o_ref):
        del c_ref
        o_ref[...] = p_ref[0] + t_ref[0].astype(F32) + t_ref[1].astype(F32) + t_ref[2].astype(F32)

    return pl.pallas_call(
        body, name="add_chips", out_shape=_sds((r, cc)),
        grid_spec=pltpu.PrefetchScalarGridSpec(
            num_scalar_prefetch=1, grid=(r // tr,),
            in_specs=[pl.BlockSpec((1, tr, cc), lambda i, c: (c[0], i, 0)),
                      pl.BlockSpec((N_CHIPS - 1, tr, cc), lambda i, c: (0, i, 0))],
            out_specs=pl.BlockSpec((tr, cc), lambda i, c: (i, 0))),
        compiler_params=_cp(1))(chip_idx, buf, t)


def _rs_sibling(qs):
    n = len(qs)

    def body(*refs):
        ins, outs = refs[:n], refs[n:2 * n]
        send_sems, recv_sems = refs[2 * n:]
        x, y, c = _place()
        copies = [_remote(ins[a], outs[a], send_sems.at[a], recv_sems.at[a], (x, y, 1 - c)) for a in range(n)]
        for cp in copies:
            cp.start()
        for cp in copies:
            cp.wait()

    return pl.pallas_call(
        body, name="rs_sibling", in_specs=[_HBM_SPEC] * n, out_specs=[_HBM_SPEC] * n,
        out_shape=[_sds(q.shape) for q in qs],
        scratch_shapes=[pltpu.SemaphoreType.DMA((n,)), pltpu.SemaphoreType.DMA((n,))])(*qs)


def _adamw_update(w, g, m, v):
    m = ADAM_B1 * m + (1.0 - ADAM_B1) * g
    v = ADAM_B2 * v + (1.0 - ADAM_B2) * jnp.square(g)
    m_hat = m / (1.0 - ADAM_B1 ** ADAM_STEP)
    v_hat = v / (1.0 - ADAM_B2 ** ADAM_STEP)
    return -ADAM_LR * (m_hat / (jnp.sqrt(v_hat) + ADAM_EPS) + ADAM_WD * w), m, v


def _adamw(w, g_parts, m, v, name):
    shape = w.shape
    cols = shape[-1]
    rows = _size(shape[:-1])
    tr = 512 if rows % 512 == 0 else rows
    spec = pl.BlockSpec((tr, cols), lambda i: (i, 0))
    n = len(g_parts)
    n_out = 4 if n > 1 else 3

    def body(*refs):
        w_ref, m_ref, v_ref = refs[:3]
        d_ref, nm_ref, nv_ref = refs[-3:]
        g = refs[3][...]
        for r in refs[4:3 + n]:
            g = g + r[...]
        if n > 1:
            refs[3 + n][...] = g
        d_ref[...], nm_ref[...], nv_ref[...] = _adamw_update(w_ref[...], g, m_ref[...], v_ref[...])

    outs = pl.pallas_call(
        body, name="adamw_" + name, grid=(rows // tr,), in_specs=[spec] * (3 + n), out_specs=[spec] * n_out,
        out_shape=[_sds((rows, cols))] * n_out, compiler_params=_cp(1))(
            *[a.reshape(rows, cols) for a in (w, m, v, *g_parts)])
    outs = tuple(o.reshape(shape) for o in outs)
    return outs if n > 1 else (g_parts[0],) + outs


def _adamw_layer(w, g_parts, m, v, layer, prev, name):
    _, r, cc = w.shape
    tr = 512 if r % 512 == 0 else r
    spec = pl.BlockSpec((1, tr, cc), lambda i: (layer, i, 0))
    n = len(g_parts)

    def body(*refs):
        w_ref, m_ref, v_ref = refs[:3]
        g_ref, d_ref, nm_ref, nv_ref = refs[-4:]
        g = refs[3][...]
        for q in refs[4:3 + n]:
            g = g + q[...]
        g = g[:, :cc]
        g_ref[0] = g
        d_ref[0], nm_ref[0], nv_ref[0] = _adamw_update(w_ref[0], g, m_ref[0], v_ref[0])

    g_specs = [pl.BlockSpec((tr, q.shape[1]), lambda i: (i, 0)) for q in g_parts]
    passed = () if prev is None else tuple(prev)
    return pl.pallas_call(
        body, name="adamw_" + name, grid=(r // tr,),
        in_specs=[spec] * 3 + g_specs + [_HBM_SPEC] * len(passed), out_specs=[spec] * 4,
        out_shape=[_sds(w.shape)] * 4, input_output_aliases={3 + n + k: k for k in range(len(passed))},
        compiler_params=_cp(1))(w, m, v, *g_parts, *passed)


def _size(shape):
    n = 1
    for s in shape:
        n *= s
    return n


_SMALL = (("dmod", (DEPTH, 3 * D_MODEL)), ("pre_norm_g", (DEPTH, D_MODEL)), ("post_norm_g", (DEPTH, D_MODEL)),
          ("even_sc_conv_w", (2, SC_KERNEL, SC_WIDTH)), ("even_sc_conv_b", (2, SC_WIDTH)),
          ("even_q_norm_g", (2, Q_LORA)), ("even_kv_norm_g", (2, KV_LORA)),
          ("odd_conv_w", (2, CONF_KERNEL, D_MODEL)), ("odd_conv_b", (2, D_MODEL)), ("odd_ln_g", (2, D_MODEL)),
          ("odd_ln_b", (2, D_MODEL)))
SMALL_ROWS = -(-sum(_size(s) for _, s in _SMALL) // (8 * 128)) * 8

_SMALL_W = (("even_sc_conv_w", (2, SC_KERNEL, SC_WIDTH // N_CHIPS)), ("odd_conv_w", (2, CONF_KERNEL, D_MODEL // N_CHIPS)),
            ("odd_conv_b", (2, D_MODEL // N_CHIPS)), ("odd_ln_g", (2, D_MODEL // N_CHIPS)),
            ("odd_ln_b", (2, D_MODEL // N_CHIPS)))
SMALL_W_ROWS = -(-sum(_size(s) for _, s in _SMALL_W) // (8 * 128)) * 8


def _pack_rows(arrays, layout, rows):
    flat = jnp.concatenate([arrays[n].reshape(-1) for n, _ in layout])
    return jnp.pad(flat, (0, rows * 128 - flat.shape[0])).reshape(rows, 128)


def _unpack_small(t):
    flat = t.reshape(-1)
    out, at = {}, 0
    for n, shape in _SMALL:
        out[n] = flat[at:at + _size(shape)].reshape(shape)
        at += _size(shape)
    return out


def _unpack_small_w(t):
    flat = t.reshape(N_CHIPS, -1)
    out, at = {}, 0
    for n, shape in _SMALL_W:
        a = flat[:, at:at + _size(shape)].reshape((N_CHIPS,) + shape)
        out[n] = jnp.moveaxis(a, 0, -2).reshape(shape[:-1] + (N_CHIPS * shape[-1],))
        at += _size(shape)
    return out


def _chip_cols(a, chip):
    n = a.shape[-1] // N_CHIPS
    return lax.dynamic_slice_in_dim(a, chip * n, n, axis=a.ndim - 1)


WEIGHT_NAMES = ("ada_w", "ada_b", "pre_norm_g", "post_norm_g", "even_w_in", "even_sc_conv_w", "even_sc_conv_b",
                "even_q_norm_g", "even_kv_norm_g", "even_w_uq", "even_w_ukv", "even_w_out", "odd_w_in", "odd_conv_w",
                "odd_conv_b", "odd_ln_g", "odd_ln_b", "odd_w_out")
GATHER_HOW = ((("even_w_in", "slot"), ("even_w_uq", "slot"), ("even_w_ukv", "slot"), ("even_w_out", "rows")),
              (("odd_w_in", "cols"), ("odd_w_out", "rows")))


def kernel(x, c, positions, ada_w, ada_b, pre_norm_g, post_norm_g, even_w_in, even_sc_conv_w, even_sc_conv_b, even_q_norm_g, even_kv_norm_g, even_w_uq, even_w_ukv, even_w_out, odd_w_in, odd_conv_w, odd_conv_b, odd_ln_g, odd_ln_b, odd_w_out, loss_target, m_ada_w, m_ada_b, m_pre_norm_g, m_post_norm_g, m_even_w_in, m_even_sc_conv_w, m_even_sc_conv_b, m_even_q_norm_g, m_even_kv_norm_g, m_even_w_uq, m_even_w_ukv, m_even_w_out, m_odd_w_in, m_odd_conv_w, m_odd_conv_b, m_odd_ln_g, m_odd_ln_b, m_odd_w_out, v_ada_w, v_ada_b, v_pre_norm_g, v_post_norm_g, v_even_w_in, v_even_sc_conv_w, v_even_sc_conv_b, v_even_q_norm_g, v_even_kv_norm_g, v_even_w_uq, v_even_w_ukv, v_even_w_out, v_odd_w_in, v_odd_conv_w, v_odd_conv_b, v_odd_ln_g, v_odd_ln_b, v_odd_w_out):
    w = dict(zip(WEIGHT_NAMES, (ada_w, ada_b, pre_norm_g, post_norm_g, even_w_in, even_sc_conv_w, even_sc_conv_b,
                                even_q_norm_g, even_kv_norm_g, even_w_uq, even_w_ukv, even_w_out, odd_w_in, odd_conv_w,
                                odd_conv_b, odd_ln_g, odd_ln_b, odd_w_out)))
    m = dict(zip(WEIGHT_NAMES, (m_ada_w, m_ada_b, m_pre_norm_g, m_post_norm_g, m_even_w_in, m_even_sc_conv_w,
                                m_even_sc_conv_b, m_even_q_norm_g, m_even_kv_norm_g, m_even_w_uq, m_even_w_ukv,
                                m_even_w_out, m_odd_w_in, m_odd_conv_w, m_odd_conv_b, m_odd_ln_g, m_odd_ln_b, m_odd_w_out)))
    v = dict(zip(WEIGHT_NAMES, (v_ada_w, v_ada_b, v_pre_norm_g, v_post_norm_g, v_even_w_in, v_even_sc_conv_w,
                                v_even_sc_conv_b, v_even_q_norm_g, v_even_kv_norm_g, v_even_w_uq, v_even_w_ukv,
                                v_even_w_out, v_odd_w_in, v_odd_conv_w, v_odd_conv_b, v_odd_ln_g, v_odd_ln_b, v_odd_w_out)))
    ix, iy, ic = _place()
    chip = 2 * ix + iy
    me = 2 * chip + ic
    s = x.shape[1]

    c_all, mod_all = _ada_fwd(jnp.broadcast_to(c, (8, D_MODEL)), ada_w, _chip_cols(ada_b, chip))
    mod = lax.dynamic_index_in_dim(mod_all, me, axis=2, keepdims=False)
    mod = mod.transpose(1, 0, 2).reshape(DEPTH, 3 * D_MODEL)

    items = [[(w[n][layer // 2].astype(MXU_DTYPE), how) for n, how in GATHER_HOW[layer % 2]] for layer in range(DEPTH)]
    groups = [items[0][:1], items[0][1:] + [(_pack_rows(w, _SMALL_W, SMALL_W_ROWS), "slot")],
              [item for layer_items in items[1:] for item in layer_items]]
    sent, dep = [], mod_all
    for number, group in enumerate(groups):
        sent.append(_gather_start(group, [_place_own(a, how, chip.reshape(1)) for a, how in group],
                                  "gather_start_%d" % number, [dep]))
        dep = sent[-1][-1]
    arrived = {}

    def group(number, after):
        if number not in arrived:
            arrived[number] = _gather_wait(groups[number], sent[number], after, "gather_wait_%d" % number)
        return arrived[number]

    def even_rest(i, uq, ukv, eout, small_w):
        wuk, wuv = _ukv_to_heads(ukv)
        wq, wq_rot = _uq_to_heads(uq)
        return {"wq": wq, "wq_rot": wq_rot, "wuk": wuk, "wuv": wuv, "w_out": eout, "sc_conv_w": small_w["even_sc_conv_w"][i]}

    def layer_weights(layer, h):
        i = layer // 2
        if layer == 0:
            def late(z):
                uq, ukv, eout, small = group(1, [z])
                return even_rest(i, uq, ukv, eout, _unpack_small_w(small))
            return {"w_in": _ein_from_shards(group(0, [h])[0]), "late": late}
        small_w = _unpack_small_w(group(1, [h])[-1])
        at = sum(len(layer_items) for layer_items in items[1:layer])
        arrays = group(2, [h])[at:at + len(items[layer])]
        if layer % 2 == 0:
            return {"w_in": _ein_from_shards(arrays[0]), **even_rest(i, *arrays[1:], small_w)}
        oin, oout = arrays
        return {"w_in": oin, "w_out": oout, "conv_w": small_w["odd_conv_w"][i], "conv_b": small_w["odd_conv_b"][i:i + 1],
                "ln_g": small_w["odd_ln_g"][i:i + 1], "ln_b": small_w["odd_ln_b"][i:i + 1]}

    in_flight, own, sib, last = {}, {}, {}, {}

    def land(layer, after):
        names, started, kept = in_flight.pop(layer)
        bufs, arrived = _rs_wait(started, after, "rs_wait_%d" % layer)
        sums = [_add_chips(b, t, chip.reshape(1)) for b, t in zip(bufs if kept is None else kept, arrived)]
        for n, mine, theirs in zip(names, sums, _rs_sibling(sums)):
            own[n, layer // 2], sib[n, layer // 2] = mine, theirs

    def grads_done(layer, bufs, dx_in):
        if layer + 1 in in_flight:
            land(layer + 1, [dx_in])
        if layer == 0:
            last.update(bufs)
            return None
        names = sorted(bufs)
        in_flight[layer] = (names, _rs_start([bufs[n] for n in names], "rs_start_%d" % layer), None)
        return in_flight[layer][1][-1]

    p = {"pre_norm_g": pre_norm_g, "post_norm_g": post_norm_g, "even_sc_conv_b": even_sc_conv_b,
         "even_q_norm_g": even_q_norm_g, "even_kv_norm_g": even_kv_norm_g}
    inv_freq = 1.0 / (ROPE_THETA ** (jnp.arange(0, QK_ROPE, 2, dtype=F32) / QK_ROPE))
    inv_freq = jnp.zeros((1, HEAD_PAD), F32).at[0, QK_NOPE:QK_NOPE + QK_ROPE].set(jnp.tile(inv_freq, 2))
    cos, sin = _rope_tables(positions.reshape(s, 1), inv_freq)

    loss, dx, g = _local_step(x[0], loss_target[0], cos, sin, mod, p, layer_weights, dep, grads_done)

    grads, deltas, new_m, new_v = {}, {}, {}, {}

    def update_layers(n, results, pairs):
        for i in pairs:
            results = _adamw_layer(w[n], [own[n, i], sib[n, i]], m[n], v[n], i, results, n)
        return results

    small_all, small_sum = _gather_sum_all(_pack_rows(g, _SMALL, SMALL_ROWS))
    names = sorted(last)
    kept = [last[n] for n in names]
    in_flight[0] = (names, _rs_start([b.astype(jnp.bfloat16) for b in kept], "rs_start_0", [small_sum]), kept)
    tot = _unpack_small(small_sum)
    dmod_all = small_all[:, :DEPTH * 3 * D_MODEL // 128].reshape(N_DEV, DEPTH, 3 * D_MODEL)
    grads["ada_w"] = _ada_bwd(c_all[:, 0, :].T, _chip_cols(dmod_all, chip).transpose(1, 0, 2))
    grads["ada_b"] = tot["dmod"]
    for n in ("pre_norm_g", "post_norm_g", "even_sc_conv_b", "even_q_norm_g", "even_kv_norm_g"):
        grads[n] = tot[n]
    for n in ("even_sc_conv_w", "odd_conv_w", "odd_conv_b", "odd_ln_g", "odd_ln_b"):
        grads[n] = _chip_cols(tot[n], chip)
    for n in list(grads):
        _, deltas[n], new_m[n], new_v[n] = _adamw(w[n], [grads[n]], m[n], v[n], n)

    for n in ("odd_w_in", "odd_w_out"):
        grads[n], deltas[n], new_m[n], new_v[n] = update_layers(n, None, (1, 0))
    partly = {n: update_layers(n, None, (1,)) for n in ("even_w_in", "even_w_out")}
    land(0, [deltas["ada_w"], deltas["odd_w_in"], partly["even_w_in"][1]])
    for n in ("even_w_in", "even_w_out"):
        grads[n], deltas[n], new_m[n], new_v[n] = update_layers(n, partly[n], (0,))
    uq_parts, ukv_parts = zip(*[[jnp.stack(part) for part in zip(*[_mla_local(q["even_mla", i]) for i in range(N_PAIRS)])]
                                for q in (own, sib)])
    for n, parts in (("even_w_uq", uq_parts), ("even_w_ukv", ukv_parts)):
        grads[n], deltas[n], new_m[n], new_v[n] = _adamw(w[n], list(parts), m[n], v[n], n)

    total_loss = lax.psum(loss[0, 0], ("x", "y", "c"))
    return (total_loss, dx[None], *[grads[n] for n in WEIGHT_NAMES], *[deltas[n] for n in WEIGHT_NAMES],
            *[new_m[n] for n in WEIGHT_NAMES], *[new_v[n] for n in WEIGHT_NAMES])
```
